```python
import jax, jax.numpy as jnp
from jax import lax
import numpy as np

D_MODEL = 1024
BATCH = 8
SEQ = 4096
DEPTH = 1

N_META = 16
CHUNK = 128
META_PAD = CHUNK - N_META
SSM_EXPAND = 2
D_INNER = SSM_EXPAND * D_MODEL
SSM_HEAD_DIM = 64
SSM_HEADS = D_INNER // SSM_HEAD_DIM
SSM_GROUPS = 4
D_STATE = 128
SSM_CONV = 4
CONV_DIM = D_INNER + 2 * SSM_GROUPS * D_STATE
ATTN_HEADS = 16
ATTN_KV_HEADS = 4
ATTN_HEAD_DIM = 64
ATTN_GROUP = ATTN_HEADS // ATTN_KV_HEADS
WINDOW = 128
ATTN_WIDTH = ATTN_HEADS * ATTN_HEAD_DIM
FFN_DIM = 2816
FFN_CONV = 3
N_IN = D_INNER + CONV_DIM + SSM_HEADS + (ATTN_HEADS + 2 * ATTN_KV_HEADS) * ATTN_HEAD_DIM + 2 * D_MODEL
EPS = 1e-6
NEG = -1e30

kernel_name = "hybrid_ssd_swa_sink_alibi_convffn"


def _rmsnorm(x, w):
    xf = x.astype(jnp.float32)
    y = xf * lax.rsqrt(jnp.mean(xf * xf, axis=-1, keepdims=True) + EPS)
    return (y * w.astype(jnp.float32)).astype(x.dtype)


def _causal_dwconv(u, w, b):
    k_width = w.shape[0]
    seq_len = u.shape[1]
    up = jnp.pad(u, ((0, 0), (k_width - 1, 0), (0, 0)))
    out = b.astype(u.dtype) + w[k_width - 1].astype(u.dtype) * u
    for k in range(k_width - 1):
        out = out + w[k].astype(u.dtype) * up[:, k:k + seq_len]
    return out


def _segsum_exp(a):
    t = a.shape[-1]
    cs = jnp.cumsum(a, axis=-1)
    mask = jnp.tril(jnp.ones((t, t), dtype=bool))
    diff = cs[..., :, None] - cs[..., None, :]
    return jnp.where(mask, jnp.exp(jnp.where(mask, diff, 0.0)), 0.0)


def _ssd_chunked(x_dt, a_dt, b_in, c_in):
    bsz, lp, n_heads, p_dim = x_dt.shape
    g, n = b_in.shape[2], b_in.shape[3]
    r = n_heads // g
    nc = lp // CHUNK
    xc = x_dt.reshape(bsz, nc, CHUNK, g, r, p_dim)
    ac = a_dt.reshape(bsz, nc, CHUNK, g, r).transpose(0, 3, 4, 1, 2)
    bc = b_in.reshape(bsz, nc, CHUNK, g, n)
    cc = c_in.reshape(bsz, nc, CHUNK, g, n)
    a_cs = jnp.cumsum(ac, axis=-1)
    lmat = _segsum_exp(ac)
    cb = jnp.einsum("bclgn,bcsgn->bgcls", cc, bc)
    y_diag = jnp.einsum("bgcls,bgrcls,bcsgrp->bclgrp", cb, lmat, xc)
    decay_states = jnp.exp(a_cs[..., -1:] - a_cs)
    states = jnp.einsum("bclgn,bgrcl,bclgrp->bcgrpn", bc, decay_states, xc)
    chunk_decay = jnp.exp(a_cs[..., -1])

    def step(h, inp):
        s_c, d_c = inp
        return h * d_c[..., None, None] + s_c, h

    h0 = jnp.zeros_like(states[:, 0])
    _, h_in = lax.scan(step, h0, (jnp.moveaxis(states, 1, 0), jnp.moveaxis(chunk_decay, -1, 0)))
    h_in = jnp.moveaxis(h_in, 0, 1)
    y_off = jnp.einsum("bclgn,bcgrpn,bgrcl->bclgrp", cc, h_in, jnp.exp(a_cs))
    return (y_diag + y_off).reshape(bsz, lp, n_heads, p_dim)


def _ssd_branch(z, xbc, dt_raw, conv_w, conv_b, dt_bias, a_log, d_skip, norm_w, w_out):
    bsz, seq_len, _ = xbc.shape
    xbc = jax.nn.silu(_causal_dwconv(xbc, conv_w, conv_b))
    xs, bs, cs = jnp.split(xbc, [D_INNER, D_INNER + SSM_GROUPS * D_STATE], axis=-1)
    xs = xs.reshape(bsz, seq_len, SSM_HEADS, SSM_HEAD_DIM).astype(jnp.float32)
    bs = bs.reshape(bsz, seq_len, SSM_GROUPS, D_STATE).astype(jnp.float32)
    cs = cs.reshape(bsz, seq_len, SSM_GROUPS, D_STATE).astype(jnp.float32)
    dt = jax.nn.softplus(dt_raw.astype(jnp.float32) + dt_bias.astype(jnp.float32))
    a = -jnp.exp(a_log.astype(jnp.float32))
    pad4 = ((0, 0), (META_PAD, 0), (0, 0), (0, 0))
    x_dt = jnp.pad(xs * dt[..., None], pad4)
    a_dt = jnp.pad(dt * a, ((0, 0), (META_PAD, 0), (0, 0)))
    y = _ssd_chunked(x_dt, a_dt, jnp.pad(bs, pad4), jnp.pad(cs, pad4))[:, META_PAD:]
    y = y + xs * d_skip.astype(jnp.float32)[:, None]
    y = y.reshape(bsz, seq_len, D_INNER).astype(z.dtype)
    y = _rmsnorm(y * jax.nn.silu(z), norm_w)
    return y @ w_out


def _swa_branch(q, k, v, sinks, w_out):
    bsz, seq_len, _ = q.shape
    lp = seq_len + META_PAD
    nb = lp // CHUNK
    scale = ATTN_HEAD_DIM ** -0.5
    q = q.reshape(bsz, seq_len, ATTN_KV_HEADS, ATTN_GROUP, ATTN_HEAD_DIM)
    k = k.reshape(bsz, seq_len, ATTN_KV_HEADS, ATTN_HEAD_DIM)
    v = v.reshape(bsz, seq_len, ATTN_KV_HEADS, ATTN_HEAD_DIM)
    qb = jnp.pad(q, ((0, 0), (META_PAD, 0), (0, 0), (0, 0), (0, 0))).reshape(
        bsz, nb, CHUNK, ATTN_KV_HEADS, ATTN_GROUP, ATTN_HEAD_DIM)
    kpad = ((0, 0), (META_PAD + CHUNK, 0), (0, 0), (0, 0))
    kp = jnp.pad(k, kpad).reshape(bsz, nb + 1, CHUNK, ATTN_KV_HEADS, ATTN_HEAD_DIM)
    vp = jnp.pad(v, kpad).reshape(bsz, nb + 1, CHUNK, ATTN_KV_HEADS, ATTN_HEAD_DIM)
    kb = jnp.concatenate([kp[:, :-1], kp[:, 1:]], axis=2)
    vb = jnp.concatenate([vp[:, :-1], vp[:, 1:]], axis=2)
    q_pos = jnp.arange(nb)[:, None] * CHUNK + jnp.arange(CHUNK)[None, :] - META_PAD
    k_pos = jnp.arange(nb)[:, None] * CHUNK + jnp.arange(2 * CHUNK)[None, :] - CHUNK - META_PAD
    dist = q_pos[:, :, None] - k_pos[:, None, :]
    band_ok = (dist >= 0) & (dist < WINDOW) & (k_pos[:, None, :] >= N_META)
    slopes = jnp.exp2(-8.0 * jnp.arange(1, ATTN_HEADS + 1, dtype=jnp.float32) / ATTN_HEADS)
    slopes = slopes.reshape(ATTN_KV_HEADS, ATTN_GROUP)
    s_band = jnp.einsum("bnqkgd,bnskd->bnkgqs", qb, kb, preferred_element_type=jnp.float32) * scale
    s_band = s_band - slopes[None, None, :, :, None, None] * dist.astype(jnp.float32)[None, :, None, None]
    s_band = jnp.where(band_ok[None, :, None, None], s_band, NEG)
    k_meta, v_meta = k[:, :N_META], v[:, :N_META]
    s_meta = jnp.einsum("bnqkgd,bmkd->bnkgqm", qb, k_meta, preferred_element_type=jnp.float32) * scale
    meta_ok = jnp.arange(N_META)[None, None, :] <= q_pos[:, :, None]
    s_meta = jnp.where(meta_ok[None, :, None, None], s_meta, NEG)
    sink = jnp.broadcast_to(sinks.astype(jnp.float32).reshape(ATTN_KV_HEADS, ATTN_GROUP)[None, None, :, :, None, None],
                            s_band.shape[:-1] + (1,))
    probs = jax.nn.softmax(jnp.concatenate([s_meta, s_band, sink], axis=-1), axis=-1).astype(v.dtype)
    out = (jnp.einsum("bnkgqm,bmkd->bnqkgd", probs[..., :N_META], v_meta)
           + jnp.einsum("bnkgqs,bnskd->bnqkgd", probs[..., N_META:N_META + 2 * CHUNK], vb))
    out = out.reshape(bsz, lp, ATTN_WIDTH)[:, META_PAD:]
    return out @ w_out


def _token_mixer(h, w_in, ssm_conv_w, ssm_conv_b, ssm_dt_bias, ssm_a_log, ssm_d_skip, ssm_norm,
                 w_ssm_out, attn_sinks, w_attn_out, w_mix_out):
    sizes = [D_INNER, CONV_DIM, SSM_HEADS, ATTN_WIDTH, ATTN_KV_HEADS * ATTN_HEAD_DIM,
             ATTN_KV_HEADS * ATTN_HEAD_DIM, 2 * D_MODEL]
    cuts = [int(c) for c in np.cumsum(sizes)[:-1]]
    z, xbc, dt_raw, q, k, v, gate_logits = jnp.split(h @ w_in, cuts, axis=-1)
    y_ssm = _ssd_branch(z, xbc, dt_raw, ssm_conv_w, ssm_conv_b, ssm_dt_bias, ssm_a_log, ssm_d_skip,
                        ssm_norm, w_ssm_out)
    y_attn = _swa_branch(q, k, v, attn_sinks, w_attn_out)
    gates = jax.nn.sigmoid(gate_logits.astype(jnp.float32)).astype(h.dtype)
    g_ssm, g_attn = jnp.split(gates, 2, axis=-1)
    return (g_ssm * y_ssm + g_attn * y_attn) @ w_mix_out


def _conv_ffn(h, w_up, conv_w, conv_b, w_down):
    u = _causal_dwconv(h @ w_up, conv_w, conv_b)
    a, g = jnp.split(u, 2, axis=-1)
    return (jax.nn.silu(a) * g) @ w_down


def _fwd_setup_inputs(seed: int = 0) -> dict:
    key = jax.random.key(seed)
    ks = jax.random.split(key, 24)
    f32 = jnp.float32

    def nrm(k, shape, scale):
        return jax.random.normal(k, shape, f32) * scale

    def gain(k, dim):
        return 1.0 + nrm(k, (DEPTH, dim), 0.01)

    dt0 = jnp.exp(jax.random.uniform(ks[6], (DEPTH, SSM_HEADS), f32, np.log(1e-3), np.log(1e-1)))
    return {
        "x": nrm(ks[0], (BATCH, SEQ, D_MODEL), 1.0),
        "meta_tokens": nrm(ks[1], (N_META, D_MODEL), 1.0),
        "norm_pre_mix": gain(ks[2], D_MODEL),
        "w_in": nrm(ks[3], (DEPTH, D_MODEL, N_IN), D_MODEL ** -0.5),
        "ssm_conv_w": nrm(ks[4], (DEPTH, SSM_CONV, CONV_DIM), 0.5 * SSM_CONV ** -0.5),
        "ssm_conv_b": nrm(ks[5], (DEPTH, CONV_DIM), 0.01),
        "ssm_dt_bias": dt0 + jnp.log(-jnp.expm1(-dt0)),
        "ssm_a_log": jnp.log(jax.random.uniform(ks[7], (DEPTH, SSM_HEADS), f32, 1.0, 16.0)),
        "ssm_d_skip": 1.0 + nrm(ks[8], (DEPTH, SSM_HEADS), 0.01),
        "ssm_norm": gain(ks[9], D_INNER),
        "w_ssm_out": nrm(ks[10], (DEPTH, D_INNER, D_MODEL), D_INNER ** -0.5),
        "attn_sinks": nrm(ks[11], (DEPTH, ATTN_HEADS), 1.0),
        "w_attn_out": nrm(ks[12], (DEPTH, ATTN_WIDTH, D_MODEL), ATTN_WIDTH ** -0.5),
        "w_mix_out": nrm(ks[13], (DEPTH, D_MODEL, D_MODEL), D_MODEL ** -0.5),
        "norm_post_mix": gain(ks[14], D_MODEL),
        "norm_pre_ffn": gain(ks[15], D_MODEL),
        "w_ffn_up": nrm(ks[16], (DEPTH, D_MODEL, 2 * FFN_DIM), D_MODEL ** -0.5),
        "ffn_conv_w": nrm(ks[17], (DEPTH, FFN_CONV, 2 * FFN_DIM), 0.5 * FFN_CONV ** -0.5),
        "ffn_conv_b": nrm(ks[18], (DEPTH, 2 * FFN_DIM), 0.01),
        "w_ffn_down": nrm(ks[19], (DEPTH, FFN_DIM, D_MODEL), FFN_DIM ** -0.5),
        "norm_post_ffn": gain(ks[20], D_MODEL),
    }


def _fwd_reference(x, meta_tokens, norm_pre_mix, w_in, ssm_conv_w, ssm_conv_b, ssm_dt_bias, ssm_a_log,
              ssm_d_skip, ssm_norm, w_ssm_out, attn_sinks, w_attn_out, w_mix_out, norm_post_mix,
              norm_pre_ffn, w_ffn_up, ffn_conv_w, ffn_conv_b, w_ffn_down, norm_post_ffn):
    bsz = x.shape[0]
    meta = jnp.broadcast_to(meta_tokens.astype(x.dtype)[None], (bsz, N_META, D_MODEL))
    h = jnp.concatenate([meta, x], axis=1)
    for l in range(DEPTH):
        mix = _token_mixer(_rmsnorm(h, norm_pre_mix[l]), w_in[l], ssm_conv_w[l], ssm_conv_b[l],
                           ssm_dt_bias[l], ssm_a_log[l], ssm_d_skip[l], ssm_norm[l], w_ssm_out[l],
                           attn_sinks[l], w_attn_out[l], w_mix_out[l])
        h = h + _rmsnorm(mix, norm_post_mix[l])
        ffn = _conv_ffn(_rmsnorm(h, norm_pre_ffn[l]), w_ffn_up[l], ffn_conv_w[l], ffn_conv_b[l], w_ffn_down[l])
        h = h + _rmsnorm(ffn, norm_post_ffn[l])
    return h[:, N_META:]


import jax as _jax
import jax.numpy as _jnp

TWIN_FORMAT = 'train_step'
FWD_PARAMS = ['x', 'meta_tokens', 'norm_pre_mix', 'w_in', 'ssm_conv_w', 'ssm_conv_b', 'ssm_dt_bias', 'ssm_a_log', 'ssm_d_skip', 'ssm_norm', 'w_ssm_out', 'attn_sinks', 'w_attn_out', 'w_mix_out', 'norm_post_mix', 'norm_pre_ffn', 'w_ffn_up', 'ffn_conv_w', 'ffn_conv_b', 'w_ffn_down', 'norm_post_ffn']
TWIN_WEIGHTS = ['meta_tokens', 'norm_pre_mix', 'w_in', 'ssm_conv_w', 'ssm_conv_b', 'ssm_dt_bias', 'ssm_a_log', 'ssm_d_skip', 'ssm_norm', 'w_ssm_out', 'attn_sinks', 'w_attn_out', 'w_mix_out', 'norm_post_mix', 'norm_pre_ffn', 'w_ffn_up', 'ffn_conv_w', 'ffn_conv_b', 'w_ffn_down', 'norm_post_ffn']
TWIN_DIFF_INPUT = 'x'
TWIN_INPUTS = ['x', 'meta_tokens', 'norm_pre_mix', 'w_in', 'ssm_conv_w', 'ssm_conv_b', 'ssm_dt_bias', 'ssm_a_log', 'ssm_d_skip', 'ssm_norm', 'w_ssm_out', 'attn_sinks', 'w_attn_out', 'w_mix_out', 'norm_post_mix', 'norm_pre_ffn', 'w_ffn_up', 'ffn_conv_w', 'ffn_conv_b', 'w_ffn_down', 'norm_post_ffn', 'loss_target', 'm_meta_tokens', 'm_norm_pre_mix', 'm_w_in', 'm_ssm_conv_w', 'm_ssm_conv_b', 'm_ssm_dt_bias', 'm_ssm_a_log', 'm_ssm_d_skip', 'm_ssm_norm', 'm_w_ssm_out', 'm_attn_sinks', 'm_w_attn_out', 'm_w_mix_out', 'm_norm_post_mix', 'm_norm_pre_ffn', 'm_w_ffn_up', 'm_ffn_conv_w', 'm_ffn_conv_b', 'm_w_ffn_down', 'm_norm_post_ffn', 'v_meta_tokens', 'v_norm_pre_mix', 'v_w_in', 'v_ssm_conv_w', 'v_ssm_conv_b', 'v_ssm_dt_bias', 'v_ssm_a_log', 'v_ssm_d_skip', 'v_ssm_norm', 'v_w_ssm_out', 'v_attn_sinks', 'v_w_attn_out', 'v_w_mix_out', 'v_norm_post_mix', 'v_norm_pre_ffn', 'v_w_ffn_up', 'v_ffn_conv_w', 'v_ffn_conv_b', 'v_w_ffn_down', 'v_norm_post_ffn']
TWIN_OUTPUTS = ['loss', 'grad_x', 'grad_meta_tokens', 'grad_norm_pre_mix', 'grad_w_in', 'grad_ssm_conv_w', 'grad_ssm_conv_b', 'grad_ssm_dt_bias', 'grad_ssm_a_log', 'grad_ssm_d_skip', 'grad_ssm_norm', 'grad_w_ssm_out', 'grad_attn_sinks', 'grad_w_attn_out', 'grad_w_mix_out', 'grad_norm_post_mix', 'grad_norm_pre_ffn', 'grad_w_ffn_up', 'grad_ffn_conv_w', 'grad_ffn_conv_b', 'grad_w_ffn_down', 'grad_norm_post_ffn', 'delta_meta_tokens', 'delta_norm_pre_mix', 'delta_w_in', 'delta_ssm_conv_w', 'delta_ssm_conv_b', 'delta_ssm_dt_bias', 'delta_ssm_a_log', 'delta_ssm_d_skip', 'delta_ssm_norm', 'delta_w_ssm_out', 'delta_attn_sinks', 'delta_w_attn_out', 'delta_w_mix_out', 'delta_norm_post_mix', 'delta_norm_pre_ffn', 'delta_w_ffn_up', 'delta_ffn_conv_w', 'delta_ffn_conv_b', 'delta_w_ffn_down', 'delta_norm_post_ffn', 'new_m_meta_tokens', 'new_m_norm_pre_mix', 'new_m_w_in', 'new_m_ssm_conv_w', 'new_m_ssm_conv_b', 'new_m_ssm_dt_bias', 'new_m_ssm_a_log', 'new_m_ssm_d_skip', 'new_m_ssm_norm', 'new_m_w_ssm_out', 'new_m_attn_sinks', 'new_m_w_attn_out', 'new_m_w_mix_out', 'new_m_norm_post_mix', 'new_m_norm_pre_ffn', 'new_m_w_ffn_up', 'new_m_ffn_conv_w', 'new_m_ffn_conv_b', 'new_m_w_ffn_down', 'new_m_norm_post_ffn', 'new_v_meta_tokens', 'new_v_norm_pre_mix', 'new_v_w_in', 'new_v_ssm_conv_w', 'new_v_ssm_conv_b', 'new_v_ssm_dt_bias', 'new_v_ssm_a_log', 'new_v_ssm_d_skip', 'new_v_ssm_norm', 'new_v_w_ssm_out', 'new_v_attn_sinks', 'new_v_w_attn_out', 'new_v_w_mix_out', 'new_v_norm_post_mix', 'new_v_norm_pre_ffn', 'new_v_w_ffn_up', 'new_v_ffn_conv_w', 'new_v_ffn_conv_b', 'new_v_w_ffn_down', 'new_v_norm_post_ffn']
TWIN_LEAF_KINDS = {'loss': 'loss', 'grad_x': 'grad_x', 'grad_meta_tokens': 'grad_w', 'grad_norm_pre_mix': 'grad_w', 'grad_w_in': 'grad_w', 'grad_ssm_conv_w': 'grad_w', 'grad_ssm_conv_b': 'grad_w', 'grad_ssm_dt_bias': 'grad_w', 'grad_ssm_a_log': 'grad_w', 'grad_ssm_d_skip': 'grad_w', 'grad_ssm_norm': 'grad_w', 'grad_w_ssm_out': 'grad_w', 'grad_attn_sinks': 'grad_w', 'grad_w_attn_out': 'grad_w', 'grad_w_mix_out': 'grad_w', 'grad_norm_post_mix': 'grad_w', 'grad_norm_pre_ffn': 'grad_w', 'grad_w_ffn_up': 'grad_w', 'grad_ffn_conv_w': 'grad_w', 'grad_ffn_conv_b': 'grad_w', 'grad_w_ffn_down': 'grad_w', 'grad_norm_post_ffn': 'grad_w', 'delta_meta_tokens': 'delta_w', 'delta_norm_pre_mix': 'delta_w', 'delta_w_in': 'delta_w', 'delta_ssm_conv_w': 'delta_w', 'delta_ssm_conv_b': 'delta_w', 'delta_ssm_dt_bias': 'delta_w', 'delta_ssm_a_log': 'delta_w', 'delta_ssm_d_skip': 'delta_w', 'delta_ssm_norm': 'delta_w', 'delta_w_ssm_out': 'delta_w', 'delta_attn_sinks': 'delta_w', 'delta_w_attn_out': 'delta_w', 'delta_w_mix_out': 'delta_w', 'delta_norm_post_mix': 'delta_w', 'delta_norm_pre_ffn': 'delta_w', 'delta_w_ffn_up': 'delta_w', 'delta_ffn_conv_w': 'delta_w', 'delta_ffn_conv_b': 'delta_w', 'delta_w_ffn_down': 'delta_w', 'delta_norm_post_ffn': 'delta_w', 'new_m_meta_tokens': 'new_m', 'new_m_norm_pre_mix': 'new_m', 'new_m_w_in': 'new_m', 'new_m_ssm_conv_w': 'new_m', 'new_m_ssm_conv_b': 'new_m', 'new_m_ssm_dt_bias': 'new_m', 'new_m_ssm_a_log': 'new_m', 'new_m_ssm_d_skip': 'new_m', 'new_m_ssm_norm': 'new_m', 'new_m_w_ssm_out': 'new_m', 'new_m_attn_sinks': 'new_m', 'new_m_w_attn_out': 'new_m', 'new_m_w_mix_out': 'new_m', 'new_m_norm_post_mix': 'new_m', 'new_m_norm_pre_ffn': 'new_m', 'new_m_w_ffn_up': 'new_m', 'new_m_ffn_conv_w': 'new_m', 'new_m_ffn_conv_b': 'new_m', 'new_m_w_ffn_down': 'new_m', 'new_m_norm_post_ffn': 'new_m', 'new_v_meta_tokens': 'new_v', 'new_v_norm_pre_mix': 'new_v', 'new_v_w_in': 'new_v', 'new_v_ssm_conv_w': 'new_v', 'new_v_ssm_conv_b': 'new_v', 'new_v_ssm_dt_bias': 'new_v', 'new_v_ssm_a_log': 'new_v', 'new_v_ssm_d_skip': 'new_v', 'new_v_ssm_norm': 'new_v', 'new_v_w_ssm_out': 'new_v', 'new_v_attn_sinks': 'new_v', 'new_v_w_attn_out': 'new_v', 'new_v_w_mix_out': 'new_v', 'new_v_norm_post_mix': 'new_v', 'new_v_norm_pre_ffn': 'new_v', 'new_v_w_ffn_up': 'new_v', 'new_v_ffn_conv_w': 'new_v', 'new_v_ffn_conv_b': 'new_v', 'new_v_w_ffn_down': 'new_v', 'new_v_norm_post_ffn': 'new_v'}


def _forward(args):
    return _fwd_reference(*[args[k] for k in FWD_PARAMS])


def _output_shape():
    out = _jax.eval_shape(lambda: _forward(_fwd_setup_inputs(0)))
    return out.shape, out.dtype

N_MICROBATCH = 1
ADAM_LR = 0.001
ADAM_B1 = 0.9
ADAM_B2 = 0.999
ADAM_EPS = 1e-08
ADAM_WD = 0.01
ADAM_STEP = 10
PER_EXAMPLE_BATCH_AXIS = {'x': 0, 'loss_target': 0}
SHARED_INPUTS = []
_WEIGHT_DTYPES = {'meta_tokens': _jnp.float32, 'norm_pre_mix': _jnp.float32, 'w_in': _jnp.float32, 'ssm_conv_w': _jnp.float32, 'ssm_conv_b': _jnp.float32, 'ssm_dt_bias': _jnp.float32, 'ssm_a_log': _jnp.float32, 'ssm_d_skip': _jnp.float32, 'ssm_norm': _jnp.float32, 'w_ssm_out': _jnp.float32, 'attn_sinks': _jnp.float32, 'w_attn_out': _jnp.float32, 'w_mix_out': _jnp.float32, 'norm_post_mix': _jnp.float32, 'norm_pre_ffn': _jnp.float32, 'w_ffn_up': _jnp.float32, 'ffn_conv_w': _jnp.float32, 'ffn_conv_b': _jnp.float32, 'w_ffn_down': _jnp.float32, 'norm_post_ffn': _jnp.float32}
MOMENT_SCALE = {'meta_tokens': 4.136085e-02, 'norm_pre_mix': 7.555875e-01, 'w_in': 2.400926e-01, 'ssm_conv_w': 5.332371e-01, 'ssm_conv_b': 8.032234e-01, 'ssm_dt_bias': 6.119962e-01, 'ssm_a_log': 2.613038e-01, 'ssm_d_skip': 2.562723e+00, 'ssm_norm': 3.151314e-01, 'w_ssm_out': 4.626576e-01, 'attn_sinks': 1.251271e-01, 'w_attn_out': 1.604501e-01, 'w_mix_out': 4.918307e-01, 'norm_post_mix': 3.185929e+01, 'norm_pre_ffn': 4.488401e-01, 'w_ffn_up': 2.054500e-01, 'ffn_conv_w': 4.254274e-01, 'ffn_conv_b': 1.199919e+00, 'w_ffn_down': 3.524340e-01, 'norm_post_ffn': 3.183847e+01}


def _to_microbatches(a, axis):
    t = _jnp.moveaxis(a, axis, 0)
    t = t.reshape((N_MICROBATCH, t.shape[0] // N_MICROBATCH) + t.shape[1:])
    return _jnp.moveaxis(t, 1, axis + 1)


def setup_inputs(seed: int = 0) -> dict:
    inp = _fwd_setup_inputs(seed)
    key = _jax.random.fold_in(_jax.random.key(seed), 7919)
    shape, _ = _output_shape()
    out = dict(inp)
    out["loss_target"] = _jax.random.normal(_jax.random.fold_in(key, 0), shape, _jnp.float32)
    for i, name in enumerate(TWIN_WEIGHTS):
        w = inp[name].astype(_jnp.float32)
        if MOMENT_SCALE is None:
            s = _jnp.sqrt(_jnp.mean(_jnp.square(w)) + 1e-30)
        else:
            s = MOMENT_SCALE[name]
        km, kv = _jax.random.split(_jax.random.fold_in(key, i + 1))
        out[name] = w
        out["m_" + name] = s * _jax.random.normal(km, w.shape, _jnp.float32)
        out["v_" + name] = (s * s) * _jax.random.uniform(kv, w.shape, _jnp.float32, 0.5, 1.5)
    if N_MICROBATCH > 1:
        for name, axis in PER_EXAMPLE_BATCH_AXIS.items():
            out[name] = _to_microbatches(out[name], axis)
    return {'x': out['x'], 'meta_tokens': out['meta_tokens'], 'norm_pre_mix': out['norm_pre_mix'], 'w_in': out['w_in'], 'ssm_conv_w': out['ssm_conv_w'], 'ssm_conv_b': out['ssm_conv_b'], 'ssm_dt_bias': out['ssm_dt_bias'], 'ssm_a_log': out['ssm_a_log'], 'ssm_d_skip': out['ssm_d_skip'], 'ssm_norm': out['ssm_norm'], 'w_ssm_out': out['w_ssm_out'], 'attn_sinks': out['attn_sinks'], 'w_attn_out': out['w_attn_out'], 'w_mix_out': out['w_mix_out'], 'norm_post_mix': out['norm_post_mix'], 'norm_pre_ffn': out['norm_pre_ffn'], 'w_ffn_up': out['w_ffn_up'], 'ffn_conv_w': out['ffn_conv_w'], 'ffn_conv_b': out['ffn_conv_b'], 'w_ffn_down': out['w_ffn_down'], 'norm_post_ffn': out['norm_post_ffn'], 'loss_target': out['loss_target'], 'm_meta_tokens': out['m_meta_tokens'], 'm_norm_pre_mix': out['m_norm_pre_mix'], 'm_w_in': out['m_w_in'], 'm_ssm_conv_w': out['m_ssm_conv_w'], 'm_ssm_conv_b': out['m_ssm_conv_b'], 'm_ssm_dt_bias': out['m_ssm_dt_bias'], 'm_ssm_a_log': out['m_ssm_a_log'], 'm_ssm_d_skip': out['m_ssm_d_skip'], 'm_ssm_norm': out['m_ssm_norm'], 'm_w_ssm_out': out['m_w_ssm_out'], 'm_attn_sinks': out['m_attn_sinks'], 'm_w_attn_out': out['m_w_attn_out'], 'm_w_mix_out': out['m_w_mix_out'], 'm_norm_post_mix': out['m_norm_post_mix'], 'm_norm_pre_ffn': out['m_norm_pre_ffn'], 'm_w_ffn_up': out['m_w_ffn_up'], 'm_ffn_conv_w': out['m_ffn_conv_w'], 'm_ffn_conv_b': out['m_ffn_conv_b'], 'm_w_ffn_down': out['m_w_ffn_down'], 'm_norm_post_ffn': out['m_norm_post_ffn'], 'v_meta_tokens': out['v_meta_tokens'], 'v_norm_pre_mix': out['v_norm_pre_mix'], 'v_w_in': out['v_w_in'], 'v_ssm_conv_w': out['v_ssm_conv_w'], 'v_ssm_conv_b': out['v_ssm_conv_b'], 'v_ssm_dt_bias': out['v_ssm_dt_bias'], 'v_ssm_a_log': out['v_ssm_a_log'], 'v_ssm_d_skip': out['v_ssm_d_skip'], 'v_ssm_norm': out['v_ssm_norm'], 'v_w_ssm_out': out['v_w_ssm_out'], 'v_attn_sinks': out['v_attn_sinks'], 'v_w_attn_out': out['v_w_attn_out'], 'v_w_mix_out': out['v_w_mix_out'], 'v_norm_post_mix': out['v_norm_post_mix'], 'v_norm_pre_ffn': out['v_norm_pre_ffn'], 'v_w_ffn_up': out['v_w_ffn_up'], 'v_ffn_conv_w': out['v_ffn_conv_w'], 'v_ffn_conv_b': out['v_ffn_conv_b'], 'v_w_ffn_down': out['v_w_ffn_down'], 'v_norm_post_ffn': out['v_norm_post_ffn']}


def _loss(weights, diff, rest, loss_target):
    with _jax.named_scope("forward"):
        args = {**rest, TWIN_DIFF_INPUT: diff, **{k: w.astype(_WEIGHT_DTYPES[k]) for k, w in weights.items()}}
        y = _forward(args)
    with _jax.named_scope("loss_head"):
        err = _jnp.square(y.astype(_jnp.float32) - loss_target)
        return 0.5 * _jnp.sum(_jnp.mean(err, axis=-1)) if err.ndim else 0.5 * err


def _adamw(w, g, m, v):
    m = ADAM_B1 * m + (1.0 - ADAM_B1) * g
    v = ADAM_B2 * v + (1.0 - ADAM_B2) * _jnp.square(g)
    m_hat = m / (1.0 - ADAM_B1 ** ADAM_STEP)
    v_hat = v / (1.0 - ADAM_B2 ** ADAM_STEP)
    delta = -ADAM_LR * (m_hat / (_jnp.sqrt(v_hat) + ADAM_EPS) + ADAM_WD * w)
    return delta, m, v


def reference(x, meta_tokens, norm_pre_mix, w_in, ssm_conv_w, ssm_conv_b, ssm_dt_bias, ssm_a_log, ssm_d_skip, ssm_norm, w_ssm_out, attn_sinks, w_attn_out, w_mix_out, norm_post_mix, norm_pre_ffn, w_ffn_up, ffn_conv_w, ffn_conv_b, w_ffn_down, norm_post_ffn, loss_target, m_meta_tokens, m_norm_pre_mix, m_w_in, m_ssm_conv_w, m_ssm_conv_b, m_ssm_dt_bias, m_ssm_a_log, m_ssm_d_skip, m_ssm_norm, m_w_ssm_out, m_attn_sinks, m_w_attn_out, m_w_mix_out, m_norm_post_mix, m_norm_pre_ffn, m_w_ffn_up, m_ffn_conv_w, m_ffn_conv_b, m_w_ffn_down, m_norm_post_ffn, v_meta_tokens, v_norm_pre_mix, v_w_in, v_ssm_conv_w, v_ssm_conv_b, v_ssm_dt_bias, v_ssm_a_log, v_ssm_d_skip, v_ssm_norm, v_w_ssm_out, v_attn_sinks, v_w_attn_out, v_w_mix_out, v_norm_post_mix, v_norm_pre_ffn, v_w_ffn_up, v_ffn_conv_w, v_ffn_conv_b, v_w_ffn_down, v_norm_post_ffn):
    given = dict(x=x, meta_tokens=meta_tokens, norm_pre_mix=norm_pre_mix, w_in=w_in, ssm_conv_w=ssm_conv_w, ssm_conv_b=ssm_conv_b, ssm_dt_bias=ssm_dt_bias, ssm_a_log=ssm_a_log, ssm_d_skip=ssm_d_skip, ssm_norm=ssm_norm, w_ssm_out=w_ssm_out, attn_sinks=attn_sinks, w_attn_out=w_attn_out, w_mix_out=w_mix_out, norm_post_mix=norm_post_mix, norm_pre_ffn=norm_pre_ffn, w_ffn_up=w_ffn_up, ffn_conv_w=ffn_conv_w, ffn_conv_b=ffn_conv_b, w_ffn_down=w_ffn_down, norm_post_ffn=norm_post_ffn, loss_target=loss_target, m_meta_tokens=m_meta_tokens, m_norm_pre_mix=m_norm_pre_mix, m_w_in=m_w_in, m_ssm_conv_w=m_ssm_conv_w, m_ssm_conv_b=m_ssm_conv_b, m_ssm_dt_bias=m_ssm_dt_bias, m_ssm_a_log=m_ssm_a_log, m_ssm_d_skip=m_ssm_d_skip, m_ssm_norm=m_ssm_norm, m_w_ssm_out=m_w_ssm_out, m_attn_sinks=m_attn_sinks, m_w_attn_out=m_w_attn_out, m_w_mix_out=m_w_mix_out, m_norm_post_mix=m_norm_post_mix, m_norm_pre_ffn=m_norm_pre_ffn, m_w_ffn_up=m_w_ffn_up, m_ffn_conv_w=m_ffn_conv_w, m_ffn_conv_b=m_ffn_conv_b, m_w_ffn_down=m_w_ffn_down, m_norm_post_ffn=m_norm_post_ffn, v_meta_tokens=v_meta_tokens, v_norm_pre_mix=v_norm_pre_mix, v_w_in=v_w_in, v_ssm_conv_w=v_ssm_conv_w, v_ssm_conv_b=v_ssm_conv_b, v_ssm_dt_bias=v_ssm_dt_bias, v_ssm_a_log=v_ssm_a_log, v_ssm_d_skip=v_ssm_d_skip, v_ssm_norm=v_ssm_norm, v_w_ssm_out=v_w_ssm_out, v_attn_sinks=v_attn_sinks, v_w_attn_out=v_w_attn_out, v_w_mix_out=v_w_mix_out, v_norm_post_mix=v_norm_post_mix, v_norm_pre_ffn=v_norm_pre_ffn, v_w_ffn_up=v_w_ffn_up, v_ffn_conv_w=v_ffn_conv_w, v_ffn_conv_b=v_ffn_conv_b, v_w_ffn_down=v_w_ffn_down, v_norm_post_ffn=v_norm_post_ffn)
    weights = {n: given[n] for n in TWIN_WEIGHTS}
    shared = {n: given[n] for n in SHARED_INPUTS}
    per_example = {n: given[n] for n in ['x']}
    grad_fn = _jax.value_and_grad(_loss, argnums=(0, 1))

    def one_microbatch(ex, loss_target):
        ex = dict(ex)
        diff = ex.pop(TWIN_DIFF_INPUT)
        return grad_fn(weights, diff, {**shared, **ex}, loss_target)

    if N_MICROBATCH == 1:
        loss, (grad_w, grad_x) = one_microbatch(per_example, given["loss_target"])
    else:
        def body(carry, xs):
            loss_sum, grad_sum = carry
            l_k, (gw_k, gx_k) = one_microbatch(xs[0], xs[1])
            with _jax.named_scope("update"):
                return (loss_sum + l_k, _jax.tree.map(_jnp.add, grad_sum, gw_k)), gx_k

        init = (_jnp.zeros((), _jnp.float32), _jax.tree.map(_jnp.zeros_like, weights))
        (loss, grad_w), grad_x = _jax.lax.scan(body, init, (per_example, given["loss_target"]))
    with _jax.named_scope("update"):
        delta_w, new_m, new_v = {}, {}, {}
        for n in TWIN_WEIGHTS:
            delta_w[n], new_m[n], new_v[n] = _adamw(weights[n], grad_w[n], given["m_" + n], given["v_" + n])
    return (loss, grad_x, *[grad_w[n] for n in TWIN_WEIGHTS], *[delta_w[n] for n in TWIN_WEIGHTS],
            *[new_m[n] for n in TWIN_WEIGHTS], *[new_v[n] for n in TWIN_WEIGHTS])
```

```python
import jax
import jax.numpy as jnp
from jax import lax
from jax.experimental import pallas as pl
from jax.experimental.pallas import tpu as pltpu

F32 = jnp.float32
BF16 = jnp.bfloat16
HIGHEST = lax.Precision.HIGHEST

D_MODEL = 1024
N_META = 16
CHUNK = 128
META_PAD = CHUNK - N_META
D_INNER = 2048
HEAD_P = 64
SSM_HEADS = 32
SSM_GROUPS = 4
HEADS_PER_GROUP = SSM_HEADS // SSM_GROUPS
GROUP_W = HEADS_PER_GROUP * HEAD_P
D_STATE = 128
SSM_CONV = 4
CONV_DIM = D_INNER + 2 * SSM_GROUPS * D_STATE
ATTN_HEADS = 16
KV_HEADS = 4
ATTN_GROUP = ATTN_HEADS // KV_HEADS
DH = 64
KV_W = KV_HEADS * DH
FFN_DIM = 2816
FFN_CONV = 3
EPS = 1e-6
NEG = -1e30
N_DEV = 8
AXES = ("x", "y", "c")

OFF_Z, OFF_GATE, OFF_XBC, OFF_Q, OFF_K, OFF_V, OFF_DT = 0, 2048, 4096, 7168, 8192, 8448, 8704
N_INP = OFF_DT + SSM_GROUPS * 128
CUT_Z, CUT_XBC, CUT_DT, CUT_Q, CUT_K, CUT_V, CUT_G = 0, 2048, 5120, 5152, 6176, 6432, 6688
N_IN = 8736

ADAM_LR, ADAM_B1, ADAM_B2, ADAM_EPS, ADAM_WD, ADAM_STEP = 0.001, 0.9, 0.999, 1e-08, 0.01, 10

VMEM_LIMIT = 56 * 1024 * 1024


def _params(n_grid):
    return pltpu.CompilerParams(dimension_semantics=("arbitrary",) * n_grid, vmem_limit_bytes=VMEM_LIMIT)


def _sds(shape, dtype):
    return jax.ShapeDtypeStruct(shape, dtype)


def _pick(n, prefs):
    for c in prefs:
        if n % c == 0:
            return c
    raise ValueError(f"no tile of {prefs} divides {n}")


def _row(tr, width, cb=0):
    return pl.BlockSpec((tr, width), lambda i: (i, cb))


def _row_rev(tr, width, nt, cb=0):
    return pl.BlockSpec((tr, width), lambda i: (nt - 1 - i, cb))


def _full(shape):
    return pl.BlockSpec(shape, lambda *_: (0,) * len(shape))


def _sigmoid(x):
    return 1.0 / (1.0 + jnp.exp(-x))


def _softplus(x):
    return jnp.maximum(x, 0.0) + jnp.log(1.0 + jnp.exp(-jnp.abs(x)))


def _rms(x):
    return lax.rsqrt(jnp.mean(x * x, axis=-1, keepdims=True) + EPS)


def _rms_bwd(x, r, w, dy):
    xh = x * r
    g = dy * w
    dx = r * (g - xh * jnp.mean(g * xh, axis=-1, keepdims=True))
    return dx, jnp.sum(dy * xh, axis=0, keepdims=True)


def _row_ids(shape, tile_index, tr):
    return tile_index * tr + lax.broadcasted_iota(jnp.int32, shape, 0)


def _shift_down(cur, prev, s):
    if s == 0:
        return cur
    row = lax.broadcasted_iota(jnp.int32, cur.shape, 0)
    return jnp.where(row < s, pltpu.roll(prev, s, 0), pltpu.roll(cur, s, 0))


def _shift_up(cur, nxt, s):
    if s == 0:
        return cur
    n = cur.shape[0]
    row = lax.broadcasted_iota(jnp.int32, cur.shape, 0)
    return jnp.where(row >= n - s, pltpu.roll(nxt, n - s, 0), pltpu.roll(cur, n - s, 0))


def _matmul(a, b, *, ta=False, tb=False, out_dtype=F32, name):
    if ta:
        k_dim, m_dim = a.shape
    else:
        m_dim, k_dim = a.shape
    n_dim = b.shape[0] if tb else b.shape[1]
    tm = _pick(m_dim, (1408, 1024, 768, 512, 384, 256, 128))
    tn = _pick(n_dim, (1024, 1408, 768, 512, 384, 256, 128))
    tk = k_dim if (not ta and k_dim <= 2816) else _pick(k_dim, (1408, 1024, 768, 512, 384, 256, 128))
    nk = k_dim // tk
    dims = (((0 if ta else 1,), (1 if tb else 0,)), ((), ()))

    def body(a_ref, b_ref, o_ref):
        r = lax.dot_general(a_ref[...].astype(BF16), b_ref[...].astype(BF16), dims, preferred_element_type=F32)
        if nk == 1:
            o_ref[...] = r.astype(o_ref.dtype)
        else:
            k = pl.program_id(2)

            @pl.when(k == 0)
            def _():
                o_ref[...] = r

            @pl.when(k > 0)
            def _():
                o_ref[...] += r

    a_spec = pl.BlockSpec((tk, tm), lambda i, j, k: (k, i)) if ta else pl.BlockSpec((tm, tk), lambda i, j, k: (i, k))
    b_spec = pl.BlockSpec((tn, tk), lambda i, j, k: (j, k)) if tb else pl.BlockSpec((tk, tn), lambda i, j, k: (k, j))
    if nk > 1:
        assert out_dtype == F32
    return pl.pallas_call(
        body, grid=(m_dim // tm, n_dim // tn, nk), in_specs=[a_spec, b_spec],
        out_specs=pl.BlockSpec((tm, tn), lambda i, j, k: (i, j)), out_shape=_sds((m_dim, n_dim), out_dtype),
        name=name, compiler_params=_params(3))(a, b)


def _prenorm(h, w):
    t_rows = h.shape[0]
    tr = _pick(t_rows, (384, 128))

    def body(h_ref, w_ref, o_ref):
        x = h_ref[...]
        o_ref[...] = (x * _rms(x) * w_ref[...]).astype(BF16)

    return pl.pallas_call(body, grid=(t_rows // tr,), in_specs=[_row(tr, D_MODEL), _full((1, D_MODEL))],
                          out_specs=_row(tr, D_MODEL), out_shape=_sds((t_rows, D_MODEL), BF16),
                          name="prenorm", compiler_params=_params(1))(h, w)


def _xbc_specs(tr, rev_nt=None):
    cbs = [OFF_XBC // 1024 + j for j in range(CONV_DIM // 1024)]
    if rev_nt is None:
        return [_row(tr, 1024, cb) for cb in cbs]
    return [_row_rev(tr, 1024, rev_nt, cb) for cb in cbs]


def _ssm_conv_fwd(proj, conv_w, conv_b):
    t_rows = proj.shape[0]
    tr = CHUNK

    def body(x0, x1, x2, w_ref, b_ref, xc_ref, xa_ref, prev):
        @pl.when(pl.program_id(0) == 0)
        def _():
            prev[...] = jnp.zeros_like(prev)

        x = jnp.concatenate([x0[...], x1[...], x2[...]], axis=1)
        p = prev[...]
        acc = b_ref[...] + w_ref[SSM_CONV - 1:SSM_CONV, :] * x
        for s in range(1, SSM_CONV):
            acc = acc + w_ref[SSM_CONV - 1 - s:SSM_CONV - s, :] * _shift_down(x, p, s)
        prev[...] = x
        xc_ref[...] = acc
        xa_ref[...] = acc * _sigmoid(acc)

    return pl.pallas_call(
        body, grid=(t_rows // tr,),
        in_specs=_xbc_specs(tr) + [_full((SSM_CONV, CONV_DIM)), _full((1, CONV_DIM))],
        out_specs=[_row(tr, CONV_DIM), _row(tr, CONV_DIM)],
        out_shape=[_sds((t_rows, CONV_DIM), F32), _sds((t_rows, CONV_DIM), F32)],
        scratch_shapes=[pltpu.VMEM((tr, CONV_DIM), F32)],
        name="ssm_conv_fwd", compiler_params=_params(1))(proj, proj, proj, conv_w, conv_b)


def _ssm_post(y, proj, w):
    t_rows = y.shape[0]
    tr = CHUNK

    def body(y_ref, z_ref, w_ref, o_ref):
        z = z_ref[...]
        yz = y_ref[...] * z * _sigmoid(z)
        o_ref[...] = (yz * _rms(yz) * w_ref[...]).astype(BF16)

    return pl.pallas_call(body, grid=(t_rows // tr,),
                          in_specs=[_row(tr, D_INNER), _row(tr, D_INNER, OFF_Z // D_INNER), _full((1, D_INNER))],
                          out_specs=_row(tr, D_INNER), out_shape=_sds((t_rows, D_INNER), BF16),
                          name="ssm_post", compiler_params=_params(1))(y, proj, w)


def _mix_fwd(proj, y_ssm, y_attn):
    t_rows = y_ssm.shape[0]
    tr = _pick(t_rows, (384, 128))

    def body(g_ref, ys_ref, ya_ref, o_ref):
        g = _sigmoid(g_ref[...])
        o_ref[...] = (g[:, :D_MODEL] * ys_ref[...] + g[:, D_MODEL:] * ya_ref[...]).astype(BF16)

    return pl.pallas_call(body, grid=(t_rows // tr,),
                          in_specs=[_row(tr, 2 * D_MODEL, OFF_GATE // (2 * D_MODEL)), _row(tr, D_MODEL), _row(tr, D_MODEL)],
                          out_specs=_row(tr, D_MODEL), out_shape=_sds((t_rows, D_MODEL), BF16),
                          name="mix_fwd", compiler_params=_params(1))(proj, y_ssm, y_attn)


def _postmix(h, mix, w_post, w_pre):
    t_rows = h.shape[0]
    tr = CHUNK

    def body(h_ref, m_ref, wp_ref, wf_ref, h1_ref, hn_ref):
        m = m_ref[...]
        h1 = h_ref[...] + m * _rms(m) * wp_ref[...]
        h1 = jnp.where(_row_ids(h1.shape, pl.program_id(0), tr) >= META_PAD, h1, 0.0)
        h1_ref[...] = h1
        hn_ref[...] = (h1 * _rms(h1) * wf_ref[...]).astype(BF16)

    return pl.pallas_call(body, grid=(t_rows // tr,),
                          in_specs=[_row(tr, D_MODEL), _row(tr, D_MODEL), _full((1, D_MODEL)), _full((1, D_MODEL))],
                          out_specs=[_row(tr, D_MODEL), _row(tr, D_MODEL)],
                          out_shape=[_sds((t_rows, D_MODEL), F32), _sds((t_rows, D_MODEL), BF16)],
                          name="postmix", compiler_params=_params(1))(h, mix, w_post, w_pre)


def _ffn_act(up, conv_w, conv_b):
    t_rows = up.shape[0]
    tr = CHUNK
    width = 2 * FFN_DIM

    def body(up_ref, w_ref, b_ref, u_ref, act_ref, prev):
        @pl.when(pl.program_id(0) == 0)
        def _():
            prev[...] = jnp.zeros_like(prev)

        x = up_ref[...]
        p = prev[...]
        u = b_ref[...] + w_ref[FFN_CONV - 1:FFN_CONV, :] * x
        for s in range(1, FFN_CONV):
            u = u + w_ref[FFN_CONV - 1 - s:FFN_CONV - s, :] * _shift_down(x, p, s)
        prev[...] = x
        u_ref[...] = u
        a = u[:, :FFN_DIM]
        act_ref[...] = (a * _sigmoid(a) * u[:, FFN_DIM:]).astype(BF16)

    return pl.pallas_call(
        body, grid=(t_rows // tr,), in_specs=[_row(tr, width), _full((FFN_CONV, width)), _full((1, width))],
        out_specs=[_row(tr, width), _row(tr, FFN_DIM)],
        out_shape=[_sds((t_rows, width), F32), _sds((t_rows, FFN_DIM), BF16)],
        scratch_shapes=[pltpu.VMEM((tr, width), F32)],
        name="ffn_act", compiler_params=_params(1))(up, conv_w, conv_b)


def _final(h1, f, target, w):
    t_rows = h1.shape[0]
    tr = CHUNK

    def body(h1_ref, f_ref, t_ref, w_ref, df_ref, dy_ref, dw_ref, loss_ref):
        i = pl.program_id(0)

        @pl.when(i == 0)
        def _():
            dw_ref[...] = jnp.zeros_like(dw_ref)
            loss_ref[...] = jnp.zeros_like(loss_ref)

        f_val = f_ref[...]
        r = _rms(f_val)
        wv = w_ref[...]
        h2 = h1_ref[...] + f_val * r * wv
        diff = jnp.where(i >= 1, h2 - t_ref[...], 0.0)
        loss_ref[...] += 0.5 * jnp.sum(diff * diff) * (1.0 / D_MODEL)
        dy = diff * (1.0 / D_MODEL)
        dy_ref[...] = dy
        df, dw = _rms_bwd(f_val, r, wv, dy)
        df_ref[...] = df.astype(BF16)
        dw_ref[...] += dw

    tgt_spec = pl.BlockSpec((tr, D_MODEL), lambda i: (jnp.maximum(i - 1, 0), 0))
    return pl.pallas_call(
        body, grid=(t_rows // tr,),
        in_specs=[_row(tr, D_MODEL), _row(tr, D_MODEL), tgt_spec, _full((1, D_MODEL))],
        out_specs=[_row(tr, D_MODEL), _row(tr, D_MODEL), _full((1, D_MODEL)), _full((1, 128))],
        out_shape=[_sds((t_rows, D_MODEL), BF16), _sds((t_rows, D_MODEL), F32), _sds((1, D_MODEL), F32), _sds((1, 128), F32)],
        name="final", compiler_params=_params(1))(h1, f, target, w)


def _ffn_act_bwd(u, up, dact, conv_w):
    t_rows = u.shape[0]
    tr = CHUNK
    nt = t_rows // tr
    width = 2 * FFN_DIM

    def body(u_ref, up_ref, da_ref, w_ref, dup_ref, dw_ref, db_ref, nxt):
        @pl.when(pl.program_id(0) == 0)
        def _():
            nxt[...] = jnp.zeros_like(nxt)
            dw_ref[...] = jnp.zeros_like(dw_ref)
            db_ref[...] = jnp.zeros_like(db_ref)

        u_val = u_ref[...]
        a, g = u_val[:, :FFN_DIM], u_val[:, FFN_DIM:]
        d = da_ref[...]
        s = _sigmoid(a)
        du = jnp.concatenate([d * g * s * (1.0 + a * (1.0 - s)), d * a * s], axis=1)
        n = nxt[...]
        x = up_ref[...]
        dup = jnp.zeros_like(du)
        for sh in range(FFN_CONV):
            k = FFN_CONV - 1 - sh
            moved = _shift_up(du, n, sh)
            dup = dup + w_ref[k:k + 1, :] * moved
            dw_ref[k:k + 1, :] += jnp.sum(moved * x, axis=0, keepdims=True)
        db_ref[...] += jnp.sum(du, axis=0, keepdims=True)
        nxt[...] = du
        dup_ref[...] = dup.astype(BF16)

    return pl.pallas_call(
        body, grid=(nt,),
        in_specs=[_row_rev(tr, width, nt), _row_rev(tr, width, nt), _row_rev(tr, FFN_DIM, nt), _full((FFN_CONV, width))],
        out_specs=[_row_rev(tr, width, nt), _full((FFN_CONV, width)), _full((1, width))],
        out_shape=[_sds((t_rows, width), BF16), _sds((FFN_CONV, width), F32), _sds((1, width), F32)],
        scratch_shapes=[pltpu.VMEM((tr, width), F32)],
        name="ffn_act_bwd", compiler_params=_params(1))(u, up, dact, conv_w)


def _postmix_bwd(h1, dhn2, dy, mix, w_pre, w_post):
    t_rows = h1.shape[0]
    tr = CHUNK

    def body(h1_ref, dhn_ref, dy_ref, m_ref, wf_ref, wp_ref, dmix_ref, dh_ref, dwf_ref, dwp_ref):
        @pl.when(pl.program_id(0) == 0)
        def _():
            dwf_ref[...] = jnp.zeros_like(dwf_ref)
            dwp_ref[...] = jnp.zeros_like(dwp_ref)

        h1v = h1_ref[...]
        dx, dwf = _rms_bwd(h1v, _rms(h1v), wf_ref[...], dhn_ref[...])
        dwf_ref[...] += dwf
        dh1 = dy_ref[...] + dx
        dh1 = jnp.where(_row_ids(dh1.shape, pl.program_id(0), tr) >= META_PAD, dh1, 0.0)
        dh_ref[...] = dh1
        m = m_ref[...]
        dmix, dwp = _rms_bwd(m, _rms(m), wp_ref[...], dh1)
        dwp_ref[...] += dwp
        dmix_ref[...] = dmix.astype(BF16)

    return pl.pallas_call(
        body, grid=(t_rows // tr,),
        in_specs=[_row(tr, D_MODEL)] * 4 + [_full((1, D_MODEL))] * 2,
        out_specs=[_row(tr, D_MODEL), _row(tr, D_MODEL), _full((1, D_MODEL)), _full((1, D_MODEL))],
        out_shape=[_sds((t_rows, D_MODEL), BF16), _sds((t_rows, D_MODEL), F32), _sds((1, D_MODEL), F32), _sds((1, D_MODEL), F32)],
        name="postmix_bwd", compiler_params=_params(1))(h1, dhn2, dy, mix, w_pre, w_post)


def _mix_bwd(dmixed, proj, y_ssm, y_attn):
    t_rows = dmixed.shape[0]
    tr = _pick(t_rows, (384, 128))

    def body(d_ref, g_ref, ys_ref, ya_ref, dys_ref, dya_ref, dg_ref):
        d = d_ref[...]
        g = _sigmoid(g_ref[...])
        g1, g2 = g[:, :D_MODEL], g[:, D_MODEL:]
        dys_ref[...] = (d * g1).astype(BF16)
        dya_ref[...] = (d * g2).astype(BF16)
        dg_ref[...] = jnp.concatenate([d * ys_ref[...] * g1 * (1.0 - g1), d * ya_ref[...] * g2 * (1.0 - g2)],
                                      axis=1).astype(BF16)

    return pl.pallas_call(
        body, grid=(t_rows // tr,),
        in_specs=[_row(tr, D_MODEL), _row(tr, 2 * D_MODEL, OFF_GATE // (2 * D_MODEL)), _row(tr, D_MODEL), _row(tr, D_MODEL)],
        out_specs=[_row(tr, D_MODEL), _row(tr, D_MODEL), _row(tr, 2 * D_MODEL)],
        out_shape=[_sds((t_rows, D_MODEL), BF16), _sds((t_rows, D_MODEL), BF16), _sds((t_rows, 2 * D_MODEL), BF16)],
        name="mix_bwd", compiler_params=_params(1))(dmixed, proj, y_ssm, y_attn)


def _ssm_post_bwd(y, proj, dyn, w):
    t_rows = y.shape[0]
    tr = CHUNK

    def body(y_ref, z_ref, d_ref, w_ref, dy_ref, dz_ref, dw_ref):
        @pl.when(pl.program_id(0) == 0)
        def _():
            dw_ref[...] = jnp.zeros_like(dw_ref)

        yv, z = y_ref[...], z_ref[...]
        sz = _sigmoid(z)
        silu = z * sz
        yz = yv * silu
        dyz, dw = _rms_bwd(yz, _rms(yz), w_ref[...], d_ref[...])
        dw_ref[...] += dw
        dy_ref[...] = dyz * silu
        dz_ref[...] = (dyz * yv * sz * (1.0 + z * (1.0 - sz))).astype(BF16)

    return pl.pallas_call(
        body, grid=(t_rows // tr,),
        in_specs=[_row(tr, D_INNER), _row(tr, D_INNER, OFF_Z // D_INNER), _row(tr, D_INNER), _full((1, D_INNER))],
        out_specs=[_row(tr, D_INNER), _row(tr, D_INNER), _full((1, D_INNER))],
        out_shape=[_sds((t_rows, D_INNER), F32), _sds((t_rows, D_INNER), BF16), _sds((1, D_INNER), F32)],
        name="ssm_post_bwd", compiler_params=_params(1))(y, proj, dyn, w)


def _ssm_conv_bwd(xc, proj, dxs, dbm, dcm, conv_w):
    t_rows = xc.shape[0]
    tr = CHUNK
    nt = t_rows // tr
    bc_w = SSM_GROUPS * D_STATE

    def body(xc_ref, x0, x1, x2, dxs_ref, db_ref, dc_ref, w_ref, dx_ref, dw_ref, dbias_ref, nxt):
        @pl.when(pl.program_id(0) == 0)
        def _():
            nxt[...] = jnp.zeros_like(nxt)
            dw_ref[...] = jnp.zeros_like(dw_ref)
            dbias_ref[...] = jnp.zeros_like(dbias_ref)

        c = xc_ref[...]
        s = _sigmoid(c)
        dact = jnp.concatenate([dxs_ref[...], db_ref[...], dc_ref[...]], axis=1)
        dpre = dact * s * (1.0 + c * (1.0 - s))
        x = jnp.concatenate([x0[...], x1[...], x2[...]], axis=1)
        n = nxt[...]
        dx = jnp.zeros_like(dpre)
        for sh in range(SSM_CONV):
            k = SSM_CONV - 1 - sh
            moved = _shift_up(dpre, n, sh)
            dx = dx + w_ref[k:k + 1, :] * moved
            dw_ref[k:k + 1, :] += jnp.sum(moved * x, axis=0, keepdims=True)
        dbias_ref[...] += jnp.sum(dpre, axis=0, keepdims=True)
        nxt[...] = dpre
        dx_ref[...] = dx.astype(BF16)

    return pl.pallas_call(
        body, grid=(nt,),
        in_specs=[_row_rev(tr, CONV_DIM, nt)] + _xbc_specs(tr, nt)
        + [_row_rev(tr, D_INNER, nt), _row_rev(tr, bc_w, nt), _row_rev(tr, bc_w, nt), _full((SSM_CONV, CONV_DIM))],
        out_specs=[_row_rev(tr, CONV_DIM, nt), _full((SSM_CONV, CONV_DIM)), _full((1, CONV_DIM))],
        out_shape=[_sds((t_rows, CONV_DIM), BF16), _sds((SSM_CONV, CONV_DIM), F32), _sds((1, CONV_DIM), F32)],
        scratch_shapes=[pltpu.VMEM((tr, CONV_DIM), F32)],
        name="ssm_conv_bwd", compiler_params=_params(1))(xc, proj, proj, proj, dxs, dbm, dcm, conv_w)


def _prenorm_bwd(h, dhn, dh, w):
    t_rows = h.shape[0]
    tr = CHUNK

    def body(h_ref, d_ref, r_ref, w_ref, o_ref, dw_ref):
        @pl.when(pl.program_id(0) == 0)
        def _():
            dw_ref[...] = jnp.zeros_like(dw_ref)

        x = h_ref[...]
        dx, dw = _rms_bwd(x, _rms(x), w_ref[...], d_ref[...])
        dw_ref[...] += dw
        o_ref[...] = r_ref[...] + dx

    return pl.pallas_call(
        body, grid=(t_rows // tr,), in_specs=[_row(tr, D_MODEL)] * 3 + [_full((1, D_MODEL))],
        out_specs=[_row(tr, D_MODEL), _full((1, D_MODEL))],
        out_shape=[_sds((t_rows, D_MODEL), F32), _sds((1, D_MODEL), F32)],
        name="prenorm_bwd", compiler_params=_params(1))(h, dhn, dh, w)


def _ssd_common(dtr_ref, dtb_ref, alog_ref, chunk_index):
    rows = lax.broadcasted_iota(jnp.int32, (CHUNK, CHUNK), 0)
    cols = lax.broadcasted_iota(jnp.int32, (CHUNK, CHUNK), 1)
    low = rows >= cols
    tril = low.astype(F32)
    raw = dtr_ref[...] + dtb_ref[0]
    live = _row_ids(raw.shape, chunk_index, CHUNK) >= META_PAD
    dt = jnp.where(live, _softplus(raw), 0.0)
    a_head = -jnp.exp(alog_ref[0])
    a = dt * a_head
    cs = jnp.dot(tril, a, precision=HIGHEST, preferred_element_type=F32)
    expand = (lax.broadcasted_iota(jnp.int32, (CHUNK, GROUP_W), 1) // HEAD_P
              == lax.broadcasted_iota(jnp.int32, (CHUNK, GROUP_W), 0)).astype(F32)
    dtx = jnp.dot(dt, expand, precision=HIGHEST, preferred_element_type=F32)
    csx = jnp.dot(cs, expand, precision=HIGHEST, preferred_element_type=F32)
    fold = (lax.broadcasted_iota(jnp.int32, (GROUP_W, CHUNK), 0) // HEAD_P
            == lax.broadcasted_iota(jnp.int32, (GROUP_W, CHUNK), 1)).astype(F32)
    return dict(low=low, triu=(rows <= cols).astype(F32), raw=raw, live=live, dt=dt, a_head=a_head, cs=cs, cs_t=cs.T,
                fold=fold, dtx=dtx, csx=csx)


def _decay_matrix(cm, j):
    diff = cm["cs"][:, j:j + 1] - cm["cs_t"][j:j + 1, :]
    return jnp.where(cm["low"], jnp.exp(jnp.where(cm["low"], diff, 0.0)), 0.0)


def _dot(a, b, dims):
    return lax.dot_general(a.astype(BF16), b.astype(BF16), (dims, ((), ())), preferred_element_type=F32)


def _ssd_specs(nt, rev):
    def idx(c):
        return nt - 1 - c if rev else c
    xs = pl.BlockSpec((CHUNK, GROUP_W), lambda g, c: (idx(c), g))
    bm = pl.BlockSpec((CHUNK, D_STATE), lambda g, c: (idx(c), D_INNER // D_STATE + g))
    cm = pl.BlockSpec((CHUNK, D_STATE), lambda g, c: (idx(c), D_INNER // D_STATE + SSM_GROUPS + g))
    dtr = pl.BlockSpec((CHUNK, 128), lambda g, c: (idx(c), OFF_DT // 128 + g))
    par = pl.BlockSpec((1, 1, 128), lambda g, c: (g, 0, 0))
    par_x = pl.BlockSpec((1, 1, GROUP_W), lambda g, c: (g, 0, 0))
    return xs, bm, cm, dtr, par, par_x, idx


def _ssd_fwd(xact, proj, dtb, alog, dskip_x):
    t_rows = xact.shape[0]
    nt = t_rows // CHUNK
    xs_spec, b_spec, c_spec, dtr_spec, par, par_x, _ = _ssd_specs(nt, False)

    def body(xs_ref, b_ref, c_ref, dtr_ref, dtb_ref, alog_ref, dsk_ref, y_ref, hst_ref, state):
        c = pl.program_id(1)

        @pl.when(c == 0)
        def _():
            state[...] = jnp.zeros_like(state)

        cm = _ssd_common(dtr_ref, dtb_ref, alog_ref, c)
        xs, bm, cmat = xs_ref[...], b_ref[...], c_ref[...]
        x_dt = xs * cm["dtx"]
        h_in = state[...]
        hst_ref[0, 0] = h_in
        y_ref[...] = _dot(cmat, h_in, ((1,), (0,))) * jnp.exp(cm["csx"]) + xs * dsk_ref[0]
        cb = _dot(cmat, bm, ((1,), (1,)))
        for j in range(HEADS_PER_GROUP):
            sl = slice(j * HEAD_P, (j + 1) * HEAD_P)
            y_ref[:, sl] += _dot(cb * _decay_matrix(cm, j), x_dt[:, sl], ((1,), (0,)))
        cs_last = cm["csx"][CHUNK - 1:CHUNK, :]
        state[...] = h_in * jnp.exp(cs_last) + _dot(bm, x_dt * jnp.exp(cs_last - cm["csx"]), ((0,), (0,)))

    return pl.pallas_call(
        body, grid=(SSM_GROUPS, nt),
        in_specs=[xs_spec, b_spec, c_spec, dtr_spec, par, par, par_x],
        out_specs=[xs_spec, pl.BlockSpec((1, 1, D_STATE, GROUP_W), lambda g, c: (c, g, 0, 0))],
        out_shape=[_sds((t_rows, D_INNER), F32), _sds((nt, SSM_GROUPS, D_STATE, GROUP_W), F32)],
        scratch_shapes=[pltpu.VMEM((D_STATE, GROUP_W), F32)],
        name="ssd_fwd", compiler_params=_params(2))(xact, xact, xact, proj, dtb, alog, dskip_x)


def _ssd_bwd(xact, proj, dtb, alog, dskip_x, dy, hst):
    t_rows = xact.shape[0]
    nt = t_rows // CHUNK
    xs_spec, b_spec, c_spec, dtr_spec, par, par_x, idx = _ssd_specs(nt, True)
    h_spec = pl.BlockSpec((1, 1, D_STATE, GROUP_W), lambda g, c: (idx(c), g, 0, 0))
    hn_spec = pl.BlockSpec((1, 1, D_STATE, GROUP_W), lambda g, c: (jnp.minimum(idx(c) + 1, nt - 1), g, 0, 0))
    bc_out = pl.BlockSpec((CHUNK, D_STATE), lambda g, c: (idx(c), g))

    def body(xs_ref, b_ref, c_ref, dtr_ref, dtb_ref, alog_ref, dsk_ref, dy_ref, h_ref, hn_ref,
             dxs_ref, db_ref, dc_ref, ddt_ref, dalog_ref, ddtb_ref, dd_ref, dstate, dx_buf):
        step = pl.program_id(1)

        @pl.when(step == 0)
        def _():
            dstate[...] = jnp.zeros_like(dstate)
            dalog_ref[...] = jnp.zeros_like(dalog_ref)
            ddtb_ref[...] = jnp.zeros_like(ddtb_ref)
            dd_ref[...] = jnp.zeros_like(dd_ref)

        cm = _ssd_common(dtr_ref, dtb_ref, alog_ref, idx(step))
        xs, bm, cmat = xs_ref[...], b_ref[...], c_ref[...]
        dsk = dsk_ref[0]
        x_dt = xs * cm["dtx"]
        h_in, h_next = h_ref[0, 0], hn_ref[0, 0]
        dyv = dy_ref[...]
        dh = dstate[...]
        grow = jnp.exp(cm["csx"])
        cs_last = cm["csx"][CHUNK - 1:CHUNK, :]
        fade = jnp.exp(cs_last - cm["csx"])
        dy_grow = dyv * grow
        x_fade = x_dt * fade
        cb = _dot(cmat, bm, ((1,), (1,)))
        ml = jnp.zeros((CHUNK, CHUNK), F32)
        row_id = lax.broadcasted_iota(jnp.int32, (CHUNK, CHUNK), 0)
        col_id = lax.broadcasted_iota(jnp.int32, (CHUNK, CHUNK), 1)
        w_rows = jnp.zeros((CHUNK, CHUNK), F32)
        w_cols = jnp.zeros((CHUNK, CHUNK), F32)
        for j in range(HEADS_PER_GROUP):
            sl = slice(j * HEAD_P, (j + 1) * HEAD_P)
            lm = _decay_matrix(cm, j)
            mlj = _dot(dyv[:, sl], x_dt[:, sl], ((1,), (1,))) * lm
            ml = ml + mlj
            wm = mlj * cb
            w_rows = jnp.where(col_id == j, jnp.sum(wm, axis=1, keepdims=True), w_rows)
            w_cols = jnp.where(row_id == j, jnp.sum(wm, axis=0, keepdims=True), w_cols)
            dx_buf[:, sl] = _dot(cb * lm, dyv[:, sl], ((0,), (0,)))
        dx_off = fade * _dot(bm, dh, ((1,), (0,)))
        dx = dx_buf[...] + dx_off
        dc_ref[...] = _dot(ml, bm, ((1,), (0,))) + _dot(dy_grow, h_in, ((1,), (1,)))
        db_ref[...] = _dot(ml, cmat, ((0,), (0,))) + _dot(x_fade, dh, ((1,), (1,)))
        fold = cm["fold"]
        y_off = _dot(cmat, h_in, ((1,), (0,))) * grow
        dcs = (w_rows - w_cols.T) + jnp.dot(dyv * y_off - x_dt * dx_off, fold, precision=HIGHEST, preferred_element_type=F32)
        tail = jnp.broadcast_to(jnp.sum(dh * h_next, axis=0, keepdims=True), (8, GROUP_W))
        tail = jnp.dot(tail, fold, precision=HIGHEST, preferred_element_type=F32)[0:1, :]
        last_row = lax.broadcasted_iota(jnp.int32, (CHUNK, 128), 0) == CHUNK - 1
        dcs = dcs + jnp.where(last_row, tail, 0.0)
        da = jnp.dot(cm["triu"], dcs, precision=HIGHEST, preferred_element_type=F32)
        ddt = da * cm["a_head"] + jnp.dot(dx * xs, fold, precision=HIGHEST, preferred_element_type=F32)
        ddt_raw = jnp.where(cm["live"], ddt * _sigmoid(cm["raw"]), 0.0)
        ddt_ref[...] = ddt_raw.astype(BF16)
        ddtb_ref[0] += jnp.sum(ddt_raw, axis=0, keepdims=True)
        dalog_ref[0] += jnp.sum(da * cm["dt"], axis=0, keepdims=True) * cm["a_head"]
        dd_ref[0] += jnp.sum(dyv * xs, axis=0, keepdims=True)
        dxs_ref[...] = dx * cm["dtx"] + dyv * dsk
        dstate[...] = dh * jnp.exp(cs_last) + _dot(cmat, dy_grow, ((0,), (0,)))

    return pl.pallas_call(
        body, grid=(SSM_GROUPS, nt),
        in_specs=[xs_spec, b_spec, c_spec, dtr_spec, par, par, par_x, xs_spec, h_spec, hn_spec],
        out_specs=[xs_spec, bc_out, bc_out, bc_out, par, par, par_x],
        out_shape=[_sds((t_rows, D_INNER), F32), _sds((t_rows, SSM_GROUPS * D_STATE), F32),
                   _sds((t_rows, SSM_GROUPS * D_STATE), F32), _sds((t_rows, SSM_GROUPS * 128), BF16),
                   _sds((SSM_GROUPS, 1, 128), F32), _sds((SSM_GROUPS, 1, 128), F32), _sds((SSM_GROUPS, 1, GROUP_W), F32)],
        scratch_shapes=[pltpu.VMEM((D_STATE, GROUP_W), F32), pltpu.VMEM((CHUNK, GROUP_W), F32)],
        name="ssd_bwd", compiler_params=_params(2))(xact, xact, xact, proj, dtb, alog, dskip_x, dy, hst, hst)


def _swa_probs(q_ref, kp_ref, kc_ref, km_ref, sink_ref, kv_head, n):
    rows_q = ATTN_GROUP * CHUNK
    q = q_ref[...]
    qs = jnp.concatenate([q[:, g * DH:(g + 1) * DH] for g in range(ATTN_GROUP)], axis=0)
    kcat = jnp.concatenate([kp_ref[0], kc_ref[0]], axis=0)
    kmeta = km_ref[0][META_PAD:, :]
    scale = DH ** -0.5
    head = (kv_head * ATTN_GROUP + lax.broadcasted_iota(jnp.int32, (rows_q, 1), 0) // CHUNK + 1).astype(F32)
    slope = jnp.exp(head * (-8.0 / ATTN_HEADS * 0.6931471805599453))
    q_in = lax.broadcasted_iota(jnp.int32, (rows_q, 2 * CHUNK), 0) % CHUNK
    s_in = lax.broadcasted_iota(jnp.int32, (rows_q, 2 * CHUNK), 1)
    dist = q_in - s_in + CHUNK
    ok = (dist >= 0) & (dist < CHUNK) & (s_in + n * CHUNK >= 2 * CHUNK)
    s_band = _dot(qs, kcat, ((1,), (1,))) * scale - slope * dist.astype(F32)
    s_band = jnp.where(ok, s_band, NEG)
    q_pos = lax.broadcasted_iota(jnp.int32, (rows_q, N_META), 0) % CHUNK + n * CHUNK - META_PAD
    ok_m = lax.broadcasted_iota(jnp.int32, (rows_q, N_META), 1) <= q_pos
    s_meta = jnp.where(ok_m, _dot(qs, kmeta, ((1,), (1,))) * scale, NEG)
    sink = sink_ref[0]
    m = jnp.maximum(jnp.maximum(jnp.max(s_band, axis=1, keepdims=True), jnp.max(s_meta, axis=1, keepdims=True)), sink)
    p_band, p_meta, p_sink = jnp.exp(s_band - m), jnp.exp(s_meta - m), jnp.exp(sink - m)
    inv = 1.0 / (jnp.sum(p_band, axis=1, keepdims=True) + jnp.sum(p_meta, axis=1, keepdims=True) + p_sink)
    return qs, kcat, kmeta, p_band * inv, p_meta * inv, p_sink * inv


def _swa_specs(nt, rev):
    def idx(n):
        return nt - 1 - n if rev else n
    q = pl.BlockSpec((CHUNK, ATTN_GROUP * DH), lambda k, n: (idx(n), k))
    q_proj = pl.BlockSpec((CHUNK, ATTN_GROUP * DH), lambda k, n: (idx(n), OFF_Q // (ATTN_GROUP * DH) + k))
    cur = pl.BlockSpec((1, CHUNK, DH), lambda k, n: (k, idx(n), 0))
    prev = pl.BlockSpec((1, CHUNK, DH), lambda k, n: (k, jnp.maximum(idx(n) - 1, 0), 0))
    first = pl.BlockSpec((1, CHUNK, DH), lambda k, n: (k, 0, 0))
    sink = pl.BlockSpec((1, ATTN_GROUP * CHUNK, 1), lambda k, n: (k, 0, 0))
    return q, q_proj, cur, prev, first, sink, idx


def _swa_fwd(proj, k_heads, v_heads, sink_rows):
    t_rows = proj.shape[0]
    nt = t_rows // CHUNK
    o_spec, q_spec, cur, prev, first, sink_spec, _ = _swa_specs(nt, False)

    def body(q_ref, kp_ref, kc_ref, km_ref, vp_ref, vc_ref, vm_ref, sink_ref, o_ref):
        _, _, _, p_band, p_meta, _ = _swa_probs(q_ref, kp_ref, kc_ref, km_ref, sink_ref, pl.program_id(0), pl.program_id(1))
        vcat = jnp.concatenate([vp_ref[0], vc_ref[0]], axis=0)
        out = _dot(p_band, vcat, ((1,), (0,))) + _dot(p_meta, vm_ref[0][META_PAD:, :], ((1,), (0,)))
        for g in range(ATTN_GROUP):
            o_ref[:, g * DH:(g + 1) * DH] = out[g * CHUNK:(g + 1) * CHUNK, :]

    return pl.pallas_call(
        body, grid=(KV_HEADS, nt), in_specs=[q_spec, prev, cur, first, prev, cur, first, sink_spec],
        out_specs=o_spec, out_shape=_sds((t_rows, ATTN_HEADS * DH), F32),
        name="swa_fwd", compiler_params=_params(2))(proj, k_heads, k_heads, k_heads, v_heads, v_heads, v_heads, sink_rows)


def _swa_bwd(proj, k_heads, v_heads, sink_rows, out, dout):
    t_rows = proj.shape[0]
    nt = t_rows // CHUNK
    o_spec, q_spec, cur, prev, first, sink_spec, idx = _swa_specs(nt, True)
    scale = DH ** -0.5

    def body(q_ref, kp_ref, kc_ref, km_ref, vp_ref, vc_ref, vm_ref, sink_ref, o_ref, do_ref,
             dq_ref, dk_ref, dv_ref, dsink_ref, carry_k, carry_v, meta_k, meta_v):
        step = pl.program_id(1)
        n = idx(step)

        @pl.when(step == 0)
        def _():
            carry_k[...] = jnp.zeros_like(carry_k)
            carry_v[...] = jnp.zeros_like(carry_v)
            meta_k[...] = jnp.zeros_like(meta_k)
            meta_v[...] = jnp.zeros_like(meta_v)
            dsink_ref[...] = jnp.zeros_like(dsink_ref)

        qs, kcat, kmeta, p_band, p_meta, p_sink = _swa_probs(q_ref, kp_ref, kc_ref, km_ref, sink_ref, pl.program_id(0), n)
        vcat = jnp.concatenate([vp_ref[0], vc_ref[0]], axis=0)
        vmeta = vm_ref[0][META_PAD:, :]
        o, do = o_ref[...], do_ref[...]
        os_ = jnp.concatenate([o[:, g * DH:(g + 1) * DH] for g in range(ATTN_GROUP)], axis=0)
        dos = jnp.concatenate([do[:, g * DH:(g + 1) * DH] for g in range(ATTN_GROUP)], axis=0)
        delta = jnp.sum(dos * os_, axis=1, keepdims=True)
        ds_band = p_band * (_dot(dos, vcat, ((1,), (1,))) - delta)
        ds_meta = p_meta * (_dot(dos, vmeta, ((1,), (1,))) - delta)
        ds_sink = -p_sink * delta
        dqs = (_dot(ds_band, kcat, ((1,), (0,))) + _dot(ds_meta, kmeta, ((1,), (0,)))) * scale
        for g in range(ATTN_GROUP):
            dq_ref[:, g * DH:(g + 1) * DH] = dqs[g * CHUNK:(g + 1) * CHUNK, :]
            dsink_ref[0, g:g + 1, :] += jnp.sum(ds_sink[g * CHUNK:(g + 1) * CHUNK, :])
        dkcat = _dot(ds_band, qs, ((0,), (0,))) * scale
        dvcat = _dot(p_band, dos, ((0,), (0,)))
        meta_k[...] += _dot(ds_meta, qs, ((0,), (0,))) * scale
        meta_v[...] += _dot(p_meta, dos, ((0,), (0,)))
        dk_ref[0] = dkcat[CHUNK:, :] + carry_k[...]
        dv_ref[0] = dvcat[CHUNK:, :] + carry_v[...]
        carry_k[...] = dkcat[:CHUNK, :]
        carry_v[...] = dvcat[:CHUNK, :]

        @pl.when(n == 0)
        def _():
            dk_ref[0, META_PAD:, :] += meta_k[...]
            dv_ref[0, META_PAD:, :] += meta_v[...]

    return pl.pallas_call(
        body, grid=(KV_HEADS, nt),
        in_specs=[q_spec, prev, cur, first, prev, cur, first, sink_spec, o_spec, o_spec],
        out_specs=[o_spec, cur, cur, pl.BlockSpec((1, 8, 128), lambda k, n: (k, 0, 0))],
        out_shape=[_sds((t_rows, ATTN_HEADS * DH), F32), _sds((KV_HEADS, t_rows, DH), F32),
                   _sds((KV_HEADS, t_rows, DH), F32), _sds((KV_HEADS, 8, 128), F32)],
        scratch_shapes=[pltpu.VMEM((CHUNK, DH), F32), pltpu.VMEM((CHUNK, DH), F32),
                        pltpu.VMEM((N_META, DH), F32), pltpu.VMEM((N_META, DH), F32)],
        name="swa_bwd", compiler_params=_params(2))(proj, k_heads, k_heads, k_heads, v_heads, v_heads, v_heads,
                                                    sink_rows, out, dout)


def _pack_w_in(w_in):
    w_dt = w_in[:, CUT_DT:CUT_Q].reshape(D_MODEL, SSM_GROUPS, HEADS_PER_GROUP)
    w_dt = jnp.pad(w_dt, ((0, 0), (0, 0), (0, 128 - HEADS_PER_GROUP))).reshape(D_MODEL, SSM_GROUPS * 128)
    return jnp.concatenate([w_in[:, CUT_Z:CUT_XBC], w_in[:, CUT_G:], w_in[:, CUT_XBC:CUT_DT], w_in[:, CUT_Q:CUT_K],
                            w_in[:, CUT_K:CUT_V], w_in[:, CUT_V:CUT_G], w_dt], axis=1)


def _unpack_w_in(wp):
    w_dt = wp[:, OFF_DT:].reshape(D_MODEL, SSM_GROUPS, 128)[:, :, :HEADS_PER_GROUP].reshape(D_MODEL, SSM_HEADS)
    return jnp.concatenate([wp[:, OFF_Z:OFF_GATE], wp[:, OFF_XBC:OFF_Q], w_dt, wp[:, OFF_Q:OFF_K], wp[:, OFF_K:OFF_V],
                            wp[:, OFF_V:OFF_DT], wp[:, OFF_GATE:OFF_XBC]], axis=1)


def _group_rows(v, width):
    return jnp.pad(v.reshape(SSM_GROUPS, 1, HEADS_PER_GROUP), ((0, 0), (0, 0), (0, width - HEADS_PER_GROUP)))


def _to_heads(a):
    return a.reshape(a.shape[0], KV_HEADS, DH).transpose(1, 0, 2)


def _from_heads(a):
    return a.transpose(1, 0, 2).reshape(a.shape[1], KV_W)


def _local_step(x, target, wt):
    seq = x.shape[0]
    h = jnp.concatenate([jnp.zeros((META_PAD, D_MODEL), F32), wt["meta_tokens"], x], axis=0)
    wp = _pack_w_in(wt["w_in"])
    dtb = _group_rows(wt["ssm_dt_bias"].reshape(-1), 128)
    alog = _group_rows(wt["ssm_a_log"].reshape(-1), 128)
    dskip_x = jnp.repeat(wt["ssm_d_skip"].reshape(-1), HEAD_P).reshape(SSM_GROUPS, 1, GROUP_W)
    sink_rows = jnp.repeat(wt["attn_sinks"].reshape(KV_HEADS, ATTN_GROUP), CHUNK, axis=1).reshape(KV_HEADS, ATTN_GROUP * CHUNK, 1)

    hn = _prenorm(h, wt["norm_pre_mix"])
    proj = _matmul(hn, wp, name="in_proj")
    xc, xact = _ssm_conv_fwd(proj, wt["ssm_conv_w"], wt["ssm_conv_b"])
    y, hst = _ssd_fwd(xact, proj, dtb, alog, dskip_x)
    yn = _ssm_post(y, proj, wt["ssm_norm"])
    y_ssm = _matmul(yn, wt["w_ssm_out"], name="ssm_out")
    k_heads = _to_heads(proj[:, OFF_K:OFF_V])
    v_heads = _to_heads(proj[:, OFF_V:OFF_DT])
    attn = _swa_fwd(proj, k_heads, v_heads, sink_rows)
    y_attn = _matmul(attn, wt["w_attn_out"], name="attn_out")
    mixed = _mix_fwd(proj, y_ssm, y_attn)
    mix = _matmul(mixed, wt["w_mix_out"], name="mix_out")
    h1, hn2 = _postmix(h, mix, wt["norm_post_mix"], wt["norm_pre_ffn"])
    up = _matmul(hn2, wt["w_ffn_up"], name="ffn_up")
    u, act = _ffn_act(up, wt["ffn_conv_w"], wt["ffn_conv_b"])
    f = _matmul(act, wt["w_ffn_down"], name="ffn_down")
    df, dy, g_norm_post_ffn, loss_row = _final(h1, f, target, wt["norm_post_ffn"])

    grads = {"norm_post_ffn": g_norm_post_ffn}
    dact = _matmul(df, wt["w_ffn_down"], tb=True, name="d_act")
    grads["w_ffn_down"] = _matmul(act, df, ta=True, name="dw_ffn_down")
    dup, grads["ffn_conv_w"], grads["ffn_conv_b"] = _ffn_act_bwd(u, up, dact, wt["ffn_conv_w"])
    dhn2 = _matmul(dup, wt["w_ffn_up"], tb=True, name="d_hn2")
    grads["w_ffn_up"] = _matmul(hn2, dup, ta=True, name="dw_ffn_up")
    dmix, dh, grads["norm_pre_ffn"], grads["norm_post_mix"] = _postmix_bwd(h1, dhn2, dy, mix, wt["norm_pre_ffn"], wt["norm_post_mix"])
    dmixed = _matmul(dmix, wt["w_mix_out"], tb=True, name="d_mixed")
    grads["w_mix_out"] = _matmul(mixed, dmix, ta=True, name="dw_mix_out")
    dy_ssm, dy_attn, dglog = _mix_bwd(dmixed, proj, y_ssm, y_attn)
    dyn = _matmul(dy_ssm, wt["w_ssm_out"], tb=True, name="d_yn")
    grads["w_ssm_out"] = _matmul(yn, dy_ssm, ta=True, name="dw_ssm_out")
    dattn = _matmul(dy_attn, wt["w_attn_out"], tb=True, name="d_attn")
    grads["w_attn_out"] = _matmul(attn, dy_attn, ta=True, name="dw_attn_out")
    dy_ssd, dz, grads["ssm_norm"] = _ssm_post_bwd(y, proj, dyn, wt["ssm_norm"])
    dxs, dbm, dcm, ddt, dalog, ddtb, dd_x = _ssd_bwd(xact, proj, dtb, alog, dskip_x, dy_ssd, hst)
    grads["ssm_a_log"] = dalog[:, 0, :HEADS_PER_GROUP].reshape(1, SSM_HEADS)
    grads["ssm_dt_bias"] = ddtb[:, 0, :HEADS_PER_GROUP].reshape(1, SSM_HEADS)
    grads["ssm_d_skip"] = dd_x.reshape(SSM_HEADS, HEAD_P).sum(axis=1).reshape(1, SSM_HEADS)
    dxbc, grads["ssm_conv_w"], grads["ssm_conv_b"] = _ssm_conv_bwd(xc, proj, dxs, dbm, dcm, wt["ssm_conv_w"])
    dq, dk_heads, dv_heads, dsink = _swa_bwd(proj, k_heads, v_heads, sink_rows, attn, dattn)
    grads["attn_sinks"] = dsink[:, :ATTN_GROUP, 0].reshape(1, ATTN_HEADS)
    dproj = jnp.concatenate([dz, dglog, dxbc, dq.astype(BF16), _from_heads(dk_heads).astype(BF16),
                             _from_heads(dv_heads).astype(BF16), ddt], axis=1)
    dhn = _matmul(dproj, wp, tb=True, name="d_hn")
    grads["w_in"] = _unpack_w_in(_matmul(hn, dproj, ta=True, name="dw_in"))
    dh_all, grads["norm_pre_mix"] = _prenorm_bwd(h, dhn, dh, wt["norm_pre_mix"])
    grads["meta_tokens"] = dh_all[META_PAD:CHUNK]
    return loss_row[0, 0], dh_all[CHUNK:CHUNK + seq], grads


def _all_gather(shards):
    n = len(shards)

    def body(*refs):
        ins, outs = refs[:n], refs[n:2 * n]
        send_sems, recv_sems, local_sems = refs[2 * n:]
        x, y, c = lax.axis_index("x"), lax.axis_index("y"), lax.axis_index("c")
        me, sibling = (x, y, c), (x, y, 1 - c)
        chips = [(1 - x, y), (x, 1 - y), (1 - x, 1 - y)]

        def slot(a, dev):
            return outs[a].at[4 * dev[0] + 2 * dev[1] + dev[2]]

        def copy(k, a, block, to, src=None):
            return pltpu.make_async_remote_copy(
                src_ref=slot(a, block) if src is None else src, dst_ref=slot(a, block),
                send_sem=send_sems.at[k, a], recv_sem=recv_sems.at[k, a],
                device_id=to, device_id_type=pl.DeviceIdType.MESH)

        mine = [pltpu.make_async_copy(ins[a], slot(a, me), local_sems.at[a]) for a in range(n)]
        for cp in mine:
            cp.start()
        first = [copy(0, a, me, sibling, src=ins[a]) for a in range(n)]
        for j, chip in enumerate(chips):
            first += [copy(1 + j, a, me, (*chip, c), src=ins[a]) for a in range(n)]
        for cp in first:
            cp.start()
        passed = []
        for j, chip in enumerate(chips):
            for a in range(n):
                copy(1 + j, a, (*chip, c), me).wait_recv()
                fwd = copy(4 + j, a, (*chip, c), sibling)
                fwd.start()
                passed.append(fwd)
        for a in range(n):
            copy(0, a, sibling, me).wait_recv()
        for j, chip in enumerate(chips):
            for a in range(n):
                copy(4 + j, a, (*chip, 1 - c), me).wait_recv()
        for cp in first + passed:
            cp.wait_send()
        for cp in mine:
            cp.wait()

    hbm = pl.BlockSpec(memory_space=pl.ANY)
    return pl.pallas_call(
        body, in_specs=[hbm] * n, out_specs=[hbm] * n,
        out_shape=[_sds((N_DEV,) + s.shape, s.dtype) for s in shards],
        scratch_shapes=[pltpu.SemaphoreType.DMA((7, n)), pltpu.SemaphoreType.DMA((7, n)), pltpu.SemaphoreType.DMA((n,))],
        name="gather_weights")(*shards)


def _exchange_partials(parts):
    n = len(parts)

    def body(*refs):
        ins, outs = refs[:n], refs[n:2 * n]
        send_sems, recv_sems, local_sems = refs[2 * n:]
        x, y, c = lax.axis_index("x"), lax.axis_index("y"), lax.axis_index("c")
        my_id = 4 * x + 2 * y + c

        def peer(k):
            return (x ^ ((k >> 2) & 1), y ^ ((k >> 1) & 1), c ^ (k & 1))

        def copy(k, a):
            p = peer(k)
            p_id = 4 * p[0] + 2 * p[1] + p[2]
            return pltpu.make_async_remote_copy(
                src_ref=ins[a].at[p_id], dst_ref=outs[a].at[my_id],
                send_sem=send_sems.at[k - 1, a], recv_sem=recv_sems.at[k - 1, a],
                device_id=p, device_id_type=pl.DeviceIdType.MESH)

        def arrival(k, a):
            p = peer(k)
            p_id = 4 * p[0] + 2 * p[1] + p[2]
            return pltpu.make_async_remote_copy(
                src_ref=ins[a].at[p_id], dst_ref=outs[a].at[p_id],
                send_sem=send_sems.at[k - 1, a], recv_sem=recv_sems.at[k - 1, a],
                device_id=p, device_id_type=pl.DeviceIdType.MESH)

        mine = [pltpu.make_async_copy(ins[a].at[my_id], outs[a].at[my_id], local_sems.at[a]) for a in range(n)]
        for cp in mine:
            cp.start()
        sends = [copy(k, a) for k in range(1, N_DEV) for a in range(n)]
        for cp in sends:
            cp.start()
        for k in range(1, N_DEV):
            for a in range(n):
                arrival(k, a).wait_recv()
        for cp in sends:
            cp.wait_send()
        for cp in mine:
            cp.wait()

    hbm = pl.BlockSpec(memory_space=pl.ANY)
    return pl.pallas_call(
        body, in_specs=[hbm] * n, out_specs=[hbm] * n, out_shape=[_sds(p.shape, p.dtype) for p in parts],
        scratch_shapes=[pltpu.SemaphoreType.DMA((7, n)), pltpu.SemaphoreType.DMA((7, n)), pltpu.SemaphoreType.DMA((n,))],
        name="exchange_grads")(*parts)


def _adamw(parts, w, m, v, name):
    rows, cols = w.shape
    tr = _pick(rows, (256, 128, 176, 64, 32, 16, 8))

    def body(p_ref, w_ref, m_ref, v_ref, g_ref, d_ref, nm_ref, nv_ref):
        g = p_ref[0].astype(F32)
        for s in range(1, N_DEV):
            g = g + p_ref[s].astype(F32)
        m_new = ADAM_B1 * m_ref[...] + (1.0 - ADAM_B1) * g
        v_new = ADAM_B2 * v_ref[...] + (1.0 - ADAM_B2) * (g * g)
        m_hat = m_new / (1.0 - ADAM_B1 ** ADAM_STEP)
        v_hat = v_new / (1.0 - ADAM_B2 ** ADAM_STEP)
        g_ref[...] = g
        d_ref[...] = -ADAM_LR * (m_hat / (jnp.sqrt(v_hat) + ADAM_EPS) + ADAM_WD * w_ref[...])
        nm_ref[...] = m_new
        nv_ref[...] = v_new

    spec = _row(tr, cols)
    return pl.pallas_call(
        body, grid=(rows // tr,), in_specs=[pl.BlockSpec((N_DEV, tr, cols), lambda i: (0, i, 0)), spec, spec, spec],
        out_specs=[spec] * 4, out_shape=[_sds((rows, cols), F32)] * 4,
        name=name, compiler_params=_params(1))(parts, w, m, v)


SMALL_REPLICATED = (("norm_pre_mix", 1024), ("ssm_conv_b", 3072), ("ssm_dt_bias", 32), ("ssm_a_log", 32),
                    ("ssm_d_skip", 32), ("ssm_norm", 2048), ("attn_sinks", 16), ("norm_post_mix", 1024),
                    ("norm_pre_ffn", 1024), ("ffn_conv_b", 5632), ("norm_post_ffn", 1024))
SMALL_SHARDED = (("meta_tokens", (N_META, D_MODEL // N_DEV)), ("ssm_conv_w", (SSM_CONV, CONV_DIM // N_DEV)),
                 ("ffn_conv_w", (FFN_CONV, 2 * FFN_DIM // N_DEV)))
BIG = (("w_in", (D_MODEL, N_IN // N_DEV), 1), ("w_ssm_out", (D_INNER // N_DEV, D_MODEL), 0),
       ("w_attn_out", (D_MODEL // N_DEV, D_MODEL), 0), ("w_mix_out", (D_MODEL // N_DEV, D_MODEL), 0),
       ("w_ffn_up", (D_MODEL, 2 * FFN_DIM // N_DEV), 1), ("w_ffn_down", (FFN_DIM // N_DEV, D_MODEL), 0))


def _rows_of(size):
    return -(-size // 128)


def _as_rows(flat):
    size = flat.shape[-1]
    rows = _rows_of(size)
    flat = jnp.pad(flat, [(0, 0)] * (flat.ndim - 1) + [(0, rows * 128 - size)])
    return flat.reshape(flat.shape[:-1] + (rows, 128))


def _pack_small(rep, sharded):
    pieces = [_as_rows(rep[name].reshape(-1)) for name, _ in SMALL_REPLICATED]
    pieces += [_as_rows(sharded[name].reshape(-1)) for name, _ in SMALL_SHARDED]
    packed = jnp.concatenate(pieces, axis=0)
    return jnp.pad(packed, ((0, -packed.shape[0] % 8), (0, 0)))


def _unpack_small(packed):
    out, row = {}, 0
    for name, size in SMALL_REPLICATED:
        out[name] = packed[row:row + _rows_of(size)].reshape(-1)[:size].reshape(1, size)
        row += _rows_of(size)
    for name, (r, c) in SMALL_SHARDED:
        out[name] = packed[row:row + _rows_of(r * c)].reshape(-1)[:r * c].reshape(r, c)
        row += _rows_of(r * c)
    return out


def _shard_major(g, shape, axis):
    r, c = shape
    if axis == 0:
        return g.reshape(N_DEV, r, c)
    return g.reshape(r, N_DEV, c).transpose(1, 0, 2)


def kernel(x, meta_tokens, norm_pre_mix, w_in, ssm_conv_w, ssm_conv_b, ssm_dt_bias, ssm_a_log, ssm_d_skip, ssm_norm, w_ssm_out, attn_sinks, w_attn_out, w_mix_out, norm_post_mix, norm_pre_ffn, w_ffn_up, ffn_conv_w, ffn_conv_b, w_ffn_down, norm_post_ffn, loss_target, m_meta_tokens, m_norm_pre_mix, m_w_in, m_ssm_conv_w, m_ssm_conv_b, m_ssm_dt_bias, m_ssm_a_log, m_ssm_d_skip, m_ssm_norm, m_w_ssm_out, m_attn_sinks, m_w_attn_out, m_w_mix_out, m_norm_post_mix, m_norm_pre_ffn, m_w_ffn_up, m_ffn_conv_w, m_ffn_conv_b, m_w_ffn_down, m_norm_post_ffn, v_meta_tokens, v_norm_pre_mix, v_w_in, v_ssm_conv_w, v_ssm_conv_b, v_ssm_dt_bias, v_ssm_a_log, v_ssm_d_skip, v_ssm_norm, v_w_ssm_out, v_attn_sinks, v_w_attn_out, v_w_mix_out, v_norm_post_mix, v_norm_pre_ffn, v_w_ffn_up, v_ffn_conv_w, v_ffn_conv_b, v_w_ffn_down, v_norm_post_ffn):
    names = ("meta_tokens", "norm_pre_mix", "w_in", "ssm_conv_w", "ssm_conv_b", "ssm_dt_bias", "ssm_a_log", "ssm_d_skip",
             "ssm_norm", "w_ssm_out", "attn_sinks", "w_attn_out", "w_mix_out", "norm_post_mix", "norm_pre_ffn", "w_ffn_up",
             "ffn_conv_w", "ffn_conv_b", "w_ffn_down", "norm_post_ffn")
    w_loc = dict(zip(names, (meta_tokens, norm_pre_mix, w_in, ssm_conv_w, ssm_conv_b, ssm_dt_bias, ssm_a_log, ssm_d_skip,
                             ssm_norm, w_ssm_out, attn_sinks, w_attn_out, w_mix_out, norm_post_mix, norm_pre_ffn, w_ffn_up,
                             ffn_conv_w, ffn_conv_b, w_ffn_down, norm_post_ffn)))
    m_loc = dict(zip(names, (m_meta_tokens, m_norm_pre_mix, m_w_in, m_ssm_conv_w, m_ssm_conv_b, m_ssm_dt_bias, m_ssm_a_log,
                             m_ssm_d_skip, m_ssm_norm, m_w_ssm_out, m_attn_sinks, m_w_attn_out, m_w_mix_out, m_norm_post_mix,
                             m_norm_pre_ffn, m_w_ffn_up, m_ffn_conv_w, m_ffn_conv_b, m_w_ffn_down, m_norm_post_ffn)))
    v_loc = dict(zip(names, (v_meta_tokens, v_norm_pre_mix, v_w_in, v_ssm_conv_w, v_ssm_conv_b, v_ssm_dt_bias, v_ssm_a_log,
                             v_ssm_d_skip, v_ssm_norm, v_w_ssm_out, v_attn_sinks, v_w_attn_out, v_w_mix_out, v_norm_post_mix,
                             v_norm_pre_ffn, v_w_ffn_up, v_ffn_conv_w, v_ffn_conv_b, v_w_ffn_down, v_norm_post_ffn)))

    def local2d(d, name):
        a = d[name]
        return a if name == "meta_tokens" else a.reshape(a.shape[1:])

    small_shard_pack = jnp.concatenate([_as_rows(local2d(w_loc, name).reshape(-1)) for name, _ in SMALL_SHARDED], axis=0)
    small_shard_pack = jnp.pad(small_shard_pack, ((0, -small_shard_pack.shape[0] % 8), (0, 0)))
    gathered = _all_gather([local2d(w_loc, name).astype(BF16) for name, _, _ in BIG] + [small_shard_pack])
    wt = {}
    for (name, (r, c), axis), g in zip(BIG, gathered[:-1]):
        wt[name] = g.reshape(N_DEV * r, c) if axis == 0 else g.transpose(1, 0, 2).reshape(r, N_DEV * c)
    row = 0
    for name, (r, c) in SMALL_SHARDED:
        blocks = gathered[-1][:, row:row + _rows_of(r * c)].reshape(N_DEV, -1)[:, :r * c].reshape(N_DEV, r, c)
        wt[name] = blocks.transpose(1, 0, 2).reshape(r, N_DEV * c)
        row += _rows_of(r * c)
    for name, size in SMALL_REPLICATED:
        wt[name] = w_loc[name].reshape(1, size)

    loss_part, grad_x, grads = _local_step(x[0], loss_target[0], wt)
    loss = lax.psum(loss_part, AXES)

    small_parts = []
    for name, (r, c) in SMALL_SHARDED:
        small_parts.append(_as_rows(_shard_major(grads[name], (r, c), 1).reshape(N_DEV, r * c)))
    rep_rows = jnp.concatenate([_as_rows(grads[name].reshape(-1)) for name, _ in SMALL_REPLICATED], axis=0)
    small_send = jnp.concatenate([jnp.broadcast_to(rep_rows[None], (N_DEV,) + rep_rows.shape)] + small_parts, axis=1)
    small_send = jnp.pad(small_send, ((0, 0), (0, -small_send.shape[1] % 8), (0, 0)))
    big_send = [_shard_major(grads[name], shape, axis).astype(BF16) for name, shape, axis in BIG]
    received = _exchange_partials(big_send + [small_send])

    grad_w, delta_w, new_m, new_v = {}, {}, {}, {}
    for (name, shape, _), parts in zip(BIG, received[:-1]):
        g, d, nm, nv = _adamw(parts, local2d(w_loc, name), local2d(m_loc, name), local2d(v_loc, name), "adamw_" + name)
        full = (1,) + shape
        grad_w[name], delta_w[name], new_m[name], new_v[name] = g.reshape(full), d.reshape(full), nm.reshape(full), nv.reshape(full)

    def small_pack(d):
        return _pack_small({name: d[name] for name, _ in SMALL_REPLICATED}, {name: local2d(d, name) for name, _ in SMALL_SHARDED})

    outs = _adamw(received[-1], small_pack(w_loc), small_pack(m_loc), small_pack(v_loc), "adamw_small")
    for dst, packed in zip((grad_w, delta_w, new_m, new_v), outs):
        for name, a in _unpack_small(packed).items():
            dst[name] = a.reshape(w_loc[name].shape)

    return (loss, grad_x[None], *[grad_w[n] for n in names], *[delta_w[n] for n in names],
            *[new_m[n] for n in names], *[new_v[n] for n in names])
```

```python
import jax
import jax.numpy as jnp
from jax import lax
from jax.experimental import pallas as pl
from jax.experimental.pallas import tpu as pltpu

F32 = jnp.float32
BF16 = jnp.bfloat16
HIGHEST = lax.Precision.HIGHEST

D_MODEL = 1024
N_META = 16
CHUNK = 128
META_PAD = CHUNK - N_META
D_INNER = 2048
HEAD_P = 64
SSM_HEADS = 32
SSM_GROUPS = 4
HEADS_PER_GROUP = SSM_HEADS // SSM_GROUPS
GROUP_W = HEADS_PER_GROUP * HEAD_P
D_STATE = 128
SSM_CONV = 4
CONV_DIM = D_INNER + 2 * SSM_GROUPS * D_STATE
ATTN_HEADS = 16
KV_HEADS = 4
ATTN_GROUP = ATTN_HEADS // KV_HEADS
DH = 64
KV_W = KV_HEADS * DH
FFN_DIM = 2816
FFN_CONV = 3
EPS = 1e-6
NEG = -1e30
N_DEV = 8
AXES = ("x", "y", "c")

OFF_Z, OFF_GATE, OFF_XBC, OFF_Q, OFF_K, OFF_V, OFF_DT = 0, 2048, 4096, 7168, 8192, 8448, 8704
N_INP = OFF_DT + SSM_GROUPS * 128
CUT_Z, CUT_XBC, CUT_DT, CUT_Q, CUT_K, CUT_V, CUT_G = 0, 2048, 5120, 5152, 6176, 6432, 6688
N_IN = 8736

ADAM_LR, ADAM_B1, ADAM_B2, ADAM_EPS, ADAM_WD, ADAM_STEP = 0.001, 0.9, 0.999, 1e-08, 0.01, 10

VMEM_LIMIT = 56 * 1024 * 1024


def _params(n_grid):
    return pltpu.CompilerParams(dimension_semantics=("arbitrary",) * n_grid, vmem_limit_bytes=VMEM_LIMIT)


def _sds(shape, dtype):
    return jax.ShapeDtypeStruct(shape, dtype)


def _pick(n, prefs):
    for c in prefs:
        if n % c == 0:
            return c
    raise ValueError(f"no tile of {prefs} divides {n}")


def _row(tr, width, cb=0):
    return pl.BlockSpec((tr, width), lambda i: (i, cb))


def _row_rev(tr, width, nt, cb=0):
    return pl.BlockSpec((tr, width), lambda i: (nt - 1 - i, cb))


def _full(shape):
    return pl.BlockSpec(shape, lambda *_: (0,) * len(shape))


def _sigmoid(x):
    return 1.0 / (1.0 + jnp.exp(-x))


def _softplus(x):
    return jnp.maximum(x, 0.0) + jnp.log(1.0 + jnp.exp(-jnp.abs(x)))


def _rms(x):
    return lax.rsqrt(jnp.mean(x * x, axis=-1, keepdims=True) + EPS)


def _rms_bwd(x, r, w, dy):
    xh = x * r
    g = dy * w
    dx = r * (g - xh * jnp.mean(g * xh, axis=-1, keepdims=True))
    return dx, jnp.sum(dy * xh, axis=0, keepdims=True)


def _row_ids(shape, tile_index, tr):
    return tile_index * tr + lax.broadcasted_iota(jnp.int32, shape, 0)


def _shift_down(cur, prev, s):
    if s == 0:
        return cur
    row = lax.broadcasted_iota(jnp.int32, cur.shape, 0)
    return jnp.where(row < s, pltpu.roll(prev, s, 0), pltpu.roll(cur, s, 0))


def _shift_up(cur, nxt, s):
    if s == 0:
        return cur
    n = cur.shape[0]
    row = lax.broadcasted_iota(jnp.int32, cur.shape, 0)
    return jnp.where(row >= n - s, pltpu.roll(nxt, n - s, 0), pltpu.roll(cur, n - s, 0))


def _matmul(a, b, *, ta=False, tb=False, out_dtype=F32, name):
    if ta:
        k_dim, m_dim = a.shape
    else:
        m_dim, k_dim = a.shape
    n_dim = b.shape[0] if tb else b.shape[1]
    tm = _pick(m_dim, (1408, 1024, 768, 512, 384, 256, 128))
    tn = _pick(n_dim, (1024, 1408, 768, 512, 384, 256, 128))
    tk = k_dim if (not ta and k_dim <= 2816) else _pick(k_dim, (1408, 1024, 768, 512, 384, 256, 128))
    nk = k_dim // tk
    dims = (((0 if ta else 1,), (1 if tb else 0,)), ((), ()))

    def body(a_ref, b_ref, o_ref):
        r = lax.dot_general(a_ref[...].astype(BF16), b_ref[...].astype(BF16), dims, preferred_element_type=F32)
        if nk == 1:
            o_ref[...] = r.astype(o_ref.dtype)
        else:
            k = pl.program_id(2)

            @pl.when(k == 0)
            def _():
                o_ref[...] = r

            @pl.when(k > 0)
            def _():
                o_ref[...] += r

    a_spec = pl.BlockSpec((tk, tm), lambda i, j, k: (k, i)) if ta else pl.BlockSpec((tm, tk), lambda i, j, k: (i, k))
    b_spec = pl.BlockSpec((tn, tk), lambda i, j, k: (j, k)) if tb else pl.BlockSpec((tk, tn), lambda i, j, k: (k, j))
    if nk > 1:
        assert out_dtype == F32
    return pl.pallas_call(
        body, grid=(m_dim // tm, n_dim // tn, nk), in_specs=[a_spec, b_spec],
        out_specs=pl.BlockSpec((tm, tn), lambda i, j, k: (i, j)), out_shape=_sds((m_dim, n_dim), out_dtype),
        name=name, compiler_params=_params(3))(a, b)


def _prenorm(h, w):
    t_rows = h.shape[0]
    tr = _pick(t_rows, (384, 128))

    def body(h_ref, w_ref, o_ref):
        x = h_ref[...]
        o_ref[...] = (x * _rms(x) * w_ref[...]).astype(BF16)

    return pl.pallas_call(body, grid=(t_rows // tr,), in_specs=[_row(tr, D_MODEL), _full((1, D_MODEL))],
                          out_specs=_row(tr, D_MODEL), out_shape=_sds((t_rows, D_MODEL), BF16),
                          name="prenorm", compiler_params=_params(1))(h, w)


def _xbc_specs(tr, rev_nt=None):
    cbs = [OFF_XBC // 1024 + j for j in range(CONV_DIM // 1024)]
    if rev_nt is None:
        return [_row(tr, 1024, cb) for cb in cbs]
    return [_row_rev(tr, 1024, rev_nt, cb) for cb in cbs]


def _ssm_conv_fwd(proj, conv_w, conv_b):
    t_rows = proj.shape[0]
    tr = CHUNK

    def body(x0, x1, x2, w_ref, b_ref, xc_ref, xa_ref, prev):
        @pl.when(pl.program_id(0) == 0)
        def _():
            prev[...] = jnp.zeros_like(prev)

        x = jnp.concatenate([x0[...], x1[...], x2[...]], axis=1)
        p = prev[...]
        acc = b_ref[...] + w_ref[SSM_CONV - 1:SSM_CONV, :] * x
        for s in range(1, SSM_CONV):
            acc = acc + w_ref[SSM_CONV - 1 - s:SSM_CONV - s, :] * _shift_down(x, p, s)
        prev[...] = x
        xc_ref[...] = acc
        xa_ref[...] = acc * _sigmoid(acc)

    return pl.pallas_call(
        body, grid=(t_rows // tr,),
        in_specs=_xbc_specs(tr) + [_full((SSM_CONV, CONV_DIM)), _full((1, CONV_DIM))],
        out_specs=[_row(tr, CONV_DIM), _row(tr, CONV_DIM)],
        out_shape=[_sds((t_rows, CONV_DIM), F32), _sds((t_rows, CONV_DIM), F32)],
        scratch_shapes=[pltpu.VMEM((tr, CONV_DIM), F32)],
        name="ssm_conv_fwd", compiler_params=_params(1))(proj, proj, proj, conv_w, conv_b)


def _ssm_post(y, proj, w):
    t_rows = y.shape[0]
    tr = CHUNK

    def body(y_ref, z_ref, w_ref, o_ref):
        z = z_ref[...]
        yz = y_ref[...] * z * _sigmoid(z)
        o_ref[...] = (yz * _rms(yz) * w_ref[...]).astype(BF16)

    return pl.pallas_call(body, grid=(t_rows // tr,),
                          in_specs=[_row(tr, D_INNER), _row(tr, D_INNER, OFF_Z // D_INNER), _full((1, D_INNER))],
                          out_specs=_row(tr, D_INNER), out_shape=_sds((t_rows, D_INNER), BF16),
                          name="ssm_post", compiler_params=_params(1))(y, proj, w)


def _mix_fwd(proj, y_ssm, y_attn):
    t_rows = y_ssm.shape[0]
    tr = _pick(t_rows, (384, 128))

    def body(g_ref, ys_ref, ya_ref, o_ref):
        g = _sigmoid(g_ref[...])
        o_ref[...] = (g[:, :D_MODEL] * ys_ref[...] + g[:, D_MODEL:] * ya_ref[...]).astype(BF16)

    return pl.pallas_call(body, grid=(t_rows // tr,),
                          in_specs=[_row(tr, 2 * D_MODEL, OFF_GATE // (2 * D_MODEL)), _row(tr, D_MODEL), _row(tr, D_MODEL)],
                          out_specs=_row(tr, D_MODEL), out_shape=_sds((t_rows, D_MODEL), BF16),
                          name="mix_fwd", compiler_params=_params(1))(proj, y_ssm, y_attn)


def _postmix(h, mix, w_post, w_pre):
    t_rows = h.shape[0]
    tr = CHUNK

    def body(h_ref, m_ref, wp_ref, wf_ref, h1_ref, hn_ref):
        m = m_ref[...]
        h1 = h_ref[...] + m * _rms(m) * wp_ref[...]
        h1 = jnp.where(_row_ids(h1.shape, pl.program_id(0), tr) >= META_PAD, h1, 0.0)
        h1_ref[...] = h1
        hn_ref[...] = (h1 * _rms(h1) * wf_ref[...]).astype(BF16)

    return pl.pallas_call(body, grid=(t_rows // tr,),
                          in_specs=[_row(tr, D_MODEL), _row(tr, D_MODEL), _full((1, D_MODEL)), _full((1, D_MODEL))],
                          out_specs=[_row(tr, D_MODEL), _row(tr, D_MODEL)],
                          out_shape=[_sds((t_rows, D_MODEL), F32), _sds((t_rows, D_MODEL), BF16)],
                          name="postmix", compiler_params=_params(1))(h, mix, w_post, w_pre)


def _ffn_act(up, conv_w, conv_b):
    t_rows = up.shape[0]
    tr = CHUNK
    width = 2 * FFN_DIM

    def body(up_ref, w_ref, b_ref, u_ref, act_ref, prev):
        @pl.when(pl.program_id(0) == 0)
        def _():
            prev[...] = jnp.zeros_like(prev)

        x = up_ref[...]
        p = prev[...]
        u = b_ref[...] + w_ref[FFN_CONV - 1:FFN_CONV, :] * x
        for s in range(1, FFN_CONV):
            u = u + w_ref[FFN_CONV - 1 - s:FFN_CONV - s, :] * _shift_down(x, p, s)
        prev[...] = x
        u_ref[...] = u
        a = u[:, :FFN_DIM]
        act_ref[...] = (a * _sigmoid(a) * u[:, FFN_DIM:]).astype(BF16)

    return pl.pallas_call(
        body, grid=(t_rows // tr,), in_specs=[_row(tr, width), _full((FFN_CONV, width)), _full((1, width))],
        out_specs=[_row(tr, width), _row(tr, FFN_DIM)],
        out_shape=[_sds((t_rows, width), F32), _sds((t_rows, FFN_DIM), BF16)],
        scratch_shapes=[pltpu.VMEM((tr, width), F32)],
        name="ffn_act", compiler_params=_params(1))(up, conv_w, conv_b)


def _final(h1, f, target, w):
    t_rows = h1.shape[0]
    tr = CHUNK

    def body(h1_ref, f_ref, t_ref, w_ref, df_ref, dy_ref, dw_ref, loss_ref):
        i = pl.program_id(0)

        @pl.when(i == 0)
        def _():
            dw_ref[...] = jnp.zeros_like(dw_ref)
            loss_ref[...] = jnp.zeros_like(loss_ref)

        f_val = f_ref[...]
        r = _rms(f_val)
        wv = w_ref[...]
        h2 = h1_ref[...] + f_val * r * wv
        diff = jnp.where(i >= 1, h2 - t_ref[...], 0.0)
        loss_ref[...] += 0.5 * jnp.sum(diff * diff) * (1.0 / D_MODEL)
        dy = diff * (1.0 / D_MODEL)
        dy_ref[...] = dy
        df, dw = _rms_bwd(f_val, r, wv, dy)
        df_ref[...] = df.astype(BF16)
        dw_ref[...] += dw

    tgt_spec = pl.BlockSpec((tr, D_MODEL), lambda i: (jnp.maximum(i - 1, 0), 0))
    return pl.pallas_call(
        body, grid=(t_rows // tr,),
        in_specs=[_row(tr, D_MODEL), _row(tr, D_MODEL), tgt_spec, _full((1, D_MODEL))],
        out_specs=[_row(tr, D_MODEL), _row(tr, D_MODEL), _full((1, D_MODEL)), _full((1, 128))],
        out_shape=[_sds((t_rows, D_MODEL), BF16), _sds((t_rows, D_MODEL), F32), _sds((1, D_MODEL), F32), _sds((1, 128), F32)],
        name="final", compiler_params=_params(1))(h1, f, target, w)


def _ffn_act_bwd(u, up, dact, conv_w):
    t_rows = u.shape[0]
    tr = CHUNK
    nt = t_rows // tr
    width = 2 * FFN_DIM

    def body(u_ref, up_ref, da_ref, w_ref, dup_ref, dw_ref, db_ref, nxt):
        @pl.when(pl.program_id(0) == 0)
        def _():
            nxt[...] = jnp.zeros_like(nxt)
            dw_ref[...] = jnp.zeros_like(dw_ref)
            db_ref[...] = jnp.zeros_like(db_ref)

        u_val = u_ref[...]
        a, g = u_val[:, :FFN_DIM], u_val[:, FFN_DIM:]
        d = da_ref[...]
        s = _sigmoid(a)
        du = jnp.concatenate([d * g * s * (1.0 + a * (1.0 - s)), d * a * s], axis=1)
        n = nxt[...]
        x = up_ref[...]
        dup = jnp.zeros_like(du)
        for sh in range(FFN_CONV):
            k = FFN_CONV - 1 - sh
            moved = _shift_up(du, n, sh)
            dup = dup + w_ref[k:k + 1, :] * moved
            dw_ref[k:k + 1, :] += jnp.sum(moved * x, axis=0, keepdims=True)
        db_ref[...] += jnp.sum(du, axis=0, keepdims=True)
        nxt[...] = du
        dup_ref[...] = dup.astype(BF16)

    return pl.pallas_call(
        body, grid=(nt,),
        in_specs=[_row_rev(tr, width, nt), _row_rev(tr, width, nt), _row_rev(tr, FFN_DIM, nt), _full((FFN_CONV, width))],
        out_specs=[_row_rev(tr, width, nt), _full((FFN_CONV, width)), _full((1, width))],
        out_shape=[_sds((t_rows, width), BF16), _sds((FFN_CONV, width), F32), _sds((1, width), F32)],
        scratch_shapes=[pltpu.VMEM((tr, width), F32)],
        name="ffn_act_bwd", compiler_params=_params(1))(u, up, dact, conv_w)


def _postmix_bwd(h1, dhn2, dy, mix, w_pre, w_post):
    t_rows = h1.shape[0]
    tr = CHUNK

    def body(h1_ref, dhn_ref, dy_ref, m_ref, wf_ref, wp_ref, dmix_ref, dh_ref, dwf_ref, dwp_ref):
        @pl.when(pl.program_id(0) == 0)
        def _():
            dwf_ref[...] = jnp.zeros_like(dwf_ref)
            dwp_ref[...] = jnp.zeros_like(dwp_ref)

        h1v = h1_ref[...]
        dx, dwf = _rms_bwd(h1v, _rms(h1v), wf_ref[...], dhn_ref[...])
        dwf_ref[...] += dwf
        dh1 = dy_ref[...] + dx
        dh1 = jnp.where(_row_ids(dh1.shape, pl.program_id(0), tr) >= META_PAD, dh1, 0.0)
        dh_ref[...] = dh1
        m = m_ref[...]
        dmix, dwp = _rms_bwd(m, _rms(m), wp_ref[...], dh1)
        dwp_ref[...] += dwp
        dmix_ref[...] = dmix.astype(BF16)

    return pl.pallas_call(
        body, grid=(t_rows // tr,),
        in_specs=[_row(tr, D_MODEL)] * 4 + [_full((1, D_MODEL))] * 2,
        out_specs=[_row(tr, D_MODEL), _row(tr, D_MODEL), _full((1, D_MODEL)), _full((1, D_MODEL))],
        out_shape=[_sds((t_rows, D_MODEL), BF16), _sds((t_rows, D_MODEL), F32), _sds((1, D_MODEL), F32), _sds((1, D_MODEL), F32)],
        name="postmix_bwd", compiler_params=_params(1))(h1, dhn2, dy, mix, w_pre, w_post)


def _mix_bwd(dmixed, proj, y_ssm, y_attn):
    t_rows = dmixed.shape[0]
    tr = _pick(t_rows, (384, 128))

    def body(d_ref, g_ref, ys_ref, ya_ref, dys_ref, dya_ref, dg_ref):
        d = d_ref[...]
        g = _sigmoid(g_ref[...])
        g1, g2 = g[:, :D_MODEL], g[:, D_MODEL:]
        dys_ref[...] = (d * g1).astype(BF16)
        dya_ref[...] = (d * g2).astype(BF16)
        dg_ref[...] = jnp.concatenate([d * ys_ref[...] * g1 * (1.0 - g1), d * ya_ref[...] * g2 * (1.0 - g2)],
                                      axis=1).astype(BF16)

    return pl.pallas_call(
        body, grid=(t_rows // tr,),
        in_specs=[_row(tr, D_MODEL), _row(tr, 2 * D_MODEL, OFF_GATE // (2 * D_MODEL)), _row(tr, D_MODEL), _row(tr, D_MODEL)],
        out_specs=[_row(tr, D_MODEL), _row(tr, D_MODEL), _row(tr, 2 * D_MODEL)],
        out_shape=[_sds((t_rows, D_MODEL), BF16), _sds((t_rows, D_MODEL), BF16), _sds((t_rows, 2 * D_MODEL), BF16)],
        name="mix_bwd", compiler_params=_params(1))(dmixed, proj, y_ssm, y_attn)


def _ssm_post_bwd(y, proj, dyn, w):
    t_rows = y.shape[0]
    tr = CHUNK

    def body(y_ref, z_ref, d_ref, w_ref, dy_ref, dz_ref, dw_ref):
        @pl.when(pl.program_id(0) == 0)
        def _():
            dw_ref[...] = jnp.zeros_like(dw_ref)

        yv, z = y_ref[...], z_ref[...]
        sz = _sigmoid(z)
        silu = z * sz
        yz = yv * silu
        dyz, dw = _rms_bwd(yz, _rms(yz), w_ref[...], d_ref[...])
        dw_ref[...] += dw
        dy_ref[...] = dyz * silu
        dz_ref[...] = (dyz * yv * sz * (1.0 + z * (1.0 - sz))).astype(BF16)

    return pl.pallas_call(
        body, grid=(t_rows // tr,),
        in_specs=[_row(tr, D_INNER), _row(tr, D_INNER, OFF_Z // D_INNER), _row(tr, D_INNER), _full((1, D_INNER))],
        out_specs=[_row(tr, D_INNER), _row(tr, D_INNER), _full((1, D_INNER))],
        out_shape=[_sds((t_rows, D_INNER), F32), _sds((t_rows, D_INNER), BF16), _sds((1, D_INNER), F32)],
        name="ssm_post_bwd", compiler_params=_params(1))(y, proj, dyn, w)


def _ssm_conv_bwd(xc, proj, dxs, dbm, dcm, conv_w):
    t_rows = xc.shape[0]
    tr = CHUNK
    nt = t_rows // tr
    bc_w = SSM_GROUPS * D_STATE

    def body(xc_ref, x0, x1, x2, dxs_ref, db_ref, dc_ref, w_ref, dx_ref, dw_ref, dbias_ref, nxt):
        @pl.when(pl.program_id(0) == 0)
        def _():
            nxt[...] = jnp.zeros_like(nxt)
            dw_ref[...] = jnp.zeros_like(dw_ref)
            dbias_ref[...] = jnp.zeros_like(dbias_ref)

        c = xc_ref[...]
        s = _sigmoid(c)
        dact = jnp.concatenate([dxs_ref[...], db_ref[...], dc_ref[...]], axis=1)
        dpre = dact * s * (1.0 + c * (1.0 - s))
        x = jnp.concatenate([x0[...], x1[...], x2[...]], axis=1)
        n = nxt[...]
        dx = jnp.zeros_like(dpre)
        for sh in range(SSM_CONV):
            k = SSM_CONV - 1 - sh
            moved = _shift_up(dpre, n, sh)
            dx = dx + w_ref[k:k + 1, :] * moved
            dw_ref[k:k + 1, :] += jnp.sum(moved * x, axis=0, keepdims=True)
        dbias_ref[...] += jnp.sum(dpre, axis=0, keepdims=True)
        nxt[...] = dpre
        dx_ref[...] = dx.astype(BF16)

    return pl.pallas_call(
        body, grid=(nt,),
        in_specs=[_row_rev(tr, CONV_DIM, nt)] + _xbc_specs(tr, nt)
        + [_row_rev(tr, D_INNER, nt), _row_rev(tr, bc_w, nt), _row_rev(tr, bc_w, nt), _full((SSM_CONV, CONV_DIM))],
        out_specs=[_row_rev(tr, CONV_DIM, nt), _full((SSM_CONV, CONV_DIM)), _full((1, CONV_DIM))],
        out_shape=[_sds((t_rows, CONV_DIM), BF16), _sds((SSM_CONV, CONV_DIM), F32), _sds((1, CONV_DIM), F32)],
        scratch_shapes=[pltpu.VMEM((tr, CONV_DIM), F32)],
        name="ssm_conv_bwd", compiler_params=_params(1))(xc, proj, proj, proj, dxs, dbm, dcm, conv_w)


def _prenorm_bwd(h, dhn, dh, w):
    t_rows = h.shape[0]
    tr = CHUNK

    def body(h_ref, d_ref, r_ref, w_ref, o_ref, dw_ref):
        @pl.when(pl.program_id(0) == 0)
        def _():
            dw_ref[...] = jnp.zeros_like(dw_ref)

        x = h_ref[...]
        dx, dw = _rms_bwd(x, _rms(x), w_ref[...], d_ref[...])
        dw_ref[...] += dw
        o_ref[...] = r_ref[...] + dx

    return pl.pallas_call(
        body, grid=(t_rows // tr,), in_specs=[_row(tr, D_MODEL)] * 3 + [_full((1, D_MODEL))],
        out_specs=[_row(tr, D_MODEL), _full((1, D_MODEL))],
        out_shape=[_sds((t_rows, D_MODEL), F32), _sds((1, D_MODEL), F32)],
        name="prenorm_bwd", compiler_params=_params(1))(h, dhn, dh, w)


def _ssd_common(dtr_ref, dtb_ref, alog_ref, chunk_index):
    rows = lax.broadcasted_iota(jnp.int32, (CHUNK, CHUNK), 0)
    cols = lax.broadcasted_iota(jnp.int32, (CHUNK, CHUNK), 1)
    low = rows >= cols
    tril = low.astype(F32)
    raw = dtr_ref[...] + dtb_ref[0]
    live = _row_ids(raw.shape, chunk_index, CHUNK) >= META_PAD
    dt = jnp.where(live, _softplus(raw), 0.0)
    a_head = -jnp.exp(alog_ref[0])
    a = dt * a_head
    cs = jnp.dot(tril, a, precision=HIGHEST, preferred_element_type=F32)
    expand = (lax.broadcasted_iota(jnp.int32, (CHUNK, GROUP_W), 1) // HEAD_P
              == lax.broadcasted_iota(jnp.int32, (CHUNK, GROUP_W), 0)).astype(F32)
    dtx = jnp.dot(dt, expand, precision=HIGHEST, preferred_element_type=F32)
    csx = jnp.dot(cs, expand, precision=HIGHEST, preferred_element_type=F32)
    fold = (lax.broadcasted_iota(jnp.int32, (GROUP_W, CHUNK), 0) // HEAD_P
            == lax.broadcasted_iota(jnp.int32, (GROUP_W, CHUNK), 1)).astype(F32)
    return dict(low=low, triu=(rows <= cols).astype(F32), raw=raw, live=live, dt=dt, a_head=a_head, cs=cs, cs_t=cs.T,
                fold=fold, dtx=dtx, csx=csx)


def _decay_matrix(cm, j):
    diff = cm["cs"][:, j:j + 1] - cm["cs_t"][j:j + 1, :]
    return jnp.where(cm["low"], jnp.exp(jnp.where(cm["low"], diff, 0.0)), 0.0)


def _dot(a, b, dims):
    return lax.dot_general(a.astype(BF16), b.astype(BF16), (dims, ((), ())), preferred_element_type=F32)


def _ssd_specs(nt, rev):
    def idx(c):
        return nt - 1 - c if rev else c
    xs = pl.BlockSpec((CHUNK, GROUP_W), lambda g, c: (idx(c), g))
    bm = pl.BlockSpec((CHUNK, D_STATE), lambda g, c: (idx(c), D_INNER // D_STATE + g))
    cm = pl.BlockSpec((CHUNK, D_STATE), lambda g, c: (idx(c), D_INNER // D_STATE + SSM_GROUPS + g))
    dtr = pl.BlockSpec((CHUNK, 128), lambda g, c: (idx(c), OFF_DT // 128 + g))
    par = pl.BlockSpec((1, 1, 128), lambda g, c: (g, 0, 0))
    par_x = pl.BlockSpec((1, 1, GROUP_W), lambda g, c: (g, 0, 0))
    return xs, bm, cm, dtr, par, par_x, idx


def _ssd_fwd(xact, proj, dtb, alog, dskip_x):
    t_rows = xact.shape[0]
    nt = t_rows // CHUNK
    xs_spec, b_spec, c_spec, dtr_spec, par, par_x, _ = _ssd_specs(nt, False)

    def body(xs_ref, b_ref, c_ref, dtr_ref, dtb_ref, alog_ref, dsk_ref, y_ref, hst_ref, state):
        c = pl.program_id(1)

        @pl.when(c == 0)
        def _():
            state[...] = jnp.zeros_like(state)

        cm = _ssd_common(dtr_ref, dtb_ref, alog_ref, c)
        xs, bm, cmat = xs_ref[...], b_ref[...], c_ref[...]
        x_dt = xs * cm["dtx"]
        h_in = state[...]
        hst_ref[0, 0] = h_in
        y_ref[...] = _dot(cmat, h_in, ((1,), (0,))) * jnp.exp(cm["csx"]) + xs * dsk_ref[0]
        cb = _dot(cmat, bm, ((1,), (1,)))
        for j in range(HEADS_PER_GROUP):
            sl = slice(j * HEAD_P, (j + 1) * HEAD_P)
            y_ref[:, sl] += _dot(cb * _decay_matrix(cm, j), x_dt[:, sl], ((1,), (0,)))
        cs_last = cm["csx"][CHUNK - 1:CHUNK, :]
        state[...] = h_in * jnp.exp(cs_last) + _dot(bm, x_dt * jnp.exp(cs_last - cm["csx"]), ((0,), (0,)))

    return pl.pallas_call(
        body, grid=(SSM_GROUPS, nt),
        in_specs=[xs_spec, b_spec, c_spec, dtr_spec, par, par, par_x],
        out_specs=[xs_spec, pl.BlockSpec((1, 1, D_STATE, GROUP_W), lambda g, c: (c, g, 0, 0))],
        out_shape=[_sds((t_rows, D_INNER), F32), _sds((nt, SSM_GROUPS, D_STATE, GROUP_W), F32)],
        scratch_shapes=[pltpu.VMEM((D_STATE, GROUP_W), F32)],
        name="ssd_fwd", compiler_params=_params(2))(xact, xact, xact, proj, dtb, alog, dskip_x)


def _ssd_bwd(xact, proj, dtb, alog, dskip_x, dy, hst):
    t_rows = xact.shape[0]
    nt = t_rows // CHUNK
    xs_spec, b_spec, c_spec, dtr_spec, par, par_x, idx = _ssd_specs(nt, True)
    h_spec = pl.BlockSpec((1, 1, D_STATE, GROUP_W), lambda g, c: (idx(c), g, 0, 0))
    hn_spec = pl.BlockSpec((1, 1, D_STATE, GROUP_W), lambda g, c: (jnp.minimum(idx(c) + 1, nt - 1), g, 0, 0))
    bc_out = pl.BlockSpec((CHUNK, D_STATE), lambda g, c: (idx(c), g))

    def body(xs_ref, b_ref, c_ref, dtr_ref, dtb_ref, alog_ref, dsk_ref, dy_ref, h_ref, hn_ref,
             dxs_ref, db_ref, dc_ref, ddt_ref, dalog_ref, ddtb_ref, dd_ref, dstate, dx_buf):
        step = pl.program_id(1)

        @pl.when(step == 0)
        def _():
            dstate[...] = jnp.zeros_like(dstate)
            dalog_ref[...] = jnp.zeros_like(dalog_ref)
            ddtb_ref[...] = jnp.zeros_like(ddtb_ref)
            dd_ref[...] = jnp.zeros_like(dd_ref)

        cm = _ssd_common(dtr_ref, dtb_ref, alog_ref, idx(step))
        xs, bm, cmat = xs_ref[...], b_ref[...], c_ref[...]
        dsk = dsk_ref[0]
        x_dt = xs * cm["dtx"]
        h_in, h_next = h_ref[0, 0], hn_ref[0, 0]
        dyv = dy_ref[...]
        dh = dstate[...]
        grow = jnp.exp(cm["csx"])
        cs_last = cm["csx"][CHUNK - 1:CHUNK, :]
        fade = jnp.exp(cs_last - cm["csx"])
        dy_grow = dyv * grow
        x_fade = x_dt * fade
        cb = _dot(cmat, bm, ((1,), (1,)))
        ml = jnp.zeros((CHUNK, CHUNK), F32)
        row_id = lax.broadcasted_iota(jnp.int32, (CHUNK, CHUNK), 0)
        col_id = lax.broadcasted_iota(jnp.int32, (CHUNK, CHUNK), 1)
        w_rows = jnp.zeros((CHUNK, CHUNK), F32)
        w_cols = jnp.zeros((CHUNK, CHUNK), F32)
        for j in range(HEADS_PER_GROUP):
            sl = slice(j * HEAD_P, (j + 1) * HEAD_P)
            lm = _decay_matrix(cm, j)
            mlj = _dot(dyv[:, sl], x_dt[:, sl], ((1,), (1,))) * lm
            ml = ml + mlj
            wm = mlj * cb
            w_rows = jnp.where(col_id == j, jnp.sum(wm, axis=1, keepdims=True), w_rows)
            w_cols = jnp.where(row_id == j, jnp.sum(wm, axis=0, keepdims=True), w_cols)
            dx_buf[:, sl] = _dot(cb * lm, dyv[:, sl], ((0,), (0,)))
        dx_off = fade * _dot(bm, dh, ((1,), (0,)))
        dx = dx_buf[...] + dx_off
        dc_ref[...] = _dot(ml, bm, ((1,), (0,))) + _dot(dy_grow, h_in, ((1,), (1,)))
        db_ref[...] = _dot(ml, cmat, ((0,), (0,))) + _dot(x_fade, dh, ((1,), (1,)))
        fold = cm["fold"]
        y_off = _dot(cmat, h_in, ((1,), (0,))) * grow
        dcs = (w_rows - w_cols.T) + jnp.dot(dyv * y_off - x_dt * dx_off, fold, precision=HIGHEST, preferred_element_type=F32)
        tail = jnp.broadcast_to(jnp.sum(dh * h_next, axis=0, keepdims=True), (8, GROUP_W))
        tail = jnp.dot(tail, fold, precision=HIGHEST, preferred_element_type=F32)[0:1, :]
        last_row = lax.broadcasted_iota(jnp.int32, (CHUNK, 128), 0) == CHUNK - 1
        dcs = dcs + jnp.where(last_row, tail, 0.0)
        da = jnp.dot(cm["triu"], dcs, precision=HIGHEST, preferred_element_type=F32)
        ddt = da * cm["a_head"] + jnp.dot(dx * xs, fold, precision=HIGHEST, preferred_element_type=F32)
        ddt_raw = jnp.where(cm["live"], ddt * _sigmoid(cm["raw"]), 0.0)
        ddt_ref[...] = ddt_raw.astype(BF16)
        ddtb_ref[0] += jnp.sum(ddt_raw, axis=0, keepdims=True)
        dalog_ref[0] += jnp.sum(da * cm["dt"], axis=0, keepdims=True) * cm["a_head"]
        dd_ref[0] += jnp.sum(dyv * xs, axis=0, keepdims=True)
        dxs_ref[...] = dx * cm["dtx"] + dyv * dsk
        dstate[...] = dh * jnp.exp(cs_last) + _dot(cmat, dy_grow, ((0,), (0,)))

    return pl.pallas_call(
        body, grid=(SSM_GROUPS, nt),
        in_specs=[xs_spec, b_spec, c_spec, dtr_spec, par, par, par_x, xs_spec, h_spec, hn_spec],
        out_specs=[xs_spec, bc_out, bc_out, bc_out, par, par, par_x],
        out_shape=[_sds((t_rows, D_INNER), F32), _sds((t_rows, SSM_GROUPS * D_STATE), F32),
                   _sds((t_rows, SSM_GROUPS * D_STATE), F32), _sds((t_rows, SSM_GROUPS * 128), BF16),
                   _sds((SSM_GROUPS, 1, 128), F32), _sds((SSM_GROUPS, 1, 128), F32), _sds((SSM_GROUPS, 1, GROUP_W), F32)],
        scratch_shapes=[pltpu.VMEM((D_STATE, GROUP_W), F32), pltpu.VMEM((CHUNK, GROUP_W), F32)],
        name="ssd_bwd", compiler_params=_params(2))(xact, xact, xact, proj, dtb, alog, dskip_x, dy, hst, hst)


def _swa_probs(q_ref, kp_ref, kc_ref, km_ref, sink_ref, kv_head, n):
    rows_q = ATTN_GROUP * CHUNK
    q = q_ref[...]
    qs = jnp.concatenate([q[:, g * DH:(g + 1) * DH] for g in range(ATTN_GROUP)], axis=0)
    kcat = jnp.concatenate([kp_ref[0], kc_ref[0]], axis=0)
    kmeta = km_ref[0][META_PAD:, :]
    scale = DH ** -0.5
    head = (kv_head * ATTN_GROUP + lax.broadcasted_iota(jnp.int32, (rows_q, 1), 0) // CHUNK + 1).astype(F32)
    slope = jnp.exp(head * (-8.0 / ATTN_HEADS * 0.6931471805599453))
    q_in = lax.broadcasted_iota(jnp.int32, (rows_q, 2 * CHUNK), 0) % CHUNK
    s_in = lax.broadcasted_iota(jnp.int32, (rows_q, 2 * CHUNK), 1)
    dist = q_in - s_in + CHUNK
    ok = (dist >= 0) & (dist < CHUNK) & (s_in + n * CHUNK >= 2 * CHUNK)
    s_band = _dot(qs, kcat, ((1,), (1,))) * scale - slope * dist.astype(F32)
    s_band = jnp.where(ok, s_band, NEG)
    q_pos = lax.broadcasted_iota(jnp.int32, (rows_q, N_META), 0) % CHUNK + n * CHUNK - META_PAD
    ok_m = lax.broadcasted_iota(jnp.int32, (rows_q, N_META), 1) <= q_pos
    s_meta = jnp.where(ok_m, _dot(qs, kmeta, ((1,), (1,))) * scale, NEG)
    sink = sink_ref[0]
    m = jnp.maximum(jnp.maximum(jnp.max(s_band, axis=1, keepdims=True), jnp.max(s_meta, axis=1, keepdims=True)), sink)
    p_band, p_meta, p_sink = jnp.exp(s_band - m), jnp.exp(s_meta - m), jnp.exp(sink - m)
    inv = 1.0 / (jnp.sum(p_band, axis=1, keepdims=True) + jnp.sum(p_meta, axis=1, keepdims=True) + p_sink)
    return qs, kcat, kmeta, p_band * inv, p_meta * inv, p_sink * inv


def _swa_specs(nt, rev):
    def idx(n):
        return nt - 1 - n if rev else n
    q = pl.BlockSpec((CHUNK, ATTN_GROUP * DH), lambda k, n: (idx(n), k))
    q_proj = pl.BlockSpec((CHUNK, ATTN_GROUP * DH), lambda k, n: (idx(n), OFF_Q // (ATTN_GROUP * DH) + k))
    cur = pl.BlockSpec((1, CHUNK, DH), lambda k, n: (k, idx(n), 0))
    prev = pl.BlockSpec((1, CHUNK, DH), lambda k, n: (k, jnp.maximum(idx(n) - 1, 0), 0))
    first = pl.BlockSpec((1, CHUNK, DH), lambda k, n: (k, 0, 0))
    sink = pl.BlockSpec((1, ATTN_GROUP * CHUNK, 1), lambda k, n: (k, 0, 0))
    return q, q_proj, cur, prev, first, sink, idx


def _swa_fwd(proj, k_heads, v_heads, sink_rows):
    t_rows = proj.shape[0]
    nt = t_rows // CHUNK
    o_spec, q_spec, cur, prev, first, sink_spec, _ = _swa_specs(nt, False)

    def body(q_ref, kp_ref, kc_ref, km_ref, vp_ref, vc_ref, vm_ref, sink_ref, o_ref):
        _, _, _, p_band, p_meta, _ = _swa_probs(q_ref, kp_ref, kc_ref, km_ref, sink_ref, pl.program_id(0), pl.program_id(1))
        vcat = jnp.concatenate([vp_ref[0], vc_ref[0]], axis=0)
        out = _dot(p_band, vcat, ((1,), (0,))) + _dot(p_meta, vm_ref[0][META_PAD:, :], ((1,), (0,)))
        for g in range(ATTN_GROUP):
            o_ref[:, g * DH:(g + 1) * DH] = out[g * CHUNK:(g + 1) * CHUNK, :]

    return pl.pallas_call(
        body, grid=(KV_HEADS, nt), in_specs=[q_spec, prev, cur, first, prev, cur, first, sink_spec],
        out_specs=o_spec, out_shape=_sds((t_rows, ATTN_HEADS * DH), F32),
        name="swa_fwd", compiler_params=_params(2))(proj, k_heads, k_heads, k_heads, v_heads, v_heads, v_heads, sink_rows)


def _swa_bwd(proj, k_heads, v_heads, sink_rows, out, dout):
    t_rows = proj.shape[0]
    nt = t_rows // CHUNK
    o_spec, q_spec, cur, prev, first, sink_spec, idx = _swa_specs(nt, True)
    scale = DH ** -0.5

    def body(q_ref, kp_ref, kc_ref, km_ref, vp_ref, vc_ref, vm_ref, sink_ref, o_ref, do_ref,
             dq_ref, dk_ref, dv_ref, dsink_ref, carry_k, carry_v, meta_k, meta_v):
        step = pl.program_id(1)
        n = idx(step)

        @pl.when(step == 0)
        def _():
            carry_k[...] = jnp.zeros_like(carry_k)
            carry_v[...] = jnp.zeros_like(carry_v)
            meta_k[...] = jnp.zeros_like(meta_k)
            meta_v[...] = jnp.zeros_like(meta_v)
            dsink_ref[...] = jnp.zeros_like(dsink_ref)

        qs, kcat, kmeta, p_band, p_meta, p_sink = _swa_probs(q_ref, kp_ref, kc_ref, km_ref, sink_ref, pl.program_id(0), n)
        vcat = jnp.concatenate([vp_ref[0], vc_ref[0]], axis=0)
        vmeta = vm_ref[0][META_PAD:, :]
        o, do = o_ref[...], do_ref[...]
        os_ = jnp.concatenate([o[:, g * DH:(g + 1) * DH] for g in range(ATTN_GROUP)], axis=0)
        dos = jnp.concatenate([do[:, g * DH:(g + 1) * DH] for g in range(ATTN_GROUP)], axis=0)
        delta = jnp.sum(dos * os_, axis=1, keepdims=True)
        ds_band = p_band * (_dot(dos, vcat, ((1,), (1,))) - delta)
        ds_meta = p_meta * (_dot(dos, vmeta, ((1,), (1,))) - delta)
        ds_sink = -p_sink * delta
        dqs = (_dot(ds_band, kcat, ((1,), (0,))) + _dot(ds_meta, kmeta, ((1,), (0,)))) * scale
        for g in range(ATTN_GROUP):
            dq_ref[:, g * DH:(g + 1) * DH] = dqs[g * CHUNK:(g + 1) * CHUNK, :]
            dsink_ref[0, g:g + 1, :] += jnp.sum(ds_sink[g * CHUNK:(g + 1) * CHUNK, :])
        dkcat = _dot(ds_band, qs, ((0,), (0,))) * scale
        dvcat = _dot(p_band, dos, ((0,), (0,)))
        meta_k[...] += _dot(ds_meta, qs, ((0,), (0,))) * scale
        meta_v[...] += _dot(p_meta, dos, ((0,), (0,)))
        dk_ref[0] = dkcat[CHUNK:, :] + carry_k[...]
        dv_ref[0] = dvcat[CHUNK:, :] + carry_v[...]
        carry_k[...] = dkcat[:CHUNK, :]
        carry_v[...] = dvcat[:CHUNK, :]

        @pl.when(n == 0)
        def _():
            dk_ref[0, META_PAD:, :] += meta_k[...]
            dv_ref[0, META_PAD:, :] += meta_v[...]

    return pl.pallas_call(
        body, grid=(KV_HEADS, nt),
        in_specs=[q_spec, prev, cur, first, prev, cur, first, sink_spec, o_spec, o_spec],
        out_specs=[o_spec, cur, cur, pl.BlockSpec((1, 8, 128), lambda k, n: (k, 0, 0))],
        out_shape=[_sds((t_rows, ATTN_HEADS * DH), F32), _sds((KV_HEADS, t_rows, DH), F32),
                   _sds((KV_HEADS, t_rows, DH), F32), _sds((KV_HEADS, 8, 128), F32)],
        scratch_shapes=[pltpu.VMEM((CHUNK, DH), F32), pltpu.VMEM((CHUNK, DH), F32),
                        pltpu.VMEM((N_META, DH), F32), pltpu.VMEM((N_META, DH), F32)],
        name="swa_bwd", compiler_params=_params(2))(proj, k_heads, k_heads, k_heads, v_heads, v_heads, v_heads,
                                                    sink_rows, out, dout)


def _pack_w_in(w_in):
    w_dt = w_in[:, CUT_DT:CUT_Q].reshape(D_MODEL, SSM_GROUPS, HEADS_PER_GROUP)
    w_dt = jnp.pad(w_dt, ((0, 0), (0, 0), (0, 128 - HEADS_PER_GROUP))).reshape(D_MODEL, SSM_GROUPS * 128)
    return jnp.concatenate([w_in[:, CUT_Z:CUT_XBC], w_in[:, CUT_G:], w_in[:, CUT_XBC:CUT_DT], w_in[:, CUT_Q:CUT_K],
                            w_in[:, CUT_K:CUT_V], w_in[:, CUT_V:CUT_G], w_dt], axis=1)


def _unpack_w_in(wp):
    w_dt = wp[:, OFF_DT:].reshape(D_MODEL, SSM_GROUPS, 128)[:, :, :HEADS_PER_GROUP].reshape(D_MODEL, SSM_HEADS)
    return jnp.concatenate([wp[:, OFF_Z:OFF_GATE], wp[:, OFF_XBC:OFF_Q], w_dt, wp[:, OFF_Q:OFF_K], wp[:, OFF_K:OFF_V],
                            wp[:, OFF_V:OFF_DT], wp[:, OFF_GATE:OFF_XBC]], axis=1)


def _group_rows(v, width):
    return jnp.pad(v.reshape(SSM_GROUPS, 1, HEADS_PER_GROUP), ((0, 0), (0, 0), (0, width - HEADS_PER_GROUP)))


def _to_heads(a):
    return a.reshape(a.shape[0], KV_HEADS, DH).transpose(1, 0, 2)


def _from_heads(a):
    return a.transpose(1, 0, 2).reshape(a.shape[1], KV_W)


def _tie(token, value):
    if token is None:
        return value
    return lax.optimization_barrier((value, token))[0]


def _local_step(x, target, wt, late_weights=None, on_grad=None):
    seq = x.shape[0]
    grads = {}

    def emit(name, g):
        grads[name] = g
        return None if on_grad is None else on_grad(name, g)
    h = jnp.concatenate([jnp.zeros((META_PAD, D_MODEL), F32), wt["meta_tokens"], x], axis=0)
    wp = _pack_w_in(wt["w_in"])
    dtb = _group_rows(wt["ssm_dt_bias"].reshape(-1), 128)
    alog = _group_rows(wt["ssm_a_log"].reshape(-1), 128)
    dskip_x = jnp.repeat(wt["ssm_d_skip"].reshape(-1), HEAD_P).reshape(SSM_GROUPS, 1, GROUP_W)
    sink_rows = jnp.repeat(wt["attn_sinks"].reshape(KV_HEADS, ATTN_GROUP), CHUNK, axis=1).reshape(KV_HEADS, ATTN_GROUP * CHUNK, 1)

    hn = _prenorm(h, wt["norm_pre_mix"])
    proj = _matmul(hn, wp, name="in_proj")
    xc, xact = _ssm_conv_fwd(proj, wt["ssm_conv_w"], wt["ssm_conv_b"])
    y, hst = _ssd_fwd(xact, proj, dtb, alog, dskip_x)
    yn = _ssm_post(y, proj, wt["ssm_norm"])
    if late_weights is not None:
        wt = {**wt, **late_weights(yn)}
    y_ssm = _matmul(yn, wt["w_ssm_out"], name="ssm_out")
    k_heads = _to_heads(proj[:, OFF_K:OFF_V])
    v_heads = _to_heads(proj[:, OFF_V:OFF_DT])
    attn = _swa_fwd(proj, k_heads, v_heads, sink_rows)
    y_attn = _matmul(attn, wt["w_attn_out"], name="attn_out")
    mixed = _mix_fwd(proj, y_ssm, y_attn)
    mix = _matmul(mixed, wt["w_mix_out"], name="mix_out")
    h1, hn2 = _postmix(h, mix, wt["norm_post_mix"], wt["norm_pre_ffn"])
    up = _matmul(hn2, wt["w_ffn_up"], name="ffn_up")
    u, act = _ffn_act(up, wt["ffn_conv_w"], wt["ffn_conv_b"])
    f = _matmul(act, wt["w_ffn_down"], name="ffn_down")
    df, dy, g_norm_post_ffn, loss_row = _final(h1, f, target, wt["norm_post_ffn"])

    grads["norm_post_ffn"] = g_norm_post_ffn
    dact = _matmul(df, wt["w_ffn_down"], tb=True, name="d_act")
    dact = _tie(emit("w_ffn_down", _matmul(act, df, ta=True, name="dw_ffn_down")), dact)
    dup, grads["ffn_conv_w"], grads["ffn_conv_b"] = _ffn_act_bwd(u, up, dact, wt["ffn_conv_w"])
    dhn2 = _matmul(dup, wt["w_ffn_up"], tb=True, name="d_hn2")
    dhn2 = _tie(emit("w_ffn_up", _matmul(hn2, dup, ta=True, name="dw_ffn_up")), dhn2)
    dmix, dh, grads["norm_pre_ffn"], grads["norm_post_mix"] = _postmix_bwd(h1, dhn2, dy, mix, wt["norm_pre_ffn"], wt["norm_post_mix"])
    dmixed = _matmul(dmix, wt["w_mix_out"], tb=True, name="d_mixed")
    dmixed = _tie(emit("w_mix_out", _matmul(mixed, dmix, ta=True, name="dw_mix_out")), dmixed)
    dy_ssm, dy_attn, dglog = _mix_bwd(dmixed, proj, y_ssm, y_attn)
    dyn = _matmul(dy_ssm, wt["w_ssm_out"], tb=True, name="d_yn")
    dyn = _tie(emit("w_ssm_out", _matmul(yn, dy_ssm, ta=True, name="dw_ssm_out")), dyn)
    dattn = _matmul(dy_attn, wt["w_attn_out"], tb=True, name="d_attn")
    dattn = _tie(emit("w_attn_out", _matmul(attn, dy_attn, ta=True, name="dw_attn_out")), dattn)
    dy_ssd, dz, grads["ssm_norm"] = _ssm_post_bwd(y, proj, dyn, wt["ssm_norm"])
    dxs, dbm, dcm, ddt, dalog, ddtb, dd_x = _ssd_bwd(xact, proj, dtb, alog, dskip_x, dy_ssd, hst)
    grads["ssm_a_log"] = dalog[:, 0, :HEADS_PER_GROUP].reshape(1, SSM_HEADS)
    grads["ssm_dt_bias"] = ddtb[:, 0, :HEADS_PER_GROUP].reshape(1, SSM_HEADS)
    grads["ssm_d_skip"] = dd_x.reshape(SSM_HEADS, HEAD_P).sum(axis=1).reshape(1, SSM_HEADS)
    dxbc, grads["ssm_conv_w"], grads["ssm_conv_b"] = _ssm_conv_bwd(xc, proj, dxs, dbm, dcm, wt["ssm_conv_w"])
    dq, dk_heads, dv_heads, dsink = _swa_bwd(proj, k_heads, v_heads, sink_rows, attn, dattn)
    grads["attn_sinks"] = dsink[:, :ATTN_GROUP, 0].reshape(1, ATTN_HEADS)
    dproj = jnp.concatenate([dz, dglog, dxbc, dq.astype(BF16), _from_heads(dk_heads).astype(BF16),
                             _from_heads(dv_heads).astype(BF16), ddt], axis=1)
    dproj = _tie(emit("w_in", _unpack_w_in(_matmul(hn, dproj, ta=True, name="dw_in"))), dproj)
    dhn = _matmul(dproj, wp, tb=True, name="d_hn")
    dh_all, grads["norm_pre_mix"] = _prenorm_bwd(h, dhn, dh, wt["norm_pre_mix"])
    grads["meta_tokens"] = dh_all[META_PAD:CHUNK]
    return loss_row[0, 0], dh_all[CHUNK:CHUNK + seq], grads


def _all_gather(shards):
    n = len(shards)

    def body(*refs):
        ins, outs = refs[:n], refs[n:2 * n]
        send_sems, recv_sems, local_sems = refs[2 * n:]
        x, y, c = lax.axis_index("x"), lax.axis_index("y"), lax.axis_index("c")
        me, sibling = (x, y, c), (x, y, 1 - c)
        chips = [(1 - x, y), (x, 1 - y), (1 - x, 1 - y)]

        def slot(a, dev):
            return outs[a].at[4 * dev[0] + 2 * dev[1] + dev[2]]

        def copy(k, a, block, to, src=None):
            return pltpu.make_async_remote_copy(
                src_ref=slot(a, block) if src is None else src, dst_ref=slot(a, block),
                send_sem=send_sems.at[k, a], recv_sem=recv_sems.at[k, a],
                device_id=to, device_id_type=pl.DeviceIdType.MESH)

        mine = [pltpu.make_async_copy(ins[a], slot(a, me), local_sems.at[a]) for a in range(n)]
        for cp in mine:
            cp.start()
        first = [copy(0, a, me, sibling, src=ins[a]) for a in range(n)]
        for j, chip in enumerate(chips):
            first += [copy(1 + j, a, me, (*chip, c), src=ins[a]) for a in range(n)]
        for cp in first:
            cp.start()
        passed = []
        for j, chip in enumerate(chips):
            for a in range(n):
                copy(1 + j, a, (*chip, c), me).wait_recv()
                fwd = copy(4 + j, a, (*chip, c), sibling)
                fwd.start()
                passed.append(fwd)
        for a in range(n):
            copy(0, a, sibling, me).wait_recv()
        for j, chip in enumerate(chips):
            for a in range(n):
                copy(4 + j, a, (*chip, 1 - c), me).wait_recv()
        for cp in first + passed:
            cp.wait_send()
        for cp in mine:
            cp.wait()

    hbm = pl.BlockSpec(memory_space=pl.ANY)
    return pl.pallas_call(
        body, in_specs=[hbm] * n, out_specs=[hbm] * n,
        out_shape=[_sds((N_DEV,) + s.shape, s.dtype) for s in shards],
        scratch_shapes=[pltpu.SemaphoreType.DMA((7, n)), pltpu.SemaphoreType.DMA((7, n)), pltpu.SemaphoreType.DMA((n,))],
        name="gather_weights")(*shards)


def _exchange_partials(parts):
    n = len(parts)

    def body(*refs):
        ins, outs = refs[:n], refs[n:2 * n]
        send_sems, recv_sems, local_sems = refs[2 * n:]
        x, y, c = lax.axis_index("x"), lax.axis_index("y"), lax.axis_index("c")
        my_id = 4 * x + 2 * y + c

        def peer(k):
            return (x ^ ((k >> 2) & 1), y ^ ((k >> 1) & 1), c ^ (k & 1))

        def copy(k, a):
            p = peer(k)
            p_id = 4 * p[0] + 2 * p[1] + p[2]
            return pltpu.make_async_remote_copy(
                src_ref=ins[a].at[p_id], dst_ref=outs[a].at[my_id],
                send_sem=send_sems.at[k - 1, a], recv_sem=recv_sems.at[k - 1, a],
                device_id=p, device_id_type=pl.DeviceIdType.MESH)

        def arrival(k, a):
            p = peer(k)
            p_id = 4 * p[0] + 2 * p[1] + p[2]
            return pltpu.make_async_remote_copy(
                src_ref=ins[a].at[p_id], dst_ref=outs[a].at[p_id],
                send_sem=send_sems.at[k - 1, a], recv_sem=recv_sems.at[k - 1, a],
                device_id=p, device_id_type=pl.DeviceIdType.MESH)

        mine = [pltpu.make_async_copy(ins[a].at[my_id], outs[a].at[my_id], local_sems.at[a]) for a in range(n)]
        for cp in mine:
            cp.start()
        sends = [copy(k, a) for k in range(1, N_DEV) for a in range(n)]
        for cp in sends:
            cp.start()
        for k in range(1, N_DEV):
            for a in range(n):
                arrival(k, a).wait_recv()
        for cp in sends:
            cp.wait_send()
        for cp in mine:
            cp.wait()

    hbm = pl.BlockSpec(memory_space=pl.ANY)
    return pl.pallas_call(
        body, in_specs=[hbm] * n, out_specs=[hbm] * n, out_shape=[_sds(p.shape, p.dtype) for p in parts],
        scratch_shapes=[pltpu.SemaphoreType.DMA((7, n)), pltpu.SemaphoreType.DMA((7, n)), pltpu.SemaphoreType.DMA((n,))],
        name="exchange_grads")(*parts)


def _peer_table():
    x, y, c = lax.axis_index("x"), lax.axis_index("y"), lax.axis_index("c")
    peers = []
    for k in range(N_DEV - 1):
        bits = k + 1
        p = (x ^ ((bits >> 2) & 1), y ^ ((bits >> 1) & 1), c ^ (bits & 1))
        peers.append((k, p, 4 * p[0] + 2 * p[1] + p[2]))
    return 4 * x + 2 * y + c, peers


_HBM = pl.BlockSpec(memory_space=pltpu.HBM)
_SEM = pl.BlockSpec(memory_space=pltpu.SEMAPHORE)
_EFFECT = pltpu.SideEffectType.DATAFLOW_SIDE_EFFECTING


def _push_copy(src, land, send_sems, recv_sems, a, k, p, src_slot, dst_slot):
    sem = a * (N_DEV - 1) + k
    return pltpu.make_async_remote_copy(
        src_ref=src[a] if src_slot is None else src[a].at[src_slot], dst_ref=land[a].at[dst_slot],
        send_sem=send_sems.at[sem], recv_sem=recv_sems.at[sem], device_id=p, device_id_type=pl.DeviceIdType.MESH)


def _push_start(srcs, scatter, name):
    n = len(srcs)
    lands = [lax.empty(s.shape if scatter else (N_DEV,) + s.shape, s.dtype) for s in srcs]

    def body(*refs):
        src, land = refs[:n], refs[n:2 * n]
        send_sems, recv_sems, token = refs[2 * n], refs[2 * n + 1], refs[-1]
        my_id, peers = _peer_table()
        for a in range(n):
            for k, p, p_id in peers:
                _push_copy(src, land, send_sems, recv_sems, a, k, p, p_id if scatter else None, my_id).start()
        token[...] = jnp.zeros_like(token)

    sems = pltpu.SemaphoreType.DMA(((N_DEV - 1) * n,))
    res = pl.pallas_call(
        body, name=name,
        out_shape=(sems, sems, *[pltpu.HBM(a.shape, a.dtype) for a in srcs + lands], _sds((8, 128), F32)),
        in_specs=[_HBM] * (2 * n), out_specs=(_SEM, _SEM, *[_HBM] * (2 * n), pl.BlockSpec(memory_space=pltpu.VMEM)),
        input_output_aliases={i: 2 + i for i in range(2 * n)},
        compiler_params=pltpu.CompilerParams(has_side_effects=_EFFECT),
    )(*[pltpu.with_memory_space_constraint(a, pltpu.HBM) for a in srcs + lands])
    return dict(send=res[0], recv=res[1], src=list(res[2:2 + n]), land=list(res[2 + n:2 + 2 * n]), token=res[-1],
                scatter=scatter)


def _push_wait(handle, after, name):
    n = len(handle["src"])
    scatter = handle["scatter"]

    def body(*refs):
        src, land = refs[:n], refs[n:2 * n]
        send_sems, recv_sems = refs[2 * n], refs[2 * n + 1]
        _, peers = _peer_table()
        for a in range(n):
            for k, p, p_id in peers:
                cp = _push_copy(src, land, send_sems, recv_sems, a, k, p, p_id if scatter else None, p_id)
                cp.wait_send()
                cp.wait_recv()

    arrays = handle["src"] + handle["land"]
    res = pl.pallas_call(
        body, name=name, out_shape=tuple(pltpu.HBM(a.shape, a.dtype) for a in arrays),
        in_specs=[_HBM] * (2 * n) + [_SEM, _SEM, pl.BlockSpec(memory_space=pl.ANY)], out_specs=tuple([_HBM] * (2 * n)),
        input_output_aliases={i: i for i in range(2 * n)},
        compiler_params=pltpu.CompilerParams(has_side_effects=_EFFECT),
    )(*arrays, handle["send"], handle["recv"], after)
    return list(res[:n]), list(res[n:])


def _adamw(parts, own, w, m, v, name):
    rows, cols = w.shape
    tr = _pick(rows, (256, 128, 176, 64, 32, 16, 8))

    def body(*refs):
        if own is None:
            p_ref, w_ref, m_ref, v_ref, g_ref, d_ref, nm_ref, nv_ref = refs
        else:
            p_ref, own_ref, w_ref, m_ref, v_ref, g_ref, d_ref, nm_ref, nv_ref = refs
            my_id = 4 * lax.axis_index("x") + 2 * lax.axis_index("y") + lax.axis_index("c")
            mine = own_ref[...].astype(F32)
        g = None
        for s in range(N_DEV):
            term = p_ref[s].astype(F32)
            if own is not None:
                term = jnp.where(my_id == s, mine, term)
            g = term if g is None else g + term
        m_new = ADAM_B1 * m_ref[...] + (1.0 - ADAM_B1) * g
        v_new = ADAM_B2 * v_ref[...] + (1.0 - ADAM_B2) * (g * g)
        m_hat = m_new / (1.0 - ADAM_B1 ** ADAM_STEP)
        v_hat = v_new / (1.0 - ADAM_B2 ** ADAM_STEP)
        g_ref[...] = g
        d_ref[...] = -ADAM_LR * (m_hat / (jnp.sqrt(v_hat) + ADAM_EPS) + ADAM_WD * w_ref[...])
        nm_ref[...] = m_new
        nv_ref[...] = v_new

    spec = _row(tr, cols)
    operands = (parts, w, m, v) if own is None else (parts, own, w, m, v)
    return pl.pallas_call(
        body, grid=(rows // tr,),
        in_specs=[pl.BlockSpec((N_DEV, tr, cols), lambda i: (0, i, 0))] + [spec] * (len(operands) - 1),
        out_specs=[spec] * 4, out_shape=[_sds((rows, cols), F32)] * 4,
        name=name, compiler_params=_params(1))(*operands)


SMALL_REPLICATED = (("norm_pre_mix", 1024), ("ssm_conv_b", 3072), ("ssm_dt_bias", 32), ("ssm_a_log", 32),
                    ("ssm_d_skip", 32), ("ssm_norm", 2048), ("attn_sinks", 16), ("norm_post_mix", 1024),
                    ("norm_pre_ffn", 1024), ("ffn_conv_b", 5632), ("norm_post_ffn", 1024))
SMALL_SHARDED = (("meta_tokens", (N_META, D_MODEL // N_DEV)), ("ssm_conv_w", (SSM_CONV, CONV_DIM // N_DEV)),
                 ("ffn_conv_w", (FFN_CONV, 2 * FFN_DIM // N_DEV)))
BIG = (("w_in", (D_MODEL, N_IN // N_DEV), 1), ("w_ssm_out", (D_INNER // N_DEV, D_MODEL), 0),
       ("w_attn_out", (D_MODEL // N_DEV, D_MODEL), 0), ("w_mix_out", (D_MODEL // N_DEV, D_MODEL), 0),
       ("w_ffn_up", (D_MODEL, 2 * FFN_DIM // N_DEV), 1), ("w_ffn_down", (FFN_DIM // N_DEV, D_MODEL), 0))


def _rows_of(size):
    return -(-size // 128)


def _as_rows(flat):
    size = flat.shape[-1]
    rows = _rows_of(size)
    flat = jnp.pad(flat, [(0, 0)] * (flat.ndim - 1) + [(0, rows * 128 - size)])
    return flat.reshape(flat.shape[:-1] + (rows, 128))


def _pack_small(rep, sharded):
    pieces = [_as_rows(rep[name].reshape(-1)) for name, _ in SMALL_REPLICATED]
    pieces += [_as_rows(sharded[name].reshape(-1)) for name, _ in SMALL_SHARDED]
    packed = jnp.concatenate(pieces, axis=0)
    return jnp.pad(packed, ((0, -packed.shape[0] % 8), (0, 0)))


def _unpack_small(packed):
    out, row = {}, 0
    for name, size in SMALL_REPLICATED:
        out[name] = packed[row:row + _rows_of(size)].reshape(-1)[:size].reshape(1, size)
        row += _rows_of(size)
    for name, (r, c) in SMALL_SHARDED:
        out[name] = packed[row:row + _rows_of(r * c)].reshape(-1)[:r * c].reshape(r, c)
        row += _rows_of(r * c)
    return out


def _shard_major(g, shape, axis):
    r, c = shape
    if axis == 0:
        return g.reshape(N_DEV, r, c)
    return g.reshape(r, N_DEV, c).transpose(1, 0, 2)


def kernel(x, meta_tokens, norm_pre_mix, w_in, ssm_conv_w, ssm_conv_b, ssm_dt_bias, ssm_a_log, ssm_d_skip, ssm_norm, w_ssm_out, attn_sinks, w_attn_out, w_mix_out, norm_post_mix, norm_pre_ffn, w_ffn_up, ffn_conv_w, ffn_conv_b, w_ffn_down, norm_post_ffn, loss_target, m_meta_tokens, m_norm_pre_mix, m_w_in, m_ssm_conv_w, m_ssm_conv_b, m_ssm_dt_bias, m_ssm_a_log, m_ssm_d_skip, m_ssm_norm, m_w_ssm_out, m_attn_sinks, m_w_attn_out, m_w_mix_out, m_norm_post_mix, m_norm_pre_ffn, m_w_ffn_up, m_ffn_conv_w, m_ffn_conv_b, m_w_ffn_down, m_norm_post_ffn, v_meta_tokens, v_norm_pre_mix, v_w_in, v_ssm_conv_w, v_ssm_conv_b, v_ssm_dt_bias, v_ssm_a_log, v_ssm_d_skip, v_ssm_norm, v_w_ssm_out, v_attn_sinks, v_w_attn_out, v_w_mix_out, v_norm_post_mix, v_norm_pre_ffn, v_w_ffn_up, v_ffn_conv_w, v_ffn_conv_b, v_w_ffn_down, v_norm_post_ffn):
    names = ("meta_tokens", "norm_pre_mix", "w_in", "ssm_conv_w", "ssm_conv_b", "ssm_dt_bias", "ssm_a_log", "ssm_d_skip",
             "ssm_norm", "w_ssm_out", "attn_sinks", "w_attn_out", "w_mix_out", "norm_post_mix", "norm_pre_ffn", "w_ffn_up",
             "ffn_conv_w", "ffn_conv_b", "w_ffn_down", "norm_post_ffn")
    w_loc = dict(zip(names, (meta_tokens, norm_pre_mix, w_in, ssm_conv_w, ssm_conv_b, ssm_dt_bias, ssm_a_log, ssm_d_skip,
                             ssm_norm, w_ssm_out, attn_sinks, w_attn_out, w_mix_out, norm_post_mix, norm_pre_ffn, w_ffn_up,
                             ffn_conv_w, ffn_conv_b, w_ffn_down, norm_post_ffn)))
    m_loc = dict(zip(names, (m_meta_tokens, m_norm_pre_mix, m_w_in, m_ssm_conv_w, m_ssm_conv_b, m_ssm_dt_bias, m_ssm_a_log,
                             m_ssm_d_skip, m_ssm_norm, m_w_ssm_out, m_attn_sinks, m_w_attn_out, m_w_mix_out, m_norm_post_mix,
                             m_norm_pre_ffn, m_w_ffn_up, m_ffn_conv_w, m_ffn_conv_b, m_w_ffn_down, m_norm_post_ffn)))
    v_loc = dict(zip(names, (v_meta_tokens, v_norm_pre_mix, v_w_in, v_ssm_conv_w, v_ssm_conv_b, v_ssm_dt_bias, v_ssm_a_log,
                             v_ssm_d_skip, v_ssm_norm, v_w_ssm_out, v_attn_sinks, v_w_attn_out, v_w_mix_out, v_norm_post_mix,
                             v_norm_pre_ffn, v_w_ffn_up, v_ffn_conv_w, v_ffn_conv_b, v_w_ffn_down, v_norm_post_ffn)))

    def local2d(d, name):
        a = d[name]
        return a if name == "meta_tokens" else a.reshape(a.shape[1:])

    my_id = 4 * lax.axis_index("x") + 2 * lax.axis_index("y") + lax.axis_index("c")
    big = {name: (shape, axis) for name, shape, axis in BIG}

    def whole(name, g):
        (r, c), axis = big[name]
        return g.reshape(N_DEV * r, c) if axis == 0 else g.transpose(1, 0, 2).reshape(r, N_DEV * c)

    small_shard_pack = jnp.concatenate([_as_rows(local2d(w_loc, name).reshape(-1)) for name, _ in SMALL_SHARDED], axis=0)
    small_shard_pack = jnp.pad(small_shard_pack, ((0, -small_shard_pack.shape[0] % 8), (0, 0)))
    first = _all_gather([local2d(w_loc, "w_in").astype(BF16), small_shard_pack])
    rest_names = [name for name, _, _ in BIG if name != "w_in"]
    rest = [local2d(w_loc, name).astype(BF16) for name in rest_names]
    rest, first = lax.optimization_barrier((rest, first))
    rest_handle = _push_start(rest, False, "gather_rest_start")
    wt = {"w_in": whole("w_in", first[0])}
    row = 0
    for name, (r, c) in SMALL_SHARDED:
        blocks = first[1][:, row:row + _rows_of(r * c)].reshape(N_DEV, -1)[:, :r * c].reshape(N_DEV, r, c)
        wt[name] = blocks.transpose(1, 0, 2).reshape(r, N_DEV * c)
        row += _rows_of(r * c)
    for name, size in SMALL_REPLICATED:
        wt[name] = w_loc[name].reshape(1, size)

    def late_weights(after):
        own, landed = _push_wait(rest_handle, after, "gather_rest_wait")
        out = {}
        for name, mine, land in zip(rest_names, own, landed):
            out[name] = whole(name, lax.dynamic_update_index_in_dim(land, mine, my_id, 0))
        return out

    sent = {}

    def on_grad(name, g):
        shape, axis = big[name]
        sent[name] = _push_start([_shard_major(g, shape, axis).astype(BF16)], True, "send_" + name)
        return sent[name]["token"]

    x_seq = _tie(rest_handle["token"], x[0])
    loss_part, grad_x, grads = _local_step(x_seq, loss_target[0], wt, late_weights, on_grad)
    loss = lax.psum(loss_part, AXES)

    small_parts = []
    for name, (r, c) in SMALL_SHARDED:
        small_parts.append(_as_rows(_shard_major(grads[name], (r, c), 1).reshape(N_DEV, r * c)))
    rep_rows = jnp.concatenate([_as_rows(grads[name].reshape(-1)) for name, _ in SMALL_REPLICATED], axis=0)
    small_send = jnp.concatenate([jnp.broadcast_to(rep_rows[None], (N_DEV,) + rep_rows.shape)] + small_parts, axis=1)
    small_send = jnp.pad(small_send, ((0, 0), (0, -small_send.shape[1] % 8), (0, 0)))
    small_received = _exchange_partials([small_send])[0]

    def small_pack(d):
        return _pack_small({name: d[name] for name, _ in SMALL_REPLICATED}, {name: local2d(d, name) for name, _ in SMALL_SHARDED})

    grad_w, delta_w, new_m, new_v = {}, {}, {}, {}
    outs = _adamw(small_received, None, small_pack(w_loc), small_pack(m_loc), small_pack(v_loc), "adamw_small")
    after = outs[0]
    for name, handle in sent.items():
        src, landed = _push_wait(handle, after, "arrived_" + name)
        own = lax.dynamic_index_in_dim(src[0], my_id, 0, keepdims=False)
        g, d, nm, nv = _adamw(landed[0], own, local2d(w_loc, name), local2d(m_loc, name), local2d(v_loc, name), "adamw_" + name)
        after = g
        full = (1,) + big[name][0]
        grad_w[name], delta_w[name], new_m[name], new_v[name] = g.reshape(full), d.reshape(full), nm.reshape(full), nv.reshape(full)
    for dst, packed in zip((grad_w, delta_w, new_m, new_v), outs):
        for name, a in _unpack_small(packed).items():
            dst[name] = a.reshape(w_loc[name].shape)

    return (loss, grad_x[None], *[grad_w[n] for n in names], *[delta_w[n] for n in names],
            *[new_m[n] for n in names], *[new_v[n] for n in names])
```

```python
import jax
import jax.numpy as jnp
from jax import lax
from jax.experimental import pallas as pl
from jax.experimental.pallas import tpu as pltpu

F32 = jnp.float32
BF16 = jnp.bfloat16
HIGHEST = lax.Precision.HIGHEST

D_MODEL = 1024
N_META = 16
CHUNK = 128
META_PAD = CHUNK - N_META
D_INNER = 2048
HEAD_P = 64
SSM_HEADS = 32
SSM_GROUPS = 4
HEADS_PER_GROUP = SSM_HEADS // SSM_GROUPS
GROUP_W = HEADS_PER_GROUP * HEAD_P
D_STATE = 128
SSM_CONV = 4
CONV_DIM = D_INNER + 2 * SSM_GROUPS * D_STATE
ATTN_HEADS = 16
KV_HEADS = 4
ATTN_GROUP = ATTN_HEADS // KV_HEADS
DH = 64
KV_W = KV_HEADS * DH
FFN_DIM = 2816
FFN_CONV = 3
EPS = 1e-6
NEG = -1e30
N_DEV = 8
AXES = ("x", "y", "c")

OFF_Z, OFF_GATE, OFF_XBC, OFF_Q, OFF_K, OFF_V, OFF_DT = 0, 2048, 4096, 7168, 8192, 8448, 8704
N_INP = OFF_DT + SSM_GROUPS * 128
CUT_Z, CUT_XBC, CUT_DT, CUT_Q, CUT_K, CUT_V, CUT_G = 0, 2048, 5120, 5152, 6176, 6432, 6688
N_IN = 8736

ADAM_LR, ADAM_B1, ADAM_B2, ADAM_EPS, ADAM_WD, ADAM_STEP = 0.001, 0.9, 0.999, 1e-08, 0.01, 10

VMEM_LIMIT = 56 * 1024 * 1024


def _params(n_grid):
    return pltpu.CompilerParams(dimension_semantics=("arbitrary",) * n_grid, vmem_limit_bytes=VMEM_LIMIT)


def _sds(shape, dtype):
    return jax.ShapeDtypeStruct(shape, dtype)


def _pick(n, prefs):
    for c in prefs:
        if n % c == 0:
            return c
    raise ValueError(f"no tile of {prefs} divides {n}")


def _row(tr, width, cb=0):
    return pl.BlockSpec((tr, width), lambda i: (i, cb))


def _row_rev(tr, width, nt, cb=0):
    return pl.BlockSpec((tr, width), lambda i: (nt - 1 - i, cb))


def _full(shape):
    return pl.BlockSpec(shape, lambda *_: (0,) * len(shape))


def _sigmoid(x):
    return 1.0 / (1.0 + jnp.exp(-x))


def _softplus(x):
    return jnp.maximum(x, 0.0) + jnp.log(1.0 + jnp.exp(-jnp.abs(x)))


def _rms(x):
    return lax.rsqrt(jnp.mean(x * x, axis=-1, keepdims=True) + EPS)


def _rms_bwd(x, r, w, dy):
    xh = x * r
    g = dy * w
    dx = r * (g - xh * jnp.mean(g * xh, axis=-1, keepdims=True))
    return dx, jnp.sum(dy * xh, axis=0, keepdims=True)


def _row_ids(shape, tile_index, tr):
    return tile_index * tr + lax.broadcasted_iota(jnp.int32, shape, 0)


def _shift_down(cur, prev, s):
    if s == 0:
        return cur
    row = lax.broadcasted_iota(jnp.int32, cur.shape, 0)
    return jnp.where(row < s, pltpu.roll(prev, s, 0), pltpu.roll(cur, s, 0))


def _shift_up(cur, nxt, s):
    if s == 0:
        return cur
    n = cur.shape[0]
    row = lax.broadcasted_iota(jnp.int32, cur.shape, 0)
    return jnp.where(row >= n - s, pltpu.roll(nxt, n - s, 0), pltpu.roll(cur, n - s, 0))


def _matmul(a, b, *, ta=False, tb=False, out_dtype=F32, name, after=None):
    if ta:
        k_dim, m_dim = a.shape
    else:
        m_dim, k_dim = a.shape
    n_dim = b.shape[0] if tb else b.shape[1]
    tm = _pick(m_dim, (1408, 1024, 768, 512, 384, 256, 128))
    tn = _pick(n_dim, (1024, 1408, 768, 512, 384, 256, 128))
    tk = k_dim if (not ta and k_dim <= 2816) else _pick(k_dim, (1408, 1024, 768, 512, 384, 256, 128))
    nk = k_dim // tk
    dims = (((0 if ta else 1,), (1 if tb else 0,)), ((), ()))

    def body(a_ref, b_ref, *rest):
        o_ref = rest[-1]
        r = lax.dot_general(a_ref[...].astype(BF16), b_ref[...].astype(BF16), dims, preferred_element_type=F32)
        if nk == 1:
            o_ref[...] = r.astype(o_ref.dtype)
        else:
            k = pl.program_id(2)

            @pl.when(k == 0)
            def _():
                o_ref[...] = r

            @pl.when(k > 0)
            def _():
                o_ref[...] += r

    a_spec = pl.BlockSpec((tk, tm), lambda i, j, k: (k, i)) if ta else pl.BlockSpec((tm, tk), lambda i, j, k: (i, k))
    b_spec = pl.BlockSpec((tn, tk), lambda i, j, k: (j, k)) if tb else pl.BlockSpec((tk, tn), lambda i, j, k: (k, j))
    if nk > 1:
        assert out_dtype == F32
    extra_specs, extra = ([], ()) if after is None else ([pl.BlockSpec(memory_space=pl.ANY)], (after,))
    return pl.pallas_call(
        body, grid=(m_dim // tm, n_dim // tn, nk), in_specs=[a_spec, b_spec] + extra_specs,
        out_specs=pl.BlockSpec((tm, tn), lambda i, j, k: (i, j)), out_shape=_sds((m_dim, n_dim), out_dtype),
        name=name, compiler_params=_params(3))(a, b, *extra)


def _prenorm(h, w):
    t_rows = h.shape[0]
    tr = _pick(t_rows, (384, 128))

    def body(h_ref, w_ref, o_ref):
        x = h_ref[...]
        o_ref[...] = (x * _rms(x) * w_ref[...]).astype(BF16)

    return pl.pallas_call(body, grid=(t_rows // tr,), in_specs=[_row(tr, D_MODEL), _full((1, D_MODEL))],
                          out_specs=_row(tr, D_MODEL), out_shape=_sds((t_rows, D_MODEL), BF16),
                          name="prenorm", compiler_params=_params(1))(h, w)


def _xbc_specs(tr, rev_nt=None):
    cbs = [OFF_XBC // 1024 + j for j in range(CONV_DIM // 1024)]
    if rev_nt is None:
        return [_row(tr, 1024, cb) for cb in cbs]
    return [_row_rev(tr, 1024, rev_nt, cb) for cb in cbs]


def _ssm_conv_fwd(proj, conv_w, conv_b):
    t_rows = proj.shape[0]
    tr = CHUNK

    def body(x0, x1, x2, w_ref, b_ref, xc_ref, xa_ref, prev):
        @pl.when(pl.program_id(0) == 0)
        def _():
            prev[...] = jnp.zeros_like(prev)

        x = jnp.concatenate([x0[...], x1[...], x2[...]], axis=1)
        p = prev[...]
        acc = b_ref[...] + w_ref[SSM_CONV - 1:SSM_CONV, :] * x
        for s in range(1, SSM_CONV):
            acc = acc + w_ref[SSM_CONV - 1 - s:SSM_CONV - s, :] * _shift_down(x, p, s)
        prev[...] = x
        xc_ref[...] = acc
        xa_ref[...] = acc * _sigmoid(acc)

    return pl.pallas_call(
        body, grid=(t_rows // tr,),
        in_specs=_xbc_specs(tr) + [_full((SSM_CONV, CONV_DIM)), _full((1, CONV_DIM))],
        out_specs=[_row(tr, CONV_DIM), _row(tr, CONV_DIM)],
        out_shape=[_sds((t_rows, CONV_DIM), F32), _sds((t_rows, CONV_DIM), F32)],
        scratch_shapes=[pltpu.VMEM((tr, CONV_DIM), F32)],
        name="ssm_conv_fwd", compiler_params=_params(1))(proj, proj, proj, conv_w, conv_b)


def _ssm_post(y, proj, w):
    t_rows = y.shape[0]
    tr = CHUNK

    def body(y_ref, z_ref, w_ref, o_ref):
        z = z_ref[...]
        yz = y_ref[...] * z * _sigmoid(z)
        o_ref[...] = (yz * _rms(yz) * w_ref[...]).astype(BF16)

    return pl.pallas_call(body, grid=(t_rows // tr,),
                          in_specs=[_row(tr, D_INNER), _row(tr, D_INNER, OFF_Z // D_INNER), _full((1, D_INNER))],
                          out_specs=_row(tr, D_INNER), out_shape=_sds((t_rows, D_INNER), BF16),
                          name="ssm_post", compiler_params=_params(1))(y, proj, w)


def _mix_fwd(proj, y_ssm, y_attn):
    t_rows = y_ssm.shape[0]
    tr = _pick(t_rows, (384, 128))

    def body(g_ref, ys_ref, ya_ref, o_ref):
        g = _sigmoid(g_ref[...])
        o_ref[...] = (g[:, :D_MODEL] * ys_ref[...] + g[:, D_MODEL:] * ya_ref[...]).astype(BF16)

    return pl.pallas_call(body, grid=(t_rows // tr,),
                          in_specs=[_row(tr, 2 * D_MODEL, OFF_GATE // (2 * D_MODEL)), _row(tr, D_MODEL), _row(tr, D_MODEL)],
                          out_specs=_row(tr, D_MODEL), out_shape=_sds((t_rows, D_MODEL), BF16),
                          name="mix_fwd", compiler_params=_params(1))(proj, y_ssm, y_attn)


def _postmix(h, mix, w_post, w_pre):
    t_rows = h.shape[0]
    tr = CHUNK

    def body(h_ref, m_ref, wp_ref, wf_ref, h1_ref, hn_ref):
        m = m_ref[...]
        h1 = h_ref[...] + m * _rms(m) * wp_ref[...]
        h1 = jnp.where(_row_ids(h1.shape, pl.program_id(0), tr) >= META_PAD, h1, 0.0)
        h1_ref[...] = h1
        hn_ref[...] = (h1 * _rms(h1) * wf_ref[...]).astype(BF16)

    return pl.pallas_call(body, grid=(t_rows // tr,),
                          in_specs=[_row(tr, D_MODEL), _row(tr, D_MODEL), _full((1, D_MODEL)), _full((1, D_MODEL))],
                          out_specs=[_row(tr, D_MODEL), _row(tr, D_MODEL)],
                          out_shape=[_sds((t_rows, D_MODEL), F32), _sds((t_rows, D_MODEL), BF16)],
                          name="postmix", compiler_params=_params(1))(h, mix, w_post, w_pre)


def _ffn_act(up, conv_w, conv_b):
    t_rows = up.shape[0]
    tr = CHUNK
    width = 2 * FFN_DIM

    def body(up_ref, w_ref, b_ref, u_ref, act_ref, prev):
        @pl.when(pl.program_id(0) == 0)
        def _():
            prev[...] = jnp.zeros_like(prev)

        x = up_ref[...]
        p = prev[...]
        u = b_ref[...] + w_ref[FFN_CONV - 1:FFN_CONV, :] * x
        for s in range(1, FFN_CONV):
            u = u + w_ref[FFN_CONV - 1 - s:FFN_CONV - s, :] * _shift_down(x, p, s)
        prev[...] = x
        u_ref[...] = u
        a = u[:, :FFN_DIM]
        act_ref[...] = (a * _sigmoid(a) * u[:, FFN_DIM:]).astype(BF16)

    return pl.pallas_call(
        body, grid=(t_rows // tr,), in_specs=[_row(tr, width), _full((FFN_CONV, width)), _full((1, width))],
        out_specs=[_row(tr, width), _row(tr, FFN_DIM)],
        out_shape=[_sds((t_rows, width), F32), _sds((t_rows, FFN_DIM), BF16)],
        scratch_shapes=[pltpu.VMEM((tr, width), F32)],
        name="ffn_act", compiler_params=_params(1))(up, conv_w, conv_b)


def _final(h1, f, target, w):
    t_rows = h1.shape[0]
    tr = CHUNK

    def body(h1_ref, f_ref, t_ref, w_ref, df_ref, dy_ref, dw_ref, loss_ref):
        i = pl.program_id(0)

        @pl.when(i == 0)
        def _():
            dw_ref[...] = jnp.zeros_like(dw_ref)
            loss_ref[...] = jnp.zeros_like(loss_ref)

        f_val = f_ref[...]
        r = _rms(f_val)
        wv = w_ref[...]
        h2 = h1_ref[...] + f_val * r * wv
        diff = jnp.where(i >= 1, h2 - t_ref[...], 0.0)
        loss_ref[...] += 0.5 * jnp.sum(diff * diff) * (1.0 / D_MODEL)
        dy = diff * (1.0 / D_MODEL)
        dy_ref[...] = dy
        df, dw = _rms_bwd(f_val, r, wv, dy)
        df_ref[...] = df.astype(BF16)
        dw_ref[...] += dw

    tgt_spec = pl.BlockSpec((tr, D_MODEL), lambda i: (jnp.maximum(i - 1, 0), 0))
    return pl.pallas_call(
        body, grid=(t_rows // tr,),
        in_specs=[_row(tr, D_MODEL), _row(tr, D_MODEL), tgt_spec, _full((1, D_MODEL))],
        out_specs=[_row(tr, D_MODEL), _row(tr, D_MODEL), _full((1, D_MODEL)), _full((1, 128))],
        out_shape=[_sds((t_rows, D_MODEL), BF16), _sds((t_rows, D_MODEL), F32), _sds((1, D_MODEL), F32), _sds((1, 128), F32)],
        name="final", compiler_params=_params(1))(h1, f, target, w)


def _ffn_act_bwd(u, up, dact, conv_w):
    t_rows = u.shape[0]
    tr = CHUNK
    nt = t_rows // tr
    width = 2 * FFN_DIM

    def body(u_ref, up_ref, da_ref, w_ref, dup_ref, dw_ref, db_ref, nxt):
        @pl.when(pl.program_id(0) == 0)
        def _():
            nxt[...] = jnp.zeros_like(nxt)
            dw_ref[...] = jnp.zeros_like(dw_ref)
            db_ref[...] = jnp.zeros_like(db_ref)

        u_val = u_ref[...]
        a, g = u_val[:, :FFN_DIM], u_val[:, FFN_DIM:]
        d = da_ref[...]
        s = _sigmoid(a)
        du = jnp.concatenate([d * g * s * (1.0 + a * (1.0 - s)), d * a * s], axis=1)
        n = nxt[...]
        x = up_ref[...]
        dup = jnp.zeros_like(du)
        for sh in range(FFN_CONV):
            k = FFN_CONV - 1 - sh
            moved = _shift_up(du, n, sh)
            dup = dup + w_ref[k:k + 1, :] * moved
            dw_ref[k:k + 1, :] += jnp.sum(moved * x, axis=0, keepdims=True)
        db_ref[...] += jnp.sum(du, axis=0, keepdims=True)
        nxt[...] = du
        dup_ref[...] = dup.astype(BF16)

    return pl.pallas_call(
        body, grid=(nt,),
        in_specs=[_row_rev(tr, width, nt), _row_rev(tr, width, nt), _row_rev(tr, FFN_DIM, nt), _full((FFN_CONV, width))],
        out_specs=[_row_rev(tr, width, nt), _full((FFN_CONV, width)), _full((1, width))],
        out_shape=[_sds((t_rows, width), BF16), _sds((FFN_CONV, width), F32), _sds((1, width), F32)],
        scratch_shapes=[pltpu.VMEM((tr, width), F32)],
        name="ffn_act_bwd", compiler_params=_params(1))(u, up, dact, conv_w)


def _postmix_bwd(h1, dhn2, dy, mix, w_pre, w_post):
    t_rows = h1.shape[0]
    tr = CHUNK

    def body(h1_ref, dhn_ref, dy_ref, m_ref, wf_ref, wp_ref, dmix_ref, dh_ref, dwf_ref, dwp_ref):
        @pl.when(pl.program_id(0) == 0)
        def _():
            dwf_ref[...] = jnp.zeros_like(dwf_ref)
            dwp_ref[...] = jnp.zeros_like(dwp_ref)

        h1v = h1_ref[...]
        dx, dwf = _rms_bwd(h1v, _rms(h1v), wf_ref[...], dhn_ref[...])
        dwf_ref[...] += dwf
        dh1 = dy_ref[...] + dx
        dh1 = jnp.where(_row_ids(dh1.shape, pl.program_id(0), tr) >= META_PAD, dh1, 0.0)
        dh_ref[...] = dh1
        m = m_ref[...]
        dmix, dwp = _rms_bwd(m, _rms(m), wp_ref[...], dh1)
        dwp_ref[...] += dwp
        dmix_ref[...] = dmix.astype(BF16)

    return pl.pallas_call(
        body, grid=(t_rows // tr,),
        in_specs=[_row(tr, D_MODEL)] * 4 + [_full((1, D_MODEL))] * 2,
        out_specs=[_row(tr, D_MODEL), _row(tr, D_MODEL), _full((1, D_MODEL)), _full((1, D_MODEL))],
        out_shape=[_sds((t_rows, D_MODEL), BF16), _sds((t_rows, D_MODEL), F32), _sds((1, D_MODEL), F32), _sds((1, D_MODEL), F32)],
        name="postmix_bwd", compiler_params=_params(1))(h1, dhn2, dy, mix, w_pre, w_post)


def _mix_bwd(dmixed, proj, y_ssm, y_attn):
    t_rows = dmixed.shape[0]
    tr = _pick(t_rows, (384, 128))

    def body(d_ref, g_ref, ys_ref, ya_ref, dys_ref, dya_ref, dg_ref):
        d = d_ref[...]
        g = _sigmoid(g_ref[...])
        g1, g2 = g[:, :D_MODEL], g[:, D_MODEL:]
        dys_ref[...] = (d * g1).astype(BF16)
        dya_ref[...] = (d * g2).astype(BF16)
        dg_ref[...] = jnp.concatenate([d * ys_ref[...] * g1 * (1.0 - g1), d * ya_ref[...] * g2 * (1.0 - g2)],
                                      axis=1).astype(BF16)

    return pl.pallas_call(
        body, grid=(t_rows // tr,),
        in_specs=[_row(tr, D_MODEL), _row(tr, 2 * D_MODEL, OFF_GATE // (2 * D_MODEL)), _row(tr, D_MODEL), _row(tr, D_MODEL)],
        out_specs=[_row(tr, D_MODEL), _row(tr, D_MODEL), _row(tr, 2 * D_MODEL)],
        out_shape=[_sds((t_rows, D_MODEL), BF16), _sds((t_rows, D_MODEL), BF16), _sds((t_rows, 2 * D_MODEL), BF16)],
        name="mix_bwd", compiler_params=_params(1))(dmixed, proj, y_ssm, y_attn)


def _ssm_post_bwd(y, proj, dyn, w):
    t_rows = y.shape[0]
    tr = CHUNK

    def body(y_ref, z_ref, d_ref, w_ref, dy_ref, dz_ref, dw_ref):
        @pl.when(pl.program_id(0) == 0)
        def _():
            dw_ref[...] = jnp.zeros_like(dw_ref)

        yv, z = y_ref[...], z_ref[...]
        sz = _sigmoid(z)
        silu = z * sz
        yz = yv * silu
        dyz, dw = _rms_bwd(yz, _rms(yz), w_ref[...], d_ref[...])
        dw_ref[...] += dw
        dy_ref[...] = dyz * silu
        dz_ref[...] = (dyz * yv * sz * (1.0 + z * (1.0 - sz))).astype(BF16)

    return pl.pallas_call(
        body, grid=(t_rows // tr,),
        in_specs=[_row(tr, D_INNER), _row(tr, D_INNER, OFF_Z // D_INNER), _row(tr, D_INNER), _full((1, D_INNER))],
        out_specs=[_row(tr, D_INNER), _row(tr, D_INNER), _full((1, D_INNER))],
        out_shape=[_sds((t_rows, D_INNER), F32), _sds((t_rows, D_INNER), BF16), _sds((1, D_INNER), F32)],
        name="ssm_post_bwd", compiler_params=_params(1))(y, proj, dyn, w)


def _ssm_conv_bwd(xc, proj, dxs, dbm, dcm, conv_w):
    t_rows = xc.shape[0]
    tr = CHUNK
    nt = t_rows // tr
    bc_w = SSM_GROUPS * D_STATE

    def body(xc_ref, x0, x1, x2, dxs_ref, db_ref, dc_ref, w_ref, dx_ref, dw_ref, dbias_ref, nxt):
        @pl.when(pl.program_id(0) == 0)
        def _():
            nxt[...] = jnp.zeros_like(nxt)
            dw_ref[...] = jnp.zeros_like(dw_ref)
            dbias_ref[...] = jnp.zeros_like(dbias_ref)

        c = xc_ref[...]
        s = _sigmoid(c)
        dact = jnp.concatenate([dxs_ref[...], db_ref[...], dc_ref[...]], axis=1)
        dpre = dact * s * (1.0 + c * (1.0 - s))
        x = jnp.concatenate([x0[...], x1[...], x2[...]], axis=1)
        n = nxt[...]
        dx = jnp.zeros_like(dpre)
        for sh in range(SSM_CONV):
            k = SSM_CONV - 1 - sh
            moved = _shift_up(dpre, n, sh)
            dx = dx + w_ref[k:k + 1, :] * moved
            dw_ref[k:k + 1, :] += jnp.sum(moved * x, axis=0, keepdims=True)
        dbias_ref[...] += jnp.sum(dpre, axis=0, keepdims=True)
        nxt[...] = dpre
        dx_ref[...] = dx.astype(BF16)

    return pl.pallas_call(
        body, grid=(nt,),
        in_specs=[_row_rev(tr, CONV_DIM, nt)] + _xbc_specs(tr, nt)
        + [_row_rev(tr, D_INNER, nt), _row_rev(tr, bc_w, nt), _row_rev(tr, bc_w, nt), _full((SSM_CONV, CONV_DIM))],
        out_specs=[_row_rev(tr, CONV_DIM, nt), _full((SSM_CONV, CONV_DIM)), _full((1, CONV_DIM))],
        out_shape=[_sds((t_rows, CONV_DIM), BF16), _sds((SSM_CONV, CONV_DIM), F32), _sds((1, CONV_DIM), F32)],
        scratch_shapes=[pltpu.VMEM((tr, CONV_DIM), F32)],
        name="ssm_conv_bwd", compiler_params=_params(1))(xc, proj, proj, proj, dxs, dbm, dcm, conv_w)


def _prenorm_bwd(h, dhn, dh, w):
    t_rows = h.shape[0]
    tr = CHUNK

    def body(h_ref, d_ref, r_ref, w_ref, o_ref, dw_ref):
        @pl.when(pl.program_id(0) == 0)
        def _():
            dw_ref[...] = jnp.zeros_like(dw_ref)

        x = h_ref[...]
        dx, dw = _rms_bwd(x, _rms(x), w_ref[...], d_ref[...])
        dw_ref[...] += dw
        o_ref[...] = r_ref[...] + dx

    return pl.pallas_call(
        body, grid=(t_rows // tr,), in_specs=[_row(tr, D_MODEL)] * 3 + [_full((1, D_MODEL))],
        out_specs=[_row(tr, D_MODEL), _full((1, D_MODEL))],
        out_shape=[_sds((t_rows, D_MODEL), F32), _sds((1, D_MODEL), F32)],
        name="prenorm_bwd", compiler_params=_params(1))(h, dhn, dh, w)


def _ssd_common(dtr_ref, dtb_ref, alog_ref, chunk_index):
    rows = lax.broadcasted_iota(jnp.int32, (CHUNK, CHUNK), 0)
    cols = lax.broadcasted_iota(jnp.int32, (CHUNK, CHUNK), 1)
    low = rows >= cols
    tril = low.astype(F32)
    raw = dtr_ref[...] + dtb_ref[0]
    live = _row_ids(raw.shape, chunk_index, CHUNK) >= META_PAD
    dt = jnp.where(live, _softplus(raw), 0.0)
    a_head = -jnp.exp(alog_ref[0])
    a = dt * a_head
    cs = jnp.dot(tril, a, precision=HIGHEST, preferred_element_type=F32)
    expand = (lax.broadcasted_iota(jnp.int32, (CHUNK, GROUP_W), 1) // HEAD_P
              == lax.broadcasted_iota(jnp.int32, (CHUNK, GROUP_W), 0)).astype(F32)
    dtx = jnp.dot(dt, expand, precision=HIGHEST, preferred_element_type=F32)
    csx = jnp.dot(cs, expand, precision=HIGHEST, preferred_element_type=F32)
    fold = (lax.broadcasted_iota(jnp.int32, (GROUP_W, CHUNK), 0) // HEAD_P
            == lax.broadcasted_iota(jnp.int32, (GROUP_W, CHUNK), 1)).astype(F32)
    return dict(low=low, triu=(rows <= cols).astype(F32), raw=raw, live=live, dt=dt, a_head=a_head, cs=cs, cs_t=cs.T,
                fold=fold, dtx=dtx, csx=csx)


def _decay_matrix(cm, j):
    diff = cm["cs"][:, j:j + 1] - cm["cs_t"][j:j + 1, :]
    return jnp.where(cm["low"], jnp.exp(jnp.where(cm["low"], diff, 0.0)), 0.0)


def _dot(a, b, dims):
    return lax.dot_general(a.astype(BF16), b.astype(BF16), (dims, ((), ())), preferred_element_type=F32)


def _ssd_specs(nt, rev):
    def idx(c):
        return nt - 1 - c if rev else c
    xs = pl.BlockSpec((CHUNK, GROUP_W), lambda g, c: (idx(c), g))
    bm = pl.BlockSpec((CHUNK, D_STATE), lambda g, c: (idx(c), D_INNER // D_STATE + g))
    cm = pl.BlockSpec((CHUNK, D_STATE), lambda g, c: (idx(c), D_INNER // D_STATE + SSM_GROUPS + g))
    dtr = pl.BlockSpec((CHUNK, 128), lambda g, c: (idx(c), OFF_DT // 128 + g))
    par = pl.BlockSpec((1, 1, 128), lambda g, c: (g, 0, 0))
    par_x = pl.BlockSpec((1, 1, GROUP_W), lambda g, c: (g, 0, 0))
    return xs, bm, cm, dtr, par, par_x, idx


def _ssd_fwd(xact, proj, dtb, alog, dskip_x):
    t_rows = xact.shape[0]
    nt = t_rows // CHUNK
    xs_spec, b_spec, c_spec, dtr_spec, par, par_x, _ = _ssd_specs(nt, False)

    def body(xs_ref, b_ref, c_ref, dtr_ref, dtb_ref, alog_ref, dsk_ref, y_ref, hst_ref, state):
        c = pl.program_id(1)

        @pl.when(c == 0)
        def _():
            state[...] = jnp.zeros_like(state)

        cm = _ssd_common(dtr_ref, dtb_ref, alog_ref, c)
        xs, bm, cmat = xs_ref[...], b_ref[...], c_ref[...]
        x_dt = xs * cm["dtx"]
        h_in = state[...]
        hst_ref[0, 0] = h_in
        y_ref[...] = _dot(cmat, h_in, ((1,), (0,))) * jnp.exp(cm["csx"]) + xs * dsk_ref[0]
        cb = _dot(cmat, bm, ((1,), (1,)))
        for j in range(HEADS_PER_GROUP):
            sl = slice(j * HEAD_P, (j + 1) * HEAD_P)
            y_ref[:, sl] += _dot(cb * _decay_matrix(cm, j), x_dt[:, sl], ((1,), (0,)))
        cs_last = cm["csx"][CHUNK - 1:CHUNK, :]
        state[...] = h_in * jnp.exp(cs_last) + _dot(bm, x_dt * jnp.exp(cs_last - cm["csx"]), ((0,), (0,)))

    return pl.pallas_call(
        body, grid=(SSM_GROUPS, nt),
        in_specs=[xs_spec, b_spec, c_spec, dtr_spec, par, par, par_x],
        out_specs=[xs_spec, pl.BlockSpec((1, 1, D_STATE, GROUP_W), lambda g, c: (c, g, 0, 0))],
        out_shape=[_sds((t_rows, D_INNER), F32), _sds((nt, SSM_GROUPS, D_STATE, GROUP_W), F32)],
        scratch_shapes=[pltpu.VMEM((D_STATE, GROUP_W), F32)],
        name="ssd_fwd", compiler_params=_params(2))(xact, xact, xact, proj, dtb, alog, dskip_x)


def _ssd_bwd(xact, proj, dtb, alog, dskip_x, dy, hst):
    t_rows = xact.shape[0]
    nt = t_rows // CHUNK
    xs_spec, b_spec, c_spec, dtr_spec, par, par_x, idx = _ssd_specs(nt, True)
    h_spec = pl.BlockSpec((1, 1, D_STATE, GROUP_W), lambda g, c: (idx(c), g, 0, 0))
    hn_spec = pl.BlockSpec((1, 1, D_STATE, GROUP_W), lambda g, c: (jnp.minimum(idx(c) + 1, nt - 1), g, 0, 0))
    bc_out = pl.BlockSpec((CHUNK, D_STATE), lambda g, c: (idx(c), g))

    def body(xs_ref, b_ref, c_ref, dtr_ref, dtb_ref, alog_ref, dsk_ref, dy_ref, h_ref, hn_ref,
             dxs_ref, db_ref, dc_ref, ddt_ref, dalog_ref, ddtb_ref, dd_ref, dstate, dx_buf):
        step = pl.program_id(1)

        @pl.when(step == 0)
        def _():
            dstate[...] = jnp.zeros_like(dstate)
            dalog_ref[...] = jnp.zeros_like(dalog_ref)
            ddtb_ref[...] = jnp.zeros_like(ddtb_ref)
            dd_ref[...] = jnp.zeros_like(dd_ref)

        cm = _ssd_common(dtr_ref, dtb_ref, alog_ref, idx(step))
        xs, bm, cmat = xs_ref[...], b_ref[...], c_ref[...]
        dsk = dsk_ref[0]
        x_dt = xs * cm["dtx"]
        h_in, h_next = h_ref[0, 0], hn_ref[0, 0]
        dyv = dy_ref[...]
        dh = dstate[...]
        grow = jnp.exp(cm["csx"])
        cs_last = cm["csx"][CHUNK - 1:CHUNK, :]
        fade = jnp.exp(cs_last - cm["csx"])
        dy_grow = dyv * grow
        x_fade = x_dt * fade
        cb = _dot(cmat, bm, ((1,), (1,)))
        ml = jnp.zeros((CHUNK, CHUNK), F32)
        row_id = lax.broadcasted_iota(jnp.int32, (CHUNK, CHUNK), 0)
        col_id = lax.broadcasted_iota(jnp.int32, (CHUNK, CHUNK), 1)
        w_rows = jnp.zeros((CHUNK, CHUNK), F32)
        w_cols = jnp.zeros((CHUNK, CHUNK), F32)
        for j in range(HEADS_PER_GROUP):
            sl = slice(j * HEAD_P, (j + 1) * HEAD_P)
            lm = _decay_matrix(cm, j)
            mlj = _dot(dyv[:, sl], x_dt[:, sl], ((1,), (1,))) * lm
            ml = ml + mlj
            wm = mlj * cb
            w_rows = jnp.where(col_id == j, jnp.sum(wm, axis=1, keepdims=True), w_rows)
            w_cols = jnp.where(row_id == j, jnp.sum(wm, axis=0, keepdims=True), w_cols)
            dx_buf[:, sl] = _dot(cb * lm, dyv[:, sl], ((0,), (0,)))
        dx_off = fade * _dot(bm, dh, ((1,), (0,)))
        dx = dx_buf[...] + dx_off
        dc_ref[...] = _dot(ml, bm, ((1,), (0,))) + _dot(dy_grow, h_in, ((1,), (1,)))
        db_ref[...] = _dot(ml, cmat, ((0,), (0,))) + _dot(x_fade, dh, ((1,), (1,)))
        fold = cm["fold"]
        y_off = _dot(cmat, h_in, ((1,), (0,))) * grow
        dcs = (w_rows - w_cols.T) + jnp.dot(dyv * y_off - x_dt * dx_off, fold, precision=HIGHEST, preferred_element_type=F32)
        tail = jnp.broadcast_to(jnp.sum(dh * h_next, axis=0, keepdims=True), (8, GROUP_W))
        tail = jnp.dot(tail, fold, precision=HIGHEST, preferred_element_type=F32)[0:1, :]
        last_row = lax.broadcasted_iota(jnp.int32, (CHUNK, 128), 0) == CHUNK - 1
        dcs = dcs + jnp.where(last_row, tail, 0.0)
        da = jnp.dot(cm["triu"], dcs, precision=HIGHEST, preferred_element_type=F32)
        ddt = da * cm["a_head"] + jnp.dot(dx * xs, fold, precision=HIGHEST, preferred_element_type=F32)
        ddt_raw = jnp.where(cm["live"], ddt * _sigmoid(cm["raw"]), 0.0)
        ddt_ref[...] = ddt_raw.astype(BF16)
        ddtb_ref[0] += jnp.sum(ddt_raw, axis=0, keepdims=True)
        dalog_ref[0] += jnp.sum(da * cm["dt"], axis=0, keepdims=True) * cm["a_head"]
        dd_ref[0] += jnp.sum(dyv * xs, axis=0, keepdims=True)
        dxs_ref[...] = dx * cm["dtx"] + dyv * dsk
        dstate[...] = dh * jnp.exp(cs_last) + _dot(cmat, dy_grow, ((0,), (0,)))

    return pl.pallas_call(
        body, grid=(SSM_GROUPS, nt),
        in_specs=[xs_spec, b_spec, c_spec, dtr_spec, par, par, par_x, xs_spec, h_spec, hn_spec],
        out_specs=[xs_spec, bc_out, bc_out, bc_out, par, par, par_x],
        out_shape=[_sds((t_rows, D_INNER), F32), _sds((t_rows, SSM_GROUPS * D_STATE), F32),
                   _sds((t_rows, SSM_GROUPS * D_STATE), F32), _sds((t_rows, SSM_GROUPS * 128), BF16),
                   _sds((SSM_GROUPS, 1, 128), F32), _sds((SSM_GROUPS, 1, 128), F32), _sds((SSM_GROUPS, 1, GROUP_W), F32)],
        scratch_shapes=[pltpu.VMEM((D_STATE, GROUP_W), F32), pltpu.VMEM((CHUNK, GROUP_W), F32)],
        name="ssd_bwd", compiler_params=_params(2))(xact, xact, xact, proj, dtb, alog, dskip_x, dy, hst, hst)


def _swa_probs(q_ref, kp_ref, kc_ref, km_ref, sink_ref, kv_head, n):
    rows_q = ATTN_GROUP * CHUNK
    q = q_ref[...]
    qs = jnp.concatenate([q[:, g * DH:(g + 1) * DH] for g in range(ATTN_GROUP)], axis=0)
    kcat = jnp.concatenate([kp_ref[0], kc_ref[0]], axis=0)
    kmeta = km_ref[0][META_PAD:, :]
    scale = DH ** -0.5
    head = (kv_head * ATTN_GROUP + lax.broadcasted_iota(jnp.int32, (rows_q, 1), 0) // CHUNK + 1).astype(F32)
    slope = jnp.exp(head * (-8.0 / ATTN_HEADS * 0.6931471805599453))
    q_in = lax.broadcasted_iota(jnp.int32, (rows_q, 2 * CHUNK), 0) % CHUNK
    s_in = lax.broadcasted_iota(jnp.int32, (rows_q, 2 * CHUNK), 1)
    dist = q_in - s_in + CHUNK
    ok = (dist >= 0) & (dist < CHUNK) & (s_in + n * CHUNK >= 2 * CHUNK)
    s_band = _dot(qs, kcat, ((1,), (1,))) * scale - slope * dist.astype(F32)
    s_band = jnp.where(ok, s_band, NEG)
    q_pos = lax.broadcasted_iota(jnp.int32, (rows_q, N_META), 0) % CHUNK + n * CHUNK - META_PAD
    ok_m = lax.broadcasted_iota(jnp.int32, (rows_q, N_META), 1) <= q_pos
    s_meta = jnp.where(ok_m, _dot(qs, kmeta, ((1,), (1,))) * scale, NEG)
    sink = sink_ref[0]
    m = jnp.maximum(jnp.maximum(jnp.max(s_band, axis=1, keepdims=True), jnp.max(s_meta, axis=1, keepdims=True)), sink)
    p_band, p_meta, p_sink = jnp.exp(s_band - m), jnp.exp(s_meta - m), jnp.exp(sink - m)
    inv = 1.0 / (jnp.sum(p_band, axis=1, keepdims=True) + jnp.sum(p_meta, axis=1, keepdims=True) + p_sink)
    return qs, kcat, kmeta, p_band * inv, p_meta * inv, p_sink * inv


def _swa_specs(nt, rev):
    def idx(n):
        return nt - 1 - n if rev else n
    q = pl.BlockSpec((CHUNK, ATTN_GROUP * DH), lambda k, n: (idx(n), k))
    q_proj = pl.BlockSpec((CHUNK, ATTN_GROUP * DH), lambda k, n: (idx(n), OFF_Q // (ATTN_GROUP * DH) + k))
    cur = pl.BlockSpec((1, CHUNK, DH), lambda k, n: (k, idx(n), 0))
    prev = pl.BlockSpec((1, CHUNK, DH), lambda k, n: (k, jnp.maximum(idx(n) - 1, 0), 0))
    first = pl.BlockSpec((1, CHUNK, DH), lambda k, n: (k, 0, 0))
    sink = pl.BlockSpec((1, ATTN_GROUP * CHUNK, 1), lambda k, n: (k, 0, 0))
    return q, q_proj, cur, prev, first, sink, idx


def _swa_fwd(proj, k_heads, v_heads, sink_rows):
    t_rows = proj.shape[0]
    nt = t_rows // CHUNK
    o_spec, q_spec, cur, prev, first, sink_spec, _ = _swa_specs(nt, False)

    def body(q_ref, kp_ref, kc_ref, km_ref, vp_ref, vc_ref, vm_ref, sink_ref, o_ref):
        _, _, _, p_band, p_meta, _ = _swa_probs(q_ref, kp_ref, kc_ref, km_ref, sink_ref, pl.program_id(0), pl.program_id(1))
        vcat = jnp.concatenate([vp_ref[0], vc_ref[0]], axis=0)
        out = _dot(p_band, vcat, ((1,), (0,))) + _dot(p_meta, vm_ref[0][META_PAD:, :], ((1,), (0,)))
        for g in range(ATTN_GROUP):
            o_ref[:, g * DH:(g + 1) * DH] = out[g * CHUNK:(g + 1) * CHUNK, :]

    return pl.pallas_call(
        body, grid=(KV_HEADS, nt), in_specs=[q_spec, prev, cur, first, prev, cur, first, sink_spec],
        out_specs=o_spec, out_shape=_sds((t_rows, ATTN_HEADS * DH), F32),
        name="swa_fwd", compiler_params=_params(2))(proj, k_heads, k_heads, k_heads, v_heads, v_heads, v_heads, sink_rows)


def _swa_bwd(proj, k_heads, v_heads, sink_rows, out, dout):
    t_rows = proj.shape[0]
    nt = t_rows // CHUNK
    o_spec, q_spec, cur, prev, first, sink_spec, idx = _swa_specs(nt, True)
    scale = DH ** -0.5

    def body(q_ref, kp_ref, kc_ref, km_ref, vp_ref, vc_ref, vm_ref, sink_ref, o_ref, do_ref,
             dq_ref, dk_ref, dv_ref, dsink_ref, carry_k, carry_v, meta_k, meta_v):
        step = pl.program_id(1)
        n = idx(step)

        @pl.when(step == 0)
        def _():
            carry_k[...] = jnp.zeros_like(carry_k)
            carry_v[...] = jnp.zeros_like(carry_v)
            meta_k[...] = jnp.zeros_like(meta_k)
            meta_v[...] = jnp.zeros_like(meta_v)
            dsink_ref[...] = jnp.zeros_like(dsink_ref)

        qs, kcat, kmeta, p_band, p_meta, p_sink = _swa_probs(q_ref, kp_ref, kc_ref, km_ref, sink_ref, pl.program_id(0), n)
        vcat = jnp.concatenate([vp_ref[0], vc_ref[0]], axis=0)
        vmeta = vm_ref[0][META_PAD:, :]
        o, do = o_ref[...], do_ref[...]
        os_ = jnp.concatenate([o[:, g * DH:(g + 1) * DH] for g in range(ATTN_GROUP)], axis=0)
        dos = jnp.concatenate([do[:, g * DH:(g + 1) * DH] for g in range(ATTN_GROUP)], axis=0)
        delta = jnp.sum(dos * os_, axis=1, keepdims=True)
        ds_band = p_band * (_dot(dos, vcat, ((1,), (1,))) - delta)
        ds_meta = p_meta * (_dot(dos, vmeta, ((1,), (1,))) - delta)
        ds_sink = -p_sink * delta
        dqs = (_dot(ds_band, kcat, ((1,), (0,))) + _dot(ds_meta, kmeta, ((1,), (0,)))) * scale
        for g in range(ATTN_GROUP):
            dq_ref[:, g * DH:(g + 1) * DH] = dqs[g * CHUNK:(g + 1) * CHUNK, :]
            dsink_ref[0, g:g + 1, :] += jnp.sum(ds_sink[g * CHUNK:(g + 1) * CHUNK, :])
        dkcat = _dot(ds_band, qs, ((0,), (0,))) * scale
        dvcat = _dot(p_band, dos, ((0,), (0,)))
        meta_k[...] += _dot(ds_meta, qs, ((0,), (0,))) * scale
        meta_v[...] += _dot(p_meta, dos, ((0,), (0,)))
        dk_ref[0] = dkcat[CHUNK:, :] + carry_k[...]
        dv_ref[0] = dvcat[CHUNK:, :] + carry_v[...]
        carry_k[...] = dkcat[:CHUNK, :]
        carry_v[...] = dvcat[:CHUNK, :]

        @pl.when(n == 0)
        def _():
            dk_ref[0, META_PAD:, :] += meta_k[...]
            dv_ref[0, META_PAD:, :] += meta_v[...]

    return pl.pallas_call(
        body, grid=(KV_HEADS, nt),
        in_specs=[q_spec, prev, cur, first, prev, cur, first, sink_spec, o_spec, o_spec],
        out_specs=[o_spec, cur, cur, pl.BlockSpec((1, 8, 128), lambda k, n: (k, 0, 0))],
        out_shape=[_sds((t_rows, ATTN_HEADS * DH), F32), _sds((KV_HEADS, t_rows, DH), F32),
                   _sds((KV_HEADS, t_rows, DH), F32), _sds((KV_HEADS, 8, 128), F32)],
        scratch_shapes=[pltpu.VMEM((CHUNK, DH), F32), pltpu.VMEM((CHUNK, DH), F32),
                        pltpu.VMEM((N_META, DH), F32), pltpu.VMEM((N_META, DH), F32)],
        name="swa_bwd", compiler_params=_params(2))(proj, k_heads, k_heads, k_heads, v_heads, v_heads, v_heads,
                                                    sink_rows, out, dout)


def _pack_w_in(w_in):
    w_dt = w_in[:, CUT_DT:CUT_Q].reshape(D_MODEL, SSM_GROUPS, HEADS_PER_GROUP)
    w_dt = jnp.pad(w_dt, ((0, 0), (0, 0), (0, 128 - HEADS_PER_GROUP))).reshape(D_MODEL, SSM_GROUPS * 128)
    return jnp.concatenate([w_in[:, CUT_Z:CUT_XBC], w_in[:, CUT_G:], w_in[:, CUT_XBC:CUT_DT], w_in[:, CUT_Q:CUT_K],
                            w_in[:, CUT_K:CUT_V], w_in[:, CUT_V:CUT_G], w_dt], axis=1)


def _unpack_w_in(wp):
    w_dt = wp[:, OFF_DT:].reshape(D_MODEL, SSM_GROUPS, 128)[:, :, :HEADS_PER_GROUP].reshape(D_MODEL, SSM_HEADS)
    return jnp.concatenate([wp[:, OFF_Z:OFF_GATE], wp[:, OFF_XBC:OFF_Q], w_dt, wp[:, OFF_Q:OFF_K], wp[:, OFF_K:OFF_V],
                            wp[:, OFF_V:OFF_DT], wp[:, OFF_GATE:OFF_XBC]], axis=1)


def _group_rows(v, width):
    return jnp.pad(v.reshape(SSM_GROUPS, 1, HEADS_PER_GROUP), ((0, 0), (0, 0), (0, width - HEADS_PER_GROUP)))


def _to_heads(a):
    return a.reshape(a.shape[0], KV_HEADS, DH).transpose(1, 0, 2)


def _from_heads(a):
    return a.transpose(1, 0, 2).reshape(a.shape[1], KV_W)


def _local_step(x, target, wt, late_weights=None, on_grad=None, started=None):
    seq = x.shape[0]
    grads = {}

    def emit(name, g):
        grads[name] = g
        return None if on_grad is None else on_grad(name, g)
    h = jnp.concatenate([jnp.zeros((META_PAD, D_MODEL), F32), wt["meta_tokens"], x], axis=0)
    wp = _pack_w_in(wt["w_in"])
    dtb = _group_rows(wt["ssm_dt_bias"].reshape(-1), 128)
    alog = _group_rows(wt["ssm_a_log"].reshape(-1), 128)
    dskip_x = jnp.repeat(wt["ssm_d_skip"].reshape(-1), HEAD_P).reshape(SSM_GROUPS, 1, GROUP_W)
    sink_rows = jnp.repeat(wt["attn_sinks"].reshape(KV_HEADS, ATTN_GROUP), CHUNK, axis=1).reshape(KV_HEADS, ATTN_GROUP * CHUNK, 1)

    hn = _prenorm(h, wt["norm_pre_mix"])
    proj = _matmul(hn, wp, name="in_proj", after=started)
    xc, xact = _ssm_conv_fwd(proj, wt["ssm_conv_w"], wt["ssm_conv_b"])
    y, hst = _ssd_fwd(xact, proj, dtb, alog, dskip_x)
    yn = _ssm_post(y, proj, wt["ssm_norm"])
    if late_weights is not None:
        wt = {**wt, **late_weights(yn)}
    y_ssm = _matmul(yn, wt["w_ssm_out"], name="ssm_out")
    k_heads = _to_heads(proj[:, OFF_K:OFF_V])
    v_heads = _to_heads(proj[:, OFF_V:OFF_DT])
    attn = _swa_fwd(proj, k_heads, v_heads, sink_rows)
    y_attn = _matmul(attn, wt["w_attn_out"], name="attn_out")
    mixed = _mix_fwd(proj, y_ssm, y_attn)
    mix = _matmul(mixed, wt["w_mix_out"], name="mix_out")
    h1, hn2 = _postmix(h, mix, wt["norm_post_mix"], wt["norm_pre_ffn"])
    up = _matmul(hn2, wt["w_ffn_up"], name="ffn_up")
    u, act = _ffn_act(up, wt["ffn_conv_w"], wt["ffn_conv_b"])
    f = _matmul(act, wt["w_ffn_down"], name="ffn_down")
    df, dy, g_norm_post_ffn, loss_row = _final(h1, f, target, wt["norm_post_ffn"])

    grads["norm_post_ffn"] = g_norm_post_ffn
    sent = emit("w_ffn_down", _matmul(act, df, ta=True, name="dw_ffn_down"))
    dact = _matmul(df, wt["w_ffn_down"], tb=True, name="d_act", after=sent)
    dup, grads["ffn_conv_w"], grads["ffn_conv_b"] = _ffn_act_bwd(u, up, dact, wt["ffn_conv_w"])
    sent = emit("w_ffn_up", _matmul(hn2, dup, ta=True, name="dw_ffn_up"))
    dhn2 = _matmul(dup, wt["w_ffn_up"], tb=True, name="d_hn2", after=sent)
    dmix, dh, grads["norm_pre_ffn"], grads["norm_post_mix"] = _postmix_bwd(h1, dhn2, dy, mix, wt["norm_pre_ffn"], wt["norm_post_mix"])
    sent = emit("w_mix_out", _matmul(mixed, dmix, ta=True, name="dw_mix_out"))
    dmixed = _matmul(dmix, wt["w_mix_out"], tb=True, name="d_mixed", after=sent)
    dy_ssm, dy_attn, dglog = _mix_bwd(dmixed, proj, y_ssm, y_attn)
    sent = emit("w_ssm_out", _matmul(yn, dy_ssm, ta=True, name="dw_ssm_out"))
    dyn = _matmul(dy_ssm, wt["w_ssm_out"], tb=True, name="d_yn", after=sent)
    sent = emit("w_attn_out", _matmul(attn, dy_attn, ta=True, name="dw_attn_out"))
    dattn = _matmul(dy_attn, wt["w_attn_out"], tb=True, name="d_attn", after=sent)
    dy_ssd, dz, grads["ssm_norm"] = _ssm_post_bwd(y, proj, dyn, wt["ssm_norm"])
    dxs, dbm, dcm, ddt, dalog, ddtb, dd_x = _ssd_bwd(xact, proj, dtb, alog, dskip_x, dy_ssd, hst)
    grads["ssm_a_log"] = dalog[:, 0, :HEADS_PER_GROUP].reshape(1, SSM_HEADS)
    grads["ssm_dt_bias"] = ddtb[:, 0, :HEADS_PER_GROUP].reshape(1, SSM_HEADS)
    grads["ssm_d_skip"] = dd_x.reshape(SSM_HEADS, HEAD_P).sum(axis=1).reshape(1, SSM_HEADS)
    dxbc, grads["ssm_conv_w"], grads["ssm_conv_b"] = _ssm_conv_bwd(xc, proj, dxs, dbm, dcm, wt["ssm_conv_w"])
    dq, dk_heads, dv_heads, dsink = _swa_bwd(proj, k_heads, v_heads, sink_rows, attn, dattn)
    grads["attn_sinks"] = dsink[:, :ATTN_GROUP, 0].reshape(1, ATTN_HEADS)
    dproj = jnp.concatenate([dz, dglog, dxbc, dq.astype(BF16), _from_heads(dk_heads).astype(BF16),
                             _from_heads(dv_heads).astype(BF16), ddt], axis=1)
    sent = emit("w_in", _unpack_w_in(_matmul(hn, dproj, ta=True, name="dw_in")))
    dhn = _matmul(dproj, wp, tb=True, name="d_hn", after=sent)
    dh_all, grads["norm_pre_mix"] = _prenorm_bwd(h, dhn, dh, wt["norm_pre_mix"])
    grads["meta_tokens"] = dh_all[META_PAD:CHUNK]
    return loss_row[0, 0], dh_all[CHUNK:CHUNK + seq], grads


def _all_gather(shards):
    n = len(shards)

    def body(*refs):
        ins, outs = refs[:n], refs[n:2 * n]
        send_sems, recv_sems, local_sems = refs[2 * n:]
        x, y, c = lax.axis_index("x"), lax.axis_index("y"), lax.axis_index("c")
        me, sibling = (x, y, c), (x, y, 1 - c)
        chips = [(1 - x, y), (x, 1 - y), (1 - x, 1 - y)]

        def slot(a, dev):
            return outs[a].at[4 * dev[0] + 2 * dev[1] + dev[2]]

        def copy(k, a, block, to, src=None):
            return pltpu.make_async_remote_copy(
                src_ref=slot(a, block) if src is None else src, dst_ref=slot(a, block),
                send_sem=send_sems.at[k, a], recv_sem=recv_sems.at[k, a],
                device_id=to, device_id_type=pl.DeviceIdType.MESH)

        mine = [pltpu.make_async_copy(ins[a], slot(a, me), local_sems.at[a]) for a in range(n)]
        for cp in mine:
            cp.start()
        first = [copy(0, a, me, sibling, src=ins[a]) for a in range(n)]
        for j, chip in enumerate(chips):
            first += [copy(1 + j, a, me, (*chip, c), src=ins[a]) for a in range(n)]
        for cp in first:
            cp.start()
        passed = []
        for j, chip in enumerate(chips):
            for a in range(n):
                copy(1 + j, a, (*chip, c), me).wait_recv()
                fwd = copy(4 + j, a, (*chip, c), sibling)
                fwd.start()
                passed.append(fwd)
        for a in range(n):
            copy(0, a, sibling, me).wait_recv()
        for j, chip in enumerate(chips):
            for a in range(n):
                copy(4 + j, a, (*chip, 1 - c), me).wait_recv()
        for cp in first + passed:
            cp.wait_send()
        for cp in mine:
            cp.wait()

    hbm = pl.BlockSpec(memory_space=pl.ANY)
    return pl.pallas_call(
        body, in_specs=[hbm] * n, out_specs=[hbm] * n,
        out_shape=[_sds((N_DEV,) + s.shape, s.dtype) for s in shards],
        scratch_shapes=[pltpu.SemaphoreType.DMA((7, n)), pltpu.SemaphoreType.DMA((7, n)), pltpu.SemaphoreType.DMA((n,))],
        name="gather_weights")(*shards)


def _exchange_partials(parts):
    n = len(parts)

    def body(*refs):
        ins, outs = refs[:n], refs[n:2 * n]
        send_sems, recv_sems, local_sems = refs[2 * n:]
        x, y, c = lax.axis_index("x"), lax.axis_index("y"), lax.axis_index("c")
        my_id = 4 * x + 2 * y + c

        def peer(k):
            return (x ^ ((k >> 2) & 1), y ^ ((k >> 1) & 1), c ^ (k & 1))

        def copy(k, a):
            p = peer(k)
            p_id = 4 * p[0] + 2 * p[1] + p[2]
            return pltpu.make_async_remote_copy(
                src_ref=ins[a].at[p_id], dst_ref=outs[a].at[my_id],
                send_sem=send_sems.at[k - 1, a], recv_sem=recv_sems.at[k - 1, a],
                device_id=p, device_id_type=pl.DeviceIdType.MESH)

        def arrival(k, a):
            p = peer(k)
            p_id = 4 * p[0] + 2 * p[1] + p[2]
            return pltpu.make_async_remote_copy(
                src_ref=ins[a].at[p_id], dst_ref=outs[a].at[p_id],
                send_sem=send_sems.at[k - 1, a], recv_sem=recv_sems.at[k - 1, a],
                device_id=p, device_id_type=pl.DeviceIdType.MESH)

        mine = [pltpu.make_async_copy(ins[a].at[my_id], outs[a].at[my_id], local_sems.at[a]) for a in range(n)]
        for cp in mine:
            cp.start()
        sends = [copy(k, a) for k in range(1, N_DEV) for a in range(n)]
        for cp in sends:
            cp.start()
        for k in range(1, N_DEV):
            for a in range(n):
                arrival(k, a).wait_recv()
        for cp in sends:
            cp.wait_send()
        for cp in mine:
            cp.wait()

    hbm = pl.BlockSpec(memory_space=pl.ANY)
    return pl.pallas_call(
        body, in_specs=[hbm] * n, out_specs=[hbm] * n, out_shape=[_sds(p.shape, p.dtype) for p in parts],
        scratch_shapes=[pltpu.SemaphoreType.DMA((7, n)), pltpu.SemaphoreType.DMA((7, n)), pltpu.SemaphoreType.DMA((n,))],
        name="exchange_grads")(*parts)


def _peer_table():
    x, y, c = lax.axis_index("x"), lax.axis_index("y"), lax.axis_index("c")
    peers = []
    for k in range(N_DEV - 1):
        bits = k + 1
        p = (x ^ ((bits >> 2) & 1), y ^ ((bits >> 1) & 1), c ^ (bits & 1))
        peers.append((k, p, 4 * p[0] + 2 * p[1] + p[2]))
    return 4 * x + 2 * y + c, peers


_HBM = pl.BlockSpec(memory_space=pltpu.HBM)
_SEM = pl.BlockSpec(memory_space=pltpu.SEMAPHORE)
_EFFECT = pltpu.SideEffectType.DATAFLOW_SIDE_EFFECTING


def _push_copy(src, land, send_sems, recv_sems, a, k, p, src_slot, dst_slot):
    sem = a * (N_DEV - 1) + k
    return pltpu.make_async_remote_copy(
        src_ref=src[a] if src_slot is None else src[a].at[src_slot], dst_ref=land[a].at[dst_slot],
        send_sem=send_sems.at[sem], recv_sem=recv_sems.at[sem], device_id=p, device_id_type=pl.DeviceIdType.MESH)


def _push_start(srcs, scatter, name):
    n = len(srcs)
    lands = [lax.empty(s.shape if scatter else (N_DEV,) + s.shape, s.dtype) for s in srcs]

    def body(*refs):
        src, land = refs[:n], refs[n:2 * n]
        send_sems, recv_sems, token = refs[2 * n], refs[2 * n + 1], refs[-1]
        my_id, peers = _peer_table()
        for a in range(n):
            for k, p, p_id in peers:
                _push_copy(src, land, send_sems, recv_sems, a, k, p, p_id if scatter else None, my_id).start()
        token[...] = jnp.zeros_like(token)

    sems = pltpu.SemaphoreType.DMA(((N_DEV - 1) * n,))
    res = pl.pallas_call(
        body, name=name,
        out_shape=(sems, sems, *[pltpu.HBM(a.shape, a.dtype) for a in srcs + lands], _sds((8, 128), F32)),
        in_specs=[_HBM] * (2 * n), out_specs=(_SEM, _SEM, *[_HBM] * (2 * n), pl.BlockSpec(memory_space=pltpu.VMEM)),
        input_output_aliases={i: 2 + i for i in range(2 * n)},
        compiler_params=pltpu.CompilerParams(has_side_effects=_EFFECT),
    )(*[pltpu.with_memory_space_constraint(a, pltpu.HBM) for a in srcs + lands])
    return dict(send=res[0], recv=res[1], src=list(res[2:2 + n]), land=list(res[2 + n:2 + 2 * n]), token=res[-1],
                scatter=scatter)


def _push_wait(handle, after, name):
    n = len(handle["src"])
    scatter = handle["scatter"]

    def body(*refs):
        src, land = refs[:n], refs[n:2 * n]
        send_sems, recv_sems = refs[2 * n], refs[2 * n + 1]
        _, peers = _peer_table()
        for a in range(n):
            for k, p, p_id in peers:
                cp = _push_copy(src, land, send_sems, recv_sems, a, k, p, p_id if scatter else None, p_id)
                cp.wait_send()
                cp.wait_recv()

    arrays = handle["src"] + handle["land"]
    res = pl.pallas_call(
        body, name=name, out_shape=tuple(pltpu.HBM(a.shape, a.dtype) for a in arrays),
        in_specs=[_HBM] * (2 * n) + [_SEM, _SEM, pl.BlockSpec(memory_space=pl.ANY)], out_specs=tuple([_HBM] * (2 * n)),
        input_output_aliases={i: i for i in range(2 * n)},
        compiler_params=pltpu.CompilerParams(has_side_effects=_EFFECT),
    )(*arrays, handle["send"], handle["recv"], after)
    return list(res[:n]), list(res[n:])


def _adamw(parts, own, w, m, v, name):
    rows, cols = w.shape
    tr = _pick(rows, (256, 128, 176, 64, 32, 16, 8))

    def body(*refs):
        if own is None:
            p_ref, w_ref, m_ref, v_ref, g_ref, d_ref, nm_ref, nv_ref = refs
        else:
            p_ref, own_ref, w_ref, m_ref, v_ref, g_ref, d_ref, nm_ref, nv_ref = refs
            my_id = 4 * lax.axis_index("x") + 2 * lax.axis_index("y") + lax.axis_index("c")
            mine = own_ref[...].astype(F32)
        g = None
        for s in range(N_DEV):
            term = p_ref[s].astype(F32)
            if own is not None:
                term = jnp.where(my_id == s, mine, term)
            g = term if g is None else g + term
        m_new = ADAM_B1 * m_ref[...] + (1.0 - ADAM_B1) * g
        v_new = ADAM_B2 * v_ref[...] + (1.0 - ADAM_B2) * (g * g)
        m_hat = m_new / (1.0 - ADAM_B1 ** ADAM_STEP)
        v_hat = v_new / (1.0 - ADAM_B2 ** ADAM_STEP)
        g_ref[...] = g
        d_ref[...] = -ADAM_LR * (m_hat / (jnp.sqrt(v_hat) + ADAM_EPS) + ADAM_WD * w_ref[...])
        nm_ref[...] = m_new
        nv_ref[...] = v_new

    spec = _row(tr, cols)
    operands = (parts, w, m, v) if own is None else (parts, own, w, m, v)
    return pl.pallas_call(
        body, grid=(rows // tr,),
        in_specs=[pl.BlockSpec((N_DEV, tr, cols), lambda i: (0, i, 0))] + [spec] * (len(operands) - 1),
        out_specs=[spec] * 4, out_shape=[_sds((rows, cols), F32)] * 4,
        name=name, compiler_params=_params(1))(*operands)


SMALL_REPLICATED = (("norm_pre_mix", 1024), ("ssm_conv_b", 3072), ("ssm_dt_bias", 32), ("ssm_a_log", 32),
                    ("ssm_d_skip", 32), ("ssm_norm", 2048), ("attn_sinks", 16), ("norm_post_mix", 1024),
                    ("norm_pre_ffn", 1024), ("ffn_conv_b", 5632), ("norm_post_ffn", 1024))
SMALL_SHARDED = (("meta_tokens", (N_META, D_MODEL // N_DEV)), ("ssm_conv_w", (SSM_CONV, CONV_DIM // N_DEV)),
                 ("ffn_conv_w", (FFN_CONV, 2 * FFN_DIM // N_DEV)))
BIG = (("w_in", (D_MODEL, N_IN // N_DEV), 1), ("w_ssm_out", (D_INNER // N_DEV, D_MODEL), 0),
       ("w_attn_out", (D_MODEL // N_DEV, D_MODEL), 0), ("w_mix_out", (D_MODEL // N_DEV, D_MODEL), 0),
       ("w_ffn_up", (D_MODEL, 2 * FFN_DIM // N_DEV), 1), ("w_ffn_down", (FFN_DIM // N_DEV, D_MODEL), 0))


def _rows_of(size):
    return -(-size // 128)


def _as_rows(flat):
    size = flat.shape[-1]
    rows = _rows_of(size)
    flat = jnp.pad(flat, [(0, 0)] * (flat.ndim - 1) + [(0, rows * 128 - size)])
    return flat.reshape(flat.shape[:-1] + (rows, 128))


def _pack_small(rep, sharded):
    pieces = [_as_rows(rep[name].reshape(-1)) for name, _ in SMALL_REPLICATED]
    pieces += [_as_rows(sharded[name].reshape(-1)) for name, _ in SMALL_SHARDED]
    packed = jnp.concatenate(pieces, axis=0)
    return jnp.pad(packed, ((0, -packed.shape[0] % 8), (0, 0)))


def _unpack_small(packed):
    out, row = {}, 0
    for name, size in SMALL_REPLICATED:
        out[name] = packed[row:row + _rows_of(size)].reshape(-1)[:size].reshape(1, size)
        row += _rows_of(size)
    for name, (r, c) in SMALL_SHARDED:
        out[name] = packed[row:row + _rows_of(r * c)].reshape(-1)[:r * c].reshape(r, c)
        row += _rows_of(r * c)
    return out


def _shard_major(g, shape, axis):
    r, c = shape
    if axis == 0:
        return g.reshape(N_DEV, r, c)
    return g.reshape(r, N_DEV, c).transpose(1, 0, 2)


def kernel(x, meta_tokens, norm_pre_mix, w_in, ssm_conv_w, ssm_conv_b, ssm_dt_bias, ssm_a_log, ssm_d_skip, ssm_norm, w_ssm_out, attn_sinks, w_attn_out, w_mix_out, norm_post_mix, norm_pre_ffn, w_ffn_up, ffn_conv_w, ffn_conv_b, w_ffn_down, norm_post_ffn, loss_target, m_meta_tokens, m_norm_pre_mix, m_w_in, m_ssm_conv_w, m_ssm_conv_b, m_ssm_dt_bias, m_ssm_a_log, m_ssm_d_skip, m_ssm_norm, m_w_ssm_out, m_attn_sinks, m_w_attn_out, m_w_mix_out, m_norm_post_mix, m_norm_pre_ffn, m_w_ffn_up, m_ffn_conv_w, m_ffn_conv_b, m_w_ffn_down, m_norm_post_ffn, v_meta_tokens, v_norm_pre_mix, v_w_in, v_ssm_conv_w, v_ssm_conv_b, v_ssm_dt_bias, v_ssm_a_log, v_ssm_d_skip, v_ssm_norm, v_w_ssm_out, v_attn_sinks, v_w_attn_out, v_w_mix_out, v_norm_post_mix, v_norm_pre_ffn, v_w_ffn_up, v_ffn_conv_w, v_ffn_conv_b, v_w_ffn_down, v_norm_post_ffn):
    names = ("meta_tokens", "norm_pre_mix", "w_in", "ssm_conv_w", "ssm_conv_b", "ssm_dt_bias", "ssm_a_log", "ssm_d_skip",
             "ssm_norm", "w_ssm_out", "attn_sinks", "w_attn_out", "w_mix_out", "norm_post_mix", "norm_pre_ffn", "w_ffn_up",
             "ffn_conv_w", "ffn_conv_b", "w_ffn_down", "norm_post_ffn")
    w_loc = dict(zip(names, (meta_tokens, norm_pre_mix, w_in, ssm_conv_w, ssm_conv_b, ssm_dt_bias, ssm_a_log, ssm_d_skip,
                             ssm_norm, w_ssm_out, attn_sinks, w_attn_out, w_mix_out, norm_post_mix, norm_pre_ffn, w_ffn_up,
                             ffn_conv_w, ffn_conv_b, w_ffn_down, norm_post_ffn)))
    m_loc = dict(zip(names, (m_meta_tokens, m_norm_pre_mix, m_w_in, m_ssm_conv_w, m_ssm_conv_b, m_ssm_dt_bias, m_ssm_a_log,
                             m_ssm_d_skip, m_ssm_norm, m_w_ssm_out, m_attn_sinks, m_w_attn_out, m_w_mix_out, m_norm_post_mix,
                             m_norm_pre_ffn, m_w_ffn_up, m_ffn_conv_w, m_ffn_conv_b, m_w_ffn_down, m_norm_post_ffn)))
    v_loc = dict(zip(names, (v_meta_tokens, v_norm_pre_mix, v_w_in, v_ssm_conv_w, v_ssm_conv_b, v_ssm_dt_bias, v_ssm_a_log,
                             v_ssm_d_skip, v_ssm_norm, v_w_ssm_out, v_attn_sinks, v_w_attn_out, v_w_mix_out, v_norm_post_mix,
                             v_norm_pre_ffn, v_w_ffn_up, v_ffn_conv_w, v_ffn_conv_b, v_w_ffn_down, v_norm_post_ffn)))

    def local2d(d, name):
        a = d[name]
        return a if name == "meta_tokens" else a.reshape(a.shape[1:])

    my_id = 4 * lax.axis_index("x") + 2 * lax.axis_index("y") + lax.axis_index("c")
    big = {name: (shape, axis) for name, shape, axis in BIG}

    def whole(name, g):
        (r, c), axis = big[name]
        return g.reshape(N_DEV * r, c) if axis == 0 else g.transpose(1, 0, 2).reshape(r, N_DEV * c)

    small_shard_pack = jnp.concatenate([_as_rows(local2d(w_loc, name).reshape(-1)) for name, _ in SMALL_SHARDED], axis=0)
    small_shard_pack = jnp.pad(small_shard_pack, ((0, -small_shard_pack.shape[0] % 8), (0, 0)))
    first = _all_gather([local2d(w_loc, "w_in").astype(BF16), small_shard_pack])
    rest_names = [name for name, _, _ in BIG if name != "w_in"]
    rest = [local2d(w_loc, name).astype(BF16) for name in rest_names]
    rest, first = lax.optimization_barrier((rest, first))
    rest_handle = _push_start(rest, False, "gather_rest_start")
    wt = {"w_in": whole("w_in", first[0])}
    row = 0
    for name, (r, c) in SMALL_SHARDED:
        blocks = first[1][:, row:row + _rows_of(r * c)].reshape(N_DEV, -1)[:, :r * c].reshape(N_DEV, r, c)
        wt[name] = blocks.transpose(1, 0, 2).reshape(r, N_DEV * c)
        row += _rows_of(r * c)
    for name, size in SMALL_REPLICATED:
        wt[name] = w_loc[name].reshape(1, size)

    def late_weights(after):
        own, landed = _push_wait(rest_handle, after, "gather_rest_wait")
        out = {}
        for name, mine, land in zip(rest_names, own, landed):
            out[name] = whole(name, lax.dynamic_update_index_in_dim(land, mine, my_id, 0))
        return out

    sent = {}

    def on_grad(name, g):
        shape, axis = big[name]
        sent[name] = _push_start([_shard_major(g, shape, axis).astype(BF16)], True, "send_" + name)
        return sent[name]["token"]

    loss_part, grad_x, grads = _local_step(x[0], loss_target[0], wt, late_weights, on_grad, rest_handle["token"])
    loss = lax.psum(loss_part, AXES)

    small_parts = []
    for name, (r, c) in SMALL_SHARDED:
        small_parts.append(_as_rows(_shard_major(grads[name], (r, c), 1).reshape(N_DEV, r * c)))
    rep_rows = jnp.concatenate([_as_rows(grads[name].reshape(-1)) for name, _ in SMALL_REPLICATED], axis=0)
    small_send = jnp.concatenate([jnp.broadcast_to(rep_rows[None], (N_DEV,) + rep_rows.shape)] + small_parts, axis=1)
    small_send = jnp.pad(small_send, ((0, 0), (0, -small_send.shape[1] % 8), (0, 0)))
    small_received = _exchange_partials([small_send])[0]

    def small_pack(d):
        return _pack_small({name: d[name] for name, _ in SMALL_REPLICATED}, {name: local2d(d, name) for name, _ in SMALL_SHARDED})

    grad_w, delta_w, new_m, new_v = {}, {}, {}, {}
    outs = _adamw(small_received, None, small_pack(w_loc), small_pack(m_loc), small_pack(v_loc), "adamw_small")
    after = outs[0]
    for name, handle in sent.items():
        src, landed = _push_wait(handle, after, "arrived_" + name)
        own = lax.dynamic_index_in_dim(src[0], my_id, 0, keepdims=False)
        g, d, nm, nv = _adamw(landed[0], own, local2d(w_loc, name), local2d(m_loc, name), local2d(v_loc, name), "adamw_" + name)
        after = g
        full = (1,) + big[name][0]
        grad_w[name], delta_w[name], new_m[name], new_v[name] = g.reshape(full), d.reshape(full), nm.reshape(full), nv.reshape(full)
    for dst, packed in zip((grad_w, delta_w, new_m, new_v), outs):
        for name, a in _unpack_small(packed).items():
            dst[name] = a.reshape(w_loc[name].shape)

    return (loss, grad_x[None], *[grad_w[n] for n in names], *[delta_w[n] for n in names],
            *[new_m[n] for n in names], *[new_v[n] for n in names])
```

```python
import jax
import jax.numpy as jnp
from jax import lax
from jax.experimental import pallas as pl
from jax.experimental.pallas import tpu as pltpu

F32 = jnp.float32
BF16 = jnp.bfloat16
HIGHEST = lax.Precision.HIGHEST

D_MODEL = 1024
N_META = 16
CHUNK = 128
META_PAD = CHUNK - N_META
D_INNER = 2048
HEAD_P = 64
SSM_HEADS = 32
SSM_GROUPS = 4
HEADS_PER_GROUP = SSM_HEADS // SSM_GROUPS
GROUP_W = HEADS_PER_GROUP * HEAD_P
D_STATE = 128
SSM_CONV = 4
CONV_DIM = D_INNER + 2 * SSM_GROUPS * D_STATE
ATTN_HEADS = 16
KV_HEADS = 4
ATTN_GROUP = ATTN_HEADS // KV_HEADS
DH = 64
KV_W = KV_HEADS * DH
FFN_DIM = 2816
FFN_CONV = 3
EPS = 1e-6
NEG = -1e30
N_DEV = 8
AXES = ("x", "y", "c")

OFF_Z, OFF_GATE, OFF_XBC, OFF_Q, OFF_K, OFF_V, OFF_DT = 0, 2048, 4096, 7168, 8192, 8448, 8704
N_INP = OFF_DT + SSM_GROUPS * 128
CUT_Z, CUT_XBC, CUT_DT, CUT_Q, CUT_K, CUT_V, CUT_G = 0, 2048, 5120, 5152, 6176, 6432, 6688
N_IN = 8736

ADAM_LR, ADAM_B1, ADAM_B2, ADAM_EPS, ADAM_WD, ADAM_STEP = 0.001, 0.9, 0.999, 1e-08, 0.01, 10

VMEM_LIMIT = 56 * 1024 * 1024


def _params(n_grid):
    return pltpu.CompilerParams(dimension_semantics=("arbitrary",) * n_grid, vmem_limit_bytes=VMEM_LIMIT)


def _sds(shape, dtype):
    return jax.ShapeDtypeStruct(shape, dtype)


def _pick(n, prefs):
    for c in prefs:
        if n % c == 0:
            return c
    raise ValueError(f"no tile of {prefs} divides {n}")


def _row(tr, width, cb=0):
    return pl.BlockSpec((tr, width), lambda i: (i, cb))


def _row_rev(tr, width, nt, cb=0):
    return pl.BlockSpec((tr, width), lambda i: (nt - 1 - i, cb))


def _full(shape):
    return pl.BlockSpec(shape, lambda *_: (0,) * len(shape))


def _sigmoid(x):
    return 1.0 / (1.0 + jnp.exp(-x))


def _softplus(x):
    return jnp.maximum(x, 0.0) + jnp.log(1.0 + jnp.exp(-jnp.abs(x)))


def _rms(x):
    return lax.rsqrt(jnp.mean(x * x, axis=-1, keepdims=True) + EPS)


def _rms_bwd(x, r, w, dy):
    xh = x * r
    g = dy * w
    dx = r * (g - xh * jnp.mean(g * xh, axis=-1, keepdims=True))
    return dx, jnp.sum(dy * xh, axis=0, keepdims=True)


def _row_ids(shape, tile_index, tr):
    return tile_index * tr + lax.broadcasted_iota(jnp.int32, shape, 0)


def _shift_down(cur, prev, s):
    if s == 0:
        return cur
    row = lax.broadcasted_iota(jnp.int32, cur.shape, 0)
    return jnp.where(row < s, pltpu.roll(prev, s, 0), pltpu.roll(cur, s, 0))


def _shift_up(cur, nxt, s):
    if s == 0:
        return cur
    n = cur.shape[0]
    row = lax.broadcasted_iota(jnp.int32, cur.shape, 0)
    return jnp.where(row >= n - s, pltpu.roll(nxt, n - s, 0), pltpu.roll(cur, n - s, 0))


def _matmul(a, b, *, ta=False, tb=False, out_dtype=F32, name, after=None):
    if ta:
        k_dim, m_dim = a.shape
    else:
        m_dim, k_dim = a.shape
    n_dim = b.shape[0] if tb else b.shape[1]
    tm = _pick(m_dim, (1408, 1024, 768, 512, 384, 256, 128))
    tn = _pick(n_dim, (1024, 1408, 768, 512, 384, 256, 128))
    tk = k_dim if (not ta and k_dim <= 2816) else _pick(k_dim, (1408, 1024, 768, 512, 384, 256, 128))
    nk = k_dim // tk
    dims = (((0 if ta else 1,), (1 if tb else 0,)), ((), ()))

    def body(a_ref, b_ref, *rest):
        o_ref = rest[-1]
        r = lax.dot_general(a_ref[...].astype(BF16), b_ref[...].astype(BF16), dims, preferred_element_type=F32)
        if nk == 1:
            o_ref[...] = r.astype(o_ref.dtype)
        else:
            k = pl.program_id(2)

            @pl.when(k == 0)
            def _():
                o_ref[...] = r

            @pl.when(k > 0)
            def _():
                o_ref[...] += r

    a_spec = pl.BlockSpec((tk, tm), lambda i, j, k: (k, i)) if ta else pl.BlockSpec((tm, tk), lambda i, j, k: (i, k))
    b_spec = pl.BlockSpec((tn, tk), lambda i, j, k: (j, k)) if tb else pl.BlockSpec((tk, tn), lambda i, j, k: (k, j))
    if nk > 1:
        assert out_dtype == F32
    extra_specs, extra = ([], ()) if after is None else ([pl.BlockSpec(memory_space=pl.ANY)], (after,))
    return pl.pallas_call(
        body, grid=(m_dim // tm, n_dim // tn, nk), in_specs=[a_spec, b_spec] + extra_specs,
        out_specs=pl.BlockSpec((tm, tn), lambda i, j, k: (i, j)), out_shape=_sds((m_dim, n_dim), out_dtype),
        name=name, compiler_params=_params(3))(a, b, *extra)


def _prenorm(h, w):
    t_rows = h.shape[0]
    tr = _pick(t_rows, (384, 128))

    def body(h_ref, w_ref, o_ref):
        x = h_ref[...]
        o_ref[...] = (x * _rms(x) * w_ref[...]).astype(BF16)

    return pl.pallas_call(body, grid=(t_rows // tr,), in_specs=[_row(tr, D_MODEL), _full((1, D_MODEL))],
                          out_specs=_row(tr, D_MODEL), out_shape=_sds((t_rows, D_MODEL), BF16),
                          name="prenorm", compiler_params=_params(1))(h, w)


def _xbc_specs(tr, rev_nt=None):
    cbs = [OFF_XBC // 1024 + j for j in range(CONV_DIM // 1024)]
    if rev_nt is None:
        return [_row(tr, 1024, cb) for cb in cbs]
    return [_row_rev(tr, 1024, rev_nt, cb) for cb in cbs]


def _ssm_conv_fwd(proj, conv_w, conv_b):
    t_rows = proj.shape[0]
    tr = CHUNK

    def body(x0, x1, x2, w_ref, b_ref, xc_ref, xa_ref, prev):
        @pl.when(pl.program_id(0) == 0)
        def _():
            prev[...] = jnp.zeros_like(prev)

        x = jnp.concatenate([x0[...], x1[...], x2[...]], axis=1)
        p = prev[...]
        acc = b_ref[...] + w_ref[SSM_CONV - 1:SSM_CONV, :] * x
        for s in range(1, SSM_CONV):
            acc = acc + w_ref[SSM_CONV - 1 - s:SSM_CONV - s, :] * _shift_down(x, p, s)
        prev[...] = x
        xc_ref[...] = acc
        xa_ref[...] = acc * _sigmoid(acc)

    return pl.pallas_call(
        body, grid=(t_rows // tr,),
        in_specs=_xbc_specs(tr) + [_full((SSM_CONV, CONV_DIM)), _full((1, CONV_DIM))],
        out_specs=[_row(tr, CONV_DIM), _row(tr, CONV_DIM)],
        out_shape=[_sds((t_rows, CONV_DIM), F32), _sds((t_rows, CONV_DIM), F32)],
        scratch_shapes=[pltpu.VMEM((tr, CONV_DIM), F32)],
        name="ssm_conv_fwd", compiler_params=_params(1))(proj, proj, proj, conv_w, conv_b)


def _ssm_post(y, proj, w):
    t_rows = y.shape[0]
    tr = CHUNK

    def body(y_ref, z_ref, w_ref, o_ref):
        z = z_ref[...]
        yz = y_ref[...] * z * _sigmoid(z)
        o_ref[...] = (yz * _rms(yz) * w_ref[...]).astype(BF16)

    return pl.pallas_call(body, grid=(t_rows // tr,),
                          in_specs=[_row(tr, D_INNER), _row(tr, D_INNER, OFF_Z // D_INNER), _full((1, D_INNER))],
                          out_specs=_row(tr, D_INNER), out_shape=_sds((t_rows, D_INNER), BF16),
                          name="ssm_post", compiler_params=_params(1))(y, proj, w)


def _mix_fwd(proj, y_ssm, y_attn):
    t_rows = y_ssm.shape[0]
    tr = _pick(t_rows, (384, 128))

    def body(g_ref, ys_ref, ya_ref, o_ref):
        g = _sigmoid(g_ref[...])
        o_ref[...] = (g[:, :D_MODEL] * ys_ref[...] + g[:, D_MODEL:] * ya_ref[...]).astype(BF16)

    return pl.pallas_call(body, grid=(t_rows // tr,),
                          in_specs=[_row(tr, 2 * D_MODEL, OFF_GATE // (2 * D_MODEL)), _row(tr, D_MODEL), _row(tr, D_MODEL)],
                          out_specs=_row(tr, D_MODEL), out_shape=_sds((t_rows, D_MODEL), BF16),
                          name="mix_fwd", compiler_params=_params(1))(proj, y_ssm, y_attn)


def _postmix(h, mix, w_post, w_pre):
    t_rows = h.shape[0]
    tr = CHUNK

    def body(h_ref, m_ref, wp_ref, wf_ref, h1_ref, hn_ref):
        m = m_ref[...]
        h1 = h_ref[...] + m * _rms(m) * wp_ref[...]
        h1 = jnp.where(_row_ids(h1.shape, pl.program_id(0), tr) >= META_PAD, h1, 0.0)
        h1_ref[...] = h1
        hn_ref[...] = (h1 * _rms(h1) * wf_ref[...]).astype(BF16)

    return pl.pallas_call(body, grid=(t_rows // tr,),
                          in_specs=[_row(tr, D_MODEL), _row(tr, D_MODEL), _full((1, D_MODEL)), _full((1, D_MODEL))],
                          out_specs=[_row(tr, D_MODEL), _row(tr, D_MODEL)],
                          out_shape=[_sds((t_rows, D_MODEL), F32), _sds((t_rows, D_MODEL), BF16)],
                          name="postmix", compiler_params=_params(1))(h, mix, w_post, w_pre)


def _ffn_act(up, conv_w, conv_b):
    t_rows = up.shape[0]
    tr = CHUNK
    width = 2 * FFN_DIM

    def body(up_ref, w_ref, b_ref, u_ref, act_ref, prev):
        @pl.when(pl.program_id(0) == 0)
        def _():
            prev[...] = jnp.zeros_like(prev)

        x = up_ref[...]
        p = prev[...]
        u = b_ref[...] + w_ref[FFN_CONV - 1:FFN_CONV, :] * x
        for s in range(1, FFN_CONV):
            u = u + w_ref[FFN_CONV - 1 - s:FFN_CONV - s, :] * _shift_down(x, p, s)
        prev[...] = x
        u_ref[...] = u
        a = u[:, :FFN_DIM]
        act_ref[...] = (a * _sigmoid(a) * u[:, FFN_DIM:]).astype(BF16)

    return pl.pallas_call(
        body, grid=(t_rows // tr,), in_specs=[_row(tr, width), _full((FFN_CONV, width)), _full((1, width))],
        out_specs=[_row(tr, width), _row(tr, FFN_DIM)],
        out_shape=[_sds((t_rows, width), F32), _sds((t_rows, FFN_DIM), BF16)],
        scratch_shapes=[pltpu.VMEM((tr, width), F32)],
        name="ffn_act", compiler_params=_params(1))(up, conv_w, conv_b)


def _final(h1, f, target, w):
    t_rows = h1.shape[0]
    tr = CHUNK

    def body(h1_ref, f_ref, t_ref, w_ref, df_ref, dy_ref, dw_ref, loss_ref):
        i = pl.program_id(0)

        @pl.when(i == 0)
        def _():
            dw_ref[...] = jnp.zeros_like(dw_ref)
            loss_ref[...] = jnp.zeros_like(loss_ref)

        f_val = f_ref[...]
        r = _rms(f_val)
        wv = w_ref[...]
        h2 = h1_ref[...] + f_val * r * wv
        diff = jnp.where(i >= 1, h2 - t_ref[...], 0.0)
        loss_ref[...] += 0.5 * jnp.sum(diff * diff) * (1.0 / D_MODEL)
        dy = diff * (1.0 / D_MODEL)
        dy_ref[...] = dy
        df, dw = _rms_bwd(f_val, r, wv, dy)
        df_ref[...] = df.astype(BF16)
        dw_ref[...] += dw

    tgt_spec = pl.BlockSpec((tr, D_MODEL), lambda i: (jnp.maximum(i - 1, 0), 0))
    return pl.pallas_call(
        body, grid=(t_rows // tr,),
        in_specs=[_row(tr, D_MODEL), _row(tr, D_MODEL), tgt_spec, _full((1, D_MODEL))],
        out_specs=[_row(tr, D_MODEL), _row(tr, D_MODEL), _full((1, D_MODEL)), _full((1, 128))],
        out_shape=[_sds((t_rows, D_MODEL), BF16), _sds((t_rows, D_MODEL), F32), _sds((1, D_MODEL), F32), _sds((1, 128), F32)],
        name="final", compiler_params=_params(1))(h1, f, target, w)


def _ffn_act_bwd(u, up, dact, conv_w):
    t_rows = u.shape[0]
    tr = CHUNK
    nt = t_rows // tr
    width = 2 * FFN_DIM

    def body(u_ref, up_ref, da_ref, w_ref, dup_ref, dw_ref, db_ref, nxt):
        @pl.when(pl.program_id(0) == 0)
        def _():
            nxt[...] = jnp.zeros_like(nxt)
            dw_ref[...] = jnp.zeros_like(dw_ref)
            db_ref[...] = jnp.zeros_like(db_ref)

        u_val = u_ref[...]
        a, g = u_val[:, :FFN_DIM], u_val[:, FFN_DIM:]
        d = da_ref[...]
        s = _sigmoid(a)
        du = jnp.concatenate([d * g * s * (1.0 + a * (1.0 - s)), d * a * s], axis=1)
        n = nxt[...]
        x = up_ref[...]
        dup = jnp.zeros_like(du)
        for sh in range(FFN_CONV):
            k = FFN_CONV - 1 - sh
            moved = _shift_up(du, n, sh)
            dup = dup + w_ref[k:k + 1, :] * moved
            dw_ref[k:k + 1, :] += jnp.sum(moved * x, axis=0, keepdims=True)
        db_ref[...] += jnp.sum(du, axis=0, keepdims=True)
        nxt[...] = du
        dup_ref[...] = dup.astype(BF16)

    return pl.pallas_call(
        body, grid=(nt,),
        in_specs=[_row_rev(tr, width, nt), _row_rev(tr, width, nt), _row_rev(tr, FFN_DIM, nt), _full((FFN_CONV, width))],
        out_specs=[_row_rev(tr, width, nt), _full((FFN_CONV, width)), _full((1, width))],
        out_shape=[_sds((t_rows, width), BF16), _sds((FFN_CONV, width), F32), _sds((1, width), F32)],
        scratch_shapes=[pltpu.VMEM((tr, width), F32)],
        name="ffn_act_bwd", compiler_params=_params(1))(u, up, dact, conv_w)


def _postmix_bwd(h1, dhn2, dy, mix, w_pre, w_post):
    t_rows = h1.shape[0]
    tr = CHUNK

    def body(h1_ref, dhn_ref, dy_ref, m_ref, wf_ref, wp_ref, dmix_ref, dh_ref, dwf_ref, dwp_ref):
        @pl.when(pl.program_id(0) == 0)
        def _():
            dwf_ref[...] = jnp.zeros_like(dwf_ref)
            dwp_ref[...] = jnp.zeros_like(dwp_ref)

        h1v = h1_ref[...]
        dx, dwf = _rms_bwd(h1v, _rms(h1v), wf_ref[...], dhn_ref[...])
        dwf_ref[...] += dwf
        dh1 = dy_ref[...] + dx
        dh1 = jnp.where(_row_ids(dh1.shape, pl.program_id(0), tr) >= META_PAD, dh1, 0.0)
        dh_ref[...] = dh1
        m = m_ref[...]
        dmix, dwp = _rms_bwd(m, _rms(m), wp_ref[...], dh1)
        dwp_ref[...] += dwp
        dmix_ref[...] = dmix.astype(BF16)

    return pl.pallas_call(
        body, grid=(t_rows // tr,),
        in_specs=[_row(tr, D_MODEL)] * 4 + [_full((1, D_MODEL))] * 2,
        out_specs=[_row(tr, D_MODEL), _row(tr, D_MODEL), _full((1, D_MODEL)), _full((1, D_MODEL))],
        out_shape=[_sds((t_rows, D_MODEL), BF16), _sds((t_rows, D_MODEL), F32), _sds((1, D_MODEL), F32), _sds((1, D_MODEL), F32)],
        name="postmix_bwd", compiler_params=_params(1))(h1, dhn2, dy, mix, w_pre, w_post)


def _mix_bwd(dmixed, proj, y_ssm, y_attn):
    t_rows = dmixed.shape[0]
    tr = _pick(t_rows, (384, 128))

    def body(d_ref, g_ref, ys_ref, ya_ref, dys_ref, dya_ref, dg_ref):
        d = d_ref[...]
        g = _sigmoid(g_ref[...])
        g1, g2 = g[:, :D_MODEL], g[:, D_MODEL:]
        dys_ref[...] = (d * g1).astype(BF16)
        dya_ref[...] = (d * g2).astype(BF16)
        dg_ref[...] = jnp.concatenate([d * ys_ref[...] * g1 * (1.0 - g1), d * ya_ref[...] * g2 * (1.0 - g2)],
                                      axis=1).astype(BF16)

    return pl.pallas_call(
        body, grid=(t_rows // tr,),
        in_specs=[_row(tr, D_MODEL), _row(tr, 2 * D_MODEL, OFF_GATE // (2 * D_MODEL)), _row(tr, D_MODEL), _row(tr, D_MODEL)],
        out_specs=[_row(tr, D_MODEL), _row(tr, D_MODEL), _row(tr, 2 * D_MODEL)],
        out_shape=[_sds((t_rows, D_MODEL), BF16), _sds((t_rows, D_MODEL), BF16), _sds((t_rows, 2 * D_MODEL), BF16)],
        name="mix_bwd", compiler_params=_params(1))(dmixed, proj, y_ssm, y_attn)


def _ssm_post_bwd(y, proj, dyn, w):
    t_rows = y.shape[0]
    tr = CHUNK

    def body(y_ref, z_ref, d_ref, w_ref, dy_ref, dz_ref, dw_ref):
        @pl.when(pl.program_id(0) == 0)
        def _():
            dw_ref[...] = jnp.zeros_like(dw_ref)

        yv, z = y_ref[...], z_ref[...]
        sz = _sigmoid(z)
        silu = z * sz
        yz = yv * silu
        dyz, dw = _rms_bwd(yz, _rms(yz), w_ref[...], d_ref[...])
        dw_ref[...] += dw
        dy_ref[...] = dyz * silu
        dz_ref[...] = (dyz * yv * sz * (1.0 + z * (1.0 - sz))).astype(BF16)

    return pl.pallas_call(
        body, grid=(t_rows // tr,),
        in_specs=[_row(tr, D_INNER), _row(tr, D_INNER, OFF_Z // D_INNER), _row(tr, D_INNER), _full((1, D_INNER))],
        out_specs=[_row(tr, D_INNER), _row(tr, D_INNER), _full((1, D_INNER))],
        out_shape=[_sds((t_rows, D_INNER), F32), _sds((t_rows, D_INNER), BF16), _sds((1, D_INNER), F32)],
        name="ssm_post_bwd", compiler_params=_params(1))(y, proj, dyn, w)


def _ssm_conv_bwd(xc, proj, dxs, dbm, dcm, conv_w):
    t_rows = xc.shape[0]
    tr = CHUNK
    nt = t_rows // tr
    bc_w = SSM_GROUPS * D_STATE

    def body(xc_ref, x0, x1, x2, dxs_ref, db_ref, dc_ref, w_ref, dx_ref, dw_ref, dbias_ref, nxt):
        @pl.when(pl.program_id(0) == 0)
        def _():
            nxt[...] = jnp.zeros_like(nxt)
            dw_ref[...] = jnp.zeros_like(dw_ref)
            dbias_ref[...] = jnp.zeros_like(dbias_ref)

        c = xc_ref[...]
        s = _sigmoid(c)
        dact = jnp.concatenate([dxs_ref[...], db_ref[...], dc_ref[...]], axis=1)
        dpre = dact * s * (1.0 + c * (1.0 - s))
        x = jnp.concatenate([x0[...], x1[...], x2[...]], axis=1)
        n = nxt[...]
        dx = jnp.zeros_like(dpre)
        for sh in range(SSM_CONV):
            k = SSM_CONV - 1 - sh
            moved = _shift_up(dpre, n, sh)
            dx = dx + w_ref[k:k + 1, :] * moved
            dw_ref[k:k + 1, :] += jnp.sum(moved * x, axis=0, keepdims=True)
        dbias_ref[...] += jnp.sum(dpre, axis=0, keepdims=True)
        nxt[...] = dpre
        dx_ref[...] = dx.astype(BF16)

    return pl.pallas_call(
        body, grid=(nt,),
        in_specs=[_row_rev(tr, CONV_DIM, nt)] + _xbc_specs(tr, nt)
        + [_row_rev(tr, D_INNER, nt), _row_rev(tr, bc_w, nt), _row_rev(tr, bc_w, nt), _full((SSM_CONV, CONV_DIM))],
        out_specs=[_row_rev(tr, CONV_DIM, nt), _full((SSM_CONV, CONV_DIM)), _full((1, CONV_DIM))],
        out_shape=[_sds((t_rows, CONV_DIM), BF16), _sds((SSM_CONV, CONV_DIM), F32), _sds((1, CONV_DIM), F32)],
        scratch_shapes=[pltpu.VMEM((tr, CONV_DIM), F32)],
        name="ssm_conv_bwd", compiler_params=_params(1))(xc, proj, proj, proj, dxs, dbm, dcm, conv_w)


def _prenorm_bwd(h, dhn, dh, w):
    t_rows = h.shape[0]
    tr = CHUNK

    def body(h_ref, d_ref, r_ref, w_ref, o_ref, dw_ref):
        @pl.when(pl.program_id(0) == 0)
        def _():
            dw_ref[...] = jnp.zeros_like(dw_ref)

        x = h_ref[...]
        dx, dw = _rms_bwd(x, _rms(x), w_ref[...], d_ref[...])
        dw_ref[...] += dw
        o_ref[...] = r_ref[...] + dx

    return pl.pallas_call(
        body, grid=(t_rows // tr,), in_specs=[_row(tr, D_MODEL)] * 3 + [_full((1, D_MODEL))],
        out_specs=[_row(tr, D_MODEL), _full((1, D_MODEL))],
        out_shape=[_sds((t_rows, D_MODEL), F32), _sds((1, D_MODEL), F32)],
        name="prenorm_bwd", compiler_params=_params(1))(h, dhn, dh, w)


def _dot01(x, m01, x_left, parts):
    acc, rest = None, x
    for i in range(parts):
        piece = rest.astype(BF16)
        term = (jnp.dot(piece, m01, preferred_element_type=F32) if x_left
                else jnp.dot(m01, piece, preferred_element_type=F32))
        acc = term if acc is None else acc + term
        if i + 1 < parts:
            rest = rest - piece.astype(F32)
    return acc


def _ssd_common(dtr_ref, dtb_ref, alog_ref, chunk_index):
    rows = lax.broadcasted_iota(jnp.int32, (CHUNK, CHUNK), 0)
    cols = lax.broadcasted_iota(jnp.int32, (CHUNK, CHUNK), 1)
    low = rows >= cols
    raw = dtr_ref[...] + dtb_ref[0]
    live = _row_ids(raw.shape, chunk_index, CHUNK) >= META_PAD
    dt = jnp.where(live, _softplus(raw), 0.0)
    a_head = -jnp.exp(alog_ref[0])
    cs = _dot01(dt * a_head, low.astype(BF16), False, 3)
    grow = jnp.exp(cs)
    fade = jnp.exp(cs[CHUNK - 1:CHUNK, :] - cs)
    expand = (lax.broadcasted_iota(jnp.int32, (CHUNK, GROUP_W), 1) // HEAD_P
              == lax.broadcasted_iota(jnp.int32, (CHUNK, GROUP_W), 0)).astype(BF16)
    fold = (lax.broadcasted_iota(jnp.int32, (GROUP_W, CHUNK), 0) // HEAD_P
            == lax.broadcasted_iota(jnp.int32, (GROUP_W, CHUNK), 1)).astype(BF16)
    return dict(low=low, triu=(rows <= cols).astype(BF16), raw=raw, live=live, dt=dt, a_head=a_head, cs=cs, cs_t=cs.T,
                fold=fold, dtx=_dot01(dt, expand, True, 2), growx=_dot01(grow, expand, True, 2),
                fadex=_dot01(fade, expand, True, 2))


def _decay_matrix(cm, j):
    diff = cm["cs"][:, j:j + 1] - cm["cs_t"][j:j + 1, :]
    return jnp.where(cm["low"], jnp.exp(jnp.where(cm["low"], diff, 0.0)), 0.0)


def _dot(a, b, dims):
    return lax.dot_general(a.astype(BF16), b.astype(BF16), (dims, ((), ())), preferred_element_type=F32)


def _ssd_specs(nt, rev):
    def idx(c):
        return nt - 1 - c if rev else c
    xs = pl.BlockSpec((CHUNK, GROUP_W), lambda g, c: (idx(c), g))
    bm = pl.BlockSpec((CHUNK, D_STATE), lambda g, c: (idx(c), D_INNER // D_STATE + g))
    cm = pl.BlockSpec((CHUNK, D_STATE), lambda g, c: (idx(c), D_INNER // D_STATE + SSM_GROUPS + g))
    dtr = pl.BlockSpec((CHUNK, 128), lambda g, c: (idx(c), OFF_DT // 128 + g))
    par = pl.BlockSpec((1, 1, 128), lambda g, c: (g, 0, 0))
    par_x = pl.BlockSpec((1, 1, GROUP_W), lambda g, c: (g, 0, 0))
    return xs, bm, cm, dtr, par, par_x, idx


def _ssd_fwd(xact, proj, dtb, alog, dskip_x):
    t_rows = xact.shape[0]
    nt = t_rows // CHUNK
    xs_spec, b_spec, c_spec, dtr_spec, par, par_x, _ = _ssd_specs(nt, False)

    def body(xs_ref, b_ref, c_ref, dtr_ref, dtb_ref, alog_ref, dsk_ref, y_ref, hst_ref, state):
        c = pl.program_id(1)

        @pl.when(c == 0)
        def _():
            state[...] = jnp.zeros_like(state)

        cm = _ssd_common(dtr_ref, dtb_ref, alog_ref, c)
        xs, bm, cmat = xs_ref[...], b_ref[...], c_ref[...]
        x_dt = xs * cm["dtx"]
        h_in = state[...]
        hst_ref[0, 0] = h_in
        y_ref[...] = _dot(cmat, h_in, ((1,), (0,))) * cm["growx"] + xs * dsk_ref[0]
        cb = _dot(cmat, bm, ((1,), (1,)))
        for j in range(HEADS_PER_GROUP):
            sl = slice(j * HEAD_P, (j + 1) * HEAD_P)
            y_ref[:, sl] += _dot(cb * _decay_matrix(cm, j), x_dt[:, sl], ((1,), (0,)))
        state[...] = h_in * cm["growx"][CHUNK - 1:CHUNK, :] + _dot(bm, x_dt * cm["fadex"], ((0,), (0,)))

    return pl.pallas_call(
        body, grid=(SSM_GROUPS, nt),
        in_specs=[xs_spec, b_spec, c_spec, dtr_spec, par, par, par_x],
        out_specs=[xs_spec, pl.BlockSpec((1, 1, D_STATE, GROUP_W), lambda g, c: (c, g, 0, 0))],
        out_shape=[_sds((t_rows, D_INNER), F32), _sds((nt, SSM_GROUPS, D_STATE, GROUP_W), F32)],
        scratch_shapes=[pltpu.VMEM((D_STATE, GROUP_W), F32)],
        name="ssd_fwd", compiler_params=_params(2))(xact, xact, xact, proj, dtb, alog, dskip_x)


def _ssd_bwd(xact, proj, dtb, alog, dskip_x, dy, hst):
    t_rows = xact.shape[0]
    nt = t_rows // CHUNK
    xs_spec, b_spec, c_spec, dtr_spec, par, par_x, idx = _ssd_specs(nt, True)
    h_spec = pl.BlockSpec((1, 1, D_STATE, GROUP_W), lambda g, c: (idx(c), g, 0, 0))
    hn_spec = pl.BlockSpec((1, 1, D_STATE, GROUP_W), lambda g, c: (jnp.minimum(idx(c) + 1, nt - 1), g, 0, 0))
    bc_out = pl.BlockSpec((CHUNK, D_STATE), lambda g, c: (idx(c), g))

    def body(xs_ref, b_ref, c_ref, dtr_ref, dtb_ref, alog_ref, dsk_ref, dy_ref, h_ref, hn_ref,
             dxs_ref, db_ref, dc_ref, ddt_ref, dalog_ref, ddtb_ref, dd_ref, dstate, dx_buf):
        step = pl.program_id(1)

        @pl.when(step == 0)
        def _():
            dstate[...] = jnp.zeros_like(dstate)
            dalog_ref[...] = jnp.zeros_like(dalog_ref)
            ddtb_ref[...] = jnp.zeros_like(ddtb_ref)
            dd_ref[...] = jnp.zeros_like(dd_ref)

        cm = _ssd_common(dtr_ref, dtb_ref, alog_ref, idx(step))
        xs, bm, cmat = xs_ref[...], b_ref[...], c_ref[...]
        dsk = dsk_ref[0]
        x_dt = xs * cm["dtx"]
        h_in, h_next = h_ref[0, 0], hn_ref[0, 0]
        dyv = dy_ref[...]
        dh = dstate[...]
        grow, fade = cm["growx"], cm["fadex"]
        dy_grow = dyv * grow
        x_fade = x_dt * fade
        cb = _dot(cmat, bm, ((1,), (1,)))
        ml = jnp.zeros((CHUNK, CHUNK), F32)
        row_id = lax.broadcasted_iota(jnp.int32, (CHUNK, CHUNK), 0)
        col_id = lax.broadcasted_iota(jnp.int32, (CHUNK, CHUNK), 1)
        w_rows = jnp.zeros((CHUNK, CHUNK), F32)
        w_cols = jnp.zeros((CHUNK, CHUNK), F32)
        for j in range(HEADS_PER_GROUP):
            sl = slice(j * HEAD_P, (j + 1) * HEAD_P)
            lm = _decay_matrix(cm, j)
            mlj = _dot(dyv[:, sl], x_dt[:, sl], ((1,), (1,))) * lm
            ml = ml + mlj
            wm = mlj * cb
            w_rows = jnp.where(col_id == j, jnp.sum(wm, axis=1, keepdims=True), w_rows)
            w_cols = jnp.where(row_id == j, jnp.sum(wm, axis=0, keepdims=True), w_cols)
            dx_buf[:, sl] = _dot(cb * lm, dyv[:, sl], ((0,), (0,)))
        dx_off = fade * _dot(bm, dh, ((1,), (0,)))
        dx = dx_buf[...] + dx_off
        dc_ref[...] = _dot(ml, bm, ((1,), (0,))) + _dot(dy_grow, h_in, ((1,), (1,)))
        db_ref[...] = _dot(ml, cmat, ((0,), (0,))) + _dot(x_fade, dh, ((1,), (1,)))
        fold = cm["fold"]
        y_off = _dot(cmat, h_in, ((1,), (0,))) * grow
        dcs = (w_rows - w_cols.T) + _dot01(dyv * y_off - x_dt * dx_off, fold, True, 2)
        tail = jnp.broadcast_to(jnp.sum(dh * h_next, axis=0, keepdims=True), (8, GROUP_W))
        tail = _dot01(tail, fold, True, 2)[0:1, :]
        last_row = lax.broadcasted_iota(jnp.int32, (CHUNK, 128), 0) == CHUNK - 1
        dcs = dcs + jnp.where(last_row, tail, 0.0)
        da = _dot01(dcs, cm["triu"], False, 3)
        ddt = da * cm["a_head"] + _dot01(dx * xs, fold, True, 2)
        ddt_raw = jnp.where(cm["live"], ddt * _sigmoid(cm["raw"]), 0.0)
        ddt_ref[...] = ddt_raw.astype(BF16)
        ddtb_ref[0] += jnp.sum(ddt_raw, axis=0, keepdims=True)
        dalog_ref[0] += jnp.sum(da * cm["dt"], axis=0, keepdims=True) * cm["a_head"]
        dd_ref[0] += jnp.sum(dyv * xs, axis=0, keepdims=True)
        dxs_ref[...] = dx * cm["dtx"] + dyv * dsk
        dstate[...] = dh * grow[CHUNK - 1:CHUNK, :] + _dot(cmat, dy_grow, ((0,), (0,)))

    return pl.pallas_call(
        body, grid=(SSM_GROUPS, nt),
        in_specs=[xs_spec, b_spec, c_spec, dtr_spec, par, par, par_x, xs_spec, h_spec, hn_spec],
        out_specs=[xs_spec, bc_out, bc_out, bc_out, par, par, par_x],
        out_shape=[_sds((t_rows, D_INNER), F32), _sds((t_rows, SSM_GROUPS * D_STATE), F32),
                   _sds((t_rows, SSM_GROUPS * D_STATE), F32), _sds((t_rows, SSM_GROUPS * 128), BF16),
                   _sds((SSM_GROUPS, 1, 128), F32), _sds((SSM_GROUPS, 1, 128), F32), _sds((SSM_GROUPS, 1, GROUP_W), F32)],
        scratch_shapes=[pltpu.VMEM((D_STATE, GROUP_W), F32), pltpu.VMEM((CHUNK, GROUP_W), F32)],
        name="ssd_bwd", compiler_params=_params(2))(xact, xact, xact, proj, dtb, alog, dskip_x, dy, hst, hst)


def _swa_bias():
    rows_q = ATTN_GROUP * CHUNK
    dist = (jnp.arange(rows_q) % CHUNK)[:, None] - jnp.arange(2 * CHUNK)[None, :] + CHUNK
    head = jnp.arange(KV_HEADS)[:, None] * ATTN_GROUP + jnp.arange(rows_q)[None, :] // CHUNK + 1
    slope = jnp.exp2(-8.0 * head.astype(F32) / ATTN_HEADS)
    return jnp.where((dist >= 0) & (dist < CHUNK), -slope[:, :, None] * dist.astype(F32)[None], NEG)


def _swa_probs(q_kv, k_prev, k_cur, k_first, sink, bias, n):
    rows_q = ATTN_GROUP * CHUNK
    qs = jnp.concatenate([q_kv[:, g * DH:(g + 1) * DH] for g in range(ATTN_GROUP)], axis=0) * (DH ** -0.5)
    kcat = jnp.concatenate([k_prev, k_cur], axis=0)
    kmeta = k_first[META_PAD:, :]
    key_ok = lax.broadcasted_iota(jnp.int32, (1, 2 * CHUNK), 1) + n * CHUNK >= 2 * CHUNK
    s_band = jnp.where(key_ok, _dot(qs, kcat, ((1,), (1,))) + bias, NEG)
    q_pos = lax.broadcasted_iota(jnp.int32, (rows_q, N_META), 0) % CHUNK + n * CHUNK - META_PAD
    ok_m = lax.broadcasted_iota(jnp.int32, (rows_q, N_META), 1) <= q_pos
    s_meta = jnp.where(ok_m, _dot(qs, kmeta, ((1,), (1,))), NEG)
    m = jnp.maximum(jnp.maximum(jnp.max(s_band, axis=1, keepdims=True), jnp.max(s_meta, axis=1, keepdims=True)), sink)
    p_band, p_meta, p_sink = jnp.exp(s_band - m), jnp.exp(s_meta - m), jnp.exp(sink - m)
    inv = 1.0 / (jnp.sum(p_band, axis=1, keepdims=True) + jnp.sum(p_meta, axis=1, keepdims=True) + p_sink)
    return qs, kcat, kmeta, p_band * inv, p_meta * inv, p_sink * inv


def _swa_specs(nt, rev):
    def idx(n):
        return nt - 1 - n if rev else n
    width = ATTN_HEADS * DH
    o = pl.BlockSpec((CHUNK, width), lambda n: (idx(n), 0))
    q_proj = pl.BlockSpec((CHUNK, width), lambda n: (idx(n), OFF_Q // width))
    cur = pl.BlockSpec((KV_HEADS, CHUNK, DH), lambda n: (0, idx(n), 0))
    prev = pl.BlockSpec((KV_HEADS, CHUNK, DH), lambda n: (0, jnp.maximum(idx(n) - 1, 0), 0))
    first = pl.BlockSpec((KV_HEADS, CHUNK, DH), lambda n: (0, 0, 0))
    sink = _full((KV_HEADS, ATTN_GROUP * CHUNK, 1))
    bias = _full((KV_HEADS, ATTN_GROUP * CHUNK, 2 * CHUNK))
    return o, q_proj, cur, prev, first, sink, bias, idx


def _swa_fwd(proj, k_heads, v_heads, sink_rows, bias):
    t_rows = proj.shape[0]
    nt = t_rows // CHUNK
    o_spec, q_spec, cur, prev, first, sink_spec, bias_spec, _ = _swa_specs(nt, False)
    kv_w = ATTN_GROUP * DH

    def body(q_ref, kp_ref, kc_ref, km_ref, vp_ref, vc_ref, vm_ref, sink_ref, bias_ref, o_ref):
        n = pl.program_id(0)
        for k in range(KV_HEADS):
            _, _, _, p_band, p_meta, _ = _swa_probs(q_ref[:, k * kv_w:(k + 1) * kv_w], kp_ref[k], kc_ref[k], km_ref[k],
                                                    sink_ref[k], bias_ref[k], n)
            vcat = jnp.concatenate([vp_ref[k], vc_ref[k]], axis=0)
            out = _dot(p_band, vcat, ((1,), (0,))) + _dot(p_meta, vm_ref[k][META_PAD:, :], ((1,), (0,)))
            for g in range(ATTN_GROUP):
                o_ref[:, k * kv_w + g * DH:k * kv_w + (g + 1) * DH] = out[g * CHUNK:(g + 1) * CHUNK, :]

    return pl.pallas_call(
        body, grid=(nt,), in_specs=[q_spec, prev, cur, first, prev, cur, first, sink_spec, bias_spec],
        out_specs=o_spec, out_shape=_sds((t_rows, ATTN_HEADS * DH), F32),
        name="swa_fwd", compiler_params=_params(1))(proj, k_heads, k_heads, k_heads, v_heads, v_heads, v_heads,
                                                    sink_rows, bias)


def _swa_bwd(proj, k_heads, v_heads, sink_rows, bias, out, dout):
    t_rows = proj.shape[0]
    nt = t_rows // CHUNK
    o_spec, q_spec, cur, prev, first, sink_spec, bias_spec, idx = _swa_specs(nt, True)
    kv_w = ATTN_GROUP * DH

    def body(q_ref, kp_ref, kc_ref, km_ref, vp_ref, vc_ref, vm_ref, sink_ref, bias_ref, o_ref, do_ref,
             dq_ref, dk_ref, dv_ref, dsink_ref, carry_k, carry_v, meta_k, meta_v):
        step = pl.program_id(0)
        n = idx(step)

        @pl.when(step == 0)
        def _():
            carry_k[...] = jnp.zeros_like(carry_k)
            carry_v[...] = jnp.zeros_like(carry_v)
            meta_k[...] = jnp.zeros_like(meta_k)
            meta_v[...] = jnp.zeros_like(meta_v)
            dsink_ref[...] = jnp.zeros_like(dsink_ref)

        for k in range(KV_HEADS):
            cols = slice(k * kv_w, (k + 1) * kv_w)
            qs, kcat, kmeta, p_band, p_meta, p_sink = _swa_probs(q_ref[:, cols], kp_ref[k], kc_ref[k], km_ref[k],
                                                                 sink_ref[k], bias_ref[k], n)
            vcat = jnp.concatenate([vp_ref[k], vc_ref[k]], axis=0)
            vmeta = vm_ref[k][META_PAD:, :]
            o, do = o_ref[:, cols], do_ref[:, cols]
            os_ = jnp.concatenate([o[:, g * DH:(g + 1) * DH] for g in range(ATTN_GROUP)], axis=0)
            dos = jnp.concatenate([do[:, g * DH:(g + 1) * DH] for g in range(ATTN_GROUP)], axis=0)
            delta = jnp.sum(dos * os_, axis=1, keepdims=True)
            ds_band = p_band * (_dot(dos, vcat, ((1,), (1,))) - delta)
            ds_meta = p_meta * (_dot(dos, vmeta, ((1,), (1,))) - delta)
            ds_sink = -p_sink * delta
            dqs = (_dot(ds_band, kcat, ((1,), (0,))) + _dot(ds_meta, kmeta, ((1,), (0,)))) * (DH ** -0.5)
            for g in range(ATTN_GROUP):
                dq_ref[:, k * kv_w + g * DH:k * kv_w + (g + 1) * DH] = dqs[g * CHUNK:(g + 1) * CHUNK, :]
                dsink_ref[k, g:g + 1, :] += jnp.sum(ds_sink[g * CHUNK:(g + 1) * CHUNK, :])
            dkcat = _dot(ds_band, qs, ((0,), (0,)))
            dvcat = _dot(p_band, dos, ((0,), (0,)))
            meta_k[k] += _dot(ds_meta, qs, ((0,), (0,)))
            meta_v[k] += _dot(p_meta, dos, ((0,), (0,)))
            dk_ref[k] = dkcat[CHUNK:, :] + carry_k[k]
            dv_ref[k] = dvcat[CHUNK:, :] + carry_v[k]
            carry_k[k] = dkcat[:CHUNK, :]
            carry_v[k] = dvcat[:CHUNK, :]

        @pl.when(n == 0)
        def _():
            dk_ref[:, META_PAD:, :] += meta_k[...]
            dv_ref[:, META_PAD:, :] += meta_v[...]

    return pl.pallas_call(
        body, grid=(nt,),
        in_specs=[q_spec, prev, cur, first, prev, cur, first, sink_spec, bias_spec, o_spec, o_spec],
        out_specs=[o_spec, cur, cur, _full((KV_HEADS, 8, 128))],
        out_shape=[_sds((t_rows, ATTN_HEADS * DH), F32), _sds((KV_HEADS, t_rows, DH), F32),
                   _sds((KV_HEADS, t_rows, DH), F32), _sds((KV_HEADS, 8, 128), F32)],
        scratch_shapes=[pltpu.VMEM((KV_HEADS, CHUNK, DH), F32), pltpu.VMEM((KV_HEADS, CHUNK, DH), F32),
                        pltpu.VMEM((KV_HEADS, N_META, DH), F32), pltpu.VMEM((KV_HEADS, N_META, DH), F32)],
        name="swa_bwd", compiler_params=_params(1))(proj, k_heads, k_heads, k_heads, v_heads, v_heads, v_heads,
                                                    sink_rows, bias, out, dout)


def _pack_w_in(w_in):
    w_dt = w_in[:, CUT_DT:CUT_Q].reshape(D_MODEL, SSM_GROUPS, HEADS_PER_GROUP)
    w_dt = jnp.pad(w_dt, ((0, 0), (0, 0), (0, 128 - HEADS_PER_GROUP))).reshape(D_MODEL, SSM_GROUPS * 128)
    return jnp.concatenate([w_in[:, CUT_Z:CUT_XBC], w_in[:, CUT_G:], w_in[:, CUT_XBC:CUT_DT], w_in[:, CUT_Q:CUT_K],
                            w_in[:, CUT_K:CUT_V], w_in[:, CUT_V:CUT_G], w_dt], axis=1)


def _unpack_w_in(wp):
    w_dt = wp[:, OFF_DT:].reshape(D_MODEL, SSM_GROUPS, 128)[:, :, :HEADS_PER_GROUP].reshape(D_MODEL, SSM_HEADS)
    return jnp.concatenate([wp[:, OFF_Z:OFF_GATE], wp[:, OFF_XBC:OFF_Q], w_dt, wp[:, OFF_Q:OFF_K], wp[:, OFF_K:OFF_V],
                            wp[:, OFF_V:OFF_DT], wp[:, OFF_GATE:OFF_XBC]], axis=1)


def _group_rows(v, width):
    return jnp.pad(v.reshape(SSM_GROUPS, 1, HEADS_PER_GROUP), ((0, 0), (0, 0), (0, width - HEADS_PER_GROUP)))


def _to_heads(a):
    return a.reshape(a.shape[0], KV_HEADS, DH).transpose(1, 0, 2)


def _from_heads(a):
    return a.transpose(1, 0, 2).reshape(a.shape[1], KV_W)


def _local_step(x, target, wt, late_weights=None, on_grad=None, started=None):
    seq = x.shape[0]
    grads = {}

    def emit(name, g):
        grads[name] = g
        return None if on_grad is None else on_grad(name, g)
    h = jnp.concatenate([jnp.zeros((META_PAD, D_MODEL), F32), wt["meta_tokens"], x], axis=0)
    wp = _pack_w_in(wt["w_in"])
    dtb = _group_rows(wt["ssm_dt_bias"].reshape(-1), 128)
    alog = _group_rows(wt["ssm_a_log"].reshape(-1), 128)
    dskip_x = jnp.repeat(wt["ssm_d_skip"].reshape(-1), HEAD_P).reshape(SSM_GROUPS, 1, GROUP_W)
    sink_rows = jnp.repeat(wt["attn_sinks"].reshape(KV_HEADS, ATTN_GROUP), CHUNK, axis=1).reshape(KV_HEADS, ATTN_GROUP * CHUNK, 1)

    hn = _prenorm(h, wt["norm_pre_mix"])
    proj = _matmul(hn, wp, name="in_proj", after=started)
    xc, xact = _ssm_conv_fwd(proj, wt["ssm_conv_w"], wt["ssm_conv_b"])
    y, hst = _ssd_fwd(xact, proj, dtb, alog, dskip_x)
    yn = _ssm_post(y, proj, wt["ssm_norm"])
    if late_weights is not None:
        wt = {**wt, **late_weights(yn)}
    y_ssm = _matmul(yn, wt["w_ssm_out"], name="ssm_out")
    k_heads = _to_heads(proj[:, OFF_K:OFF_V])
    v_heads = _to_heads(proj[:, OFF_V:OFF_DT])
    bias = _swa_bias()
    attn = _swa_fwd(proj, k_heads, v_heads, sink_rows, bias)
    y_attn = _matmul(attn, wt["w_attn_out"], name="attn_out")
    mixed = _mix_fwd(proj, y_ssm, y_attn)
    mix = _matmul(mixed, wt["w_mix_out"], name="mix_out")
    h1, hn2 = _postmix(h, mix, wt["norm_post_mix"], wt["norm_pre_ffn"])
    up = _matmul(hn2, wt["w_ffn_up"], name="ffn_up")
    u, act = _ffn_act(up, wt["ffn_conv_w"], wt["ffn_conv_b"])
    f = _matmul(act, wt["w_ffn_down"], name="ffn_down")
    df, dy, g_norm_post_ffn, loss_row = _final(h1, f, target, wt["norm_post_ffn"])

    grads["norm_post_ffn"] = g_norm_post_ffn
    sent = emit("w_ffn_down", _matmul(act, df, ta=True, name="dw_ffn_down"))
    dact = _matmul(df, wt["w_ffn_down"], tb=True, name="d_act", after=sent)
    dup, grads["ffn_conv_w"], grads["ffn_conv_b"] = _ffn_act_bwd(u, up, dact, wt["ffn_conv_w"])
    sent = emit("w_ffn_up", _matmul(hn2, dup, ta=True, name="dw_ffn_up"))
    dhn2 = _matmul(dup, wt["w_ffn_up"], tb=True, name="d_hn2", after=sent)
    dmix, dh, grads["norm_pre_ffn"], grads["norm_post_mix"] = _postmix_bwd(h1, dhn2, dy, mix, wt["norm_pre_ffn"], wt["norm_post_mix"])
    sent = emit("w_mix_out", _matmul(mixed, dmix, ta=True, name="dw_mix_out"))
    dmixed = _matmul(dmix, wt["w_mix_out"], tb=True, name="d_mixed", after=sent)
    dy_ssm, dy_attn, dglog = _mix_bwd(dmixed, proj, y_ssm, y_attn)
    sent = emit("w_ssm_out", _matmul(yn, dy_ssm, ta=True, name="dw_ssm_out"))
    dyn = _matmul(dy_ssm, wt["w_ssm_out"], tb=True, name="d_yn", after=sent)
    sent = emit("w_attn_out", _matmul(attn, dy_attn, ta=True, name="dw_attn_out"))
    dattn = _matmul(dy_attn, wt["w_attn_out"], tb=True, name="d_attn", after=sent)
    dy_ssd, dz, grads["ssm_norm"] = _ssm_post_bwd(y, proj, dyn, wt["ssm_norm"])
    dxs, dbm, dcm, ddt, dalog, ddtb, dd_x = _ssd_bwd(xact, proj, dtb, alog, dskip_x, dy_ssd, hst)
    grads["ssm_a_log"] = dalog[:, 0, :HEADS_PER_GROUP].reshape(1, SSM_HEADS)
    grads["ssm_dt_bias"] = ddtb[:, 0, :HEADS_PER_GROUP].reshape(1, SSM_HEADS)
    grads["ssm_d_skip"] = dd_x.reshape(SSM_HEADS, HEAD_P).sum(axis=1).reshape(1, SSM_HEADS)
    dxbc, grads["ssm_conv_w"], grads["ssm_conv_b"] = _ssm_conv_bwd(xc, proj, dxs, dbm, dcm, wt["ssm_conv_w"])
    dq, dk_heads, dv_heads, dsink = _swa_bwd(proj, k_heads, v_heads, sink_rows, bias, attn, dattn)
    grads["attn_sinks"] = dsink[:, :ATTN_GROUP, 0].reshape(1, ATTN_HEADS)
    dproj = jnp.concatenate([dz, dglog, dxbc, dq.astype(BF16), _from_heads(dk_heads).astype(BF16),
                             _from_heads(dv_heads).astype(BF16), ddt], axis=1)
    sent = emit("w_in", _unpack_w_in(_matmul(hn, dproj, ta=True, name="dw_in")))
    dhn = _matmul(dproj, wp, tb=True, name="d_hn", after=sent)
    dh_all, grads["norm_pre_mix"] = _prenorm_bwd(h, dhn, dh, wt["norm_pre_mix"])
    grads["meta_tokens"] = dh_all[META_PAD:CHUNK]
    return loss_row[0, 0], dh_all[CHUNK:CHUNK + seq], grads


def _all_gather(shards):
    n = len(shards)

    def body(*refs):
        ins, outs = refs[:n], refs[n:2 * n]
        send_sems, recv_sems, local_sems = refs[2 * n:]
        x, y, c = lax.axis_index("x"), lax.axis_index("y"), lax.axis_index("c")
        me, sibling = (x, y, c), (x, y, 1 - c)
        chips = [(1 - x, y), (x, 1 - y), (1 - x, 1 - y)]

        def slot(a, dev):
            return outs[a].at[4 * dev[0] + 2 * dev[1] + dev[2]]

        def copy(k, a, block, to, src=None):
            return pltpu.make_async_remote_copy(
                src_ref=slot(a, block) if src is None else src, dst_ref=slot(a, block),
                send_sem=send_sems.at[k, a], recv_sem=recv_sems.at[k, a],
                device_id=to, device_id_type=pl.DeviceIdType.MESH)

        mine = [pltpu.make_async_copy(ins[a], slot(a, me), local_sems.at[a]) for a in range(n)]
        for cp in mine:
            cp.start()
        first = [copy(0, a, me, sibling, src=ins[a]) for a in range(n)]
        for j, chip in enumerate(chips):
            first += [copy(1 + j, a, me, (*chip, c), src=ins[a]) for a in range(n)]
        for cp in first:
            cp.start()
        passed = []
        for j, chip in enumerate(chips):
            for a in range(n):
                copy(1 + j, a, (*chip, c), me).wait_recv()
                fwd = copy(4 + j, a, (*chip, c), sibling)
                fwd.start()
                passed.append(fwd)
        for a in range(n):
            copy(0, a, sibling, me).wait_recv()
        for j, chip in enumerate(chips):
            for a in range(n):
                copy(4 + j, a, (*chip, 1 - c), me).wait_recv()
        for cp in first + passed:
            cp.wait_send()
        for cp in mine:
            cp.wait()

    hbm = pl.BlockSpec(memory_space=pl.ANY)
    return pl.pallas_call(
        body, in_specs=[hbm] * n, out_specs=[hbm] * n,
        out_shape=[_sds((N_DEV,) + s.shape, s.dtype) for s in shards],
        scratch_shapes=[pltpu.SemaphoreType.DMA((7, n)), pltpu.SemaphoreType.DMA((7, n)), pltpu.SemaphoreType.DMA((n,))],
        name="gather_weights")(*shards)


def _exchange_partials(parts):
    n = len(parts)

    def body(*refs):
        ins, outs = refs[:n], refs[n:2 * n]
        send_sems, recv_sems, local_sems = refs[2 * n:]
        x, y, c = lax.axis_index("x"), lax.axis_index("y"), lax.axis_index("c")
        my_id = 4 * x + 2 * y + c

        def peer(k):
            return (x ^ ((k >> 2) & 1), y ^ ((k >> 1) & 1), c ^ (k & 1))

        def copy(k, a):
            p = peer(k)
            p_id = 4 * p[0] + 2 * p[1] + p[2]
            return pltpu.make_async_remote_copy(
                src_ref=ins[a].at[p_id], dst_ref=outs[a].at[my_id],
                send_sem=send_sems.at[k - 1, a], recv_sem=recv_sems.at[k - 1, a],
                device_id=p, device_id_type=pl.DeviceIdType.MESH)

        def arrival(k, a):
            p = peer(k)
            p_id = 4 * p[0] + 2 * p[1] + p[2]
            return pltpu.make_async_remote_copy(
                src_ref=ins[a].at[p_id], dst_ref=outs[a].at[p_id],
                send_sem=send_sems.at[k - 1, a], recv_sem=recv_sems.at[k - 1, a],
                device_id=p, device_id_type=pl.DeviceIdType.MESH)

        mine = [pltpu.make_async_copy(ins[a].at[my_id], outs[a].at[my_id], local_sems.at[a]) for a in range(n)]
        for cp in mine:
            cp.start()
        sends = [copy(k, a) for k in range(1, N_DEV) for a in range(n)]
        for cp in sends:
            cp.start()
        for k in range(1, N_DEV):
            for a in range(n):
                arrival(k, a).wait_recv()
        for cp in sends:
            cp.wait_send()
        for cp in mine:
            cp.wait()

    hbm = pl.BlockSpec(memory_space=pl.ANY)
    return pl.pallas_call(
        body, in_specs=[hbm] * n, out_specs=[hbm] * n, out_shape=[_sds(p.shape, p.dtype) for p in parts],
        scratch_shapes=[pltpu.SemaphoreType.DMA((7, n)), pltpu.SemaphoreType.DMA((7, n)), pltpu.SemaphoreType.DMA((n,))],
        name="exchange_grads")(*parts)


def _peer_table():
    x, y, c = lax.axis_index("x"), lax.axis_index("y"), lax.axis_index("c")
    peers = []
    for k in range(N_DEV - 1):
        bits = k + 1
        p = (x ^ ((bits >> 2) & 1), y ^ ((bits >> 1) & 1), c ^ (bits & 1))
        peers.append((k, p, 4 * p[0] + 2 * p[1] + p[2]))
    return 4 * x + 2 * y + c, peers


_HBM = pl.BlockSpec(memory_space=pltpu.HBM)
_SEM = pl.BlockSpec(memory_space=pltpu.SEMAPHORE)
_EFFECT = pltpu.SideEffectType.DATAFLOW_SIDE_EFFECTING


def _push_copy(src, land, send_sems, recv_sems, a, k, p, src_slot, dst_slot):
    sem = a * (N_DEV - 1) + k
    return pltpu.make_async_remote_copy(
        src_ref=src[a] if src_slot is None else src[a].at[src_slot], dst_ref=land[a].at[dst_slot],
        send_sem=send_sems.at[sem], recv_sem=recv_sems.at[sem], device_id=p, device_id_type=pl.DeviceIdType.MESH)


def _push_start(srcs, scatter, name):
    n = len(srcs)
    lands = [lax.empty(s.shape if scatter else (N_DEV,) + s.shape, s.dtype) for s in srcs]

    def body(*refs):
        src, land = refs[:n], refs[n:2 * n]
        send_sems, recv_sems, token = refs[2 * n], refs[2 * n + 1], refs[-1]
        my_id, peers = _peer_table()
        for a in range(n):
            for k, p, p_id in peers:
                _push_copy(src, land, send_sems, recv_sems, a, k, p, p_id if scatter else None, my_id).start()
        token[...] = jnp.zeros_like(token)

    sems = pltpu.SemaphoreType.DMA(((N_DEV - 1) * n,))
    res = pl.pallas_call(
        body, name=name,
        out_shape=(sems, sems, *[pltpu.HBM(a.shape, a.dtype) for a in srcs + lands], _sds((8, 128), F32)),
        in_specs=[_HBM] * (2 * n), out_specs=(_SEM, _SEM, *[_HBM] * (2 * n), pl.BlockSpec(memory_space=pltpu.VMEM)),
        input_output_aliases={i: 2 + i for i in range(2 * n)},
        compiler_params=pltpu.CompilerParams(has_side_effects=_EFFECT),
    )(*[pltpu.with_memory_space_constraint(a, pltpu.HBM) for a in srcs + lands])
    return dict(send=res[0], recv=res[1], src=list(res[2:2 + n]), land=list(res[2 + n:2 + 2 * n]), token=res[-1],
                scatter=scatter)


def _push_wait(handle, after, name):
    n = len(handle["src"])
    scatter = handle["scatter"]

    def body(*refs):
        src, land = refs[:n], refs[n:2 * n]
        send_sems, recv_sems = refs[2 * n], refs[2 * n + 1]
        _, peers = _peer_table()
        for a in range(n):
            for k, p, p_id in peers:
                cp = _push_copy(src, land, send_sems, recv_sems, a, k, p, p_id if scatter else None, p_id)
                cp.wait_send()
                cp.wait_recv()

    arrays = handle["src"] + handle["land"]
    res = pl.pallas_call(
        body, name=name, out_shape=tuple(pltpu.HBM(a.shape, a.dtype) for a in arrays),
        in_specs=[_HBM] * (2 * n) + [_SEM, _SEM, pl.BlockSpec(memory_space=pl.ANY)], out_specs=tuple([_HBM] * (2 * n)),
        input_output_aliases={i: i for i in range(2 * n)},
        compiler_params=pltpu.CompilerParams(has_side_effects=_EFFECT),
    )(*arrays, handle["send"], handle["recv"], after)
    return list(res[:n]), list(res[n:])


def _adamw(parts, own, w, m, v, name):
    rows, cols = w.shape
    tr = _pick(rows, (256, 128, 176, 64, 32, 16, 8))

    def body(*refs):
        if own is None:
            p_ref, w_ref, m_ref, v_ref, g_ref, d_ref, nm_ref, nv_ref = refs
        else:
            p_ref, own_ref, w_ref, m_ref, v_ref, g_ref, d_ref, nm_ref, nv_ref = refs
            my_id = 4 * lax.axis_index("x") + 2 * lax.axis_index("y") + lax.axis_index("c")
            mine = own_ref[...].astype(F32)
        g = None
        for s in range(N_DEV):
            term = p_ref[s].astype(F32)
            if own is not None:
                term = jnp.where(my_id == s, mine, term)
            g = term if g is None else g + term
        m_new = ADAM_B1 * m_ref[...] + (1.0 - ADAM_B1) * g
        v_new = ADAM_B2 * v_ref[...] + (1.0 - ADAM_B2) * (g * g)
        m_hat = m_new / (1.0 - ADAM_B1 ** ADAM_STEP)
        v_hat = v_new / (1.0 - ADAM_B2 ** ADAM_STEP)
        g_ref[...] = g
        d_ref[...] = -ADAM_LR * (m_hat / (jnp.sqrt(v_hat) + ADAM_EPS) + ADAM_WD * w_ref[...])
        nm_ref[...] = m_new
        nv_ref[...] = v_new

    spec = _row(tr, cols)
    operands = (parts, w, m, v) if own is None else (parts, own, w, m, v)
    return pl.pallas_call(
        body, grid=(rows // tr,),
        in_specs=[pl.BlockSpec((N_DEV, tr, cols), lambda i: (0, i, 0))] + [spec] * (len(operands) - 1),
        out_specs=[spec] * 4, out_shape=[_sds((rows, cols), F32)] * 4,
        name=name, compiler_params=_params(1))(*operands)


SMALL_REPLICATED = (("norm_pre_mix", 1024), ("ssm_conv_b", 3072), ("ssm_dt_bias", 32), ("ssm_a_log", 32),
                    ("ssm_d_skip", 32), ("ssm_norm", 2048), ("attn_sinks", 16), ("norm_post_mix", 1024),
                    ("norm_pre_ffn", 1024), ("ffn_conv_b", 5632), ("norm_post_ffn", 1024))
SMALL_SHARDED = (("meta_tokens", (N_META, D_MODEL // N_DEV)), ("ssm_conv_w", (SSM_CONV, CONV_DIM // N_DEV)),
                 ("ffn_conv_w", (FFN_CONV, 2 * FFN_DIM // N_DEV)))
BIG = (("w_in", (D_MODEL, N_IN // N_DEV), 1), ("w_ssm_out", (D_INNER // N_DEV, D_MODEL), 0),
       ("w_attn_out", (D_MODEL // N_DEV, D_MODEL), 0), ("w_mix_out", (D_MODEL // N_DEV, D_MODEL), 0),
       ("w_ffn_up", (D_MODEL, 2 * FFN_DIM // N_DEV), 1), ("w_ffn_down", (FFN_DIM // N_DEV, D_MODEL), 0))


def _rows_of(size):
    return -(-size // 128)


def _as_rows(flat):
    size = flat.shape[-1]
    rows = _rows_of(size)
    flat = jnp.pad(flat, [(0, 0)] * (flat.ndim - 1) + [(0, rows * 128 - size)])
    return flat.reshape(flat.shape[:-1] + (rows, 128))


def _pack_small(rep, sharded):
    pieces = [_as_rows(rep[name].reshape(-1)) for name, _ in SMALL_REPLICATED]
    pieces += [_as_rows(sharded[name].reshape(-1)) for name, _ in SMALL_SHARDED]
    packed = jnp.concatenate(pieces, axis=0)
    return jnp.pad(packed, ((0, -packed.shape[0] % 8), (0, 0)))


def _unpack_small(packed):
    out, row = {}, 0
    for name, size in SMALL_REPLICATED:
        out[name] = packed[row:row + _rows_of(size)].reshape(-1)[:size].reshape(1, size)
        row += _rows_of(size)
    for name, (r, c) in SMALL_SHARDED:
        out[name] = packed[row:row + _rows_of(r * c)].reshape(-1)[:r * c].reshape(r, c)
        row += _rows_of(r * c)
    return out


def _shard_major(g, shape, axis):
    r, c = shape
    if axis == 0:
        return g.reshape(N_DEV, r, c)
    return g.reshape(r, N_DEV, c).transpose(1, 0, 2)


def kernel(x, meta_tokens, norm_pre_mix, w_in, ssm_conv_w, ssm_conv_b, ssm_dt_bias, ssm_a_log, ssm_d_skip, ssm_norm, w_ssm_out, attn_sinks, w_attn_out, w_mix_out, norm_post_mix, norm_pre_ffn, w_ffn_up, ffn_conv_w, ffn_conv_b, w_ffn_down, norm_post_ffn, loss_target, m_meta_tokens, m_norm_pre_mix, m_w_in, m_ssm_conv_w, m_ssm_conv_b, m_ssm_dt_bias, m_ssm_a_log, m_ssm_d_skip, m_ssm_norm, m_w_ssm_out, m_attn_sinks, m_w_attn_out, m_w_mix_out, m_norm_post_mix, m_norm_pre_ffn, m_w_ffn_up, m_ffn_conv_w, m_ffn_conv_b, m_w_ffn_down, m_norm_post_ffn, v_meta_tokens, v_norm_pre_mix, v_w_in, v_ssm_conv_w, v_ssm_conv_b, v_ssm_dt_bias, v_ssm_a_log, v_ssm_d_skip, v_ssm_norm, v_w_ssm_out, v_attn_sinks, v_w_attn_out, v_w_mix_out, v_norm_post_mix, v_norm_pre_ffn, v_w_ffn_up, v_ffn_conv_w, v_ffn_conv_b, v_w_ffn_down, v_norm_post_ffn):
    names = ("meta_tokens", "norm_pre_mix", "w_in", "ssm_conv_w", "ssm_conv_b", "ssm_dt_bias", "ssm_a_log", "ssm_d_skip",
             "ssm_norm", "w_ssm_out", "attn_sinks", "w_attn_out", "w_mix_out", "norm_post_mix", "norm_pre_ffn", "w_ffn_up",
             "ffn_conv_w", "ffn_conv_b", "w_ffn_down", "norm_post_ffn")
    w_loc = dict(zip(names, (meta_tokens, norm_pre_mix, w_in, ssm_conv_w, ssm_conv_b, ssm_dt_bias, ssm_a_log, ssm_d_skip,
                             ssm_norm, w_ssm_out, attn_sinks, w_attn_out, w_mix_out, norm_post_mix, norm_pre_ffn, w_ffn_up,
                             ffn_conv_w, ffn_conv_b, w_ffn_down, norm_post_ffn)))
    m_loc = dict(zip(names, (m_meta_tokens, m_norm_pre_mix, m_w_in, m_ssm_conv_w, m_ssm_conv_b, m_ssm_dt_bias, m_ssm_a_log,
                             m_ssm_d_skip, m_ssm_norm, m_w_ssm_out, m_attn_sinks, m_w_attn_out, m_w_mix_out, m_norm_post_mix,
                             m_norm_pre_ffn, m_w_ffn_up, m_ffn_conv_w, m_ffn_conv_b, m_w_ffn_down, m_norm_post_ffn)))
    v_loc = dict(zip(names, (v_meta_tokens, v_norm_pre_mix, v_w_in, v_ssm_conv_w, v_ssm_conv_b, v_ssm_dt_bias, v_ssm_a_log,
                             v_ssm_d_skip, v_ssm_norm, v_w_ssm_out, v_attn_sinks, v_w_attn_out, v_w_mix_out, v_norm_post_mix,
                             v_norm_pre_ffn, v_w_ffn_up, v_ffn_conv_w, v_ffn_conv_b, v_w_ffn_down, v_norm_post_ffn)))

    def local2d(d, name):
        a = d[name]
        return a if name == "meta_tokens" else a.reshape(a.shape[1:])

    my_id = 4 * lax.axis_index("x") + 2 * lax.axis_index("y") + lax.axis_index("c")
    big = {name: (shape, axis) for name, shape, axis in BIG}

    def whole(name, g):
        (r, c), axis = big[name]
        return g.reshape(N_DEV * r, c) if axis == 0 else g.transpose(1, 0, 2).reshape(r, N_DEV * c)

    small_shard_pack = jnp.concatenate([_as_rows(local2d(w_loc, name).reshape(-1)) for name, _ in SMALL_SHARDED], axis=0)
    small_shard_pack = jnp.pad(small_shard_pack, ((0, -small_shard_pack.shape[0] % 8), (0, 0)))
    first = _all_gather([local2d(w_loc, "w_in").astype(BF16), small_shard_pack])
    rest_names = [name for name, _, _ in BIG if name != "w_in"]
    rest = [local2d(w_loc, name).astype(BF16) for name in rest_names]
    rest, first = lax.optimization_barrier((rest, first))
    rest_handle = _push_start(rest, False, "gather_rest_start")
    wt = {"w_in": whole("w_in", first[0])}
    row = 0
    for name, (r, c) in SMALL_SHARDED:
        blocks = first[1][:, row:row + _rows_of(r * c)].reshape(N_DEV, -1)[:, :r * c].reshape(N_DEV, r, c)
        wt[name] = blocks.transpose(1, 0, 2).reshape(r, N_DEV * c)
        row += _rows_of(r * c)
    for name, size in SMALL_REPLICATED:
        wt[name] = w_loc[name].reshape(1, size)

    def late_weights(after):
        own, landed = _push_wait(rest_handle, after, "gather_rest_wait")
        out = {}
        for name, mine, land in zip(rest_names, own, landed):
            out[name] = whole(name, lax.dynamic_update_index_in_dim(land, mine, my_id, 0))
        return out

    sent = {}

    def on_grad(name, g):
        shape, axis = big[name]
        sent[name] = _push_start([_shard_major(g, shape, axis).astype(BF16)], True, "send_" + name)
        return sent[name]["token"]

    loss_part, grad_x, grads = _local_step(x[0], loss_target[0], wt, late_weights, on_grad, rest_handle["token"])
    loss = lax.psum(loss_part, AXES)

    small_parts = []
    for name, (r, c) in SMALL_SHARDED:
        small_parts.append(_as_rows(_shard_major(grads[name], (r, c), 1).reshape(N_DEV, r * c)))
    rep_rows = jnp.concatenate([_as_rows(grads[name].reshape(-1)) for name, _ in SMALL_REPLICATED], axis=0)
    small_send = jnp.concatenate([jnp.broadcast_to(rep_rows[None], (N_DEV,) + rep_rows.shape)] + small_parts, axis=1)
    small_send = jnp.pad(small_send, ((0, 0), (0, -small_send.shape[1] % 8), (0, 0)))
    small_received = _exchange_partials([small_send])[0]

    def small_pack(d):
        return _pack_small({name: d[name] for name, _ in SMALL_REPLICATED}, {name: local2d(d, name) for name, _ in SMALL_SHARDED})

    grad_w, delta_w, new_m, new_v = {}, {}, {}, {}
    outs = _adamw(small_received, None, small_pack(w_loc), small_pack(m_loc), small_pack(v_loc), "adamw_small")
    after = outs[0]
    for name, handle in sent.items():
        src, landed = _push_wait(handle, after, "arrived_" + name)
        own = lax.dynamic_index_in_dim(src[0], my_id, 0, keepdims=False)
        g, d, nm, nv = _adamw(landed[0], own, local2d(w_loc, name), local2d(m_loc, name), local2d(v_loc, name), "adamw_" + name)
        after = g
        full = (1,) + big[name][0]
        grad_w[name], delta_w[name], new_m[name], new_v[name] = g.reshape(full), d.reshape(full), nm.reshape(full), nv.reshape(full)
    for dst, packed in zip((grad_w, delta_w, new_m, new_v), outs):
        for name, a in _unpack_small(packed).items():
            dst[name] = a.reshape(w_loc[name].shape)

    return (loss, grad_x[None], *[grad_w[n] for n in names], *[delta_w[n] for n in names],
            *[new_m[n] for n in names], *[new_v[n] for n in names])
```

```python
import jax
import jax.numpy as jnp
from jax import lax
from jax.experimental import pallas as pl
from jax.experimental.pallas import tpu as pltpu

F32 = jnp.float32
BF16 = jnp.bfloat16
HIGHEST = lax.Precision.HIGHEST

D_MODEL = 1024
N_META = 16
CHUNK = 128
META_PAD = CHUNK - N_META
D_INNER = 2048
HEAD_P = 64
SSM_HEADS = 32
SSM_GROUPS = 4
HEADS_PER_GROUP = SSM_HEADS // SSM_GROUPS
GROUP_W = HEADS_PER_GROUP * HEAD_P
D_STATE = 128
SSM_CONV = 4
CONV_DIM = D_INNER + 2 * SSM_GROUPS * D_STATE
ATTN_HEADS = 16
KV_HEADS = 4
ATTN_GROUP = ATTN_HEADS // KV_HEADS
DH = 64
KV_W = KV_HEADS * DH
FFN_DIM = 2816
FFN_CONV = 3
EPS = 1e-6
NEG = -1e30
N_DEV = 8
AXES = ("x", "y", "c")

OFF_Z, OFF_GATE, OFF_XBC, OFF_Q, OFF_K, OFF_V, OFF_DT = 0, 2048, 4096, 7168, 8192, 8448, 8704
N_INP = OFF_DT + SSM_GROUPS * 128
CUT_Z, CUT_XBC, CUT_DT, CUT_Q, CUT_K, CUT_V, CUT_G = 0, 2048, 5120, 5152, 6176, 6432, 6688
N_IN = 8736

ADAM_LR, ADAM_B1, ADAM_B2, ADAM_EPS, ADAM_WD, ADAM_STEP = 0.001, 0.9, 0.999, 1e-08, 0.01, 10

VMEM_LIMIT = 56 * 1024 * 1024


def _params(n_grid):
    return pltpu.CompilerParams(dimension_semantics=("arbitrary",) * n_grid, vmem_limit_bytes=VMEM_LIMIT)


def _sds(shape, dtype):
    return jax.ShapeDtypeStruct(shape, dtype)


def _pick(n, prefs):
    for c in prefs:
        if n % c == 0:
            return c
    raise ValueError(f"no tile of {prefs} divides {n}")


def _row(tr, width, cb=0):
    return pl.BlockSpec((tr, width), lambda i: (i, cb))


def _row_rev(tr, width, nt, cb=0):
    return pl.BlockSpec((tr, width), lambda i: (nt - 1 - i, cb))


def _full(shape):
    return pl.BlockSpec(shape, lambda *_: (0,) * len(shape))


def _sigmoid(x):
    return 1.0 / (1.0 + jnp.exp(-x))


def _softplus(x):
    return jnp.maximum(x, 0.0) + jnp.log(1.0 + jnp.exp(-jnp.abs(x)))


def _rms(x):
    return lax.rsqrt(jnp.mean(x * x, axis=-1, keepdims=True) + EPS)


def _rms_bwd(x, r, w, dy):
    xh = x * r
    g = dy * w
    dx = r * (g - xh * jnp.mean(g * xh, axis=-1, keepdims=True))
    return dx, jnp.sum(dy * xh, axis=0, keepdims=True)


def _row_ids(shape, tile_index, tr):
    return tile_index * tr + lax.broadcasted_iota(jnp.int32, shape, 0)


def _shift_down(cur, prev, s):
    if s == 0:
        return cur
    row = lax.broadcasted_iota(jnp.int32, cur.shape, 0)
    return jnp.where(row < s, pltpu.roll(prev, s, 0), pltpu.roll(cur, s, 0))


def _shift_up(cur, nxt, s):
    if s == 0:
        return cur
    n = cur.shape[0]
    row = lax.broadcasted_iota(jnp.int32, cur.shape, 0)
    return jnp.where(row >= n - s, pltpu.roll(nxt, n - s, 0), pltpu.roll(cur, n - s, 0))


def _matmul(a, b, *, ta=False, tb=False, out_dtype=F32, name, after=None):
    if ta:
        k_dim, m_dim = a.shape
    else:
        m_dim, k_dim = a.shape
    n_dim = b.shape[0] if tb else b.shape[1]
    tm = _pick(m_dim, (1408, 1024, 768, 512, 384, 256, 128))
    tn = _pick(n_dim, (1024, 1408, 768, 512, 384, 256, 128))
    tk = k_dim if (not ta and k_dim <= 2816) else _pick(k_dim, (1408, 1024, 768, 512, 384, 256, 128))
    nk = k_dim // tk
    dims = (((0 if ta else 1,), (1 if tb else 0,)), ((), ()))

    use_acc = nk > 1 and out_dtype != F32

    def body(a_ref, b_ref, *rest):
        o_ref = rest[-2] if use_acc else rest[-1]
        acc_ref = rest[-1] if use_acc else o_ref
        r = lax.dot_general(a_ref[...].astype(BF16), b_ref[...].astype(BF16), dims, preferred_element_type=F32)
        if nk == 1:
            o_ref[...] = r.astype(o_ref.dtype)
        else:
            k = pl.program_id(2)

            @pl.when(k == 0)
            def _():
                acc_ref[...] = r

            @pl.when(k > 0)
            def _():
                acc_ref[...] += r

            if use_acc:
                @pl.when(k == nk - 1)
                def _():
                    o_ref[...] = acc_ref[...].astype(o_ref.dtype)

    a_spec = pl.BlockSpec((tk, tm), lambda i, j, k: (k, i)) if ta else pl.BlockSpec((tm, tk), lambda i, j, k: (i, k))
    b_spec = pl.BlockSpec((tn, tk), lambda i, j, k: (j, k)) if tb else pl.BlockSpec((tk, tn), lambda i, j, k: (k, j))
    extra_specs, extra = ([], ()) if after is None else ([pl.BlockSpec(memory_space=pl.ANY)], (after,))
    return pl.pallas_call(
        body, grid=(m_dim // tm, n_dim // tn, nk), in_specs=[a_spec, b_spec] + extra_specs,
        out_specs=pl.BlockSpec((tm, tn), lambda i, j, k: (i, j)), out_shape=_sds((m_dim, n_dim), out_dtype),
        scratch_shapes=[pltpu.VMEM((tm, tn), F32)] if use_acc else [],
        name=name, compiler_params=_params(3))(a, b, *extra)


def _prenorm(h, w):
    t_rows = h.shape[0]
    tr = _pick(t_rows, (384, 128))

    def body(h_ref, w_ref, o_ref):
        x = h_ref[...]
        o_ref[...] = (x * _rms(x) * w_ref[...]).astype(BF16)

    return pl.pallas_call(body, grid=(t_rows // tr,), in_specs=[_row(tr, D_MODEL), _full((1, D_MODEL))],
                          out_specs=_row(tr, D_MODEL), out_shape=_sds((t_rows, D_MODEL), BF16),
                          name="prenorm", compiler_params=_params(1))(h, w)


def _xbc_specs(tr, rev_nt=None):
    cbs = [OFF_XBC // 1024 + j for j in range(CONV_DIM // 1024)]
    if rev_nt is None:
        return [_row(tr, 1024, cb) for cb in cbs]
    return [_row_rev(tr, 1024, rev_nt, cb) for cb in cbs]


def _ssm_conv_fwd(proj, conv_w, conv_b):
    t_rows = proj.shape[0]
    tr = CHUNK

    def body(x0, x1, x2, w_ref, b_ref, xc_ref, xa_ref, prev):
        @pl.when(pl.program_id(0) == 0)
        def _():
            prev[...] = jnp.zeros_like(prev)

        x = jnp.concatenate([x0[...], x1[...], x2[...]], axis=1)
        p = prev[...]
        acc = b_ref[...] + w_ref[SSM_CONV - 1:SSM_CONV, :] * x
        for s in range(1, SSM_CONV):
            acc = acc + w_ref[SSM_CONV - 1 - s:SSM_CONV - s, :] * _shift_down(x, p, s)
        prev[...] = x
        xc_ref[...] = acc
        xa_ref[...] = acc * _sigmoid(acc)

    return pl.pallas_call(
        body, grid=(t_rows // tr,),
        in_specs=_xbc_specs(tr) + [_full((SSM_CONV, CONV_DIM)), _full((1, CONV_DIM))],
        out_specs=[_row(tr, CONV_DIM), _row(tr, CONV_DIM)],
        out_shape=[_sds((t_rows, CONV_DIM), F32), _sds((t_rows, CONV_DIM), F32)],
        scratch_shapes=[pltpu.VMEM((tr, CONV_DIM), F32)],
        name="ssm_conv_fwd", compiler_params=_params(1))(proj, proj, proj, conv_w, conv_b)


def _ssm_post(y, proj, w):
    t_rows = y.shape[0]
    tr = CHUNK

    def body(y_ref, z_ref, w_ref, o_ref):
        z = z_ref[...]
        yz = y_ref[...] * z * _sigmoid(z)
        o_ref[...] = (yz * _rms(yz) * w_ref[...]).astype(BF16)

    return pl.pallas_call(body, grid=(t_rows // tr,),
                          in_specs=[_row(tr, D_INNER), _row(tr, D_INNER, OFF_Z // D_INNER), _full((1, D_INNER))],
                          out_specs=_row(tr, D_INNER), out_shape=_sds((t_rows, D_INNER), BF16),
                          name="ssm_post", compiler_params=_params(1))(y, proj, w)


def _mix_fwd(proj, y_ssm, y_attn):
    t_rows = y_ssm.shape[0]
    tr = _pick(t_rows, (384, 128))

    def body(g_ref, ys_ref, ya_ref, o_ref):
        g = _sigmoid(g_ref[...])
        o_ref[...] = (g[:, :D_MODEL] * ys_ref[...] + g[:, D_MODEL:] * ya_ref[...]).astype(BF16)

    return pl.pallas_call(body, grid=(t_rows // tr,),
                          in_specs=[_row(tr, 2 * D_MODEL, OFF_GATE // (2 * D_MODEL)), _row(tr, D_MODEL), _row(tr, D_MODEL)],
                          out_specs=_row(tr, D_MODEL), out_shape=_sds((t_rows, D_MODEL), BF16),
                          name="mix_fwd", compiler_params=_params(1))(proj, y_ssm, y_attn)


def _postmix(h, mix, w_post, w_pre):
    t_rows = h.shape[0]
    tr = CHUNK

    def body(h_ref, m_ref, wp_ref, wf_ref, h1_ref, hn_ref):
        m = m_ref[...]
        h1 = h_ref[...] + m * _rms(m) * wp_ref[...]
        h1 = jnp.where(_row_ids(h1.shape, pl.program_id(0), tr) >= META_PAD, h1, 0.0)
        h1_ref[...] = h1
        hn_ref[...] = (h1 * _rms(h1) * wf_ref[...]).astype(BF16)

    return pl.pallas_call(body, grid=(t_rows // tr,),
                          in_specs=[_row(tr, D_MODEL), _row(tr, D_MODEL), _full((1, D_MODEL)), _full((1, D_MODEL))],
                          out_specs=[_row(tr, D_MODEL), _row(tr, D_MODEL)],
                          out_shape=[_sds((t_rows, D_MODEL), F32), _sds((t_rows, D_MODEL), BF16)],
                          name="postmix", compiler_params=_params(1))(h, mix, w_post, w_pre)


def _ffn_act(up, conv_w, conv_b):
    t_rows = up.shape[0]
    tr = CHUNK
    width = 2 * FFN_DIM

    def body(up_ref, w_ref, b_ref, u_ref, act_ref, prev):
        @pl.when(pl.program_id(0) == 0)
        def _():
            prev[...] = jnp.zeros_like(prev)

        x = up_ref[...]
        p = prev[...]
        u = b_ref[...] + w_ref[FFN_CONV - 1:FFN_CONV, :] * x
        for s in range(1, FFN_CONV):
            u = u + w_ref[FFN_CONV - 1 - s:FFN_CONV - s, :] * _shift_down(x, p, s)
        prev[...] = x
        u_ref[...] = u
        a = u[:, :FFN_DIM]
        act_ref[...] = (a * _sigmoid(a) * u[:, FFN_DIM:]).astype(BF16)

    return pl.pallas_call(
        body, grid=(t_rows // tr,), in_specs=[_row(tr, width), _full((FFN_CONV, width)), _full((1, width))],
        out_specs=[_row(tr, width), _row(tr, FFN_DIM)],
        out_shape=[_sds((t_rows, width), F32), _sds((t_rows, FFN_DIM), BF16)],
        scratch_shapes=[pltpu.VMEM((tr, width), F32)],
        name="ffn_act", compiler_params=_params(1))(up, conv_w, conv_b)


def _final(h1, f, target, w):
    t_rows = h1.shape[0]
    tr = CHUNK

    def body(h1_ref, f_ref, t_ref, w_ref, df_ref, dy_ref, dw_ref, loss_ref):
        i = pl.program_id(0)

        @pl.when(i == 0)
        def _():
            dw_ref[...] = jnp.zeros_like(dw_ref)
            loss_ref[...] = jnp.zeros_like(loss_ref)

        f_val = f_ref[...]
        r = _rms(f_val)
        wv = w_ref[...]
        h2 = h1_ref[...] + f_val * r * wv
        diff = jnp.where(i >= 1, h2 - t_ref[...], 0.0)
        loss_ref[...] += 0.5 * jnp.sum(diff * diff) * (1.0 / D_MODEL)
        dy = diff * (1.0 / D_MODEL)
        dy_ref[...] = dy
        df, dw = _rms_bwd(f_val, r, wv, dy)
        df_ref[...] = df.astype(BF16)
        dw_ref[...] += dw

    tgt_spec = pl.BlockSpec((tr, D_MODEL), lambda i: (jnp.maximum(i - 1, 0), 0))
    return pl.pallas_call(
        body, grid=(t_rows // tr,),
        in_specs=[_row(tr, D_MODEL), _row(tr, D_MODEL), tgt_spec, _full((1, D_MODEL))],
        out_specs=[_row(tr, D_MODEL), _row(tr, D_MODEL), _full((1, D_MODEL)), _full((1, 128))],
        out_shape=[_sds((t_rows, D_MODEL), BF16), _sds((t_rows, D_MODEL), F32), _sds((1, D_MODEL), F32), _sds((1, 128), F32)],
        name="final", compiler_params=_params(1))(h1, f, target, w)


def _ffn_act_bwd(u, up, dact, conv_w):
    t_rows = u.shape[0]
    tr = CHUNK
    nt = t_rows // tr
    width = 2 * FFN_DIM

    def body(u_ref, up_ref, da_ref, w_ref, dup_ref, dw_ref, db_ref, nxt):
        @pl.when(pl.program_id(0) == 0)
        def _():
            nxt[...] = jnp.zeros_like(nxt)
            dw_ref[...] = jnp.zeros_like(dw_ref)
            db_ref[...] = jnp.zeros_like(db_ref)

        u_val = u_ref[...]
        a, g = u_val[:, :FFN_DIM], u_val[:, FFN_DIM:]
        d = da_ref[...]
        s = _sigmoid(a)
        du = jnp.concatenate([d * g * s * (1.0 + a * (1.0 - s)), d * a * s], axis=1)
        n = nxt[...]
        x = up_ref[...]
        dup = jnp.zeros_like(du)
        for sh in range(FFN_CONV):
            k = FFN_CONV - 1 - sh
            moved = _shift_up(du, n, sh)
            dup = dup + w_ref[k:k + 1, :] * moved
            dw_ref[k:k + 1, :] += jnp.sum(moved * x, axis=0, keepdims=True)
        db_ref[...] += jnp.sum(du, axis=0, keepdims=True)
        nxt[...] = du
        dup_ref[...] = dup.astype(BF16)

    return pl.pallas_call(
        body, grid=(nt,),
        in_specs=[_row_rev(tr, width, nt), _row_rev(tr, width, nt), _row_rev(tr, FFN_DIM, nt), _full((FFN_CONV, width))],
        out_specs=[_row_rev(tr, width, nt), _full((FFN_CONV, width)), _full((1, width))],
        out_shape=[_sds((t_rows, width), BF16), _sds((FFN_CONV, width), F32), _sds((1, width), F32)],
        scratch_shapes=[pltpu.VMEM((tr, width), F32)],
        name="ffn_act_bwd", compiler_params=_params(1))(u, up, dact, conv_w)


def _postmix_bwd(h1, dhn2, dy, mix, w_pre, w_post):
    t_rows = h1.shape[0]
    tr = CHUNK

    def body(h1_ref, dhn_ref, dy_ref, m_ref, wf_ref, wp_ref, dmix_ref, dh_ref, dwf_ref, dwp_ref):
        @pl.when(pl.program_id(0) == 0)
        def _():
            dwf_ref[...] = jnp.zeros_like(dwf_ref)
            dwp_ref[...] = jnp.zeros_like(dwp_ref)

        h1v = h1_ref[...]
        dx, dwf = _rms_bwd(h1v, _rms(h1v), wf_ref[...], dhn_ref[...])
        dwf_ref[...] += dwf
        dh1 = dy_ref[...] + dx
        dh1 = jnp.where(_row_ids(dh1.shape, pl.program_id(0), tr) >= META_PAD, dh1, 0.0)
        dh_ref[...] = dh1
        m = m_ref[...]
        dmix, dwp = _rms_bwd(m, _rms(m), wp_ref[...], dh1)
        dwp_ref[...] += dwp
        dmix_ref[...] = dmix.astype(BF16)

    return pl.pallas_call(
        body, grid=(t_rows // tr,),
        in_specs=[_row(tr, D_MODEL)] * 4 + [_full((1, D_MODEL))] * 2,
        out_specs=[_row(tr, D_MODEL), _row(tr, D_MODEL), _full((1, D_MODEL)), _full((1, D_MODEL))],
        out_shape=[_sds((t_rows, D_MODEL), BF16), _sds((t_rows, D_MODEL), F32), _sds((1, D_MODEL), F32), _sds((1, D_MODEL), F32)],
        name="postmix_bwd", compiler_params=_params(1))(h1, dhn2, dy, mix, w_pre, w_post)


def _mix_bwd(dmixed, proj, y_ssm, y_attn):
    t_rows = dmixed.shape[0]
    tr = _pick(t_rows, (384, 128))

    def body(d_ref, g_ref, ys_ref, ya_ref, dys_ref, dya_ref, dg_ref):
        d = d_ref[...]
        g = _sigmoid(g_ref[...])
        g1, g2 = g[:, :D_MODEL], g[:, D_MODEL:]
        dys_ref[...] = (d * g1).astype(BF16)
        dya_ref[...] = (d * g2).astype(BF16)
        dg_ref[...] = jnp.concatenate([d * ys_ref[...] * g1 * (1.0 - g1), d * ya_ref[...] * g2 * (1.0 - g2)],
                                      axis=1).astype(BF16)

    return pl.pallas_call(
        body, grid=(t_rows // tr,),
        in_specs=[_row(tr, D_MODEL), _row(tr, 2 * D_MODEL, OFF_GATE // (2 * D_MODEL)), _row(tr, D_MODEL), _row(tr, D_MODEL)],
        out_specs=[_row(tr, D_MODEL), _row(tr, D_MODEL), _row(tr, 2 * D_MODEL)],
        out_shape=[_sds((t_rows, D_MODEL), BF16), _sds((t_rows, D_MODEL), BF16), _sds((t_rows, 2 * D_MODEL), BF16)],
        name="mix_bwd", compiler_params=_params(1))(dmixed, proj, y_ssm, y_attn)


def _ssm_post_bwd(y, proj, dyn, w):
    t_rows = y.shape[0]
    tr = CHUNK

    def body(y_ref, z_ref, d_ref, w_ref, dy_ref, dz_ref, dw_ref):
        @pl.when(pl.program_id(0) == 0)
        def _():
            dw_ref[...] = jnp.zeros_like(dw_ref)

        yv, z = y_ref[...], z_ref[...]
        sz = _sigmoid(z)
        silu = z * sz
        yz = yv * silu
        dyz, dw = _rms_bwd(yz, _rms(yz), w_ref[...], d_ref[...])
        dw_ref[...] += dw
        dy_ref[...] = dyz * silu
        dz_ref[...] = (dyz * yv * sz * (1.0 + z * (1.0 - sz))).astype(BF16)

    return pl.pallas_call(
        body, grid=(t_rows // tr,),
        in_specs=[_row(tr, D_INNER), _row(tr, D_INNER, OFF_Z // D_INNER), _row(tr, D_INNER), _full((1, D_INNER))],
        out_specs=[_row(tr, D_INNER), _row(tr, D_INNER), _full((1, D_INNER))],
        out_shape=[_sds((t_rows, D_INNER), F32), _sds((t_rows, D_INNER), BF16), _sds((1, D_INNER), F32)],
        name="ssm_post_bwd", compiler_params=_params(1))(y, proj, dyn, w)


def _ssm_conv_bwd(xc, proj, dxs, dbm, dcm, conv_w):
    t_rows = xc.shape[0]
    tr = CHUNK
    nt = t_rows // tr
    bc_w = SSM_GROUPS * D_STATE

    def body(xc_ref, x0, x1, x2, dxs_ref, db_ref, dc_ref, w_ref, dx_ref, dw_ref, dbias_ref, nxt):
        @pl.when(pl.program_id(0) == 0)
        def _():
            nxt[...] = jnp.zeros_like(nxt)
            dw_ref[...] = jnp.zeros_like(dw_ref)
            dbias_ref[...] = jnp.zeros_like(dbias_ref)

        c = xc_ref[...]
        s = _sigmoid(c)
        dact = jnp.concatenate([dxs_ref[...], db_ref[...], dc_ref[...]], axis=1)
        dpre = dact * s * (1.0 + c * (1.0 - s))
        x = jnp.concatenate([x0[...], x1[...], x2[...]], axis=1)
        n = nxt[...]
        dx = jnp.zeros_like(dpre)
        for sh in range(SSM_CONV):
            k = SSM_CONV - 1 - sh
            moved = _shift_up(dpre, n, sh)
            dx = dx + w_ref[k:k + 1, :] * moved
            dw_ref[k:k + 1, :] += jnp.sum(moved * x, axis=0, keepdims=True)
        dbias_ref[...] += jnp.sum(dpre, axis=0, keepdims=True)
        nxt[...] = dpre
        dx_ref[...] = dx.astype(BF16)

    return pl.pallas_call(
        body, grid=(nt,),
        in_specs=[_row_rev(tr, CONV_DIM, nt)] + _xbc_specs(tr, nt)
        + [_row_rev(tr, D_INNER, nt), _row_rev(tr, bc_w, nt), _row_rev(tr, bc_w, nt), _full((SSM_CONV, CONV_DIM))],
        out_specs=[_row_rev(tr, CONV_DIM, nt), _full((SSM_CONV, CONV_DIM)), _full((1, CONV_DIM))],
        out_shape=[_sds((t_rows, CONV_DIM), BF16), _sds((SSM_CONV, CONV_DIM), F32), _sds((1, CONV_DIM), F32)],
        scratch_shapes=[pltpu.VMEM((tr, CONV_DIM), F32)],
        name="ssm_conv_bwd", compiler_params=_params(1))(xc, proj, proj, proj, dxs, dbm, dcm, conv_w)


def _prenorm_bwd(h, dhn, dh, w):
    t_rows = h.shape[0]
    tr = CHUNK

    def body(h_ref, d_ref, r_ref, w_ref, o_ref, dw_ref):
        @pl.when(pl.program_id(0) == 0)
        def _():
            dw_ref[...] = jnp.zeros_like(dw_ref)

        x = h_ref[...]
        dx, dw = _rms_bwd(x, _rms(x), w_ref[...], d_ref[...])
        dw_ref[...] += dw
        o_ref[...] = r_ref[...] + dx

    return pl.pallas_call(
        body, grid=(t_rows // tr,), in_specs=[_row(tr, D_MODEL)] * 3 + [_full((1, D_MODEL))],
        out_specs=[_row(tr, D_MODEL), _full((1, D_MODEL))],
        out_shape=[_sds((t_rows, D_MODEL), F32), _sds((1, D_MODEL), F32)],
        name="prenorm_bwd", compiler_params=_params(1))(h, dhn, dh, w)


def _dot01(x, m01, x_left, parts):
    acc, rest = None, x
    for i in range(parts):
        piece = rest.astype(BF16)
        term = (jnp.dot(piece, m01, preferred_element_type=F32) if x_left
                else jnp.dot(m01, piece, preferred_element_type=F32))
        acc = term if acc is None else acc + term
        if i + 1 < parts:
            rest = rest - piece.astype(F32)
    return acc


def _ssd_common(dtr_ref, dtb_ref, alog_ref, chunk_index):
    rows = lax.broadcasted_iota(jnp.int32, (CHUNK, CHUNK), 0)
    cols = lax.broadcasted_iota(jnp.int32, (CHUNK, CHUNK), 1)
    low = rows >= cols
    raw = dtr_ref[...] + dtb_ref[0]
    live = _row_ids(raw.shape, chunk_index, CHUNK) >= META_PAD
    dt = jnp.where(live, _softplus(raw), 0.0)
    a_head = -jnp.exp(alog_ref[0])
    cs = _dot01(dt * a_head, low.astype(BF16), False, 3)
    grow = jnp.exp(cs)
    fade = jnp.exp(cs[CHUNK - 1:CHUNK, :] - cs)
    expand = (lax.broadcasted_iota(jnp.int32, (CHUNK, GROUP_W), 1) // HEAD_P
              == lax.broadcasted_iota(jnp.int32, (CHUNK, GROUP_W), 0)).astype(BF16)
    fold = (lax.broadcasted_iota(jnp.int32, (GROUP_W, CHUNK), 0) // HEAD_P
            == lax.broadcasted_iota(jnp.int32, (GROUP_W, CHUNK), 1)).astype(BF16)
    return dict(low=low, triu=(rows <= cols).astype(BF16), raw=raw, live=live, dt=dt, a_head=a_head, cs=cs, cs_t=cs.T,
                fold=fold, dtx=_dot01(dt, expand, True, 2), growx=_dot01(grow, expand, True, 2),
                fadex=_dot01(fade, expand, True, 2))


def _decay_matrix(cm, j):
    diff = cm["cs"][:, j:j + 1] - cm["cs_t"][j:j + 1, :]
    return jnp.where(cm["low"], jnp.exp(jnp.where(cm["low"], diff, 0.0)), 0.0)


def _dot(a, b, dims):
    return lax.dot_general(a.astype(BF16), b.astype(BF16), (dims, ((), ())), preferred_element_type=F32)


def _ssd_specs(nt, rev):
    def idx(c):
        return nt - 1 - c if rev else c
    xs = pl.BlockSpec((CHUNK, GROUP_W), lambda g, c: (idx(c), g))
    bm = pl.BlockSpec((CHUNK, D_STATE), lambda g, c: (idx(c), D_INNER // D_STATE + g))
    cm = pl.BlockSpec((CHUNK, D_STATE), lambda g, c: (idx(c), D_INNER // D_STATE + SSM_GROUPS + g))
    dtr = pl.BlockSpec((CHUNK, 128), lambda g, c: (idx(c), OFF_DT // 128 + g))
    par = pl.BlockSpec((1, 1, 128), lambda g, c: (g, 0, 0))
    par_x = pl.BlockSpec((1, 1, GROUP_W), lambda g, c: (g, 0, 0))
    return xs, bm, cm, dtr, par, par_x, idx


def _ssd_fwd(xact, proj, dtb, alog, dskip_x):
    t_rows = xact.shape[0]
    nt = t_rows // CHUNK
    xs_spec, b_spec, c_spec, dtr_spec, par, par_x, _ = _ssd_specs(nt, False)

    def body(xs_ref, b_ref, c_ref, dtr_ref, dtb_ref, alog_ref, dsk_ref, y_ref, hst_ref, state):
        c = pl.program_id(1)

        @pl.when(c == 0)
        def _():
            state[...] = jnp.zeros_like(state)

        cm = _ssd_common(dtr_ref, dtb_ref, alog_ref, c)
        xs, bm, cmat = xs_ref[...], b_ref[...], c_ref[...]
        x_dt = xs * cm["dtx"]
        h_in = state[...]
        hst_ref[0, 0] = h_in
        y_ref[...] = _dot(cmat, h_in, ((1,), (0,))) * cm["growx"] + xs * dsk_ref[0]
        cb = _dot(cmat, bm, ((1,), (1,)))
        for j in range(HEADS_PER_GROUP):
            sl = slice(j * HEAD_P, (j + 1) * HEAD_P)
            y_ref[:, sl] += _dot(cb * _decay_matrix(cm, j), x_dt[:, sl], ((1,), (0,)))
        state[...] = h_in * cm["growx"][CHUNK - 1:CHUNK, :] + _dot(bm, x_dt * cm["fadex"], ((0,), (0,)))

    return pl.pallas_call(
        body, grid=(SSM_GROUPS, nt),
        in_specs=[xs_spec, b_spec, c_spec, dtr_spec, par, par, par_x],
        out_specs=[xs_spec, pl.BlockSpec((1, 1, D_STATE, GROUP_W), lambda g, c: (c, g, 0, 0))],
        out_shape=[_sds((t_rows, D_INNER), F32), _sds((nt, SSM_GROUPS, D_STATE, GROUP_W), F32)],
        scratch_shapes=[pltpu.VMEM((D_STATE, GROUP_W), F32)],
        name="ssd_fwd", compiler_params=_params(2))(xact, xact, xact, proj, dtb, alog, dskip_x)


def _ssd_bwd(xact, proj, dtb, alog, dskip_x, dy, hst):
    t_rows = xact.shape[0]
    nt = t_rows // CHUNK
    xs_spec, b_spec, c_spec, dtr_spec, par, par_x, idx = _ssd_specs(nt, True)
    h_spec = pl.BlockSpec((1, 1, D_STATE, GROUP_W), lambda g, c: (idx(c), g, 0, 0))
    hn_spec = pl.BlockSpec((1, 1, D_STATE, GROUP_W), lambda g, c: (jnp.minimum(idx(c) + 1, nt - 1), g, 0, 0))
    bc_out = pl.BlockSpec((CHUNK, D_STATE), lambda g, c: (idx(c), g))

    def body(xs_ref, b_ref, c_ref, dtr_ref, dtb_ref, alog_ref, dsk_ref, dy_ref, h_ref, hn_ref,
             dxs_ref, db_ref, dc_ref, ddt_ref, dalog_ref, ddtb_ref, dd_ref, dstate, dx_buf):
        step = pl.program_id(1)

        @pl.when(step == 0)
        def _():
            dstate[...] = jnp.zeros_like(dstate)
            dalog_ref[...] = jnp.zeros_like(dalog_ref)
            ddtb_ref[...] = jnp.zeros_like(ddtb_ref)
            dd_ref[...] = jnp.zeros_like(dd_ref)

        cm = _ssd_common(dtr_ref, dtb_ref, alog_ref, idx(step))
        xs, bm, cmat = xs_ref[...], b_ref[...], c_ref[...]
        dsk = dsk_ref[0]
        x_dt = xs * cm["dtx"]
        h_in, h_next = h_ref[0, 0], hn_ref[0, 0]
        dyv = dy_ref[...]
        dh = dstate[...]
        grow, fade = cm["growx"], cm["fadex"]
        dy_grow = dyv * grow
        x_fade = x_dt * fade
        cb = _dot(cmat, bm, ((1,), (1,)))
        ml = jnp.zeros((CHUNK, CHUNK), F32)
        row_id = lax.broadcasted_iota(jnp.int32, (CHUNK, CHUNK), 0)
        col_id = lax.broadcasted_iota(jnp.int32, (CHUNK, CHUNK), 1)
        w_rows = jnp.zeros((CHUNK, CHUNK), F32)
        w_cols = jnp.zeros((CHUNK, CHUNK), F32)
        for j in range(HEADS_PER_GROUP):
            sl = slice(j * HEAD_P, (j + 1) * HEAD_P)
            lm = _decay_matrix(cm, j)
            mlj = _dot(dyv[:, sl], x_dt[:, sl], ((1,), (1,))) * lm
            ml = ml + mlj
            wm = mlj * cb
            w_rows = jnp.where(col_id == j, jnp.sum(wm, axis=1, keepdims=True), w_rows)
            w_cols = jnp.where(row_id == j, jnp.sum(wm, axis=0, keepdims=True), w_cols)
            dx_buf[:, sl] = _dot(cb * lm, dyv[:, sl], ((0,), (0,)))
        dx_off = fade * _dot(bm, dh, ((1,), (0,)))
        dx = dx_buf[...] + dx_off
        dc_ref[...] = _dot(ml, bm, ((1,), (0,))) + _dot(dy_grow, h_in, ((1,), (1,)))
        db_ref[...] = _dot(ml, cmat, ((0,), (0,))) + _dot(x_fade, dh, ((1,), (1,)))
        fold = cm["fold"]
        y_off = _dot(cmat, h_in, ((1,), (0,))) * grow
        dcs = (w_rows - w_cols.T) + _dot01(dyv * y_off - x_dt * dx_off, fold, True, 2)
        tail = jnp.broadcast_to(jnp.sum(dh * h_next, axis=0, keepdims=True), (8, GROUP_W))
        tail = _dot01(tail, fold, True, 2)[0:1, :]
        last_row = lax.broadcasted_iota(jnp.int32, (CHUNK, 128), 0) == CHUNK - 1
        dcs = dcs + jnp.where(last_row, tail, 0.0)
        da = _dot01(dcs, cm["triu"], False, 3)
        ddt = da * cm["a_head"] + _dot01(dx * xs, fold, True, 2)
        ddt_raw = jnp.where(cm["live"], ddt * _sigmoid(cm["raw"]), 0.0)
        ddt_ref[...] = ddt_raw.astype(BF16)
        ddtb_ref[0] += jnp.sum(ddt_raw, axis=0, keepdims=True)
        dalog_ref[0] += jnp.sum(da * cm["dt"], axis=0, keepdims=True) * cm["a_head"]
        dd_ref[0] += jnp.sum(dyv * xs, axis=0, keepdims=True)
        dxs_ref[...] = dx * cm["dtx"] + dyv * dsk
        dstate[...] = dh * grow[CHUNK - 1:CHUNK, :] + _dot(cmat, dy_grow, ((0,), (0,)))

    return pl.pallas_call(
        body, grid=(SSM_GROUPS, nt),
        in_specs=[xs_spec, b_spec, c_spec, dtr_spec, par, par, par_x, xs_spec, h_spec, hn_spec],
        out_specs=[xs_spec, bc_out, bc_out, bc_out, par, par, par_x],
        out_shape=[_sds((t_rows, D_INNER), F32), _sds((t_rows, SSM_GROUPS * D_STATE), F32),
                   _sds((t_rows, SSM_GROUPS * D_STATE), F32), _sds((t_rows, SSM_GROUPS * 128), BF16),
                   _sds((SSM_GROUPS, 1, 128), F32), _sds((SSM_GROUPS, 1, 128), F32), _sds((SSM_GROUPS, 1, GROUP_W), F32)],
        scratch_shapes=[pltpu.VMEM((D_STATE, GROUP_W), F32), pltpu.VMEM((CHUNK, GROUP_W), F32)],
        name="ssd_bwd", compiler_params=_params(2))(xact, xact, xact, proj, dtb, alog, dskip_x, dy, hst, hst)


def _swa_bias():
    rows_q = ATTN_GROUP * CHUNK
    dist = (jnp.arange(rows_q) % CHUNK)[:, None] - jnp.arange(2 * CHUNK)[None, :] + CHUNK
    head = jnp.arange(KV_HEADS)[:, None] * ATTN_GROUP + jnp.arange(rows_q)[None, :] // CHUNK + 1
    slope = jnp.exp2(-8.0 * head.astype(F32) / ATTN_HEADS)
    return jnp.where((dist >= 0) & (dist < CHUNK), -slope[:, :, None] * dist.astype(F32)[None], NEG)


def _swa_probs(q_kv, k_prev, k_cur, k_first, sink, bias, n):
    rows_q = ATTN_GROUP * CHUNK
    qs = jnp.concatenate([q_kv[:, g * DH:(g + 1) * DH] for g in range(ATTN_GROUP)], axis=0) * (DH ** -0.5)
    kcat = jnp.concatenate([k_prev, k_cur], axis=0)
    kmeta = k_first[META_PAD:, :]
    key_ok = lax.broadcasted_iota(jnp.int32, (1, 2 * CHUNK), 1) + n * CHUNK >= 2 * CHUNK
    s_band = jnp.where(key_ok, _dot(qs, kcat, ((1,), (1,))) + bias, NEG)
    q_pos = lax.broadcasted_iota(jnp.int32, (rows_q, N_META), 0) % CHUNK + n * CHUNK - META_PAD
    ok_m = lax.broadcasted_iota(jnp.int32, (rows_q, N_META), 1) <= q_pos
    s_meta = jnp.where(ok_m, _dot(qs, kmeta, ((1,), (1,))), NEG)
    m = jnp.maximum(jnp.maximum(jnp.max(s_band, axis=1, keepdims=True), jnp.max(s_meta, axis=1, keepdims=True)), sink)
    p_band, p_meta, p_sink = jnp.exp(s_band - m), jnp.exp(s_meta - m), jnp.exp(sink - m)
    inv = 1.0 / (jnp.sum(p_band, axis=1, keepdims=True) + jnp.sum(p_meta, axis=1, keepdims=True) + p_sink)
    return qs, kcat, kmeta, p_band * inv, p_meta * inv, p_sink * inv


def _swa_specs(nt, rev):
    def idx(n):
        return nt - 1 - n if rev else n
    width = ATTN_HEADS * DH
    o = pl.BlockSpec((CHUNK, width), lambda n: (idx(n), 0))
    q_proj = pl.BlockSpec((CHUNK, width), lambda n: (idx(n), OFF_Q // width))
    def kv(col0, chunk_of):
        return pl.BlockSpec((CHUNK, KV_W), lambda n: (chunk_of(idx(n)), col0 // KV_W))

    chunks = (lambda c: jnp.maximum(c - 1, 0)), (lambda c: c), (lambda c: 0)
    k_specs = [kv(OFF_K, f) for f in chunks]
    v_specs = [kv(OFF_V, f) for f in chunks]
    dkv = pl.BlockSpec((CHUNK, KV_W), lambda n: (idx(n), 0))
    sink = _full((KV_HEADS, ATTN_GROUP * CHUNK, 1))
    bias = _full((KV_HEADS, ATTN_GROUP * CHUNK, 2 * CHUNK))
    return o, q_proj, k_specs, v_specs, dkv, sink, bias, idx


def _swa_fwd(proj, sink_rows, bias):
    t_rows = proj.shape[0]
    nt = t_rows // CHUNK
    o_spec, q_spec, k_specs, v_specs, _, sink_spec, bias_spec, _ = _swa_specs(nt, False)
    kv_w = ATTN_GROUP * DH

    def body(q_ref, kp_ref, kc_ref, km_ref, vp_ref, vc_ref, vm_ref, sink_ref, bias_ref, o_ref):
        n = pl.program_id(0)
        for k in range(KV_HEADS):
            hd = slice(k * DH, (k + 1) * DH)
            _, _, _, p_band, p_meta, _ = _swa_probs(q_ref[:, k * kv_w:(k + 1) * kv_w], kp_ref[:, hd], kc_ref[:, hd],
                                                    km_ref[:, hd], sink_ref[k], bias_ref[k], n)
            vcat = jnp.concatenate([vp_ref[:, hd], vc_ref[:, hd]], axis=0)
            out = _dot(p_band, vcat, ((1,), (0,))) + _dot(p_meta, vm_ref[:, hd][META_PAD:, :], ((1,), (0,)))
            for g in range(ATTN_GROUP):
                o_ref[:, k * kv_w + g * DH:k * kv_w + (g + 1) * DH] = out[g * CHUNK:(g + 1) * CHUNK, :]

    return pl.pallas_call(
        body, grid=(nt,), in_specs=[q_spec] + k_specs + v_specs + [sink_spec, bias_spec],
        out_specs=o_spec, out_shape=_sds((t_rows, ATTN_HEADS * DH), F32),
        name="swa_fwd", compiler_params=_params(1))(proj, proj, proj, proj, proj, proj, proj, sink_rows, bias)


def _swa_bwd(proj, sink_rows, bias, out, dout):
    t_rows = proj.shape[0]
    nt = t_rows // CHUNK
    o_spec, q_spec, k_specs, v_specs, dkv_spec, sink_spec, bias_spec, idx = _swa_specs(nt, True)
    kv_w = ATTN_GROUP * DH

    def body(q_ref, kp_ref, kc_ref, km_ref, vp_ref, vc_ref, vm_ref, sink_ref, bias_ref, o_ref, do_ref,
             dq_ref, dk_ref, dv_ref, dsink_ref, carry_k, carry_v, meta_k, meta_v, dk_buf, dv_buf):
        step = pl.program_id(0)
        n = idx(step)

        @pl.when(step == 0)
        def _():
            carry_k[...] = jnp.zeros_like(carry_k)
            carry_v[...] = jnp.zeros_like(carry_v)
            meta_k[...] = jnp.zeros_like(meta_k)
            meta_v[...] = jnp.zeros_like(meta_v)
            dsink_ref[...] = jnp.zeros_like(dsink_ref)

        for k in range(KV_HEADS):
            cols = slice(k * kv_w, (k + 1) * kv_w)
            hd = slice(k * DH, (k + 1) * DH)
            qs, kcat, kmeta, p_band, p_meta, p_sink = _swa_probs(q_ref[:, cols], kp_ref[:, hd], kc_ref[:, hd],
                                                                 km_ref[:, hd], sink_ref[k], bias_ref[k], n)
            vcat = jnp.concatenate([vp_ref[:, hd], vc_ref[:, hd]], axis=0)
            vmeta = vm_ref[:, hd][META_PAD:, :]
            o, do = o_ref[:, cols], do_ref[:, cols]
            os_ = jnp.concatenate([o[:, g * DH:(g + 1) * DH] for g in range(ATTN_GROUP)], axis=0)
            dos = jnp.concatenate([do[:, g * DH:(g + 1) * DH] for g in range(ATTN_GROUP)], axis=0)
            delta = jnp.sum(dos * os_, axis=1, keepdims=True)
            ds_band = p_band * (_dot(dos, vcat, ((1,), (1,))) - delta)
            ds_meta = p_meta * (_dot(dos, vmeta, ((1,), (1,))) - delta)
            ds_sink = -p_sink * delta
            dqs = (_dot(ds_band, kcat, ((1,), (0,))) + _dot(ds_meta, kmeta, ((1,), (0,)))) * (DH ** -0.5)
            for g in range(ATTN_GROUP):
                dq_ref[:, k * kv_w + g * DH:k * kv_w + (g + 1) * DH] = dqs[g * CHUNK:(g + 1) * CHUNK, :].astype(BF16)
                dsink_ref[k, g:g + 1, :] += jnp.sum(ds_sink[g * CHUNK:(g + 1) * CHUNK, :])
            dkcat = _dot(ds_band, qs, ((0,), (0,)))
            dvcat = _dot(p_band, dos, ((0,), (0,)))
            meta_k[:, hd] += _dot(ds_meta, qs, ((0,), (0,)))
            meta_v[:, hd] += _dot(p_meta, dos, ((0,), (0,)))
            dk_buf[:, hd] = dkcat[CHUNK:, :] + carry_k[:, hd]
            dv_buf[:, hd] = dvcat[CHUNK:, :] + carry_v[:, hd]
            carry_k[:, hd] = dkcat[:CHUNK, :]
            carry_v[:, hd] = dvcat[:CHUNK, :]

        @pl.when(n == 0)
        def _():
            dk_buf[META_PAD:, :] += meta_k[...]
            dv_buf[META_PAD:, :] += meta_v[...]

        dk_ref[...] = dk_buf[...].astype(BF16)
        dv_ref[...] = dv_buf[...].astype(BF16)

    return pl.pallas_call(
        body, grid=(nt,),
        in_specs=[q_spec] + k_specs + v_specs + [sink_spec, bias_spec, o_spec, o_spec],
        out_specs=[o_spec, dkv_spec, dkv_spec, _full((KV_HEADS, 8, 128))],
        out_shape=[_sds((t_rows, ATTN_HEADS * DH), BF16), _sds((t_rows, KV_W), BF16),
                   _sds((t_rows, KV_W), BF16), _sds((KV_HEADS, 8, 128), F32)],
        scratch_shapes=[pltpu.VMEM((CHUNK, KV_W), F32), pltpu.VMEM((CHUNK, KV_W), F32),
                        pltpu.VMEM((N_META, KV_W), F32), pltpu.VMEM((N_META, KV_W), F32),
                        pltpu.VMEM((CHUNK, KV_W), F32), pltpu.VMEM((CHUNK, KV_W), F32)],
        name="swa_bwd", compiler_params=_params(1))(proj, proj, proj, proj, proj, proj, proj, sink_rows, bias, out, dout)


def _pack_w_in_t(w_in_t):
    w_dt = w_in_t[CUT_DT:CUT_Q].reshape(SSM_GROUPS, HEADS_PER_GROUP, D_MODEL)
    w_dt = jnp.pad(w_dt, ((0, 0), (0, 128 - HEADS_PER_GROUP), (0, 0))).reshape(SSM_GROUPS * 128, D_MODEL)
    return jnp.concatenate([w_in_t[CUT_Z:CUT_XBC], w_in_t[CUT_G:], w_in_t[CUT_XBC:CUT_DT], w_in_t[CUT_Q:CUT_K],
                            w_in_t[CUT_K:CUT_V], w_in_t[CUT_V:CUT_G], w_dt], axis=0)


def _unpack_w_in_t(wp_t):
    w_dt = wp_t[OFF_DT:].reshape(SSM_GROUPS, 128, D_MODEL)[:, :HEADS_PER_GROUP].reshape(SSM_HEADS, D_MODEL)
    return jnp.concatenate([wp_t[OFF_Z:OFF_GATE], wp_t[OFF_XBC:OFF_Q], w_dt, wp_t[OFF_Q:OFF_K], wp_t[OFF_K:OFF_V],
                            wp_t[OFF_V:OFF_DT], wp_t[OFF_GATE:OFF_XBC]], axis=0)


def _group_rows(v, width):
    return jnp.pad(v.reshape(SSM_GROUPS, 1, HEADS_PER_GROUP), ((0, 0), (0, 0), (0, width - HEADS_PER_GROUP)))


def _local_step(x, target, wt, late_weights=None, on_grad=None, started=None):
    seq = x.shape[0]
    grads = {}

    def emit(name, g):
        grads[name] = g
        return None if on_grad is None else on_grad(name, g)
    h = jnp.concatenate([jnp.zeros((META_PAD, D_MODEL), F32), wt["meta_tokens"], x], axis=0)
    wp_t = _pack_w_in_t(wt["w_in_t"])
    dtb = _group_rows(wt["ssm_dt_bias"].reshape(-1), 128)
    alog = _group_rows(wt["ssm_a_log"].reshape(-1), 128)
    dskip_x = jnp.repeat(wt["ssm_d_skip"].reshape(-1), HEAD_P).reshape(SSM_GROUPS, 1, GROUP_W)
    sink_rows = jnp.repeat(wt["attn_sinks"].reshape(KV_HEADS, ATTN_GROUP), CHUNK, axis=1).reshape(KV_HEADS, ATTN_GROUP * CHUNK, 1)

    hn = _prenorm(h, wt["norm_pre_mix"])
    proj = _matmul(hn, wp_t, tb=True, name="in_proj", after=started)
    xc, xact = _ssm_conv_fwd(proj, wt["ssm_conv_w"], wt["ssm_conv_b"])
    y, hst = _ssd_fwd(xact, proj, dtb, alog, dskip_x)
    yn = _ssm_post(y, proj, wt["ssm_norm"])
    if late_weights is not None:
        wt = {**wt, **late_weights(yn)}
    y_ssm = _matmul(yn, wt["w_ssm_out"], name="ssm_out")
    bias = _swa_bias()
    attn = _swa_fwd(proj, sink_rows, bias)
    y_attn = _matmul(attn, wt["w_attn_out"], name="attn_out")
    mixed = _mix_fwd(proj, y_ssm, y_attn)
    mix = _matmul(mixed, wt["w_mix_out"], name="mix_out")
    h1, hn2 = _postmix(h, mix, wt["norm_post_mix"], wt["norm_pre_ffn"])
    up = _matmul(hn2, wt["w_ffn_up_t"], tb=True, name="ffn_up")
    u, act = _ffn_act(up, wt["ffn_conv_w"], wt["ffn_conv_b"])
    f = _matmul(act, wt["w_ffn_down"], name="ffn_down")
    df, dy, g_norm_post_ffn, loss_row = _final(h1, f, target, wt["norm_post_ffn"])

    grads["norm_post_ffn"] = g_norm_post_ffn
    sent = emit("w_ffn_down", _matmul(act, df, ta=True, out_dtype=BF16, name="dw_ffn_down"))
    dact = _matmul(df, wt["w_ffn_down"], tb=True, name="d_act", after=sent)
    dup, grads["ffn_conv_w"], grads["ffn_conv_b"] = _ffn_act_bwd(u, up, dact, wt["ffn_conv_w"])
    sent = emit("w_ffn_up_t", _matmul(dup, hn2, ta=True, out_dtype=BF16, name="dw_ffn_up"))
    dhn2 = _matmul(dup, wt["w_ffn_up_t"], name="d_hn2", after=sent)
    dmix, dh, grads["norm_pre_ffn"], grads["norm_post_mix"] = _postmix_bwd(h1, dhn2, dy, mix, wt["norm_pre_ffn"], wt["norm_post_mix"])
    sent = emit("w_mix_out", _matmul(mixed, dmix, ta=True, out_dtype=BF16, name="dw_mix_out"))
    dmixed = _matmul(dmix, wt["w_mix_out"], tb=True, name="d_mixed", after=sent)
    dy_ssm, dy_attn, dglog = _mix_bwd(dmixed, proj, y_ssm, y_attn)
    sent = emit("w_ssm_out", _matmul(yn, dy_ssm, ta=True, out_dtype=BF16, name="dw_ssm_out"))
    dyn = _matmul(dy_ssm, wt["w_ssm_out"], tb=True, name="d_yn", after=sent)
    sent = emit("w_attn_out", _matmul(attn, dy_attn, ta=True, out_dtype=BF16, name="dw_attn_out"))
    dattn = _matmul(dy_attn, wt["w_attn_out"], tb=True, name="d_attn", after=sent)
    dy_ssd, dz, grads["ssm_norm"] = _ssm_post_bwd(y, proj, dyn, wt["ssm_norm"])
    dxs, dbm, dcm, ddt, dalog, ddtb, dd_x = _ssd_bwd(xact, proj, dtb, alog, dskip_x, dy_ssd, hst)
    grads["ssm_a_log"] = dalog[:, 0, :HEADS_PER_GROUP].reshape(1, SSM_HEADS)
    grads["ssm_dt_bias"] = ddtb[:, 0, :HEADS_PER_GROUP].reshape(1, SSM_HEADS)
    grads["ssm_d_skip"] = dd_x.reshape(SSM_HEADS, HEAD_P).sum(axis=1).reshape(1, SSM_HEADS)
    dxbc, grads["ssm_conv_w"], grads["ssm_conv_b"] = _ssm_conv_bwd(xc, proj, dxs, dbm, dcm, wt["ssm_conv_w"])
    dq, dk, dv, dsink = _swa_bwd(proj, sink_rows, bias, attn, dattn)
    grads["attn_sinks"] = dsink[:, :ATTN_GROUP, 0].reshape(1, ATTN_HEADS)
    dproj = jnp.concatenate([dz, dglog, dxbc, dq, dk, dv, ddt], axis=1)
    sent = emit("w_in_t", _unpack_w_in_t(_matmul(dproj, hn, ta=True, out_dtype=BF16, name="dw_in")))
    dhn = _matmul(dproj, wp_t, name="d_hn", after=sent)
    dh_all, grads["norm_pre_mix"] = _prenorm_bwd(h, dhn, dh, wt["norm_pre_mix"])
    grads["meta_tokens"] = dh_all[META_PAD:CHUNK]
    return loss_row[0, 0], dh_all[CHUNK:CHUNK + seq], grads


def _all_gather(shards):
    n = len(shards)

    def body(*refs):
        ins, outs = refs[:n], refs[n:2 * n]
        send_sems, recv_sems, local_sems = refs[2 * n:]
        x, y, c = lax.axis_index("x"), lax.axis_index("y"), lax.axis_index("c")
        me, sibling = (x, y, c), (x, y, 1 - c)
        chips = [(1 - x, y), (x, 1 - y), (1 - x, 1 - y)]

        def slot(a, dev):
            return outs[a].at[4 * dev[0] + 2 * dev[1] + dev[2]]

        def copy(k, a, block, to, src=None):
            return pltpu.make_async_remote_copy(
                src_ref=slot(a, block) if src is None else src, dst_ref=slot(a, block),
                send_sem=send_sems.at[k, a], recv_sem=recv_sems.at[k, a],
                device_id=to, device_id_type=pl.DeviceIdType.MESH)

        mine = [pltpu.make_async_copy(ins[a], slot(a, me), local_sems.at[a]) for a in range(n)]
        for cp in mine:
            cp.start()
        first = [copy(0, a, me, sibling, src=ins[a]) for a in range(n)]
        for j, chip in enumerate(chips):
            first += [copy(1 + j, a, me, (*chip, c), src=ins[a]) for a in range(n)]
        for cp in first:
            cp.start()
        passed = []
        for j, chip in enumerate(chips):
            for a in range(n):
                copy(1 + j, a, (*chip, c), me).wait_recv()
                fwd = copy(4 + j, a, (*chip, c), sibling)
                fwd.start()
                passed.append(fwd)
        for a in range(n):
            copy(0, a, sibling, me).wait_recv()
        for j, chip in enumerate(chips):
            for a in range(n):
                copy(4 + j, a, (*chip, 1 - c), me).wait_recv()
        for cp in first + passed:
            cp.wait_send()
        for cp in mine:
            cp.wait()

    hbm = pl.BlockSpec(memory_space=pl.ANY)
    return pl.pallas_call(
        body, in_specs=[hbm] * n, out_specs=[hbm] * n,
        out_shape=[_sds((N_DEV,) + s.shape, s.dtype) for s in shards],
        scratch_shapes=[pltpu.SemaphoreType.DMA((7, n)), pltpu.SemaphoreType.DMA((7, n)), pltpu.SemaphoreType.DMA((n,))],
        name="gather_weights")(*shards)


def _exchange_partials(parts):
    n = len(parts)

    def body(*refs):
        ins, outs = refs[:n], refs[n:2 * n]
        send_sems, recv_sems, local_sems = refs[2 * n:]
        x, y, c = lax.axis_index("x"), lax.axis_index("y"), lax.axis_index("c")
        my_id = 4 * x + 2 * y + c

        def peer(k):
            return (x ^ ((k >> 2) & 1), y ^ ((k >> 1) & 1), c ^ (k & 1))

        def copy(k, a):
            p = peer(k)
            p_id = 4 * p[0] + 2 * p[1] + p[2]
            return pltpu.make_async_remote_copy(
                src_ref=ins[a].at[p_id], dst_ref=outs[a].at[my_id],
                send_sem=send_sems.at[k - 1, a], recv_sem=recv_sems.at[k - 1, a],
                device_id=p, device_id_type=pl.DeviceIdType.MESH)

        def arrival(k, a):
            p = peer(k)
            p_id = 4 * p[0] + 2 * p[1] + p[2]
            return pltpu.make_async_remote_copy(
                src_ref=ins[a].at[p_id], dst_ref=outs[a].at[p_id],
                send_sem=send_sems.at[k - 1, a], recv_sem=recv_sems.at[k - 1, a],
                device_id=p, device_id_type=pl.DeviceIdType.MESH)

        mine = [pltpu.make_async_copy(ins[a].at[my_id], outs[a].at[my_id], local_sems.at[a]) for a in range(n)]
        for cp in mine:
            cp.start()
        sends = [copy(k, a) for k in range(1, N_DEV) for a in range(n)]
        for cp in sends:
            cp.start()
        for k in range(1, N_DEV):
            for a in range(n):
                arrival(k, a).wait_recv()
        for cp in sends:
            cp.wait_send()
        for cp in mine:
            cp.wait()

    hbm = pl.BlockSpec(memory_space=pl.ANY)
    return pl.pallas_call(
        body, in_specs=[hbm] * n, out_specs=[hbm] * n, out_shape=[_sds(p.shape, p.dtype) for p in parts],
        scratch_shapes=[pltpu.SemaphoreType.DMA((7, n)), pltpu.SemaphoreType.DMA((7, n)), pltpu.SemaphoreType.DMA((n,))],
        name="exchange_grads")(*parts)


def _peer_table():
    x, y, c = lax.axis_index("x"), lax.axis_index("y"), lax.axis_index("c")
    peers = []
    for k in range(N_DEV - 1):
        bits = k + 1
        p = (x ^ ((bits >> 2) & 1), y ^ ((bits >> 1) & 1), c ^ (bits & 1))
        peers.append((k, p, 4 * p[0] + 2 * p[1] + p[2]))
    return 4 * x + 2 * y + c, peers


_HBM = pl.BlockSpec(memory_space=pltpu.HBM)
_SEM = pl.BlockSpec(memory_space=pltpu.SEMAPHORE)
_EFFECT = pltpu.SideEffectType.DATAFLOW_SIDE_EFFECTING


def _push_copy(src, land, send_sems, recv_sems, a, k, p, src_slot, dst_slot):
    sem = a * (N_DEV - 1) + k
    return pltpu.make_async_remote_copy(
        src_ref=src[a] if src_slot is None else src[a].at[src_slot], dst_ref=land[a].at[dst_slot],
        send_sem=send_sems.at[sem], recv_sem=recv_sems.at[sem], device_id=p, device_id_type=pl.DeviceIdType.MESH)


def _push_start(srcs, scatter, name):
    n = len(srcs)
    lands = [lax.empty(s.shape if scatter else (N_DEV,) + s.shape, s.dtype) for s in srcs]

    def body(*refs):
        src, land = refs[:n], refs[n:2 * n]
        send_sems, recv_sems, token = refs[2 * n], refs[2 * n + 1], refs[-1]
        my_id, peers = _peer_table()
        for a in range(n):
            for k, p, p_id in peers:
                _push_copy(src, land, send_sems, recv_sems, a, k, p, p_id if scatter else None, my_id).start()
        token[...] = jnp.zeros_like(token)

    sems = pltpu.SemaphoreType.DMA(((N_DEV - 1) * n,))
    res = pl.pallas_call(
        body, name=name,
        out_shape=(sems, sems, *[pltpu.HBM(a.shape, a.dtype) for a in srcs + lands], _sds((8, 128), F32)),
        in_specs=[_HBM] * (2 * n), out_specs=(_SEM, _SEM, *[_HBM] * (2 * n), pl.BlockSpec(memory_space=pltpu.VMEM)),
        input_output_aliases={i: 2 + i for i in range(2 * n)},
        compiler_params=pltpu.CompilerParams(has_side_effects=_EFFECT),
    )(*[pltpu.with_memory_space_constraint(a, pltpu.HBM) for a in srcs + lands])
    return dict(send=res[0], recv=res[1], src=list(res[2:2 + n]), land=list(res[2 + n:2 + 2 * n]), token=res[-1],
                scatter=scatter)


def _push_wait(handle, after, name):
    n = len(handle["src"])
    scatter = handle["scatter"]

    def body(*refs):
        src, land = refs[:n], refs[n:2 * n]
        send_sems, recv_sems = refs[2 * n], refs[2 * n + 1]
        _, peers = _peer_table()
        for a in range(n):
            for k, p, p_id in peers:
                cp = _push_copy(src, land, send_sems, recv_sems, a, k, p, p_id if scatter else None, p_id)
                cp.wait_send()
                cp.wait_recv()

    arrays = handle["src"] + handle["land"]
    res = pl.pallas_call(
        body, name=name, out_shape=tuple(pltpu.HBM(a.shape, a.dtype) for a in arrays),
        in_specs=[_HBM] * (2 * n) + [_SEM, _SEM, pl.BlockSpec(memory_space=pl.ANY)], out_specs=tuple([_HBM] * (2 * n)),
        input_output_aliases={i: i for i in range(2 * n)},
        compiler_params=pltpu.CompilerParams(has_side_effects=_EFFECT),
    )(*arrays, handle["send"], handle["recv"], after)
    return list(res[:n]), list(res[n:])


def _slot_sum(p_ref, own_ref):
    if own_ref is not None:
        my_id = 4 * lax.axis_index("x") + 2 * lax.axis_index("y") + lax.axis_index("c")
        mine = own_ref[...].astype(F32)
    g = None
    for s in range(p_ref.shape[0]):
        term = p_ref[s].astype(F32)
        if own_ref is not None:
            term = jnp.where(my_id == s, mine, term)
        g = term if g is None else g + term
    return g


def _sum_partials(parts, own, name):
    _, rows, cols = parts.shape
    tc = _pick(cols, (256, 128))

    def body(p_ref, own_ref, o_ref):
        o_ref[...] = _slot_sum(p_ref, own_ref)

    spec = pl.BlockSpec((rows, tc), lambda j: (0, j))
    return pl.pallas_call(
        body, grid=(cols // tc,), in_specs=[pl.BlockSpec((N_DEV, rows, tc), lambda j: (0, 0, j)), spec],
        out_specs=spec, out_shape=_sds((rows, cols), F32), name=name, compiler_params=_params(1))(parts, own)


def _to_bf16(arrays):
    n = len(arrays)

    def body(*refs):
        for i in range(n):
            refs[n + i][...] = refs[i][...].astype(BF16)

    return pl.pallas_call(body, out_shape=[_sds(a.shape, BF16) for a in arrays], name="weights_to_bf16",
                          compiler_params=pltpu.CompilerParams(vmem_limit_bytes=VMEM_LIMIT))(*arrays)


def _adamw(parts, own, w, m, v, name):
    rows, cols = w.shape
    tr = _pick(rows, (256, 128, 176, 64, 32, 16, 8))

    def body(*refs):
        if own is None:
            p_ref, w_ref, m_ref, v_ref, g_ref, d_ref, nm_ref, nv_ref = refs
            own_ref = None
        else:
            p_ref, own_ref, w_ref, m_ref, v_ref, g_ref, d_ref, nm_ref, nv_ref = refs
        g = _slot_sum(p_ref, own_ref)
        m_new = ADAM_B1 * m_ref[...] + (1.0 - ADAM_B1) * g
        v_new = ADAM_B2 * v_ref[...] + (1.0 - ADAM_B2) * (g * g)
        m_hat = m_new / (1.0 - ADAM_B1 ** ADAM_STEP)
        v_hat = v_new / (1.0 - ADAM_B2 ** ADAM_STEP)
        g_ref[...] = g
        d_ref[...] = -ADAM_LR * (m_hat / (jnp.sqrt(v_hat) + ADAM_EPS) + ADAM_WD * w_ref[...])
        nm_ref[...] = m_new
        nv_ref[...] = v_new

    spec = _row(tr, cols)
    operands = (parts, w, m, v) if own is None else (parts, own, w, m, v)
    return pl.pallas_call(
        body, grid=(rows // tr,),
        in_specs=[pl.BlockSpec((parts.shape[0], tr, cols), lambda i: (0, i, 0))] + [spec] * (len(operands) - 1),
        out_specs=[spec] * 4, out_shape=[_sds((rows, cols), F32)] * 4,
        name=name, compiler_params=_params(1))(*operands)


SMALL_REPLICATED = (("norm_pre_mix", 1024), ("ssm_conv_b", 3072), ("ssm_dt_bias", 32), ("ssm_a_log", 32),
                    ("ssm_d_skip", 32), ("ssm_norm", 2048), ("attn_sinks", 16), ("norm_post_mix", 1024),
                    ("norm_pre_ffn", 1024), ("ffn_conv_b", 5632), ("norm_post_ffn", 1024))
SMALL_SHARDED = (("meta_tokens", (N_META, D_MODEL // N_DEV)), ("ssm_conv_w", (SSM_CONV, CONV_DIM // N_DEV)),
                 ("ffn_conv_w", (FFN_CONV, 2 * FFN_DIM // N_DEV)))
BIG = (("w_in", (D_MODEL, N_IN // N_DEV), 1), ("w_ssm_out", (D_INNER // N_DEV, D_MODEL), 0),
       ("w_attn_out", (D_MODEL // N_DEV, D_MODEL), 0), ("w_mix_out", (D_MODEL // N_DEV, D_MODEL), 0),
       ("w_ffn_up", (D_MODEL, 2 * FFN_DIM // N_DEV), 1), ("w_ffn_down", (FFN_DIM // N_DEV, D_MODEL), 0))


def _rows_of(size):
    return -(-size // 128)


def _as_rows(flat):
    size = flat.shape[-1]
    rows = _rows_of(size)
    flat = jnp.pad(flat, [(0, 0)] * (flat.ndim - 1) + [(0, rows * 128 - size)])
    return flat.reshape(flat.shape[:-1] + (rows, 128))


def _pack_small(rep, sharded):
    pieces = [_as_rows(rep[name].reshape(-1)) for name, _ in SMALL_REPLICATED]
    pieces += [_as_rows(sharded[name].reshape(-1)) for name, _ in SMALL_SHARDED]
    packed = jnp.concatenate(pieces, axis=0)
    return jnp.pad(packed, ((0, -packed.shape[0] % 8), (0, 0)))


def _unpack_small(packed):
    out, row = {}, 0
    for name, size in SMALL_REPLICATED:
        out[name] = packed[row:row + _rows_of(size)].reshape(-1)[:size].reshape(1, size)
        row += _rows_of(size)
    for name, (r, c) in SMALL_SHARDED:
        out[name] = packed[row:row + _rows_of(r * c)].reshape(-1)[:r * c].reshape(r, c)
        row += _rows_of(r * c)
    return out


def _shard_major(g, shape, axis):
    r, c = shape
    if axis == 0:
        return g.reshape(N_DEV, r, c)
    return g.reshape(r, N_DEV, c).transpose(1, 0, 2)


def kernel(x, meta_tokens, norm_pre_mix, w_in, ssm_conv_w, ssm_conv_b, ssm_dt_bias, ssm_a_log, ssm_d_skip, ssm_norm, w_ssm_out, attn_sinks, w_attn_out, w_mix_out, norm_post_mix, norm_pre_ffn, w_ffn_up, ffn_conv_w, ffn_conv_b, w_ffn_down, norm_post_ffn, loss_target, m_meta_tokens, m_norm_pre_mix, m_w_in, m_ssm_conv_w, m_ssm_conv_b, m_ssm_dt_bias, m_ssm_a_log, m_ssm_d_skip, m_ssm_norm, m_w_ssm_out, m_attn_sinks, m_w_attn_out, m_w_mix_out, m_norm_post_mix, m_norm_pre_ffn, m_w_ffn_up, m_ffn_conv_w, m_ffn_conv_b, m_w_ffn_down, m_norm_post_ffn, v_meta_tokens, v_norm_pre_mix, v_w_in, v_ssm_conv_w, v_ssm_conv_b, v_ssm_dt_bias, v_ssm_a_log, v_ssm_d_skip, v_ssm_norm, v_w_ssm_out, v_attn_sinks, v_w_attn_out, v_w_mix_out, v_norm_post_mix, v_norm_pre_ffn, v_w_ffn_up, v_ffn_conv_w, v_ffn_conv_b, v_w_ffn_down, v_norm_post_ffn):
    names = ("meta_tokens", "norm_pre_mix", "w_in", "ssm_conv_w", "ssm_conv_b", "ssm_dt_bias", "ssm_a_log", "ssm_d_skip",
             "ssm_norm", "w_ssm_out", "attn_sinks", "w_attn_out", "w_mix_out", "norm_post_mix", "norm_pre_ffn", "w_ffn_up",
             "ffn_conv_w", "ffn_conv_b", "w_ffn_down", "norm_post_ffn")
    w_loc = dict(zip(names, (meta_tokens, norm_pre_mix, w_in, ssm_conv_w, ssm_conv_b, ssm_dt_bias, ssm_a_log, ssm_d_skip,
                             ssm_norm, w_ssm_out, attn_sinks, w_attn_out, w_mix_out, norm_post_mix, norm_pre_ffn, w_ffn_up,
                             ffn_conv_w, ffn_conv_b, w_ffn_down, norm_post_ffn)))
    m_loc = dict(zip(names, (m_meta_tokens, m_norm_pre_mix, m_w_in, m_ssm_conv_w, m_ssm_conv_b, m_ssm_dt_bias, m_ssm_a_log,
                             m_ssm_d_skip, m_ssm_norm, m_w_ssm_out, m_attn_sinks, m_w_attn_out, m_w_mix_out, m_norm_post_mix,
                             m_norm_pre_ffn, m_w_ffn_up, m_ffn_conv_w, m_ffn_conv_b, m_w_ffn_down, m_norm_post_ffn)))
    v_loc = dict(zip(names, (v_meta_tokens, v_norm_pre_mix, v_w_in, v_ssm_conv_w, v_ssm_conv_b, v_ssm_dt_bias, v_ssm_a_log,
                             v_ssm_d_skip, v_ssm_norm, v_w_ssm_out, v_attn_sinks, v_w_attn_out, v_w_mix_out, v_norm_post_mix,
                             v_norm_pre_ffn, v_w_ffn_up, v_ffn_conv_w, v_ffn_conv_b, v_w_ffn_down, v_norm_post_ffn)))

    def local2d(d, name):
        a = d[name]
        return a if name == "meta_tokens" else a.reshape(a.shape[1:])

    my_id = 4 * lax.axis_index("x") + 2 * lax.axis_index("y") + lax.axis_index("c")
    big = {name: (shape, axis) for name, shape, axis in BIG}

    def whole(name, g):
        return g.reshape(N_DEV * g.shape[1], g.shape[2])

    def key(name):
        return name + "_t" if big[name][1] == 1 else name

    by_rows = [name for name, _, axis in BIG if axis == 0]
    send_bf16 = dict(zip(by_rows, _to_bf16([local2d(w_loc, name) for name in by_rows])))
    for name, _, axis in BIG:
        if axis == 1:
            send_bf16[name] = local2d(w_loc, name).T.astype(BF16)
    small_shard_pack = jnp.concatenate([_as_rows(local2d(w_loc, name).reshape(-1)) for name, _ in SMALL_SHARDED], axis=0)
    small_shard_pack = jnp.pad(small_shard_pack, ((0, -small_shard_pack.shape[0] % 8), (0, 0)))
    first = _all_gather([send_bf16["w_in"], small_shard_pack])
    rest_names = [name for name, _, _ in BIG if name != "w_in"]
    rest = [send_bf16[name] for name in rest_names]
    rest, first = lax.optimization_barrier((rest, first))
    rest_handle = _push_start(rest, False, "gather_rest_start")
    wt = {"w_in_t": whole("w_in", first[0])}
    row = 0
    for name, (r, c) in SMALL_SHARDED:
        blocks = first[1][:, row:row + _rows_of(r * c)].reshape(N_DEV, -1)[:, :r * c].reshape(N_DEV, r, c)
        wt[name] = blocks.transpose(1, 0, 2).reshape(r, N_DEV * c)
        row += _rows_of(r * c)
    for name, size in SMALL_REPLICATED:
        wt[name] = w_loc[name].reshape(1, size)

    def late_weights(after):
        own, landed = _push_wait(rest_handle, after, "gather_rest_wait")
        out = {}
        for name, mine, land in zip(rest_names, own, landed):
            out[key(name)] = whole(name, lax.dynamic_update_index_in_dim(land, mine, my_id, 0))
        return out

    sent = {}

    def on_grad(known_as, g):
        name = known_as.removesuffix("_t")
        by_owner = g.reshape(N_DEV, g.shape[0] // N_DEV, g.shape[1])
        sent[name] = _push_start([by_owner], True, "send_" + name)
        return sent[name]["token"]

    loss_part, grad_x, grads = _local_step(x[0], loss_target[0], wt, late_weights, on_grad, rest_handle["token"])
    loss = lax.psum(loss_part, AXES)

    small_parts = []
    for name, (r, c) in SMALL_SHARDED:
        small_parts.append(_as_rows(_shard_major(grads[name], (r, c), 1).reshape(N_DEV, r * c)))
    rep_rows = jnp.concatenate([_as_rows(grads[name].reshape(-1)) for name, _ in SMALL_REPLICATED], axis=0)
    small_send = jnp.concatenate([jnp.broadcast_to(rep_rows[None], (N_DEV,) + rep_rows.shape)] + small_parts, axis=1)
    small_send = jnp.pad(small_send, ((0, 0), (0, -small_send.shape[1] % 8), (0, 0)))
    small_received = _exchange_partials([small_send])[0]

    def small_pack(d):
        return _pack_small({name: d[name] for name, _ in SMALL_REPLICATED}, {name: local2d(d, name) for name, _ in SMALL_SHARDED})

    grad_w, delta_w, new_m, new_v = {}, {}, {}, {}
    outs = _adamw(small_received, None, small_pack(w_loc), small_pack(m_loc), small_pack(v_loc), "adamw_small")
    after = outs[0]
    for name, handle in sent.items():
        src, landed = _push_wait(handle, after, "arrived_" + name)
        own = lax.dynamic_index_in_dim(src[0], my_id, 0, keepdims=False)
        parts = landed[0]
        if big[name][1] == 1:
            parts, own = _sum_partials(parts, own, "sum_" + name).T[None], None
        g, d, nm, nv = _adamw(parts, own, local2d(w_loc, name), local2d(m_loc, name), local2d(v_loc, name), "adamw_" + name)
        after = g
        full = (1,) + big[name][0]
        grad_w[name], delta_w[name], new_m[name], new_v[name] = g.reshape(full), d.reshape(full), nm.reshape(full), nv.reshape(full)
    for dst, packed in zip((grad_w, delta_w, new_m, new_v), outs):
        for name, a in _unpack_small(packed).items():
            dst[name] = a.reshape(w_loc[name].shape)

    return (loss, grad_x[None], *[grad_w[n] for n in names], *[delta_w[n] for n in names],
            *[new_m[n] for n in names], *[new_v[n] for n in names])
```

```python
import jax
import jax.numpy as jnp
from jax import lax
from jax.experimental import pallas as pl
from jax.experimental.pallas import tpu as pltpu

F32 = jnp.float32
BF16 = jnp.bfloat16
HIGHEST = lax.Precision.HIGHEST

D_MODEL = 1024
N_META = 16
CHUNK = 128
META_PAD = CHUNK - N_META
D_INNER = 2048
HEAD_P = 64
SSM_HEADS = 32
SSM_GROUPS = 4
HEADS_PER_GROUP = SSM_HEADS // SSM_GROUPS
GROUP_W = HEADS_PER_GROUP * HEAD_P
D_STATE = 128
SSM_CONV = 4
CONV_DIM = D_INNER + 2 * SSM_GROUPS * D_STATE
ATTN_HEADS = 16
KV_HEADS = 4
ATTN_GROUP = ATTN_HEADS // KV_HEADS
DH = 64
KV_W = KV_HEADS * DH
FFN_DIM = 2816
FFN_CONV = 3
EPS = 1e-6
NEG = -1e30
N_DEV = 8
AXES = ("x", "y", "c")

OFF_Z, OFF_GATE, OFF_XBC, OFF_Q, OFF_K, OFF_V, OFF_DT = 0, 2048, 4096, 7168, 8192, 8448, 8704
N_INP = OFF_DT + SSM_GROUPS * 128
CUT_Z, CUT_XBC, CUT_DT, CUT_Q, CUT_K, CUT_V, CUT_G = 0, 2048, 5120, 5152, 6176, 6432, 6688
N_IN = 8736

ADAM_LR, ADAM_B1, ADAM_B2, ADAM_EPS, ADAM_WD, ADAM_STEP = 0.001, 0.9, 0.999, 1e-08, 0.01, 10

VMEM_LIMIT = 56 * 1024 * 1024


def _params(n_grid):
    return pltpu.CompilerParams(dimension_semantics=("arbitrary",) * n_grid, vmem_limit_bytes=VMEM_LIMIT)


def _sds(shape, dtype):
    return jax.ShapeDtypeStruct(shape, dtype)


def _pick(n, prefs):
    for c in prefs:
        if n % c == 0:
            return c
    raise ValueError(f"no tile of {prefs} divides {n}")


def _row(tr, width, cb=0):
    return pl.BlockSpec((tr, width), lambda i: (i, cb))


def _row_rev(tr, width, nt, cb=0):
    return pl.BlockSpec((tr, width), lambda i: (nt - 1 - i, cb))


def _full(shape):
    return pl.BlockSpec(shape, lambda *_: (0,) * len(shape))


def _sigmoid(x):
    return 1.0 / (1.0 + jnp.exp(-x))


def _softplus(x):
    return jnp.maximum(x, 0.0) + jnp.log(1.0 + jnp.exp(-jnp.abs(x)))


def _rms(x):
    return lax.rsqrt(jnp.mean(x * x, axis=-1, keepdims=True) + EPS)


def _rms_bwd(x, r, w, dy):
    xh = x * r
    g = dy * w
    dx = r * (g - xh * jnp.mean(g * xh, axis=-1, keepdims=True))
    return dx, jnp.sum(dy * xh, axis=0, keepdims=True)


def _row_ids(shape, tile_index, tr):
    return tile_index * tr + lax.broadcasted_iota(jnp.int32, shape, 0)


def _shift_down(cur, prev, s):
    if s == 0:
        return cur
    row = lax.broadcasted_iota(jnp.int32, cur.shape, 0)
    return jnp.where(row < s, pltpu.roll(prev, s, 0), pltpu.roll(cur, s, 0))


def _shift_up(cur, nxt, s):
    if s == 0:
        return cur
    n = cur.shape[0]
    row = lax.broadcasted_iota(jnp.int32, cur.shape, 0)
    return jnp.where(row >= n - s, pltpu.roll(nxt, n - s, 0), pltpu.roll(cur, n - s, 0))


def _matmul(a, b, *, ta=False, tb=False, out_dtype=F32, name, after=None):
    if ta:
        k_dim, m_dim = a.shape
    else:
        m_dim, k_dim = a.shape
    n_dim = b.shape[0] if tb else b.shape[1]
    tm = _pick(m_dim, (1408, 1024, 768, 512, 384, 256, 128))
    tn = _pick(n_dim, (1024, 1408, 768, 512, 384, 256, 128))
    tk = k_dim if (not ta and k_dim <= 2816) else _pick(k_dim, (1408, 1024, 768, 512, 384, 256, 128))
    nk = k_dim // tk
    dims = (((0 if ta else 1,), (1 if tb else 0,)), ((), ()))

    use_acc = nk > 1 and out_dtype != F32

    def body(a_ref, b_ref, *rest):
        o_ref = rest[-2] if use_acc else rest[-1]
        acc_ref = rest[-1] if use_acc else o_ref
        r = lax.dot_general(a_ref[...].astype(BF16), b_ref[...].astype(BF16), dims, preferred_element_type=F32)
        if nk == 1:
            o_ref[...] = r.astype(o_ref.dtype)
        else:
            k = pl.program_id(2)

            @pl.when(k == 0)
            def _():
                acc_ref[...] = r

            @pl.when(k > 0)
            def _():
                acc_ref[...] += r

            if use_acc:
                @pl.when(k == nk - 1)
                def _():
                    o_ref[...] = acc_ref[...].astype(o_ref.dtype)

    a_spec = pl.BlockSpec((tk, tm), lambda i, j, k: (k, i)) if ta else pl.BlockSpec((tm, tk), lambda i, j, k: (i, k))
    b_spec = pl.BlockSpec((tn, tk), lambda i, j, k: (j, k)) if tb else pl.BlockSpec((tk, tn), lambda i, j, k: (k, j))
    extra_specs, extra = ([], ()) if after is None else ([pl.BlockSpec(memory_space=pl.ANY)], (after,))
    return pl.pallas_call(
        body, grid=(m_dim // tm, n_dim // tn, nk), in_specs=[a_spec, b_spec] + extra_specs,
        out_specs=pl.BlockSpec((tm, tn), lambda i, j, k: (i, j)), out_shape=_sds((m_dim, n_dim), out_dtype),
        scratch_shapes=[pltpu.VMEM((tm, tn), F32)] if use_acc else [],
        name=name, compiler_params=_params(3))(a, b, *extra)


def _seq_specs():
    return [pl.BlockSpec((CHUNK, D_MODEL), lambda i: (jnp.maximum(i - 1, 0), 0)), _full((N_META, D_MODEL))]


def _seq_tile(x_ref, meta_ref, i):
    first = jnp.concatenate([jnp.zeros((META_PAD, D_MODEL), F32), meta_ref[...]], axis=0)
    return jnp.where(i == 0, first, x_ref[...])


def _prenorm(x, meta, w):
    t_rows = x.shape[0] + CHUNK

    def body(x_ref, meta_ref, w_ref, o_ref):
        h = _seq_tile(x_ref, meta_ref, pl.program_id(0))
        o_ref[...] = (h * _rms(h) * w_ref[...]).astype(BF16)

    return pl.pallas_call(body, grid=(t_rows // CHUNK,), in_specs=_seq_specs() + [_full((1, D_MODEL))],
                          out_specs=_row(CHUNK, D_MODEL), out_shape=_sds((t_rows, D_MODEL), BF16),
                          name="prenorm", compiler_params=_params(1))(x, meta, w)


def _xbc_specs(tr, rev_nt=None):
    cbs = [OFF_XBC // 1024 + j for j in range(CONV_DIM // 1024)]
    if rev_nt is None:
        return [_row(tr, 1024, cb) for cb in cbs]
    return [_row_rev(tr, 1024, rev_nt, cb) for cb in cbs]


def _ssm_conv_fwd(proj, conv_w, conv_b):
    t_rows = proj.shape[0]
    tr = CHUNK

    def body(x0, x1, x2, w_ref, b_ref, xc_ref, xa_ref, prev):
        @pl.when(pl.program_id(0) == 0)
        def _():
            prev[...] = jnp.zeros_like(prev)

        x = jnp.concatenate([x0[...], x1[...], x2[...]], axis=1)
        p = prev[...]
        acc = b_ref[...] + w_ref[SSM_CONV - 1:SSM_CONV, :] * x
        for s in range(1, SSM_CONV):
            acc = acc + w_ref[SSM_CONV - 1 - s:SSM_CONV - s, :] * _shift_down(x, p, s)
        prev[...] = x
        xc_ref[...] = acc
        xa_ref[...] = acc * _sigmoid(acc)

    return pl.pallas_call(
        body, grid=(t_rows // tr,),
        in_specs=_xbc_specs(tr) + [_full((SSM_CONV, CONV_DIM)), _full((1, CONV_DIM))],
        out_specs=[_row(tr, CONV_DIM), _row(tr, CONV_DIM)],
        out_shape=[_sds((t_rows, CONV_DIM), F32), _sds((t_rows, CONV_DIM), F32)],
        scratch_shapes=[pltpu.VMEM((tr, CONV_DIM), F32)],
        name="ssm_conv_fwd", compiler_params=_params(1))(proj, proj, proj, conv_w, conv_b)


def _ssm_post(y, proj, w):
    t_rows = y.shape[0]
    tr = CHUNK

    def body(y_ref, z_ref, w_ref, o_ref):
        z = z_ref[...]
        yz = y_ref[...] * z * _sigmoid(z)
        o_ref[...] = (yz * _rms(yz) * w_ref[...]).astype(BF16)

    return pl.pallas_call(body, grid=(t_rows // tr,),
                          in_specs=[_row(tr, D_INNER), _row(tr, D_INNER, OFF_Z // D_INNER), _full((1, D_INNER))],
                          out_specs=_row(tr, D_INNER), out_shape=_sds((t_rows, D_INNER), BF16),
                          name="ssm_post", compiler_params=_params(1))(y, proj, w)


def _mix_fwd(proj, y_ssm, y_attn):
    t_rows = y_ssm.shape[0]
    tr = _pick(t_rows, (384, 128))

    def body(g_ref, ys_ref, ya_ref, o_ref):
        g = _sigmoid(g_ref[...])
        o_ref[...] = (g[:, :D_MODEL] * ys_ref[...] + g[:, D_MODEL:] * ya_ref[...]).astype(BF16)

    return pl.pallas_call(body, grid=(t_rows // tr,),
                          in_specs=[_row(tr, 2 * D_MODEL, OFF_GATE // (2 * D_MODEL)), _row(tr, D_MODEL), _row(tr, D_MODEL)],
                          out_specs=_row(tr, D_MODEL), out_shape=_sds((t_rows, D_MODEL), BF16),
                          name="mix_fwd", compiler_params=_params(1))(proj, y_ssm, y_attn)


def _postmix(x, meta, mix, w_post, w_pre):
    t_rows = mix.shape[0]
    tr = CHUNK

    def body(x_ref, meta_ref, m_ref, wp_ref, wf_ref, h1_ref, hn_ref):
        m = m_ref[...]
        h1 = _seq_tile(x_ref, meta_ref, pl.program_id(0)) + m * _rms(m) * wp_ref[...]
        h1 = jnp.where(_row_ids(h1.shape, pl.program_id(0), tr) >= META_PAD, h1, 0.0)
        h1_ref[...] = h1
        hn_ref[...] = (h1 * _rms(h1) * wf_ref[...]).astype(BF16)

    return pl.pallas_call(body, grid=(t_rows // tr,),
                          in_specs=_seq_specs() + [_row(tr, D_MODEL), _full((1, D_MODEL)), _full((1, D_MODEL))],
                          out_specs=[_row(tr, D_MODEL), _row(tr, D_MODEL)],
                          out_shape=[_sds((t_rows, D_MODEL), F32), _sds((t_rows, D_MODEL), BF16)],
                          name="postmix", compiler_params=_params(1))(x, meta, mix, w_post, w_pre)


def _ffn_act(up, conv_w, conv_b):
    t_rows = up.shape[0]
    tr = CHUNK
    width = 2 * FFN_DIM

    def body(up_ref, w_ref, b_ref, u_ref, act_ref, prev):
        @pl.when(pl.program_id(0) == 0)
        def _():
            prev[...] = jnp.zeros_like(prev)

        x = up_ref[...]
        p = prev[...]
        u = b_ref[...] + w_ref[FFN_CONV - 1:FFN_CONV, :] * x
        for s in range(1, FFN_CONV):
            u = u + w_ref[FFN_CONV - 1 - s:FFN_CONV - s, :] * _shift_down(x, p, s)
        prev[...] = x
        u_ref[...] = u
        a = u[:, :FFN_DIM]
        act_ref[...] = (a * _sigmoid(a) * u[:, FFN_DIM:]).astype(BF16)

    return pl.pallas_call(
        body, grid=(t_rows // tr,), in_specs=[_row(tr, width), _full((FFN_CONV, width)), _full((1, width))],
        out_specs=[_row(tr, width), _row(tr, FFN_DIM)],
        out_shape=[_sds((t_rows, width), F32), _sds((t_rows, FFN_DIM), BF16)],
        scratch_shapes=[pltpu.VMEM((tr, width), F32)],
        name="ffn_act", compiler_params=_params(1))(up, conv_w, conv_b)


def _final(h1, f, target, w):
    t_rows = h1.shape[0]
    tr = CHUNK

    def body(h1_ref, f_ref, t_ref, w_ref, df_ref, dy_ref, dw_ref, loss_ref):
        i = pl.program_id(0)

        @pl.when(i == 0)
        def _():
            dw_ref[...] = jnp.zeros_like(dw_ref)
            loss_ref[...] = jnp.zeros_like(loss_ref)

        f_val = f_ref[...]
        r = _rms(f_val)
        wv = w_ref[...]
        h2 = h1_ref[...] + f_val * r * wv
        diff = jnp.where(i >= 1, h2 - t_ref[...], 0.0)
        loss_ref[...] += 0.5 * jnp.sum(diff * diff) * (1.0 / D_MODEL)
        dy = diff * (1.0 / D_MODEL)
        dy_ref[...] = dy
        df, dw = _rms_bwd(f_val, r, wv, dy)
        df_ref[...] = df.astype(BF16)
        dw_ref[...] += dw

    tgt_spec = pl.BlockSpec((tr, D_MODEL), lambda i: (jnp.maximum(i - 1, 0), 0))
    return pl.pallas_call(
        body, grid=(t_rows // tr,),
        in_specs=[_row(tr, D_MODEL), _row(tr, D_MODEL), tgt_spec, _full((1, D_MODEL))],
        out_specs=[_row(tr, D_MODEL), _row(tr, D_MODEL), _full((1, D_MODEL)), _full((1, 128))],
        out_shape=[_sds((t_rows, D_MODEL), BF16), _sds((t_rows, D_MODEL), F32), _sds((1, D_MODEL), F32), _sds((1, 128), F32)],
        name="final", compiler_params=_params(1))(h1, f, target, w)


def _ffn_act_bwd(u, up, dact, conv_w):
    t_rows = u.shape[0]
    tr = CHUNK
    nt = t_rows // tr
    width = 2 * FFN_DIM

    def body(u_ref, up_ref, da_ref, w_ref, dup_ref, dw_ref, db_ref, nxt):
        @pl.when(pl.program_id(0) == 0)
        def _():
            nxt[...] = jnp.zeros_like(nxt)
            dw_ref[...] = jnp.zeros_like(dw_ref)
            db_ref[...] = jnp.zeros_like(db_ref)

        u_val = u_ref[...]
        a, g = u_val[:, :FFN_DIM], u_val[:, FFN_DIM:]
        d = da_ref[...]
        s = _sigmoid(a)
        du = jnp.concatenate([d * g * s * (1.0 + a * (1.0 - s)), d * a * s], axis=1)
        n = nxt[...]
        x = up_ref[...]
        dup = jnp.zeros_like(du)
        for sh in range(FFN_CONV):
            k = FFN_CONV - 1 - sh
            moved = _shift_up(du, n, sh)
            dup = dup + w_ref[k:k + 1, :] * moved
            dw_ref[k:k + 1, :] += jnp.sum(moved * x, axis=0, keepdims=True)
        db_ref[...] += jnp.sum(du, axis=0, keepdims=True)
        nxt[...] = du
        dup_ref[...] = dup.astype(BF16)

    return pl.pallas_call(
        body, grid=(nt,),
        in_specs=[_row_rev(tr, width, nt), _row_rev(tr, width, nt), _row_rev(tr, FFN_DIM, nt), _full((FFN_CONV, width))],
        out_specs=[_row_rev(tr, width, nt), _full((FFN_CONV, width)), _full((1, width))],
        out_shape=[_sds((t_rows, width), BF16), _sds((FFN_CONV, width), F32), _sds((1, width), F32)],
        scratch_shapes=[pltpu.VMEM((tr, width), F32)],
        name="ffn_act_bwd", compiler_params=_params(1))(u, up, dact, conv_w)


def _postmix_bwd(h1, dhn2, dy, mix, w_pre, w_post):
    t_rows = h1.shape[0]
    tr = CHUNK

    def body(h1_ref, dhn_ref, dy_ref, m_ref, wf_ref, wp_ref, dmix_ref, dh_ref, dwf_ref, dwp_ref):
        @pl.when(pl.program_id(0) == 0)
        def _():
            dwf_ref[...] = jnp.zeros_like(dwf_ref)
            dwp_ref[...] = jnp.zeros_like(dwp_ref)

        h1v = h1_ref[...]
        dx, dwf = _rms_bwd(h1v, _rms(h1v), wf_ref[...], dhn_ref[...])
        dwf_ref[...] += dwf
        dh1 = dy_ref[...] + dx
        dh1 = jnp.where(_row_ids(dh1.shape, pl.program_id(0), tr) >= META_PAD, dh1, 0.0)
        dh_ref[...] = dh1
        m = m_ref[...]
        dmix, dwp = _rms_bwd(m, _rms(m), wp_ref[...], dh1)
        dwp_ref[...] += dwp
        dmix_ref[...] = dmix.astype(BF16)

    return pl.pallas_call(
        body, grid=(t_rows // tr,),
        in_specs=[_row(tr, D_MODEL)] * 4 + [_full((1, D_MODEL))] * 2,
        out_specs=[_row(tr, D_MODEL), _row(tr, D_MODEL), _full((1, D_MODEL)), _full((1, D_MODEL))],
        out_shape=[_sds((t_rows, D_MODEL), BF16), _sds((t_rows, D_MODEL), F32), _sds((1, D_MODEL), F32), _sds((1, D_MODEL), F32)],
        name="postmix_bwd", compiler_params=_params(1))(h1, dhn2, dy, mix, w_pre, w_post)


def _mix_bwd(dmixed, proj, y_ssm, y_attn):
    t_rows = dmixed.shape[0]
    tr = _pick(t_rows, (384, 128))

    def body(d_ref, g_ref, ys_ref, ya_ref, dys_ref, dya_ref, dg_ref):
        d = d_ref[...]
        g = _sigmoid(g_ref[...])
        g1, g2 = g[:, :D_MODEL], g[:, D_MODEL:]
        dys_ref[...] = (d * g1).astype(BF16)
        dya_ref[...] = (d * g2).astype(BF16)
        dg_ref[...] = jnp.concatenate([d * ys_ref[...] * g1 * (1.0 - g1), d * ya_ref[...] * g2 * (1.0 - g2)],
                                      axis=1).astype(BF16)

    return pl.pallas_call(
        body, grid=(t_rows // tr,),
        in_specs=[_row(tr, D_MODEL), _row(tr, 2 * D_MODEL, OFF_GATE // (2 * D_MODEL)), _row(tr, D_MODEL), _row(tr, D_MODEL)],
        out_specs=[_row(tr, D_MODEL), _row(tr, D_MODEL), _row(tr, 2 * D_MODEL)],
        out_shape=[_sds((t_rows, D_MODEL), BF16), _sds((t_rows, D_MODEL), BF16), _sds((t_rows, 2 * D_MODEL), BF16)],
        name="mix_bwd", compiler_params=_params(1))(dmixed, proj, y_ssm, y_attn)


def _ssm_post_bwd(y, proj, dyn, w):
    t_rows = y.shape[0]
    tr = CHUNK

    def body(y_ref, z_ref, d_ref, w_ref, dy_ref, dz_ref, dw_ref):
        @pl.when(pl.program_id(0) == 0)
        def _():
            dw_ref[...] = jnp.zeros_like(dw_ref)

        yv, z = y_ref[...], z_ref[...]
        sz = _sigmoid(z)
        silu = z * sz
        yz = yv * silu
        dyz, dw = _rms_bwd(yz, _rms(yz), w_ref[...], d_ref[...])
        dw_ref[...] += dw
        dy_ref[...] = dyz * silu
        dz_ref[...] = (dyz * yv * sz * (1.0 + z * (1.0 - sz))).astype(BF16)

    return pl.pallas_call(
        body, grid=(t_rows // tr,),
        in_specs=[_row(tr, D_INNER), _row(tr, D_INNER, OFF_Z // D_INNER), _row(tr, D_INNER), _full((1, D_INNER))],
        out_specs=[_row(tr, D_INNER), _row(tr, D_INNER), _full((1, D_INNER))],
        out_shape=[_sds((t_rows, D_INNER), F32), _sds((t_rows, D_INNER), BF16), _sds((1, D_INNER), F32)],
        name="ssm_post_bwd", compiler_params=_params(1))(y, proj, dyn, w)


def _ssm_conv_bwd(xc, proj, dxs, dbm, dcm, conv_w):
    t_rows = xc.shape[0]
    tr = CHUNK
    nt = t_rows // tr
    bc_w = SSM_GROUPS * D_STATE

    def body(xc_ref, x0, x1, x2, dxs_ref, db_ref, dc_ref, w_ref, dx_ref, dw_ref, dbias_ref, nxt):
        @pl.when(pl.program_id(0) == 0)
        def _():
            nxt[...] = jnp.zeros_like(nxt)
            dw_ref[...] = jnp.zeros_like(dw_ref)
            dbias_ref[...] = jnp.zeros_like(dbias_ref)

        c = xc_ref[...]
        s = _sigmoid(c)
        dact = jnp.concatenate([dxs_ref[...], db_ref[...], dc_ref[...]], axis=1)
        dpre = dact * s * (1.0 + c * (1.0 - s))
        x = jnp.concatenate([x0[...], x1[...], x2[...]], axis=1)
        n = nxt[...]
        dx = jnp.zeros_like(dpre)
        for sh in range(SSM_CONV):
            k = SSM_CONV - 1 - sh
            moved = _shift_up(dpre, n, sh)
            dx = dx + w_ref[k:k + 1, :] * moved
            dw_ref[k:k + 1, :] += jnp.sum(moved * x, axis=0, keepdims=True)
        dbias_ref[...] += jnp.sum(dpre, axis=0, keepdims=True)
        nxt[...] = dpre
        dx_ref[...] = dx.astype(BF16)

    return pl.pallas_call(
        body, grid=(nt,),
        in_specs=[_row_rev(tr, CONV_DIM, nt)] + _xbc_specs(tr, nt)
        + [_row_rev(tr, D_INNER, nt), _row_rev(tr, bc_w, nt), _row_rev(tr, bc_w, nt), _full((SSM_CONV, CONV_DIM))],
        out_specs=[_row_rev(tr, CONV_DIM, nt), _full((SSM_CONV, CONV_DIM)), _full((1, CONV_DIM))],
        out_shape=[_sds((t_rows, CONV_DIM), BF16), _sds((SSM_CONV, CONV_DIM), F32), _sds((1, CONV_DIM), F32)],
        scratch_shapes=[pltpu.VMEM((tr, CONV_DIM), F32)],
        name="ssm_conv_bwd", compiler_params=_params(1))(xc, proj, proj, proj, dxs, dbm, dcm, conv_w)


def _prenorm_bwd(x, meta, dhn, dh, w):
    t_rows = dhn.shape[0]
    tr = CHUNK

    def body(x_ref, meta_ref, d_ref, r_ref, w_ref, dx_ref, dmeta_ref, dw_ref):
        i = pl.program_id(0)

        @pl.when(i == 0)
        def _():
            dw_ref[...] = jnp.zeros_like(dw_ref)

        h = _seq_tile(x_ref, meta_ref, i)
        dx, dw = _rms_bwd(h, _rms(h), w_ref[...], d_ref[...])
        dw_ref[...] += dw
        dh_tile = r_ref[...] + dx
        dx_ref[...] = dh_tile

        @pl.when(i == 0)
        def _():
            dmeta_ref[...] = dh_tile[META_PAD:, :]

    return pl.pallas_call(
        body, grid=(t_rows // tr,), in_specs=_seq_specs() + [_row(tr, D_MODEL)] * 2 + [_full((1, D_MODEL))],
        out_specs=[pl.BlockSpec((tr, D_MODEL), lambda i: (jnp.maximum(i - 1, 0), 0)), _full((N_META, D_MODEL)),
                   _full((1, D_MODEL))],
        out_shape=[_sds((t_rows - tr, D_MODEL), F32), _sds((N_META, D_MODEL), F32), _sds((1, D_MODEL), F32)],
        name="prenorm_bwd", compiler_params=_params(1))(x, meta, dhn, dh, w)


def _dot01(x, m01, x_left, parts):
    acc, rest = None, x
    for i in range(parts):
        piece = rest.astype(BF16)
        term = (jnp.dot(piece, m01, preferred_element_type=F32) if x_left
                else jnp.dot(m01, piece, preferred_element_type=F32))
        acc = term if acc is None else acc + term
        if i + 1 < parts:
            rest = rest - piece.astype(F32)
    return acc


def _ssd_common(dtr_ref, dtb_ref, alog_ref, chunk_index):
    rows = lax.broadcasted_iota(jnp.int32, (CHUNK, CHUNK), 0)
    cols = lax.broadcasted_iota(jnp.int32, (CHUNK, CHUNK), 1)
    low = rows >= cols
    raw = dtr_ref[...] + dtb_ref[0]
    live = _row_ids(raw.shape, chunk_index, CHUNK) >= META_PAD
    dt = jnp.where(live, _softplus(raw), 0.0)
    a_head = -jnp.exp(alog_ref[0])
    cs = _dot01(dt * a_head, low.astype(BF16), False, 3)
    grow = jnp.exp(cs)
    fade = jnp.exp(cs[CHUNK - 1:CHUNK, :] - cs)
    expand = (lax.broadcasted_iota(jnp.int32, (CHUNK, GROUP_W), 1) // HEAD_P
              == lax.broadcasted_iota(jnp.int32, (CHUNK, GROUP_W), 0)).astype(BF16)
    fold = (lax.broadcasted_iota(jnp.int32, (GROUP_W, CHUNK), 0) // HEAD_P
            == lax.broadcasted_iota(jnp.int32, (GROUP_W, CHUNK), 1)).astype(BF16)
    return dict(low=low, triu=(rows <= cols).astype(BF16), raw=raw, live=live, dt=dt, a_head=a_head, cs=cs, cs_t=cs.T,
                fold=fold, dtx=_dot01(dt, expand, True, 2), growx=_dot01(grow, expand, True, 2),
                fadex=_dot01(fade, expand, True, 2))


def _decay_matrix(cm, j):
    diff = cm["cs"][:, j:j + 1] - cm["cs_t"][j:j + 1, :]
    return jnp.where(cm["low"], jnp.exp(jnp.where(cm["low"], diff, 0.0)), 0.0)


def _dot(a, b, dims):
    return lax.dot_general(a.astype(BF16), b.astype(BF16), (dims, ((), ())), preferred_element_type=F32)


def _ssd_specs(nt, rev):
    def idx(c):
        return nt - 1 - c if rev else c
    xs = pl.BlockSpec((CHUNK, GROUP_W), lambda g, c: (idx(c), g))
    bm = pl.BlockSpec((CHUNK, D_STATE), lambda g, c: (idx(c), D_INNER // D_STATE + g))
    cm = pl.BlockSpec((CHUNK, D_STATE), lambda g, c: (idx(c), D_INNER // D_STATE + SSM_GROUPS + g))
    dtr = pl.BlockSpec((CHUNK, 128), lambda g, c: (idx(c), OFF_DT // 128 + g))
    par = pl.BlockSpec((1, 1, 128), lambda g, c: (g, 0, 0))
    par_x = pl.BlockSpec((1, 1, GROUP_W), lambda g, c: (g, 0, 0))
    return xs, bm, cm, dtr, par, par_x, idx


def _ssd_fwd(xact, proj, dtb, alog, dskip_x):
    t_rows = xact.shape[0]
    nt = t_rows // CHUNK
    xs_spec, b_spec, c_spec, dtr_spec, par, par_x, _ = _ssd_specs(nt, False)

    def body(xs_ref, b_ref, c_ref, dtr_ref, dtb_ref, alog_ref, dsk_ref, y_ref, hst_ref, state):
        c = pl.program_id(1)

        @pl.when(c == 0)
        def _():
            state[...] = jnp.zeros_like(state)

        cm = _ssd_common(dtr_ref, dtb_ref, alog_ref, c)
        xs, bm, cmat = xs_ref[...], b_ref[...], c_ref[...]
        x_dt = xs * cm["dtx"]
        h_in = state[...]
        hst_ref[0, 0] = h_in
        y_ref[...] = _dot(cmat, h_in, ((1,), (0,))) * cm["growx"] + xs * dsk_ref[0]
        cb = _dot(cmat, bm, ((1,), (1,)))
        for j in range(HEADS_PER_GROUP):
            sl = slice(j * HEAD_P, (j + 1) * HEAD_P)
            y_ref[:, sl] += _dot(cb * _decay_matrix(cm, j), x_dt[:, sl], ((1,), (0,)))
        state[...] = h_in * cm["growx"][CHUNK - 1:CHUNK, :] + _dot(bm, x_dt * cm["fadex"], ((0,), (0,)))

    return pl.pallas_call(
        body, grid=(SSM_GROUPS, nt),
        in_specs=[xs_spec, b_spec, c_spec, dtr_spec, par, par, par_x],
        out_specs=[xs_spec, pl.BlockSpec((1, 1, D_STATE, GROUP_W), lambda g, c: (c, g, 0, 0))],
        out_shape=[_sds((t_rows, D_INNER), F32), _sds((nt, SSM_GROUPS, D_STATE, GROUP_W), F32)],
        scratch_shapes=[pltpu.VMEM((D_STATE, GROUP_W), F32)],
        name="ssd_fwd", compiler_params=_params(2))(xact, xact, xact, proj, dtb, alog, dskip_x)


def _ssd_bwd(xact, proj, dtb, alog, dskip_x, dy, hst):
    t_rows = xact.shape[0]
    nt = t_rows // CHUNK
    xs_spec, b_spec, c_spec, dtr_spec, par, par_x, idx = _ssd_specs(nt, True)
    h_spec = pl.BlockSpec((1, 1, D_STATE, GROUP_W), lambda g, c: (idx(c), g, 0, 0))
    hn_spec = pl.BlockSpec((1, 1, D_STATE, GROUP_W), lambda g, c: (jnp.minimum(idx(c) + 1, nt - 1), g, 0, 0))
    bc_out = pl.BlockSpec((CHUNK, D_STATE), lambda g, c: (idx(c), g))

    def body(xs_ref, b_ref, c_ref, dtr_ref, dtb_ref, alog_ref, dsk_ref, dy_ref, h_ref, hn_ref,
             dxs_ref, db_ref, dc_ref, ddt_ref, dalog_ref, ddtb_ref, dd_ref, dstate, dx_buf):
        step = pl.program_id(1)

        @pl.when(step == 0)
        def _():
            dstate[...] = jnp.zeros_like(dstate)
            dalog_ref[...] = jnp.zeros_like(dalog_ref)
            ddtb_ref[...] = jnp.zeros_like(ddtb_ref)
            dd_ref[...] = jnp.zeros_like(dd_ref)

        cm = _ssd_common(dtr_ref, dtb_ref, alog_ref, idx(step))
        xs, bm, cmat = xs_ref[...], b_ref[...], c_ref[...]
        dsk = dsk_ref[0]
        x_dt = xs * cm["dtx"]
        h_in, h_next = h_ref[0, 0], hn_ref[0, 0]
        dyv = dy_ref[...]
        dh = dstate[...]
        grow, fade = cm["growx"], cm["fadex"]
        dy_grow = dyv * grow
        x_fade = x_dt * fade
        cb = _dot(cmat, bm, ((1,), (1,)))
        ml = jnp.zeros((CHUNK, CHUNK), F32)
        row_id = lax.broadcasted_iota(jnp.int32, (CHUNK, CHUNK), 0)
        col_id = lax.broadcasted_iota(jnp.int32, (CHUNK, CHUNK), 1)
        w_rows = jnp.zeros((CHUNK, CHUNK), F32)
        w_cols = jnp.zeros((CHUNK, CHUNK), F32)
        for j in range(HEADS_PER_GROUP):
            sl = slice(j * HEAD_P, (j + 1) * HEAD_P)
            lm = _decay_matrix(cm, j)
            mlj = _dot(dyv[:, sl], x_dt[:, sl], ((1,), (1,))) * lm
            ml = ml + mlj
            wm = mlj * cb
            w_rows = jnp.where(col_id == j, jnp.sum(wm, axis=1, keepdims=True), w_rows)
            w_cols = jnp.where(row_id == j, jnp.sum(wm, axis=0, keepdims=True), w_cols)
            dx_buf[:, sl] = _dot(cb * lm, dyv[:, sl], ((0,), (0,)))
        dx_off = fade * _dot(bm, dh, ((1,), (0,)))
        dx = dx_buf[...] + dx_off
        dc_ref[...] = _dot(ml, bm, ((1,), (0,))) + _dot(dy_grow, h_in, ((1,), (1,)))
        db_ref[...] = _dot(ml, cmat, ((0,), (0,))) + _dot(x_fade, dh, ((1,), (1,)))
        fold = cm["fold"]
        y_off = _dot(cmat, h_in, ((1,), (0,))) * grow
        dcs = (w_rows - w_cols.T) + _dot01(dyv * y_off - x_dt * dx_off, fold, True, 2)
        tail = jnp.broadcast_to(jnp.sum(dh * h_next, axis=0, keepdims=True), (8, GROUP_W))
        tail = _dot01(tail, fold, True, 2)[0:1, :]
        last_row = lax.broadcasted_iota(jnp.int32, (CHUNK, 128), 0) == CHUNK - 1
        dcs = dcs + jnp.where(last_row, tail, 0.0)
        da = _dot01(dcs, cm["triu"], False, 3)
        ddt = da * cm["a_head"] + _dot01(dx * xs, fold, True, 2)
        ddt_raw = jnp.where(cm["live"], ddt * _sigmoid(cm["raw"]), 0.0)
        ddt_ref[...] = ddt_raw.astype(BF16)
        ddtb_ref[0] += jnp.sum(ddt_raw, axis=0, keepdims=True)
        dalog_ref[0] += jnp.sum(da * cm["dt"], axis=0, keepdims=True) * cm["a_head"]
        dd_ref[0] += jnp.sum(dyv * xs, axis=0, keepdims=True)
        dxs_ref[...] = dx * cm["dtx"] + dyv * dsk
        dstate[...] = dh * grow[CHUNK - 1:CHUNK, :] + _dot(cmat, dy_grow, ((0,), (0,)))

    return pl.pallas_call(
        body, grid=(SSM_GROUPS, nt),
        in_specs=[xs_spec, b_spec, c_spec, dtr_spec, par, par, par_x, xs_spec, h_spec, hn_spec],
        out_specs=[xs_spec, bc_out, bc_out, bc_out, par, par, par_x],
        out_shape=[_sds((t_rows, D_INNER), F32), _sds((t_rows, SSM_GROUPS * D_STATE), F32),
                   _sds((t_rows, SSM_GROUPS * D_STATE), F32), _sds((t_rows, SSM_GROUPS * 128), BF16),
                   _sds((SSM_GROUPS, 1, 128), F32), _sds((SSM_GROUPS, 1, 128), F32), _sds((SSM_GROUPS, 1, GROUP_W), F32)],
        scratch_shapes=[pltpu.VMEM((D_STATE, GROUP_W), F32), pltpu.VMEM((CHUNK, GROUP_W), F32)],
        name="ssd_bwd", compiler_params=_params(2))(xact, xact, xact, proj, dtb, alog, dskip_x, dy, hst, hst)


def _swa_bias():
    rows_q = ATTN_GROUP * CHUNK
    dist = (jnp.arange(rows_q) % CHUNK)[:, None] - jnp.arange(2 * CHUNK)[None, :] + CHUNK
    head = jnp.arange(KV_HEADS)[:, None] * ATTN_GROUP + jnp.arange(rows_q)[None, :] // CHUNK + 1
    slope = jnp.exp2(-8.0 * head.astype(F32) / ATTN_HEADS)
    return jnp.where((dist >= 0) & (dist < CHUNK), -slope[:, :, None] * dist.astype(F32)[None], NEG)


def _swa_probs(q_kv, k_prev, k_cur, k_first, sink, bias, n):
    rows_q = ATTN_GROUP * CHUNK
    qs = jnp.concatenate([q_kv[:, g * DH:(g + 1) * DH] for g in range(ATTN_GROUP)], axis=0) * (DH ** -0.5)
    kcat = jnp.concatenate([k_prev, k_cur], axis=0)
    kmeta = k_first[META_PAD:, :]
    key_ok = lax.broadcasted_iota(jnp.int32, (1, 2 * CHUNK), 1) + n * CHUNK >= 2 * CHUNK
    s_band = jnp.where(key_ok, _dot(qs, kcat, ((1,), (1,))) + bias, NEG)
    q_pos = lax.broadcasted_iota(jnp.int32, (rows_q, N_META), 0) % CHUNK + n * CHUNK - META_PAD
    ok_m = lax.broadcasted_iota(jnp.int32, (rows_q, N_META), 1) <= q_pos
    s_meta = jnp.where(ok_m, _dot(qs, kmeta, ((1,), (1,))), NEG)
    m = jnp.maximum(jnp.maximum(jnp.max(s_band, axis=1, keepdims=True), jnp.max(s_meta, axis=1, keepdims=True)), sink)
    p_band, p_meta, p_sink = jnp.exp(s_band - m), jnp.exp(s_meta - m), jnp.exp(sink - m)
    inv = 1.0 / (jnp.sum(p_band, axis=1, keepdims=True) + jnp.sum(p_meta, axis=1, keepdims=True) + p_sink)
    return qs, kcat, kmeta, p_band * inv, p_meta * inv, p_sink * inv


def _swa_specs(nt, rev):
    def idx(n):
        return nt - 1 - n if rev else n
    width = ATTN_HEADS * DH
    o = pl.BlockSpec((CHUNK, width), lambda n: (idx(n), 0))
    q_proj = pl.BlockSpec((CHUNK, width), lambda n: (idx(n), OFF_Q // width))
    def kv(col0, chunk_of):
        return pl.BlockSpec((CHUNK, KV_W), lambda n: (chunk_of(idx(n)), col0 // KV_W))

    chunks = (lambda c: jnp.maximum(c - 1, 0)), (lambda c: c), (lambda c: 0)
    k_specs = [kv(OFF_K, f) for f in chunks]
    v_specs = [kv(OFF_V, f) for f in chunks]
    dkv = pl.BlockSpec((CHUNK, KV_W), lambda n: (idx(n), 0))
    sink = _full((KV_HEADS, ATTN_GROUP * CHUNK, 1))
    bias = _full((KV_HEADS, ATTN_GROUP * CHUNK, 2 * CHUNK))
    return o, q_proj, k_specs, v_specs, dkv, sink, bias, idx


def _swa_fwd(proj, sink_rows, bias):
    t_rows = proj.shape[0]
    nt = t_rows // CHUNK
    o_spec, q_spec, k_specs, v_specs, _, sink_spec, bias_spec, _ = _swa_specs(nt, False)
    kv_w = ATTN_GROUP * DH

    def body(q_ref, kp_ref, kc_ref, km_ref, vp_ref, vc_ref, vm_ref, sink_ref, bias_ref, o_ref):
        n = pl.program_id(0)
        for k in range(KV_HEADS):
            hd = slice(k * DH, (k + 1) * DH)
            _, _, _, p_band, p_meta, _ = _swa_probs(q_ref[:, k * kv_w:(k + 1) * kv_w], kp_ref[:, hd], kc_ref[:, hd],
                                                    km_ref[:, hd], sink_ref[k], bias_ref[k], n)
            vcat = jnp.concatenate([vp_ref[:, hd], vc_ref[:, hd]], axis=0)
            out = _dot(p_band, vcat, ((1,), (0,))) + _dot(p_meta, vm_ref[:, hd][META_PAD:, :], ((1,), (0,)))
            for g in range(ATTN_GROUP):
                o_ref[:, k * kv_w + g * DH:k * kv_w + (g + 1) * DH] = out[g * CHUNK:(g + 1) * CHUNK, :]

    return pl.pallas_call(
        body, grid=(nt,), in_specs=[q_spec] + k_specs + v_specs + [sink_spec, bias_spec],
        out_specs=o_spec, out_shape=_sds((t_rows, ATTN_HEADS * DH), F32),
        name="swa_fwd", compiler_params=_params(1))(proj, proj, proj, proj, proj, proj, proj, sink_rows, bias)


def _swa_bwd(proj, sink_rows, bias, out, dout):
    t_rows = proj.shape[0]
    nt = t_rows // CHUNK
    o_spec, q_spec, k_specs, v_specs, dkv_spec, sink_spec, bias_spec, idx = _swa_specs(nt, True)
    kv_w = ATTN_GROUP * DH

    def body(q_ref, kp_ref, kc_ref, km_ref, vp_ref, vc_ref, vm_ref, sink_ref, bias_ref, o_ref, do_ref,
             dq_ref, dk_ref, dv_ref, dsink_ref, carry_k, carry_v, meta_k, meta_v, dk_buf, dv_buf):
        step = pl.program_id(0)
        n = idx(step)

        @pl.when(step == 0)
        def _():
            carry_k[...] = jnp.zeros_like(carry_k)
            carry_v[...] = jnp.zeros_like(carry_v)
            meta_k[...] = jnp.zeros_like(meta_k)
            meta_v[...] = jnp.zeros_like(meta_v)
            dsink_ref[...] = jnp.zeros_like(dsink_ref)

        for k in range(KV_HEADS):
            cols = slice(k * kv_w, (k + 1) * kv_w)
            hd = slice(k * DH, (k + 1) * DH)
            qs, kcat, kmeta, p_band, p_meta, p_sink = _swa_probs(q_ref[:, cols], kp_ref[:, hd], kc_ref[:, hd],
                                                                 km_ref[:, hd], sink_ref[k], bias_ref[k], n)
            vcat = jnp.concatenate([vp_ref[:, hd], vc_ref[:, hd]], axis=0)
            vmeta = vm_ref[:, hd][META_PAD:, :]
            o, do = o_ref[:, cols], do_ref[:, cols]
            os_ = jnp.concatenate([o[:, g * DH:(g + 1) * DH] for g in range(ATTN_GROUP)], axis=0)
            dos = jnp.concatenate([do[:, g * DH:(g + 1) * DH] for g in range(ATTN_GROUP)], axis=0)
            delta = jnp.sum(dos * os_, axis=1, keepdims=True)
            ds_band = p_band * (_dot(dos, vcat, ((1,), (1,))) - delta)
            ds_meta = p_meta * (_dot(dos, vmeta, ((1,), (1,))) - delta)
            ds_sink = -p_sink * delta
            dqs = (_dot(ds_band, kcat, ((1,), (0,))) + _dot(ds_meta, kmeta, ((1,), (0,)))) * (DH ** -0.5)
            for g in range(ATTN_GROUP):
                dq_ref[:, k * kv_w + g * DH:k * kv_w + (g + 1) * DH] = dqs[g * CHUNK:(g + 1) * CHUNK, :].astype(BF16)
                dsink_ref[k, g:g + 1, :] += jnp.sum(ds_sink[g * CHUNK:(g + 1) * CHUNK, :])
            dkcat = _dot(ds_band, qs, ((0,), (0,)))
            dvcat = _dot(p_band, dos, ((0,), (0,)))
            meta_k[:, hd] += _dot(ds_meta, qs, ((0,), (0,)))
            meta_v[:, hd] += _dot(p_meta, dos, ((0,), (0,)))
            dk_buf[:, hd] = dkcat[CHUNK:, :] + carry_k[:, hd]
            dv_buf[:, hd] = dvcat[CHUNK:, :] + carry_v[:, hd]
            carry_k[:, hd] = dkcat[:CHUNK, :]
            carry_v[:, hd] = dvcat[:CHUNK, :]

        @pl.when(n == 0)
        def _():
            dk_buf[META_PAD:, :] += meta_k[...]
            dv_buf[META_PAD:, :] += meta_v[...]

        dk_ref[...] = dk_buf[...].astype(BF16)
        dv_ref[...] = dv_buf[...].astype(BF16)

    return pl.pallas_call(
        body, grid=(nt,),
        in_specs=[q_spec] + k_specs + v_specs + [sink_spec, bias_spec, o_spec, o_spec],
        out_specs=[o_spec, dkv_spec, dkv_spec, _full((KV_HEADS, 8, 128))],
        out_shape=[_sds((t_rows, ATTN_HEADS * DH), BF16), _sds((t_rows, KV_W), BF16),
                   _sds((t_rows, KV_W), BF16), _sds((KV_HEADS, 8, 128), F32)],
        scratch_shapes=[pltpu.VMEM((CHUNK, KV_W), F32), pltpu.VMEM((CHUNK, KV_W), F32),
                        pltpu.VMEM((N_META, KV_W), F32), pltpu.VMEM((N_META, KV_W), F32),
                        pltpu.VMEM((CHUNK, KV_W), F32), pltpu.VMEM((CHUNK, KV_W), F32)],
        name="swa_bwd", compiler_params=_params(1))(proj, proj, proj, proj, proj, proj, proj, sink_rows, bias, out, dout)


def _pack_w_in_t(w_in_t):
    w_dt = w_in_t[CUT_DT:CUT_Q].reshape(SSM_GROUPS, HEADS_PER_GROUP, D_MODEL)
    w_dt = jnp.pad(w_dt, ((0, 0), (0, 128 - HEADS_PER_GROUP), (0, 0))).reshape(SSM_GROUPS * 128, D_MODEL)
    return jnp.concatenate([w_in_t[CUT_Z:CUT_XBC], w_in_t[CUT_G:], w_in_t[CUT_XBC:CUT_DT], w_in_t[CUT_Q:CUT_K],
                            w_in_t[CUT_K:CUT_V], w_in_t[CUT_V:CUT_G], w_dt], axis=0)


def _unpack_w_in_t(wp_t):
    w_dt = wp_t[OFF_DT:].reshape(SSM_GROUPS, 128, D_MODEL)[:, :HEADS_PER_GROUP].reshape(SSM_HEADS, D_MODEL)
    return jnp.concatenate([wp_t[OFF_Z:OFF_GATE], wp_t[OFF_XBC:OFF_Q], w_dt, wp_t[OFF_Q:OFF_K], wp_t[OFF_K:OFF_V],
                            wp_t[OFF_V:OFF_DT], wp_t[OFF_GATE:OFF_XBC]], axis=0)


def _group_rows(v, width):
    return jnp.pad(v.reshape(SSM_GROUPS, 1, HEADS_PER_GROUP), ((0, 0), (0, 0), (0, width - HEADS_PER_GROUP)))


def _local_step(x, target, wt, late_weights=None, on_grad=None, started=None):
    seq = x.shape[0]
    grads = {}

    def emit(name, g):
        grads[name] = g
        return None if on_grad is None else on_grad(name, g)
    meta = wt["meta_tokens"]
    wp_t = _pack_w_in_t(wt["w_in_t"])
    dtb = _group_rows(wt["ssm_dt_bias"].reshape(-1), 128)
    alog = _group_rows(wt["ssm_a_log"].reshape(-1), 128)
    dskip_x = jnp.repeat(wt["ssm_d_skip"].reshape(-1), HEAD_P).reshape(SSM_GROUPS, 1, GROUP_W)
    sink_rows = jnp.repeat(wt["attn_sinks"].reshape(KV_HEADS, ATTN_GROUP), CHUNK, axis=1).reshape(KV_HEADS, ATTN_GROUP * CHUNK, 1)

    hn = _prenorm(x, meta, wt["norm_pre_mix"])
    proj = _matmul(hn, wp_t, tb=True, name="in_proj", after=started)
    xc, xact = _ssm_conv_fwd(proj, wt["ssm_conv_w"], wt["ssm_conv_b"])
    y, hst = _ssd_fwd(xact, proj, dtb, alog, dskip_x)
    yn = _ssm_post(y, proj, wt["ssm_norm"])
    if late_weights is not None:
        wt = {**wt, **late_weights(yn)}
    y_ssm = _matmul(yn, wt["w_ssm_out"], name="ssm_out")
    bias = _swa_bias()
    attn = _swa_fwd(proj, sink_rows, bias)
    y_attn = _matmul(attn, wt["w_attn_out"], name="attn_out")
    mixed = _mix_fwd(proj, y_ssm, y_attn)
    mix = _matmul(mixed, wt["w_mix_out"], name="mix_out")
    h1, hn2 = _postmix(x, meta, mix, wt["norm_post_mix"], wt["norm_pre_ffn"])
    up = _matmul(hn2, wt["w_ffn_up_t"], tb=True, name="ffn_up")
    u, act = _ffn_act(up, wt["ffn_conv_w"], wt["ffn_conv_b"])
    f = _matmul(act, wt["w_ffn_down"], name="ffn_down")
    df, dy, g_norm_post_ffn, loss_row = _final(h1, f, target, wt["norm_post_ffn"])

    grads["norm_post_ffn"] = g_norm_post_ffn
    sent = emit("w_ffn_down", _matmul(act, df, ta=True, out_dtype=BF16, name="dw_ffn_down"))
    dact = _matmul(df, wt["w_ffn_down"], tb=True, name="d_act", after=sent)
    dup, grads["ffn_conv_w"], grads["ffn_conv_b"] = _ffn_act_bwd(u, up, dact, wt["ffn_conv_w"])
    sent = emit("w_ffn_up_t", _matmul(dup, hn2, ta=True, out_dtype=BF16, name="dw_ffn_up"))
    dhn2 = _matmul(dup, wt["w_ffn_up_t"], name="d_hn2", after=sent)
    dmix, dh, grads["norm_pre_ffn"], grads["norm_post_mix"] = _postmix_bwd(h1, dhn2, dy, mix, wt["norm_pre_ffn"], wt["norm_post_mix"])
    sent = emit("w_mix_out", _matmul(mixed, dmix, ta=True, out_dtype=BF16, name="dw_mix_out"))
    dmixed = _matmul(dmix, wt["w_mix_out"], tb=True, name="d_mixed", after=sent)
    dy_ssm, dy_attn, dglog = _mix_bwd(dmixed, proj, y_ssm, y_attn)
    sent = emit("w_ssm_out", _matmul(yn, dy_ssm, ta=True, out_dtype=BF16, name="dw_ssm_out"))
    dyn = _matmul(dy_ssm, wt["w_ssm_out"], tb=True, name="d_yn", after=sent)
    sent = emit("w_attn_out", _matmul(attn, dy_attn, ta=True, out_dtype=BF16, name="dw_attn_out"))
    dattn = _matmul(dy_attn, wt["w_attn_out"], tb=True, name="d_attn", after=sent)
    dy_ssd, dz, grads["ssm_norm"] = _ssm_post_bwd(y, proj, dyn, wt["ssm_norm"])
    dxs, dbm, dcm, ddt, dalog, ddtb, dd_x = _ssd_bwd(xact, proj, dtb, alog, dskip_x, dy_ssd, hst)
    grads["ssm_a_log"] = dalog[:, 0, :HEADS_PER_GROUP].reshape(1, SSM_HEADS)
    grads["ssm_dt_bias"] = ddtb[:, 0, :HEADS_PER_GROUP].reshape(1, SSM_HEADS)
    grads["ssm_d_skip"] = dd_x.reshape(SSM_HEADS, HEAD_P).sum(axis=1).reshape(1, SSM_HEADS)
    dxbc, grads["ssm_conv_w"], grads["ssm_conv_b"] = _ssm_conv_bwd(xc, proj, dxs, dbm, dcm, wt["ssm_conv_w"])
    dq, dk, dv, dsink = _swa_bwd(proj, sink_rows, bias, attn, dattn)
    grads["attn_sinks"] = dsink[:, :ATTN_GROUP, 0].reshape(1, ATTN_HEADS)
    dproj = jnp.concatenate([dz, dglog, dxbc, dq, dk, dv, ddt], axis=1)
    sent = emit("w_in_t", _unpack_w_in_t(_matmul(dproj, hn, ta=True, out_dtype=BF16, name="dw_in")))
    dhn = _matmul(dproj, wp_t, name="d_hn", after=sent)
    grad_x, grads["meta_tokens"], grads["norm_pre_mix"] = _prenorm_bwd(x, meta, dhn, dh, wt["norm_pre_mix"])
    return loss_row[0, 0], grad_x, grads


def _all_gather(shards):
    n = len(shards)

    def body(*refs):
        ins, outs = refs[:n], refs[n:2 * n]
        send_sems, recv_sems, local_sems = refs[2 * n:]
        x, y, c = lax.axis_index("x"), lax.axis_index("y"), lax.axis_index("c")
        me, sibling = (x, y, c), (x, y, 1 - c)
        chips = [(1 - x, y), (x, 1 - y), (1 - x, 1 - y)]

        def slot(a, dev):
            return outs[a].at[4 * dev[0] + 2 * dev[1] + dev[2]]

        def copy(k, a, block, to, src=None):
            return pltpu.make_async_remote_copy(
                src_ref=slot(a, block) if src is None else src, dst_ref=slot(a, block),
                send_sem=send_sems.at[k, a], recv_sem=recv_sems.at[k, a],
                device_id=to, device_id_type=pl.DeviceIdType.MESH)

        mine = [pltpu.make_async_copy(ins[a], slot(a, me), local_sems.at[a]) for a in range(n)]
        for cp in mine:
            cp.start()
        first = [copy(0, a, me, sibling, src=ins[a]) for a in range(n)]
        for j, chip in enumerate(chips):
            first += [copy(1 + j, a, me, (*chip, c), src=ins[a]) for a in range(n)]
        for cp in first:
            cp.start()
        passed = []
        for j, chip in enumerate(chips):
            for a in range(n):
                copy(1 + j, a, (*chip, c), me).wait_recv()
                fwd = copy(4 + j, a, (*chip, c), sibling)
                fwd.start()
                passed.append(fwd)
        for a in range(n):
            copy(0, a, sibling, me).wait_recv()
        for j, chip in enumerate(chips):
            for a in range(n):
                copy(4 + j, a, (*chip, 1 - c), me).wait_recv()
        for cp in first + passed:
            cp.wait_send()
        for cp in mine:
            cp.wait()

    hbm = pl.BlockSpec(memory_space=pl.ANY)
    return pl.pallas_call(
        body, in_specs=[hbm] * n, out_specs=[hbm] * n,
        out_shape=[_sds((N_DEV,) + s.shape, s.dtype) for s in shards],
        scratch_shapes=[pltpu.SemaphoreType.DMA((7, n)), pltpu.SemaphoreType.DMA((7, n)), pltpu.SemaphoreType.DMA((n,))],
        name="gather_weights")(*shards)


def _peer_table():
    x, y, c = lax.axis_index("x"), lax.axis_index("y"), lax.axis_index("c")
    peers = []
    for k in range(N_DEV - 1):
        bits = k + 1
        p = (x ^ ((bits >> 2) & 1), y ^ ((bits >> 1) & 1), c ^ (bits & 1))
        peers.append((k, p, 4 * p[0] + 2 * p[1] + p[2]))
    return 4 * x + 2 * y + c, peers


_HBM = pl.BlockSpec(memory_space=pltpu.HBM)
_SEM = pl.BlockSpec(memory_space=pltpu.SEMAPHORE)
_EFFECT = pltpu.SideEffectType.DATAFLOW_SIDE_EFFECTING


def _push_copy(src, land, send_sems, recv_sems, a, k, p, src_slot, dst_slot):
    sem = a * (N_DEV - 1) + k
    return pltpu.make_async_remote_copy(
        src_ref=src[a] if src_slot is None else src[a].at[src_slot], dst_ref=land[a].at[dst_slot],
        send_sem=send_sems.at[sem], recv_sem=recv_sems.at[sem], device_id=p, device_id_type=pl.DeviceIdType.MESH)


def _push_start(srcs, scatter, name):
    n = len(srcs)
    lands = [lax.empty(s.shape if scatter else (N_DEV,) + s.shape, s.dtype) for s in srcs]

    def body(*refs):
        src, land = refs[:n], refs[n:2 * n]
        send_sems, recv_sems, token = refs[2 * n], refs[2 * n + 1], refs[-1]
        my_id, peers = _peer_table()
        for a in range(n):
            for k, p, p_id in peers:
                _push_copy(src, land, send_sems, recv_sems, a, k, p, p_id if scatter else None, my_id).start()
        token[...] = jnp.zeros_like(token)

    sems = pltpu.SemaphoreType.DMA(((N_DEV - 1) * n,))
    res = pl.pallas_call(
        body, name=name,
        out_shape=(sems, sems, *[pltpu.HBM(a.shape, a.dtype) for a in srcs + lands], _sds((8, 128), F32)),
        in_specs=[_HBM] * (2 * n), out_specs=(_SEM, _SEM, *[_HBM] * (2 * n), pl.BlockSpec(memory_space=pltpu.VMEM)),
        input_output_aliases={i: 2 + i for i in range(2 * n)},
        compiler_params=pltpu.CompilerParams(has_side_effects=_EFFECT),
    )(*[pltpu.with_memory_space_constraint(a, pltpu.HBM) for a in srcs + lands])
    return dict(send=res[0], recv=res[1], src=list(res[2:2 + n]), land=list(res[2 + n:2 + 2 * n]), token=res[-1],
                scatter=scatter)


def _push_wait(handle, after, name):
    n = len(handle["src"])
    scatter = handle["scatter"]

    def body(*refs):
        src, land = refs[:n], refs[n:2 * n]
        send_sems, recv_sems = refs[2 * n], refs[2 * n + 1]
        _, peers = _peer_table()
        for a in range(n):
            for k, p, p_id in peers:
                cp = _push_copy(src, land, send_sems, recv_sems, a, k, p, p_id if scatter else None, p_id)
                cp.wait_send()
                cp.wait_recv()

    arrays = handle["src"] + handle["land"]
    res = pl.pallas_call(
        body, name=name, out_shape=tuple(pltpu.HBM(a.shape, a.dtype) for a in arrays),
        in_specs=[_HBM] * (2 * n) + [_SEM, _SEM, pl.BlockSpec(memory_space=pl.ANY)], out_specs=tuple([_HBM] * (2 * n)),
        input_output_aliases={i: i for i in range(2 * n)},
        compiler_params=pltpu.CompilerParams(has_side_effects=_EFFECT),
    )(*arrays, handle["send"], handle["recv"], after)
    return list(res[:n]), list(res[n:])


def _slot_sum(p_ref, own_ref):
    if own_ref is not None:
        my_id = 4 * lax.axis_index("x") + 2 * lax.axis_index("y") + lax.axis_index("c")
        mine = own_ref[...].astype(F32)
    g = None
    for s in range(p_ref.shape[0]):
        term = p_ref[s].astype(F32)
        if own_ref is not None:
            term = jnp.where(my_id == s, mine, term)
        g = term if g is None else g + term
    return g


def _to_bf16(arrays):
    n = len(arrays)

    def body(*refs):
        for i in range(n):
            refs[n + i][...] = refs[i][...].astype(BF16)

    return pl.pallas_call(body, out_shape=[_sds(a.shape, BF16) for a in arrays], name="weights_to_bf16",
                          compiler_params=pltpu.CompilerParams(vmem_limit_bytes=VMEM_LIMIT))(*arrays)


def _adamw(parts, own, w, m, v, name):
    rows, cols = w.shape
    if rows % 16 == 0:
        tr, tc = _pick(rows, (256, 128, 176, 64, 32, 16)), cols
    else:
        tr, tc = rows, _pick(cols, (256, 128))

    def body(*refs):
        if own is None:
            p_ref, w_ref, m_ref, v_ref, g_ref, d_ref, nm_ref, nv_ref = refs
            own_ref = None
        else:
            p_ref, own_ref, w_ref, m_ref, v_ref, g_ref, d_ref, nm_ref, nv_ref = refs
        g = _slot_sum(p_ref, own_ref)
        m_new = ADAM_B1 * m_ref[...] + (1.0 - ADAM_B1) * g
        v_new = ADAM_B2 * v_ref[...] + (1.0 - ADAM_B2) * (g * g)
        m_hat = m_new / (1.0 - ADAM_B1 ** ADAM_STEP)
        v_hat = v_new / (1.0 - ADAM_B2 ** ADAM_STEP)
        g_ref[...] = g
        d_ref[...] = -ADAM_LR * (m_hat / (jnp.sqrt(v_hat) + ADAM_EPS) + ADAM_WD * w_ref[...])
        nm_ref[...] = m_new
        nv_ref[...] = v_new

    by_rows = tc == cols
    spec = pl.BlockSpec((tr, tc), (lambda i: (i, 0)) if by_rows else (lambda i: (0, i)))
    parts_spec = pl.BlockSpec((parts.shape[0], tr, tc), (lambda i: (0, i, 0)) if by_rows else (lambda i: (0, 0, i)))
    operands = (parts, w, m, v) if own is None else (parts, own, w, m, v)
    return pl.pallas_call(
        body, grid=(rows // tr if by_rows else cols // tc,),
        in_specs=[parts_spec] + [spec] * (len(operands) - 1),
        out_specs=[spec] * 4, out_shape=[_sds((rows, cols), F32)] * 4,
        name=name, compiler_params=_params(1))(*operands)


SMALL_REPLICATED = (("norm_pre_mix", 1024), ("ssm_conv_b", 3072), ("ssm_dt_bias", 32), ("ssm_a_log", 32),
                    ("ssm_d_skip", 32), ("ssm_norm", 2048), ("attn_sinks", 16), ("norm_post_mix", 1024),
                    ("norm_pre_ffn", 1024), ("ffn_conv_b", 5632), ("norm_post_ffn", 1024))
SMALL_SHARDED = (("meta_tokens", (N_META, D_MODEL // N_DEV)), ("ssm_conv_w", (SSM_CONV, CONV_DIM // N_DEV)),
                 ("ffn_conv_w", (FFN_CONV, 2 * FFN_DIM // N_DEV)))
BIG = (("w_in", (D_MODEL, N_IN // N_DEV), 1), ("w_ssm_out", (D_INNER // N_DEV, D_MODEL), 0),
       ("w_attn_out", (D_MODEL // N_DEV, D_MODEL), 0), ("w_mix_out", (D_MODEL // N_DEV, D_MODEL), 0),
       ("w_ffn_up", (D_MODEL, 2 * FFN_DIM // N_DEV), 1), ("w_ffn_down", (FFN_DIM // N_DEV, D_MODEL), 0))


def _rows_of(size):
    return -(-size // 128)


def _as_rows(flat):
    size = flat.shape[-1]
    rows = _rows_of(size)
    flat = jnp.pad(flat, [(0, 0)] * (flat.ndim - 1) + [(0, rows * 128 - size)])
    return flat.reshape(flat.shape[:-1] + (rows, 128))


def _pack_small(rep, sharded):
    pieces = [_as_rows(rep[name].reshape(-1)) for name, _ in SMALL_REPLICATED]
    pieces += [_as_rows(sharded[name].reshape(-1)) for name, _ in SMALL_SHARDED]
    packed = jnp.concatenate(pieces, axis=0)
    return jnp.pad(packed, ((0, -packed.shape[0] % 8), (0, 0)))


def _unpack_small(packed):
    out, row = {}, 0
    for name, size in SMALL_REPLICATED:
        out[name] = packed[row:row + _rows_of(size)].reshape(-1)[:size].reshape(1, size)
        row += _rows_of(size)
    for name, (r, c) in SMALL_SHARDED:
        out[name] = packed[row:row + _rows_of(r * c)].reshape(-1)[:r * c].reshape(r, c)
        row += _rows_of(r * c)
    return out


def _shard_major(g, shape, axis):
    r, c = shape
    if axis == 0:
        return g.reshape(N_DEV, r, c)
    return g.reshape(r, N_DEV, c).transpose(1, 0, 2)


def kernel(x, meta_tokens, norm_pre_mix, w_in, ssm_conv_w, ssm_conv_b, ssm_dt_bias, ssm_a_log, ssm_d_skip, ssm_norm, w_ssm_out, attn_sinks, w_attn_out, w_mix_out, norm_post_mix, norm_pre_ffn, w_ffn_up, ffn_conv_w, ffn_conv_b, w_ffn_down, norm_post_ffn, loss_target, m_meta_tokens, m_norm_pre_mix, m_w_in, m_ssm_conv_w, m_ssm_conv_b, m_ssm_dt_bias, m_ssm_a_log, m_ssm_d_skip, m_ssm_norm, m_w_ssm_out, m_attn_sinks, m_w_attn_out, m_w_mix_out, m_norm_post_mix, m_norm_pre_ffn, m_w_ffn_up, m_ffn_conv_w, m_ffn_conv_b, m_w_ffn_down, m_norm_post_ffn, v_meta_tokens, v_norm_pre_mix, v_w_in, v_ssm_conv_w, v_ssm_conv_b, v_ssm_dt_bias, v_ssm_a_log, v_ssm_d_skip, v_ssm_norm, v_w_ssm_out, v_attn_sinks, v_w_attn_out, v_w_mix_out, v_norm_post_mix, v_norm_pre_ffn, v_w_ffn_up, v_ffn_conv_w, v_ffn_conv_b, v_w_ffn_down, v_norm_post_ffn):
    names = ("meta_tokens", "norm_pre_mix", "w_in", "ssm_conv_w", "ssm_conv_b", "ssm_dt_bias", "ssm_a_log", "ssm_d_skip",
             "ssm_norm", "w_ssm_out", "attn_sinks", "w_attn_out", "w_mix_out", "norm_post_mix", "norm_pre_ffn", "w_ffn_up",
             "ffn_conv_w", "ffn_conv_b", "w_ffn_down", "norm_post_ffn")
    w_loc = dict(zip(names, (meta_tokens, norm_pre_mix, w_in, ssm_conv_w, ssm_conv_b, ssm_dt_bias, ssm_a_log, ssm_d_skip,
                             ssm_norm, w_ssm_out, attn_sinks, w_attn_out, w_mix_out, norm_post_mix, norm_pre_ffn, w_ffn_up,
                             ffn_conv_w, ffn_conv_b, w_ffn_down, norm_post_ffn)))
    m_loc = dict(zip(names, (m_meta_tokens, m_norm_pre_mix, m_w_in, m_ssm_conv_w, m_ssm_conv_b, m_ssm_dt_bias, m_ssm_a_log,
                             m_ssm_d_skip, m_ssm_norm, m_w_ssm_out, m_attn_sinks, m_w_attn_out, m_w_mix_out, m_norm_post_mix,
                             m_norm_pre_ffn, m_w_ffn_up, m_ffn_conv_w, m_ffn_conv_b, m_w_ffn_down, m_norm_post_ffn)))
    v_loc = dict(zip(names, (v_meta_tokens, v_norm_pre_mix, v_w_in, v_ssm_conv_w, v_ssm_conv_b, v_ssm_dt_bias, v_ssm_a_log,
                             v_ssm_d_skip, v_ssm_norm, v_w_ssm_out, v_attn_sinks, v_w_attn_out, v_w_mix_out, v_norm_post_mix,
                             v_norm_pre_ffn, v_w_ffn_up, v_ffn_conv_w, v_ffn_conv_b, v_w_ffn_down, v_norm_post_ffn)))

    def local2d(d, name):
        a = d[name]
        return a if name == "meta_tokens" else a.reshape(a.shape[1:])

    my_id = 4 * lax.axis_index("x") + 2 * lax.axis_index("y") + lax.axis_index("c")
    big = {name: (shape, axis) for name, shape, axis in BIG}

    def whole(name, g):
        return g.reshape(N_DEV * g.shape[1], g.shape[2])

    def key(name):
        return name + "_t" if big[name][1] == 1 else name

    by_rows = [name for name, _, axis in BIG if axis == 0]
    send_bf16 = dict(zip(by_rows, _to_bf16([local2d(w_loc, name) for name in by_rows])))
    for name, _, axis in BIG:
        if axis == 1:
            send_bf16[name] = local2d(w_loc, name).T.astype(BF16)
    small_shard_pack = jnp.concatenate([_as_rows(local2d(w_loc, name).reshape(-1)) for name, _ in SMALL_SHARDED], axis=0)
    small_shard_pack = jnp.pad(small_shard_pack, ((0, -small_shard_pack.shape[0] % 8), (0, 0)))
    first = _all_gather([send_bf16["w_in"], small_shard_pack])
    rest_names = [name for name, _, _ in BIG if name != "w_in"]
    rest = [send_bf16[name] for name in rest_names]
    rest, first = lax.optimization_barrier((rest, first))
    rest_handle = _push_start(rest, False, "gather_rest_start")
    wt = {"w_in_t": whole("w_in", first[0])}
    row = 0
    for name, (r, c) in SMALL_SHARDED:
        blocks = first[1][:, row:row + _rows_of(r * c)].reshape(N_DEV, -1)[:, :r * c].reshape(N_DEV, r, c)
        wt[name] = blocks.transpose(1, 0, 2).reshape(r, N_DEV * c)
        row += _rows_of(r * c)
    for name, size in SMALL_REPLICATED:
        wt[name] = w_loc[name].reshape(1, size)

    def late_weights(after):
        own, landed = _push_wait(rest_handle, after, "gather_rest_wait")
        out = {}
        for name, mine, land in zip(rest_names, own, landed):
            out[key(name)] = whole(name, lax.dynamic_update_index_in_dim(land, mine, my_id, 0))
        return out

    sent = {}

    def on_grad(known_as, g):
        name = known_as.removesuffix("_t")
        by_owner = g.reshape(N_DEV, g.shape[0] // N_DEV, g.shape[1])
        sent[name] = _push_start([by_owner], True, "send_" + name)
        return sent[name]["token"]

    loss_part, grad_x, grads = _local_step(x[0], loss_target[0], wt, late_weights, on_grad, rest_handle["token"])
    loss = lax.psum(loss_part, AXES)

    small_parts = []
    for name, (r, c) in SMALL_SHARDED:
        small_parts.append(_as_rows(_shard_major(grads[name], (r, c), 1).reshape(N_DEV, r * c)))
    rep_rows = jnp.concatenate([_as_rows(grads[name].reshape(-1)) for name, _ in SMALL_REPLICATED], axis=0)
    small_send = jnp.concatenate([jnp.broadcast_to(rep_rows[None], (N_DEV,) + rep_rows.shape)] + small_parts, axis=1)
    small_send = jnp.pad(small_send, ((0, 0), (0, -small_send.shape[1] % 8), (0, 0)))
    small_handle = _push_start([small_send], True, "send_small")

    def small_pack(d):
        return _pack_small({name: d[name] for name, _ in SMALL_REPLICATED}, {name: local2d(d, name) for name, _ in SMALL_SHARDED})

    def arrived(handle, after, name):
        src, landed = _push_wait(handle, after, "arrived_" + name)
        return landed[0], lax.dynamic_index_in_dim(src[0], my_id, 0, keepdims=False)

    grad_w, delta_w, new_m, new_v = {}, {}, {}, {}
    outs = None
    after = small_handle["token"]
    for name, handle in sent.items():
        if name == "w_in":
            parts, own = arrived(small_handle, after, "small")
            outs = _adamw(parts, own, small_pack(w_loc), small_pack(m_loc), small_pack(v_loc), "adamw_small")
            after = outs[0]
        parts, own = arrived(handle, after, name)
        turned = big[name][1] == 1
        state = [local2d(d, name).T if turned else local2d(d, name) for d in (w_loc, m_loc, v_loc)]
        results = _adamw(parts, own, *state, "adamw_" + name)
        after = results[0]
        full = (1,) + big[name][0]
        for dst, a in zip((grad_w, delta_w, new_m, new_v), results):
            dst[name] = (a.T if turned else a).reshape(full)
    for dst, packed in zip((grad_w, delta_w, new_m, new_v), outs):
        for name, a in _unpack_small(packed).items():
            dst[name] = a.reshape(w_loc[name].shape)

    return (loss, grad_x[None], *[grad_w[n] for n in names], *[delta_w[n] for n in names],
            *[new_m[n] for n in names], *[new_v[n] for n in names])
```

```python
import jax
import jax.numpy as jnp
from jax import lax
from jax.experimental import pallas as pl
from jax.experimental.pallas import tpu as pltpu

F32 = jnp.float32
BF16 = jnp.bfloat16
HIGHEST = lax.Precision.HIGHEST

D_MODEL = 1024
N_META = 16
CHUNK = 128
META_PAD = CHUNK - N_META
D_INNER = 2048
HEAD_P = 64
SSM_HEADS = 32
SSM_GROUPS = 4
HEADS_PER_GROUP = SSM_HEADS // SSM_GROUPS
GROUP_W = HEADS_PER_GROUP * HEAD_P
D_STATE = 128
SSM_CONV = 4
CONV_DIM = D_INNER + 2 * SSM_GROUPS * D_STATE
ATTN_HEADS = 16
KV_HEADS = 4
ATTN_GROUP = ATTN_HEADS // KV_HEADS
DH = 64
KV_W = KV_HEADS * DH
FFN_DIM = 2816
FFN_CONV = 3
EPS = 1e-6
NEG = -1e30
N_DEV = 8
AXES = ("x", "y", "c")

OFF_Z, OFF_GATE, OFF_XBC, OFF_Q, OFF_K, OFF_V, OFF_DT = 0, 2048, 4096, 7168, 8192, 8448, 8704
N_INP = OFF_DT + SSM_GROUPS * 128
CUT_Z, CUT_XBC, CUT_DT, CUT_Q, CUT_K, CUT_V, CUT_G = 0, 2048, 5120, 5152, 6176, 6432, 6688
N_IN = 8736

ADAM_LR, ADAM_B1, ADAM_B2, ADAM_EPS, ADAM_WD, ADAM_STEP = 0.001, 0.9, 0.999, 1e-08, 0.01, 10

VMEM_LIMIT = 56 * 1024 * 1024


def _params(n_grid):
    return pltpu.CompilerParams(dimension_semantics=("arbitrary",) * n_grid, vmem_limit_bytes=VMEM_LIMIT)


def _sds(shape, dtype):
    return jax.ShapeDtypeStruct(shape, dtype)


def _pick(n, prefs):
    for c in prefs:
        if n % c == 0:
            return c
    raise ValueError(f"no tile of {prefs} divides {n}")


def _row(tr, width, cb=0):
    return pl.BlockSpec((tr, width), lambda i: (i, cb))


def _row_rev(tr, width, nt, cb=0):
    return pl.BlockSpec((tr, width), lambda i: (nt - 1 - i, cb))


def _full(shape):
    return pl.BlockSpec(shape, lambda *_: (0,) * len(shape))


def _sigmoid(x):
    return 1.0 / (1.0 + jnp.exp(-x))


def _softplus(x):
    return jnp.maximum(x, 0.0) + jnp.log(1.0 + jnp.exp(-jnp.abs(x)))


def _rms(x):
    return lax.rsqrt(jnp.mean(x * x, axis=-1, keepdims=True) + EPS)


def _rms_bwd(x, r, w, dy):
    xh = x * r
    g = dy * w
    dx = r * (g - xh * jnp.mean(g * xh, axis=-1, keepdims=True))
    return dx, jnp.sum(dy * xh, axis=0, keepdims=True)


def _row_ids(shape, tile_index, tr):
    return tile_index * tr + lax.broadcasted_iota(jnp.int32, shape, 0)


def _shift_down(cur, prev, s):
    if s == 0:
        return cur
    row = lax.broadcasted_iota(jnp.int32, cur.shape, 0)
    return jnp.where(row < s, pltpu.roll(prev, s, 0), pltpu.roll(cur, s, 0))


def _shift_up(cur, nxt, s):
    if s == 0:
        return cur
    n = cur.shape[0]
    row = lax.broadcasted_iota(jnp.int32, cur.shape, 0)
    return jnp.where(row >= n - s, pltpu.roll(nxt, n - s, 0), pltpu.roll(cur, n - s, 0))


def _matmul(a, b, *, ta=False, tb=False, out_dtype=F32, name, after=None):
    if ta:
        k_dim, m_dim = a.shape
    else:
        m_dim, k_dim = a.shape
    n_dim = b.shape[0] if tb else b.shape[1]
    tm = _pick(m_dim, (1408, 1024, 768, 512, 384, 256, 128))
    tn = _pick(n_dim, (1024, 1408, 768, 512, 384, 256, 128))
    if ta:
        tk = _pick(k_dim, (1408, 1024, 768, 512, 384, 256, 128))
    else:
        tk = k_dim if k_dim <= 3072 else _pick(k_dim, (3072, 2816, 2048, 1024))
    nk = k_dim // tk
    dims = (((0 if ta else 1,), (1 if tb else 0,)), ((), ()))

    use_acc = nk > 1 and out_dtype != F32

    def body(a_ref, b_ref, *rest):
        o_ref = rest[-2] if use_acc else rest[-1]
        acc_ref = rest[-1] if use_acc else o_ref
        r = lax.dot_general(a_ref[...].astype(BF16), b_ref[...].astype(BF16), dims, preferred_element_type=F32)
        if nk == 1:
            o_ref[...] = r.astype(o_ref.dtype)
        else:
            k = pl.program_id(2)

            @pl.when(k == 0)
            def _():
                acc_ref[...] = r

            @pl.when(k > 0)
            def _():
                acc_ref[...] += r

            if use_acc:
                @pl.when(k == nk - 1)
                def _():
                    o_ref[...] = acc_ref[...].astype(o_ref.dtype)

    a_spec = pl.BlockSpec((tk, tm), lambda i, j, k: (k, i)) if ta else pl.BlockSpec((tm, tk), lambda i, j, k: (i, k))
    b_spec = pl.BlockSpec((tn, tk), lambda i, j, k: (j, k)) if tb else pl.BlockSpec((tk, tn), lambda i, j, k: (k, j))
    extra_specs, extra = ([], ()) if after is None else ([pl.BlockSpec(memory_space=pl.ANY)], (after,))
    return pl.pallas_call(
        body, grid=(m_dim // tm, n_dim // tn, nk), in_specs=[a_spec, b_spec] + extra_specs,
        out_specs=pl.BlockSpec((tm, tn), lambda i, j, k: (i, j)), out_shape=_sds((m_dim, n_dim), out_dtype),
        scratch_shapes=[pltpu.VMEM((tm, tn), F32)] if use_acc else [],
        name=name, compiler_params=_params(3))(a, b, *extra)


def _seq_specs():
    return [pl.BlockSpec((CHUNK, D_MODEL), lambda i: (jnp.maximum(i - 1, 0), 0)), _full((N_META, D_MODEL))]


def _seq_tile(x_ref, meta_ref, i):
    first = jnp.concatenate([jnp.zeros((META_PAD, D_MODEL), F32), meta_ref[...]], axis=0)
    return jnp.where(i == 0, first, x_ref[...])


def _prenorm(x, meta, w):
    t_rows = x.shape[0] + CHUNK

    def body(x_ref, meta_ref, w_ref, o_ref):
        h = _seq_tile(x_ref, meta_ref, pl.program_id(0))
        o_ref[...] = (h * _rms(h) * w_ref[...]).astype(BF16)

    return pl.pallas_call(body, grid=(t_rows // CHUNK,), in_specs=_seq_specs() + [_full((1, D_MODEL))],
                          out_specs=_row(CHUNK, D_MODEL), out_shape=_sds((t_rows, D_MODEL), BF16),
                          name="prenorm", compiler_params=_params(1))(x, meta, w)


def _xbc_specs(tr, rev_nt=None):
    cbs = [OFF_XBC // 1024 + j for j in range(CONV_DIM // 1024)]
    if rev_nt is None:
        return [_row(tr, 1024, cb) for cb in cbs]
    return [_row_rev(tr, 1024, rev_nt, cb) for cb in cbs]


def _ssm_conv_fwd(proj, conv_w, conv_b):
    t_rows = proj.shape[0]
    tr = CHUNK

    def body(x0, x1, x2, w_ref, b_ref, xc_ref, xa_ref, prev):
        @pl.when(pl.program_id(0) == 0)
        def _():
            prev[...] = jnp.zeros_like(prev)

        x = jnp.concatenate([x0[...], x1[...], x2[...]], axis=1)
        p = prev[...]
        acc = b_ref[...] + w_ref[SSM_CONV - 1:SSM_CONV, :] * x
        for s in range(1, SSM_CONV):
            acc = acc + w_ref[SSM_CONV - 1 - s:SSM_CONV - s, :] * _shift_down(x, p, s)
        prev[...] = x
        xc_ref[...] = acc
        xa_ref[...] = acc * _sigmoid(acc)

    return pl.pallas_call(
        body, grid=(t_rows // tr,),
        in_specs=_xbc_specs(tr) + [_full((SSM_CONV, CONV_DIM)), _full((1, CONV_DIM))],
        out_specs=[_row(tr, CONV_DIM), _row(tr, CONV_DIM)],
        out_shape=[_sds((t_rows, CONV_DIM), F32), _sds((t_rows, CONV_DIM), F32)],
        scratch_shapes=[pltpu.VMEM((tr, CONV_DIM), F32)],
        name="ssm_conv_fwd", compiler_params=_params(1))(proj, proj, proj, conv_w, conv_b)


def _ssm_post(y, proj, w):
    t_rows = y.shape[0]
    tr = CHUNK

    def body(y_ref, z_ref, w_ref, o_ref):
        z = z_ref[...]
        yz = y_ref[...] * z * _sigmoid(z)
        o_ref[...] = (yz * _rms(yz) * w_ref[...]).astype(BF16)

    return pl.pallas_call(body, grid=(t_rows // tr,),
                          in_specs=[_row(tr, D_INNER), _row(tr, D_INNER, OFF_Z // D_INNER), _full((1, D_INNER))],
                          out_specs=_row(tr, D_INNER), out_shape=_sds((t_rows, D_INNER), BF16),
                          name="ssm_post", compiler_params=_params(1))(y, proj, w)


def _mix_fwd(proj, y_ssm, y_attn):
    t_rows = y_ssm.shape[0]
    tr = _pick(t_rows, (384, 128))

    def body(g_ref, ys_ref, ya_ref, o_ref):
        g = _sigmoid(g_ref[...])
        o_ref[...] = (g[:, :D_MODEL] * ys_ref[...] + g[:, D_MODEL:] * ya_ref[...]).astype(BF16)

    return pl.pallas_call(body, grid=(t_rows // tr,),
                          in_specs=[_row(tr, 2 * D_MODEL, OFF_GATE // (2 * D_MODEL)), _row(tr, D_MODEL), _row(tr, D_MODEL)],
                          out_specs=_row(tr, D_MODEL), out_shape=_sds((t_rows, D_MODEL), BF16),
                          name="mix_fwd", compiler_params=_params(1))(proj, y_ssm, y_attn)


def _postmix(x, meta, mix, w_post, w_pre):
    t_rows = mix.shape[0]
    tr = CHUNK

    def body(x_ref, meta_ref, m_ref, wp_ref, wf_ref, h1_ref, hn_ref):
        m = m_ref[...]
        h1 = _seq_tile(x_ref, meta_ref, pl.program_id(0)) + m * _rms(m) * wp_ref[...]
        h1 = jnp.where(_row_ids(h1.shape, pl.program_id(0), tr) >= META_PAD, h1, 0.0)
        h1_ref[...] = h1
        hn_ref[...] = (h1 * _rms(h1) * wf_ref[...]).astype(BF16)

    return pl.pallas_call(body, grid=(t_rows // tr,),
                          in_specs=_seq_specs() + [_row(tr, D_MODEL), _full((1, D_MODEL)), _full((1, D_MODEL))],
                          out_specs=[_row(tr, D_MODEL), _row(tr, D_MODEL)],
                          out_shape=[_sds((t_rows, D_MODEL), F32), _sds((t_rows, D_MODEL), BF16)],
                          name="postmix", compiler_params=_params(1))(x, meta, mix, w_post, w_pre)


def _ffn_act(up, conv_w, conv_b):
    t_rows = up.shape[0]
    tr = CHUNK
    width = 2 * FFN_DIM

    def body(up_ref, w_ref, b_ref, u_ref, act_ref, prev):
        @pl.when(pl.program_id(0) == 0)
        def _():
            prev[...] = jnp.zeros_like(prev)

        x = up_ref[...]
        p = prev[...]
        u = b_ref[...] + w_ref[FFN_CONV - 1:FFN_CONV, :] * x
        for s in range(1, FFN_CONV):
            u = u + w_ref[FFN_CONV - 1 - s:FFN_CONV - s, :] * _shift_down(x, p, s)
        prev[...] = x
        u_ref[...] = u
        a = u[:, :FFN_DIM]
        act_ref[...] = (a * _sigmoid(a) * u[:, FFN_DIM:]).astype(BF16)

    return pl.pallas_call(
        body, grid=(t_rows // tr,), in_specs=[_row(tr, width), _full((FFN_CONV, width)), _full((1, width))],
        out_specs=[_row(tr, width), _row(tr, FFN_DIM)],
        out_shape=[_sds((t_rows, width), F32), _sds((t_rows, FFN_DIM), BF16)],
        scratch_shapes=[pltpu.VMEM((tr, width), F32)],
        name="ffn_act", compiler_params=_params(1))(up, conv_w, conv_b)


def _final(h1, f, target, w):
    t_rows = h1.shape[0]
    tr = CHUNK

    def body(h1_ref, f_ref, t_ref, w_ref, df_ref, dy_ref, dw_ref, loss_ref):
        i = pl.program_id(0)

        @pl.when(i == 0)
        def _():
            dw_ref[...] = jnp.zeros_like(dw_ref)
            loss_ref[...] = jnp.zeros_like(loss_ref)

        f_val = f_ref[...]
        r = _rms(f_val)
        wv = w_ref[...]
        h2 = h1_ref[...] + f_val * r * wv
        diff = jnp.where(i >= 1, h2 - t_ref[...], 0.0)
        loss_ref[...] += 0.5 * jnp.sum(diff * diff) * (1.0 / D_MODEL)
        dy = diff * (1.0 / D_MODEL)
        dy_ref[...] = dy
        df, dw = _rms_bwd(f_val, r, wv, dy)
        df_ref[...] = df.astype(BF16)
        dw_ref[...] += dw

    tgt_spec = pl.BlockSpec((tr, D_MODEL), lambda i: (jnp.maximum(i - 1, 0), 0))
    return pl.pallas_call(
        body, grid=(t_rows // tr,),
        in_specs=[_row(tr, D_MODEL), _row(tr, D_MODEL), tgt_spec, _full((1, D_MODEL))],
        out_specs=[_row(tr, D_MODEL), _row(tr, D_MODEL), _full((1, D_MODEL)), _full((1, 128))],
        out_shape=[_sds((t_rows, D_MODEL), BF16), _sds((t_rows, D_MODEL), F32), _sds((1, D_MODEL), F32), _sds((1, 128), F32)],
        name="final", compiler_params=_params(1))(h1, f, target, w)


def _ffn_act_bwd(u, up, dact, conv_w):
    t_rows = u.shape[0]
    tr = CHUNK
    nt = t_rows // tr
    width = 2 * FFN_DIM

    def body(u_ref, up_ref, da_ref, w_ref, dup_ref, dw_ref, db_ref, nxt):
        @pl.when(pl.program_id(0) == 0)
        def _():
            nxt[...] = jnp.zeros_like(nxt)
            dw_ref[...] = jnp.zeros_like(dw_ref)
            db_ref[...] = jnp.zeros_like(db_ref)

        u_val = u_ref[...]
        a, g = u_val[:, :FFN_DIM], u_val[:, FFN_DIM:]
        d = da_ref[...]
        s = _sigmoid(a)
        du = jnp.concatenate([d * g * s * (1.0 + a * (1.0 - s)), d * a * s], axis=1)
        n = nxt[...]
        x = up_ref[...]
        dup = jnp.zeros_like(du)
        for sh in range(FFN_CONV):
            k = FFN_CONV - 1 - sh
            moved = _shift_up(du, n, sh)
            dup = dup + w_ref[k:k + 1, :] * moved
            dw_ref[k:k + 1, :] += jnp.sum(moved * x, axis=0, keepdims=True)
        db_ref[...] += jnp.sum(du, axis=0, keepdims=True)
        nxt[...] = du
        dup_ref[...] = dup.astype(BF16)

    return pl.pallas_call(
        body, grid=(nt,),
        in_specs=[_row_rev(tr, width, nt), _row_rev(tr, width, nt), _row_rev(tr, FFN_DIM, nt), _full((FFN_CONV, width))],
        out_specs=[_row_rev(tr, width, nt), _full((FFN_CONV, width)), _full((1, width))],
        out_shape=[_sds((t_rows, width), BF16), _sds((FFN_CONV, width), F32), _sds((1, width), F32)],
        scratch_shapes=[pltpu.VMEM((tr, width), F32)],
        name="ffn_act_bwd", compiler_params=_params(1))(u, up, dact, conv_w)


def _postmix_bwd(h1, dhn2, dy, mix, w_pre, w_post):
    t_rows = h1.shape[0]
    tr = CHUNK

    def body(h1_ref, dhn_ref, dy_ref, m_ref, wf_ref, wp_ref, dmix_ref, dh_ref, dwf_ref, dwp_ref):
        @pl.when(pl.program_id(0) == 0)
        def _():
            dwf_ref[...] = jnp.zeros_like(dwf_ref)
            dwp_ref[...] = jnp.zeros_like(dwp_ref)

        h1v = h1_ref[...]
        dx, dwf = _rms_bwd(h1v, _rms(h1v), wf_ref[...], dhn_ref[...])
        dwf_ref[...] += dwf
        dh1 = dy_ref[...] + dx
        dh1 = jnp.where(_row_ids(dh1.shape, pl.program_id(0), tr) >= META_PAD, dh1, 0.0)
        dh_ref[...] = dh1
        m = m_ref[...]
        dmix, dwp = _rms_bwd(m, _rms(m), wp_ref[...], dh1)
        dwp_ref[...] += dwp
        dmix_ref[...] = dmix.astype(BF16)

    return pl.pallas_call(
        body, grid=(t_rows // tr,),
        in_specs=[_row(tr, D_MODEL)] * 4 + [_full((1, D_MODEL))] * 2,
        out_specs=[_row(tr, D_MODEL), _row(tr, D_MODEL), _full((1, D_MODEL)), _full((1, D_MODEL))],
        out_shape=[_sds((t_rows, D_MODEL), BF16), _sds((t_rows, D_MODEL), F32), _sds((1, D_MODEL), F32), _sds((1, D_MODEL), F32)],
        name="postmix_bwd", compiler_params=_params(1))(h1, dhn2, dy, mix, w_pre, w_post)


def _mix_bwd(dmixed, proj, y_ssm, y_attn):
    t_rows = dmixed.shape[0]
    tr = _pick(t_rows, (384, 128))

    def body(d_ref, g_ref, ys_ref, ya_ref, dys_ref, dya_ref, dg_ref):
        d = d_ref[...]
        g = _sigmoid(g_ref[...])
        g1, g2 = g[:, :D_MODEL], g[:, D_MODEL:]
        dys_ref[...] = (d * g1).astype(BF16)
        dya_ref[...] = (d * g2).astype(BF16)
        dg_ref[...] = jnp.concatenate([d * ys_ref[...] * g1 * (1.0 - g1), d * ya_ref[...] * g2 * (1.0 - g2)],
                                      axis=1).astype(BF16)

    return pl.pallas_call(
        body, grid=(t_rows // tr,),
        in_specs=[_row(tr, D_MODEL), _row(tr, 2 * D_MODEL, OFF_GATE // (2 * D_MODEL)), _row(tr, D_MODEL), _row(tr, D_MODEL)],
        out_specs=[_row(tr, D_MODEL), _row(tr, D_MODEL), _row(tr, 2 * D_MODEL)],
        out_shape=[_sds((t_rows, D_MODEL), BF16), _sds((t_rows, D_MODEL), BF16), _sds((t_rows, 2 * D_MODEL), BF16)],
        name="mix_bwd", compiler_params=_params(1))(dmixed, proj, y_ssm, y_attn)


def _ssm_post_bwd(y, proj, dyn, w):
    t_rows = y.shape[0]
    tr = CHUNK

    def body(y_ref, z_ref, d_ref, w_ref, dy_ref, dz_ref, dw_ref):
        @pl.when(pl.program_id(0) == 0)
        def _():
            dw_ref[...] = jnp.zeros_like(dw_ref)

        yv, z = y_ref[...], z_ref[...]
        sz = _sigmoid(z)
        silu = z * sz
        yz = yv * silu
        dyz, dw = _rms_bwd(yz, _rms(yz), w_ref[...], d_ref[...])
        dw_ref[...] += dw
        dy_ref[...] = dyz * silu
        dz_ref[...] = (dyz * yv * sz * (1.0 + z * (1.0 - sz))).astype(BF16)

    return pl.pallas_call(
        body, grid=(t_rows // tr,),
        in_specs=[_row(tr, D_INNER), _row(tr, D_INNER, OFF_Z // D_INNER), _row(tr, D_INNER), _full((1, D_INNER))],
        out_specs=[_row(tr, D_INNER), _row(tr, D_INNER), _full((1, D_INNER))],
        out_shape=[_sds((t_rows, D_INNER), F32), _sds((t_rows, D_INNER), BF16), _sds((1, D_INNER), F32)],
        name="ssm_post_bwd", compiler_params=_params(1))(y, proj, dyn, w)


def _ssm_conv_bwd(xc, proj, dxs, dbm, dcm, conv_w):
    t_rows = xc.shape[0]
    tr = CHUNK
    nt = t_rows // tr
    bc_w = SSM_GROUPS * D_STATE

    def body(xc_ref, x0, x1, x2, dxs_ref, db_ref, dc_ref, w_ref, dx_ref, dw_ref, dbias_ref, nxt):
        @pl.when(pl.program_id(0) == 0)
        def _():
            nxt[...] = jnp.zeros_like(nxt)
            dw_ref[...] = jnp.zeros_like(dw_ref)
            dbias_ref[...] = jnp.zeros_like(dbias_ref)

        c = xc_ref[...]
        s = _sigmoid(c)
        dact = jnp.concatenate([dxs_ref[...], db_ref[...], dc_ref[...]], axis=1)
        dpre = dact * s * (1.0 + c * (1.0 - s))
        x = jnp.concatenate([x0[...], x1[...], x2[...]], axis=1)
        n = nxt[...]
        dx = jnp.zeros_like(dpre)
        for sh in range(SSM_CONV):
            k = SSM_CONV - 1 - sh
            moved = _shift_up(dpre, n, sh)
            dx = dx + w_ref[k:k + 1, :] * moved
            dw_ref[k:k + 1, :] += jnp.sum(moved * x, axis=0, keepdims=True)
        dbias_ref[...] += jnp.sum(dpre, axis=0, keepdims=True)
        nxt[...] = dpre
        dx_ref[...] = dx.astype(BF16)

    return pl.pallas_call(
        body, grid=(nt,),
        in_specs=[_row_rev(tr, CONV_DIM, nt)] + _xbc_specs(tr, nt)
        + [_row_rev(tr, D_INNER, nt), _row_rev(tr, bc_w, nt), _row_rev(tr, bc_w, nt), _full((SSM_CONV, CONV_DIM))],
        out_specs=[_row_rev(tr, CONV_DIM, nt), _full((SSM_CONV, CONV_DIM)), _full((1, CONV_DIM))],
        out_shape=[_sds((t_rows, CONV_DIM), BF16), _sds((SSM_CONV, CONV_DIM), F32), _sds((1, CONV_DIM), F32)],
        scratch_shapes=[pltpu.VMEM((tr, CONV_DIM), F32)],
        name="ssm_conv_bwd", compiler_params=_params(1))(xc, proj, proj, proj, dxs, dbm, dcm, conv_w)


def _prenorm_bwd(x, meta, dhn, dh, w):
    t_rows = dhn.shape[0]
    tr = CHUNK

    def body(x_ref, meta_ref, d_ref, r_ref, w_ref, dx_ref, dmeta_ref, dw_ref):
        i = pl.program_id(0)

        @pl.when(i == 0)
        def _():
            dw_ref[...] = jnp.zeros_like(dw_ref)

        h = _seq_tile(x_ref, meta_ref, i)
        dx, dw = _rms_bwd(h, _rms(h), w_ref[...], d_ref[...])
        dw_ref[...] += dw
        dh_tile = r_ref[...] + dx
        dx_ref[...] = dh_tile

        @pl.when(i == 0)
        def _():
            dmeta_ref[...] = dh_tile[META_PAD:, :]

    return pl.pallas_call(
        body, grid=(t_rows // tr,), in_specs=_seq_specs() + [_row(tr, D_MODEL)] * 2 + [_full((1, D_MODEL))],
        out_specs=[pl.BlockSpec((tr, D_MODEL), lambda i: (jnp.maximum(i - 1, 0), 0)), _full((N_META, D_MODEL)),
                   _full((1, D_MODEL))],
        out_shape=[_sds((t_rows - tr, D_MODEL), F32), _sds((N_META, D_MODEL), F32), _sds((1, D_MODEL), F32)],
        name="prenorm_bwd", compiler_params=_params(1))(x, meta, dhn, dh, w)


def _dot01(x, m01, x_left, parts):
    acc, rest = None, x
    for i in range(parts):
        piece = rest.astype(BF16)
        term = (jnp.dot(piece, m01, preferred_element_type=F32) if x_left
                else jnp.dot(m01, piece, preferred_element_type=F32))
        acc = term if acc is None else acc + term
        if i + 1 < parts:
            rest = rest - piece.astype(F32)
    return acc


def _ssd_common(dt_raw, dt_bias, a_log, chunk_index):
    rows = lax.broadcasted_iota(jnp.int32, (CHUNK, CHUNK), 0)
    cols = lax.broadcasted_iota(jnp.int32, (CHUNK, CHUNK), 1)
    low = rows >= cols
    raw = dt_raw + dt_bias
    live = _row_ids(raw.shape, chunk_index, CHUNK) >= META_PAD
    dt = jnp.where(live, _softplus(raw), 0.0)
    a_head = -jnp.exp(a_log)
    cs = _dot01(dt * a_head, low.astype(BF16), False, 3)
    grow = jnp.exp(cs)
    fade = jnp.exp(cs[CHUNK - 1:CHUNK, :] - cs)
    expand = (lax.broadcasted_iota(jnp.int32, (CHUNK, GROUP_W), 1) // HEAD_P
              == lax.broadcasted_iota(jnp.int32, (CHUNK, GROUP_W), 0)).astype(BF16)
    fold = (lax.broadcasted_iota(jnp.int32, (GROUP_W, CHUNK), 0) // HEAD_P
            == lax.broadcasted_iota(jnp.int32, (GROUP_W, CHUNK), 1)).astype(BF16)
    return dict(low=low, triu=(rows <= cols).astype(BF16), raw=raw, live=live, dt=dt, a_head=a_head, cs=cs, cs_t=cs.T,
                fold=fold, dtx=_dot01(dt, expand, True, 2), growx=_dot01(grow, expand, True, 2),
                fadex=_dot01(fade, expand, True, 2))


def _decay_matrix(cm, j):
    diff = cm["cs"][:, j:j + 1] - cm["cs_t"][j:j + 1, :]
    return jnp.where(cm["low"], jnp.exp(jnp.where(cm["low"], diff, 0.0)), 0.0)


def _dot(a, b, dims):
    return lax.dot_general(a.astype(BF16), b.astype(BF16), (dims, ((), ())), preferred_element_type=F32)


def _dot_fine(a, b, dims):
    a_hi, b_hi = a.astype(BF16), b.astype(BF16)
    a_lo, b_lo = (a - a_hi.astype(F32)).astype(BF16), (b - b_hi.astype(F32)).astype(BF16)
    dn = (dims, ((), ()))
    return (lax.dot_general(a_hi, b_hi, dn, preferred_element_type=F32)
            + lax.dot_general(a_hi, b_lo, dn, preferred_element_type=F32)
            + lax.dot_general(a_lo, b_hi, dn, preferred_element_type=F32))


def _ssd_specs(nt, rev):
    def idx(c):
        return nt - 1 - c if rev else c
    bc_w = SSM_GROUPS * D_STATE
    xs = pl.BlockSpec((CHUNK, D_INNER), lambda c: (idx(c), 0))
    bm = pl.BlockSpec((CHUNK, bc_w), lambda c: (idx(c), D_INNER // bc_w))
    cm = pl.BlockSpec((CHUNK, bc_w), lambda c: (idx(c), D_INNER // bc_w + 1))
    dtr = pl.BlockSpec((CHUNK, SSM_GROUPS * 128), lambda c: (idx(c), OFF_DT // (SSM_GROUPS * 128)))
    par = _full((SSM_GROUPS, 1, 128))
    par_x = _full((SSM_GROUPS, 1, GROUP_W))
    return xs, bm, cm, dtr, par, par_x, idx


def _group_cols(g, width):
    return slice(g * width, (g + 1) * width)


def _ssd_fwd(xact, proj, dtb, alog, dskip_x):
    t_rows = xact.shape[0]
    nt = t_rows // CHUNK
    xs_spec, b_spec, c_spec, dtr_spec, par, par_x, _ = _ssd_specs(nt, False)

    def body(xs_ref, b_ref, c_ref, dtr_ref, dtb_ref, alog_ref, dsk_ref, y_ref, hst_ref, state):
        c = pl.program_id(0)

        @pl.when(c == 0)
        def _():
            state[...] = jnp.zeros_like(state)

        for g in range(SSM_GROUPS):
            wide, narrow = _group_cols(g, GROUP_W), _group_cols(g, D_STATE)
            cm = _ssd_common(dtr_ref[:, narrow], dtb_ref[g], alog_ref[g], c)
            xs, bm, cmat = xs_ref[:, wide], b_ref[:, narrow], c_ref[:, narrow]
            x_dt = xs * cm["dtx"]
            h_in = state[g]
            hst_ref[0, g] = h_in
            y_ref[:, wide] = _dot(cmat, h_in, ((1,), (0,))) * cm["growx"] + xs * dsk_ref[g]
            cb = _dot(cmat, bm, ((1,), (1,)))
            for j in range(HEADS_PER_GROUP):
                sl = slice(g * GROUP_W + j * HEAD_P, g * GROUP_W + (j + 1) * HEAD_P)
                y_ref[:, sl] += _dot(cb * _decay_matrix(cm, j), x_dt[:, j * HEAD_P:(j + 1) * HEAD_P], ((1,), (0,)))
            state[g] = h_in * cm["growx"][CHUNK - 1:CHUNK, :] + _dot_fine(bm, x_dt * cm["fadex"], ((0,), (0,)))

    return pl.pallas_call(
        body, grid=(nt,),
        in_specs=[xs_spec, b_spec, c_spec, dtr_spec, par, par, par_x],
        out_specs=[xs_spec, pl.BlockSpec((1, SSM_GROUPS, D_STATE, GROUP_W), lambda c: (c, 0, 0, 0))],
        out_shape=[_sds((t_rows, D_INNER), F32), _sds((nt, SSM_GROUPS, D_STATE, GROUP_W), F32)],
        scratch_shapes=[pltpu.VMEM((SSM_GROUPS, D_STATE, GROUP_W), F32)],
        name="ssd_fwd", compiler_params=_params(1))(xact, xact, xact, proj, dtb, alog, dskip_x)


def _ssd_bwd(xact, proj, dtb, alog, dskip_x, dy, hst):
    t_rows = xact.shape[0]
    nt = t_rows // CHUNK
    xs_spec, b_spec, c_spec, dtr_spec, par, par_x, idx = _ssd_specs(nt, True)
    h_spec = pl.BlockSpec((1, SSM_GROUPS, D_STATE, GROUP_W), lambda c: (idx(c), 0, 0, 0))
    hn_spec = pl.BlockSpec((1, SSM_GROUPS, D_STATE, GROUP_W), lambda c: (jnp.minimum(idx(c) + 1, nt - 1), 0, 0, 0))
    bc_out = pl.BlockSpec((CHUNK, SSM_GROUPS * D_STATE), lambda c: (idx(c), 0))

    def body(xs_ref, b_ref, c_ref, dtr_ref, dtb_ref, alog_ref, dsk_ref, dy_ref, h_ref, hn_ref,
             dxs_ref, db_ref, dc_ref, ddt_ref, dalog_ref, ddtb_ref, dd_ref, dstate, dx_buf):
        step = pl.program_id(0)

        @pl.when(step == 0)
        def _():
            dstate[...] = jnp.zeros_like(dstate)
            dalog_ref[...] = jnp.zeros_like(dalog_ref)
            ddtb_ref[...] = jnp.zeros_like(ddtb_ref)
            dd_ref[...] = jnp.zeros_like(dd_ref)

        for g in range(SSM_GROUPS):
            _ssd_bwd_group(g, idx(step), xs_ref, b_ref, c_ref, dtr_ref, dtb_ref, alog_ref, dsk_ref, dy_ref, h_ref, hn_ref,
                           dxs_ref, db_ref, dc_ref, ddt_ref, dalog_ref, ddtb_ref, dd_ref, dstate, dx_buf)

    return pl.pallas_call(
        body, grid=(nt,),
        in_specs=[xs_spec, b_spec, c_spec, dtr_spec, par, par, par_x, xs_spec, h_spec, hn_spec],
        out_specs=[xs_spec, bc_out, bc_out, bc_out, par, par, par_x],
        out_shape=[_sds((t_rows, D_INNER), F32), _sds((t_rows, SSM_GROUPS * D_STATE), F32),
                   _sds((t_rows, SSM_GROUPS * D_STATE), F32), _sds((t_rows, SSM_GROUPS * 128), BF16),
                   _sds((SSM_GROUPS, 1, 128), F32), _sds((SSM_GROUPS, 1, 128), F32), _sds((SSM_GROUPS, 1, GROUP_W), F32)],
        scratch_shapes=[pltpu.VMEM((SSM_GROUPS, D_STATE, GROUP_W), F32), pltpu.VMEM((CHUNK, GROUP_W), F32)],
        name="ssd_bwd", compiler_params=_params(1))(xact, xact, xact, proj, dtb, alog, dskip_x, dy, hst, hst)


def _ssd_bwd_group(g, chunk, xs_ref, b_ref, c_ref, dtr_ref, dtb_ref, alog_ref, dsk_ref, dy_ref, h_ref, hn_ref,
                   dxs_ref, db_ref, dc_ref, ddt_ref, dalog_ref, ddtb_ref, dd_ref, dstate, dx_buf):
    if True:
        wide, narrow = _group_cols(g, GROUP_W), _group_cols(g, D_STATE)
        cm = _ssd_common(dtr_ref[:, narrow], dtb_ref[g], alog_ref[g], chunk)
        xs, bm, cmat = xs_ref[:, wide], b_ref[:, narrow], c_ref[:, narrow]
        dsk = dsk_ref[g]
        x_dt = xs * cm["dtx"]
        h_in, h_next = h_ref[0, g], hn_ref[0, g]
        dyv = dy_ref[:, wide]
        dh = dstate[g]
        grow, fade = cm["growx"], cm["fadex"]
        dy_grow = dyv * grow
        x_fade = x_dt * fade
        cb = _dot(cmat, bm, ((1,), (1,)))
        ml = jnp.zeros((CHUNK, CHUNK), F32)
        row_id = lax.broadcasted_iota(jnp.int32, (CHUNK, CHUNK), 0)
        col_id = lax.broadcasted_iota(jnp.int32, (CHUNK, CHUNK), 1)
        w_rows = jnp.zeros((CHUNK, CHUNK), F32)
        w_cols = jnp.zeros((CHUNK, CHUNK), F32)
        for j in range(HEADS_PER_GROUP):
            sl = slice(j * HEAD_P, (j + 1) * HEAD_P)
            lm = _decay_matrix(cm, j)
            mlj = _dot(dyv[:, sl], x_dt[:, sl], ((1,), (1,))) * lm
            ml = ml + mlj
            wm = mlj * cb
            w_rows = jnp.where(col_id == j, jnp.sum(wm, axis=1, keepdims=True), w_rows)
            w_cols = jnp.where(row_id == j, jnp.sum(wm, axis=0, keepdims=True), w_cols)
            dx_buf[:, sl] = _dot(cb * lm, dyv[:, sl], ((0,), (0,)))
        dx_off = fade * _dot_fine(bm, dh, ((1,), (0,)))
        dx = dx_buf[...] + dx_off
        dc_ref[:, narrow] = _dot(ml, bm, ((1,), (0,))) + _dot(dy_grow, h_in, ((1,), (1,)))
        db_ref[:, narrow] = _dot(ml, cmat, ((0,), (0,))) + _dot(x_fade, dh, ((1,), (1,)))
        fold = cm["fold"]
        y_off = _dot_fine(cmat, h_in, ((1,), (0,))) * grow
        dcs = (w_rows - w_cols.T) + _dot01(dyv * y_off - x_dt * dx_off, fold, True, 2)
        tail = jnp.broadcast_to(jnp.sum(dh * h_next, axis=0, keepdims=True), (8, GROUP_W))
        tail = _dot01(tail, fold, True, 2)[0:1, :]
        last_row = lax.broadcasted_iota(jnp.int32, (CHUNK, 128), 0) == CHUNK - 1
        dcs = dcs + jnp.where(last_row, tail, 0.0)
        da = _dot01(dcs, cm["triu"], False, 3)
        ddt = da * cm["a_head"] + _dot01(dx * xs, fold, True, 2)
        ddt_raw = jnp.where(cm["live"], ddt * _sigmoid(cm["raw"]), 0.0)
        ddt_ref[:, narrow] = ddt_raw.astype(BF16)
        ddtb_ref[g] += jnp.sum(ddt_raw, axis=0, keepdims=True)
        dalog_ref[g] += jnp.sum(da * cm["dt"], axis=0, keepdims=True) * cm["a_head"]
        dd_ref[g] += jnp.sum(dyv * xs, axis=0, keepdims=True)
        dxs_ref[:, wide] = dx * cm["dtx"] + dyv * dsk
        dstate[g] = dh * grow[CHUNK - 1:CHUNK, :] + _dot_fine(cmat, dy_grow, ((0,), (0,)))


def _swa_bias():
    rows_q = ATTN_GROUP * CHUNK
    dist = (jnp.arange(rows_q) % CHUNK)[:, None] - jnp.arange(2 * CHUNK)[None, :] + CHUNK
    head = jnp.arange(KV_HEADS)[:, None] * ATTN_GROUP + jnp.arange(rows_q)[None, :] // CHUNK + 1
    slope = jnp.exp2(-8.0 * head.astype(F32) / ATTN_HEADS)
    return jnp.where((dist >= 0) & (dist < CHUNK), -slope[:, :, None] * dist.astype(F32)[None], NEG)


def _swa_probs(q_kv, k_prev, k_cur, k_first, sink, bias, n):
    rows_q = ATTN_GROUP * CHUNK
    qs = jnp.concatenate([q_kv[:, g * DH:(g + 1) * DH] for g in range(ATTN_GROUP)], axis=0) * (DH ** -0.5)
    kcat = jnp.concatenate([k_prev, k_cur], axis=0)
    kmeta = k_first[META_PAD:, :]
    key_ok = lax.broadcasted_iota(jnp.int32, (1, 2 * CHUNK), 1) + n * CHUNK >= 2 * CHUNK
    s_band = jnp.where(key_ok, _dot(qs, kcat, ((1,), (1,))) + bias, NEG)
    q_pos = lax.broadcasted_iota(jnp.int32, (rows_q, N_META), 0) % CHUNK + n * CHUNK - META_PAD
    ok_m = lax.broadcasted_iota(jnp.int32, (rows_q, N_META), 1) <= q_pos
    s_meta = jnp.where(ok_m, _dot(qs, kmeta, ((1,), (1,))), NEG)
    m = jnp.maximum(jnp.maximum(jnp.max(s_band, axis=1, keepdims=True), jnp.max(s_meta, axis=1, keepdims=True)), sink)
    p_band, p_meta, p_sink = jnp.exp(s_band - m), jnp.exp(s_meta - m), jnp.exp(sink - m)
    inv = 1.0 / (jnp.sum(p_band, axis=1, keepdims=True) + jnp.sum(p_meta, axis=1, keepdims=True) + p_sink)
    return qs, kcat, kmeta, p_band * inv, p_meta * inv, p_sink * inv


def _swa_specs(nt, rev):
    def idx(n):
        return nt - 1 - n if rev else n
    width = ATTN_HEADS * DH
    o = pl.BlockSpec((CHUNK, width), lambda n: (idx(n), 0))
    q_proj = pl.BlockSpec((CHUNK, width), lambda n: (idx(n), OFF_Q // width))
    def kv(col0, chunk_of):
        return pl.BlockSpec((CHUNK, KV_W), lambda n: (chunk_of(idx(n)), col0 // KV_W))

    chunks = (lambda c: jnp.maximum(c - 1, 0)), (lambda c: c), (lambda c: 0)
    k_specs = [kv(OFF_K, f) for f in chunks]
    v_specs = [kv(OFF_V, f) for f in chunks]
    dkv = pl.BlockSpec((CHUNK, KV_W), lambda n: (idx(n), 0))
    sink = _full((KV_HEADS, ATTN_GROUP * CHUNK, 1))
    bias = _full((KV_HEADS, ATTN_GROUP * CHUNK, 2 * CHUNK))
    return o, q_proj, k_specs, v_specs, dkv, sink, bias, idx


def _swa_fwd(proj, sink_rows, bias):
    t_rows = proj.shape[0]
    nt = t_rows // CHUNK
    o_spec, q_spec, k_specs, v_specs, _, sink_spec, bias_spec, _ = _swa_specs(nt, False)
    kv_w = ATTN_GROUP * DH

    def body(q_ref, kp_ref, kc_ref, km_ref, vp_ref, vc_ref, vm_ref, sink_ref, bias_ref, o_ref):
        n = pl.program_id(0)
        for k in range(KV_HEADS):
            hd = slice(k * DH, (k + 1) * DH)
            _, _, _, p_band, p_meta, _ = _swa_probs(q_ref[:, k * kv_w:(k + 1) * kv_w], kp_ref[:, hd], kc_ref[:, hd],
                                                    km_ref[:, hd], sink_ref[k], bias_ref[k], n)
            vcat = jnp.concatenate([vp_ref[:, hd], vc_ref[:, hd]], axis=0)
            out = _dot(p_band, vcat, ((1,), (0,))) + _dot(p_meta, vm_ref[:, hd][META_PAD:, :], ((1,), (0,)))
            for g in range(ATTN_GROUP):
                o_ref[:, k * kv_w + g * DH:k * kv_w + (g + 1) * DH] = out[g * CHUNK:(g + 1) * CHUNK, :]

    return pl.pallas_call(
        body, grid=(nt,), in_specs=[q_spec] + k_specs + v_specs + [sink_spec, bias_spec],
        out_specs=o_spec, out_shape=_sds((t_rows, ATTN_HEADS * DH), F32),
        name="swa_fwd", compiler_params=_params(1))(proj, proj, proj, proj, proj, proj, proj, sink_rows, bias)


def _swa_bwd(proj, sink_rows, bias, out, dout):
    t_rows = proj.shape[0]
    nt = t_rows // CHUNK
    o_spec, q_spec, k_specs, v_specs, dkv_spec, sink_spec, bias_spec, idx = _swa_specs(nt, True)
    kv_w = ATTN_GROUP * DH

    def body(q_ref, kp_ref, kc_ref, km_ref, vp_ref, vc_ref, vm_ref, sink_ref, bias_ref, o_ref, do_ref,
             dq_ref, dk_ref, dv_ref, dsink_ref, carry_k, carry_v, meta_k, meta_v, dk_buf, dv_buf, dq_buf):
        step = pl.program_id(0)
        n = idx(step)

        @pl.when(step == 0)
        def _():
            carry_k[...] = jnp.zeros_like(carry_k)
            carry_v[...] = jnp.zeros_like(carry_v)
            meta_k[...] = jnp.zeros_like(meta_k)
            meta_v[...] = jnp.zeros_like(meta_v)
            dsink_ref[...] = jnp.zeros_like(dsink_ref)

        for k in range(KV_HEADS):
            cols = slice(k * kv_w, (k + 1) * kv_w)
            hd = slice(k * DH, (k + 1) * DH)
            qs, kcat, kmeta, p_band, p_meta, p_sink = _swa_probs(q_ref[:, cols], kp_ref[:, hd], kc_ref[:, hd],
                                                                 km_ref[:, hd], sink_ref[k], bias_ref[k], n)
            vcat = jnp.concatenate([vp_ref[:, hd], vc_ref[:, hd]], axis=0)
            vmeta = vm_ref[:, hd][META_PAD:, :]
            o, do = o_ref[:, cols], do_ref[:, cols]
            os_ = jnp.concatenate([o[:, g * DH:(g + 1) * DH] for g in range(ATTN_GROUP)], axis=0)
            dos = jnp.concatenate([do[:, g * DH:(g + 1) * DH] for g in range(ATTN_GROUP)], axis=0)
            delta = jnp.sum(dos * os_, axis=1, keepdims=True)
            ds_band = p_band * (_dot(dos, vcat, ((1,), (1,))) - delta)
            ds_meta = p_meta * (_dot(dos, vmeta, ((1,), (1,))) - delta)
            ds_sink = -p_sink * delta
            dqs = (_dot(ds_band, kcat, ((1,), (0,))) + _dot(ds_meta, kmeta, ((1,), (0,)))) * (DH ** -0.5)
            for g in range(ATTN_GROUP):
                dq_buf[:, k * kv_w + g * DH:k * kv_w + (g + 1) * DH] = dqs[g * CHUNK:(g + 1) * CHUNK, :]
                dsink_ref[k, g:g + 1, :] += jnp.sum(ds_sink[g * CHUNK:(g + 1) * CHUNK, :])
            dkcat = _dot(ds_band, qs, ((0,), (0,)))
            dvcat = _dot(p_band, dos, ((0,), (0,)))
            meta_k[:, hd] += _dot(ds_meta, qs, ((0,), (0,)))
            meta_v[:, hd] += _dot(p_meta, dos, ((0,), (0,)))
            dk_buf[:, hd] = dkcat[CHUNK:, :] + carry_k[:, hd]
            dv_buf[:, hd] = dvcat[CHUNK:, :] + carry_v[:, hd]
            carry_k[:, hd] = dkcat[:CHUNK, :]
            carry_v[:, hd] = dvcat[:CHUNK, :]

        @pl.when(n == 0)
        def _():
            dk_buf[META_PAD:, :] += meta_k[...]
            dv_buf[META_PAD:, :] += meta_v[...]

        dq_ref[...] = dq_buf[...].astype(BF16)
        dk_ref[...] = dk_buf[...].astype(BF16)
        dv_ref[...] = dv_buf[...].astype(BF16)

    return pl.pallas_call(
        body, grid=(nt,),
        in_specs=[q_spec] + k_specs + v_specs + [sink_spec, bias_spec, o_spec, o_spec],
        out_specs=[o_spec, dkv_spec, dkv_spec, _full((KV_HEADS, 8, 128))],
        out_shape=[_sds((t_rows, ATTN_HEADS * DH), BF16), _sds((t_rows, KV_W), BF16),
                   _sds((t_rows, KV_W), BF16), _sds((KV_HEADS, 8, 128), F32)],
        scratch_shapes=[pltpu.VMEM((CHUNK, KV_W), F32), pltpu.VMEM((CHUNK, KV_W), F32),
                        pltpu.VMEM((N_META, KV_W), F32), pltpu.VMEM((N_META, KV_W), F32),
                        pltpu.VMEM((CHUNK, KV_W), F32), pltpu.VMEM((CHUNK, KV_W), F32),
                        pltpu.VMEM((CHUNK, ATTN_HEADS * DH), F32)],
        name="swa_bwd", compiler_params=_params(1))(proj, proj, proj, proj, proj, proj, proj, sink_rows, bias, out, dout)


def _pack_w_in_t(w_in_t):
    w_dt = w_in_t[CUT_DT:CUT_Q].reshape(SSM_GROUPS, HEADS_PER_GROUP, D_MODEL)
    w_dt = jnp.pad(w_dt, ((0, 0), (0, 128 - HEADS_PER_GROUP), (0, 0))).reshape(SSM_GROUPS * 128, D_MODEL)
    return jnp.concatenate([w_in_t[CUT_Z:CUT_XBC], w_in_t[CUT_G:], w_in_t[CUT_XBC:CUT_DT], w_in_t[CUT_Q:CUT_K],
                            w_in_t[CUT_K:CUT_V], w_in_t[CUT_V:CUT_G], w_dt], axis=0)


def _unpack_w_in_t(wp_t):
    w_dt = wp_t[OFF_DT:].reshape(SSM_GROUPS, 128, D_MODEL)[:, :HEADS_PER_GROUP].reshape(SSM_HEADS, D_MODEL)
    return jnp.concatenate([wp_t[OFF_Z:OFF_GATE], wp_t[OFF_XBC:OFF_Q], w_dt, wp_t[OFF_Q:OFF_K], wp_t[OFF_K:OFF_V],
                            wp_t[OFF_V:OFF_DT], wp_t[OFF_GATE:OFF_XBC]], axis=0)


def _group_rows(v, width):
    return jnp.pad(v.reshape(SSM_GROUPS, 1, HEADS_PER_GROUP), ((0, 0), (0, 0), (0, width - HEADS_PER_GROUP)))


def _local_step(x, target, wt, late_weights=None, on_grad=None, started=None):
    seq = x.shape[0]
    grads = {}

    def emit(name, g):
        grads[name] = g
        return None if on_grad is None else on_grad(name, g)
    meta = wt["meta_tokens"]
    wp_t = _pack_w_in_t(wt["w_in_t"])
    dtb = _group_rows(wt["ssm_dt_bias"].reshape(-1), 128)
    alog = _group_rows(wt["ssm_a_log"].reshape(-1), 128)
    dskip_x = jnp.repeat(wt["ssm_d_skip"].reshape(-1), HEAD_P).reshape(SSM_GROUPS, 1, GROUP_W)
    sink_rows = jnp.repeat(wt["attn_sinks"].reshape(KV_HEADS, ATTN_GROUP), CHUNK, axis=1).reshape(KV_HEADS, ATTN_GROUP * CHUNK, 1)

    hn = _prenorm(x, meta, wt["norm_pre_mix"])
    proj = _matmul(hn, wp_t, tb=True, name="in_proj", after=started)
    xc, xact = _ssm_conv_fwd(proj, wt["ssm_conv_w"], wt["ssm_conv_b"])
    y, hst = _ssd_fwd(xact, proj, dtb, alog, dskip_x)
    yn = _ssm_post(y, proj, wt["ssm_norm"])
    if late_weights is not None:
        wt = {**wt, **late_weights(yn)}
    y_ssm = _matmul(yn, wt["w_ssm_out"], name="ssm_out")
    bias = _swa_bias()
    attn = _swa_fwd(proj, sink_rows, bias)
    y_attn = _matmul(attn, wt["w_attn_out"], name="attn_out")
    mixed = _mix_fwd(proj, y_ssm, y_attn)
    mix = _matmul(mixed, wt["w_mix_out"], name="mix_out")
    h1, hn2 = _postmix(x, meta, mix, wt["norm_post_mix"], wt["norm_pre_ffn"])
    up = _matmul(hn2, wt["w_ffn_up_t"], tb=True, name="ffn_up")
    u, act = _ffn_act(up, wt["ffn_conv_w"], wt["ffn_conv_b"])
    f = _matmul(act, wt["w_ffn_down"], name="ffn_down")
    df, dy, g_norm_post_ffn, loss_row = _final(h1, f, target, wt["norm_post_ffn"])

    grads["norm_post_ffn"] = g_norm_post_ffn
    sent = emit("w_ffn_down", _matmul(act, df, ta=True, out_dtype=BF16, name="dw_ffn_down"))
    dact = _matmul(df, wt["w_ffn_down"], tb=True, name="d_act", after=sent)
    dup, grads["ffn_conv_w"], grads["ffn_conv_b"] = _ffn_act_bwd(u, up, dact, wt["ffn_conv_w"])
    sent = emit("w_ffn_up_t", _matmul(dup, hn2, ta=True, out_dtype=BF16, name="dw_ffn_up"))
    dhn2 = _matmul(dup, wt["w_ffn_up_t"], name="d_hn2", after=sent)
    dmix, dh, grads["norm_pre_ffn"], grads["norm_post_mix"] = _postmix_bwd(h1, dhn2, dy, mix, wt["norm_pre_ffn"], wt["norm_post_mix"])
    sent = emit("w_mix_out", _matmul(mixed, dmix, ta=True, out_dtype=BF16, name="dw_mix_out"))
    dmixed = _matmul(dmix, wt["w_mix_out"], tb=True, name="d_mixed", after=sent)
    dy_ssm, dy_attn, dglog = _mix_bwd(dmixed, proj, y_ssm, y_attn)
    sent = emit("w_ssm_out", _matmul(yn, dy_ssm, ta=True, out_dtype=BF16, name="dw_ssm_out"))
    dyn = _matmul(dy_ssm, wt["w_ssm_out"], tb=True, name="d_yn", after=sent)
    sent = emit("w_attn_out", _matmul(attn, dy_attn, ta=True, out_dtype=BF16, name="dw_attn_out"))
    dattn = _matmul(dy_attn, wt["w_attn_out"], tb=True, name="d_attn", after=sent)
    dy_ssd, dz, grads["ssm_norm"] = _ssm_post_bwd(y, proj, dyn, wt["ssm_norm"])
    dxs, dbm, dcm, ddt, dalog, ddtb, dd_x = _ssd_bwd(xact, proj, dtb, alog, dskip_x, dy_ssd, hst)
    grads["ssm_a_log"] = dalog[:, 0, :HEADS_PER_GROUP].reshape(1, SSM_HEADS)
    grads["ssm_dt_bias"] = ddtb[:, 0, :HEADS_PER_GROUP].reshape(1, SSM_HEADS)
    grads["ssm_d_skip"] = dd_x.reshape(SSM_HEADS, HEAD_P).sum(axis=1).reshape(1, SSM_HEADS)
    dxbc, grads["ssm_conv_w"], grads["ssm_conv_b"] = _ssm_conv_bwd(xc, proj, dxs, dbm, dcm, wt["ssm_conv_w"])
    dq, dk, dv, dsink = _swa_bwd(proj, sink_rows, bias, attn, dattn)
    grads["attn_sinks"] = dsink[:, :ATTN_GROUP, 0].reshape(1, ATTN_HEADS)
    dproj = jnp.concatenate([dz, dglog, dxbc, dq, dk, dv, ddt], axis=1)
    sent = emit("w_in_t", _unpack_w_in_t(_matmul(dproj, hn, ta=True, out_dtype=BF16, name="dw_in")))
    dhn = _matmul(dproj, wp_t, name="d_hn", after=sent)
    grad_x, grads["meta_tokens"], grads["norm_pre_mix"] = _prenorm_bwd(x, meta, dhn, dh, wt["norm_pre_mix"])
    return loss_row[0, 0], grad_x, grads


def _all_gather(shards):
    n = len(shards)

    def body(*refs):
        ins, outs = refs[:n], refs[n:2 * n]
        send_sems, recv_sems, local_sems = refs[2 * n:]
        x, y, c = lax.axis_index("x"), lax.axis_index("y"), lax.axis_index("c")
        me, sibling = (x, y, c), (x, y, 1 - c)
        chips = [(1 - x, y), (x, 1 - y), (1 - x, 1 - y)]

        def slot(a, dev):
            return outs[a].at[4 * dev[0] + 2 * dev[1] + dev[2]]

        def copy(k, a, block, to, src=None):
            return pltpu.make_async_remote_copy(
                src_ref=slot(a, block) if src is None else src, dst_ref=slot(a, block),
                send_sem=send_sems.at[k, a], recv_sem=recv_sems.at[k, a],
                device_id=to, device_id_type=pl.DeviceIdType.MESH)

        mine = [pltpu.make_async_copy(ins[a], slot(a, me), local_sems.at[a]) for a in range(n)]
        for cp in mine:
            cp.start()
        first = [copy(0, a, me, sibling, src=ins[a]) for a in range(n)]
        for j, chip in enumerate(chips):
            first += [copy(1 + j, a, me, (*chip, c), src=ins[a]) for a in range(n)]
        for cp in first:
            cp.start()
        passed = []
        for j, chip in enumerate(chips):
            for a in range(n):
                copy(1 + j, a, (*chip, c), me).wait_recv()
                fwd = copy(4 + j, a, (*chip, c), sibling)
                fwd.start()
                passed.append(fwd)
        for a in range(n):
            copy(0, a, sibling, me).wait_recv()
        for j, chip in enumerate(chips):
            for a in range(n):
                copy(4 + j, a, (*chip, 1 - c), me).wait_recv()
        for cp in first + passed:
            cp.wait_send()
        for cp in mine:
            cp.wait()

    hbm = pl.BlockSpec(memory_space=pl.ANY)
    return pl.pallas_call(
        body, in_specs=[hbm] * n, out_specs=[hbm] * n,
        out_shape=[_sds((N_DEV,) + s.shape, s.dtype) for s in shards],
        scratch_shapes=[pltpu.SemaphoreType.DMA((7, n)), pltpu.SemaphoreType.DMA((7, n)), pltpu.SemaphoreType.DMA((n,))],
        name="gather_weights")(*shards)


def _peer_table():
    x, y, c = lax.axis_index("x"), lax.axis_index("y"), lax.axis_index("c")
    peers = []
    for k in range(N_DEV - 1):
        bits = k + 1
        p = (x ^ ((bits >> 2) & 1), y ^ ((bits >> 1) & 1), c ^ (bits & 1))
        peers.append((k, p, 4 * p[0] + 2 * p[1] + p[2]))
    return 4 * x + 2 * y + c, peers


_HBM = pl.BlockSpec(memory_space=pltpu.HBM)
_SEM = pl.BlockSpec(memory_space=pltpu.SEMAPHORE)
_EFFECT = pltpu.SideEffectType.DATAFLOW_SIDE_EFFECTING


def _push_copy(src, land, send_sems, recv_sems, a, k, p, src_slot, dst_slot):
    sem = a * (N_DEV - 1) + k
    return pltpu.make_async_remote_copy(
        src_ref=src[a] if src_slot is None else src[a].at[src_slot], dst_ref=land[a].at[dst_slot],
        send_sem=send_sems.at[sem], recv_sem=recv_sems.at[sem], device_id=p, device_id_type=pl.DeviceIdType.MESH)


def _push_start(srcs, scatter, name):
    n = len(srcs)
    lands = [lax.empty(s.shape if scatter else (N_DEV,) + s.shape, s.dtype) for s in srcs]

    def body(*refs):
        src, land = refs[:n], refs[n:2 * n]
        send_sems, recv_sems, token = refs[2 * n], refs[2 * n + 1], refs[-1]
        my_id, peers = _peer_table()
        for a in range(n):
            for k, p, p_id in peers:
                _push_copy(src, land, send_sems, recv_sems, a, k, p, p_id if scatter else None, my_id).start()
        token[...] = jnp.zeros_like(token)

    sems = pltpu.SemaphoreType.DMA(((N_DEV - 1) * n,))
    res = pl.pallas_call(
        body, name=name,
        out_shape=(sems, sems, *[pltpu.HBM(a.shape, a.dtype) for a in srcs + lands], _sds((8, 128), F32)),
        in_specs=[_HBM] * (2 * n), out_specs=(_SEM, _SEM, *[_HBM] * (2 * n), pl.BlockSpec(memory_space=pltpu.VMEM)),
        input_output_aliases={i: 2 + i for i in range(2 * n)},
        compiler_params=pltpu.CompilerParams(has_side_effects=_EFFECT),
    )(*[pltpu.with_memory_space_constraint(a, pltpu.HBM) for a in srcs + lands])
    return dict(send=res[0], recv=res[1], src=list(res[2:2 + n]), land=list(res[2 + n:2 + 2 * n]), token=res[-1],
                scatter=scatter)


def _push_wait(handle, after, name):
    n = len(handle["src"])
    scatter = handle["scatter"]

    def body(*refs):
        src, land = refs[:n], refs[n:2 * n]
        send_sems, recv_sems = refs[2 * n], refs[2 * n + 1]
        _, peers = _peer_table()
        for a in range(n):
            for k, p, p_id in peers:
                cp = _push_copy(src, land, send_sems, recv_sems, a, k, p, p_id if scatter else None, p_id)
                cp.wait_send()
                cp.wait_recv()

    arrays = handle["src"] + handle["land"]
    res = pl.pallas_call(
        body, name=name, out_shape=tuple(pltpu.HBM(a.shape, a.dtype) for a in arrays),
        in_specs=[_HBM] * (2 * n) + [_SEM, _SEM, pl.BlockSpec(memory_space=pl.ANY)], out_specs=tuple([_HBM] * (2 * n)),
        input_output_aliases={i: i for i in range(2 * n)},
        compiler_params=pltpu.CompilerParams(has_side_effects=_EFFECT),
    )(*arrays, handle["send"], handle["recv"], after)
    return list(res[:n]), list(res[n:])


def _slot_sum(p_ref, own_ref):
    if own_ref is not None:
        my_id = 4 * lax.axis_index("x") + 2 * lax.axis_index("y") + lax.axis_index("c")
        mine = own_ref[...].astype(F32)
    g = None
    for s in range(p_ref.shape[0]):
        term = p_ref[s].astype(F32)
        if own_ref is not None:
            term = jnp.where(my_id == s, mine, term)
        g = term if g is None else g + term
    return g


def _to_bf16(arrays):
    n = len(arrays)

    def body(*refs):
        for i in range(n):
            refs[n + i][...] = refs[i][...].astype(BF16)

    return pl.pallas_call(body, out_shape=[_sds(a.shape, BF16) for a in arrays], name="weights_to_bf16",
                          compiler_params=pltpu.CompilerParams(vmem_limit_bytes=VMEM_LIMIT))(*arrays)


def _adamw(parts, own, w, m, v, name):
    rows, cols = w.shape
    if rows % 16 == 0:
        tr, tc = _pick(rows, (256, 128, 176, 64, 32, 16)), cols
    else:
        tr, tc = rows, _pick(cols, (256, 128))

    def body(*refs):
        if own is None:
            p_ref, w_ref, m_ref, v_ref, g_ref, d_ref, nm_ref, nv_ref = refs
            own_ref = None
        else:
            p_ref, own_ref, w_ref, m_ref, v_ref, g_ref, d_ref, nm_ref, nv_ref = refs
        g = _slot_sum(p_ref, own_ref)
        m_new = ADAM_B1 * m_ref[...] + (1.0 - ADAM_B1) * g
        v_new = ADAM_B2 * v_ref[...] + (1.0 - ADAM_B2) * (g * g)
        m_hat = m_new / (1.0 - ADAM_B1 ** ADAM_STEP)
        v_hat = v_new / (1.0 - ADAM_B2 ** ADAM_STEP)
        g_ref[...] = g
        d_ref[...] = -ADAM_LR * (m_hat / (jnp.sqrt(v_hat) + ADAM_EPS) + ADAM_WD * w_ref[...])
        nm_ref[...] = m_new
        nv_ref[...] = v_new

    by_rows = tc == cols
    spec = pl.BlockSpec((tr, tc), (lambda i: (i, 0)) if by_rows else (lambda i: (0, i)))
    parts_spec = pl.BlockSpec((parts.shape[0], tr, tc), (lambda i: (0, i, 0)) if by_rows else (lambda i: (0, 0, i)))
    operands = (parts, w, m, v) if own is None else (parts, own, w, m, v)
    return pl.pallas_call(
        body, grid=(rows // tr if by_rows else cols // tc,),
        in_specs=[parts_spec] + [spec] * (len(operands) - 1),
        out_specs=[spec] * 4, out_shape=[_sds((rows, cols), F32)] * 4,
        name=name, compiler_params=_params(1))(*operands)


SMALL_REPLICATED = (("norm_pre_mix", 1024), ("ssm_conv_b", 3072), ("ssm_dt_bias", 32), ("ssm_a_log", 32),
                    ("ssm_d_skip", 32), ("ssm_norm", 2048), ("attn_sinks", 16), ("norm_post_mix", 1024),
                    ("norm_pre_ffn", 1024), ("ffn_conv_b", 5632), ("norm_post_ffn", 1024))
SMALL_SHARDED = (("meta_tokens", (N_META, D_MODEL // N_DEV)), ("ssm_conv_w", (SSM_CONV, CONV_DIM // N_DEV)),
                 ("ffn_conv_w", (FFN_CONV, 2 * FFN_DIM // N_DEV)))
BIG = (("w_in", (D_MODEL, N_IN // N_DEV), 1), ("w_ssm_out", (D_INNER // N_DEV, D_MODEL), 0),
       ("w_attn_out", (D_MODEL // N_DEV, D_MODEL), 0), ("w_mix_out", (D_MODEL // N_DEV, D_MODEL), 0),
       ("w_ffn_up", (D_MODEL, 2 * FFN_DIM // N_DEV), 1), ("w_ffn_down", (FFN_DIM // N_DEV, D_MODEL), 0))


def _rows_of(size):
    return -(-size // 128)


def _as_rows(flat):
    size = flat.shape[-1]
    rows = _rows_of(size)
    flat = jnp.pad(flat, [(0, 0)] * (flat.ndim - 1) + [(0, rows * 128 - size)])
    return flat.reshape(flat.shape[:-1] + (rows, 128))


def _pack_small(rep, sharded):
    pieces = [_as_rows(rep[name].reshape(-1)) for name, _ in SMALL_REPLICATED]
    pieces += [_as_rows(sharded[name].reshape(-1)) for name, _ in SMALL_SHARDED]
    packed = jnp.concatenate(pieces, axis=0)
    return jnp.pad(packed, ((0, -packed.shape[0] % 8), (0, 0)))


def _unpack_small(packed):
    out, row = {}, 0
    for name, size in SMALL_REPLICATED:
        out[name] = packed[row:row + _rows_of(size)].reshape(-1)[:size].reshape(1, size)
        row += _rows_of(size)
    for name, (r, c) in SMALL_SHARDED:
        out[name] = packed[row:row + _rows_of(r * c)].reshape(-1)[:r * c].reshape(r, c)
        row += _rows_of(r * c)
    return out


def _shard_major(g, shape, axis):
    r, c = shape
    if axis == 0:
        return g.reshape(N_DEV, r, c)
    return g.reshape(r, N_DEV, c).transpose(1, 0, 2)


def kernel(x, meta_tokens, norm_pre_mix, w_in, ssm_conv_w, ssm_conv_b, ssm_dt_bias, ssm_a_log, ssm_d_skip, ssm_norm, w_ssm_out, attn_sinks, w_attn_out, w_mix_out, norm_post_mix, norm_pre_ffn, w_ffn_up, ffn_conv_w, ffn_conv_b, w_ffn_down, norm_post_ffn, loss_target, m_meta_tokens, m_norm_pre_mix, m_w_in, m_ssm_conv_w, m_ssm_conv_b, m_ssm_dt_bias, m_ssm_a_log, m_ssm_d_skip, m_ssm_norm, m_w_ssm_out, m_attn_sinks, m_w_attn_out, m_w_mix_out, m_norm_post_mix, m_norm_pre_ffn, m_w_ffn_up, m_ffn_conv_w, m_ffn_conv_b, m_w_ffn_down, m_norm_post_ffn, v_meta_tokens, v_norm_pre_mix, v_w_in, v_ssm_conv_w, v_ssm_conv_b, v_ssm_dt_bias, v_ssm_a_log, v_ssm_d_skip, v_ssm_norm, v_w_ssm_out, v_attn_sinks, v_w_attn_out, v_w_mix_out, v_norm_post_mix, v_norm_pre_ffn, v_w_ffn_up, v_ffn_conv_w, v_ffn_conv_b, v_w_ffn_down, v_norm_post_ffn):
    names = ("meta_tokens", "norm_pre_mix", "w_in", "ssm_conv_w", "ssm_conv_b", "ssm_dt_bias", "ssm_a_log", "ssm_d_skip",
             "ssm_norm", "w_ssm_out", "attn_sinks", "w_attn_out", "w_mix_out", "norm_post_mix", "norm_pre_ffn", "w_ffn_up",
             "ffn_conv_w", "ffn_conv_b", "w_ffn_down", "norm_post_ffn")
    w_loc = dict(zip(names, (meta_tokens, norm_pre_mix, w_in, ssm_conv_w, ssm_conv_b, ssm_dt_bias, ssm_a_log, ssm_d_skip,
                             ssm_norm, w_ssm_out, attn_sinks, w_attn_out, w_mix_out, norm_post_mix, norm_pre_ffn, w_ffn_up,
                             ffn_conv_w, ffn_conv_b, w_ffn_down, norm_post_ffn)))
    m_loc = dict(zip(names, (m_meta_tokens, m_norm_pre_mix, m_w_in, m_ssm_conv_w, m_ssm_conv_b, m_ssm_dt_bias, m_ssm_a_log,
                             m_ssm_d_skip, m_ssm_norm, m_w_ssm_out, m_attn_sinks, m_w_attn_out, m_w_mix_out, m_norm_post_mix,
                             m_norm_pre_ffn, m_w_ffn_up, m_ffn_conv_w, m_ffn_conv_b, m_w_ffn_down, m_norm_post_ffn)))
    v_loc = dict(zip(names, (v_meta_tokens, v_norm_pre_mix, v_w_in, v_ssm_conv_w, v_ssm_conv_b, v_ssm_dt_bias, v_ssm_a_log,
                             v_ssm_d_skip, v_ssm_norm, v_w_ssm_out, v_attn_sinks, v_w_attn_out, v_w_mix_out, v_norm_post_mix,
                             v_norm_pre_ffn, v_w_ffn_up, v_ffn_conv_w, v_ffn_conv_b, v_w_ffn_down, v_norm_post_ffn)))

    def local2d(d, name):
        a = d[name]
        return a if name == "meta_tokens" else a.reshape(a.shape[1:])

    def turned2d(d, name):
        a = jnp.swapaxes(d[name], 1, 2)
        return a.reshape(a.shape[1:])

    my_id = 4 * lax.axis_index("x") + 2 * lax.axis_index("y") + lax.axis_index("c")
    big = {name: (shape, axis) for name, shape, axis in BIG}

    def whole(name, g):
        return g.reshape(N_DEV * g.shape[1], g.shape[2])

    def key(name):
        return name + "_t" if big[name][1] == 1 else name

    by_rows = [name for name, _, axis in BIG if axis == 0]
    send_bf16 = dict(zip(by_rows, _to_bf16([local2d(w_loc, name) for name in by_rows])))
    for name, _, axis in BIG:
        if axis == 1:
            send_bf16[name] = turned2d(w_loc, name).astype(BF16)
    small_shard_pack = jnp.concatenate([_as_rows(local2d(w_loc, name).reshape(-1)) for name, _ in SMALL_SHARDED], axis=0)
    small_shard_pack = jnp.pad(small_shard_pack, ((0, -small_shard_pack.shape[0] % 8), (0, 0)))
    first = _all_gather([send_bf16["w_in"], small_shard_pack])
    rest_names = [name for name, _, _ in BIG if name != "w_in"]
    rest = [send_bf16[name] for name in rest_names]
    rest, first = lax.optimization_barrier((rest, first))
    rest_handle = _push_start(rest, False, "gather_rest_start")
    wt = {"w_in_t": whole("w_in", first[0])}
    row = 0
    for name, (r, c) in SMALL_SHARDED:
        blocks = first[1][:, row:row + _rows_of(r * c)].reshape(N_DEV, -1)[:, :r * c].reshape(N_DEV, r, c)
        wt[name] = blocks.transpose(1, 0, 2).reshape(r, N_DEV * c)
        row += _rows_of(r * c)
    for name, size in SMALL_REPLICATED:
        wt[name] = w_loc[name].reshape(1, size)

    def late_weights(after):
        own, landed = _push_wait(rest_handle, after, "gather_rest_wait")
        out = {}
        for name, mine, land in zip(rest_names, own, landed):
            out[key(name)] = whole(name, lax.dynamic_update_index_in_dim(land, mine, my_id, 0))
        return out

    sent = {}

    def on_grad(known_as, g):
        name = known_as.removesuffix("_t")
        by_owner = g.reshape(N_DEV, g.shape[0] // N_DEV, g.shape[1])
        sent[name] = _push_start([by_owner], True, "send_" + name)
        return sent[name]["token"]

    loss_part, grad_x, grads = _local_step(x[0], loss_target[0], wt, late_weights, on_grad, rest_handle["token"])
    loss = lax.psum(loss_part, AXES)

    small_parts = []
    for name, (r, c) in SMALL_SHARDED:
        small_parts.append(_as_rows(_shard_major(grads[name], (r, c), 1).reshape(N_DEV, r * c)))
    rep_rows = jnp.concatenate([_as_rows(grads[name].reshape(-1)) for name, _ in SMALL_REPLICATED], axis=0)
    small_send = jnp.concatenate([jnp.broadcast_to(rep_rows[None], (N_DEV,) + rep_rows.shape)] + small_parts, axis=1)
    small_send = jnp.pad(small_send, ((0, 0), (0, -small_send.shape[1] % 8), (0, 0)))
    small_handle = _push_start([small_send], True, "send_small")

    def small_pack(d):
        return _pack_small({name: d[name] for name, _ in SMALL_REPLICATED}, {name: local2d(d, name) for name, _ in SMALL_SHARDED})

    def arrived(handle, after, name):
        src, landed = _push_wait(handle, after, "arrived_" + name)
        return landed[0], lax.dynamic_index_in_dim(src[0], my_id, 0, keepdims=False)

    grad_w, delta_w, new_m, new_v = {}, {}, {}, {}
    outs = None
    after = small_handle["token"]
    for name, handle in sent.items():
        if name == "w_in":
            parts, own = arrived(small_handle, after, "small")
            outs = _adamw(parts, own, small_pack(w_loc), small_pack(m_loc), small_pack(v_loc), "adamw_small")
            after = outs[0]
        parts, own = arrived(handle, after, name)
        turned = big[name][1] == 1
        state = [turned2d(d, name) if turned else local2d(d, name) for d in (w_loc, m_loc, v_loc)]
        results = _adamw(parts, own, *state, "adamw_" + name)
        after = results[0]
        full = (1,) + big[name][0]
        for dst, a in zip((grad_w, delta_w, new_m, new_v), results):
            dst[name] = jnp.swapaxes(a[None], 1, 2) if turned else a.reshape(full)
    for dst, packed in zip((grad_w, delta_w, new_m, new_v), outs):
        for name, a in _unpack_small(packed).items():
            dst[name] = a.reshape(w_loc[name].shape)

    return (loss, grad_x[None], *[grad_w[n] for n in names], *[delta_w[n] for n in names],
            *[new_m[n] for n in names], *[new_v[n] for n in names])
```

```python
import jax
import jax.numpy as jnp
from jax import lax
from jax.experimental import pallas as pl
from jax.experimental.pallas import tpu as pltpu

F32 = jnp.float32
BF16 = jnp.bfloat16
HIGHEST = lax.Precision.HIGHEST

D_MODEL = 1024
N_META = 16
CHUNK = 128
META_PAD = CHUNK - N_META
D_INNER = 2048
HEAD_P = 64
SSM_HEADS = 32
SSM_GROUPS = 4
HEADS_PER_GROUP = SSM_HEADS // SSM_GROUPS
GROUP_W = HEADS_PER_GROUP * HEAD_P
D_STATE = 128
SSM_CONV = 4
CONV_DIM = D_INNER + 2 * SSM_GROUPS * D_STATE
ATTN_HEADS = 16
KV_HEADS = 4
ATTN_GROUP = ATTN_HEADS // KV_HEADS
DH = 64
KV_W = KV_HEADS * DH
FFN_DIM = 2816
FFN_CONV = 3
EPS = 1e-6
NEG = -1e30
N_DEV = 8
AXES = ("x", "y", "c")

OFF_Z, OFF_GATE, OFF_XBC, OFF_Q, OFF_K, OFF_V, OFF_DT = 0, 2048, 4096, 7168, 8192, 8448, 8704
N_INP = OFF_DT + SSM_GROUPS * 128
CUT_Z, CUT_XBC, CUT_DT, CUT_Q, CUT_K, CUT_V, CUT_G = 0, 2048, 5120, 5152, 6176, 6432, 6688
N_IN = 8736

ADAM_LR, ADAM_B1, ADAM_B2, ADAM_EPS, ADAM_WD, ADAM_STEP = 0.001, 0.9, 0.999, 1e-08, 0.01, 10

VMEM_LIMIT = 56 * 1024 * 1024


def _params(n_grid):
    return pltpu.CompilerParams(dimension_semantics=("arbitrary",) * n_grid, vmem_limit_bytes=VMEM_LIMIT)


def _sds(shape, dtype):
    return jax.ShapeDtypeStruct(shape, dtype)


def _pick(n, prefs):
    for c in prefs:
        if n % c == 0:
            return c
    raise ValueError(f"no tile of {prefs} divides {n}")


def _row(tr, width, cb=0):
    return pl.BlockSpec((tr, width), lambda i: (i, cb))


def _row_rev(tr, width, nt, cb=0):
    return pl.BlockSpec((tr, width), lambda i: (nt - 1 - i, cb))


def _full(shape):
    return pl.BlockSpec(shape, lambda *_: (0,) * len(shape))


def _sigmoid(x):
    return 1.0 / (1.0 + jnp.exp(-x))


def _softplus(x):
    return jnp.maximum(x, 0.0) + jnp.log(1.0 + jnp.exp(-jnp.abs(x)))


def _rms(x):
    return lax.rsqrt(jnp.mean(x * x, axis=-1, keepdims=True) + EPS)


def _rms_bwd(x, r, w, dy):
    xh = x * r
    g = dy * w
    dx = r * (g - xh * jnp.mean(g * xh, axis=-1, keepdims=True))
    return dx, jnp.sum(dy * xh, axis=0, keepdims=True)


def _row_ids(shape, tile_index, tr):
    return tile_index * tr + lax.broadcasted_iota(jnp.int32, shape, 0)


def _shift_down(cur, prev, s):
    if s == 0:
        return cur
    row = lax.broadcasted_iota(jnp.int32, cur.shape, 0)
    return jnp.where(row < s, pltpu.roll(prev, s, 0), pltpu.roll(cur, s, 0))


def _shift_up(cur, nxt, s):
    if s == 0:
        return cur
    n = cur.shape[0]
    row = lax.broadcasted_iota(jnp.int32, cur.shape, 0)
    return jnp.where(row >= n - s, pltpu.roll(nxt, n - s, 0), pltpu.roll(cur, n - s, 0))


def _matmul(a, b, *, ta=False, tb=False, out_dtype=F32, name, after=None):
    if ta:
        k_dim, m_dim = a.shape
    else:
        m_dim, k_dim = a.shape
    n_dim = b.shape[0] if tb else b.shape[1]
    tm = _pick(m_dim, (1408, 1024, 768, 512, 384, 256, 128))
    tn = _pick(n_dim, (1024, 1408, 768, 512, 384, 256, 128))
    if ta:
        tk = _pick(k_dim, (1408, 1024, 768, 512, 384, 256, 128))
    else:
        tk = k_dim if k_dim <= 3072 else _pick(k_dim, (3072, 2816, 2048, 1024))
    nk = k_dim // tk
    dims = (((0 if ta else 1,), (1 if tb else 0,)), ((), ()))

    use_acc = nk > 1 and out_dtype != F32

    def body(a_ref, b_ref, *rest):
        o_ref = rest[-2] if use_acc else rest[-1]
        acc_ref = rest[-1] if use_acc else o_ref
        r = lax.dot_general(a_ref[...].astype(BF16), b_ref[...].astype(BF16), dims, preferred_element_type=F32)
        if nk == 1:
            o_ref[...] = r.astype(o_ref.dtype)
        else:
            k = pl.program_id(2)

            @pl.when(k == 0)
            def _():
                acc_ref[...] = r

            @pl.when(k > 0)
            def _():
                acc_ref[...] += r

            if use_acc:
                @pl.when(k == nk - 1)
                def _():
                    o_ref[...] = acc_ref[...].astype(o_ref.dtype)

    a_spec = pl.BlockSpec((tk, tm), lambda i, j, k: (k, i)) if ta else pl.BlockSpec((tm, tk), lambda i, j, k: (i, k))
    b_spec = pl.BlockSpec((tn, tk), lambda i, j, k: (j, k)) if tb else pl.BlockSpec((tk, tn), lambda i, j, k: (k, j))
    extra_specs, extra = ([], ()) if after is None else ([pl.BlockSpec(memory_space=pl.ANY)], (after,))
    return pl.pallas_call(
        body, grid=(m_dim // tm, n_dim // tn, nk), in_specs=[a_spec, b_spec] + extra_specs,
        out_specs=pl.BlockSpec((tm, tn), lambda i, j, k: (i, j)), out_shape=_sds((m_dim, n_dim), out_dtype),
        scratch_shapes=[pltpu.VMEM((tm, tn), F32)] if use_acc else [],
        name=name, compiler_params=_params(3))(a, b, *extra)


def _seq_specs():
    return [pl.BlockSpec((CHUNK, D_MODEL), lambda i: (jnp.maximum(i - 1, 0), 0)), _full((N_META, D_MODEL))]


def _seq_tile(x_ref, meta_ref, i):
    first = jnp.concatenate([jnp.zeros((META_PAD, D_MODEL), F32), meta_ref[...]], axis=0)
    return jnp.where(i == 0, first, x_ref[...])


def _prenorm(x, meta, w):
    t_rows = x.shape[0] + CHUNK

    def body(x_ref, meta_ref, w_ref, o_ref):
        h = _seq_tile(x_ref, meta_ref, pl.program_id(0))
        o_ref[...] = (h * _rms(h) * w_ref[...]).astype(BF16)

    return pl.pallas_call(body, grid=(t_rows // CHUNK,), in_specs=_seq_specs() + [_full((1, D_MODEL))],
                          out_specs=_row(CHUNK, D_MODEL), out_shape=_sds((t_rows, D_MODEL), BF16),
                          name="prenorm", compiler_params=_params(1))(x, meta, w)


def _xbc_specs(tr, rev_nt=None):
    cbs = [OFF_XBC // 1024 + j for j in range(CONV_DIM // 1024)]
    if rev_nt is None:
        return [_row(tr, 1024, cb) for cb in cbs]
    return [_row_rev(tr, 1024, rev_nt, cb) for cb in cbs]


def _ssm_conv_fwd(proj, conv_w, conv_b):
    t_rows = proj.shape[0]
    tr = CHUNK

    def body(x0, x1, x2, w_ref, b_ref, xc_ref, xa_ref, prev):
        @pl.when(pl.program_id(0) == 0)
        def _():
            prev[...] = jnp.zeros_like(prev)

        x = jnp.concatenate([x0[...], x1[...], x2[...]], axis=1).astype(F32)
        p = prev[...]
        acc = b_ref[...] + w_ref[SSM_CONV - 1:SSM_CONV, :] * x
        for s in range(1, SSM_CONV):
            acc = acc + w_ref[SSM_CONV - 1 - s:SSM_CONV - s, :] * _shift_down(x, p, s)
        prev[...] = x
        xc_ref[...] = acc
        xa_ref[...] = acc * _sigmoid(acc)

    return pl.pallas_call(
        body, grid=(t_rows // tr,),
        in_specs=_xbc_specs(tr) + [_full((SSM_CONV, CONV_DIM)), _full((1, CONV_DIM))],
        out_specs=[_row(tr, CONV_DIM), _row(tr, CONV_DIM)],
        out_shape=[_sds((t_rows, CONV_DIM), F32), _sds((t_rows, CONV_DIM), F32)],
        scratch_shapes=[pltpu.VMEM((tr, CONV_DIM), F32)],
        name="ssm_conv_fwd", compiler_params=_params(1))(proj, proj, proj, conv_w, conv_b)


def _ssm_post(y, proj, w):
    t_rows = y.shape[0]
    tr = CHUNK

    def body(y_ref, z_ref, w_ref, o_ref):
        z = z_ref[...].astype(F32)
        yz = y_ref[...] * z * _sigmoid(z)
        o_ref[...] = (yz * _rms(yz) * w_ref[...]).astype(BF16)

    return pl.pallas_call(body, grid=(t_rows // tr,),
                          in_specs=[_row(tr, D_INNER), _row(tr, D_INNER, OFF_Z // D_INNER), _full((1, D_INNER))],
                          out_specs=_row(tr, D_INNER), out_shape=_sds((t_rows, D_INNER), BF16),
                          name="ssm_post", compiler_params=_params(1))(y, proj, w)


def _mix_fwd(proj, y_ssm, y_attn):
    t_rows = y_ssm.shape[0]
    tr = _pick(t_rows, (384, 128))

    def body(g_ref, ys_ref, ya_ref, o_ref):
        g = _sigmoid(g_ref[...].astype(F32))
        o_ref[...] = (g[:, :D_MODEL] * ys_ref[...] + g[:, D_MODEL:] * ya_ref[...]).astype(BF16)

    return pl.pallas_call(body, grid=(t_rows // tr,),
                          in_specs=[_row(tr, 2 * D_MODEL, OFF_GATE // (2 * D_MODEL)), _row(tr, D_MODEL), _row(tr, D_MODEL)],
                          out_specs=_row(tr, D_MODEL), out_shape=_sds((t_rows, D_MODEL), BF16),
                          name="mix_fwd", compiler_params=_params(1))(proj, y_ssm, y_attn)


def _postmix(x, meta, mix, w_post, w_pre):
    t_rows = mix.shape[0]
    tr = CHUNK

    def body(x_ref, meta_ref, m_ref, wp_ref, wf_ref, h1_ref, hn_ref):
        m = m_ref[...]
        h1 = _seq_tile(x_ref, meta_ref, pl.program_id(0)) + m * _rms(m) * wp_ref[...]
        h1 = jnp.where(_row_ids(h1.shape, pl.program_id(0), tr) >= META_PAD, h1, 0.0)
        h1_ref[...] = h1
        hn_ref[...] = (h1 * _rms(h1) * wf_ref[...]).astype(BF16)

    return pl.pallas_call(body, grid=(t_rows // tr,),
                          in_specs=_seq_specs() + [_row(tr, D_MODEL), _full((1, D_MODEL)), _full((1, D_MODEL))],
                          out_specs=[_row(tr, D_MODEL), _row(tr, D_MODEL)],
                          out_shape=[_sds((t_rows, D_MODEL), F32), _sds((t_rows, D_MODEL), BF16)],
                          name="postmix", compiler_params=_params(1))(x, meta, mix, w_post, w_pre)


def _ffn_conv(x, before, w_ref, b_ref):
    u = b_ref[...] + w_ref[FFN_CONV - 1:FFN_CONV, :] * x
    for s in range(1, FFN_CONV):
        u = u + w_ref[FFN_CONV - 1 - s:FFN_CONV - s, :] * _shift_down(x, before, s)
    return u


def _ffn_act(up, conv_w, conv_b):
    t_rows = up.shape[0]
    tr = CHUNK
    width = 2 * FFN_DIM

    def body(up_ref, w_ref, b_ref, act_ref, prev):
        @pl.when(pl.program_id(0) == 0)
        def _():
            prev[...] = jnp.zeros_like(prev)

        x = up_ref[...].astype(F32)
        u = _ffn_conv(x, prev[...], w_ref, b_ref)
        prev[...] = x
        a = u[:, :FFN_DIM]
        act_ref[...] = (a * _sigmoid(a) * u[:, FFN_DIM:]).astype(BF16)

    return pl.pallas_call(
        body, grid=(t_rows // tr,), in_specs=[_row(tr, width), _full((FFN_CONV, width)), _full((1, width))],
        out_specs=_row(tr, FFN_DIM), out_shape=_sds((t_rows, FFN_DIM), BF16),
        scratch_shapes=[pltpu.VMEM((tr, width), F32)],
        name="ffn_act", compiler_params=_params(1))(up, conv_w, conv_b)


def _final(h1, f, target, w):
    t_rows = h1.shape[0]
    tr = CHUNK

    def body(h1_ref, f_ref, t_ref, w_ref, df_ref, dy_ref, dw_ref, loss_ref):
        i = pl.program_id(0)

        @pl.when(i == 0)
        def _():
            dw_ref[...] = jnp.zeros_like(dw_ref)
            loss_ref[...] = jnp.zeros_like(loss_ref)

        f_val = f_ref[...]
        r = _rms(f_val)
        wv = w_ref[...]
        h2 = h1_ref[...] + f_val * r * wv
        diff = jnp.where(i >= 1, h2 - t_ref[...], 0.0)
        loss_ref[...] += 0.5 * jnp.sum(diff * diff) * (1.0 / D_MODEL)
        dy = diff * (1.0 / D_MODEL)
        dy_ref[...] = dy
        df, dw = _rms_bwd(f_val, r, wv, dy)
        df_ref[...] = df.astype(BF16)
        dw_ref[...] += dw

    tgt_spec = pl.BlockSpec((tr, D_MODEL), lambda i: (jnp.maximum(i - 1, 0), 0))
    return pl.pallas_call(
        body, grid=(t_rows // tr,),
        in_specs=[_row(tr, D_MODEL), _row(tr, D_MODEL), tgt_spec, _full((1, D_MODEL))],
        out_specs=[_row(tr, D_MODEL), _row(tr, D_MODEL), _full((1, D_MODEL)), _full((1, 128))],
        out_shape=[_sds((t_rows, D_MODEL), BF16), _sds((t_rows, D_MODEL), F32), _sds((1, D_MODEL), F32), _sds((1, 128), F32)],
        name="final", compiler_params=_params(1))(h1, f, target, w)


def _ffn_act_bwd(up, dact, conv_w, conv_b):
    t_rows = up.shape[0]
    tr = CHUNK
    nt = t_rows // tr
    width = 2 * FFN_DIM

    def body(up_ref, before_ref, da_ref, w_ref, b_ref, dup_ref, dw_ref, db_ref, nxt):
        @pl.when(pl.program_id(0) == 0)
        def _():
            nxt[...] = jnp.zeros_like(nxt)
            dw_ref[...] = jnp.zeros_like(dw_ref)
            db_ref[...] = jnp.zeros_like(db_ref)

        x = up_ref[...].astype(F32)
        before = jnp.where(pl.program_id(0) == nt - 1, 0.0, before_ref[...].astype(F32))
        u_val = _ffn_conv(x, before, w_ref, b_ref)
        a, g = u_val[:, :FFN_DIM], u_val[:, FFN_DIM:]
        d = da_ref[...].astype(F32)
        s = _sigmoid(a)
        du = jnp.concatenate([d * g * s * (1.0 + a * (1.0 - s)), d * a * s], axis=1)
        n = nxt[...]
        dup = jnp.zeros_like(du)
        for sh in range(FFN_CONV):
            k = FFN_CONV - 1 - sh
            moved = _shift_up(du, n, sh)
            dup = dup + w_ref[k:k + 1, :] * moved
            dw_ref[k:k + 1, :] += jnp.sum(moved * x, axis=0, keepdims=True)
        db_ref[...] += jnp.sum(du, axis=0, keepdims=True)
        nxt[...] = du
        dup_ref[...] = dup.astype(BF16)

    return pl.pallas_call(
        body, grid=(nt,),
        in_specs=[_row_rev(tr, width, nt), pl.BlockSpec((tr, width), lambda i: (jnp.maximum(nt - 2 - i, 0), 0)),
                  _row_rev(tr, FFN_DIM, nt), _full((FFN_CONV, width)), _full((1, width))],
        out_specs=[_row_rev(tr, width, nt), _full((FFN_CONV, width)), _full((1, width))],
        out_shape=[_sds((t_rows, width), BF16), _sds((FFN_CONV, width), F32), _sds((1, width), F32)],
        scratch_shapes=[pltpu.VMEM((tr, width), F32)],
        name="ffn_act_bwd", compiler_params=_params(1))(up, up, dact, conv_w, conv_b)


def _postmix_bwd(h1, dhn2, dy, mix, w_pre, w_post):
    t_rows = h1.shape[0]
    tr = CHUNK

    def body(h1_ref, dhn_ref, dy_ref, m_ref, wf_ref, wp_ref, dmix_ref, dh_ref, dwf_ref, dwp_ref):
        @pl.when(pl.program_id(0) == 0)
        def _():
            dwf_ref[...] = jnp.zeros_like(dwf_ref)
            dwp_ref[...] = jnp.zeros_like(dwp_ref)

        h1v = h1_ref[...]
        dx, dwf = _rms_bwd(h1v, _rms(h1v), wf_ref[...], dhn_ref[...])
        dwf_ref[...] += dwf
        dh1 = dy_ref[...] + dx
        dh1 = jnp.where(_row_ids(dh1.shape, pl.program_id(0), tr) >= META_PAD, dh1, 0.0)
        dh_ref[...] = dh1
        m = m_ref[...]
        dmix, dwp = _rms_bwd(m, _rms(m), wp_ref[...], dh1)
        dwp_ref[...] += dwp
        dmix_ref[...] = dmix.astype(BF16)

    return pl.pallas_call(
        body, grid=(t_rows // tr,),
        in_specs=[_row(tr, D_MODEL)] * 4 + [_full((1, D_MODEL))] * 2,
        out_specs=[_row(tr, D_MODEL), _row(tr, D_MODEL), _full((1, D_MODEL)), _full((1, D_MODEL))],
        out_shape=[_sds((t_rows, D_MODEL), BF16), _sds((t_rows, D_MODEL), F32), _sds((1, D_MODEL), F32), _sds((1, D_MODEL), F32)],
        name="postmix_bwd", compiler_params=_params(1))(h1, dhn2, dy, mix, w_pre, w_post)


def _mix_bwd(dmixed, proj, y_ssm, y_attn):
    t_rows = dmixed.shape[0]
    tr = _pick(t_rows, (384, 128))

    def body(d_ref, g_ref, ys_ref, ya_ref, dys_ref, dya_ref, dg_ref):
        d = d_ref[...]
        g = _sigmoid(g_ref[...].astype(F32))
        g1, g2 = g[:, :D_MODEL], g[:, D_MODEL:]
        dys_ref[...] = (d * g1).astype(BF16)
        dya_ref[...] = (d * g2).astype(BF16)
        dg_ref[...] = jnp.concatenate([d * ys_ref[...] * g1 * (1.0 - g1), d * ya_ref[...] * g2 * (1.0 - g2)],
                                      axis=1).astype(BF16)

    return pl.pallas_call(
        body, grid=(t_rows // tr,),
        in_specs=[_row(tr, D_MODEL), _row(tr, 2 * D_MODEL, OFF_GATE // (2 * D_MODEL)), _row(tr, D_MODEL), _row(tr, D_MODEL)],
        out_specs=[_row(tr, D_MODEL), _row(tr, D_MODEL), _row(tr, 2 * D_MODEL)],
        out_shape=[_sds((t_rows, D_MODEL), BF16), _sds((t_rows, D_MODEL), BF16), _sds((t_rows, 2 * D_MODEL), BF16)],
        name="mix_bwd", compiler_params=_params(1))(dmixed, proj, y_ssm, y_attn)


def _ssm_post_bwd(y, proj, dyn, w):
    t_rows = y.shape[0]
    tr = CHUNK

    def body(y_ref, z_ref, d_ref, w_ref, dy_ref, dz_ref, dw_ref):
        @pl.when(pl.program_id(0) == 0)
        def _():
            dw_ref[...] = jnp.zeros_like(dw_ref)

        yv, z = y_ref[...], z_ref[...].astype(F32)
        sz = _sigmoid(z)
        silu = z * sz
        yz = yv * silu
        dyz, dw = _rms_bwd(yz, _rms(yz), w_ref[...], d_ref[...].astype(F32))
        dw_ref[...] += dw
        dy_ref[...] = dyz * silu
        dz_ref[...] = (dyz * yv * sz * (1.0 + z * (1.0 - sz))).astype(BF16)

    return pl.pallas_call(
        body, grid=(t_rows // tr,),
        in_specs=[_row(tr, D_INNER), _row(tr, D_INNER, OFF_Z // D_INNER), _row(tr, D_INNER), _full((1, D_INNER))],
        out_specs=[_row(tr, D_INNER), _row(tr, D_INNER), _full((1, D_INNER))],
        out_shape=[_sds((t_rows, D_INNER), F32), _sds((t_rows, D_INNER), BF16), _sds((1, D_INNER), F32)],
        name="ssm_post_bwd", compiler_params=_params(1))(y, proj, dyn, w)


def _ssm_conv_bwd(xc, proj, dxs, dbm, dcm, conv_w):
    t_rows = xc.shape[0]
    tr = CHUNK
    nt = t_rows // tr
    bc_w = SSM_GROUPS * D_STATE

    def body(xc_ref, x0, x1, x2, dxs_ref, db_ref, dc_ref, w_ref, dx_ref, dw_ref, dbias_ref, nxt):
        @pl.when(pl.program_id(0) == 0)
        def _():
            nxt[...] = jnp.zeros_like(nxt)
            dw_ref[...] = jnp.zeros_like(dw_ref)
            dbias_ref[...] = jnp.zeros_like(dbias_ref)

        c = xc_ref[...]
        s = _sigmoid(c)
        dact = jnp.concatenate([dxs_ref[...], db_ref[...], dc_ref[...]], axis=1)
        dpre = dact * s * (1.0 + c * (1.0 - s))
        x = jnp.concatenate([x0[...], x1[...], x2[...]], axis=1).astype(F32)
        n = nxt[...]
        dx = jnp.zeros_like(dpre)
        for sh in range(SSM_CONV):
            k = SSM_CONV - 1 - sh
            moved = _shift_up(dpre, n, sh)
            dx = dx + w_ref[k:k + 1, :] * moved
            dw_ref[k:k + 1, :] += jnp.sum(moved * x, axis=0, keepdims=True)
        dbias_ref[...] += jnp.sum(dpre, axis=0, keepdims=True)
        nxt[...] = dpre
        dx_ref[...] = dx.astype(BF16)

    return pl.pallas_call(
        body, grid=(nt,),
        in_specs=[_row_rev(tr, CONV_DIM, nt)] + _xbc_specs(tr, nt)
        + [_row_rev(tr, D_INNER, nt), _row_rev(tr, bc_w, nt), _row_rev(tr, bc_w, nt), _full((SSM_CONV, CONV_DIM))],
        out_specs=[_row_rev(tr, CONV_DIM, nt), _full((SSM_CONV, CONV_DIM)), _full((1, CONV_DIM))],
        out_shape=[_sds((t_rows, CONV_DIM), BF16), _sds((SSM_CONV, CONV_DIM), F32), _sds((1, CONV_DIM), F32)],
        scratch_shapes=[pltpu.VMEM((tr, CONV_DIM), F32)],
        name="ssm_conv_bwd", compiler_params=_params(1))(xc, proj, proj, proj, dxs, dbm, dcm, conv_w)


def _prenorm_bwd(x, meta, dhn, dh, w):
    t_rows = dhn.shape[0]
    tr = CHUNK

    def body(x_ref, meta_ref, d_ref, r_ref, w_ref, dx_ref, dmeta_ref, dw_ref):
        i = pl.program_id(0)

        @pl.when(i == 0)
        def _():
            dw_ref[...] = jnp.zeros_like(dw_ref)

        h = _seq_tile(x_ref, meta_ref, i)
        dx, dw = _rms_bwd(h, _rms(h), w_ref[...], d_ref[...])
        dw_ref[...] += dw
        dh_tile = r_ref[...] + dx
        dx_ref[...] = dh_tile

        @pl.when(i == 0)
        def _():
            dmeta_ref[...] = dh_tile[META_PAD:, :]

    return pl.pallas_call(
        body, grid=(t_rows // tr,), in_specs=_seq_specs() + [_row(tr, D_MODEL)] * 2 + [_full((1, D_MODEL))],
        out_specs=[pl.BlockSpec((tr, D_MODEL), lambda i: (jnp.maximum(i - 1, 0), 0)), _full((N_META, D_MODEL)),
                   _full((1, D_MODEL))],
        out_shape=[_sds((t_rows - tr, D_MODEL), F32), _sds((N_META, D_MODEL), F32), _sds((1, D_MODEL), F32)],
        name="prenorm_bwd", compiler_params=_params(1))(x, meta, dhn, dh, w)


def _dot01(x, m01, x_left, parts):
    acc, rest = None, x
    for i in range(parts):
        piece = rest.astype(BF16)
        term = (jnp.dot(piece, m01, preferred_element_type=F32) if x_left
                else jnp.dot(m01, piece, preferred_element_type=F32))
        acc = term if acc is None else acc + term
        if i + 1 < parts:
            rest = rest - piece.astype(F32)
    return acc


def _ssd_common(dt_raw, dt_bias, a_log, chunk_index):
    rows = lax.broadcasted_iota(jnp.int32, (CHUNK, CHUNK), 0)
    cols = lax.broadcasted_iota(jnp.int32, (CHUNK, CHUNK), 1)
    low = rows >= cols
    raw = dt_raw + dt_bias
    live = _row_ids(raw.shape, chunk_index, CHUNK) >= META_PAD
    dt = jnp.where(live, _softplus(raw), 0.0)
    a_head = -jnp.exp(a_log)
    cs = _dot01(dt * a_head, low.astype(BF16), False, 3)
    grow = jnp.exp(cs)
    fade = jnp.exp(cs[CHUNK - 1:CHUNK, :] - cs)
    expand = (lax.broadcasted_iota(jnp.int32, (CHUNK, GROUP_W), 1) // HEAD_P
              == lax.broadcasted_iota(jnp.int32, (CHUNK, GROUP_W), 0)).astype(BF16)
    fold = (lax.broadcasted_iota(jnp.int32, (GROUP_W, CHUNK), 0) // HEAD_P
            == lax.broadcasted_iota(jnp.int32, (GROUP_W, CHUNK), 1)).astype(BF16)
    return dict(low=low, triu=(rows <= cols).astype(BF16), raw=raw, live=live, dt=dt, a_head=a_head, cs=cs, cs_t=cs.T,
                fold=fold, dtx=_dot01(dt, expand, True, 2), growx=_dot01(grow, expand, True, 2),
                fadex=_dot01(fade, expand, True, 2))


def _decay_matrix(cm, j):
    diff = cm["cs"][:, j:j + 1] - cm["cs_t"][j:j + 1, :]
    return jnp.where(cm["low"], jnp.exp(jnp.where(cm["low"], diff, 0.0)), 0.0)


def _dot(a, b, dims):
    return lax.dot_general(a.astype(BF16), b.astype(BF16), (dims, ((), ())), preferred_element_type=F32)


def _dot_fine(a, b, dims):
    a_hi, b_hi = a.astype(BF16), b.astype(BF16)
    a_lo, b_lo = (a - a_hi.astype(F32)).astype(BF16), (b - b_hi.astype(F32)).astype(BF16)
    dn = (dims, ((), ()))
    return (lax.dot_general(a_hi, b_hi, dn, preferred_element_type=F32)
            + lax.dot_general(a_hi, b_lo, dn, preferred_element_type=F32)
            + lax.dot_general(a_lo, b_hi, dn, preferred_element_type=F32))


def _ssd_specs(nt, rev):
    def idx(c):
        return nt - 1 - c if rev else c
    bc_w = SSM_GROUPS * D_STATE
    xs = pl.BlockSpec((CHUNK, D_INNER), lambda c: (idx(c), 0))
    bm = pl.BlockSpec((CHUNK, bc_w), lambda c: (idx(c), D_INNER // bc_w))
    cm = pl.BlockSpec((CHUNK, bc_w), lambda c: (idx(c), D_INNER // bc_w + 1))
    dtr = pl.BlockSpec((CHUNK, SSM_GROUPS * 128), lambda c: (idx(c), 0))
    par = _full((SSM_GROUPS, 1, 128))
    par_x = _full((SSM_GROUPS, 1, GROUP_W))
    return xs, bm, cm, dtr, par, par_x, idx


def _group_cols(g, width):
    return slice(g * width, (g + 1) * width)


def _ssd_fwd(xact, dt_raw, dtb, alog, dskip_x):
    t_rows = xact.shape[0]
    nt = t_rows // CHUNK
    xs_spec, b_spec, c_spec, dtr_spec, par, par_x, _ = _ssd_specs(nt, False)

    def body(xs_ref, b_ref, c_ref, dtr_ref, dtb_ref, alog_ref, dsk_ref, y_ref, hst_ref, state):
        c = pl.program_id(0)

        @pl.when(c == 0)
        def _():
            state[...] = jnp.zeros_like(state)

        for g in range(SSM_GROUPS):
            wide, narrow = _group_cols(g, GROUP_W), _group_cols(g, D_STATE)
            cm = _ssd_common(dtr_ref[:, narrow], dtb_ref[g], alog_ref[g], c)
            xs, bm, cmat = xs_ref[:, wide], b_ref[:, narrow], c_ref[:, narrow]
            x_dt = xs * cm["dtx"]
            h_in = state[g]
            hst_ref[0, g] = h_in
            y_ref[:, wide] = _dot(cmat, h_in, ((1,), (0,))) * cm["growx"] + xs * dsk_ref[g]
            cb = _dot(cmat, bm, ((1,), (1,)))
            for j in range(HEADS_PER_GROUP):
                sl = slice(g * GROUP_W + j * HEAD_P, g * GROUP_W + (j + 1) * HEAD_P)
                y_ref[:, sl] += _dot(cb * _decay_matrix(cm, j), x_dt[:, j * HEAD_P:(j + 1) * HEAD_P], ((1,), (0,)))
            state[g] = h_in * cm["growx"][CHUNK - 1:CHUNK, :] + _dot_fine(bm, x_dt * cm["fadex"], ((0,), (0,)))

    return pl.pallas_call(
        body, grid=(nt,),
        in_specs=[xs_spec, b_spec, c_spec, dtr_spec, par, par, par_x],
        out_specs=[xs_spec, pl.BlockSpec((1, SSM_GROUPS, D_STATE, GROUP_W), lambda c: (c, 0, 0, 0))],
        out_shape=[_sds((t_rows, D_INNER), F32), _sds((nt, SSM_GROUPS, D_STATE, GROUP_W), F32)],
        scratch_shapes=[pltpu.VMEM((SSM_GROUPS, D_STATE, GROUP_W), F32)],
        name="ssd_fwd", compiler_params=_params(1))(xact, xact, xact, dt_raw, dtb, alog, dskip_x)


def _ssd_bwd(xact, dt_raw, dtb, alog, dskip_x, dy, hst):
    t_rows = xact.shape[0]
    nt = t_rows // CHUNK
    xs_spec, b_spec, c_spec, dtr_spec, par, par_x, idx = _ssd_specs(nt, True)
    h_spec = pl.BlockSpec((1, SSM_GROUPS, D_STATE, GROUP_W), lambda c: (idx(c), 0, 0, 0))
    hn_spec = pl.BlockSpec((1, SSM_GROUPS, D_STATE, GROUP_W), lambda c: (jnp.minimum(idx(c) + 1, nt - 1), 0, 0, 0))
    bc_out = pl.BlockSpec((CHUNK, SSM_GROUPS * D_STATE), lambda c: (idx(c), 0))

    def body(xs_ref, b_ref, c_ref, dtr_ref, dtb_ref, alog_ref, dsk_ref, dy_ref, h_ref, hn_ref,
             dxs_ref, db_ref, dc_ref, ddt_ref, dalog_ref, ddtb_ref, dd_ref, dstate, dx_buf):
        step = pl.program_id(0)

        @pl.when(step == 0)
        def _():
            dstate[...] = jnp.zeros_like(dstate)
            dalog_ref[...] = jnp.zeros_like(dalog_ref)
            ddtb_ref[...] = jnp.zeros_like(ddtb_ref)
            dd_ref[...] = jnp.zeros_like(dd_ref)

        for g in range(SSM_GROUPS):
            _ssd_bwd_group(g, idx(step), xs_ref, b_ref, c_ref, dtr_ref, dtb_ref, alog_ref, dsk_ref, dy_ref, h_ref, hn_ref,
                           dxs_ref, db_ref, dc_ref, ddt_ref, dalog_ref, ddtb_ref, dd_ref, dstate, dx_buf)

    return pl.pallas_call(
        body, grid=(nt,),
        in_specs=[xs_spec, b_spec, c_spec, dtr_spec, par, par, par_x, xs_spec, h_spec, hn_spec],
        out_specs=[xs_spec, bc_out, bc_out, bc_out, par, par, par_x],
        out_shape=[_sds((t_rows, D_INNER), F32), _sds((t_rows, SSM_GROUPS * D_STATE), F32),
                   _sds((t_rows, SSM_GROUPS * D_STATE), F32), _sds((t_rows, SSM_GROUPS * 128), BF16),
                   _sds((SSM_GROUPS, 1, 128), F32), _sds((SSM_GROUPS, 1, 128), F32), _sds((SSM_GROUPS, 1, GROUP_W), F32)],
        scratch_shapes=[pltpu.VMEM((SSM_GROUPS, D_STATE, GROUP_W), F32), pltpu.VMEM((CHUNK, GROUP_W), F32)],
        name="ssd_bwd", compiler_params=_params(1))(xact, xact, xact, dt_raw, dtb, alog, dskip_x, dy, hst, hst)


def _ssd_bwd_group(g, chunk, xs_ref, b_ref, c_ref, dtr_ref, dtb_ref, alog_ref, dsk_ref, dy_ref, h_ref, hn_ref,
                   dxs_ref, db_ref, dc_ref, ddt_ref, dalog_ref, ddtb_ref, dd_ref, dstate, dx_buf):
    if True:
        wide, narrow = _group_cols(g, GROUP_W), _group_cols(g, D_STATE)
        cm = _ssd_common(dtr_ref[:, narrow], dtb_ref[g], alog_ref[g], chunk)
        xs, bm, cmat = xs_ref[:, wide], b_ref[:, narrow], c_ref[:, narrow]
        dsk = dsk_ref[g]
        x_dt = xs * cm["dtx"]
        h_in, h_next = h_ref[0, g], hn_ref[0, g]
        dyv = dy_ref[:, wide]
        dh = dstate[g]
        grow, fade = cm["growx"], cm["fadex"]
        dy_grow = dyv * grow
        x_fade = x_dt * fade
        cb = _dot(cmat, bm, ((1,), (1,)))
        ml = jnp.zeros((CHUNK, CHUNK), F32)
        row_id = lax.broadcasted_iota(jnp.int32, (CHUNK, CHUNK), 0)
        col_id = lax.broadcasted_iota(jnp.int32, (CHUNK, CHUNK), 1)
        w_rows = jnp.zeros((CHUNK, CHUNK), F32)
        w_cols = jnp.zeros((CHUNK, CHUNK), F32)
        for j in range(HEADS_PER_GROUP):
            sl = slice(j * HEAD_P, (j + 1) * HEAD_P)
            lm = _decay_matrix(cm, j)
            mlj = _dot(dyv[:, sl], x_dt[:, sl], ((1,), (1,))) * lm
            ml = ml + mlj
            wm = mlj * cb
            w_rows = jnp.where(col_id == j, jnp.sum(wm, axis=1, keepdims=True), w_rows)
            w_cols = jnp.where(row_id == j, jnp.sum(wm, axis=0, keepdims=True), w_cols)
            dx_buf[:, sl] = _dot(cb * lm, dyv[:, sl], ((0,), (0,)))
        dx_off = fade * _dot_fine(bm, dh, ((1,), (0,)))
        dx = dx_buf[...] + dx_off
        dc_ref[:, narrow] = _dot(ml, bm, ((1,), (0,))) + _dot(dy_grow, h_in, ((1,), (1,)))
        db_ref[:, narrow] = _dot(ml, cmat, ((0,), (0,))) + _dot(x_fade, dh, ((1,), (1,)))
        fold = cm["fold"]
        y_off = _dot_fine(cmat, h_in, ((1,), (0,))) * grow
        dcs = (w_rows - w_cols.T) + _dot01(dyv * y_off - x_dt * dx_off, fold, True, 2)
        tail = jnp.broadcast_to(jnp.sum(dh * h_next, axis=0, keepdims=True), (8, GROUP_W))
        tail = _dot01(tail, fold, True, 2)[0:1, :]
        last_row = lax.broadcasted_iota(jnp.int32, (CHUNK, 128), 0) == CHUNK - 1
        dcs = dcs + jnp.where(last_row, tail, 0.0)
        da = _dot01(dcs, cm["triu"], False, 3)
        ddt = da * cm["a_head"] + _dot01(dx * xs, fold, True, 2)
        ddt_raw = jnp.where(cm["live"], ddt * _sigmoid(cm["raw"]), 0.0)
        ddt_ref[:, narrow] = ddt_raw.astype(BF16)
        ddtb_ref[g] += jnp.sum(ddt_raw, axis=0, keepdims=True)
        dalog_ref[g] += jnp.sum(da * cm["dt"], axis=0, keepdims=True) * cm["a_head"]
        dd_ref[g] += jnp.sum(dyv * xs, axis=0, keepdims=True)
        dxs_ref[:, wide] = dx * cm["dtx"] + dyv * dsk
        dstate[g] = dh * grow[CHUNK - 1:CHUNK, :] + _dot_fine(cmat, dy_grow, ((0,), (0,)))


def _swa_bias():
    rows_q = ATTN_GROUP * CHUNK
    dist = (jnp.arange(rows_q) % CHUNK)[:, None] - jnp.arange(2 * CHUNK)[None, :] + CHUNK
    head = jnp.arange(KV_HEADS)[:, None] * ATTN_GROUP + jnp.arange(rows_q)[None, :] // CHUNK + 1
    slope = jnp.exp2(-8.0 * head.astype(F32) / ATTN_HEADS)
    return jnp.where((dist >= 0) & (dist < CHUNK), -slope[:, :, None] * dist.astype(F32)[None], NEG)


def _swa_probs(q_kv, k_prev, k_cur, k_first, sink, bias, n):
    rows_q = ATTN_GROUP * CHUNK
    qs = jnp.concatenate([q_kv[:, g * DH:(g + 1) * DH] for g in range(ATTN_GROUP)], axis=0) * (DH ** -0.5)
    kcat = jnp.concatenate([k_prev, k_cur], axis=0)
    kmeta = k_first[META_PAD:, :]
    key_ok = lax.broadcasted_iota(jnp.int32, (1, 2 * CHUNK), 1) + n * CHUNK >= 2 * CHUNK
    s_band = jnp.where(key_ok, _dot(qs, kcat, ((1,), (1,))) + bias, NEG)
    q_pos = lax.broadcasted_iota(jnp.int32, (rows_q, N_META), 0) % CHUNK + n * CHUNK - META_PAD
    ok_m = lax.broadcasted_iota(jnp.int32, (rows_q, N_META), 1) <= q_pos
    s_meta = jnp.where(ok_m, _dot(qs, kmeta, ((1,), (1,))), NEG)
    m = jnp.maximum(jnp.maximum(jnp.max(s_band, axis=1, keepdims=True), jnp.max(s_meta, axis=1, keepdims=True)), sink)
    p_band, p_meta, p_sink = jnp.exp(s_band - m), jnp.exp(s_meta - m), jnp.exp(sink - m)
    inv = 1.0 / (jnp.sum(p_band, axis=1, keepdims=True) + jnp.sum(p_meta, axis=1, keepdims=True) + p_sink)
    return qs, kcat, kmeta, p_band * inv, p_meta * inv, p_sink * inv


def _swa_specs(nt, rev):
    def idx(n):
        return nt - 1 - n if rev else n
    width = ATTN_HEADS * DH
    o = pl.BlockSpec((CHUNK, width), lambda n: (idx(n), 0))
    q_proj = pl.BlockSpec((CHUNK, width), lambda n: (idx(n), OFF_Q // width))
    def kv(col0, chunk_of):
        return pl.BlockSpec((CHUNK, KV_W), lambda n: (chunk_of(idx(n)), col0 // KV_W))

    chunks = (lambda c: jnp.maximum(c - 1, 0)), (lambda c: c), (lambda c: 0)
    k_specs = [kv(OFF_K, f) for f in chunks]
    v_specs = [kv(OFF_V, f) for f in chunks]
    dkv = pl.BlockSpec((CHUNK, KV_W), lambda n: (idx(n), 0))
    sink = _full((KV_HEADS, ATTN_GROUP * CHUNK, 1))
    bias = _full((KV_HEADS, ATTN_GROUP * CHUNK, 2 * CHUNK))
    return o, q_proj, k_specs, v_specs, dkv, sink, bias, idx


def _swa_fwd(proj, sink_rows, bias):
    t_rows = proj.shape[0]
    nt = t_rows // CHUNK
    o_spec, q_spec, k_specs, v_specs, _, sink_spec, bias_spec, _ = _swa_specs(nt, False)
    kv_w = ATTN_GROUP * DH

    def body(q_ref, kp_ref, kc_ref, km_ref, vp_ref, vc_ref, vm_ref, sink_ref, bias_ref, o_ref):
        n = pl.program_id(0)
        for k in range(KV_HEADS):
            hd = slice(k * DH, (k + 1) * DH)
            _, _, _, p_band, p_meta, _ = _swa_probs(q_ref[:, k * kv_w:(k + 1) * kv_w], kp_ref[:, hd], kc_ref[:, hd],
                                                    km_ref[:, hd], sink_ref[k], bias_ref[k], n)
            vcat = jnp.concatenate([vp_ref[:, hd], vc_ref[:, hd]], axis=0)
            out = _dot(p_band, vcat, ((1,), (0,))) + _dot(p_meta, vm_ref[:, hd][META_PAD:, :], ((1,), (0,)))
            for g in range(ATTN_GROUP):
                o_ref[:, k * kv_w + g * DH:k * kv_w + (g + 1) * DH] = out[g * CHUNK:(g + 1) * CHUNK, :]

    return pl.pallas_call(
        body, grid=(nt,), in_specs=[q_spec] + k_specs + v_specs + [sink_spec, bias_spec],
        out_specs=o_spec, out_shape=_sds((t_rows, ATTN_HEADS * DH), F32),
        name="swa_fwd", compiler_params=_params(1))(proj, proj, proj, proj, proj, proj, proj, sink_rows, bias)


def _swa_bwd(proj, sink_rows, bias, out, dout):
    t_rows = proj.shape[0]
    nt = t_rows // CHUNK
    o_spec, q_spec, k_specs, v_specs, dkv_spec, sink_spec, bias_spec, idx = _swa_specs(nt, True)
    kv_w = ATTN_GROUP * DH

    def body(q_ref, kp_ref, kc_ref, km_ref, vp_ref, vc_ref, vm_ref, sink_ref, bias_ref, o_ref, do_ref,
             dq_ref, dk_ref, dv_ref, dsink_ref, carry_k, carry_v, meta_k, meta_v, dk_buf, dv_buf, dq_buf):
        step = pl.program_id(0)
        n = idx(step)

        @pl.when(step == 0)
        def _():
            carry_k[...] = jnp.zeros_like(carry_k)
            carry_v[...] = jnp.zeros_like(carry_v)
            meta_k[...] = jnp.zeros_like(meta_k)
            meta_v[...] = jnp.zeros_like(meta_v)
            dsink_ref[...] = jnp.zeros_like(dsink_ref)

        for k in range(KV_HEADS):
            cols = slice(k * kv_w, (k + 1) * kv_w)
            hd = slice(k * DH, (k + 1) * DH)
            qs, kcat, kmeta, p_band, p_meta, p_sink = _swa_probs(q_ref[:, cols], kp_ref[:, hd], kc_ref[:, hd],
                                                                 km_ref[:, hd], sink_ref[k], bias_ref[k], n)
            vcat = jnp.concatenate([vp_ref[:, hd], vc_ref[:, hd]], axis=0)
            vmeta = vm_ref[:, hd][META_PAD:, :]
            o, do = o_ref[:, cols], do_ref[:, cols]
            os_ = jnp.concatenate([o[:, g * DH:(g + 1) * DH] for g in range(ATTN_GROUP)], axis=0)
            dos = jnp.concatenate([do[:, g * DH:(g + 1) * DH] for g in range(ATTN_GROUP)], axis=0)
            delta = jnp.sum(dos * os_, axis=1, keepdims=True)
            ds_band = p_band * (_dot(dos, vcat, ((1,), (1,))) - delta)
            ds_meta = p_meta * (_dot(dos, vmeta, ((1,), (1,))) - delta)
            ds_sink = -p_sink * delta
            dqs = (_dot(ds_band, kcat, ((1,), (0,))) + _dot(ds_meta, kmeta, ((1,), (0,)))) * (DH ** -0.5)
            for g in range(ATTN_GROUP):
                dq_buf[:, k * kv_w + g * DH:k * kv_w + (g + 1) * DH] = dqs[g * CHUNK:(g + 1) * CHUNK, :]
                dsink_ref[k, g:g + 1, :] += jnp.sum(ds_sink[g * CHUNK:(g + 1) * CHUNK, :])
            dkcat = _dot(ds_band, qs, ((0,), (0,)))
            dvcat = _dot(p_band, dos, ((0,), (0,)))
            meta_k[:, hd] += _dot(ds_meta, qs, ((0,), (0,)))
            meta_v[:, hd] += _dot(p_meta, dos, ((0,), (0,)))
            dk_buf[:, hd] = dkcat[CHUNK:, :] + carry_k[:, hd]
            dv_buf[:, hd] = dvcat[CHUNK:, :] + carry_v[:, hd]
            carry_k[:, hd] = dkcat[:CHUNK, :]
            carry_v[:, hd] = dvcat[:CHUNK, :]

        @pl.when(n == 0)
        def _():
            dk_buf[META_PAD:, :] += meta_k[...]
            dv_buf[META_PAD:, :] += meta_v[...]

        dq_ref[...] = dq_buf[...].astype(BF16)
        dk_ref[...] = dk_buf[...].astype(BF16)
        dv_ref[...] = dv_buf[...].astype(BF16)

    return pl.pallas_call(
        body, grid=(nt,),
        in_specs=[q_spec] + k_specs + v_specs + [sink_spec, bias_spec, o_spec, o_spec],
        out_specs=[o_spec, dkv_spec, dkv_spec, _full((KV_HEADS, 8, 128))],
        out_shape=[_sds((t_rows, ATTN_HEADS * DH), BF16), _sds((t_rows, KV_W), BF16),
                   _sds((t_rows, KV_W), BF16), _sds((KV_HEADS, 8, 128), F32)],
        scratch_shapes=[pltpu.VMEM((CHUNK, KV_W), F32), pltpu.VMEM((CHUNK, KV_W), F32),
                        pltpu.VMEM((N_META, KV_W), F32), pltpu.VMEM((N_META, KV_W), F32),
                        pltpu.VMEM((CHUNK, KV_W), F32), pltpu.VMEM((CHUNK, KV_W), F32),
                        pltpu.VMEM((CHUNK, ATTN_HEADS * DH), F32)],
        name="swa_bwd", compiler_params=_params(1))(proj, proj, proj, proj, proj, proj, proj, sink_rows, bias, out, dout)


def _pack_w_in_t(w_in_t):
    w_dt = w_in_t[CUT_DT:CUT_Q].reshape(SSM_GROUPS, HEADS_PER_GROUP, D_MODEL)
    w_dt = jnp.pad(w_dt, ((0, 0), (0, 128 - HEADS_PER_GROUP), (0, 0))).reshape(SSM_GROUPS * 128, D_MODEL)
    return jnp.concatenate([w_in_t[CUT_Z:CUT_XBC], w_in_t[CUT_G:], w_in_t[CUT_XBC:CUT_DT], w_in_t[CUT_Q:CUT_K],
                            w_in_t[CUT_K:CUT_V], w_in_t[CUT_V:CUT_G], w_dt], axis=0)


def _unpack_w_in_t(wp_t):
    w_dt = wp_t[OFF_DT:].reshape(SSM_GROUPS, 128, D_MODEL)[:, :HEADS_PER_GROUP].reshape(SSM_HEADS, D_MODEL)
    return jnp.concatenate([wp_t[OFF_Z:OFF_GATE], wp_t[OFF_XBC:OFF_Q], w_dt, wp_t[OFF_Q:OFF_K], wp_t[OFF_K:OFF_V],
                            wp_t[OFF_V:OFF_DT], wp_t[OFF_GATE:OFF_XBC]], axis=0)


def _group_rows(v, width):
    return jnp.pad(v.reshape(SSM_GROUPS, 1, HEADS_PER_GROUP), ((0, 0), (0, 0), (0, width - HEADS_PER_GROUP)))


def _local_step(x, target, wt, late_weights=None, on_grad=None, started=None):
    seq = x.shape[0]
    grads = {}

    def emit(name, g):
        grads[name] = g
        return None if on_grad is None else on_grad(name, g)
    meta = wt["meta_tokens"]
    wp_t = _pack_w_in_t(wt["w_in_t"])
    dtb = _group_rows(wt["ssm_dt_bias"].reshape(-1), 128)
    alog = _group_rows(wt["ssm_a_log"].reshape(-1), 128)
    dskip_x = jnp.repeat(wt["ssm_d_skip"].reshape(-1), HEAD_P).reshape(SSM_GROUPS, 1, GROUP_W)
    sink_rows = jnp.repeat(wt["attn_sinks"].reshape(KV_HEADS, ATTN_GROUP), CHUNK, axis=1).reshape(KV_HEADS, ATTN_GROUP * CHUNK, 1)

    hn = _prenorm(x, meta, wt["norm_pre_mix"])
    proj = _matmul(hn, wp_t, tb=True, out_dtype=BF16, name="in_proj", after=started)
    dt_raw = _matmul(hn, wp_t[OFF_DT:], tb=True, name="in_proj_dt")
    xc, xact = _ssm_conv_fwd(proj, wt["ssm_conv_w"], wt["ssm_conv_b"])
    y, hst = _ssd_fwd(xact, dt_raw, dtb, alog, dskip_x)
    yn = _ssm_post(y, proj, wt["ssm_norm"])
    if late_weights is not None:
        wt = {**wt, **late_weights(yn)}
    y_ssm = _matmul(yn, wt["w_ssm_out"], name="ssm_out")
    bias = _swa_bias()
    attn = _swa_fwd(proj, sink_rows, bias)
    y_attn = _matmul(attn, wt["w_attn_out"], name="attn_out")
    mixed = _mix_fwd(proj, y_ssm, y_attn)
    mix = _matmul(mixed, wt["w_mix_out"], name="mix_out")
    h1, hn2 = _postmix(x, meta, mix, wt["norm_post_mix"], wt["norm_pre_ffn"])
    up = _matmul(hn2, wt["w_ffn_up_t"], tb=True, out_dtype=BF16, name="ffn_up")
    act = _ffn_act(up, wt["ffn_conv_w"], wt["ffn_conv_b"])
    f = _matmul(act, wt["w_ffn_down"], name="ffn_down")
    df, dy, g_norm_post_ffn, loss_row = _final(h1, f, target, wt["norm_post_ffn"])

    grads["norm_post_ffn"] = g_norm_post_ffn
    sent = emit("w_ffn_down", _matmul(act, df, ta=True, out_dtype=BF16, name="dw_ffn_down"))
    dact = _matmul(df, wt["w_ffn_down"], tb=True, out_dtype=BF16, name="d_act", after=sent)
    dup, grads["ffn_conv_w"], grads["ffn_conv_b"] = _ffn_act_bwd(up, dact, wt["ffn_conv_w"], wt["ffn_conv_b"])
    sent = emit("w_ffn_up_t", _matmul(dup, hn2, ta=True, out_dtype=BF16, name="dw_ffn_up"))
    dhn2 = _matmul(dup, wt["w_ffn_up_t"], name="d_hn2", after=sent)
    dmix, dh, grads["norm_pre_ffn"], grads["norm_post_mix"] = _postmix_bwd(h1, dhn2, dy, mix, wt["norm_pre_ffn"], wt["norm_post_mix"])
    sent = emit("w_mix_out", _matmul(mixed, dmix, ta=True, out_dtype=BF16, name="dw_mix_out"))
    dmixed = _matmul(dmix, wt["w_mix_out"], tb=True, name="d_mixed", after=sent)
    dy_ssm, dy_attn, dglog = _mix_bwd(dmixed, proj, y_ssm, y_attn)
    sent = emit("w_ssm_out", _matmul(yn, dy_ssm, ta=True, out_dtype=BF16, name="dw_ssm_out"))
    dyn = _matmul(dy_ssm, wt["w_ssm_out"], tb=True, out_dtype=BF16, name="d_yn", after=sent)
    sent = emit("w_attn_out", _matmul(attn, dy_attn, ta=True, out_dtype=BF16, name="dw_attn_out"))
    dattn = _matmul(dy_attn, wt["w_attn_out"], tb=True, name="d_attn", after=sent)
    dy_ssd, dz, grads["ssm_norm"] = _ssm_post_bwd(y, proj, dyn, wt["ssm_norm"])
    dxs, dbm, dcm, ddt, dalog, ddtb, dd_x = _ssd_bwd(xact, dt_raw, dtb, alog, dskip_x, dy_ssd, hst)
    grads["ssm_a_log"] = dalog[:, 0, :HEADS_PER_GROUP].reshape(1, SSM_HEADS)
    grads["ssm_dt_bias"] = ddtb[:, 0, :HEADS_PER_GROUP].reshape(1, SSM_HEADS)
    grads["ssm_d_skip"] = dd_x.reshape(SSM_HEADS, HEAD_P).sum(axis=1).reshape(1, SSM_HEADS)
    dxbc, grads["ssm_conv_w"], grads["ssm_conv_b"] = _ssm_conv_bwd(xc, proj, dxs, dbm, dcm, wt["ssm_conv_w"])
    dq, dk, dv, dsink = _swa_bwd(proj, sink_rows, bias, attn, dattn)
    grads["attn_sinks"] = dsink[:, :ATTN_GROUP, 0].reshape(1, ATTN_HEADS)
    dproj = jnp.concatenate([dz, dglog, dxbc, dq, dk, dv, ddt], axis=1)
    sent = emit("w_in_t", _unpack_w_in_t(_matmul(dproj, hn, ta=True, out_dtype=BF16, name="dw_in")))
    dhn = _matmul(dproj, wp_t, name="d_hn", after=sent)
    grad_x, grads["meta_tokens"], grads["norm_pre_mix"] = _prenorm_bwd(x, meta, dhn, dh, wt["norm_pre_mix"])
    return loss_row[0, 0], grad_x, grads


def _all_gather(shards):
    n = len(shards)

    def body(*refs):
        ins, outs = refs[:n], refs[n:2 * n]
        send_sems, recv_sems, local_sems = refs[2 * n:]
        x, y, c = lax.axis_index("x"), lax.axis_index("y"), lax.axis_index("c")
        me, sibling = (x, y, c), (x, y, 1 - c)
        chips = [(1 - x, y), (x, 1 - y), (1 - x, 1 - y)]

        def slot(a, dev):
            return outs[a].at[4 * dev[0] + 2 * dev[1] + dev[2]]

        def copy(k, a, block, to, src=None):
            return pltpu.make_async_remote_copy(
                src_ref=slot(a, block) if src is None else src, dst_ref=slot(a, block),
                send_sem=send_sems.at[k, a], recv_sem=recv_sems.at[k, a],
                device_id=to, device_id_type=pl.DeviceIdType.MESH)

        mine = [pltpu.make_async_copy(ins[a], slot(a, me), local_sems.at[a]) for a in range(n)]
        for cp in mine:
            cp.start()
        first = [copy(0, a, me, sibling, src=ins[a]) for a in range(n)]
        for j, chip in enumerate(chips):
            first += [copy(1 + j, a, me, (*chip, c), src=ins[a]) for a in range(n)]
        for cp in first:
            cp.start()
        passed = []
        for j, chip in enumerate(chips):
            for a in range(n):
                copy(1 + j, a, (*chip, c), me).wait_recv()
                fwd = copy(4 + j, a, (*chip, c), sibling)
                fwd.start()
                passed.append(fwd)
        for a in range(n):
            copy(0, a, sibling, me).wait_recv()
        for j, chip in enumerate(chips):
            for a in range(n):
                copy(4 + j, a, (*chip, 1 - c), me).wait_recv()
        for cp in first + passed:
            cp.wait_send()
        for cp in mine:
            cp.wait()

    hbm = pl.BlockSpec(memory_space=pl.ANY)
    return pl.pallas_call(
        body, in_specs=[hbm] * n, out_specs=[hbm] * n,
        out_shape=[_sds((N_DEV,) + s.shape, s.dtype) for s in shards],
        scratch_shapes=[pltpu.SemaphoreType.DMA((7, n)), pltpu.SemaphoreType.DMA((7, n)), pltpu.SemaphoreType.DMA((n,))],
        name="gather_weights")(*shards)


def _peer_table():
    x, y, c = lax.axis_index("x"), lax.axis_index("y"), lax.axis_index("c")
    peers = []
    for k in range(N_DEV - 1):
        bits = k + 1
        p = (x ^ ((bits >> 2) & 1), y ^ ((bits >> 1) & 1), c ^ (bits & 1))
        peers.append((k, p, 4 * p[0] + 2 * p[1] + p[2]))
    return 4 * x + 2 * y + c, peers


_HBM = pl.BlockSpec(memory_space=pltpu.HBM)
_SEM = pl.BlockSpec(memory_space=pltpu.SEMAPHORE)
_EFFECT = pltpu.SideEffectType.DATAFLOW_SIDE_EFFECTING


def _push_copy(src, land, send_sems, recv_sems, a, k, p, src_slot, dst_slot):
    sem = a * (N_DEV - 1) + k
    return pltpu.make_async_remote_copy(
        src_ref=src[a] if src_slot is None else src[a].at[src_slot], dst_ref=land[a].at[dst_slot],
        send_sem=send_sems.at[sem], recv_sem=recv_sems.at[sem], device_id=p, device_id_type=pl.DeviceIdType.MESH)


def _push_start(srcs, scatter, name):
    n = len(srcs)
    lands = [lax.empty(s.shape if scatter else (N_DEV,) + s.shape, s.dtype) for s in srcs]

    def body(*refs):
        src, land = refs[:n], refs[n:2 * n]
        send_sems, recv_sems, token = refs[2 * n], refs[2 * n + 1], refs[-1]
        my_id, peers = _peer_table()
        for a in range(n):
            for k, p, p_id in peers:
                _push_copy(src, land, send_sems, recv_sems, a, k, p, p_id if scatter else None, my_id).start()
        token[...] = jnp.zeros_like(token)

    sems = pltpu.SemaphoreType.DMA(((N_DEV - 1) * n,))
    res = pl.pallas_call(
        body, name=name,
        out_shape=(sems, sems, *[pltpu.HBM(a.shape, a.dtype) for a in srcs + lands], _sds((8, 128), F32)),
        in_specs=[_HBM] * (2 * n), out_specs=(_SEM, _SEM, *[_HBM] * (2 * n), pl.BlockSpec(memory_space=pltpu.VMEM)),
        input_output_aliases={i: 2 + i for i in range(2 * n)},
        compiler_params=pltpu.CompilerParams(has_side_effects=_EFFECT),
    )(*[pltpu.with_memory_space_constraint(a, pltpu.HBM) for a in srcs + lands])
    return dict(send=res[0], recv=res[1], src=list(res[2:2 + n]), land=list(res[2 + n:2 + 2 * n]), token=res[-1],
                scatter=scatter)


def _push_wait(handle, after, name):
    n = len(handle["src"])
    scatter = handle["scatter"]

    def body(*refs):
        src, land = refs[:n], refs[n:2 * n]
        send_sems, recv_sems = refs[2 * n], refs[2 * n + 1]
        _, peers = _peer_table()
        for a in range(n):
            for k, p, p_id in peers:
                cp = _push_copy(src, land, send_sems, recv_sems, a, k, p, p_id if scatter else None, p_id)
                cp.wait_send()
                cp.wait_recv()

    arrays = handle["src"] + handle["land"]
    res = pl.pallas_call(
        body, name=name, out_shape=tuple(pltpu.HBM(a.shape, a.dtype) for a in arrays),
        in_specs=[_HBM] * (2 * n) + [_SEM, _SEM, pl.BlockSpec(memory_space=pl.ANY)], out_specs=tuple([_HBM] * (2 * n)),
        input_output_aliases={i: i for i in range(2 * n)},
        compiler_params=pltpu.CompilerParams(has_side_effects=_EFFECT),
    )(*arrays, handle["send"], handle["recv"], after)
    return list(res[:n]), list(res[n:])


def _slot_sum(p_ref, own_ref):
    if own_ref is not None:
        my_id = 4 * lax.axis_index("x") + 2 * lax.axis_index("y") + lax.axis_index("c")
        mine = own_ref[...].astype(F32)
    g = None
    for s in range(p_ref.shape[0]):
        term = p_ref[s].astype(F32)
        if own_ref is not None:
            term = jnp.where(my_id == s, mine, term)
        g = term if g is None else g + term
    return g


def _to_bf16(arrays):
    n = len(arrays)

    def body(*refs):
        for i in range(n):
            refs[n + i][...] = refs[i][...].astype(BF16)

    return pl.pallas_call(body, out_shape=[_sds(a.shape, BF16) for a in arrays], name="weights_to_bf16",
                          compiler_params=pltpu.CompilerParams(vmem_limit_bytes=VMEM_LIMIT))(*arrays)


def _adamw(parts, own, w, m, v, name):
    rows, cols = w.shape
    if rows % 16 == 0:
        tr, tc = _pick(rows, (256, 128, 176, 64, 32, 16)), cols
    else:
        tr, tc = rows, _pick(cols, (256, 128))

    def body(*refs):
        if own is None:
            p_ref, w_ref, m_ref, v_ref, g_ref, d_ref, nm_ref, nv_ref = refs
            own_ref = None
        else:
            p_ref, own_ref, w_ref, m_ref, v_ref, g_ref, d_ref, nm_ref, nv_ref = refs
        g = _slot_sum(p_ref, own_ref)
        m_new = ADAM_B1 * m_ref[...] + (1.0 - ADAM_B1) * g
        v_new = ADAM_B2 * v_ref[...] + (1.0 - ADAM_B2) * (g * g)
        m_hat = m_new / (1.0 - ADAM_B1 ** ADAM_STEP)
        v_hat = v_new / (1.0 - ADAM_B2 ** ADAM_STEP)
        g_ref[...] = g
        d_ref[...] = -ADAM_LR * (m_hat / (jnp.sqrt(v_hat) + ADAM_EPS) + ADAM_WD * w_ref[...])
        nm_ref[...] = m_new
        nv_ref[...] = v_new

    by_rows = tc == cols
    spec = pl.BlockSpec((tr, tc), (lambda i: (i, 0)) if by_rows else (lambda i: (0, i)))
    parts_spec = pl.BlockSpec((parts.shape[0], tr, tc), (lambda i: (0, i, 0)) if by_rows else (lambda i: (0, 0, i)))
    operands = (parts, w, m, v) if own is None else (parts, own, w, m, v)
    return pl.pallas_call(
        body, grid=(rows // tr if by_rows else cols // tc,),
        in_specs=[parts_spec] + [spec] * (len(operands) - 1),
        out_specs=[spec] * 4, out_shape=[_sds((rows, cols), F32)] * 4,
        name=name, compiler_params=_params(1))(*operands)


SMALL_REPLICATED = (("norm_pre_mix", 1024), ("ssm_conv_b", 3072), ("ssm_dt_bias", 32), ("ssm_a_log", 32),
                    ("ssm_d_skip", 32), ("ssm_norm", 2048), ("attn_sinks", 16), ("norm_post_mix", 1024),
                    ("norm_pre_ffn", 1024), ("ffn_conv_b", 5632), ("norm_post_ffn", 1024))
SMALL_SHARDED = (("meta_tokens", (N_META, D_MODEL // N_DEV)), ("ssm_conv_w", (SSM_CONV, CONV_DIM // N_DEV)),
                 ("ffn_conv_w", (FFN_CONV, 2 * FFN_DIM // N_DEV)))
BIG = (("w_in", (D_MODEL, N_IN // N_DEV), 1), ("w_ssm_out", (D_INNER // N_DEV, D_MODEL), 0),
       ("w_attn_out", (D_MODEL // N_DEV, D_MODEL), 0), ("w_mix_out", (D_MODEL // N_DEV, D_MODEL), 0),
       ("w_ffn_up", (D_MODEL, 2 * FFN_DIM // N_DEV), 1), ("w_ffn_down", (FFN_DIM // N_DEV, D_MODEL), 0))


def _rows_of(size):
    return -(-size // 128)


def _as_rows(flat):
    size = flat.shape[-1]
    rows = _rows_of(size)
    flat = jnp.pad(flat, [(0, 0)] * (flat.ndim - 1) + [(0, rows * 128 - size)])
    return flat.reshape(flat.shape[:-1] + (rows, 128))


def _pack_small(rep, sharded):
    pieces = [_as_rows(rep[name].reshape(-1)) for name, _ in SMALL_REPLICATED]
    pieces += [_as_rows(sharded[name].reshape(-1)) for name, _ in SMALL_SHARDED]
    packed = jnp.concatenate(pieces, axis=0)
    return jnp.pad(packed, ((0, -packed.shape[0] % 8), (0, 0)))


def _unpack_small(packed):
    out, row = {}, 0
    for name, size in SMALL_REPLICATED:
        out[name] = packed[row:row + _rows_of(size)].reshape(-1)[:size].reshape(1, size)
        row += _rows_of(size)
    for name, (r, c) in SMALL_SHARDED:
        out[name] = packed[row:row + _rows_of(r * c)].reshape(-1)[:r * c].reshape(r, c)
        row += _rows_of(r * c)
    return out


def _shard_major(g, shape, axis):
    r, c = shape
    if axis == 0:
        return g.reshape(N_DEV, r, c)
    return g.reshape(r, N_DEV, c).transpose(1, 0, 2)


def kernel(x, meta_tokens, norm_pre_mix, w_in, ssm_conv_w, ssm_conv_b, ssm_dt_bias, ssm_a_log, ssm_d_skip, ssm_norm, w_ssm_out, attn_sinks, w_attn_out, w_mix_out, norm_post_mix, norm_pre_ffn, w_ffn_up, ffn_conv_w, ffn_conv_b, w_ffn_down, norm_post_ffn, loss_target, m_meta_tokens, m_norm_pre_mix, m_w_in, m_ssm_conv_w, m_ssm_conv_b, m_ssm_dt_bias, m_ssm_a_log, m_ssm_d_skip, m_ssm_norm, m_w_ssm_out, m_attn_sinks, m_w_attn_out, m_w_mix_out, m_norm_post_mix, m_norm_pre_ffn, m_w_ffn_up, m_ffn_conv_w, m_ffn_conv_b, m_w_ffn_down, m_norm_post_ffn, v_meta_tokens, v_norm_pre_mix, v_w_in, v_ssm_conv_w, v_ssm_conv_b, v_ssm_dt_bias, v_ssm_a_log, v_ssm_d_skip, v_ssm_norm, v_w_ssm_out, v_attn_sinks, v_w_attn_out, v_w_mix_out, v_norm_post_mix, v_norm_pre_ffn, v_w_ffn_up, v_ffn_conv_w, v_ffn_conv_b, v_w_ffn_down, v_norm_post_ffn):
    names = ("meta_tokens", "norm_pre_mix", "w_in", "ssm_conv_w", "ssm_conv_b", "ssm_dt_bias", "ssm_a_log", "ssm_d_skip",
             "ssm_norm", "w_ssm_out", "attn_sinks", "w_attn_out", "w_mix_out", "norm_post_mix", "norm_pre_ffn", "w_ffn_up",
             "ffn_conv_w", "ffn_conv_b", "w_ffn_down", "norm_post_ffn")
    w_loc = dict(zip(names, (meta_tokens, norm_pre_mix, w_in, ssm_conv_w, ssm_conv_b, ssm_dt_bias, ssm_a_log, ssm_d_skip,
                             ssm_norm, w_ssm_out, attn_sinks, w_attn_out, w_mix_out, norm_post_mix, norm_pre_ffn, w_ffn_up,
                             ffn_conv_w, ffn_conv_b, w_ffn_down, norm_post_ffn)))
    m_loc = dict(zip(names, (m_meta_tokens, m_norm_pre_mix, m_w_in, m_ssm_conv_w, m_ssm_conv_b, m_ssm_dt_bias, m_ssm_a_log,
                             m_ssm_d_skip, m_ssm_norm, m_w_ssm_out, m_attn_sinks, m_w_attn_out, m_w_mix_out, m_norm_post_mix,
                             m_norm_pre_ffn, m_w_ffn_up, m_ffn_conv_w, m_ffn_conv_b, m_w_ffn_down, m_norm_post_ffn)))
    v_loc = dict(zip(names, (v_meta_tokens, v_norm_pre_mix, v_w_in, v_ssm_conv_w, v_ssm_conv_b, v_ssm_dt_bias, v_ssm_a_log,
                             v_ssm_d_skip, v_ssm_norm, v_w_ssm_out, v_attn_sinks, v_w_attn_out, v_w_mix_out, v_norm_post_mix,
                             v_norm_pre_ffn, v_w_ffn_up, v_ffn_conv_w, v_ffn_conv_b, v_w_ffn_down, v_norm_post_ffn)))

    def local2d(d, name):
        a = d[name]
        return a if name == "meta_tokens" else a.reshape(a.shape[1:])

    def turned2d(d, name):
        a = jnp.swapaxes(d[name], 1, 2)
        return a.reshape(a.shape[1:])

    my_id = 4 * lax.axis_index("x") + 2 * lax.axis_index("y") + lax.axis_index("c")
    big = {name: (shape, axis) for name, shape, axis in BIG}

    def whole(name, g):
        return g.reshape(N_DEV * g.shape[1], g.shape[2])

    def key(name):
        return name + "_t" if big[name][1] == 1 else name

    by_rows = [name for name, _, axis in BIG if axis == 0]
    send_bf16 = dict(zip(by_rows, _to_bf16([local2d(w_loc, name) for name in by_rows])))
    for name, _, axis in BIG:
        if axis == 1:
            send_bf16[name] = turned2d(w_loc, name).astype(BF16)
    small_shard_pack = jnp.concatenate([_as_rows(local2d(w_loc, name).reshape(-1)) for name, _ in SMALL_SHARDED], axis=0)
    small_shard_pack = jnp.pad(small_shard_pack, ((0, -small_shard_pack.shape[0] % 8), (0, 0)))
    first = _all_gather([send_bf16["w_in"], small_shard_pack])
    rest_names = [name for name, _, _ in BIG if name != "w_in"]
    rest = [send_bf16[name] for name in rest_names]
    rest, first = lax.optimization_barrier((rest, first))
    rest_handle = _push_start(rest, False, "gather_rest_start")
    wt = {"w_in_t": whole("w_in", first[0])}
    row = 0
    for name, (r, c) in SMALL_SHARDED:
        blocks = first[1][:, row:row + _rows_of(r * c)].reshape(N_DEV, -1)[:, :r * c].reshape(N_DEV, r, c)
        wt[name] = blocks.transpose(1, 0, 2).reshape(r, N_DEV * c)
        row += _rows_of(r * c)
    for name, size in SMALL_REPLICATED:
        wt[name] = w_loc[name].reshape(1, size)

    def late_weights(after):
        own, landed = _push_wait(rest_handle, after, "gather_rest_wait")
        out = {}
        for name, mine, land in zip(rest_names, own, landed):
            out[key(name)] = whole(name, lax.dynamic_update_index_in_dim(land, mine, my_id, 0))
        return out

    sent = {}

    def on_grad(known_as, g):
        name = known_as.removesuffix("_t")
        by_owner = g.reshape(N_DEV, g.shape[0] // N_DEV, g.shape[1])
        sent[name] = _push_start([by_owner], True, "send_" + name)
        return sent[name]["token"]

    loss_part, grad_x, grads = _local_step(x[0], loss_target[0], wt, late_weights, on_grad, rest_handle["token"])
    loss = lax.psum(loss_part, AXES)

    small_parts = []
    for name, (r, c) in SMALL_SHARDED:
        small_parts.append(_as_rows(_shard_major(grads[name], (r, c), 1).reshape(N_DEV, r * c)))
    rep_rows = jnp.concatenate([_as_rows(grads[name].reshape(-1)) for name, _ in SMALL_REPLICATED], axis=0)
    small_send = jnp.concatenate([jnp.broadcast_to(rep_rows[None], (N_DEV,) + rep_rows.shape)] + small_parts, axis=1)
    small_send = jnp.pad(small_send, ((0, 0), (0, -small_send.shape[1] % 8), (0, 0)))
    small_handle = _push_start([small_send], True, "send_small")

    def small_pack(d):
        return _pack_small({name: d[name] for name, _ in SMALL_REPLICATED}, {name: local2d(d, name) for name, _ in SMALL_SHARDED})

    def arrived(handle, after, name):
        src, landed = _push_wait(handle, after, "arrived_" + name)
        return landed[0], lax.dynamic_index_in_dim(src[0], my_id, 0, keepdims=False)

    grad_w, delta_w, new_m, new_v = {}, {}, {}, {}
    outs = None
    after = small_handle["token"]
    for name, handle in sent.items():
        if name == "w_in":
            parts, own = arrived(small_handle, after, "small")
            outs = _adamw(parts, own, small_pack(w_loc), small_pack(m_loc), small_pack(v_loc), "adamw_small")
            after = outs[0]
        parts, own = arrived(handle, after, name)
        turned = big[name][1] == 1
        state = [turned2d(d, name) if turned else local2d(d, name) for d in (w_loc, m_loc, v_loc)]
        results = _adamw(parts, own, *state, "adamw_" + name)
        after = results[0]
        full = (1,) + big[name][0]
        for dst, a in zip((grad_w, delta_w, new_m, new_v), results):
            dst[name] = jnp.swapaxes(a[None], 1, 2) if turned else a.reshape(full)
    for dst, packed in zip((grad_w, delta_w, new_m, new_v), outs):
        for name, a in _unpack_small(packed).items():
            dst[name] = a.reshape(w_loc[name].shape)

    return (loss, grad_x[None], *[grad_w[n] for n in names], *[delta_w[n] for n in names],
            *[new_m[n] for n in names], *[new_v[n] for n in names])
```

```python
import jax
import jax.numpy as jnp
from jax import lax
from jax.experimental import pallas as pl
from jax.experimental.pallas import tpu as pltpu

F32 = jnp.float32
BF16 = jnp.bfloat16
HIGHEST = lax.Precision.HIGHEST

D_MODEL = 1024
N_META = 16
CHUNK = 128
META_PAD = CHUNK - N_META
D_INNER = 2048
HEAD_P = 64
SSM_HEADS = 32
SSM_GROUPS = 4
HEADS_PER_GROUP = SSM_HEADS // SSM_GROUPS
GROUP_W = HEADS_PER_GROUP * HEAD_P
D_STATE = 128
SSM_CONV = 4
CONV_DIM = D_INNER + 2 * SSM_GROUPS * D_STATE
ATTN_HEADS = 16
KV_HEADS = 4
ATTN_GROUP = ATTN_HEADS // KV_HEADS
DH = 64
KV_W = KV_HEADS * DH
FFN_DIM = 2816
FFN_CONV = 3
EPS = 1e-6
NEG = -1e30
N_DEV = 8
AXES = ("x", "y", "c")

OFF_Z, OFF_GATE, OFF_XBC, OFF_Q, OFF_K, OFF_V, OFF_DT = 0, 2048, 4096, 7168, 8192, 8448, 8704
N_INP = OFF_DT + SSM_GROUPS * 128
CUT_Z, CUT_XBC, CUT_DT, CUT_Q, CUT_K, CUT_V, CUT_G = 0, 2048, 5120, 5152, 6176, 6432, 6688
N_IN = 8736

ADAM_LR, ADAM_B1, ADAM_B2, ADAM_EPS, ADAM_WD, ADAM_STEP = 0.001, 0.9, 0.999, 1e-08, 0.01, 10

VMEM_LIMIT = 56 * 1024 * 1024


def _params(n_grid):
    return pltpu.CompilerParams(dimension_semantics=("arbitrary",) * n_grid, vmem_limit_bytes=VMEM_LIMIT)


def _sds(shape, dtype):
    return jax.ShapeDtypeStruct(shape, dtype)


def _pick(n, prefs):
    for c in prefs:
        if n % c == 0:
            return c
    raise ValueError(f"no tile of {prefs} divides {n}")


def _row(tr, width, cb=0):
    return pl.BlockSpec((tr, width), lambda i: (i, cb))


def _row_rev(tr, width, nt, cb=0):
    return pl.BlockSpec((tr, width), lambda i: (nt - 1 - i, cb))


def _full(shape):
    return pl.BlockSpec(shape, lambda *_: (0,) * len(shape))


def _sigmoid(x):
    return 1.0 / (1.0 + jnp.exp(-x))


def _softplus(x):
    return jnp.maximum(x, 0.0) + jnp.log(1.0 + jnp.exp(-jnp.abs(x)))


def _rms(x):
    return lax.rsqrt(jnp.mean(x * x, axis=-1, keepdims=True) + EPS)


def _rms_bwd(x, r, w, dy):
    xh = x * r
    g = dy * w
    dx = r * (g - xh * jnp.mean(g * xh, axis=-1, keepdims=True))
    return dx, jnp.sum(dy * xh, axis=0, keepdims=True)


def _row_ids(shape, tile_index, tr):
    return tile_index * tr + lax.broadcasted_iota(jnp.int32, shape, 0)


HALO = 8


def _causal_taps(x, halo, first_step, taps):
    n = x.shape[0]

    @pl.when(first_step)
    def _():
        halo[...] = jnp.zeros_like(halo)

    before = halo[...]
    row = lax.broadcasted_iota(jnp.int32, before.shape, 0)
    shifted = [x]
    for s in range(1, taps):
        rolled = pltpu.roll(x, s, 0)
        head = jnp.where(row < s, pltpu.roll(before, s, 0), rolled[0:HALO, :])
        shifted.append(jnp.concatenate([head, rolled[HALO:, :]], axis=0))
    halo[...] = x[n - HALO:, :]
    return shifted


def _anticausal_taps(x, halo, first_step, taps):
    n = x.shape[0]

    @pl.when(first_step)
    def _():
        halo[...] = jnp.zeros_like(halo)

    after = halo[...]
    row = lax.broadcasted_iota(jnp.int32, after.shape, 0)
    shifted = [x]
    for s in range(1, taps):
        rolled = pltpu.roll(x, n - s, 0)
        tail = jnp.where(row >= HALO - s, pltpu.roll(after, HALO - s, 0), rolled[n - HALO:, :])
        shifted.append(jnp.concatenate([rolled[:n - HALO, :], tail], axis=0))
    halo[...] = x[0:HALO, :]
    return shifted


def _matmul(a, b, *, ta=False, tb=False, out_dtype=F32, name, after=None):
    if ta:
        k_dim, m_dim = a.shape
    else:
        m_dim, k_dim = a.shape
    n_dim = b.shape[0] if tb else b.shape[1]
    tm = _pick(m_dim, (1408, 1024, 768, 512, 384, 256, 128))
    tn = _pick(n_dim, (1024, 1408, 768, 512, 384, 256, 128))
    if ta:
        tk = _pick(k_dim, (1408, 1024, 768, 512, 384, 256, 128))
    else:
        tk = k_dim if k_dim <= 3072 else _pick(k_dim, (3072, 2816, 2048, 1024))
    nk = k_dim // tk
    dims = (((0 if ta else 1,), (1 if tb else 0,)), ((), ()))

    use_acc = nk > 1 and out_dtype != F32

    def body(a_ref, b_ref, *rest):
        o_ref = rest[-2] if use_acc else rest[-1]
        acc_ref = rest[-1] if use_acc else o_ref
        r = lax.dot_general(a_ref[...].astype(BF16), b_ref[...].astype(BF16), dims, preferred_element_type=F32)
        if nk == 1:
            o_ref[...] = r.astype(o_ref.dtype)
        else:
            k = pl.program_id(2)

            @pl.when(k == 0)
            def _():
                acc_ref[...] = r

            @pl.when(k > 0)
            def _():
                acc_ref[...] += r

            if use_acc:
                @pl.when(k == nk - 1)
                def _():
                    o_ref[...] = acc_ref[...].astype(o_ref.dtype)

    a_spec = pl.BlockSpec((tk, tm), lambda i, j, k: (k, i)) if ta else pl.BlockSpec((tm, tk), lambda i, j, k: (i, k))
    b_spec = pl.BlockSpec((tn, tk), lambda i, j, k: (j, k)) if tb else pl.BlockSpec((tk, tn), lambda i, j, k: (k, j))
    extra_specs, extra = ([], ()) if after is None else ([pl.BlockSpec(memory_space=pl.ANY)], (after,))
    return pl.pallas_call(
        body, grid=(m_dim // tm, n_dim // tn, nk), in_specs=[a_spec, b_spec] + extra_specs,
        out_specs=pl.BlockSpec((tm, tn), lambda i, j, k: (i, j)), out_shape=_sds((m_dim, n_dim), out_dtype),
        scratch_shapes=[pltpu.VMEM((tm, tn), F32)] if use_acc else [],
        name=name, compiler_params=_params(3))(a, b, *extra)


def _seq_specs():
    return [pl.BlockSpec((CHUNK, D_MODEL), lambda i: (jnp.maximum(i - 1, 0), 0)), _full((N_META, D_MODEL))]


def _seq_tile(x_ref, meta_ref, i):
    first = jnp.concatenate([jnp.zeros((META_PAD, D_MODEL), F32), meta_ref[...]], axis=0)
    return jnp.where(i == 0, first, x_ref[...])


def _prenorm(x, meta, w):
    t_rows = x.shape[0] + CHUNK

    def body(x_ref, meta_ref, w_ref, o_ref):
        h = _seq_tile(x_ref, meta_ref, pl.program_id(0))
        o_ref[...] = (h * _rms(h) * w_ref[...]).astype(BF16)

    return pl.pallas_call(body, grid=(t_rows // CHUNK,), in_specs=_seq_specs() + [_full((1, D_MODEL))],
                          out_specs=_row(CHUNK, D_MODEL), out_shape=_sds((t_rows, D_MODEL), BF16),
                          name="prenorm", compiler_params=_params(1))(x, meta, w)


def _xbc_specs(tr, rev_nt=None):
    cbs = [OFF_XBC // 1024 + j for j in range(CONV_DIM // 1024)]
    if rev_nt is None:
        return [_row(tr, 1024, cb) for cb in cbs]
    return [_row_rev(tr, 1024, rev_nt, cb) for cb in cbs]


def _ssm_conv_fwd(proj, conv_w, conv_b):
    t_rows = proj.shape[0]
    tr = CHUNK

    def body(x0, x1, x2, w_ref, b_ref, xc_ref, xa_ref, hist):
        x = jnp.concatenate([x0[...], x1[...], x2[...]], axis=1).astype(F32)
        acc = b_ref[...]
        for s, moved in enumerate(_causal_taps(x, hist, pl.program_id(0) == 0, SSM_CONV)):
            acc = acc + w_ref[SSM_CONV - 1 - s:SSM_CONV - s, :] * moved
        xc_ref[...] = acc
        xa_ref[...] = acc * _sigmoid(acc)

    return pl.pallas_call(
        body, grid=(t_rows // tr,),
        in_specs=_xbc_specs(tr) + [_full((SSM_CONV, CONV_DIM)), _full((1, CONV_DIM))],
        out_specs=[_row(tr, CONV_DIM), _row(tr, CONV_DIM)],
        out_shape=[_sds((t_rows, CONV_DIM), F32), _sds((t_rows, CONV_DIM), F32)],
        scratch_shapes=[pltpu.VMEM((HALO, CONV_DIM), F32)],
        name="ssm_conv_fwd", compiler_params=_params(1))(proj, proj, proj, conv_w, conv_b)


def _ssm_post(y, proj, w):
    t_rows = y.shape[0]
    tr = CHUNK

    def body(y_ref, z_ref, w_ref, o_ref):
        z = z_ref[...].astype(F32)
        yz = y_ref[...] * z * _sigmoid(z)
        o_ref[...] = (yz * _rms(yz) * w_ref[...]).astype(BF16)

    return pl.pallas_call(body, grid=(t_rows // tr,),
                          in_specs=[_row(tr, D_INNER), _row(tr, D_INNER, OFF_Z // D_INNER), _full((1, D_INNER))],
                          out_specs=_row(tr, D_INNER), out_shape=_sds((t_rows, D_INNER), BF16),
                          name="ssm_post", compiler_params=_params(1))(y, proj, w)


def _mix_fwd(proj, y_ssm, y_attn):
    t_rows = y_ssm.shape[0]
    tr = _pick(t_rows, (384, 128))

    def body(g_ref, ys_ref, ya_ref, o_ref):
        g = _sigmoid(g_ref[...].astype(F32))
        o_ref[...] = (g[:, :D_MODEL] * ys_ref[...] + g[:, D_MODEL:] * ya_ref[...]).astype(BF16)

    return pl.pallas_call(body, grid=(t_rows // tr,),
                          in_specs=[_row(tr, 2 * D_MODEL, OFF_GATE // (2 * D_MODEL)), _row(tr, D_MODEL), _row(tr, D_MODEL)],
                          out_specs=_row(tr, D_MODEL), out_shape=_sds((t_rows, D_MODEL), BF16),
                          name="mix_fwd", compiler_params=_params(1))(proj, y_ssm, y_attn)


def _postmix(x, meta, mix, w_post, w_pre):
    t_rows = mix.shape[0]
    tr = CHUNK

    def body(x_ref, meta_ref, m_ref, wp_ref, wf_ref, h1_ref, hn_ref):
        m = m_ref[...]
        h1 = _seq_tile(x_ref, meta_ref, pl.program_id(0)) + m * _rms(m) * wp_ref[...]
        h1 = jnp.where(_row_ids(h1.shape, pl.program_id(0), tr) >= META_PAD, h1, 0.0)
        h1_ref[...] = h1
        hn_ref[...] = (h1 * _rms(h1) * wf_ref[...]).astype(BF16)

    return pl.pallas_call(body, grid=(t_rows // tr,),
                          in_specs=_seq_specs() + [_row(tr, D_MODEL), _full((1, D_MODEL)), _full((1, D_MODEL))],
                          out_specs=[_row(tr, D_MODEL), _row(tr, D_MODEL)],
                          out_shape=[_sds((t_rows, D_MODEL), F32), _sds((t_rows, D_MODEL), BF16)],
                          name="postmix", compiler_params=_params(1))(x, meta, mix, w_post, w_pre)


def _ffn_act(up, conv_w, conv_b):
    t_rows = up.shape[0]
    tr = CHUNK
    width = 2 * FFN_DIM

    def body(up_ref, w_ref, b_ref, u_ref, act_ref, hist):
        x = up_ref[...].astype(F32)
        u = b_ref[...]
        for s, moved in enumerate(_causal_taps(x, hist, pl.program_id(0) == 0, FFN_CONV)):
            u = u + w_ref[FFN_CONV - 1 - s:FFN_CONV - s, :] * moved
        u_ref[...] = u.astype(BF16)
        a = u[:, :FFN_DIM]
        act_ref[...] = (a * _sigmoid(a) * u[:, FFN_DIM:]).astype(BF16)

    return pl.pallas_call(
        body, grid=(t_rows // tr,), in_specs=[_row(tr, width), _full((FFN_CONV, width)), _full((1, width))],
        out_specs=[_row(tr, width), _row(tr, FFN_DIM)],
        out_shape=[_sds((t_rows, width), BF16), _sds((t_rows, FFN_DIM), BF16)],
        scratch_shapes=[pltpu.VMEM((HALO, width), F32)],
        name="ffn_act", compiler_params=_params(1))(up, conv_w, conv_b)


def _final(h1, f, target, w):
    t_rows = h1.shape[0]
    tr = CHUNK

    def body(h1_ref, f_ref, t_ref, w_ref, df_ref, dy_ref, dw_ref, loss_ref):
        i = pl.program_id(0)

        @pl.when(i == 0)
        def _():
            dw_ref[...] = jnp.zeros_like(dw_ref)
            loss_ref[...] = jnp.zeros_like(loss_ref)

        f_val = f_ref[...]
        r = _rms(f_val)
        wv = w_ref[...]
        h2 = h1_ref[...] + f_val * r * wv
        diff = jnp.where(i >= 1, h2 - t_ref[...], 0.0)
        loss_ref[...] += 0.5 * jnp.sum(diff * diff) * (1.0 / D_MODEL)
        dy = diff * (1.0 / D_MODEL)
        dy_ref[...] = dy
        df, dw = _rms_bwd(f_val, r, wv, dy)
        df_ref[...] = df.astype(BF16)
        dw_ref[...] += dw

    tgt_spec = pl.BlockSpec((tr, D_MODEL), lambda i: (jnp.maximum(i - 1, 0), 0))
    return pl.pallas_call(
        body, grid=(t_rows // tr,),
        in_specs=[_row(tr, D_MODEL), _row(tr, D_MODEL), tgt_spec, _full((1, D_MODEL))],
        out_specs=[_row(tr, D_MODEL), _row(tr, D_MODEL), _full((1, D_MODEL)), _full((1, 128))],
        out_shape=[_sds((t_rows, D_MODEL), BF16), _sds((t_rows, D_MODEL), F32), _sds((1, D_MODEL), F32), _sds((1, 128), F32)],
        name="final", compiler_params=_params(1))(h1, f, target, w)


def _ffn_act_bwd(u, up, dact, conv_w):
    t_rows = u.shape[0]
    tr = CHUNK
    nt = t_rows // tr
    width = 2 * FFN_DIM

    def body(u_ref, up_ref, da_ref, w_ref, dup_ref, dw_ref, db_ref, ahead):
        @pl.when(pl.program_id(0) == 0)
        def _():
            dw_ref[...] = jnp.zeros_like(dw_ref)
            db_ref[...] = jnp.zeros_like(db_ref)

        u_val = u_ref[...].astype(F32)
        a, g = u_val[:, :FFN_DIM], u_val[:, FFN_DIM:]
        d = da_ref[...].astype(F32)
        s = _sigmoid(a)
        du = jnp.concatenate([d * g * s * (1.0 + a * (1.0 - s)), d * a * s], axis=1)
        x = up_ref[...].astype(F32)
        dup = None
        for sh, moved in enumerate(_anticausal_taps(du, ahead, pl.program_id(0) == 0, FFN_CONV)):
            k = FFN_CONV - 1 - sh
            term = w_ref[k:k + 1, :] * moved
            dup = term if dup is None else dup + term
            dw_ref[k:k + 1, :] += jnp.sum(moved * x, axis=0, keepdims=True)
        db_ref[...] += jnp.sum(du, axis=0, keepdims=True)
        dup_ref[...] = dup.astype(BF16)

    return pl.pallas_call(
        body, grid=(nt,),
        in_specs=[_row_rev(tr, width, nt), _row_rev(tr, width, nt), _row_rev(tr, FFN_DIM, nt), _full((FFN_CONV, width))],
        out_specs=[_row_rev(tr, width, nt), _full((FFN_CONV, width)), _full((1, width))],
        out_shape=[_sds((t_rows, width), BF16), _sds((FFN_CONV, width), F32), _sds((1, width), F32)],
        scratch_shapes=[pltpu.VMEM((HALO, width), F32)],
        name="ffn_act_bwd", compiler_params=_params(1))(u, up, dact, conv_w)


def _postmix_bwd(h1, dhn2, dy, mix, w_pre, w_post):
    t_rows = h1.shape[0]
    tr = CHUNK

    def body(h1_ref, dhn_ref, dy_ref, m_ref, wf_ref, wp_ref, dmix_ref, dh_ref, dwf_ref, dwp_ref):
        @pl.when(pl.program_id(0) == 0)
        def _():
            dwf_ref[...] = jnp.zeros_like(dwf_ref)
            dwp_ref[...] = jnp.zeros_like(dwp_ref)

        h1v = h1_ref[...]
        dx, dwf = _rms_bwd(h1v, _rms(h1v), wf_ref[...], dhn_ref[...])
        dwf_ref[...] += dwf
        dh1 = dy_ref[...] + dx
        dh1 = jnp.where(_row_ids(dh1.shape, pl.program_id(0), tr) >= META_PAD, dh1, 0.0)
        dh_ref[...] = dh1
        m = m_ref[...]
        dmix, dwp = _rms_bwd(m, _rms(m), wp_ref[...], dh1)
        dwp_ref[...] += dwp
        dmix_ref[...] = dmix.astype(BF16)

    return pl.pallas_call(
        body, grid=(t_rows // tr,),
        in_specs=[_row(tr, D_MODEL)] * 4 + [_full((1, D_MODEL))] * 2,
        out_specs=[_row(tr, D_MODEL), _row(tr, D_MODEL), _full((1, D_MODEL)), _full((1, D_MODEL))],
        out_shape=[_sds((t_rows, D_MODEL), BF16), _sds((t_rows, D_MODEL), F32), _sds((1, D_MODEL), F32), _sds((1, D_MODEL), F32)],
        name="postmix_bwd", compiler_params=_params(1))(h1, dhn2, dy, mix, w_pre, w_post)


def _mix_bwd(dmixed, proj, y_ssm, y_attn):
    t_rows = dmixed.shape[0]
    tr = _pick(t_rows, (384, 128))

    def body(d_ref, g_ref, ys_ref, ya_ref, dys_ref, dya_ref, dg_ref):
        d = d_ref[...]
        g = _sigmoid(g_ref[...].astype(F32))
        g1, g2 = g[:, :D_MODEL], g[:, D_MODEL:]
        dys_ref[...] = (d * g1).astype(BF16)
        dya_ref[...] = (d * g2).astype(BF16)
        dg_ref[...] = jnp.concatenate([d * ys_ref[...] * g1 * (1.0 - g1), d * ya_ref[...] * g2 * (1.0 - g2)],
                                      axis=1).astype(BF16)

    return pl.pallas_call(
        body, grid=(t_rows // tr,),
        in_specs=[_row(tr, D_MODEL), _row(tr, 2 * D_MODEL, OFF_GATE // (2 * D_MODEL)), _row(tr, D_MODEL), _row(tr, D_MODEL)],
        out_specs=[_row(tr, D_MODEL), _row(tr, D_MODEL), _row(tr, 2 * D_MODEL)],
        out_shape=[_sds((t_rows, D_MODEL), BF16), _sds((t_rows, D_MODEL), BF16), _sds((t_rows, 2 * D_MODEL), BF16)],
        name="mix_bwd", compiler_params=_params(1))(dmixed, proj, y_ssm, y_attn)


def _ssm_post_bwd(y, proj, dyn, w):
    t_rows = y.shape[0]
    tr = CHUNK

    def body(y_ref, z_ref, d_ref, w_ref, dy_ref, dz_ref, dw_ref):
        @pl.when(pl.program_id(0) == 0)
        def _():
            dw_ref[...] = jnp.zeros_like(dw_ref)

        yv, z = y_ref[...], z_ref[...].astype(F32)
        sz = _sigmoid(z)
        silu = z * sz
        yz = yv * silu
        dyz, dw = _rms_bwd(yz, _rms(yz), w_ref[...], d_ref[...].astype(F32))
        dw_ref[...] += dw
        dy_ref[...] = dyz * silu
        dz_ref[...] = (dyz * yv * sz * (1.0 + z * (1.0 - sz))).astype(BF16)

    return pl.pallas_call(
        body, grid=(t_rows // tr,),
        in_specs=[_row(tr, D_INNER), _row(tr, D_INNER, OFF_Z // D_INNER), _row(tr, D_INNER), _full((1, D_INNER))],
        out_specs=[_row(tr, D_INNER), _row(tr, D_INNER), _full((1, D_INNER))],
        out_shape=[_sds((t_rows, D_INNER), F32), _sds((t_rows, D_INNER), BF16), _sds((1, D_INNER), F32)],
        name="ssm_post_bwd", compiler_params=_params(1))(y, proj, dyn, w)


def _ssm_conv_bwd(xc, proj, dxs, dbm, dcm, conv_w):
    t_rows = xc.shape[0]
    tr = CHUNK
    nt = t_rows // tr
    bc_w = SSM_GROUPS * D_STATE

    def body(xc_ref, x0, x1, x2, dxs_ref, db_ref, dc_ref, w_ref, dx_ref, dw_ref, dbias_ref, ahead):
        @pl.when(pl.program_id(0) == 0)
        def _():
            dw_ref[...] = jnp.zeros_like(dw_ref)
            dbias_ref[...] = jnp.zeros_like(dbias_ref)

        c = xc_ref[...]
        s = _sigmoid(c)
        dact = jnp.concatenate([dxs_ref[...], db_ref[...], dc_ref[...]], axis=1)
        dpre = dact * s * (1.0 + c * (1.0 - s))
        x = jnp.concatenate([x0[...], x1[...], x2[...]], axis=1).astype(F32)
        dx = None
        for sh, moved in enumerate(_anticausal_taps(dpre, ahead, pl.program_id(0) == 0, SSM_CONV)):
            k = SSM_CONV - 1 - sh
            term = w_ref[k:k + 1, :] * moved
            dx = term if dx is None else dx + term
            dw_ref[k:k + 1, :] += jnp.sum(moved * x, axis=0, keepdims=True)
        dbias_ref[...] += jnp.sum(dpre, axis=0, keepdims=True)
        dx_ref[...] = dx.astype(BF16)

    return pl.pallas_call(
        body, grid=(nt,),
        in_specs=[_row_rev(tr, CONV_DIM, nt)] + _xbc_specs(tr, nt)
        + [_row_rev(tr, D_INNER, nt), _row_rev(tr, bc_w, nt), _row_rev(tr, bc_w, nt), _full((SSM_CONV, CONV_DIM))],
        out_specs=[_row_rev(tr, CONV_DIM, nt), _full((SSM_CONV, CONV_DIM)), _full((1, CONV_DIM))],
        out_shape=[_sds((t_rows, CONV_DIM), BF16), _sds((SSM_CONV, CONV_DIM), F32), _sds((1, CONV_DIM), F32)],
        scratch_shapes=[pltpu.VMEM((HALO, CONV_DIM), F32)],
        name="ssm_conv_bwd", compiler_params=_params(1))(xc, proj, proj, proj, dxs, dbm, dcm, conv_w)


def _prenorm_bwd(x, meta, dhn, dh, w):
    t_rows = dhn.shape[0]
    tr = CHUNK

    def body(x_ref, meta_ref, d_ref, r_ref, w_ref, dx_ref, dmeta_ref, dw_ref):
        i = pl.program_id(0)

        @pl.when(i == 0)
        def _():
            dw_ref[...] = jnp.zeros_like(dw_ref)

        h = _seq_tile(x_ref, meta_ref, i)
        dx, dw = _rms_bwd(h, _rms(h), w_ref[...], d_ref[...])
        dw_ref[...] += dw
        dh_tile = r_ref[...] + dx
        dx_ref[...] = dh_tile

        @pl.when(i == 0)
        def _():
            dmeta_ref[...] = dh_tile[META_PAD:, :]

    return pl.pallas_call(
        body, grid=(t_rows // tr,), in_specs=_seq_specs() + [_row(tr, D_MODEL)] * 2 + [_full((1, D_MODEL))],
        out_specs=[pl.BlockSpec((tr, D_MODEL), lambda i: (jnp.maximum(i - 1, 0), 0)), _full((N_META, D_MODEL)),
                   _full((1, D_MODEL))],
        out_shape=[_sds((t_rows - tr, D_MODEL), F32), _sds((N_META, D_MODEL), F32), _sds((1, D_MODEL), F32)],
        name="prenorm_bwd", compiler_params=_params(1))(x, meta, dhn, dh, w)


def _dot01(x, m01, x_left, parts):
    acc, rest = None, x
    for i in range(parts):
        piece = rest.astype(BF16)
        term = (jnp.dot(piece, m01, preferred_element_type=F32) if x_left
                else jnp.dot(m01, piece, preferred_element_type=F32))
        acc = term if acc is None else acc + term
        if i + 1 < parts:
            rest = rest - piece.astype(F32)
    return acc


def _ssd_common(dt_raw, dt_bias, a_log, chunk_index):
    rows = lax.broadcasted_iota(jnp.int32, (CHUNK, CHUNK), 0)
    cols = lax.broadcasted_iota(jnp.int32, (CHUNK, CHUNK), 1)
    low = rows >= cols
    raw = dt_raw + dt_bias
    live = _row_ids(raw.shape, chunk_index, CHUNK) >= META_PAD
    dt = jnp.where(live, _softplus(raw), 0.0)
    a_head = -jnp.exp(a_log)
    cs = _dot01(dt * a_head, low.astype(BF16), False, 3)
    grow = jnp.exp(cs)
    fade = jnp.exp(cs[CHUNK - 1:CHUNK, :] - cs)
    expand = (lax.broadcasted_iota(jnp.int32, (CHUNK, GROUP_W), 1) // HEAD_P
              == lax.broadcasted_iota(jnp.int32, (CHUNK, GROUP_W), 0)).astype(BF16)
    fold = (lax.broadcasted_iota(jnp.int32, (GROUP_W, CHUNK), 0) // HEAD_P
            == lax.broadcasted_iota(jnp.int32, (GROUP_W, CHUNK), 1)).astype(BF16)
    return dict(low=low, triu=(rows <= cols).astype(BF16), raw=raw, live=live, dt=dt, a_head=a_head, cs=cs, cs_t=cs.T,
                fold=fold, dtx=_dot01(dt, expand, True, 2), growx=_dot01(grow, expand, True, 2),
                fadex=_dot01(fade, expand, True, 2))


def _decay_matrix(cm, j):
    diff = cm["cs"][:, j:j + 1] - cm["cs_t"][j:j + 1, :]
    return jnp.where(cm["low"], jnp.exp(jnp.where(cm["low"], diff, 0.0)), 0.0)


def _dot(a, b, dims):
    return lax.dot_general(a.astype(BF16), b.astype(BF16), (dims, ((), ())), preferred_element_type=F32)


def _dot_fine(a, b, dims):
    a_hi, b_hi = a.astype(BF16), b.astype(BF16)
    a_lo, b_lo = (a - a_hi.astype(F32)).astype(BF16), (b - b_hi.astype(F32)).astype(BF16)
    dn = (dims, ((), ()))
    return (lax.dot_general(a_hi, b_hi, dn, preferred_element_type=F32)
            + lax.dot_general(a_hi, b_lo, dn, preferred_element_type=F32)
            + lax.dot_general(a_lo, b_hi, dn, preferred_element_type=F32))


def _ssd_specs(nt, rev):
    def idx(c):
        return nt - 1 - c if rev else c
    bc_w = SSM_GROUPS * D_STATE
    xs = pl.BlockSpec((CHUNK, D_INNER), lambda c: (idx(c), 0))
    bm = pl.BlockSpec((CHUNK, bc_w), lambda c: (idx(c), D_INNER // bc_w))
    cm = pl.BlockSpec((CHUNK, bc_w), lambda c: (idx(c), D_INNER // bc_w + 1))
    dtr = pl.BlockSpec((CHUNK, SSM_GROUPS * 128), lambda c: (idx(c), OFF_DT // (SSM_GROUPS * 128)))
    par = _full((SSM_GROUPS, 1, 128))
    par_x = _full((SSM_GROUPS, 1, GROUP_W))
    return xs, bm, cm, dtr, par, par_x, idx


def _group_cols(g, width):
    return slice(g * width, (g + 1) * width)


def _ssd_fwd(xact, proj, dtb, alog, dskip_x):
    t_rows = xact.shape[0]
    nt = t_rows // CHUNK
    xs_spec, b_spec, c_spec, dtr_spec, par, par_x, _ = _ssd_specs(nt, False)

    def body(xs_ref, b_ref, c_ref, dtr_ref, dtb_ref, alog_ref, dsk_ref, y_ref, hst_ref, state):
        c = pl.program_id(0)

        @pl.when(c == 0)
        def _():
            state[...] = jnp.zeros_like(state)

        for g in range(SSM_GROUPS):
            wide, narrow = _group_cols(g, GROUP_W), _group_cols(g, D_STATE)
            cm = _ssd_common(dtr_ref[:, narrow], dtb_ref[g], alog_ref[g], c)
            xs, bm, cmat = xs_ref[:, wide], b_ref[:, narrow], c_ref[:, narrow]
            x_dt = xs * cm["dtx"]
            h_in = state[g]
            hst_ref[0, g] = h_in
            y_ref[:, wide] = _dot(cmat, h_in, ((1,), (0,))) * cm["growx"] + xs * dsk_ref[g]
            cb = _dot(cmat, bm, ((1,), (1,)))
            for j in range(HEADS_PER_GROUP):
                sl = slice(g * GROUP_W + j * HEAD_P, g * GROUP_W + (j + 1) * HEAD_P)
                y_ref[:, sl] += _dot(cb * _decay_matrix(cm, j), x_dt[:, j * HEAD_P:(j + 1) * HEAD_P], ((1,), (0,)))
            state[g] = h_in * cm["growx"][CHUNK - 1:CHUNK, :] + _dot_fine(bm, x_dt * cm["fadex"], ((0,), (0,)))

    return pl.pallas_call(
        body, grid=(nt,),
        in_specs=[xs_spec, b_spec, c_spec, dtr_spec, par, par, par_x],
        out_specs=[xs_spec, pl.BlockSpec((1, SSM_GROUPS, D_STATE, GROUP_W), lambda c: (c, 0, 0, 0))],
        out_shape=[_sds((t_rows, D_INNER), F32), _sds((nt, SSM_GROUPS, D_STATE, GROUP_W), F32)],
        scratch_shapes=[pltpu.VMEM((SSM_GROUPS, D_STATE, GROUP_W), F32)],
        name="ssd_fwd", compiler_params=_params(1))(xact, xact, xact, proj, dtb, alog, dskip_x)


def _ssd_bwd(xact, proj, dtb, alog, dskip_x, dy, hst):
    t_rows = xact.shape[0]
    nt = t_rows // CHUNK
    xs_spec, b_spec, c_spec, dtr_spec, par, par_x, idx = _ssd_specs(nt, True)
    h_spec = pl.BlockSpec((1, SSM_GROUPS, D_STATE, GROUP_W), lambda c: (idx(c), 0, 0, 0))
    hn_spec = pl.BlockSpec((1, SSM_GROUPS, D_STATE, GROUP_W), lambda c: (jnp.minimum(idx(c) + 1, nt - 1), 0, 0, 0))
    bc_out = pl.BlockSpec((CHUNK, SSM_GROUPS * D_STATE), lambda c: (idx(c), 0))

    def body(xs_ref, b_ref, c_ref, dtr_ref, dtb_ref, alog_ref, dsk_ref, dy_ref, h_ref, hn_ref,
             dxs_ref, db_ref, dc_ref, ddt_ref, dalog_ref, ddtb_ref, dd_ref, dstate, dx_buf):
        step = pl.program_id(0)

        @pl.when(step == 0)
        def _():
            dstate[...] = jnp.zeros_like(dstate)
            dalog_ref[...] = jnp.zeros_like(dalog_ref)
            ddtb_ref[...] = jnp.zeros_like(ddtb_ref)
            dd_ref[...] = jnp.zeros_like(dd_ref)

        for g in range(SSM_GROUPS):
            _ssd_bwd_group(g, idx(step), xs_ref, b_ref, c_ref, dtr_ref, dtb_ref, alog_ref, dsk_ref, dy_ref, h_ref, hn_ref,
                           dxs_ref, db_ref, dc_ref, ddt_ref, dalog_ref, ddtb_ref, dd_ref, dstate, dx_buf)

    return pl.pallas_call(
        body, grid=(nt,),
        in_specs=[xs_spec, b_spec, c_spec, dtr_spec, par, par, par_x, xs_spec, h_spec, hn_spec],
        out_specs=[xs_spec, bc_out, bc_out, bc_out, par, par, par_x],
        out_shape=[_sds((t_rows, D_INNER), F32), _sds((t_rows, SSM_GROUPS * D_STATE), F32),
                   _sds((t_rows, SSM_GROUPS * D_STATE), F32), _sds((t_rows, SSM_GROUPS * 128), BF16),
                   _sds((SSM_GROUPS, 1, 128), F32), _sds((SSM_GROUPS, 1, 128), F32), _sds((SSM_GROUPS, 1, GROUP_W), F32)],
        scratch_shapes=[pltpu.VMEM((SSM_GROUPS, D_STATE, GROUP_W), F32), pltpu.VMEM((CHUNK, GROUP_W), F32)],
        name="ssd_bwd", compiler_params=_params(1))(xact, xact, xact, proj, dtb, alog, dskip_x, dy, hst, hst)


def _ssd_bwd_group(g, chunk, xs_ref, b_ref, c_ref, dtr_ref, dtb_ref, alog_ref, dsk_ref, dy_ref, h_ref, hn_ref,
                   dxs_ref, db_ref, dc_ref, ddt_ref, dalog_ref, ddtb_ref, dd_ref, dstate, dx_buf):
    if True:
        wide, narrow = _group_cols(g, GROUP_W), _group_cols(g, D_STATE)
        cm = _ssd_common(dtr_ref[:, narrow], dtb_ref[g], alog_ref[g], chunk)
        xs, bm, cmat = xs_ref[:, wide], b_ref[:, narrow], c_ref[:, narrow]
        dsk = dsk_ref[g]
        x_dt = xs * cm["dtx"]
        h_in, h_next = h_ref[0, g], hn_ref[0, g]
        dyv = dy_ref[:, wide]
        dh = dstate[g]
        grow, fade = cm["growx"], cm["fadex"]
        dy_grow = dyv * grow
        x_fade = x_dt * fade
        cb = _dot(cmat, bm, ((1,), (1,)))
        ml = jnp.zeros((CHUNK, CHUNK), F32)
        row_id = lax.broadcasted_iota(jnp.int32, (CHUNK, CHUNK), 0)
        col_id = lax.broadcasted_iota(jnp.int32, (CHUNK, CHUNK), 1)
        w_rows = jnp.zeros((CHUNK, CHUNK), F32)
        w_cols = jnp.zeros((CHUNK, CHUNK), F32)
        for j in range(HEADS_PER_GROUP):
            sl = slice(j * HEAD_P, (j + 1) * HEAD_P)
            lm = _decay_matrix(cm, j)
            mlj = _dot(dyv[:, sl], x_dt[:, sl], ((1,), (1,))) * lm
            ml = ml + mlj
            wm = mlj * cb
            w_rows = jnp.where(col_id == j, jnp.sum(wm, axis=1, keepdims=True), w_rows)
            w_cols = jnp.where(row_id == j, jnp.sum(wm, axis=0, keepdims=True), w_cols)
            dx_buf[:, sl] = _dot(cb * lm, dyv[:, sl], ((0,), (0,)))
        dx_off = fade * _dot_fine(bm, dh, ((1,), (0,)))
        dx = dx_buf[...] + dx_off
        dc_ref[:, narrow] = _dot(ml, bm, ((1,), (0,))) + _dot(dy_grow, h_in, ((1,), (1,)))
        db_ref[:, narrow] = _dot(ml, cmat, ((0,), (0,))) + _dot(x_fade, dh, ((1,), (1,)))
        fold = cm["fold"]
        y_off = _dot_fine(cmat, h_in, ((1,), (0,))) * grow
        dcs = (w_rows - w_cols.T) + _dot01(dyv * y_off - x_dt * dx_off, fold, True, 2)
        tail = jnp.broadcast_to(jnp.sum(dh * h_next, axis=0, keepdims=True), (8, GROUP_W))
        tail = _dot01(tail, fold, True, 2)[0:1, :]
        last_row = lax.broadcasted_iota(jnp.int32, (CHUNK, 128), 0) == CHUNK - 1
        dcs = dcs + jnp.where(last_row, tail, 0.0)
        da = _dot01(dcs, cm["triu"], False, 3)
        ddt = da * cm["a_head"] + _dot01(dx * xs, fold, True, 2)
        ddt_raw = jnp.where(cm["live"], ddt * _sigmoid(cm["raw"]), 0.0)
        ddt_ref[:, narrow] = ddt_raw.astype(BF16)
        ddtb_ref[g] += jnp.sum(ddt_raw, axis=0, keepdims=True)
        dalog_ref[g] += jnp.sum(da * cm["dt"], axis=0, keepdims=True) * cm["a_head"]
        dd_ref[g] += jnp.sum(dyv * xs, axis=0, keepdims=True)
        dxs_ref[:, wide] = dx * cm["dtx"] + dyv * dsk
        dstate[g] = dh * grow[CHUNK - 1:CHUNK, :] + _dot_fine(cmat, dy_grow, ((0,), (0,)))


def _swa_bias():
    rows_q = ATTN_GROUP * CHUNK
    dist = (jnp.arange(rows_q) % CHUNK)[:, None] - jnp.arange(2 * CHUNK)[None, :] + CHUNK
    head = jnp.arange(KV_HEADS)[:, None] * ATTN_GROUP + jnp.arange(rows_q)[None, :] // CHUNK + 1
    slope = jnp.exp2(-8.0 * head.astype(F32) / ATTN_HEADS)
    return jnp.where((dist >= 0) & (dist < CHUNK), -slope[:, :, None] * dist.astype(F32)[None], NEG)


def _swa_probs(q_kv, k_prev, k_cur, k_first, sink, bias, n):
    rows_q = ATTN_GROUP * CHUNK
    qs = jnp.concatenate([q_kv[:, g * DH:(g + 1) * DH] for g in range(ATTN_GROUP)], axis=0) * (DH ** -0.5)
    kcat = jnp.concatenate([k_prev, k_cur], axis=0)
    kmeta = k_first[META_PAD:, :]
    key_ok = lax.broadcasted_iota(jnp.int32, (1, 2 * CHUNK), 1) + n * CHUNK >= 2 * CHUNK
    s_band = jnp.where(key_ok, _dot(qs, kcat, ((1,), (1,))) + bias, NEG)
    q_pos = lax.broadcasted_iota(jnp.int32, (rows_q, N_META), 0) % CHUNK + n * CHUNK - META_PAD
    ok_m = lax.broadcasted_iota(jnp.int32, (rows_q, N_META), 1) <= q_pos
    s_meta = jnp.where(ok_m, _dot(qs, kmeta, ((1,), (1,))), NEG)
    m = jnp.maximum(jnp.maximum(jnp.max(s_band, axis=1, keepdims=True), jnp.max(s_meta, axis=1, keepdims=True)), sink)
    p_band, p_meta, p_sink = jnp.exp(s_band - m), jnp.exp(s_meta - m), jnp.exp(sink - m)
    inv = 1.0 / (jnp.sum(p_band, axis=1, keepdims=True) + jnp.sum(p_meta, axis=1, keepdims=True) + p_sink)
    return qs, kcat, kmeta, p_band * inv, p_meta * inv, p_sink * inv


def _swa_specs(nt, rev):
    def idx(n):
        return nt - 1 - n if rev else n
    width = ATTN_HEADS * DH
    o = pl.BlockSpec((CHUNK, width), lambda n: (idx(n), 0))
    q_proj = pl.BlockSpec((CHUNK, width), lambda n: (idx(n), OFF_Q // width))
    def kv(col0, chunk_of):
        return pl.BlockSpec((CHUNK, KV_W), lambda n: (chunk_of(idx(n)), col0 // KV_W))

    chunks = (lambda c: jnp.maximum(c - 1, 0)), (lambda c: c), (lambda c: 0)
    k_specs = [kv(OFF_K, f) for f in chunks]
    v_specs = [kv(OFF_V, f) for f in chunks]
    dkv = pl.BlockSpec((CHUNK, KV_W), lambda n: (idx(n), 0))
    sink = _full((KV_HEADS, ATTN_GROUP * CHUNK, 1))
    bias = _full((KV_HEADS, ATTN_GROUP * CHUNK, 2 * CHUNK))
    return o, q_proj, k_specs, v_specs, dkv, sink, bias, idx


def _swa_fwd(proj, sink_rows, bias):
    t_rows = proj.shape[0]
    nt = t_rows // CHUNK
    o_spec, q_spec, k_specs, v_specs, _, sink_spec, bias_spec, _ = _swa_specs(nt, False)
    kv_w = ATTN_GROUP * DH

    def body(q_ref, kp_ref, kc_ref, km_ref, vp_ref, vc_ref, vm_ref, sink_ref, bias_ref, o_ref):
        n = pl.program_id(0)
        for k in range(KV_HEADS):
            hd = slice(k * DH, (k + 1) * DH)
            _, _, _, p_band, p_meta, _ = _swa_probs(q_ref[:, k * kv_w:(k + 1) * kv_w], kp_ref[:, hd], kc_ref[:, hd],
                                                    km_ref[:, hd], sink_ref[k], bias_ref[k], n)
            vcat = jnp.concatenate([vp_ref[:, hd], vc_ref[:, hd]], axis=0)
            out = _dot(p_band, vcat, ((1,), (0,))) + _dot(p_meta, vm_ref[:, hd][META_PAD:, :], ((1,), (0,)))
            for g in range(ATTN_GROUP):
                o_ref[:, k * kv_w + g * DH:k * kv_w + (g + 1) * DH] = out[g * CHUNK:(g + 1) * CHUNK, :]

    return pl.pallas_call(
        body, grid=(nt,), in_specs=[q_spec] + k_specs + v_specs + [sink_spec, bias_spec],
        out_specs=o_spec, out_shape=_sds((t_rows, ATTN_HEADS * DH), F32),
        name="swa_fwd", compiler_params=_params(1))(proj, proj, proj, proj, proj, proj, proj, sink_rows, bias)


def _swa_bwd(proj, sink_rows, bias, out, dout):
    t_rows = proj.shape[0]
    nt = t_rows // CHUNK
    o_spec, q_spec, k_specs, v_specs, dkv_spec, sink_spec, bias_spec, idx = _swa_specs(nt, True)
    kv_w = ATTN_GROUP * DH

    def body(q_ref, kp_ref, kc_ref, km_ref, vp_ref, vc_ref, vm_ref, sink_ref, bias_ref, o_ref, do_ref,
             dq_ref, dk_ref, dv_ref, dsink_ref, carry_k, carry_v, meta_k, meta_v, dk_buf, dv_buf, dq_buf):
        step = pl.program_id(0)
        n = idx(step)

        @pl.when(step == 0)
        def _():
            carry_k[...] = jnp.zeros_like(carry_k)
            carry_v[...] = jnp.zeros_like(carry_v)
            meta_k[...] = jnp.zeros_like(meta_k)
            meta_v[...] = jnp.zeros_like(meta_v)
            dsink_ref[...] = jnp.zeros_like(dsink_ref)

        for k in range(KV_HEADS):
            cols = slice(k * kv_w, (k + 1) * kv_w)
            hd = slice(k * DH, (k + 1) * DH)
            qs, kcat, kmeta, p_band, p_meta, p_sink = _swa_probs(q_ref[:, cols], kp_ref[:, hd], kc_ref[:, hd],
                                                                 km_ref[:, hd], sink_ref[k], bias_ref[k], n)
            vcat = jnp.concatenate([vp_ref[:, hd], vc_ref[:, hd]], axis=0)
            vmeta = vm_ref[:, hd][META_PAD:, :]
            o, do = o_ref[:, cols], do_ref[:, cols]
            os_ = jnp.concatenate([o[:, g * DH:(g + 1) * DH] for g in range(ATTN_GROUP)], axis=0)
            dos = jnp.concatenate([do[:, g * DH:(g + 1) * DH] for g in range(ATTN_GROUP)], axis=0)
            delta = jnp.sum(dos * os_, axis=1, keepdims=True)
            ds_band = p_band * (_dot(dos, vcat, ((1,), (1,))) - delta)
            ds_meta = p_meta * (_dot(dos, vmeta, ((1,), (1,))) - delta)
            ds_sink = -p_sink * delta
            dqs = (_dot(ds_band, kcat, ((1,), (0,))) + _dot(ds_meta, kmeta, ((1,), (0,)))) * (DH ** -0.5)
            for g in range(ATTN_GROUP):
                dq_buf[:, k * kv_w + g * DH:k * kv_w + (g + 1) * DH] = dqs[g * CHUNK:(g + 1) * CHUNK, :]
                dsink_ref[k, g:g + 1, :] += jnp.sum(ds_sink[g * CHUNK:(g + 1) * CHUNK, :])
            dkcat = _dot(ds_band, qs, ((0,), (0,)))
            dvcat = _dot(p_band, dos, ((0,), (0,)))
            meta_k[:, hd] += _dot(ds_meta, qs, ((0,), (0,)))
            meta_v[:, hd] += _dot(p_meta, dos, ((0,), (0,)))
            dk_buf[:, hd] = dkcat[CHUNK:, :] + carry_k[:, hd]
            dv_buf[:, hd] = dvcat[CHUNK:, :] + carry_v[:, hd]
            carry_k[:, hd] = dkcat[:CHUNK, :]
            carry_v[:, hd] = dvcat[:CHUNK, :]

        @pl.when(n == 0)
        def _():
            dk_buf[META_PAD:, :] += meta_k[...]
            dv_buf[META_PAD:, :] += meta_v[...]

        dq_ref[...] = dq_buf[...].astype(BF16)
        dk_ref[...] = dk_buf[...].astype(BF16)
        dv_ref[...] = dv_buf[...].astype(BF16)

    return pl.pallas_call(
        body, grid=(nt,),
        in_specs=[q_spec] + k_specs + v_specs + [sink_spec, bias_spec, o_spec, o_spec],
        out_specs=[o_spec, dkv_spec, dkv_spec, _full((KV_HEADS, 8, 128))],
        out_shape=[_sds((t_rows, ATTN_HEADS * DH), BF16), _sds((t_rows, KV_W), BF16),
                   _sds((t_rows, KV_W), BF16), _sds((KV_HEADS, 8, 128), F32)],
        scratch_shapes=[pltpu.VMEM((CHUNK, KV_W), F32), pltpu.VMEM((CHUNK, KV_W), F32),
                        pltpu.VMEM((N_META, KV_W), F32), pltpu.VMEM((N_META, KV_W), F32),
                        pltpu.VMEM((CHUNK, KV_W), F32), pltpu.VMEM((CHUNK, KV_W), F32),
                        pltpu.VMEM((CHUNK, ATTN_HEADS * DH), F32)],
        name="swa_bwd", compiler_params=_params(1))(proj, proj, proj, proj, proj, proj, proj, sink_rows, bias, out, dout)


def _pack_w_in_t(w_in_t):
    w_dt = w_in_t[CUT_DT:CUT_Q].reshape(SSM_GROUPS, HEADS_PER_GROUP, D_MODEL)
    w_dt = jnp.pad(w_dt, ((0, 0), (0, 128 - HEADS_PER_GROUP), (0, 0))).reshape(SSM_GROUPS * 128, D_MODEL)
    return jnp.concatenate([w_in_t[CUT_Z:CUT_XBC], w_in_t[CUT_G:], w_in_t[CUT_XBC:CUT_DT], w_in_t[CUT_Q:CUT_K],
                            w_in_t[CUT_K:CUT_V], w_in_t[CUT_V:CUT_G], w_dt], axis=0)


def _unpack_w_in_t(wp_t):
    w_dt = wp_t[OFF_DT:].reshape(SSM_GROUPS, 128, D_MODEL)[:, :HEADS_PER_GROUP].reshape(SSM_HEADS, D_MODEL)
    return jnp.concatenate([wp_t[OFF_Z:OFF_GATE], wp_t[OFF_XBC:OFF_Q], w_dt, wp_t[OFF_Q:OFF_K], wp_t[OFF_K:OFF_V],
                            wp_t[OFF_V:OFF_DT], wp_t[OFF_GATE:OFF_XBC]], axis=0)


def _group_rows(v, width):
    return jnp.pad(v.reshape(SSM_GROUPS, 1, HEADS_PER_GROUP), ((0, 0), (0, 0), (0, width - HEADS_PER_GROUP)))


def _local_step(x, target, wt, late_weights=None, on_grad=None, started=None):
    seq = x.shape[0]
    grads = {}

    def emit(name, g):
        grads[name] = g
        return None if on_grad is None else on_grad(name, g)
    meta = wt["meta_tokens"]
    wp_t = _pack_w_in_t(wt["w_in_t"])
    dtb = _group_rows(wt["ssm_dt_bias"].reshape(-1), 128)
    alog = _group_rows(wt["ssm_a_log"].reshape(-1), 128)
    dskip_x = jnp.repeat(wt["ssm_d_skip"].reshape(-1), HEAD_P).reshape(SSM_GROUPS, 1, GROUP_W)
    sink_rows = jnp.repeat(wt["attn_sinks"].reshape(KV_HEADS, ATTN_GROUP), CHUNK, axis=1).reshape(KV_HEADS, ATTN_GROUP * CHUNK, 1)

    hn = _prenorm(x, meta, wt["norm_pre_mix"])
    proj = _matmul(hn, wp_t, tb=True, name="in_proj", after=started)
    xc, xact = _ssm_conv_fwd(proj, wt["ssm_conv_w"], wt["ssm_conv_b"])
    y, hst = _ssd_fwd(xact, proj, dtb, alog, dskip_x)
    yn = _ssm_post(y, proj, wt["ssm_norm"])
    if late_weights is not None:
        wt = {**wt, **late_weights(yn)}
    y_ssm = _matmul(yn, wt["w_ssm_out"], name="ssm_out")
    bias = _swa_bias()
    attn = _swa_fwd(proj, sink_rows, bias)
    y_attn = _matmul(attn, wt["w_attn_out"], name="attn_out")
    mixed = _mix_fwd(proj, y_ssm, y_attn)
    mix = _matmul(mixed, wt["w_mix_out"], name="mix_out")
    h1, hn2 = _postmix(x, meta, mix, wt["norm_post_mix"], wt["norm_pre_ffn"])
    up = _matmul(hn2, wt["w_ffn_up_t"], tb=True, out_dtype=BF16, name="ffn_up")
    u, act = _ffn_act(up, wt["ffn_conv_w"], wt["ffn_conv_b"])
    f = _matmul(act, wt["w_ffn_down"], name="ffn_down")
    df, dy, g_norm_post_ffn, loss_row = _final(h1, f, target, wt["norm_post_ffn"])

    grads["norm_post_ffn"] = g_norm_post_ffn
    sent = emit("w_ffn_down", _matmul(act, df, ta=True, out_dtype=BF16, name="dw_ffn_down"))
    dact = _matmul(df, wt["w_ffn_down"], tb=True, out_dtype=BF16, name="d_act", after=sent)
    dup, grads["ffn_conv_w"], grads["ffn_conv_b"] = _ffn_act_bwd(u, up, dact, wt["ffn_conv_w"])
    sent = emit("w_ffn_up_t", _matmul(dup, hn2, ta=True, out_dtype=BF16, name="dw_ffn_up"))
    dhn2 = _matmul(dup, wt["w_ffn_up_t"], name="d_hn2", after=sent)
    dmix, dh, grads["norm_pre_ffn"], grads["norm_post_mix"] = _postmix_bwd(h1, dhn2, dy, mix, wt["norm_pre_ffn"], wt["norm_post_mix"])
    sent = emit("w_mix_out", _matmul(mixed, dmix, ta=True, out_dtype=BF16, name="dw_mix_out"))
    dmixed = _matmul(dmix, wt["w_mix_out"], tb=True, name="d_mixed", after=sent)
    dy_ssm, dy_attn, dglog = _mix_bwd(dmixed, proj, y_ssm, y_attn)
    sent = emit("w_ssm_out", _matmul(yn, dy_ssm, ta=True, out_dtype=BF16, name="dw_ssm_out"))
    dyn = _matmul(dy_ssm, wt["w_ssm_out"], tb=True, out_dtype=BF16, name="d_yn", after=sent)
    sent = emit("w_attn_out", _matmul(attn, dy_attn, ta=True, out_dtype=BF16, name="dw_attn_out"))
    dattn = _matmul(dy_attn, wt["w_attn_out"], tb=True, name="d_attn", after=sent)
    dy_ssd, dz, grads["ssm_norm"] = _ssm_post_bwd(y, proj, dyn, wt["ssm_norm"])
    dxs, dbm, dcm, ddt, dalog, ddtb, dd_x = _ssd_bwd(xact, proj, dtb, alog, dskip_x, dy_ssd, hst)
    grads["ssm_a_log"] = dalog[:, 0, :HEADS_PER_GROUP].reshape(1, SSM_HEADS)
    grads["ssm_dt_bias"] = ddtb[:, 0, :HEADS_PER_GROUP].reshape(1, SSM_HEADS)
    grads["ssm_d_skip"] = dd_x.reshape(SSM_HEADS, HEAD_P).sum(axis=1).reshape(1, SSM_HEADS)
    dxbc, grads["ssm_conv_w"], grads["ssm_conv_b"] = _ssm_conv_bwd(xc, proj, dxs, dbm, dcm, wt["ssm_conv_w"])
    dq, dk, dv, dsink = _swa_bwd(proj, sink_rows, bias, attn, dattn)
    grads["attn_sinks"] = dsink[:, :ATTN_GROUP, 0].reshape(1, ATTN_HEADS)
    dproj = jnp.concatenate([dz, dglog, dxbc, dq, dk, dv, ddt], axis=1)
    sent = emit("w_in_t", _unpack_w_in_t(_matmul(dproj, hn, ta=True, out_dtype=BF16, name="dw_in")))
    dhn = _matmul(dproj, wp_t, name="d_hn", after=sent)
    grad_x, grads["meta_tokens"], grads["norm_pre_mix"] = _prenorm_bwd(x, meta, dhn, dh, wt["norm_pre_mix"])
    return loss_row[0, 0], grad_x, grads


def _all_gather(shards):
    n = len(shards)

    def body(*refs):
        ins, outs = refs[:n], refs[n:2 * n]
        send_sems, recv_sems, local_sems = refs[2 * n:]
        x, y, c = lax.axis_index("x"), lax.axis_index("y"), lax.axis_index("c")
        me, sibling = (x, y, c), (x, y, 1 - c)
        chips = [(1 - x, y), (x, 1 - y), (1 - x, 1 - y)]

        def slot(a, dev):
            return outs[a].at[4 * dev[0] + 2 * dev[1] + dev[2]]

        def copy(k, a, block, to, src=None):
            return pltpu.make_async_remote_copy(
                src_ref=slot(a, block) if src is None else src, dst_ref=slot(a, block),
                send_sem=send_sems.at[k, a], recv_sem=recv_sems.at[k, a],
                device_id=to, device_id_type=pl.DeviceIdType.MESH)

        mine = [pltpu.make_async_copy(ins[a], slot(a, me), local_sems.at[a]) for a in range(n)]
        for cp in mine:
            cp.start()
        first = [copy(0, a, me, sibling, src=ins[a]) for a in range(n)]
        for j, chip in enumerate(chips):
            first += [copy(1 + j, a, me, (*chip, c), src=ins[a]) for a in range(n)]
        for cp in first:
            cp.start()
        passed = []
        for j, chip in enumerate(chips):
            for a in range(n):
                copy(1 + j, a, (*chip, c), me).wait_recv()
                fwd = copy(4 + j, a, (*chip, c), sibling)
                fwd.start()
                passed.append(fwd)
        for a in range(n):
            copy(0, a, sibling, me).wait_recv()
        for j, chip in enumerate(chips):
            for a in range(n):
                copy(4 + j, a, (*chip, 1 - c), me).wait_recv()
        for cp in first + passed:
            cp.wait_send()
        for cp in mine:
            cp.wait()

    hbm = pl.BlockSpec(memory_space=pl.ANY)
    return pl.pallas_call(
        body, in_specs=[hbm] * n, out_specs=[hbm] * n,
        out_shape=[_sds((N_DEV,) + s.shape, s.dtype) for s in shards],
        scratch_shapes=[pltpu.SemaphoreType.DMA((7, n)), pltpu.SemaphoreType.DMA((7, n)), pltpu.SemaphoreType.DMA((n,))],
        name="gather_weights")(*shards)


def _peer_table():
    x, y, c = lax.axis_index("x"), lax.axis_index("y"), lax.axis_index("c")
    peers = []
    for k in range(N_DEV - 1):
        bits = k + 1
        p = (x ^ ((bits >> 2) & 1), y ^ ((bits >> 1) & 1), c ^ (bits & 1))
        peers.append((k, p, 4 * p[0] + 2 * p[1] + p[2]))
    return 4 * x + 2 * y + c, peers


_HBM = pl.BlockSpec(memory_space=pltpu.HBM)
_SEM = pl.BlockSpec(memory_space=pltpu.SEMAPHORE)
_EFFECT = pltpu.SideEffectType.DATAFLOW_SIDE_EFFECTING


def _push_copy(src, land, send_sems, recv_sems, a, k, p, src_slot, dst_slot):
    sem = a * (N_DEV - 1) + k
    return pltpu.make_async_remote_copy(
        src_ref=src[a] if src_slot is None else src[a].at[src_slot], dst_ref=land[a].at[dst_slot],
        send_sem=send_sems.at[sem], recv_sem=recv_sems.at[sem], device_id=p, device_id_type=pl.DeviceIdType.MESH)


def _push_start(srcs, scatter, name):
    n = len(srcs)
    lands = [lax.empty(s.shape if scatter else (N_DEV,) + s.shape, s.dtype) for s in srcs]

    def body(*refs):
        src, land = refs[:n], refs[n:2 * n]
        send_sems, recv_sems, token = refs[2 * n], refs[2 * n + 1], refs[-1]
        my_id, peers = _peer_table()
        for a in range(n):
            for k, p, p_id in peers:
                _push_copy(src, land, send_sems, recv_sems, a, k, p, p_id if scatter else None, my_id).start()
        token[...] = jnp.zeros_like(token)

    sems = pltpu.SemaphoreType.DMA(((N_DEV - 1) * n,))
    res = pl.pallas_call(
        body, name=name,
        out_shape=(sems, sems, *[pltpu.HBM(a.shape, a.dtype) for a in srcs + lands], _sds((8, 128), F32)),
        in_specs=[_HBM] * (2 * n), out_specs=(_SEM, _SEM, *[_HBM] * (2 * n), pl.BlockSpec(memory_space=pltpu.VMEM)),
        input_output_aliases={i: 2 + i for i in range(2 * n)},
        compiler_params=pltpu.CompilerParams(has_side_effects=_EFFECT),
    )(*[pltpu.with_memory_space_constraint(a, pltpu.HBM) for a in srcs + lands])
    return dict(send=res[0], recv=res[1], src=list(res[2:2 + n]), land=list(res[2 + n:2 + 2 * n]), token=res[-1],
                scatter=scatter)


def _push_wait(handle, after, name):
    n = len(handle["src"])
    scatter = handle["scatter"]

    def body(*refs):
        src, land = refs[:n], refs[n:2 * n]
        send_sems, recv_sems = refs[2 * n], refs[2 * n + 1]
        _, peers = _peer_table()
        for a in range(n):
            for k, p, p_id in peers:
                cp = _push_copy(src, land, send_sems, recv_sems, a, k, p, p_id if scatter else None, p_id)
                cp.wait_send()
                cp.wait_recv()

    arrays = handle["src"] + handle["land"]
    res = pl.pallas_call(
        body, name=name, out_shape=tuple(pltpu.HBM(a.shape, a.dtype) for a in arrays),
        in_specs=[_HBM] * (2 * n) + [_SEM, _SEM, pl.BlockSpec(memory_space=pl.ANY)], out_specs=tuple([_HBM] * (2 * n)),
        input_output_aliases={i: i for i in range(2 * n)},
        compiler_params=pltpu.CompilerParams(has_side_effects=_EFFECT),
    )(*arrays, handle["send"], handle["recv"], after)
    return list(res[:n]), list(res[n:])


def _slot_sum(p_ref, own_ref):
    if own_ref is not None:
        my_id = 4 * lax.axis_index("x") + 2 * lax.axis_index("y") + lax.axis_index("c")
        mine = own_ref[...].astype(F32)
    g = None
    for s in range(p_ref.shape[0]):
        term = p_ref[s].astype(F32)
        if own_ref is not None:
            term = jnp.where(my_id == s, mine, term)
        g = term if g is None else g + term
    return g


def _to_bf16(arrays):
    n = len(arrays)

    def body(*refs):
        for i in range(n):
            refs[n + i][...] = refs[i][...].astype(BF16)

    return pl.pallas_call(body, out_shape=[_sds(a.shape, BF16) for a in arrays], name="weights_to_bf16",
                          compiler_params=pltpu.CompilerParams(vmem_limit_bytes=VMEM_LIMIT))(*arrays)


def _adamw(parts, own, w, m, v, name):
    rows, cols = w.shape
    if rows % 16 == 0:
        tr, tc = _pick(rows, (256, 128, 176, 64, 32, 16)), cols
    else:
        tr, tc = rows, _pick(cols, (256, 128))

    def body(*refs):
        if own is None:
            p_ref, w_ref, m_ref, v_ref, g_ref, d_ref, nm_ref, nv_ref = refs
            own_ref = None
        else:
            p_ref, own_ref, w_ref, m_ref, v_ref, g_ref, d_ref, nm_ref, nv_ref = refs
        g = _slot_sum(p_ref, own_ref)
        m_new = ADAM_B1 * m_ref[...] + (1.0 - ADAM_B1) * g
        v_new = ADAM_B2 * v_ref[...] + (1.0 - ADAM_B2) * (g * g)
        m_hat = m_new / (1.0 - ADAM_B1 ** ADAM_STEP)
        v_hat = v_new / (1.0 - ADAM_B2 ** ADAM_STEP)
        g_ref[...] = g
        d_ref[...] = -ADAM_LR * (m_hat / (jnp.sqrt(v_hat) + ADAM_EPS) + ADAM_WD * w_ref[...])
        nm_ref[...] = m_new
        nv_ref[...] = v_new

    by_rows = tc == cols
    spec = pl.BlockSpec((tr, tc), (lambda i: (i, 0)) if by_rows else (lambda i: (0, i)))
    parts_spec = pl.BlockSpec((parts.shape[0], tr, tc), (lambda i: (0, i, 0)) if by_rows else (lambda i: (0, 0, i)))
    operands = (parts, w, m, v) if own is None else (parts, own, w, m, v)
    return pl.pallas_call(
        body, grid=(rows // tr if by_rows else cols // tc,),
        in_specs=[parts_spec] + [spec] * (len(operands) - 1),
        out_specs=[spec] * 4, out_shape=[_sds((rows, cols), F32)] * 4,
        name=name, compiler_params=_params(1))(*operands)


SMALL_REPLICATED = (("norm_pre_mix", 1024), ("ssm_conv_b", 3072), ("ssm_dt_bias", 32), ("ssm_a_log", 32),
                    ("ssm_d_skip", 32), ("ssm_norm", 2048), ("attn_sinks", 16), ("norm_post_mix", 1024),
                    ("norm_pre_ffn", 1024), ("ffn_conv_b", 5632), ("norm_post_ffn", 1024))
SMALL_SHARDED = (("meta_tokens", (N_META, D_MODEL // N_DEV)), ("ssm_conv_w", (SSM_CONV, CONV_DIM // N_DEV)),
                 ("ffn_conv_w", (FFN_CONV, 2 * FFN_DIM // N_DEV)))
BIG = (("w_in", (D_MODEL, N_IN // N_DEV), 1), ("w_ssm_out", (D_INNER // N_DEV, D_MODEL), 0),
       ("w_attn_out", (D_MODEL // N_DEV, D_MODEL), 0), ("w_mix_out", (D_MODEL // N_DEV, D_MODEL), 0),
       ("w_ffn_up", (D_MODEL, 2 * FFN_DIM // N_DEV), 1), ("w_ffn_down", (FFN_DIM // N_DEV, D_MODEL), 0))


def _rows_of(size):
    return -(-size // 128)


def _as_rows(flat):
    size = flat.shape[-1]
    rows = _rows_of(size)
    flat = jnp.pad(flat, [(0, 0)] * (flat.ndim - 1) + [(0, rows * 128 - size)])
    return flat.reshape(flat.shape[:-1] + (rows, 128))


def _pack_small(rep, sharded):
    pieces = [_as_rows(rep[name].reshape(-1)) for name, _ in SMALL_REPLICATED]
    pieces += [_as_rows(sharded[name].reshape(-1)) for name, _ in SMALL_SHARDED]
    packed = jnp.concatenate(pieces, axis=0)
    return jnp.pad(packed, ((0, -packed.shape[0] % 8), (0, 0)))


def _unpack_small(packed):
    out, row = {}, 0
    for name, size in SMALL_REPLICATED:
        out[name] = packed[row:row + _rows_of(size)].reshape(-1)[:size].reshape(1, size)
        row += _rows_of(size)
    for name, (r, c) in SMALL_SHARDED:
        out[name] = packed[row:row + _rows_of(r * c)].reshape(-1)[:r * c].reshape(r, c)
        row += _rows_of(r * c)
    return out


def _shard_major(g, shape, axis):
    r, c = shape
    if axis == 0:
        return g.reshape(N_DEV, r, c)
    return g.reshape(r, N_DEV, c).transpose(1, 0, 2)


def kernel(x, meta_tokens, norm_pre_mix, w_in, ssm_conv_w, ssm_conv_b, ssm_dt_bias, ssm_a_log, ssm_d_skip, ssm_norm, w_ssm_out, attn_sinks, w_attn_out, w_mix_out, norm_post_mix, norm_pre_ffn, w_ffn_up, ffn_conv_w, ffn_conv_b, w_ffn_down, norm_post_ffn, loss_target, m_meta_tokens, m_norm_pre_mix, m_w_in, m_ssm_conv_w, m_ssm_conv_b, m_ssm_dt_bias, m_ssm_a_log, m_ssm_d_skip, m_ssm_norm, m_w_ssm_out, m_attn_sinks, m_w_attn_out, m_w_mix_out, m_norm_post_mix, m_norm_pre_ffn, m_w_ffn_up, m_ffn_conv_w, m_ffn_conv_b, m_w_ffn_down, m_norm_post_ffn, v_meta_tokens, v_norm_pre_mix, v_w_in, v_ssm_conv_w, v_ssm_conv_b, v_ssm_dt_bias, v_ssm_a_log, v_ssm_d_skip, v_ssm_norm, v_w_ssm_out, v_attn_sinks, v_w_attn_out, v_w_mix_out, v_norm_post_mix, v_norm_pre_ffn, v_w_ffn_up, v_ffn_conv_w, v_ffn_conv_b, v_w_ffn_down, v_norm_post_ffn):
    names = ("meta_tokens", "norm_pre_mix", "w_in", "ssm_conv_w", "ssm_conv_b", "ssm_dt_bias", "ssm_a_log", "ssm_d_skip",
             "ssm_norm", "w_ssm_out", "attn_sinks", "w_attn_out", "w_mix_out", "norm_post_mix", "norm_pre_ffn", "w_ffn_up",
             "ffn_conv_w", "ffn_conv_b", "w_ffn_down", "norm_post_ffn")
    w_loc = dict(zip(names, (meta_tokens, norm_pre_mix, w_in, ssm_conv_w, ssm_conv_b, ssm_dt_bias, ssm_a_log, ssm_d_skip,
                             ssm_norm, w_ssm_out, attn_sinks, w_attn_out, w_mix_out, norm_post_mix, norm_pre_ffn, w_ffn_up,
                             ffn_conv_w, ffn_conv_b, w_ffn_down, norm_post_ffn)))
    m_loc = dict(zip(names, (m_meta_tokens, m_norm_pre_mix, m_w_in, m_ssm_conv_w, m_ssm_conv_b, m_ssm_dt_bias, m_ssm_a_log,
                             m_ssm_d_skip, m_ssm_norm, m_w_ssm_out, m_attn_sinks, m_w_attn_out, m_w_mix_out, m_norm_post_mix,
                             m_norm_pre_ffn, m_w_ffn_up, m_ffn_conv_w, m_ffn_conv_b, m_w_ffn_down, m_norm_post_ffn)))
    v_loc = dict(zip(names, (v_meta_tokens, v_norm_pre_mix, v_w_in, v_ssm_conv_w, v_ssm_conv_b, v_ssm_dt_bias, v_ssm_a_log,
                             v_ssm_d_skip, v_ssm_norm, v_w_ssm_out, v_attn_sinks, v_w_attn_out, v_w_mix_out, v_norm_post_mix,
                             v_norm_pre_ffn, v_w_ffn_up, v_ffn_conv_w, v_ffn_conv_b, v_w_ffn_down, v_norm_post_ffn)))

    def local2d(d, name):
        a = d[name]
        return a if name == "meta_tokens" else a.reshape(a.shape[1:])

    def turned2d(d, name):
        a = jnp.swapaxes(d[name], 1, 2)
        return a.reshape(a.shape[1:])

    my_id = 4 * lax.axis_index("x") + 2 * lax.axis_index("y") + lax.axis_index("c")
    big = {name: (shape, axis) for name, shape, axis in BIG}

    def whole(name, g):
        return g.reshape(N_DEV * g.shape[1], g.shape[2])

    def key(name):
        return name + "_t" if big[name][1] == 1 else name

    by_rows = [name for name, _, axis in BIG if axis == 0]
    send_bf16 = dict(zip(by_rows, _to_bf16([local2d(w_loc, name) for name in by_rows])))
    for name, _, axis in BIG:
        if axis == 1:
            send_bf16[name] = turned2d(w_loc, name).astype(BF16)
    small_shard_pack = jnp.concatenate([_as_rows(local2d(w_loc, name).reshape(-1)) for name, _ in SMALL_SHARDED], axis=0)
    small_shard_pack = jnp.pad(small_shard_pack, ((0, -small_shard_pack.shape[0] % 8), (0, 0)))
    first = _all_gather([send_bf16["w_in"], small_shard_pack])
    rest_names = [name for name, _, _ in BIG if name != "w_in"]
    rest = [send_bf16[name] for name in rest_names]
    rest, first = lax.optimization_barrier((rest, first))
    rest_handle = _push_start(rest, False, "gather_rest_start")
    wt = {"w_in_t": whole("w_in", first[0])}
    row = 0
    for name, (r, c) in SMALL_SHARDED:
        blocks = first[1][:, row:row + _rows_of(r * c)].reshape(N_DEV, -1)[:, :r * c].reshape(N_DEV, r, c)
        wt[name] = blocks.transpose(1, 0, 2).reshape(r, N_DEV * c)
        row += _rows_of(r * c)
    for name, size in SMALL_REPLICATED:
        wt[name] = w_loc[name].reshape(1, size)

    def late_weights(after):
        own, landed = _push_wait(rest_handle, after, "gather_rest_wait")
        out = {}
        for name, mine, land in zip(rest_names, own, landed):
            out[key(name)] = whole(name, lax.dynamic_update_index_in_dim(land, mine, my_id, 0))
        return out

    sent = {}

    def on_grad(known_as, g):
        name = known_as.removesuffix("_t")
        by_owner = g.reshape(N_DEV, g.shape[0] // N_DEV, g.shape[1])
        sent[name] = _push_start([by_owner], True, "send_" + name)
        return sent[name]["token"]

    loss_part, grad_x, grads = _local_step(x[0], loss_target[0], wt, late_weights, on_grad, rest_handle["token"])
    loss = lax.psum(loss_part, AXES)

    small_parts = []
    for name, (r, c) in SMALL_SHARDED:
        small_parts.append(_as_rows(_shard_major(grads[name], (r, c), 1).reshape(N_DEV, r * c)))
    rep_rows = jnp.concatenate([_as_rows(grads[name].reshape(-1)) for name, _ in SMALL_REPLICATED], axis=0)
    small_send = jnp.concatenate([jnp.broadcast_to(rep_rows[None], (N_DEV,) + rep_rows.shape)] + small_parts, axis=1)
    small_send = jnp.pad(small_send, ((0, 0), (0, -small_send.shape[1] % 8), (0, 0)))
    small_handle = _push_start([small_send], True, "send_small")

    def small_pack(d):
        return _pack_small({name: d[name] for name, _ in SMALL_REPLICATED}, {name: local2d(d, name) for name, _ in SMALL_SHARDED})

    def arrived(handle, after, name):
        src, landed = _push_wait(handle, after, "arrived_" + name)
        return landed[0], lax.dynamic_index_in_dim(src[0], my_id, 0, keepdims=False)

    grad_w, delta_w, new_m, new_v = {}, {}, {}, {}
    outs = None
    after = small_handle["token"]
    for name, handle in sent.items():
        if name == "w_in":
            parts, own = arrived(small_handle, after, "small")
            outs = _adamw(parts, own, small_pack(w_loc), small_pack(m_loc), small_pack(v_loc), "adamw_small")
            after = outs[0]
        parts, own = arrived(handle, after, name)
        turned = big[name][1] == 1
        state = [turned2d(d, name) if turned else local2d(d, name) for d in (w_loc, m_loc, v_loc)]
        results = _adamw(parts, own, *state, "adamw_" + name)
        after = results[0]
        full = (1,) + big[name][0]
        for dst, a in zip((grad_w, delta_w, new_m, new_v), results):
            dst[name] = jnp.swapaxes(a[None], 1, 2) if turned else a.reshape(full)
    for dst, packed in zip((grad_w, delta_w, new_m, new_v), outs):
        for name, a in _unpack_small(packed).items():
            dst[name] = a.reshape(w_loc[name].shape)

    return (loss, grad_x[None], *[grad_w[n] for n in names], *[delta_w[n] for n in names],
            *[new_m[n] for n in names], *[new_v[n] for n in names])
```

```python
import jax
import jax.numpy as jnp
from jax import lax
from jax.experimental import pallas as pl
from jax.experimental.pallas import tpu as pltpu

F32 = jnp.float32
BF16 = jnp.bfloat16

D_MODEL = 1024
N_META = 16
CHUNK = 128
META_PAD = CHUNK - N_META
D_INNER = 2048
HEAD_P = 64
SSM_HEADS = 32
SSM_GROUPS = 4
HEADS_PER_GROUP = SSM_HEADS // SSM_GROUPS
GROUP_W = HEADS_PER_GROUP * HEAD_P
D_STATE = 128
SSM_CONV = 4
CONV_DIM = D_INNER + 2 * SSM_GROUPS * D_STATE
ATTN_HEADS = 16
KV_HEADS = 4
ATTN_GROUP = ATTN_HEADS // KV_HEADS
DH = 64
KV_W = KV_HEADS * DH
FFN_DIM = 2816
FFN_CONV = 3
EPS = 1e-6
NEG = -1e30
N_DEV = 8
AXES = ("x", "y", "c")

OFF_Z, OFF_GATE, OFF_DT, OFF_Q, OFF_K, OFF_V, OFF_XBC = 0, 2048, 4096, 4608, 5632, 5888, 6144
N_INP = OFF_XBC + CONV_DIM
QKV_W = OFF_XBC - OFF_Q
CUT_Z, CUT_XBC, CUT_DT, CUT_Q, CUT_K, CUT_V, CUT_G = 0, 2048, 5120, 5152, 6176, 6432, 6688
N_IN = 8736

ADAM_LR, ADAM_B1, ADAM_B2, ADAM_EPS, ADAM_WD, ADAM_STEP = 0.001, 0.9, 0.999, 1e-08, 0.01, 10

VMEM_LIMIT = 56 * 1024 * 1024


def _params(n_grid):
    return pltpu.CompilerParams(dimension_semantics=("arbitrary",) * n_grid, vmem_limit_bytes=VMEM_LIMIT)


def _sds(shape, dtype):
    return jax.ShapeDtypeStruct(shape, dtype)


def _pick(n, prefs):
    for c in prefs:
        if n % c == 0:
            return c
    raise ValueError(f"no tile of {prefs} divides {n}")


def _row(tr, width, cb=0):
    return pl.BlockSpec((tr, width), lambda i: (i, cb))


def _row_rev(tr, width, nt, cb=0):
    return pl.BlockSpec((tr, width), lambda i: (nt - 1 - i, cb))


def _full(shape):
    return pl.BlockSpec(shape, lambda *_: (0,) * len(shape))


def _sigmoid(x):
    return 1.0 / (1.0 + jnp.exp(-x))


def _softplus(x):
    return jnp.maximum(x, 0.0) + jnp.log(1.0 + jnp.exp(-jnp.abs(x)))


def _rms(x):
    return lax.rsqrt(jnp.mean(x * x, axis=-1, keepdims=True) + EPS)


def _rms_bwd(x, r, w, dy):
    xh = x * r
    g = dy * w
    dx = r * (g - xh * jnp.mean(g * xh, axis=-1, keepdims=True))
    return dx, jnp.sum(dy * xh, axis=0, keepdims=True)


def _row_ids(shape, tile_index, tr):
    return tile_index * tr + lax.broadcasted_iota(jnp.int32, shape, 0)


HALO = 8


def _causal_taps(x, halo, first_step, taps):
    n = x.shape[0]

    @pl.when(first_step)
    def _():
        halo[...] = jnp.zeros_like(halo)

    before = halo[...]
    row = lax.broadcasted_iota(jnp.int32, before.shape, 0)
    shifted = [x]
    for s in range(1, taps):
        rolled = pltpu.roll(x, s, 0)
        head = jnp.where(row < s, pltpu.roll(before, s, 0), rolled[0:HALO, :])
        shifted.append(jnp.concatenate([head, rolled[HALO:, :]], axis=0))
    halo[...] = x[n - HALO:, :]
    return shifted


def _anticausal_taps(x, halo, first_step, taps):
    n = x.shape[0]

    @pl.when(first_step)
    def _():
        halo[...] = jnp.zeros_like(halo)

    after = halo[...]
    row = lax.broadcasted_iota(jnp.int32, after.shape, 0)
    shifted = [x]
    for s in range(1, taps):
        rolled = pltpu.roll(x, n - s, 0)
        tail = jnp.where(row >= HALO - s, pltpu.roll(after, HALO - s, 0), rolled[n - HALO:, :])
        shifted.append(jnp.concatenate([rolled[:n - HALO, :], tail], axis=0))
    halo[...] = x[0:HALO, :]
    return shifted


def _matmul(a, b, *, ta=False, tb=False, out_dtype=F32, name, after=None):
    if ta:
        k_dim, m_dim = a.shape
    else:
        m_dim, k_dim = a.shape
    n_dim = b.shape[0] if tb else b.shape[1]
    tm = _pick(m_dim, (1408, 1024, 768, 512, 384, 256, 128))
    tn = _pick(n_dim, (1024, 1408, 768, 512, 384, 256, 128))
    if ta:
        tk = _pick(k_dim, (1408, 1024, 768, 512, 384, 256, 128))
    else:
        tk = k_dim if k_dim <= 3072 else _pick(k_dim, (3072, 2816, 2048, 1024))
    nk = k_dim // tk
    dims = (((0 if ta else 1,), (1 if tb else 0,)), ((), ()))

    use_acc = nk > 1 and out_dtype != F32

    def body(a_ref, b_ref, *rest):
        o_ref = rest[-2] if use_acc else rest[-1]
        acc_ref = rest[-1] if use_acc else o_ref
        r = lax.dot_general(a_ref[...].astype(BF16), b_ref[...].astype(BF16), dims, preferred_element_type=F32)
        if nk == 1:
            o_ref[...] = r.astype(o_ref.dtype)
        else:
            k = pl.program_id(2)

            @pl.when(k == 0)
            def _():
                acc_ref[...] = r

            @pl.when(k > 0)
            def _():
                acc_ref[...] += r

            if use_acc:
                @pl.when(k == nk - 1)
                def _():
                    o_ref[...] = acc_ref[...].astype(o_ref.dtype)

    a_spec = pl.BlockSpec((tk, tm), lambda i, j, k: (k, i)) if ta else pl.BlockSpec((tm, tk), lambda i, j, k: (i, k))
    b_spec = pl.BlockSpec((tn, tk), lambda i, j, k: (j, k)) if tb else pl.BlockSpec((tk, tn), lambda i, j, k: (k, j))
    extra_specs, extra = ([], ()) if after is None else ([pl.BlockSpec(memory_space=pl.ANY)], (after,))
    return pl.pallas_call(
        body, grid=(m_dim // tm, n_dim // tn, nk), in_specs=[a_spec, b_spec] + extra_specs,
        out_specs=pl.BlockSpec((tm, tn), lambda i, j, k: (i, j)), out_shape=_sds((m_dim, n_dim), out_dtype),
        scratch_shapes=[pltpu.VMEM((tm, tn), F32)] if use_acc else [],
        name=name, compiler_params=_params(3))(a, b, *extra)


def _seq_specs():
    return [pl.BlockSpec((CHUNK, D_MODEL), lambda i: (jnp.maximum(i - 1, 0), 0)), _full((N_META, D_MODEL))]


def _seq_tile(x_ref, meta_ref, i):
    first = jnp.concatenate([jnp.zeros((META_PAD, D_MODEL), F32), meta_ref[...]], axis=0)
    return jnp.where(i == 0, first, x_ref[...])


def _prenorm(x, meta, w):
    t_rows = x.shape[0] + CHUNK

    def body(x_ref, meta_ref, w_ref, o_ref):
        h = _seq_tile(x_ref, meta_ref, pl.program_id(0))
        o_ref[...] = (h * _rms(h) * w_ref[...]).astype(BF16)

    return pl.pallas_call(body, grid=(t_rows // CHUNK,), in_specs=_seq_specs() + [_full((1, D_MODEL))],
                          out_specs=_row(CHUNK, D_MODEL), out_shape=_sds((t_rows, D_MODEL), BF16),
                          name="prenorm", compiler_params=_params(1))(x, meta, w)


def _xbc_specs(tr, rev_nt=None):
    cbs = [OFF_XBC // 1024 + j for j in range(CONV_DIM // 1024)]
    if rev_nt is None:
        return [_row(tr, 1024, cb) for cb in cbs]
    return [_row_rev(tr, 1024, rev_nt, cb) for cb in cbs]


def _ssm_conv_fwd(proj, conv_w, conv_b):
    t_rows = proj.shape[0]
    tr = CHUNK

    def body(x0, x1, x2, w_ref, b_ref, xc_ref, xa_ref, hist):
        x = jnp.concatenate([x0[...], x1[...], x2[...]], axis=1).astype(F32)
        acc = b_ref[...]
        for s, moved in enumerate(_causal_taps(x, hist, pl.program_id(0) == 0, SSM_CONV)):
            acc = acc + w_ref[SSM_CONV - 1 - s:SSM_CONV - s, :] * moved
        xc_ref[...] = acc
        xa_ref[...] = acc * _sigmoid(acc)

    return pl.pallas_call(
        body, grid=(t_rows // tr,),
        in_specs=_xbc_specs(tr) + [_full((SSM_CONV, CONV_DIM)), _full((1, CONV_DIM))],
        out_specs=[_row(tr, CONV_DIM), _row(tr, CONV_DIM)],
        out_shape=[_sds((t_rows, CONV_DIM), F32), _sds((t_rows, CONV_DIM), F32)],
        scratch_shapes=[pltpu.VMEM((HALO, CONV_DIM), F32)],
        name="ssm_conv_fwd", compiler_params=_params(1))(proj, proj, proj, conv_w, conv_b)


def _ssm_post(y, proj, w):
    t_rows = y.shape[0]
    tr = CHUNK

    def body(y_ref, z_ref, w_ref, o_ref):
        z = z_ref[...].astype(F32)
        yz = y_ref[...] * z * _sigmoid(z)
        o_ref[...] = (yz * _rms(yz) * w_ref[...]).astype(BF16)

    return pl.pallas_call(body, grid=(t_rows // tr,),
                          in_specs=[_row(tr, D_INNER), _row(tr, D_INNER, OFF_Z // D_INNER), _full((1, D_INNER))],
                          out_specs=_row(tr, D_INNER), out_shape=_sds((t_rows, D_INNER), BF16),
                          name="ssm_post", compiler_params=_params(1))(y, proj, w)


def _mix_fwd(proj, y_ssm, y_attn):
    t_rows = y_ssm.shape[0]
    tr = _pick(t_rows, (384, 128))

    def body(g_ref, ys_ref, ya_ref, o_ref):
        g = _sigmoid(g_ref[...].astype(F32))
        o_ref[...] = (g[:, :D_MODEL] * ys_ref[...] + g[:, D_MODEL:] * ya_ref[...]).astype(BF16)

    return pl.pallas_call(body, grid=(t_rows // tr,),
                          in_specs=[_row(tr, 2 * D_MODEL, OFF_GATE // (2 * D_MODEL)), _row(tr, D_MODEL), _row(tr, D_MODEL)],
                          out_specs=_row(tr, D_MODEL), out_shape=_sds((t_rows, D_MODEL), BF16),
                          name="mix_fwd", compiler_params=_params(1))(proj, y_ssm, y_attn)


def _postmix(x, meta, mix, w_post, w_pre):
    t_rows = mix.shape[0]
    tr = CHUNK

    def body(x_ref, meta_ref, m_ref, wp_ref, wf_ref, h1_ref, hn_ref):
        m = m_ref[...]
        h1 = _seq_tile(x_ref, meta_ref, pl.program_id(0)) + m * _rms(m) * wp_ref[...]
        h1 = jnp.where(_row_ids(h1.shape, pl.program_id(0), tr) >= META_PAD, h1, 0.0)
        h1_ref[...] = h1
        hn_ref[...] = (h1 * _rms(h1) * wf_ref[...]).astype(BF16)

    return pl.pallas_call(body, grid=(t_rows // tr,),
                          in_specs=_seq_specs() + [_row(tr, D_MODEL), _full((1, D_MODEL)), _full((1, D_MODEL))],
                          out_specs=[_row(tr, D_MODEL), _row(tr, D_MODEL)],
                          out_shape=[_sds((t_rows, D_MODEL), F32), _sds((t_rows, D_MODEL), BF16)],
                          name="postmix", compiler_params=_params(1))(x, meta, mix, w_post, w_pre)


def _ffn_act(up, conv_w, conv_b):
    t_rows = up.shape[0]
    tr = CHUNK
    width = 2 * FFN_DIM

    def body(up_ref, w_ref, b_ref, u_ref, act_ref, hist):
        x = up_ref[...].astype(F32)
        u = b_ref[...]
        for s, moved in enumerate(_causal_taps(x, hist, pl.program_id(0) == 0, FFN_CONV)):
            u = u + w_ref[FFN_CONV - 1 - s:FFN_CONV - s, :] * moved
        u_ref[...] = u.astype(BF16)
        a = u[:, :FFN_DIM]
        act_ref[...] = (a * _sigmoid(a) * u[:, FFN_DIM:]).astype(BF16)

    return pl.pallas_call(
        body, grid=(t_rows // tr,), in_specs=[_row(tr, width), _full((FFN_CONV, width)), _full((1, width))],
        out_specs=[_row(tr, width), _row(tr, FFN_DIM)],
        out_shape=[_sds((t_rows, width), BF16), _sds((t_rows, FFN_DIM), BF16)],
        scratch_shapes=[pltpu.VMEM((HALO, width), F32)],
        name="ffn_act", compiler_params=_params(1))(up, conv_w, conv_b)


def _final(h1, f, target, w):
    t_rows = h1.shape[0]
    tr = CHUNK

    def body(h1_ref, f_ref, t_ref, w_ref, df_ref, dy_ref, dw_ref, loss_ref):
        i = pl.program_id(0)

        @pl.when(i == 0)
        def _():
            dw_ref[...] = jnp.zeros_like(dw_ref)
            loss_ref[...] = jnp.zeros_like(loss_ref)

        f_val = f_ref[...]
        r = _rms(f_val)
        wv = w_ref[...]
        h2 = h1_ref[...] + f_val * r * wv
        diff = jnp.where(i >= 1, h2 - t_ref[...], 0.0)
        loss_ref[...] += 0.5 * jnp.sum(diff * diff) * (1.0 / D_MODEL)
        dy = diff * (1.0 / D_MODEL)
        dy_ref[...] = dy
        df, dw = _rms_bwd(f_val, r, wv, dy)
        df_ref[...] = df.astype(BF16)
        dw_ref[...] += dw

    tgt_spec = pl.BlockSpec((tr, D_MODEL), lambda i: (jnp.maximum(i - 1, 0), 0))
    return pl.pallas_call(
        body, grid=(t_rows // tr,),
        in_specs=[_row(tr, D_MODEL), _row(tr, D_MODEL), tgt_spec, _full((1, D_MODEL))],
        out_specs=[_row(tr, D_MODEL), _row(tr, D_MODEL), _full((1, D_MODEL)), _full((1, 128))],
        out_shape=[_sds((t_rows, D_MODEL), BF16), _sds((t_rows, D_MODEL), F32), _sds((1, D_MODEL), F32), _sds((1, 128), F32)],
        name="final", compiler_params=_params(1))(h1, f, target, w)


def _ffn_act_bwd(u, up, dact, conv_w):
    t_rows = u.shape[0]
    tr = CHUNK
    nt = t_rows // tr
    width = 2 * FFN_DIM

    def body(u_ref, up_ref, da_ref, w_ref, dup_ref, dw_ref, db_ref, ahead):
        @pl.when(pl.program_id(0) == 0)
        def _():
            dw_ref[...] = jnp.zeros_like(dw_ref)
            db_ref[...] = jnp.zeros_like(db_ref)

        u_val = u_ref[...].astype(F32)
        a, g = u_val[:, :FFN_DIM], u_val[:, FFN_DIM:]
        d = da_ref[...].astype(F32)
        s = _sigmoid(a)
        du = jnp.concatenate([d * g * s * (1.0 + a * (1.0 - s)), d * a * s], axis=1)
        x = up_ref[...].astype(F32)
        dup = None
        for sh, moved in enumerate(_anticausal_taps(du, ahead, pl.program_id(0) == 0, FFN_CONV)):
            k = FFN_CONV - 1 - sh
            term = w_ref[k:k + 1, :] * moved
            dup = term if dup is None else dup + term
            dw_ref[k:k + 1, :] += jnp.sum(moved * x, axis=0, keepdims=True)
        db_ref[...] += jnp.sum(du, axis=0, keepdims=True)
        dup_ref[...] = dup.astype(BF16)

    return pl.pallas_call(
        body, grid=(nt,),
        in_specs=[_row_rev(tr, width, nt), _row_rev(tr, width, nt), _row_rev(tr, FFN_DIM, nt), _full((FFN_CONV, width))],
        out_specs=[_row_rev(tr, width, nt), _full((FFN_CONV, width)), _full((1, width))],
        out_shape=[_sds((t_rows, width), BF16), _sds((FFN_CONV, width), F32), _sds((1, width), F32)],
        scratch_shapes=[pltpu.VMEM((HALO, width), F32)],
        name="ffn_act_bwd", compiler_params=_params(1))(u, up, dact, conv_w)


def _postmix_bwd(h1, dhn2, dy, mix, w_pre, w_post):
    t_rows = h1.shape[0]
    tr = CHUNK

    def body(h1_ref, dhn_ref, dy_ref, m_ref, wf_ref, wp_ref, dmix_ref, dh_ref, dwf_ref, dwp_ref):
        @pl.when(pl.program_id(0) == 0)
        def _():
            dwf_ref[...] = jnp.zeros_like(dwf_ref)
            dwp_ref[...] = jnp.zeros_like(dwp_ref)

        h1v = h1_ref[...]
        dx, dwf = _rms_bwd(h1v, _rms(h1v), wf_ref[...], dhn_ref[...])
        dwf_ref[...] += dwf
        dh1 = dy_ref[...] + dx
        dh1 = jnp.where(_row_ids(dh1.shape, pl.program_id(0), tr) >= META_PAD, dh1, 0.0)
        dh_ref[...] = dh1
        m = m_ref[...]
        dmix, dwp = _rms_bwd(m, _rms(m), wp_ref[...], dh1)
        dwp_ref[...] += dwp
        dmix_ref[...] = dmix.astype(BF16)

    return pl.pallas_call(
        body, grid=(t_rows // tr,),
        in_specs=[_row(tr, D_MODEL)] * 4 + [_full((1, D_MODEL))] * 2,
        out_specs=[_row(tr, D_MODEL), _row(tr, D_MODEL), _full((1, D_MODEL)), _full((1, D_MODEL))],
        out_shape=[_sds((t_rows, D_MODEL), BF16), _sds((t_rows, D_MODEL), F32), _sds((1, D_MODEL), F32), _sds((1, D_MODEL), F32)],
        name="postmix_bwd", compiler_params=_params(1))(h1, dhn2, dy, mix, w_pre, w_post)


_ANY = pl.BlockSpec(memory_space=pl.ANY)


def _mix_bwd(dmixed, proj, y_ssm, y_attn, dproj):
    t_rows = dmixed.shape[0]
    tr = _pick(t_rows, (384, 128))

    def body(d_ref, g_ref, ys_ref, ya_ref, _, dys_ref, dya_ref, dg_ref):
        d = d_ref[...]
        g = _sigmoid(g_ref[...].astype(F32))
        g1, g2 = g[:, :D_MODEL], g[:, D_MODEL:]
        dys_ref[...] = (d * g1).astype(BF16)
        dya_ref[...] = (d * g2).astype(BF16)
        dg_ref[...] = jnp.concatenate([d * ys_ref[...] * g1 * (1.0 - g1), d * ya_ref[...] * g2 * (1.0 - g2)],
                                      axis=1).astype(BF16)

    return pl.pallas_call(
        body, grid=(t_rows // tr,),
        in_specs=[_row(tr, D_MODEL), _row(tr, 2 * D_MODEL, OFF_GATE // (2 * D_MODEL)), _row(tr, D_MODEL), _row(tr, D_MODEL),
                  _ANY],
        out_specs=[_row(tr, D_MODEL), _row(tr, D_MODEL), _row(tr, 2 * D_MODEL, OFF_GATE // (2 * D_MODEL))],
        out_shape=[_sds((t_rows, D_MODEL), BF16), _sds((t_rows, D_MODEL), BF16), _sds(dproj.shape, dproj.dtype)],
        input_output_aliases={4: 2},
        name="mix_bwd", compiler_params=_params(1))(dmixed, proj, y_ssm, y_attn, dproj)


def _ssm_post_bwd(y, proj, dyn, w, dproj):
    t_rows = y.shape[0]
    tr = CHUNK

    def body(y_ref, z_ref, d_ref, w_ref, _, dy_ref, dz_ref, dw_ref):
        @pl.when(pl.program_id(0) == 0)
        def _():
            dw_ref[...] = jnp.zeros_like(dw_ref)

        yv, z = y_ref[...], z_ref[...].astype(F32)
        sz = _sigmoid(z)
        silu = z * sz
        yz = yv * silu
        dyz, dw = _rms_bwd(yz, _rms(yz), w_ref[...], d_ref[...].astype(F32))
        dw_ref[...] += dw
        dy_ref[...] = dyz * silu
        dz_ref[...] = (dyz * yv * sz * (1.0 + z * (1.0 - sz))).astype(BF16)

    return pl.pallas_call(
        body, grid=(t_rows // tr,),
        in_specs=[_row(tr, D_INNER), _row(tr, D_INNER, OFF_Z // D_INNER), _row(tr, D_INNER), _full((1, D_INNER)), _ANY],
        out_specs=[_row(tr, D_INNER), _row(tr, D_INNER, OFF_Z // D_INNER), _full((1, D_INNER))],
        out_shape=[_sds((t_rows, D_INNER), F32), _sds(dproj.shape, dproj.dtype), _sds((1, D_INNER), F32)],
        input_output_aliases={4: 1},
        name="ssm_post_bwd", compiler_params=_params(1))(y, proj, dyn, w, dproj)


def _ssm_conv_bwd(xc, proj, dxs, dbm, dcm, conv_w, dproj):
    t_rows = xc.shape[0]
    tr = CHUNK
    nt = t_rows // tr
    bc_w = SSM_GROUPS * D_STATE

    def body(xc_ref, x0, x1, x2, dxs_ref, db_ref, dc_ref, w_ref, _, dx_ref, dw_ref, dbias_ref, ahead):
        @pl.when(pl.program_id(0) == 0)
        def _():
            dw_ref[...] = jnp.zeros_like(dw_ref)
            dbias_ref[...] = jnp.zeros_like(dbias_ref)

        c = xc_ref[...]
        s = _sigmoid(c)
        dact = jnp.concatenate([dxs_ref[...], db_ref[...], dc_ref[...]], axis=1)
        dpre = dact * s * (1.0 + c * (1.0 - s))
        x = jnp.concatenate([x0[...], x1[...], x2[...]], axis=1).astype(F32)
        dx = None
        for sh, moved in enumerate(_anticausal_taps(dpre, ahead, pl.program_id(0) == 0, SSM_CONV)):
            k = SSM_CONV - 1 - sh
            term = w_ref[k:k + 1, :] * moved
            dx = term if dx is None else dx + term
            dw_ref[k:k + 1, :] += jnp.sum(moved * x, axis=0, keepdims=True)
        dbias_ref[...] += jnp.sum(dpre, axis=0, keepdims=True)
        dx_ref[...] = dx.astype(BF16)

    return pl.pallas_call(
        body, grid=(nt,),
        in_specs=[_row_rev(tr, CONV_DIM, nt)] + _xbc_specs(tr, nt)
        + [_row_rev(tr, D_INNER, nt), _row_rev(tr, bc_w, nt), _row_rev(tr, bc_w, nt), _full((SSM_CONV, CONV_DIM)), _ANY],
        out_specs=[_row_rev(tr, CONV_DIM, nt, OFF_XBC // CONV_DIM), _full((SSM_CONV, CONV_DIM)), _full((1, CONV_DIM))],
        out_shape=[_sds(dproj.shape, dproj.dtype), _sds((SSM_CONV, CONV_DIM), F32), _sds((1, CONV_DIM), F32)],
        scratch_shapes=[pltpu.VMEM((HALO, CONV_DIM), F32)],
        input_output_aliases={8: 0},
        name="ssm_conv_bwd", compiler_params=_params(1))(xc, proj, proj, proj, dxs, dbm, dcm, conv_w, dproj)


def _prenorm_bwd(x, meta, dhn, dh, w):
    t_rows = dhn.shape[0]
    tr = CHUNK

    def body(x_ref, meta_ref, d_ref, r_ref, w_ref, dx_ref, dmeta_ref, dw_ref):
        i = pl.program_id(0)

        @pl.when(i == 0)
        def _():
            dw_ref[...] = jnp.zeros_like(dw_ref)

        h = _seq_tile(x_ref, meta_ref, i)
        dx, dw = _rms_bwd(h, _rms(h), w_ref[...], d_ref[...])
        dw_ref[...] += dw
        dh_tile = r_ref[...] + dx
        dx_ref[...] = dh_tile

        @pl.when(i == 0)
        def _():
            dmeta_ref[...] = dh_tile[META_PAD:, :]

    return pl.pallas_call(
        body, grid=(t_rows // tr,), in_specs=_seq_specs() + [_row(tr, D_MODEL)] * 2 + [_full((1, D_MODEL))],
        out_specs=[pl.BlockSpec((tr, D_MODEL), lambda i: (jnp.maximum(i - 1, 0), 0)), _full((N_META, D_MODEL)),
                   _full((1, D_MODEL))],
        out_shape=[_sds((t_rows - tr, D_MODEL), F32), _sds((N_META, D_MODEL), F32), _sds((1, D_MODEL), F32)],
        name="prenorm_bwd", compiler_params=_params(1))(x, meta, dhn, dh, w)


def _dot01(x, m01, x_left, parts):
    acc, rest = None, x
    for i in range(parts):
        piece = rest.astype(BF16)
        term = (jnp.dot(piece, m01, preferred_element_type=F32) if x_left
                else jnp.dot(m01, piece, preferred_element_type=F32))
        acc = term if acc is None else acc + term
        if i + 1 < parts:
            rest = rest - piece.astype(F32)
    return acc


def _ssd_common(dt_raw, dt_bias, a_log, chunk_index):
    rows = lax.broadcasted_iota(jnp.int32, (CHUNK, CHUNK), 0)
    cols = lax.broadcasted_iota(jnp.int32, (CHUNK, CHUNK), 1)
    low = rows >= cols
    raw = dt_raw + dt_bias
    live = _row_ids(raw.shape, chunk_index, CHUNK) >= META_PAD
    dt = jnp.where(live, _softplus(raw), 0.0)
    a_head = -jnp.exp(a_log)
    cs = _dot01(dt * a_head, low.astype(BF16), False, 3)
    grow = jnp.exp(cs)
    fade = jnp.exp(cs[CHUNK - 1:CHUNK, :] - cs)
    expand = (lax.broadcasted_iota(jnp.int32, (CHUNK, GROUP_W), 1) // HEAD_P
              == lax.broadcasted_iota(jnp.int32, (CHUNK, GROUP_W), 0)).astype(BF16)
    fold = (lax.broadcasted_iota(jnp.int32, (GROUP_W, CHUNK), 0) // HEAD_P
            == lax.broadcasted_iota(jnp.int32, (GROUP_W, CHUNK), 1)).astype(BF16)
    return dict(low=low, triu=(rows <= cols).astype(BF16), raw=raw, live=live, dt=dt, a_head=a_head, cs=cs, cs_t=cs.T,
                fold=fold, dtx=_dot01(dt, expand, True, 2), growx=_dot01(grow, expand, True, 2),
                fadex=_dot01(fade, expand, True, 2))


def _decay_matrix(cm, j):
    diff = cm["cs"][:, j:j + 1] - cm["cs_t"][j:j + 1, :]
    return jnp.where(cm["low"], jnp.exp(jnp.where(cm["low"], diff, 0.0)), 0.0)


def _dot(a, b, dims):
    return lax.dot_general(a.astype(BF16), b.astype(BF16), (dims, ((), ())), preferred_element_type=F32)


def _dot_fine(a, b, dims):
    a_hi, b_hi = a.astype(BF16), b.astype(BF16)
    a_lo, b_lo = (a - a_hi.astype(F32)).astype(BF16), (b - b_hi.astype(F32)).astype(BF16)
    dn = (dims, ((), ()))
    return (lax.dot_general(a_hi, b_hi, dn, preferred_element_type=F32)
            + lax.dot_general(a_hi, b_lo, dn, preferred_element_type=F32)
            + lax.dot_general(a_lo, b_hi, dn, preferred_element_type=F32))


def _ssd_specs(nt, rev):
    def idx(c):
        return nt - 1 - c if rev else c
    bc_w = SSM_GROUPS * D_STATE
    xs = pl.BlockSpec((CHUNK, D_INNER), lambda c: (idx(c), 0))
    bm = pl.BlockSpec((CHUNK, bc_w), lambda c: (idx(c), D_INNER // bc_w))
    cm = pl.BlockSpec((CHUNK, bc_w), lambda c: (idx(c), D_INNER // bc_w + 1))
    dtr = pl.BlockSpec((CHUNK, SSM_GROUPS * 128), lambda c: (idx(c), OFF_DT // (SSM_GROUPS * 128)))
    par = _full((SSM_GROUPS, 1, 128))
    par_x = _full((SSM_GROUPS, 1, GROUP_W))
    return xs, bm, cm, dtr, par, par_x, idx


def _group_cols(g, width):
    return slice(g * width, (g + 1) * width)


def _ssd_fwd(xact, proj, dtb, alog, dskip_x):
    t_rows = xact.shape[0]
    nt = t_rows // CHUNK
    xs_spec, b_spec, c_spec, dtr_spec, par, par_x, _ = _ssd_specs(nt, False)

    def body(xs_ref, b_ref, c_ref, dtr_ref, dtb_ref, alog_ref, dsk_ref, y_ref, hst_ref, state):
        c = pl.program_id(0)

        @pl.when(c == 0)
        def _():
            state[...] = jnp.zeros_like(state)

        for g in range(SSM_GROUPS):
            wide, narrow = _group_cols(g, GROUP_W), _group_cols(g, D_STATE)
            cm = _ssd_common(dtr_ref[:, narrow], dtb_ref[g], alog_ref[g], c)
            xs, bm, cmat = xs_ref[:, wide], b_ref[:, narrow], c_ref[:, narrow]
            x_dt = xs * cm["dtx"]
            h_in = state[g]
            hst_ref[0, g] = h_in
            y_ref[:, wide] = _dot(cmat, h_in, ((1,), (0,))) * cm["growx"] + xs * dsk_ref[g]
            cb = _dot(cmat, bm, ((1,), (1,)))
            for j in range(HEADS_PER_GROUP):
                sl = slice(g * GROUP_W + j * HEAD_P, g * GROUP_W + (j + 1) * HEAD_P)
                y_ref[:, sl] += _dot(cb * _decay_matrix(cm, j), x_dt[:, j * HEAD_P:(j + 1) * HEAD_P], ((1,), (0,)))
            state[g] = h_in * cm["growx"][CHUNK - 1:CHUNK, :] + _dot_fine(bm, x_dt * cm["fadex"], ((0,), (0,)))

    return pl.pallas_call(
        body, grid=(nt,),
        in_specs=[xs_spec, b_spec, c_spec, dtr_spec, par, par, par_x],
        out_specs=[xs_spec, pl.BlockSpec((1, SSM_GROUPS, D_STATE, GROUP_W), lambda c: (c, 0, 0, 0))],
        out_shape=[_sds((t_rows, D_INNER), F32), _sds((nt, SSM_GROUPS, D_STATE, GROUP_W), F32)],
        scratch_shapes=[pltpu.VMEM((SSM_GROUPS, D_STATE, GROUP_W), F32)],
        name="ssd_fwd", compiler_params=_params(1))(xact, xact, xact, proj, dtb, alog, dskip_x)


def _ssd_bwd(xact, proj, dtb, alog, dskip_x, dy, hst, dproj):
    t_rows = xact.shape[0]
    nt = t_rows // CHUNK
    xs_spec, b_spec, c_spec, dtr_spec, par, par_x, idx = _ssd_specs(nt, True)
    h_spec = pl.BlockSpec((1, SSM_GROUPS, D_STATE, GROUP_W), lambda c: (idx(c), 0, 0, 0))
    hn_spec = pl.BlockSpec((1, SSM_GROUPS, D_STATE, GROUP_W), lambda c: (jnp.minimum(idx(c) + 1, nt - 1), 0, 0, 0))
    bc_out = pl.BlockSpec((CHUNK, SSM_GROUPS * D_STATE), lambda c: (idx(c), 0))

    def body(xs_ref, b_ref, c_ref, dtr_ref, dtb_ref, alog_ref, dsk_ref, dy_ref, h_ref, hn_ref, _,
             dxs_ref, db_ref, dc_ref, ddt_ref, dalog_ref, ddtb_ref, dd_ref, dstate, dx_buf):
        step = pl.program_id(0)

        @pl.when(step == 0)
        def _():
            dstate[...] = jnp.zeros_like(dstate)
            dalog_ref[...] = jnp.zeros_like(dalog_ref)
            ddtb_ref[...] = jnp.zeros_like(ddtb_ref)
            dd_ref[...] = jnp.zeros_like(dd_ref)

        for g in range(SSM_GROUPS):
            _ssd_bwd_group(g, idx(step), xs_ref, b_ref, c_ref, dtr_ref, dtb_ref, alog_ref, dsk_ref, dy_ref, h_ref, hn_ref,
                           dxs_ref, db_ref, dc_ref, ddt_ref, dalog_ref, ddtb_ref, dd_ref, dstate, dx_buf)

    return pl.pallas_call(
        body, grid=(nt,),
        in_specs=[xs_spec, b_spec, c_spec, dtr_spec, par, par, par_x, xs_spec, h_spec, hn_spec, _ANY],
        out_specs=[xs_spec, bc_out, bc_out, dtr_spec, par, par, par_x],
        out_shape=[_sds((t_rows, D_INNER), F32), _sds((t_rows, SSM_GROUPS * D_STATE), F32),
                   _sds((t_rows, SSM_GROUPS * D_STATE), F32), _sds(dproj.shape, dproj.dtype),
                   _sds((SSM_GROUPS, 1, 128), F32), _sds((SSM_GROUPS, 1, 128), F32), _sds((SSM_GROUPS, 1, GROUP_W), F32)],
        scratch_shapes=[pltpu.VMEM((SSM_GROUPS, D_STATE, GROUP_W), F32), pltpu.VMEM((CHUNK, GROUP_W), F32)],
        input_output_aliases={10: 3},
        name="ssd_bwd", compiler_params=_params(1))(xact, xact, xact, proj, dtb, alog, dskip_x, dy, hst, hst, dproj)


def _ssd_bwd_group(g, chunk, xs_ref, b_ref, c_ref, dtr_ref, dtb_ref, alog_ref, dsk_ref, dy_ref, h_ref, hn_ref,
                   dxs_ref, db_ref, dc_ref, ddt_ref, dalog_ref, ddtb_ref, dd_ref, dstate, dx_buf):
    wide, narrow = _group_cols(g, GROUP_W), _group_cols(g, D_STATE)
    cm = _ssd_common(dtr_ref[:, narrow], dtb_ref[g], alog_ref[g], chunk)
    xs, bm, cmat = xs_ref[:, wide], b_ref[:, narrow], c_ref[:, narrow]
    dsk = dsk_ref[g]
    x_dt = xs * cm["dtx"]
    h_in, h_next = h_ref[0, g], hn_ref[0, g]
    dyv = dy_ref[:, wide]
    dh = dstate[g]
    grow, fade = cm["growx"], cm["fadex"]
    dy_grow = dyv * grow
    x_fade = x_dt * fade
    cb = _dot(cmat, bm, ((1,), (1,)))
    ml = jnp.zeros((CHUNK, CHUNK), F32)
    row_id = lax.broadcasted_iota(jnp.int32, (CHUNK, CHUNK), 0)
    col_id = lax.broadcasted_iota(jnp.int32, (CHUNK, CHUNK), 1)
    w_rows = jnp.zeros((CHUNK, CHUNK), F32)
    w_cols = jnp.zeros((CHUNK, CHUNK), F32)
    for j in range(HEADS_PER_GROUP):
        sl = slice(j * HEAD_P, (j + 1) * HEAD_P)
        lm = _decay_matrix(cm, j)
        mlj = _dot(dyv[:, sl], x_dt[:, sl], ((1,), (1,))) * lm
        ml = ml + mlj
        wm = mlj * cb
        w_rows = jnp.where(col_id == j, jnp.sum(wm, axis=1, keepdims=True), w_rows)
        w_cols = jnp.where(row_id == j, jnp.sum(wm, axis=0, keepdims=True), w_cols)
        dx_buf[:, sl] = _dot(cb * lm, dyv[:, sl], ((0,), (0,)))
    dx_off = fade * _dot_fine(bm, dh, ((1,), (0,)))
    dx = dx_buf[...] + dx_off
    dc_ref[:, narrow] = _dot(ml, bm, ((1,), (0,))) + _dot(dy_grow, h_in, ((1,), (1,)))
    db_ref[:, narrow] = _dot(ml, cmat, ((0,), (0,))) + _dot(x_fade, dh, ((1,), (1,)))
    fold = cm["fold"]
    y_off = _dot_fine(cmat, h_in, ((1,), (0,))) * grow
    dcs = (w_rows - w_cols.T) + _dot01(dyv * y_off - x_dt * dx_off, fold, True, 2)
    tail = jnp.broadcast_to(jnp.sum(dh * h_next, axis=0, keepdims=True), (8, GROUP_W))
    tail = _dot01(tail, fold, True, 2)[0:1, :]
    last_row = lax.broadcasted_iota(jnp.int32, (CHUNK, 128), 0) == CHUNK - 1
    dcs = dcs + jnp.where(last_row, tail, 0.0)
    da = _dot01(dcs, cm["triu"], False, 3)
    ddt = da * cm["a_head"] + _dot01(dx * xs, fold, True, 2)
    ddt_raw = jnp.where(cm["live"], ddt * _sigmoid(cm["raw"]), 0.0)
    ddt_ref[:, narrow] = ddt_raw.astype(BF16)
    ddtb_ref[g] += jnp.sum(ddt_raw, axis=0, keepdims=True)
    dalog_ref[g] += jnp.sum(da * cm["dt"], axis=0, keepdims=True) * cm["a_head"]
    dd_ref[g] += jnp.sum(dyv * xs, axis=0, keepdims=True)
    dxs_ref[:, wide] = dx * cm["dtx"] + dyv * dsk
    dstate[g] = dh * grow[CHUNK - 1:CHUNK, :] + _dot_fine(cmat, dy_grow, ((0,), (0,)))


def _swa_bias():
    rows_q = ATTN_GROUP * CHUNK
    dist = (jnp.arange(rows_q) % CHUNK)[:, None] - jnp.arange(2 * CHUNK)[None, :] + CHUNK
    head = jnp.arange(KV_HEADS)[:, None] * ATTN_GROUP + jnp.arange(rows_q)[None, :] // CHUNK + 1
    slope = jnp.exp2(-8.0 * head.astype(F32) / ATTN_HEADS)
    return jnp.where((dist >= 0) & (dist < CHUNK), -slope[:, :, None] * dist.astype(F32)[None], NEG)


def _swa_probs(q_kv, k_prev, k_cur, k_first, sink, bias, n):
    rows_q = ATTN_GROUP * CHUNK
    qs = jnp.concatenate([q_kv[:, g * DH:(g + 1) * DH] for g in range(ATTN_GROUP)], axis=0) * (DH ** -0.5)
    kcat = jnp.concatenate([k_prev, k_cur], axis=0)
    kmeta = k_first[META_PAD:, :]
    key_ok = lax.broadcasted_iota(jnp.int32, (1, 2 * CHUNK), 1) + n * CHUNK >= 2 * CHUNK
    s_band = jnp.where(key_ok, _dot(qs, kcat, ((1,), (1,))) + bias, NEG)
    q_pos = lax.broadcasted_iota(jnp.int32, (rows_q, N_META), 0) % CHUNK + n * CHUNK - META_PAD
    ok_m = lax.broadcasted_iota(jnp.int32, (rows_q, N_META), 1) <= q_pos
    s_meta = jnp.where(ok_m, _dot(qs, kmeta, ((1,), (1,))), NEG)
    m = jnp.maximum(jnp.maximum(jnp.max(s_band, axis=1, keepdims=True), jnp.max(s_meta, axis=1, keepdims=True)), sink)
    p_band, p_meta, p_sink = jnp.exp(s_band - m), jnp.exp(s_meta - m), jnp.exp(sink - m)
    inv = 1.0 / (jnp.sum(p_band, axis=1, keepdims=True) + jnp.sum(p_meta, axis=1, keepdims=True) + p_sink)
    return qs, kcat, kmeta, p_band * inv, p_meta * inv, p_sink * inv


def _swa_specs(nt, rev):
    def idx(n):
        return nt - 1 - n if rev else n
    o = pl.BlockSpec((CHUNK, ATTN_HEADS * DH), lambda n: (idx(n), 0))
    chunks = (lambda c: jnp.maximum(c - 1, 0)), (lambda c: c), (lambda c: 0)
    qkv = [pl.BlockSpec((CHUNK, QKV_W), lambda n, f=f: (f(idx(n)), OFF_Q // QKV_W)) for f in chunks]
    sink = _full((KV_HEADS, ATTN_GROUP * CHUNK, 1))
    bias = _full((KV_HEADS, ATTN_GROUP * CHUNK, 2 * CHUNK))
    return o, qkv, sink, bias, idx


def _head_cols(k):
    kv_w = ATTN_GROUP * DH
    q0, k0, v0 = k * kv_w, OFF_K - OFF_Q + k * DH, OFF_V - OFF_Q + k * DH
    return slice(q0, q0 + kv_w), slice(k0, k0 + DH), slice(v0, v0 + DH)


def _swa_fwd(proj, sink_rows, bias):
    t_rows = proj.shape[0]
    nt = t_rows // CHUNK
    o_spec, qkv_specs, sink_spec, bias_spec, _ = _swa_specs(nt, False)
    kv_w = ATTN_GROUP * DH

    def body(prev_ref, cur_ref, first_ref, sink_ref, bias_ref, o_ref):
        n = pl.program_id(0)
        for k in range(KV_HEADS):
            qc, kc, vc = _head_cols(k)
            _, _, _, p_band, p_meta, _ = _swa_probs(cur_ref[:, qc], prev_ref[:, kc], cur_ref[:, kc], first_ref[:, kc],
                                                    sink_ref[k], bias_ref[k], n)
            vcat = jnp.concatenate([prev_ref[:, vc], cur_ref[:, vc]], axis=0)
            out = _dot(p_band, vcat, ((1,), (0,))) + _dot(p_meta, first_ref[:, vc][META_PAD:, :], ((1,), (0,)))
            for g in range(ATTN_GROUP):
                o_ref[:, k * kv_w + g * DH:k * kv_w + (g + 1) * DH] = out[g * CHUNK:(g + 1) * CHUNK, :]

    return pl.pallas_call(
        body, grid=(nt,), in_specs=qkv_specs + [sink_spec, bias_spec],
        out_specs=o_spec, out_shape=_sds((t_rows, ATTN_HEADS * DH), F32),
        name="swa_fwd", compiler_params=_params(1))(proj, proj, proj, sink_rows, bias)


def _swa_bwd(proj, sink_rows, bias, out, dout, dproj):
    t_rows = proj.shape[0]
    nt = t_rows // CHUNK
    o_spec, qkv_specs, sink_spec, bias_spec, idx = _swa_specs(nt, True)
    kv_w = ATTN_GROUP * DH
    k_off, v_off = OFF_K - OFF_Q, OFF_V - OFF_Q

    def body(prev_ref, cur_ref, first_ref, sink_ref, bias_ref, o_ref, do_ref, _, dqkv_ref, dsink_ref,
             carry_k, carry_v, meta_k, meta_v, dqkv_buf):
        step = pl.program_id(0)
        n = idx(step)

        @pl.when(step == 0)
        def _():
            carry_k[...] = jnp.zeros_like(carry_k)
            carry_v[...] = jnp.zeros_like(carry_v)
            meta_k[...] = jnp.zeros_like(meta_k)
            meta_v[...] = jnp.zeros_like(meta_v)
            dsink_ref[...] = jnp.zeros_like(dsink_ref)

        for k in range(KV_HEADS):
            cols = slice(k * kv_w, (k + 1) * kv_w)
            hd = slice(k * DH, (k + 1) * DH)
            qc, kc, vc = _head_cols(k)
            qs, kcat, kmeta, p_band, p_meta, p_sink = _swa_probs(cur_ref[:, qc], prev_ref[:, kc], cur_ref[:, kc],
                                                                 first_ref[:, kc], sink_ref[k], bias_ref[k], n)
            vcat = jnp.concatenate([prev_ref[:, vc], cur_ref[:, vc]], axis=0)
            vmeta = first_ref[:, vc][META_PAD:, :]
            o, do = o_ref[:, cols], do_ref[:, cols]
            os_ = jnp.concatenate([o[:, g * DH:(g + 1) * DH] for g in range(ATTN_GROUP)], axis=0)
            dos = jnp.concatenate([do[:, g * DH:(g + 1) * DH] for g in range(ATTN_GROUP)], axis=0)
            delta = jnp.sum(dos * os_, axis=1, keepdims=True)
            ds_band = p_band * (_dot(dos, vcat, ((1,), (1,))) - delta)
            ds_meta = p_meta * (_dot(dos, vmeta, ((1,), (1,))) - delta)
            ds_sink = -p_sink * delta
            dqs = (_dot(ds_band, kcat, ((1,), (0,))) + _dot(ds_meta, kmeta, ((1,), (0,)))) * (DH ** -0.5)
            for g in range(ATTN_GROUP):
                dqkv_buf[:, k * kv_w + g * DH:k * kv_w + (g + 1) * DH] = dqs[g * CHUNK:(g + 1) * CHUNK, :]
                dsink_ref[k, g:g + 1, :] += jnp.sum(ds_sink[g * CHUNK:(g + 1) * CHUNK, :])
            dkcat = _dot(ds_band, qs, ((0,), (0,)))
            dvcat = _dot(p_band, dos, ((0,), (0,)))
            meta_k[:, hd] += _dot(ds_meta, qs, ((0,), (0,)))
            meta_v[:, hd] += _dot(p_meta, dos, ((0,), (0,)))
            dqkv_buf[:, kc] = dkcat[CHUNK:, :] + carry_k[:, hd]
            dqkv_buf[:, vc] = dvcat[CHUNK:, :] + carry_v[:, hd]
            carry_k[:, hd] = dkcat[:CHUNK, :]
            carry_v[:, hd] = dvcat[:CHUNK, :]

        @pl.when(n == 0)
        def _():
            dqkv_buf[META_PAD:, k_off:k_off + KV_W] += meta_k[...]
            dqkv_buf[META_PAD:, v_off:v_off + KV_W] += meta_v[...]

        dqkv_ref[...] = dqkv_buf[...].astype(BF16)

    return pl.pallas_call(
        body, grid=(nt,),
        in_specs=qkv_specs + [sink_spec, bias_spec, o_spec, o_spec, pl.BlockSpec(memory_space=pl.ANY)],
        out_specs=[qkv_specs[1], _full((KV_HEADS, 8, 128))],
        out_shape=[_sds(dproj.shape, dproj.dtype), _sds((KV_HEADS, 8, 128), F32)],
        scratch_shapes=[pltpu.VMEM((CHUNK, KV_W), F32), pltpu.VMEM((CHUNK, KV_W), F32),
                        pltpu.VMEM((N_META, KV_W), F32), pltpu.VMEM((N_META, KV_W), F32),
                        pltpu.VMEM((CHUNK, QKV_W), F32)],
        input_output_aliases={7: 0},
        name="swa_bwd", compiler_params=_params(1))(proj, proj, proj, sink_rows, bias, out, dout, dproj)


def _pack_w_in_t(w_in_t):
    w_dt = w_in_t[CUT_DT:CUT_Q].reshape(SSM_GROUPS, HEADS_PER_GROUP, D_MODEL)
    w_dt = jnp.pad(w_dt, ((0, 0), (0, 128 - HEADS_PER_GROUP), (0, 0))).reshape(SSM_GROUPS * 128, D_MODEL)
    return jnp.concatenate([w_in_t[CUT_Z:CUT_XBC], w_in_t[CUT_G:], w_dt, w_in_t[CUT_Q:CUT_G], w_in_t[CUT_XBC:CUT_DT]], axis=0)


def _unpack_w_in_t(wp_t):
    w_dt = wp_t[OFF_DT:OFF_Q].reshape(SSM_GROUPS, 128, D_MODEL)[:, :HEADS_PER_GROUP].reshape(SSM_HEADS, D_MODEL)
    return jnp.concatenate([wp_t[OFF_Z:OFF_GATE], wp_t[OFF_XBC:], w_dt, wp_t[OFF_Q:OFF_XBC], wp_t[OFF_GATE:OFF_DT]], axis=0)


def _group_rows(v, width):
    return jnp.pad(v.reshape(SSM_GROUPS, 1, HEADS_PER_GROUP), ((0, 0), (0, 0), (0, width - HEADS_PER_GROUP)))


def _local_step(x, target, wt, late_weights=None, on_grad=None, started=None):
    seq = x.shape[0]
    grads = {}

    def emit(name, g):
        grads[name] = g
        return None if on_grad is None else on_grad(name, g)
    meta = wt["meta_tokens"]
    wp_t = _pack_w_in_t(wt["w_in_t"])
    dtb = _group_rows(wt["ssm_dt_bias"].reshape(-1), 128)
    alog = _group_rows(wt["ssm_a_log"].reshape(-1), 128)
    dskip_x = jnp.repeat(wt["ssm_d_skip"].reshape(-1), HEAD_P).reshape(SSM_GROUPS, 1, GROUP_W)
    sink_rows = jnp.repeat(wt["attn_sinks"].reshape(KV_HEADS, ATTN_GROUP), CHUNK, axis=1).reshape(KV_HEADS, ATTN_GROUP * CHUNK, 1)

    hn = _prenorm(x, meta, wt["norm_pre_mix"])
    proj = _matmul(hn, wp_t, tb=True, name="in_proj", after=started)
    xc, xact = _ssm_conv_fwd(proj, wt["ssm_conv_w"], wt["ssm_conv_b"])
    y, hst = _ssd_fwd(xact, proj, dtb, alog, dskip_x)
    yn = _ssm_post(y, proj, wt["ssm_norm"])
    if late_weights is not None:
        wt = {**wt, **late_weights(yn)}
    y_ssm = _matmul(yn, wt["w_ssm_out"], name="ssm_out")
    bias = _swa_bias()
    attn = _swa_fwd(proj, sink_rows, bias)
    y_attn = _matmul(attn, wt["w_attn_out"], name="attn_out")
    mixed = _mix_fwd(proj, y_ssm, y_attn)
    mix = _matmul(mixed, wt["w_mix_out"], name="mix_out")
    h1, hn2 = _postmix(x, meta, mix, wt["norm_post_mix"], wt["norm_pre_ffn"])
    up = _matmul(hn2, wt["w_ffn_up_t"], tb=True, out_dtype=BF16, name="ffn_up")
    u, act = _ffn_act(up, wt["ffn_conv_w"], wt["ffn_conv_b"])
    f = _matmul(act, wt["w_ffn_down"], name="ffn_down")
    df, dy, g_norm_post_ffn, loss_row = _final(h1, f, target, wt["norm_post_ffn"])

    grads["norm_post_ffn"] = g_norm_post_ffn
    sent = emit("w_ffn_down", _matmul(act, df, ta=True, out_dtype=BF16, name="dw_ffn_down"))
    dact = _matmul(df, wt["w_ffn_down"], tb=True, out_dtype=BF16, name="d_act", after=sent)
    dup, grads["ffn_conv_w"], grads["ffn_conv_b"] = _ffn_act_bwd(u, up, dact, wt["ffn_conv_w"])
    sent = emit("w_ffn_up_t", _matmul(dup, hn2, ta=True, out_dtype=BF16, name="dw_ffn_up"))
    dhn2 = _matmul(dup, wt["w_ffn_up_t"], name="d_hn2", after=sent)
    dmix, dh, grads["norm_pre_ffn"], grads["norm_post_mix"] = _postmix_bwd(h1, dhn2, dy, mix, wt["norm_pre_ffn"], wt["norm_post_mix"])
    sent = emit("w_mix_out", _matmul(mixed, dmix, ta=True, out_dtype=BF16, name="dw_mix_out"))
    dmixed = _matmul(dmix, wt["w_mix_out"], tb=True, name="d_mixed", after=sent)
    dy_ssm, dy_attn, dproj = _mix_bwd(dmixed, proj, y_ssm, y_attn, lax.empty(proj.shape, BF16))
    sent = emit("w_ssm_out", _matmul(yn, dy_ssm, ta=True, out_dtype=BF16, name="dw_ssm_out"))
    dyn = _matmul(dy_ssm, wt["w_ssm_out"], tb=True, out_dtype=BF16, name="d_yn", after=sent)
    sent = emit("w_attn_out", _matmul(attn, dy_attn, ta=True, out_dtype=BF16, name="dw_attn_out"))
    dattn = _matmul(dy_attn, wt["w_attn_out"], tb=True, name="d_attn", after=sent)
    dy_ssd, dproj, grads["ssm_norm"] = _ssm_post_bwd(y, proj, dyn, wt["ssm_norm"], dproj)
    dxs, dbm, dcm, dproj, dalog, ddtb, dd_x = _ssd_bwd(xact, proj, dtb, alog, dskip_x, dy_ssd, hst, dproj)
    grads["ssm_a_log"] = dalog[:, 0, :HEADS_PER_GROUP].reshape(1, SSM_HEADS)
    grads["ssm_dt_bias"] = ddtb[:, 0, :HEADS_PER_GROUP].reshape(1, SSM_HEADS)
    grads["ssm_d_skip"] = dd_x.reshape(SSM_HEADS, HEAD_P).sum(axis=1).reshape(1, SSM_HEADS)
    dproj, grads["ssm_conv_w"], grads["ssm_conv_b"] = _ssm_conv_bwd(xc, proj, dxs, dbm, dcm, wt["ssm_conv_w"], dproj)
    dproj, dsink = _swa_bwd(proj, sink_rows, bias, attn, dattn, dproj)
    grads["attn_sinks"] = dsink[:, :ATTN_GROUP, 0].reshape(1, ATTN_HEADS)
    sent = emit("w_in_t", _unpack_w_in_t(_matmul(dproj, hn, ta=True, out_dtype=BF16, name="dw_in")))
    dhn = _matmul(dproj, wp_t, name="d_hn", after=sent)
    grad_x, grads["meta_tokens"], grads["norm_pre_mix"] = _prenorm_bwd(x, meta, dhn, dh, wt["norm_pre_mix"])
    return loss_row[0, 0], grad_x, grads


def _all_gather(shards):
    n = len(shards)

    def body(*refs):
        ins, outs = refs[:n], refs[n:2 * n]
        send_sems, recv_sems, local_sems = refs[2 * n:]
        x, y, c = lax.axis_index("x"), lax.axis_index("y"), lax.axis_index("c")
        me, sibling = (x, y, c), (x, y, 1 - c)
        chips = [(1 - x, y), (x, 1 - y), (1 - x, 1 - y)]

        def slot(a, dev):
            return outs[a].at[4 * dev[0] + 2 * dev[1] + dev[2]]

        def copy(k, a, block, to, src=None):
            return pltpu.make_async_remote_copy(
                src_ref=slot(a, block) if src is None else src, dst_ref=slot(a, block),
                send_sem=send_sems.at[k, a], recv_sem=recv_sems.at[k, a],
                device_id=to, device_id_type=pl.DeviceIdType.MESH)

        mine = [pltpu.make_async_copy(ins[a], slot(a, me), local_sems.at[a]) for a in range(n)]
        for cp in mine:
            cp.start()
        first = [copy(0, a, me, sibling, src=ins[a]) for a in range(n)]
        for j, chip in enumerate(chips):
            first += [copy(1 + j, a, me, (*chip, c), src=ins[a]) for a in range(n)]
        for cp in first:
            cp.start()
        passed = []
        for j, chip in enumerate(chips):
            for a in range(n):
                copy(1 + j, a, (*chip, c), me).wait_recv()
                fwd = copy(4 + j, a, (*chip, c), sibling)
                fwd.start()
                passed.append(fwd)
        for a in range(n):
            copy(0, a, sibling, me).wait_recv()
        for j, chip in enumerate(chips):
            for a in range(n):
                copy(4 + j, a, (*chip, 1 - c), me).wait_recv()
        for cp in first + passed:
            cp.wait_send()
        for cp in mine:
            cp.wait()

    hbm = pl.BlockSpec(memory_space=pl.ANY)
    return pl.pallas_call(
        body, in_specs=[hbm] * n, out_specs=[hbm] * n,
        out_shape=[_sds((N_DEV,) + s.shape, s.dtype) for s in shards],
        scratch_shapes=[pltpu.SemaphoreType.DMA((7, n)), pltpu.SemaphoreType.DMA((7, n)), pltpu.SemaphoreType.DMA((n,))],
        name="gather_weights")(*shards)


def _peer_table():
    x, y, c = lax.axis_index("x"), lax.axis_index("y"), lax.axis_index("c")
    peers = []
    for k in range(N_DEV - 1):
        bits = k + 1
        p = (x ^ ((bits >> 2) & 1), y ^ ((bits >> 1) & 1), c ^ (bits & 1))
        peers.append((k, p, 4 * p[0] + 2 * p[1] + p[2]))
    return 4 * x + 2 * y + c, peers


_HBM = pl.BlockSpec(memory_space=pltpu.HBM)
_SEM = pl.BlockSpec(memory_space=pltpu.SEMAPHORE)
_EFFECT = pltpu.SideEffectType.DATAFLOW_SIDE_EFFECTING


def _push_copy(src, land, send_sems, recv_sems, a, k, p, src_slot, dst_slot):
    sem = a * (N_DEV - 1) + k
    return pltpu.make_async_remote_copy(
        src_ref=src[a] if src_slot is None else src[a].at[src_slot], dst_ref=land[a].at[dst_slot],
        send_sem=send_sems.at[sem], recv_sem=recv_sems.at[sem], device_id=p, device_id_type=pl.DeviceIdType.MESH)


def _push_start(srcs, scatter, name):
    n = len(srcs)
    lands = [lax.empty(s.shape if scatter else (N_DEV,) + s.shape, s.dtype) for s in srcs]

    def body(*refs):
        src, land = refs[:n], refs[n:2 * n]
        send_sems, recv_sems, token = refs[2 * n], refs[2 * n + 1], refs[-1]
        my_id, peers = _peer_table()
        for a in range(n):
            for k, p, p_id in peers:
                _push_copy(src, land, send_sems, recv_sems, a, k, p, p_id if scatter else None, my_id).start()
        token[...] = jnp.zeros_like(token)

    sems = pltpu.SemaphoreType.DMA(((N_DEV - 1) * n,))
    res = pl.pallas_call(
        body, name=name,
        out_shape=(sems, sems, *[pltpu.HBM(a.shape, a.dtype) for a in srcs + lands], _sds((8, 128), F32)),
        in_specs=[_HBM] * (2 * n), out_specs=(_SEM, _SEM, *[_HBM] * (2 * n), pl.BlockSpec(memory_space=pltpu.VMEM)),
        input_output_aliases={i: 2 + i for i in range(2 * n)},
        compiler_params=pltpu.CompilerParams(has_side_effects=_EFFECT),
    )(*[pltpu.with_memory_space_constraint(a, pltpu.HBM) for a in srcs + lands])
    return dict(send=res[0], recv=res[1], src=list(res[2:2 + n]), land=list(res[2 + n:2 + 2 * n]), token=res[-1],
                scatter=scatter)


def _push_wait(handle, after, name):
    n = len(handle["src"])
    scatter = handle["scatter"]

    def body(*refs):
        src, land = refs[:n], refs[n:2 * n]
        send_sems, recv_sems = refs[2 * n], refs[2 * n + 1]
        _, peers = _peer_table()
        for a in range(n):
            for k, p, p_id in peers:
                cp = _push_copy(src, land, send_sems, recv_sems, a, k, p, p_id if scatter else None, p_id)
                cp.wait_send()
                cp.wait_recv()

    arrays = handle["src"] + handle["land"]
    res = pl.pallas_call(
        body, name=name, out_shape=tuple(pltpu.HBM(a.shape, a.dtype) for a in arrays),
        in_specs=[_HBM] * (2 * n) + [_SEM, _SEM, pl.BlockSpec(memory_space=pl.ANY)], out_specs=tuple([_HBM] * (2 * n)),
        input_output_aliases={i: i for i in range(2 * n)},
        compiler_params=pltpu.CompilerParams(has_side_effects=_EFFECT),
    )(*arrays, handle["send"], handle["recv"], after)
    return list(res[:n]), list(res[n:])


def _slot_sum(p_ref, own_ref):
    if own_ref is not None:
        my_id = 4 * lax.axis_index("x") + 2 * lax.axis_index("y") + lax.axis_index("c")
        mine = own_ref[...].astype(F32)
    g = None
    for s in range(p_ref.shape[0]):
        term = p_ref[s].astype(F32)
        if own_ref is not None:
            term = jnp.where(my_id == s, mine, term)
        g = term if g is None else g + term
    return g


def _to_bf16(arrays):
    n = len(arrays)

    def body(*refs):
        for i in range(n):
            refs[n + i][...] = refs[i][...].astype(BF16)

    return pl.pallas_call(body, out_shape=[_sds(a.shape, BF16) for a in arrays], name="weights_to_bf16",
                          compiler_params=pltpu.CompilerParams(vmem_limit_bytes=VMEM_LIMIT))(*arrays)


def _adamw(parts, own, w, m, v, name):
    rows, cols = w.shape
    if rows % 16 == 0:
        tr, tc = _pick(rows, (256, 128, 176, 64, 32, 16)), cols
    else:
        tr, tc = rows, _pick(cols, (256, 128))

    def body(*refs):
        if own is None:
            p_ref, w_ref, m_ref, v_ref, g_ref, d_ref, nm_ref, nv_ref = refs
            own_ref = None
        else:
            p_ref, own_ref, w_ref, m_ref, v_ref, g_ref, d_ref, nm_ref, nv_ref = refs
        g = _slot_sum(p_ref, own_ref)
        m_new = ADAM_B1 * m_ref[...] + (1.0 - ADAM_B1) * g
        v_new = ADAM_B2 * v_ref[...] + (1.0 - ADAM_B2) * (g * g)
        m_hat = m_new / (1.0 - ADAM_B1 ** ADAM_STEP)
        v_hat = v_new / (1.0 - ADAM_B2 ** ADAM_STEP)
        g_ref[...] = g
        d_ref[...] = -ADAM_LR * (m_hat / (jnp.sqrt(v_hat) + ADAM_EPS) + ADAM_WD * w_ref[...])
        nm_ref[...] = m_new
        nv_ref[...] = v_new

    by_rows = tc == cols
    spec = pl.BlockSpec((tr, tc), (lambda i: (i, 0)) if by_rows else (lambda i: (0, i)))
    parts_spec = pl.BlockSpec((parts.shape[0], tr, tc), (lambda i: (0, i, 0)) if by_rows else (lambda i: (0, 0, i)))
    operands = (parts, w, m, v) if own is None else (parts, own, w, m, v)
    return pl.pallas_call(
        body, grid=(rows // tr if by_rows else cols // tc,),
        in_specs=[parts_spec] + [spec] * (len(operands) - 1),
        out_specs=[spec] * 4, out_shape=[_sds((rows, cols), F32)] * 4,
        name=name, compiler_params=_params(1))(*operands)


SMALL_REPLICATED = (("norm_pre_mix", 1024), ("ssm_conv_b", 3072), ("ssm_dt_bias", 32), ("ssm_a_log", 32),
                    ("ssm_d_skip", 32), ("ssm_norm", 2048), ("attn_sinks", 16), ("norm_post_mix", 1024),
                    ("norm_pre_ffn", 1024), ("ffn_conv_b", 5632), ("norm_post_ffn", 1024))
SMALL_SHARDED = (("meta_tokens", (N_META, D_MODEL // N_DEV)), ("ssm_conv_w", (SSM_CONV, CONV_DIM // N_DEV)),
                 ("ffn_conv_w", (FFN_CONV, 2 * FFN_DIM // N_DEV)))
BIG = (("w_in", (D_MODEL, N_IN // N_DEV), 1), ("w_ssm_out", (D_INNER // N_DEV, D_MODEL), 0),
       ("w_attn_out", (D_MODEL // N_DEV, D_MODEL), 0), ("w_mix_out", (D_MODEL // N_DEV, D_MODEL), 0),
       ("w_ffn_up", (D_MODEL, 2 * FFN_DIM // N_DEV), 1), ("w_ffn_down", (FFN_DIM // N_DEV, D_MODEL), 0))


def _rows_of(size):
    return -(-size // 128)


def _as_rows(flat):
    size = flat.shape[-1]
    rows = _rows_of(size)
    flat = jnp.pad(flat, [(0, 0)] * (flat.ndim - 1) + [(0, rows * 128 - size)])
    return flat.reshape(flat.shape[:-1] + (rows, 128))


def _pack_small(rep, sharded):
    pieces = [_as_rows(rep[name].reshape(-1)) for name, _ in SMALL_REPLICATED]
    pieces += [_as_rows(sharded[name].reshape(-1)) for name, _ in SMALL_SHARDED]
    packed = jnp.concatenate(pieces, axis=0)
    return jnp.pad(packed, ((0, -packed.shape[0] % 8), (0, 0)))


def _unpack_small(packed):
    out, row = {}, 0
    for name, size in SMALL_REPLICATED:
        out[name] = packed[row:row + _rows_of(size)].reshape(-1)[:size].reshape(1, size)
        row += _rows_of(size)
    for name, (r, c) in SMALL_SHARDED:
        out[name] = packed[row:row + _rows_of(r * c)].reshape(-1)[:r * c].reshape(r, c)
        row += _rows_of(r * c)
    return out


def _shard_major(g, shape, axis):
    r, c = shape
    if axis == 0:
        return g.reshape(N_DEV, r, c)
    return g.reshape(r, N_DEV, c).transpose(1, 0, 2)


def kernel(x, meta_tokens, norm_pre_mix, w_in, ssm_conv_w, ssm_conv_b, ssm_dt_bias, ssm_a_log, ssm_d_skip, ssm_norm, w_ssm_out, attn_sinks, w_attn_out, w_mix_out, norm_post_mix, norm_pre_ffn, w_ffn_up, ffn_conv_w, ffn_conv_b, w_ffn_down, norm_post_ffn, loss_target, m_meta_tokens, m_norm_pre_mix, m_w_in, m_ssm_conv_w, m_ssm_conv_b, m_ssm_dt_bias, m_ssm_a_log, m_ssm_d_skip, m_ssm_norm, m_w_ssm_out, m_attn_sinks, m_w_attn_out, m_w_mix_out, m_norm_post_mix, m_norm_pre_ffn, m_w_ffn_up, m_ffn_conv_w, m_ffn_conv_b, m_w_ffn_down, m_norm_post_ffn, v_meta_tokens, v_norm_pre_mix, v_w_in, v_ssm_conv_w, v_ssm_conv_b, v_ssm_dt_bias, v_ssm_a_log, v_ssm_d_skip, v_ssm_norm, v_w_ssm_out, v_attn_sinks, v_w_attn_out, v_w_mix_out, v_norm_post_mix, v_norm_pre_ffn, v_w_ffn_up, v_ffn_conv_w, v_ffn_conv_b, v_w_ffn_down, v_norm_post_ffn):
    names = ("meta_tokens", "norm_pre_mix", "w_in", "ssm_conv_w", "ssm_conv_b", "ssm_dt_bias", "ssm_a_log", "ssm_d_skip",
             "ssm_norm", "w_ssm_out", "attn_sinks", "w_attn_out", "w_mix_out", "norm_post_mix", "norm_pre_ffn", "w_ffn_up",
             "ffn_conv_w", "ffn_conv_b", "w_ffn_down", "norm_post_ffn")
    w_loc = dict(zip(names, (meta_tokens, norm_pre_mix, w_in, ssm_conv_w, ssm_conv_b, ssm_dt_bias, ssm_a_log, ssm_d_skip,
                             ssm_norm, w_ssm_out, attn_sinks, w_attn_out, w_mix_out, norm_post_mix, norm_pre_ffn, w_ffn_up,
                             ffn_conv_w, ffn_conv_b, w_ffn_down, norm_post_ffn)))
    m_loc = dict(zip(names, (m_meta_tokens, m_norm_pre_mix, m_w_in, m_ssm_conv_w, m_ssm_conv_b, m_ssm_dt_bias, m_ssm_a_log,
                             m_ssm_d_skip, m_ssm_norm, m_w_ssm_out, m_attn_sinks, m_w_attn_out, m_w_mix_out, m_norm_post_mix,
                             m_norm_pre_ffn, m_w_ffn_up, m_ffn_conv_w, m_ffn_conv_b, m_w_ffn_down, m_norm_post_ffn)))
    v_loc = dict(zip(names, (v_meta_tokens, v_norm_pre_mix, v_w_in, v_ssm_conv_w, v_ssm_conv_b, v_ssm_dt_bias, v_ssm_a_log,
                             v_ssm_d_skip, v_ssm_norm, v_w_ssm_out, v_attn_sinks, v_w_attn_out, v_w_mix_out, v_norm_post_mix,
                             v_norm_pre_ffn, v_w_ffn_up, v_ffn_conv_w, v_ffn_conv_b, v_w_ffn_down, v_norm_post_ffn)))

    def local2d(d, name):
        a = d[name]
        return a if name == "meta_tokens" else a.reshape(a.shape[1:])

    def turned2d(d, name):
        a = jnp.swapaxes(d[name], 1, 2)
        return a.reshape(a.shape[1:])

    my_id = 4 * lax.axis_index("x") + 2 * lax.axis_index("y") + lax.axis_index("c")
    big = {name: (shape, axis) for name, shape, axis in BIG}

    def whole(name, g):
        return g.reshape(N_DEV * g.shape[1], g.shape[2])

    def key(name):
        return name + "_t" if big[name][1] == 1 else name

    by_rows = [name for name, _, axis in BIG if axis == 0]
    send_bf16 = dict(zip(by_rows, _to_bf16([local2d(w_loc, name) for name in by_rows])))
    for name, _, axis in BIG:
        if axis == 1:
            send_bf16[name] = turned2d(w_loc, name).astype(BF16)
    small_shard_pack = jnp.concatenate([_as_rows(local2d(w_loc, name).reshape(-1)) for name, _ in SMALL_SHARDED], axis=0)
    small_shard_pack = jnp.pad(small_shard_pack, ((0, -small_shard_pack.shape[0] % 8), (0, 0)))
    first = _all_gather([send_bf16["w_in"], small_shard_pack])
    rest_names = [name for name, _, _ in BIG if name != "w_in"]
    rest = [send_bf16[name] for name in rest_names]
    rest, first = lax.optimization_barrier((rest, first))
    rest_handle = _push_start(rest, False, "gather_rest_start")
    wt = {"w_in_t": whole("w_in", first[0])}
    row = 0
    for name, (r, c) in SMALL_SHARDED:
        blocks = first[1][:, row:row + _rows_of(r * c)].reshape(N_DEV, -1)[:, :r * c].reshape(N_DEV, r, c)
        wt[name] = blocks.transpose(1, 0, 2).reshape(r, N_DEV * c)
        row += _rows_of(r * c)
    for name, size in SMALL_REPLICATED:
        wt[name] = w_loc[name].reshape(1, size)

    def late_weights(after):
        own, landed = _push_wait(rest_handle, after, "gather_rest_wait")
        out = {}
        for name, mine, land in zip(rest_names, own, landed):
            out[key(name)] = whole(name, lax.dynamic_update_index_in_dim(land, mine, my_id, 0))
        return out

    sent = {}

    def on_grad(known_as, g):
        name = known_as.removesuffix("_t")
        by_owner = g.reshape(N_DEV, g.shape[0] // N_DEV, g.shape[1])
        sent[name] = _push_start([by_owner], True, "send_" + name)
        return sent[name]["token"]

    loss_part, grad_x, grads = _local_step(x[0], loss_target[0], wt, late_weights, on_grad, rest_handle["token"])
    loss = lax.psum(loss_part, AXES)

    small_parts = []
    for name, (r, c) in SMALL_SHARDED:
        small_parts.append(_as_rows(_shard_major(grads[name], (r, c), 1).reshape(N_DEV, r * c)))
    rep_rows = jnp.concatenate([_as_rows(grads[name].reshape(-1)) for name, _ in SMALL_REPLICATED], axis=0)
    small_send = jnp.concatenate([jnp.broadcast_to(rep_rows[None], (N_DEV,) + rep_rows.shape)] + small_parts, axis=1)
    small_send = jnp.pad(small_send, ((0, 0), (0, -small_send.shape[1] % 8), (0, 0)))
    small_handle = _push_start([small_send], True, "send_small")

    def small_pack(d):
        return _pack_small({name: d[name] for name, _ in SMALL_REPLICATED}, {name: local2d(d, name) for name, _ in SMALL_SHARDED})

    def arrived(handle, after, name):
        src, landed = _push_wait(handle, after, "arrived_" + name)
        return landed[0], lax.dynamic_index_in_dim(src[0], my_id, 0, keepdims=False)

    grad_w, delta_w, new_m, new_v = {}, {}, {}, {}
    outs = None
    after = small_handle["token"]
    for name, handle in sent.items():
        if name == "w_in":
            parts, own = arrived(small_handle, after, "small")
            outs = _adamw(parts, own, small_pack(w_loc), small_pack(m_loc), small_pack(v_loc), "adamw_small")
            after = outs[0]
        parts, own = arrived(handle, after, name)
        turned = big[name][1] == 1
        state = [turned2d(d, name) if turned else local2d(d, name) for d in (w_loc, m_loc, v_loc)]
        results = _adamw(parts, own, *state, "adamw_" + name)
        after = results[0]
        full = (1,) + big[name][0]
        for dst, a in zip((grad_w, delta_w, new_m, new_v), results):
            dst[name] = jnp.swapaxes(a[None], 1, 2) if turned else a.reshape(full)
    for dst, packed in zip((grad_w, delta_w, new_m, new_v), outs):
        for name, a in _unpack_small(packed).items():
            dst[name] = a.reshape(w_loc[name].shape)

    return (loss, grad_x[None], *[grad_w[n] for n in names], *[delta_w[n] for n in names],
            *[new_m[n] for n in names], *[new_v[n] for n in names])
```

```python
import jax
import jax.numpy as jnp
from jax import lax
from jax.experimental import pallas as pl
from jax.experimental.pallas import tpu as pltpu

F32 = jnp.float32
BF16 = jnp.bfloat16

D_MODEL = 1024
N_META = 16
CHUNK = 128
META_PAD = CHUNK - N_META
D_INNER = 2048
HEAD_P = 64
SSM_HEADS = 32
SSM_GROUPS = 4
HEADS_PER_GROUP = SSM_HEADS // SSM_GROUPS
GROUP_W = HEADS_PER_GROUP * HEAD_P
D_STATE = 128
SSM_CONV = 4
CONV_DIM = D_INNER + 2 * SSM_GROUPS * D_STATE
ATTN_HEADS = 16
KV_HEADS = 4
ATTN_GROUP = ATTN_HEADS // KV_HEADS
DH = 64
KV_W = KV_HEADS * DH
FFN_DIM = 2816
FFN_CONV = 3
EPS = 1e-6
NEG = -1e30
N_DEV = 8
AXES = ("x", "y", "c")

OFF_Z, OFF_GATE, OFF_DT, OFF_Q, OFF_K, OFF_V, OFF_XBC = 0, 2048, 4096, 4608, 5632, 5888, 6144
N_INP = OFF_XBC + CONV_DIM
QKV_W = OFF_XBC - OFF_Q
CUT_Z, CUT_XBC, CUT_DT, CUT_Q, CUT_K, CUT_V, CUT_G = 0, 2048, 5120, 5152, 6176, 6432, 6688
N_IN = 8736

ADAM_LR, ADAM_B1, ADAM_B2, ADAM_EPS, ADAM_WD, ADAM_STEP = 0.001, 0.9, 0.999, 1e-08, 0.01, 10

VMEM_LIMIT = 56 * 1024 * 1024


def _params(n_grid):
    return pltpu.CompilerParams(dimension_semantics=("arbitrary",) * n_grid, vmem_limit_bytes=VMEM_LIMIT)


def _sds(shape, dtype):
    return jax.ShapeDtypeStruct(shape, dtype)


def _pick(n, prefs):
    for c in prefs:
        if n % c == 0:
            return c
    raise ValueError(f"no tile of {prefs} divides {n}")


def _row(tr, width, cb=0):
    return pl.BlockSpec((tr, width), lambda i: (i, cb))


def _row_rev(tr, width, nt, cb=0):
    return pl.BlockSpec((tr, width), lambda i: (nt - 1 - i, cb))


def _full(shape):
    return pl.BlockSpec(shape, lambda *_: (0,) * len(shape))


def _sigmoid(x):
    return 1.0 / (1.0 + jnp.exp(-x))


def _softplus(x):
    return jnp.maximum(x, 0.0) + jnp.log(1.0 + jnp.exp(-jnp.abs(x)))


def _rms(x):
    return lax.rsqrt(jnp.mean(x * x, axis=-1, keepdims=True) + EPS)


def _rms_bwd(x, r, w, dy):
    xh = x * r
    g = dy * w
    dx = r * (g - xh * jnp.mean(g * xh, axis=-1, keepdims=True))
    return dx, jnp.sum(dy * xh, axis=0, keepdims=True)


def _row_ids(shape, tile_index, tr):
    return tile_index * tr + lax.broadcasted_iota(jnp.int32, shape, 0)


HALO = 8


def _causal_taps(x, halo, first_step, taps):
    n = x.shape[0]

    @pl.when(first_step)
    def _():
        halo[...] = jnp.zeros_like(halo)

    before = halo[...]
    row = lax.broadcasted_iota(jnp.int32, before.shape, 0)
    shifted = [x]
    for s in range(1, taps):
        rolled = pltpu.roll(x, s, 0)
        head = jnp.where(row < s, pltpu.roll(before, s, 0), rolled[0:HALO, :])
        shifted.append(jnp.concatenate([head, rolled[HALO:, :]], axis=0))
    halo[...] = x[n - HALO:, :]
    return shifted


def _anticausal_taps(x, halo, first_step, taps):
    n = x.shape[0]

    @pl.when(first_step)
    def _():
        halo[...] = jnp.zeros_like(halo)

    after = halo[...]
    row = lax.broadcasted_iota(jnp.int32, after.shape, 0)
    shifted = [x]
    for s in range(1, taps):
        rolled = pltpu.roll(x, n - s, 0)
        tail = jnp.where(row >= HALO - s, pltpu.roll(after, HALO - s, 0), rolled[n - HALO:, :])
        shifted.append(jnp.concatenate([rolled[:n - HALO, :], tail], axis=0))
    halo[...] = x[0:HALO, :]
    return shifted


def _matmul(a, b, *, ta=False, tb=False, out_dtype=F32, name, after=None):
    if ta:
        k_dim, m_dim = a.shape
    else:
        m_dim, k_dim = a.shape
    n_dim = b.shape[0] if tb else b.shape[1]
    tm = _pick(m_dim, (1408, 1024, 768, 512, 384, 256, 128))
    tn = _pick(n_dim, (1024, 1408, 768, 512, 384, 256, 128))
    if ta:
        tk = _pick(k_dim, (1408, 1024, 768, 512, 384, 256, 128))
    else:
        tk = k_dim if k_dim <= 3072 else _pick(k_dim, (3072, 2816, 2048, 1024))
    nk = k_dim // tk
    dims = (((0 if ta else 1,), (1 if tb else 0,)), ((), ()))

    use_acc = nk > 1 and out_dtype != F32

    def body(a_ref, b_ref, *rest):
        o_ref = rest[-2] if use_acc else rest[-1]
        acc_ref = rest[-1] if use_acc else o_ref
        r = lax.dot_general(a_ref[...].astype(BF16), b_ref[...].astype(BF16), dims, preferred_element_type=F32)
        if nk == 1:
            o_ref[...] = r.astype(o_ref.dtype)
        else:
            k = pl.program_id(2)

            @pl.when(k == 0)
            def _():
                acc_ref[...] = r

            @pl.when(k > 0)
            def _():
                acc_ref[...] += r

            if use_acc:
                @pl.when(k == nk - 1)
                def _():
                    o_ref[...] = acc_ref[...].astype(o_ref.dtype)

    a_spec = pl.BlockSpec((tk, tm), lambda i, j, k: (k, i)) if ta else pl.BlockSpec((tm, tk), lambda i, j, k: (i, k))
    b_spec = pl.BlockSpec((tn, tk), lambda i, j, k: (j, k)) if tb else pl.BlockSpec((tk, tn), lambda i, j, k: (k, j))
    extra_specs, extra = ([], ()) if after is None else ([pl.BlockSpec(memory_space=pl.ANY)], (after,))
    return pl.pallas_call(
        body, grid=(m_dim // tm, n_dim // tn, nk), in_specs=[a_spec, b_spec] + extra_specs,
        out_specs=pl.BlockSpec((tm, tn), lambda i, j, k: (i, j)), out_shape=_sds((m_dim, n_dim), out_dtype),
        scratch_shapes=[pltpu.VMEM((tm, tn), F32)] if use_acc else [],
        name=name, compiler_params=_params(3))(a, b, *extra)


def _seq_specs():
    return [pl.BlockSpec((CHUNK, D_MODEL), lambda i: (jnp.maximum(i - 1, 0), 0)), _full((N_META, D_MODEL))]


def _seq_tile(x_ref, meta_ref, i):
    first = jnp.concatenate([jnp.zeros((META_PAD, D_MODEL), F32), meta_ref[...]], axis=0)
    return jnp.where(i == 0, first, x_ref[...])


def _prenorm(x, meta, w):
    t_rows = x.shape[0] + CHUNK

    def body(x_ref, meta_ref, w_ref, o_ref):
        h = _seq_tile(x_ref, meta_ref, pl.program_id(0))
        o_ref[...] = (h * _rms(h) * w_ref[...]).astype(BF16)

    return pl.pallas_call(body, grid=(t_rows // CHUNK,), in_specs=_seq_specs() + [_full((1, D_MODEL))],
                          out_specs=_row(CHUNK, D_MODEL), out_shape=_sds((t_rows, D_MODEL), BF16),
                          name="prenorm", compiler_params=_params(1))(x, meta, w)


def _xbc_specs(tr, rev_nt=None):
    cbs = [OFF_XBC // 1024 + j for j in range(CONV_DIM // 1024)]
    if rev_nt is None:
        return [_row(tr, 1024, cb) for cb in cbs]
    return [_row_rev(tr, 1024, rev_nt, cb) for cb in cbs]


def _ssm_conv_fwd(proj, conv_w, conv_b):
    t_rows = proj.shape[0]
    tr = CHUNK

    def body(x0, x1, x2, w_ref, b_ref, xc_ref, xa_ref, hist):
        x = jnp.concatenate([x0[...], x1[...], x2[...]], axis=1).astype(F32)
        acc = b_ref[...]
        for s, moved in enumerate(_causal_taps(x, hist, pl.program_id(0) == 0, SSM_CONV)):
            acc = acc + w_ref[SSM_CONV - 1 - s:SSM_CONV - s, :] * moved
        xc_ref[...] = acc
        xa_ref[...] = acc * _sigmoid(acc)

    return pl.pallas_call(
        body, grid=(t_rows // tr,),
        in_specs=_xbc_specs(tr) + [_full((SSM_CONV, CONV_DIM)), _full((1, CONV_DIM))],
        out_specs=[_row(tr, CONV_DIM), _row(tr, CONV_DIM)],
        out_shape=[_sds((t_rows, CONV_DIM), F32), _sds((t_rows, CONV_DIM), F32)],
        scratch_shapes=[pltpu.VMEM((HALO, CONV_DIM), F32)],
        name="ssm_conv_fwd", compiler_params=_params(1))(proj, proj, proj, conv_w, conv_b)


def _ssm_post(y, proj, w):
    t_rows = y.shape[0]
    tr = CHUNK

    def body(y_ref, z_ref, w_ref, o_ref):
        z = z_ref[...].astype(F32)
        yz = y_ref[...] * z * _sigmoid(z)
        o_ref[...] = (yz * _rms(yz) * w_ref[...]).astype(BF16)

    return pl.pallas_call(body, grid=(t_rows // tr,),
                          in_specs=[_row(tr, D_INNER), _row(tr, D_INNER, OFF_Z // D_INNER), _full((1, D_INNER))],
                          out_specs=_row(tr, D_INNER), out_shape=_sds((t_rows, D_INNER), BF16),
                          name="ssm_post", compiler_params=_params(1))(y, proj, w)


def _mix_fwd(proj, y_ssm, y_attn):
    t_rows = y_ssm.shape[0]
    tr = _pick(t_rows, (384, 128))

    def body(g_ref, ys_ref, ya_ref, o_ref):
        g = _sigmoid(g_ref[...].astype(F32))
        o_ref[...] = (g[:, :D_MODEL] * ys_ref[...] + g[:, D_MODEL:] * ya_ref[...]).astype(BF16)

    return pl.pallas_call(body, grid=(t_rows // tr,),
                          in_specs=[_row(tr, 2 * D_MODEL, OFF_GATE // (2 * D_MODEL)), _row(tr, D_MODEL), _row(tr, D_MODEL)],
                          out_specs=_row(tr, D_MODEL), out_shape=_sds((t_rows, D_MODEL), BF16),
                          name="mix_fwd", compiler_params=_params(1))(proj, y_ssm, y_attn)


def _postmix(x, meta, mix, w_post, w_pre):
    t_rows = mix.shape[0]
    tr = CHUNK

    def body(x_ref, meta_ref, m_ref, wp_ref, wf_ref, h1_ref, hn_ref):
        m = m_ref[...]
        h1 = _seq_tile(x_ref, meta_ref, pl.program_id(0)) + m * _rms(m) * wp_ref[...]
        h1 = jnp.where(_row_ids(h1.shape, pl.program_id(0), tr) >= META_PAD, h1, 0.0)
        h1_ref[...] = h1
        hn_ref[...] = (h1 * _rms(h1) * wf_ref[...]).astype(BF16)

    return pl.pallas_call(body, grid=(t_rows // tr,),
                          in_specs=_seq_specs() + [_row(tr, D_MODEL), _full((1, D_MODEL)), _full((1, D_MODEL))],
                          out_specs=[_row(tr, D_MODEL), _row(tr, D_MODEL)],
                          out_shape=[_sds((t_rows, D_MODEL), F32), _sds((t_rows, D_MODEL), BF16)],
                          name="postmix", compiler_params=_params(1))(x, meta, mix, w_post, w_pre)


def _ffn_act(up, conv_w, conv_b):
    t_rows = up.shape[0]
    tr = CHUNK
    width = 2 * FFN_DIM

    def body(up_ref, w_ref, b_ref, u_ref, act_ref, hist):
        x = up_ref[...].astype(F32)
        u = b_ref[...]
        for s, moved in enumerate(_causal_taps(x, hist, pl.program_id(0) == 0, FFN_CONV)):
            u = u + w_ref[FFN_CONV - 1 - s:FFN_CONV - s, :] * moved
        u_ref[...] = u.astype(BF16)
        a = u[:, :FFN_DIM]
        act_ref[...] = (a * _sigmoid(a) * u[:, FFN_DIM:]).astype(BF16)

    return pl.pallas_call(
        body, grid=(t_rows // tr,), in_specs=[_row(tr, width), _full((FFN_CONV, width)), _full((1, width))],
        out_specs=[_row(tr, width), _row(tr, FFN_DIM)],
        out_shape=[_sds((t_rows, width), BF16), _sds((t_rows, FFN_DIM), BF16)],
        scratch_shapes=[pltpu.VMEM((HALO, width), F32)],
        name="ffn_act", compiler_params=_params(1))(up, conv_w, conv_b)


def _final(h1, f, target, w):
    t_rows = h1.shape[0]
    tr = CHUNK

    def body(h1_ref, f_ref, t_ref, w_ref, df_ref, dy_ref, dw_ref, loss_ref):
        i = pl.program_id(0)

        @pl.when(i == 0)
        def _():
            dw_ref[...] = jnp.zeros_like(dw_ref)
            loss_ref[...] = jnp.zeros_like(loss_ref)

        f_val = f_ref[...]
        r = _rms(f_val)
        wv = w_ref[...]
        h2 = h1_ref[...] + f_val * r * wv
        diff = jnp.where(i >= 1, h2 - t_ref[...], 0.0)
        loss_ref[...] += 0.5 * jnp.sum(diff * diff) * (1.0 / D_MODEL)
        dy = diff * (1.0 / D_MODEL)
        dy_ref[...] = dy
        df, dw = _rms_bwd(f_val, r, wv, dy)
        df_ref[...] = df.astype(BF16)
        dw_ref[...] += dw

    tgt_spec = pl.BlockSpec((tr, D_MODEL), lambda i: (jnp.maximum(i - 1, 0), 0))
    return pl.pallas_call(
        body, grid=(t_rows // tr,),
        in_specs=[_row(tr, D_MODEL), _row(tr, D_MODEL), tgt_spec, _full((1, D_MODEL))],
        out_specs=[_row(tr, D_MODEL), _row(tr, D_MODEL), _full((1, D_MODEL)), _full((1, 128))],
        out_shape=[_sds((t_rows, D_MODEL), BF16), _sds((t_rows, D_MODEL), F32), _sds((1, D_MODEL), F32), _sds((1, 128), F32)],
        name="final", compiler_params=_params(1))(h1, f, target, w)


def _ffn_act_bwd(u, up, dact, conv_w):
    t_rows = u.shape[0]
    tr = CHUNK
    nt = t_rows // tr
    width = 2 * FFN_DIM

    def body(u_ref, up_ref, da_ref, w_ref, dup_ref, dw_ref, db_ref, ahead):
        @pl.when(pl.program_id(0) == 0)
        def _():
            dw_ref[...] = jnp.zeros_like(dw_ref)
            db_ref[...] = jnp.zeros_like(db_ref)

        u_val = u_ref[...].astype(F32)
        a, g = u_val[:, :FFN_DIM], u_val[:, FFN_DIM:]
        d = da_ref[...].astype(F32)
        s = _sigmoid(a)
        du = jnp.concatenate([d * g * s * (1.0 + a * (1.0 - s)), d * a * s], axis=1)
        x = up_ref[...].astype(F32)
        dup = None
        for sh, moved in enumerate(_anticausal_taps(du, ahead, pl.program_id(0) == 0, FFN_CONV)):
            k = FFN_CONV - 1 - sh
            term = w_ref[k:k + 1, :] * moved
            dup = term if dup is None else dup + term
            dw_ref[k:k + 1, :] += jnp.sum(moved * x, axis=0, keepdims=True)
        db_ref[...] += jnp.sum(du, axis=0, keepdims=True)
        dup_ref[...] = dup.astype(BF16)

    return pl.pallas_call(
        body, grid=(nt,),
        in_specs=[_row_rev(tr, width, nt), _row_rev(tr, width, nt), _row_rev(tr, FFN_DIM, nt), _full((FFN_CONV, width))],
        out_specs=[_row_rev(tr, width, nt), _full((FFN_CONV, width)), _full((1, width))],
        out_shape=[_sds((t_rows, width), BF16), _sds((FFN_CONV, width), F32), _sds((1, width), F32)],
        scratch_shapes=[pltpu.VMEM((HALO, width), F32)],
        name="ffn_act_bwd", compiler_params=_params(1))(u, up, dact, conv_w)


def _postmix_bwd(h1, dhn2, dy, mix, w_pre, w_post):
    t_rows = h1.shape[0]
    tr = CHUNK

    def body(h1_ref, dhn_ref, dy_ref, m_ref, wf_ref, wp_ref, dmix_ref, dh_ref, dwf_ref, dwp_ref):
        @pl.when(pl.program_id(0) == 0)
        def _():
            dwf_ref[...] = jnp.zeros_like(dwf_ref)
            dwp_ref[...] = jnp.zeros_like(dwp_ref)

        h1v = h1_ref[...]
        dx, dwf = _rms_bwd(h1v, _rms(h1v), wf_ref[...], dhn_ref[...])
        dwf_ref[...] += dwf
        dh1 = dy_ref[...] + dx
        dh1 = jnp.where(_row_ids(dh1.shape, pl.program_id(0), tr) >= META_PAD, dh1, 0.0)
        dh_ref[...] = dh1
        m = m_ref[...]
        dmix, dwp = _rms_bwd(m, _rms(m), wp_ref[...], dh1)
        dwp_ref[...] += dwp
        dmix_ref[...] = dmix.astype(BF16)

    return pl.pallas_call(
        body, grid=(t_rows // tr,),
        in_specs=[_row(tr, D_MODEL)] * 4 + [_full((1, D_MODEL))] * 2,
        out_specs=[_row(tr, D_MODEL), _row(tr, D_MODEL), _full((1, D_MODEL)), _full((1, D_MODEL))],
        out_shape=[_sds((t_rows, D_MODEL), BF16), _sds((t_rows, D_MODEL), F32), _sds((1, D_MODEL), F32), _sds((1, D_MODEL), F32)],
        name="postmix_bwd", compiler_params=_params(1))(h1, dhn2, dy, mix, w_pre, w_post)


_ANY = pl.BlockSpec(memory_space=pl.ANY)


def _mix_bwd(dmixed, proj, y_ssm, y_attn, dproj):
    t_rows = dmixed.shape[0]
    tr = _pick(t_rows, (384, 128))

    def body(d_ref, g_ref, ys_ref, ya_ref, _, dys_ref, dya_ref, dg_ref):
        d = d_ref[...]
        g = _sigmoid(g_ref[...].astype(F32))
        g1, g2 = g[:, :D_MODEL], g[:, D_MODEL:]
        dys_ref[...] = (d * g1).astype(BF16)
        dya_ref[...] = (d * g2).astype(BF16)
        dg_ref[...] = jnp.concatenate([d * ys_ref[...] * g1 * (1.0 - g1), d * ya_ref[...] * g2 * (1.0 - g2)],
                                      axis=1).astype(BF16)

    return pl.pallas_call(
        body, grid=(t_rows // tr,),
        in_specs=[_row(tr, D_MODEL), _row(tr, 2 * D_MODEL, OFF_GATE // (2 * D_MODEL)), _row(tr, D_MODEL), _row(tr, D_MODEL),
                  _ANY],
        out_specs=[_row(tr, D_MODEL), _row(tr, D_MODEL), _row(tr, 2 * D_MODEL, OFF_GATE // (2 * D_MODEL))],
        out_shape=[_sds((t_rows, D_MODEL), BF16), _sds((t_rows, D_MODEL), BF16), _sds(dproj.shape, dproj.dtype)],
        input_output_aliases={4: 2},
        name="mix_bwd", compiler_params=_params(1))(dmixed, proj, y_ssm, y_attn, dproj)


def _ssm_post_bwd(y, proj, dyn, w, dproj):
    t_rows = y.shape[0]
    tr = CHUNK

    def body(y_ref, z_ref, d_ref, w_ref, _, dy_ref, dz_ref, dw_ref):
        @pl.when(pl.program_id(0) == 0)
        def _():
            dw_ref[...] = jnp.zeros_like(dw_ref)

        yv, z = y_ref[...], z_ref[...].astype(F32)
        sz = _sigmoid(z)
        silu = z * sz
        yz = yv * silu
        dyz, dw = _rms_bwd(yz, _rms(yz), w_ref[...], d_ref[...].astype(F32))
        dw_ref[...] += dw
        dy_ref[...] = dyz * silu
        dz_ref[...] = (dyz * yv * sz * (1.0 + z * (1.0 - sz))).astype(BF16)

    return pl.pallas_call(
        body, grid=(t_rows // tr,),
        in_specs=[_row(tr, D_INNER), _row(tr, D_INNER, OFF_Z // D_INNER), _row(tr, D_INNER), _full((1, D_INNER)), _ANY],
        out_specs=[_row(tr, D_INNER), _row(tr, D_INNER, OFF_Z // D_INNER), _full((1, D_INNER))],
        out_shape=[_sds((t_rows, D_INNER), F32), _sds(dproj.shape, dproj.dtype), _sds((1, D_INNER), F32)],
        input_output_aliases={4: 1},
        name="ssm_post_bwd", compiler_params=_params(1))(y, proj, dyn, w, dproj)


def _ssm_conv_bwd(xc, proj, dxs, dbm, dcm, conv_w, dproj):
    t_rows = xc.shape[0]
    tr = CHUNK
    nt = t_rows // tr
    bc_w = SSM_GROUPS * D_STATE

    def body(xc_ref, x0, x1, x2, dxs_ref, db_ref, dc_ref, w_ref, _, dx_ref, dw_ref, dbias_ref, ahead):
        @pl.when(pl.program_id(0) == 0)
        def _():
            dw_ref[...] = jnp.zeros_like(dw_ref)
            dbias_ref[...] = jnp.zeros_like(dbias_ref)

        c = xc_ref[...]
        s = _sigmoid(c)
        dact = jnp.concatenate([dxs_ref[...], db_ref[...], dc_ref[...]], axis=1)
        dpre = dact * s * (1.0 + c * (1.0 - s))
        x = jnp.concatenate([x0[...], x1[...], x2[...]], axis=1).astype(F32)
        dx = None
        for sh, moved in enumerate(_anticausal_taps(dpre, ahead, pl.program_id(0) == 0, SSM_CONV)):
            k = SSM_CONV - 1 - sh
            term = w_ref[k:k + 1, :] * moved
            dx = term if dx is None else dx + term
            dw_ref[k:k + 1, :] += jnp.sum(moved * x, axis=0, keepdims=True)
        dbias_ref[...] += jnp.sum(dpre, axis=0, keepdims=True)
        dx_ref[...] = dx.astype(BF16)

    return pl.pallas_call(
        body, grid=(nt,),
        in_specs=[_row_rev(tr, CONV_DIM, nt)] + _xbc_specs(tr, nt)
        + [_row_rev(tr, D_INNER, nt), _row_rev(tr, bc_w, nt), _row_rev(tr, bc_w, nt), _full((SSM_CONV, CONV_DIM)), _ANY],
        out_specs=[_row_rev(tr, CONV_DIM, nt, OFF_XBC // CONV_DIM), _full((SSM_CONV, CONV_DIM)), _full((1, CONV_DIM))],
        out_shape=[_sds(dproj.shape, dproj.dtype), _sds((SSM_CONV, CONV_DIM), F32), _sds((1, CONV_DIM), F32)],
        scratch_shapes=[pltpu.VMEM((HALO, CONV_DIM), F32)],
        input_output_aliases={8: 0},
        name="ssm_conv_bwd", compiler_params=_params(1))(xc, proj, proj, proj, dxs, dbm, dcm, conv_w, dproj)


def _prenorm_bwd(x, meta, dhn, dh, w):
    t_rows = dhn.shape[0]
    tr = CHUNK

    def body(x_ref, meta_ref, d_ref, r_ref, w_ref, dx_ref, dmeta_ref, dw_ref):
        i = pl.program_id(0)

        @pl.when(i == 0)
        def _():
            dw_ref[...] = jnp.zeros_like(dw_ref)

        h = _seq_tile(x_ref, meta_ref, i)
        dx, dw = _rms_bwd(h, _rms(h), w_ref[...], d_ref[...])
        dw_ref[...] += dw
        dh_tile = r_ref[...] + dx
        dx_ref[...] = dh_tile

        @pl.when(i == 0)
        def _():
            dmeta_ref[...] = dh_tile[META_PAD:, :]

    return pl.pallas_call(
        body, grid=(t_rows // tr,), in_specs=_seq_specs() + [_row(tr, D_MODEL)] * 2 + [_full((1, D_MODEL))],
        out_specs=[pl.BlockSpec((tr, D_MODEL), lambda i: (jnp.maximum(i - 1, 0), 0)), _full((N_META, D_MODEL)),
                   _full((1, D_MODEL))],
        out_shape=[_sds((t_rows - tr, D_MODEL), F32), _sds((N_META, D_MODEL), F32), _sds((1, D_MODEL), F32)],
        name="prenorm_bwd", compiler_params=_params(1))(x, meta, dhn, dh, w)


def _dot01(x, m01, x_left, parts):
    acc, rest = None, x
    for i in range(parts):
        piece = rest.astype(BF16)
        term = (jnp.dot(piece, m01, preferred_element_type=F32) if x_left
                else jnp.dot(m01, piece, preferred_element_type=F32))
        acc = term if acc is None else acc + term
        if i + 1 < parts:
            rest = rest - piece.astype(F32)
    return acc


def _ssd_common(dt_raw, dt_bias, a_log, chunk_index):
    rows = lax.broadcasted_iota(jnp.int32, (CHUNK, CHUNK), 0)
    cols = lax.broadcasted_iota(jnp.int32, (CHUNK, CHUNK), 1)
    low = rows >= cols
    raw = dt_raw + dt_bias
    live = _row_ids(raw.shape, chunk_index, CHUNK) >= META_PAD
    dt = jnp.where(live, _softplus(raw), 0.0)
    a_head = -jnp.exp(a_log)
    cs = _dot01(dt * a_head, low.astype(BF16), False, 3)
    grow = jnp.exp(cs)
    fade = jnp.exp(cs[CHUNK - 1:CHUNK, :] - cs)
    expand = (lax.broadcasted_iota(jnp.int32, (CHUNK, GROUP_W), 1) // HEAD_P
              == lax.broadcasted_iota(jnp.int32, (CHUNK, GROUP_W), 0)).astype(BF16)
    fold = (lax.broadcasted_iota(jnp.int32, (GROUP_W, CHUNK), 0) // HEAD_P
            == lax.broadcasted_iota(jnp.int32, (GROUP_W, CHUNK), 1)).astype(BF16)
    return dict(low=low, triu=(rows <= cols).astype(BF16), raw=raw, live=live, dt=dt, a_head=a_head, cs=cs, cs_t=cs.T,
                fold=fold, dtx=_dot01(dt, expand, True, 2), growx=_dot01(grow, expand, True, 2),
                fadex=_dot01(fade, expand, True, 2))


def _decay_matrix(cm, j):
    diff = cm["cs"][:, j:j + 1] - cm["cs_t"][j:j + 1, :]
    return jnp.where(cm["low"], jnp.exp(jnp.where(cm["low"], diff, 0.0)), 0.0)


def _dot(a, b, dims):
    return lax.dot_general(a.astype(BF16), b.astype(BF16), (dims, ((), ())), preferred_element_type=F32)


def _dot_fine(a, b, dims):
    a_hi, b_hi = a.astype(BF16), b.astype(BF16)
    a_lo, b_lo = (a - a_hi.astype(F32)).astype(BF16), (b - b_hi.astype(F32)).astype(BF16)
    dn = (dims, ((), ()))
    return (lax.dot_general(a_hi, b_hi, dn, preferred_element_type=F32)
            + lax.dot_general(a_hi, b_lo, dn, preferred_element_type=F32)
            + lax.dot_general(a_lo, b_hi, dn, preferred_element_type=F32))


def _ssd_specs(nt, rev):
    def idx(c):
        return nt - 1 - c if rev else c
    bc_w = SSM_GROUPS * D_STATE
    xs = pl.BlockSpec((CHUNK, D_INNER), lambda c: (idx(c), 0))
    bm = pl.BlockSpec((CHUNK, bc_w), lambda c: (idx(c), D_INNER // bc_w))
    cm = pl.BlockSpec((CHUNK, bc_w), lambda c: (idx(c), D_INNER // bc_w + 1))
    dtr = pl.BlockSpec((CHUNK, SSM_GROUPS * 128), lambda c: (idx(c), OFF_DT // (SSM_GROUPS * 128)))
    par = _full((SSM_GROUPS, 1, 128))
    par_x = _full((SSM_GROUPS, 1, GROUP_W))
    return xs, bm, cm, dtr, par, par_x, idx


def _group_cols(g, width):
    return slice(g * width, (g + 1) * width)


def _ssd_fwd(xact, proj, dtb, alog, dskip_x):
    t_rows = xact.shape[0]
    nt = t_rows // CHUNK
    xs_spec, b_spec, c_spec, dtr_spec, par, par_x, _ = _ssd_specs(nt, False)

    def body(xs_ref, b_ref, c_ref, dtr_ref, dtb_ref, alog_ref, dsk_ref, y_ref, hst_ref, state):
        c = pl.program_id(0)

        @pl.when(c == 0)
        def _():
            state[...] = jnp.zeros_like(state)

        for g in range(SSM_GROUPS):
            wide, narrow = _group_cols(g, GROUP_W), _group_cols(g, D_STATE)
            cm = _ssd_common(dtr_ref[:, narrow], dtb_ref[g], alog_ref[g], c)
            xs, bm, cmat = xs_ref[:, wide], b_ref[:, narrow], c_ref[:, narrow]
            x_dt = xs * cm["dtx"]
            h_in = state[g]
            hst_ref[0, g] = h_in
            y_ref[:, wide] = _dot(cmat, h_in, ((1,), (0,))) * cm["growx"] + xs * dsk_ref[g]
            cb = _dot(cmat, bm, ((1,), (1,)))
            for j in range(HEADS_PER_GROUP):
                sl = slice(g * GROUP_W + j * HEAD_P, g * GROUP_W + (j + 1) * HEAD_P)
                y_ref[:, sl] += _dot(cb * _decay_matrix(cm, j), x_dt[:, j * HEAD_P:(j + 1) * HEAD_P], ((1,), (0,)))
            state[g] = h_in * cm["growx"][CHUNK - 1:CHUNK, :] + _dot_fine(bm, x_dt * cm["fadex"], ((0,), (0,)))

    return pl.pallas_call(
        body, grid=(nt,),
        in_specs=[xs_spec, b_spec, c_spec, dtr_spec, par, par, par_x],
        out_specs=[xs_spec, pl.BlockSpec((1, SSM_GROUPS, D_STATE, GROUP_W), lambda c: (c, 0, 0, 0))],
        out_shape=[_sds((t_rows, D_INNER), F32), _sds((nt, SSM_GROUPS, D_STATE, GROUP_W), F32)],
        scratch_shapes=[pltpu.VMEM((SSM_GROUPS, D_STATE, GROUP_W), F32)],
        name="ssd_fwd", compiler_params=_params(1))(xact, xact, xact, proj, dtb, alog, dskip_x)


def _ssd_bwd(xact, proj, dtb, alog, dskip_x, dy, hst, dproj):
    t_rows = xact.shape[0]
    nt = t_rows // CHUNK
    xs_spec, b_spec, c_spec, dtr_spec, par, par_x, idx = _ssd_specs(nt, True)
    h_spec = pl.BlockSpec((1, SSM_GROUPS, D_STATE, GROUP_W), lambda c: (idx(c), 0, 0, 0))
    hn_spec = pl.BlockSpec((1, SSM_GROUPS, D_STATE, GROUP_W), lambda c: (jnp.minimum(idx(c) + 1, nt - 1), 0, 0, 0))
    bc_out = pl.BlockSpec((CHUNK, SSM_GROUPS * D_STATE), lambda c: (idx(c), 0))

    def body(xs_ref, b_ref, c_ref, dtr_ref, dtb_ref, alog_ref, dsk_ref, dy_ref, h_ref, hn_ref, _,
             dxs_ref, db_ref, dc_ref, ddt_ref, dalog_ref, ddtb_ref, dd_ref, dstate, dx_buf):
        step = pl.program_id(0)

        @pl.when(step == 0)
        def _():
            dstate[...] = jnp.zeros_like(dstate)
            dalog_ref[...] = jnp.zeros_like(dalog_ref)
            ddtb_ref[...] = jnp.zeros_like(ddtb_ref)
            dd_ref[...] = jnp.zeros_like(dd_ref)

        for g in range(SSM_GROUPS):
            _ssd_bwd_group(g, idx(step), xs_ref, b_ref, c_ref, dtr_ref, dtb_ref, alog_ref, dsk_ref, dy_ref, h_ref, hn_ref,
                           dxs_ref, db_ref, dc_ref, ddt_ref, dalog_ref, ddtb_ref, dd_ref, dstate, dx_buf)

    return pl.pallas_call(
        body, grid=(nt,),
        in_specs=[xs_spec, b_spec, c_spec, dtr_spec, par, par, par_x, xs_spec, h_spec, hn_spec, _ANY],
        out_specs=[xs_spec, bc_out, bc_out, dtr_spec, par, par, par_x],
        out_shape=[_sds((t_rows, D_INNER), F32), _sds((t_rows, SSM_GROUPS * D_STATE), F32),
                   _sds((t_rows, SSM_GROUPS * D_STATE), F32), _sds(dproj.shape, dproj.dtype),
                   _sds((SSM_GROUPS, 1, 128), F32), _sds((SSM_GROUPS, 1, 128), F32), _sds((SSM_GROUPS, 1, GROUP_W), F32)],
        scratch_shapes=[pltpu.VMEM((SSM_GROUPS, D_STATE, GROUP_W), F32), pltpu.VMEM((CHUNK, GROUP_W), F32)],
        input_output_aliases={10: 3},
        name="ssd_bwd", compiler_params=_params(1))(xact, xact, xact, proj, dtb, alog, dskip_x, dy, hst, hst, dproj)


def _ssd_bwd_group(g, chunk, xs_ref, b_ref, c_ref, dtr_ref, dtb_ref, alog_ref, dsk_ref, dy_ref, h_ref, hn_ref,
                   dxs_ref, db_ref, dc_ref, ddt_ref, dalog_ref, ddtb_ref, dd_ref, dstate, dx_buf):
    wide, narrow = _group_cols(g, GROUP_W), _group_cols(g, D_STATE)
    cm = _ssd_common(dtr_ref[:, narrow], dtb_ref[g], alog_ref[g], chunk)
    xs, bm, cmat = xs_ref[:, wide], b_ref[:, narrow], c_ref[:, narrow]
    dsk = dsk_ref[g]
    x_dt = xs * cm["dtx"]
    h_in, h_next = h_ref[0, g], hn_ref[0, g]
    dyv = dy_ref[:, wide]
    dh = dstate[g]
    grow, fade = cm["growx"], cm["fadex"]
    dy_grow = dyv * grow
    x_fade = x_dt * fade
    cb = _dot(cmat, bm, ((1,), (1,)))
    ml = jnp.zeros((CHUNK, CHUNK), F32)
    row_id = lax.broadcasted_iota(jnp.int32, (CHUNK, CHUNK), 0)
    col_id = lax.broadcasted_iota(jnp.int32, (CHUNK, CHUNK), 1)
    w_rows = jnp.zeros((CHUNK, CHUNK), F32)
    w_cols = jnp.zeros((CHUNK, CHUNK), F32)
    for j in range(HEADS_PER_GROUP):
        sl = slice(j * HEAD_P, (j + 1) * HEAD_P)
        lm = _decay_matrix(cm, j)
        mlj = _dot(dyv[:, sl], x_dt[:, sl], ((1,), (1,))) * lm
        ml = ml + mlj
        wm = mlj * cb
        w_rows = jnp.where(col_id == j, jnp.sum(wm, axis=1, keepdims=True), w_rows)
        w_cols = jnp.where(row_id == j, jnp.sum(wm, axis=0, keepdims=True), w_cols)
        dx_buf[:, sl] = _dot(cb * lm, dyv[:, sl], ((0,), (0,)))
    dx_off = fade * _dot_fine(bm, dh, ((1,), (0,)))
    dx = dx_buf[...] + dx_off
    dc_ref[:, narrow] = _dot(ml, bm, ((1,), (0,))) + _dot(dy_grow, h_in, ((1,), (1,)))
    db_ref[:, narrow] = _dot(ml, cmat, ((0,), (0,))) + _dot(x_fade, dh, ((1,), (1,)))
    fold = cm["fold"]
    y_off = _dot_fine(cmat, h_in, ((1,), (0,))) * grow
    dcs = (w_rows - w_cols.T) + _dot01(dyv * y_off - x_dt * dx_off, fold, True, 2)
    tail = jnp.broadcast_to(jnp.sum(dh * h_next, axis=0, keepdims=True), (8, GROUP_W))
    tail = _dot01(tail, fold, True, 2)[0:1, :]
    last_row = lax.broadcasted_iota(jnp.int32, (CHUNK, 128), 0) == CHUNK - 1
    dcs = dcs + jnp.where(last_row, tail, 0.0)
    da = _dot01(dcs, cm["triu"], False, 3)
    ddt = da * cm["a_head"] + _dot01(dx * xs, fold, True, 2)
    ddt_raw = jnp.where(cm["live"], ddt * _sigmoid(cm["raw"]), 0.0)
    ddt_ref[:, narrow] = ddt_raw.astype(BF16)
    ddtb_ref[g] += jnp.sum(ddt_raw, axis=0, keepdims=True)
    dalog_ref[g] += jnp.sum(da * cm["dt"], axis=0, keepdims=True) * cm["a_head"]
    dd_ref[g] += jnp.sum(dyv * xs, axis=0, keepdims=True)
    dxs_ref[:, wide] = dx * cm["dtx"] + dyv * dsk
    dstate[g] = dh * grow[CHUNK - 1:CHUNK, :] + _dot_fine(cmat, dy_grow, ((0,), (0,)))


def _swa_bias():
    rows_q = ATTN_GROUP * CHUNK
    dist = (jnp.arange(rows_q) % CHUNK)[:, None] - jnp.arange(2 * CHUNK)[None, :] + CHUNK
    head = jnp.arange(KV_HEADS)[:, None] * ATTN_GROUP + jnp.arange(rows_q)[None, :] // CHUNK + 1
    slope = jnp.exp2(-8.0 * head.astype(F32) / ATTN_HEADS)
    return jnp.where((dist >= 0) & (dist < CHUNK), -slope[:, :, None] * dist.astype(F32)[None], NEG)


def _swa_probs(q_kv, k_prev, k_cur, k_first, sink, bias, n):
    rows_q = ATTN_GROUP * CHUNK
    qs = jnp.concatenate([q_kv[:, g * DH:(g + 1) * DH] for g in range(ATTN_GROUP)], axis=0) * (DH ** -0.5)
    kcat = jnp.concatenate([k_prev, k_cur], axis=0)
    kmeta = k_first[META_PAD:, :]
    key_ok = lax.broadcasted_iota(jnp.int32, (1, 2 * CHUNK), 1) + n * CHUNK >= 2 * CHUNK
    s_band = jnp.where(key_ok, _dot(qs, kcat, ((1,), (1,))) + bias, NEG)
    q_pos = lax.broadcasted_iota(jnp.int32, (rows_q, N_META), 0) % CHUNK + n * CHUNK - META_PAD
    ok_m = lax.broadcasted_iota(jnp.int32, (rows_q, N_META), 1) <= q_pos
    s_meta = jnp.where(ok_m, _dot(qs, kmeta, ((1,), (1,))), NEG)
    m = jnp.maximum(jnp.maximum(jnp.max(s_band, axis=1, keepdims=True), jnp.max(s_meta, axis=1, keepdims=True)), sink)
    p_band, p_meta, p_sink = jnp.exp(s_band - m), jnp.exp(s_meta - m), jnp.exp(sink - m)
    inv = 1.0 / (jnp.sum(p_band, axis=1, keepdims=True) + jnp.sum(p_meta, axis=1, keepdims=True) + p_sink)
    return qs, kcat, kmeta, p_band * inv, p_meta * inv, p_sink * inv


def _swa_specs(nt, rev):
    def idx(n):
        return nt - 1 - n if rev else n
    o = pl.BlockSpec((CHUNK, ATTN_HEADS * DH), lambda n: (idx(n), 0))
    chunks = (lambda c: jnp.maximum(c - 1, 0)), (lambda c: c), (lambda c: 0)
    qkv = [pl.BlockSpec((CHUNK, QKV_W), lambda n, f=f: (f(idx(n)), OFF_Q // QKV_W)) for f in chunks]
    sink = _full((KV_HEADS, ATTN_GROUP * CHUNK, 1))
    bias = _full((KV_HEADS, ATTN_GROUP * CHUNK, 2 * CHUNK))
    return o, qkv, sink, bias, idx


def _head_cols(k):
    kv_w = ATTN_GROUP * DH
    q0, k0, v0 = k * kv_w, OFF_K - OFF_Q + k * DH, OFF_V - OFF_Q + k * DH
    return slice(q0, q0 + kv_w), slice(k0, k0 + DH), slice(v0, v0 + DH)


def _swa_fwd(proj, sink_rows, bias):
    t_rows = proj.shape[0]
    nt = t_rows // CHUNK
    o_spec, qkv_specs, sink_spec, bias_spec, _ = _swa_specs(nt, False)
    kv_w = ATTN_GROUP * DH

    def body(prev_ref, cur_ref, first_ref, sink_ref, bias_ref, o_ref):
        n = pl.program_id(0)
        for k in range(KV_HEADS):
            qc, kc, vc = _head_cols(k)
            _, _, _, p_band, p_meta, _ = _swa_probs(cur_ref[:, qc], prev_ref[:, kc], cur_ref[:, kc], first_ref[:, kc],
                                                    sink_ref[k], bias_ref[k], n)
            vcat = jnp.concatenate([prev_ref[:, vc], cur_ref[:, vc]], axis=0)
            out = _dot(p_band, vcat, ((1,), (0,))) + _dot(p_meta, first_ref[:, vc][META_PAD:, :], ((1,), (0,)))
            for g in range(ATTN_GROUP):
                o_ref[:, k * kv_w + g * DH:k * kv_w + (g + 1) * DH] = out[g * CHUNK:(g + 1) * CHUNK, :]

    return pl.pallas_call(
        body, grid=(nt,), in_specs=qkv_specs + [sink_spec, bias_spec],
        out_specs=o_spec, out_shape=_sds((t_rows, ATTN_HEADS * DH), F32),
        name="swa_fwd", compiler_params=_params(1))(proj, proj, proj, sink_rows, bias)


def _swa_bwd(proj, sink_rows, bias, out, dout, dproj):
    t_rows = proj.shape[0]
    nt = t_rows // CHUNK
    o_spec, qkv_specs, sink_spec, bias_spec, idx = _swa_specs(nt, True)
    kv_w = ATTN_GROUP * DH
    k_off, v_off = OFF_K - OFF_Q, OFF_V - OFF_Q

    def body(prev_ref, cur_ref, first_ref, sink_ref, bias_ref, o_ref, do_ref, _, dqkv_ref, dsink_ref,
             carry_k, carry_v, meta_k, meta_v, dqkv_buf):
        step = pl.program_id(0)
        n = idx(step)

        @pl.when(step == 0)
        def _():
            carry_k[...] = jnp.zeros_like(carry_k)
            carry_v[...] = jnp.zeros_like(carry_v)
            meta_k[...] = jnp.zeros_like(meta_k)
            meta_v[...] = jnp.zeros_like(meta_v)
            dsink_ref[...] = jnp.zeros_like(dsink_ref)

        for k in range(KV_HEADS):
            cols = slice(k * kv_w, (k + 1) * kv_w)
            hd = slice(k * DH, (k + 1) * DH)
            qc, kc, vc = _head_cols(k)
            qs, kcat, kmeta, p_band, p_meta, p_sink = _swa_probs(cur_ref[:, qc], prev_ref[:, kc], cur_ref[:, kc],
                                                                 first_ref[:, kc], sink_ref[k], bias_ref[k], n)
            vcat = jnp.concatenate([prev_ref[:, vc], cur_ref[:, vc]], axis=0)
            vmeta = first_ref[:, vc][META_PAD:, :]
            o, do = o_ref[:, cols], do_ref[:, cols]
            os_ = jnp.concatenate([o[:, g * DH:(g + 1) * DH] for g in range(ATTN_GROUP)], axis=0)
            dos = jnp.concatenate([do[:, g * DH:(g + 1) * DH] for g in range(ATTN_GROUP)], axis=0)
            delta = jnp.sum(dos * os_, axis=1, keepdims=True)
            ds_band = p_band * (_dot(dos, vcat, ((1,), (1,))) - delta)
            ds_meta = p_meta * (_dot(dos, vmeta, ((1,), (1,))) - delta)
            ds_sink = -p_sink * delta
            dqs = (_dot(ds_band, kcat, ((1,), (0,))) + _dot(ds_meta, kmeta, ((1,), (0,)))) * (DH ** -0.5)
            for g in range(ATTN_GROUP):
                dqkv_buf[:, k * kv_w + g * DH:k * kv_w + (g + 1) * DH] = dqs[g * CHUNK:(g + 1) * CHUNK, :]
                dsink_ref[k, g:g + 1, :] += jnp.sum(ds_sink[g * CHUNK:(g + 1) * CHUNK, :])
            dkcat = _dot(ds_band, qs, ((0,), (0,)))
            dvcat = _dot(p_band, dos, ((0,), (0,)))
            meta_k[:, hd] += _dot(ds_meta, qs, ((0,), (0,)))
            meta_v[:, hd] += _dot(p_meta, dos, ((0,), (0,)))
            dqkv_buf[:, kc] = dkcat[CHUNK:, :] + carry_k[:, hd]
            dqkv_buf[:, vc] = dvcat[CHUNK:, :] + carry_v[:, hd]
            carry_k[:, hd] = dkcat[:CHUNK, :]
            carry_v[:, hd] = dvcat[:CHUNK, :]

        @pl.when(n == 0)
        def _():
            dqkv_buf[META_PAD:, k_off:k_off + KV_W] += meta_k[...]
            dqkv_buf[META_PAD:, v_off:v_off + KV_W] += meta_v[...]

        dqkv_ref[...] = dqkv_buf[...].astype(BF16)

    return pl.pallas_call(
        body, grid=(nt,),
        in_specs=qkv_specs + [sink_spec, bias_spec, o_spec, o_spec, pl.BlockSpec(memory_space=pl.ANY)],
        out_specs=[qkv_specs[1], _full((KV_HEADS, 8, 128))],
        out_shape=[_sds(dproj.shape, dproj.dtype), _sds((KV_HEADS, 8, 128), F32)],
        scratch_shapes=[pltpu.VMEM((CHUNK, KV_W), F32), pltpu.VMEM((CHUNK, KV_W), F32),
                        pltpu.VMEM((N_META, KV_W), F32), pltpu.VMEM((N_META, KV_W), F32),
                        pltpu.VMEM((CHUNK, QKV_W), F32)],
        input_output_aliases={7: 0},
        name="swa_bwd", compiler_params=_params(1))(proj, proj, proj, sink_rows, bias, out, dout, dproj)


def _pack_w_in_t(w_in_t):
    w_dt = w_in_t[CUT_DT:CUT_Q].reshape(SSM_GROUPS, HEADS_PER_GROUP, D_MODEL)
    w_dt = jnp.pad(w_dt, ((0, 0), (0, 128 - HEADS_PER_GROUP), (0, 0))).reshape(SSM_GROUPS * 128, D_MODEL)
    return jnp.concatenate([w_in_t[CUT_Z:CUT_XBC], w_in_t[CUT_G:], w_dt, w_in_t[CUT_Q:CUT_G], w_in_t[CUT_XBC:CUT_DT]], axis=0)


def _unpack_w_in_t(wp_t):
    w_dt = wp_t[OFF_DT:OFF_Q].reshape(SSM_GROUPS, 128, D_MODEL)[:, :HEADS_PER_GROUP].reshape(SSM_HEADS, D_MODEL)
    return jnp.concatenate([wp_t[OFF_Z:OFF_GATE], wp_t[OFF_XBC:], w_dt, wp_t[OFF_Q:OFF_XBC], wp_t[OFF_GATE:OFF_DT]], axis=0)


def _group_rows(v, width):
    return jnp.pad(v.reshape(SSM_GROUPS, 1, HEADS_PER_GROUP), ((0, 0), (0, 0), (0, width - HEADS_PER_GROUP)))


def _local_step(x, target, wt, late_weights=None, on_grad=None, started=None):
    seq = x.shape[0]
    grads = {}

    def emit(name, g):
        grads[name] = g
        return None if on_grad is None else on_grad(name, g)
    meta = wt["meta_tokens"]
    wp_t = _pack_w_in_t(wt["w_in_t"])
    dtb = _group_rows(wt["ssm_dt_bias"].reshape(-1), 128)
    alog = _group_rows(wt["ssm_a_log"].reshape(-1), 128)
    dskip_x = jnp.repeat(wt["ssm_d_skip"].reshape(-1), HEAD_P).reshape(SSM_GROUPS, 1, GROUP_W)
    sink_rows = jnp.repeat(wt["attn_sinks"].reshape(KV_HEADS, ATTN_GROUP), CHUNK, axis=1).reshape(KV_HEADS, ATTN_GROUP * CHUNK, 1)

    hn = _prenorm(x, meta, wt["norm_pre_mix"])
    proj = _matmul(hn, wp_t, tb=True, name="in_proj", after=started)
    xc, xact = _ssm_conv_fwd(proj, wt["ssm_conv_w"], wt["ssm_conv_b"])
    y, hst = _ssd_fwd(xact, proj, dtb, alog, dskip_x)
    yn = _ssm_post(y, proj, wt["ssm_norm"])
    if late_weights is not None:
        wt = {**wt, **late_weights(yn)}
    y_ssm = _matmul(yn, wt["w_ssm_out"], name="ssm_out")
    bias = _swa_bias()
    attn = _swa_fwd(proj, sink_rows, bias)
    y_attn = _matmul(attn, wt["w_attn_out"], name="attn_out")
    mixed = _mix_fwd(proj, y_ssm, y_attn)
    mix = _matmul(mixed, wt["w_mix_out"], name="mix_out")
    h1, hn2 = _postmix(x, meta, mix, wt["norm_post_mix"], wt["norm_pre_ffn"])
    up = _matmul(hn2, wt["w_ffn_up_t"], tb=True, out_dtype=BF16, name="ffn_up")
    u, act = _ffn_act(up, wt["ffn_conv_w"], wt["ffn_conv_b"])
    f = _matmul(act, wt["w_ffn_down"], name="ffn_down")
    df, dy, g_norm_post_ffn, loss_row = _final(h1, f, target, wt["norm_post_ffn"])

    grads["norm_post_ffn"] = g_norm_post_ffn
    sent = emit("w_ffn_down", _matmul(act, df, ta=True, out_dtype=BF16, name="dw_ffn_down"))
    dact = _matmul(df, wt["w_ffn_down"], tb=True, out_dtype=BF16, name="d_act", after=sent)
    dup, grads["ffn_conv_w"], grads["ffn_conv_b"] = _ffn_act_bwd(u, up, dact, wt["ffn_conv_w"])
    sent = emit("w_ffn_up_t", _matmul(dup, hn2, ta=True, out_dtype=BF16, name="dw_ffn_up"))
    dhn2 = _matmul(dup, wt["w_ffn_up_t"], name="d_hn2", after=sent)
    dmix, dh, grads["norm_pre_ffn"], grads["norm_post_mix"] = _postmix_bwd(h1, dhn2, dy, mix, wt["norm_pre_ffn"], wt["norm_post_mix"])
    sent = emit("w_mix_out", _matmul(mixed, dmix, ta=True, out_dtype=BF16, name="dw_mix_out"))
    dmixed = _matmul(dmix, wt["w_mix_out"], tb=True, name="d_mixed", after=sent)
    dy_ssm, dy_attn, dproj = _mix_bwd(dmixed, proj, y_ssm, y_attn, lax.empty(proj.shape, BF16))
    sent = emit("w_ssm_out", _matmul(yn, dy_ssm, ta=True, out_dtype=BF16, name="dw_ssm_out"))
    dyn = _matmul(dy_ssm, wt["w_ssm_out"], tb=True, out_dtype=BF16, name="d_yn", after=sent)
    sent = emit("w_attn_out", _matmul(attn, dy_attn, ta=True, out_dtype=BF16, name="dw_attn_out"))
    dattn = _matmul(dy_attn, wt["w_attn_out"], tb=True, name="d_attn", after=sent)
    dy_ssd, dproj, grads["ssm_norm"] = _ssm_post_bwd(y, proj, dyn, wt["ssm_norm"], dproj)
    dxs, dbm, dcm, dproj, dalog, ddtb, dd_x = _ssd_bwd(xact, proj, dtb, alog, dskip_x, dy_ssd, hst, dproj)
    grads["ssm_a_log"] = dalog[:, 0, :HEADS_PER_GROUP].reshape(1, SSM_HEADS)
    grads["ssm_dt_bias"] = ddtb[:, 0, :HEADS_PER_GROUP].reshape(1, SSM_HEADS)
    grads["ssm_d_skip"] = dd_x.reshape(SSM_HEADS, HEAD_P).sum(axis=1).reshape(1, SSM_HEADS)
    dproj, grads["ssm_conv_w"], grads["ssm_conv_b"] = _ssm_conv_bwd(xc, proj, dxs, dbm, dcm, wt["ssm_conv_w"], dproj)
    dproj, dsink = _swa_bwd(proj, sink_rows, bias, attn, dattn, dproj)
    grads["attn_sinks"] = dsink[:, :ATTN_GROUP, 0].reshape(1, ATTN_HEADS)
    sent = emit("w_in_t", _unpack_w_in_t(_matmul(dproj, hn, ta=True, out_dtype=BF16, name="dw_in")))
    dhn = _matmul(dproj, wp_t, name="d_hn", after=sent)
    grad_x, grads["meta_tokens"], grads["norm_pre_mix"] = _prenorm_bwd(x, meta, dhn, dh, wt["norm_pre_mix"])
    return loss_row[0, 0], grad_x, grads


def _all_gather(shards):
    n = len(shards)

    def body(*refs):
        ins, outs = refs[:n], refs[n:2 * n]
        send_sems, recv_sems, local_sems = refs[2 * n:]
        x, y, c = lax.axis_index("x"), lax.axis_index("y"), lax.axis_index("c")
        me, sibling = (x, y, c), (x, y, 1 - c)
        chips = [(1 - x, y), (x, 1 - y), (1 - x, 1 - y)]

        def slot(a, dev):
            return outs[a].at[4 * dev[0] + 2 * dev[1] + dev[2]]

        def copy(k, a, block, to, src=None):
            return pltpu.make_async_remote_copy(
                src_ref=slot(a, block) if src is None else src, dst_ref=slot(a, block),
                send_sem=send_sems.at[k, a], recv_sem=recv_sems.at[k, a],
                device_id=to, device_id_type=pl.DeviceIdType.MESH)

        mine = [pltpu.make_async_copy(ins[a], slot(a, me), local_sems.at[a]) for a in range(n)]
        for cp in mine:
            cp.start()
        first = [copy(0, a, me, sibling, src=ins[a]) for a in range(n)]
        for j, chip in enumerate(chips):
            first += [copy(1 + j, a, me, (*chip, c), src=ins[a]) for a in range(n)]
        for cp in first:
            cp.start()
        passed = []
        for j, chip in enumerate(chips):
            for a in range(n):
                copy(1 + j, a, (*chip, c), me).wait_recv()
                fwd = copy(4 + j, a, (*chip, c), sibling)
                fwd.start()
                passed.append(fwd)
        for a in range(n):
            copy(0, a, sibling, me).wait_recv()
        for j, chip in enumerate(chips):
            for a in range(n):
                copy(4 + j, a, (*chip, 1 - c), me).wait_recv()
        for cp in first + passed:
            cp.wait_send()
        for cp in mine:
            cp.wait()

    hbm = pl.BlockSpec(memory_space=pl.ANY)
    return pl.pallas_call(
        body, in_specs=[hbm] * n, out_specs=[hbm] * n,
        out_shape=[_sds((N_DEV,) + s.shape, s.dtype) for s in shards],
        scratch_shapes=[pltpu.SemaphoreType.DMA((7, n)), pltpu.SemaphoreType.DMA((7, n)), pltpu.SemaphoreType.DMA((n,))],
        name="gather_weights")(*shards)


def _peer_table():
    x, y, c = lax.axis_index("x"), lax.axis_index("y"), lax.axis_index("c")
    peers = []
    for k in range(N_DEV - 1):
        bits = k + 1
        p = (x ^ ((bits >> 2) & 1), y ^ ((bits >> 1) & 1), c ^ (bits & 1))
        peers.append((k, p, 4 * p[0] + 2 * p[1] + p[2]))
    return 4 * x + 2 * y + c, peers


_HBM = pl.BlockSpec(memory_space=pltpu.HBM)
_SEM = pl.BlockSpec(memory_space=pltpu.SEMAPHORE)
_EFFECT = pltpu.SideEffectType.DATAFLOW_SIDE_EFFECTING


def _push_copy(src, land, send_sems, recv_sems, a, k, p, src_slot, dst_slot):
    sem = a * (N_DEV - 1) + k
    return pltpu.make_async_remote_copy(
        src_ref=src[a] if src_slot is None else src[a].at[src_slot], dst_ref=land[a].at[dst_slot],
        send_sem=send_sems.at[sem], recv_sem=recv_sems.at[sem], device_id=p, device_id_type=pl.DeviceIdType.MESH)


def _push_start(srcs, scatter, name):
    n = len(srcs)
    lands = [lax.empty(s.shape if scatter else (N_DEV,) + s.shape, s.dtype) for s in srcs]

    def body(*refs):
        src, land = refs[:n], refs[n:2 * n]
        send_sems, recv_sems, token = refs[2 * n], refs[2 * n + 1], refs[-1]
        my_id, peers = _peer_table()
        for a in range(n):
            for k, p, p_id in peers:
                _push_copy(src, land, send_sems, recv_sems, a, k, p, p_id if scatter else None, my_id).start()
        token[...] = jnp.zeros_like(token)

    sems = pltpu.SemaphoreType.DMA(((N_DEV - 1) * n,))
    res = pl.pallas_call(
        body, name=name,
        out_shape=(sems, sems, *[pltpu.HBM(a.shape, a.dtype) for a in srcs + lands], _sds((8, 128), F32)),
        in_specs=[_HBM] * (2 * n), out_specs=(_SEM, _SEM, *[_HBM] * (2 * n), pl.BlockSpec(memory_space=pltpu.VMEM)),
        input_output_aliases={i: 2 + i for i in range(2 * n)},
        compiler_params=pltpu.CompilerParams(has_side_effects=_EFFECT),
    )(*[pltpu.with_memory_space_constraint(a, pltpu.HBM) for a in srcs + lands])
    return dict(send=res[0], recv=res[1], src=list(res[2:2 + n]), land=list(res[2 + n:2 + 2 * n]), token=res[-1],
                scatter=scatter)


def _push_wait(handle, after, name):
    n = len(handle["src"])
    scatter = handle["scatter"]

    def body(*refs):
        src, land = refs[:n], refs[n:2 * n]
        send_sems, recv_sems = refs[2 * n], refs[2 * n + 1]
        _, peers = _peer_table()
        for a in range(n):
            for k, p, p_id in peers:
                cp = _push_copy(src, land, send_sems, recv_sems, a, k, p, p_id if scatter else None, p_id)
                cp.wait_send()
                cp.wait_recv()

    arrays = handle["src"] + handle["land"]
    res = pl.pallas_call(
        body, name=name, out_shape=tuple(pltpu.HBM(a.shape, a.dtype) for a in arrays),
        in_specs=[_HBM] * (2 * n) + [_SEM, _SEM, pl.BlockSpec(memory_space=pl.ANY)], out_specs=tuple([_HBM] * (2 * n)),
        input_output_aliases={i: i for i in range(2 * n)},
        compiler_params=pltpu.CompilerParams(has_side_effects=_EFFECT),
    )(*arrays, handle["send"], handle["recv"], after)
    return list(res[:n]), list(res[n:])


def _slot_sum(p_ref, own_ref):
    if own_ref is not None:
        my_id = 4 * lax.axis_index("x") + 2 * lax.axis_index("y") + lax.axis_index("c")
        mine = own_ref[...].astype(F32)
    g = None
    for s in range(p_ref.shape[0]):
        term = p_ref[s].astype(F32)
        if own_ref is not None:
            term = jnp.where(my_id == s, mine, term)
        g = term if g is None else g + term
    return g


def _to_bf16(arrays):
    n = len(arrays)

    def body(*refs):
        for i in range(n):
            refs[n + i][...] = refs[i][...].astype(BF16)

    return pl.pallas_call(body, out_shape=[_sds(a.shape, BF16) for a in arrays], name="weights_to_bf16",
                          compiler_params=pltpu.CompilerParams(vmem_limit_bytes=VMEM_LIMIT))(*arrays)


def _adamw(parts, own, w, m, v, name):
    unit_rows = w.ndim == 3
    rows, cols = w.shape[0], w.shape[-1]
    if rows % 16 == 0:
        tr, tc = _pick(rows, (256, 128, 176, 64, 32, 16)), cols
    else:
        tr, tc = rows, _pick(cols, (256, 128))

    def body(*refs):
        if own is None:
            p_ref, w_ref, m_ref, v_ref, g_ref, d_ref, nm_ref, nv_ref = refs
            own_ref = None
        else:
            p_ref, own_ref, w_ref, m_ref, v_ref, g_ref, d_ref, nm_ref, nv_ref = refs
        g = _slot_sum(p_ref, own_ref)
        if unit_rows:
            g = g.reshape(tr, 1, tc)
        m_new = ADAM_B1 * m_ref[...] + (1.0 - ADAM_B1) * g
        v_new = ADAM_B2 * v_ref[...] + (1.0 - ADAM_B2) * (g * g)
        m_hat = m_new / (1.0 - ADAM_B1 ** ADAM_STEP)
        v_hat = v_new / (1.0 - ADAM_B2 ** ADAM_STEP)
        g_ref[...] = g
        d_ref[...] = -ADAM_LR * (m_hat / (jnp.sqrt(v_hat) + ADAM_EPS) + ADAM_WD * w_ref[...])
        nm_ref[...] = m_new
        nv_ref[...] = v_new

    by_rows = tc == cols
    spec = pl.BlockSpec((tr, tc), (lambda i: (i, 0)) if by_rows else (lambda i: (0, i)))
    state_spec = spec if not unit_rows else pl.BlockSpec((tr, 1, tc), (lambda i: (i, 0, 0)) if by_rows else (lambda i: (0, 0, i)))
    parts_spec = pl.BlockSpec((parts.shape[0], tr, tc), (lambda i: (0, i, 0)) if by_rows else (lambda i: (0, 0, i)))
    operands = (parts, w, m, v) if own is None else (parts, own, w, m, v)
    return pl.pallas_call(
        body, grid=(rows // tr if by_rows else cols // tc,),
        in_specs=[parts_spec] + ([] if own is None else [spec]) + [state_spec] * 3,
        out_specs=[state_spec] * 4, out_shape=[_sds(w.shape, F32)] * 4,
        name=name, compiler_params=_params(1))(*operands)


SMALL_REPLICATED = (("norm_pre_mix", 1024), ("ssm_conv_b", 3072), ("ssm_dt_bias", 32), ("ssm_a_log", 32),
                    ("ssm_d_skip", 32), ("ssm_norm", 2048), ("attn_sinks", 16), ("norm_post_mix", 1024),
                    ("norm_pre_ffn", 1024), ("ffn_conv_b", 5632), ("norm_post_ffn", 1024))
SMALL_SHARDED = (("meta_tokens", (N_META, D_MODEL // N_DEV)), ("ssm_conv_w", (SSM_CONV, CONV_DIM // N_DEV)),
                 ("ffn_conv_w", (FFN_CONV, 2 * FFN_DIM // N_DEV)))
BIG = (("w_in", (D_MODEL, N_IN // N_DEV), 1), ("w_ssm_out", (D_INNER // N_DEV, D_MODEL), 0),
       ("w_attn_out", (D_MODEL // N_DEV, D_MODEL), 0), ("w_mix_out", (D_MODEL // N_DEV, D_MODEL), 0),
       ("w_ffn_up", (D_MODEL, 2 * FFN_DIM // N_DEV), 1), ("w_ffn_down", (FFN_DIM // N_DEV, D_MODEL), 0))


def _rows_of(size):
    return -(-size // 128)


def _as_rows(flat):
    size = flat.shape[-1]
    rows = _rows_of(size)
    flat = jnp.pad(flat, [(0, 0)] * (flat.ndim - 1) + [(0, rows * 128 - size)])
    return flat.reshape(flat.shape[:-1] + (rows, 128))


def _pack_small(rep, sharded):
    pieces = [_as_rows(rep[name].reshape(-1)) for name, _ in SMALL_REPLICATED]
    pieces += [_as_rows(sharded[name].reshape(-1)) for name, _ in SMALL_SHARDED]
    packed = jnp.concatenate(pieces, axis=0)
    return jnp.pad(packed, ((0, -packed.shape[0] % 8), (0, 0)))


def _unpack_small(packed):
    out, row = {}, 0
    for name, size in SMALL_REPLICATED:
        out[name] = packed[row:row + _rows_of(size)].reshape(-1)[:size].reshape(1, size)
        row += _rows_of(size)
    for name, (r, c) in SMALL_SHARDED:
        out[name] = packed[row:row + _rows_of(r * c)].reshape(-1)[:r * c].reshape(r, c)
        row += _rows_of(r * c)
    return out


def _shard_major(g, shape, axis):
    r, c = shape
    if axis == 0:
        return g.reshape(N_DEV, r, c)
    return g.reshape(r, N_DEV, c).transpose(1, 0, 2)


def kernel(x, meta_tokens, norm_pre_mix, w_in, ssm_conv_w, ssm_conv_b, ssm_dt_bias, ssm_a_log, ssm_d_skip, ssm_norm, w_ssm_out, attn_sinks, w_attn_out, w_mix_out, norm_post_mix, norm_pre_ffn, w_ffn_up, ffn_conv_w, ffn_conv_b, w_ffn_down, norm_post_ffn, loss_target, m_meta_tokens, m_norm_pre_mix, m_w_in, m_ssm_conv_w, m_ssm_conv_b, m_ssm_dt_bias, m_ssm_a_log, m_ssm_d_skip, m_ssm_norm, m_w_ssm_out, m_attn_sinks, m_w_attn_out, m_w_mix_out, m_norm_post_mix, m_norm_pre_ffn, m_w_ffn_up, m_ffn_conv_w, m_ffn_conv_b, m_w_ffn_down, m_norm_post_ffn, v_meta_tokens, v_norm_pre_mix, v_w_in, v_ssm_conv_w, v_ssm_conv_b, v_ssm_dt_bias, v_ssm_a_log, v_ssm_d_skip, v_ssm_norm, v_w_ssm_out, v_attn_sinks, v_w_attn_out, v_w_mix_out, v_norm_post_mix, v_norm_pre_ffn, v_w_ffn_up, v_ffn_conv_w, v_ffn_conv_b, v_w_ffn_down, v_norm_post_ffn):
    names = ("meta_tokens", "norm_pre_mix", "w_in", "ssm_conv_w", "ssm_conv_b", "ssm_dt_bias", "ssm_a_log", "ssm_d_skip",
             "ssm_norm", "w_ssm_out", "attn_sinks", "w_attn_out", "w_mix_out", "norm_post_mix", "norm_pre_ffn", "w_ffn_up",
             "ffn_conv_w", "ffn_conv_b", "w_ffn_down", "norm_post_ffn")
    w_loc = dict(zip(names, (meta_tokens, norm_pre_mix, w_in, ssm_conv_w, ssm_conv_b, ssm_dt_bias, ssm_a_log, ssm_d_skip,
                             ssm_norm, w_ssm_out, attn_sinks, w_attn_out, w_mix_out, norm_post_mix, norm_pre_ffn, w_ffn_up,
                             ffn_conv_w, ffn_conv_b, w_ffn_down, norm_post_ffn)))
    m_loc = dict(zip(names, (m_meta_tokens, m_norm_pre_mix, m_w_in, m_ssm_conv_w, m_ssm_conv_b, m_ssm_dt_bias, m_ssm_a_log,
                             m_ssm_d_skip, m_ssm_norm, m_w_ssm_out, m_attn_sinks, m_w_attn_out, m_w_mix_out, m_norm_post_mix,
                             m_norm_pre_ffn, m_w_ffn_up, m_ffn_conv_w, m_ffn_conv_b, m_w_ffn_down, m_norm_post_ffn)))
    v_loc = dict(zip(names, (v_meta_tokens, v_norm_pre_mix, v_w_in, v_ssm_conv_w, v_ssm_conv_b, v_ssm_dt_bias, v_ssm_a_log,
                             v_ssm_d_skip, v_ssm_norm, v_w_ssm_out, v_attn_sinks, v_w_attn_out, v_w_mix_out, v_norm_post_mix,
                             v_norm_pre_ffn, v_w_ffn_up, v_ffn_conv_w, v_ffn_conv_b, v_w_ffn_down, v_norm_post_ffn)))

    def local2d(d, name):
        a = d[name]
        return a if name == "meta_tokens" else a.reshape(a.shape[1:])

    def turned2d(d, name):
        a = jnp.swapaxes(d[name], 1, 2)
        return a.reshape(a.shape[1:])

    my_id = 4 * lax.axis_index("x") + 2 * lax.axis_index("y") + lax.axis_index("c")
    big = {name: (shape, axis) for name, shape, axis in BIG}

    def whole(name, g):
        return g.reshape(N_DEV * g.shape[1], g.shape[2])

    def key(name):
        return name + "_t" if big[name][1] == 1 else name

    by_rows = [name for name, _, axis in BIG if axis == 0]
    send_bf16 = dict(zip(by_rows, _to_bf16([local2d(w_loc, name) for name in by_rows])))
    for name, _, axis in BIG:
        if axis == 1:
            send_bf16[name] = turned2d(w_loc, name).astype(BF16)
    small_shard_pack = jnp.concatenate([_as_rows(local2d(w_loc, name).reshape(-1)) for name, _ in SMALL_SHARDED], axis=0)
    small_shard_pack = jnp.pad(small_shard_pack, ((0, -small_shard_pack.shape[0] % 8), (0, 0)))
    first = _all_gather([send_bf16["w_in"], small_shard_pack])
    rest_names = [name for name, _, _ in BIG if name != "w_in"]
    rest = [send_bf16[name] for name in rest_names]
    rest, first = lax.optimization_barrier((rest, first))
    rest_handle = _push_start(rest, False, "gather_rest_start")
    wt = {"w_in_t": whole("w_in", first[0])}
    row = 0
    for name, (r, c) in SMALL_SHARDED:
        blocks = first[1][:, row:row + _rows_of(r * c)].reshape(N_DEV, -1)[:, :r * c].reshape(N_DEV, r, c)
        wt[name] = blocks.transpose(1, 0, 2).reshape(r, N_DEV * c)
        row += _rows_of(r * c)
    for name, size in SMALL_REPLICATED:
        wt[name] = w_loc[name].reshape(1, size)

    def late_weights(after):
        own, landed = _push_wait(rest_handle, after, "gather_rest_wait")
        out = {}
        for name, mine, land in zip(rest_names, own, landed):
            out[key(name)] = whole(name, lax.dynamic_update_index_in_dim(land, mine, my_id, 0))
        return out

    sent = {}

    def on_grad(known_as, g):
        name = known_as.removesuffix("_t")
        by_owner = g.reshape(N_DEV, g.shape[0] // N_DEV, g.shape[1])
        sent[name] = _push_start([by_owner], True, "send_" + name)
        return sent[name]["token"]

    loss_part, grad_x, grads = _local_step(x[0], loss_target[0], wt, late_weights, on_grad, rest_handle["token"])
    loss = lax.psum(loss_part, AXES)

    small_parts = []
    for name, (r, c) in SMALL_SHARDED:
        small_parts.append(_as_rows(_shard_major(grads[name], (r, c), 1).reshape(N_DEV, r * c)))
    rep_rows = jnp.concatenate([_as_rows(grads[name].reshape(-1)) for name, _ in SMALL_REPLICATED], axis=0)
    small_send = jnp.concatenate([jnp.broadcast_to(rep_rows[None], (N_DEV,) + rep_rows.shape)] + small_parts, axis=1)
    small_send = jnp.pad(small_send, ((0, 0), (0, -small_send.shape[1] % 8), (0, 0)))
    small_handle = _push_start([small_send], True, "send_small")

    def small_pack(d):
        return _pack_small({name: d[name] for name, _ in SMALL_REPLICATED}, {name: local2d(d, name) for name, _ in SMALL_SHARDED})

    def arrived(handle, after, name):
        src, landed = _push_wait(handle, after, "arrived_" + name)
        return landed[0], lax.dynamic_index_in_dim(src[0], my_id, 0, keepdims=False)

    grad_w, delta_w, new_m, new_v = {}, {}, {}, {}
    outs = None
    after = small_handle["token"]
    for name, handle in sent.items():
        if name == "w_in":
            parts, own = arrived(small_handle, after, "small")
            outs = _adamw(parts, own, small_pack(w_loc), small_pack(m_loc), small_pack(v_loc), "adamw_small")
            after = outs[0]
        parts, own = arrived(handle, after, name)
        turned = big[name][1] == 1
        unit_rows = turned and big[name][0][1] % 8 != 0
        if unit_rows:
            state = [jnp.transpose(d[name], (2, 0, 1)) for d in (w_loc, m_loc, v_loc)]
        else:
            state = [turned2d(d, name) if turned else local2d(d, name) for d in (w_loc, m_loc, v_loc)]
        results = _adamw(parts, own, *state, "adamw_" + name)
        after = results[0]
        full = (1,) + big[name][0]
        for dst, a in zip((grad_w, delta_w, new_m, new_v), results):
            if unit_rows:
                dst[name] = jnp.transpose(a, (1, 2, 0))
            else:
                dst[name] = jnp.swapaxes(a[None], 1, 2) if turned else a.reshape(full)
    for dst, packed in zip((grad_w, delta_w, new_m, new_v), outs):
        for name, a in _unpack_small(packed).items():
            dst[name] = a.reshape(w_loc[name].shape)

    return (loss, grad_x[None], *[grad_w[n] for n in names], *[delta_w[n] for n in names],
            *[new_m[n] for n in names], *[new_v[n] for n in names])
```

```python
import jax
import jax.numpy as jnp
from jax import lax
from jax.experimental import pallas as pl
from jax.experimental.pallas import tpu as pltpu

F32 = jnp.float32
BF16 = jnp.bfloat16

D_MODEL = 1024
N_META = 16
CHUNK = 128
META_PAD = CHUNK - N_META
D_INNER = 2048
HEAD_P = 64
SSM_HEADS = 32
SSM_GROUPS = 4
HEADS_PER_GROUP = SSM_HEADS // SSM_GROUPS
GROUP_W = HEADS_PER_GROUP * HEAD_P
D_STATE = 128
SSM_CONV = 4
CONV_DIM = D_INNER + 2 * SSM_GROUPS * D_STATE
ATTN_HEADS = 16
KV_HEADS = 4
ATTN_GROUP = ATTN_HEADS // KV_HEADS
DH = 64
KV_W = KV_HEADS * DH
FFN_DIM = 2816
FFN_CONV = 3
EPS = 1e-6
NEG = -1e30
N_DEV = 8
AXES = ("x", "y", "c")

OFF_Z, OFF_GATE, OFF_DT, OFF_Q, OFF_K, OFF_V, OFF_XBC = 0, 2048, 4096, 4608, 5632, 5888, 6144
N_INP = OFF_XBC + CONV_DIM
QKV_W = OFF_XBC - OFF_Q
CUT_Z, CUT_XBC, CUT_DT, CUT_Q, CUT_K, CUT_V, CUT_G = 0, 2048, 5120, 5152, 6176, 6432, 6688
N_IN = 8736

ADAM_LR, ADAM_B1, ADAM_B2, ADAM_EPS, ADAM_WD, ADAM_STEP = 0.001, 0.9, 0.999, 1e-08, 0.01, 10

VMEM_LIMIT = 56 * 1024 * 1024


def _params(n_grid):
    return pltpu.CompilerParams(dimension_semantics=("arbitrary",) * n_grid, vmem_limit_bytes=VMEM_LIMIT)


def _sds(shape, dtype):
    return jax.ShapeDtypeStruct(shape, dtype)


def _pick(n, prefs):
    for c in prefs:
        if n % c == 0:
            return c
    raise ValueError(f"no tile of {prefs} divides {n}")


def _row(tr, width, cb=0):
    return pl.BlockSpec((tr, width), lambda i: (i, cb))


def _row_rev(tr, width, nt, cb=0):
    return pl.BlockSpec((tr, width), lambda i: (nt - 1 - i, cb))


def _full(shape):
    return pl.BlockSpec(shape, lambda *_: (0,) * len(shape))


def _sigmoid(x):
    return 1.0 / (1.0 + jnp.exp(-x))


def _softplus(x):
    return jnp.maximum(x, 0.0) + jnp.log(1.0 + jnp.exp(-jnp.abs(x)))


def _rms(x):
    return lax.rsqrt(jnp.mean(x * x, axis=-1, keepdims=True) + EPS)


def _rms_bwd(x, r, w, dy):
    xh = x * r
    g = dy * w
    dx = r * (g - xh * jnp.mean(g * xh, axis=-1, keepdims=True))
    return dx, jnp.sum(dy * xh, axis=0, keepdims=True)


def _row_ids(shape, tile_index, tr):
    return tile_index * tr + lax.broadcasted_iota(jnp.int32, shape, 0)


HALO = 8
STRIP = 256
STRIP_BWD = 128


def _causal_taps(x, halo, first_step, taps):
    n = x.shape[0]

    @pl.when(first_step)
    def _():
        halo[...] = jnp.zeros_like(halo)

    before = halo[...]
    row = lax.broadcasted_iota(jnp.int32, before.shape, 0)
    shifted = [x]
    for s in range(1, taps):
        rolled = pltpu.roll(x, s, 0)
        head = jnp.where(row < s, pltpu.roll(before, s, 0), rolled[0:HALO, :])
        shifted.append(jnp.concatenate([head, rolled[HALO:, :]], axis=0))
    halo[...] = x[n - HALO:, :]
    return shifted


def _anticausal_taps(x, halo, first_step, taps):
    n = x.shape[0]

    @pl.when(first_step)
    def _():
        halo[...] = jnp.zeros_like(halo)

    after = halo[...]
    row = lax.broadcasted_iota(jnp.int32, after.shape, 0)
    shifted = [x]
    for s in range(1, taps):
        rolled = pltpu.roll(x, n - s, 0)
        tail = jnp.where(row >= HALO - s, pltpu.roll(after, HALO - s, 0), rolled[n - HALO:, :])
        shifted.append(jnp.concatenate([rolled[:n - HALO, :], tail], axis=0))
    halo[...] = x[0:HALO, :]
    return shifted


def _matmul(a, b, *, ta=False, tb=False, out_dtype=F32, name, after=None):
    if ta:
        k_dim, m_dim = a.shape
    else:
        m_dim, k_dim = a.shape
    n_dim = b.shape[0] if tb else b.shape[1]
    tm = _pick(m_dim, (1408, 1024, 768, 512, 384, 256, 128))
    tn = _pick(n_dim, (1024, 1408, 768, 512, 384, 256, 128))
    if ta:
        tk = _pick(k_dim, (1408, 1024, 768, 512, 384, 256, 128))
    else:
        tk = k_dim if k_dim <= 3072 else _pick(k_dim, (3072, 2816, 2048, 1024))
    nk = k_dim // tk
    dims = (((0 if ta else 1,), (1 if tb else 0,)), ((), ()))

    use_acc = nk > 1 and out_dtype != F32

    def body(a_ref, b_ref, *rest):
        o_ref = rest[-2] if use_acc else rest[-1]
        acc_ref = rest[-1] if use_acc else o_ref
        r = lax.dot_general(a_ref[...].astype(BF16), b_ref[...].astype(BF16), dims, preferred_element_type=F32)
        if nk == 1:
            o_ref[...] = r.astype(o_ref.dtype)
        else:
            k = pl.program_id(2)

            @pl.when(k == 0)
            def _():
                acc_ref[...] = r

            @pl.when(k > 0)
            def _():
                acc_ref[...] += r

            if use_acc:
                @pl.when(k == nk - 1)
                def _():
                    o_ref[...] = acc_ref[...].astype(o_ref.dtype)

    a_spec = pl.BlockSpec((tk, tm), lambda i, j, k: (k, i)) if ta else pl.BlockSpec((tm, tk), lambda i, j, k: (i, k))
    b_spec = pl.BlockSpec((tn, tk), lambda i, j, k: (j, k)) if tb else pl.BlockSpec((tk, tn), lambda i, j, k: (k, j))
    extra_specs, extra = ([], ()) if after is None else ([pl.BlockSpec(memory_space=pl.ANY)], (after,))
    return pl.pallas_call(
        body, grid=(m_dim // tm, n_dim // tn, nk), in_specs=[a_spec, b_spec] + extra_specs,
        out_specs=pl.BlockSpec((tm, tn), lambda i, j, k: (i, j)), out_shape=_sds((m_dim, n_dim), out_dtype),
        scratch_shapes=[pltpu.VMEM((tm, tn), F32)] if use_acc else [],
        name=name, compiler_params=_params(3))(a, b, *extra)


def _seq_specs():
    return [pl.BlockSpec((CHUNK, D_MODEL), lambda i: (jnp.maximum(i - 1, 0), 0)), _full((N_META, D_MODEL))]


def _seq_tile(x_ref, meta_ref, i):
    first = jnp.concatenate([jnp.zeros((META_PAD, D_MODEL), F32), meta_ref[...]], axis=0)
    return jnp.where(i == 0, first, x_ref[...])


def _prenorm(x, meta, w):
    t_rows = x.shape[0] + CHUNK

    def body(x_ref, meta_ref, w_ref, o_ref):
        h = _seq_tile(x_ref, meta_ref, pl.program_id(0))
        o_ref[...] = (h * _rms(h) * w_ref[...]).astype(BF16)

    return pl.pallas_call(body, grid=(t_rows // CHUNK,), in_specs=_seq_specs() + [_full((1, D_MODEL))],
                          out_specs=_row(CHUNK, D_MODEL), out_shape=_sds((t_rows, D_MODEL), BF16),
                          name="prenorm", compiler_params=_params(1))(x, meta, w)


def _ssm_conv_fwd(proj, conv_w, conv_b):
    t_rows = proj.shape[0]
    tr = CHUNK

    def body(x_ref, w_ref, b_ref, xc_ref, xa_ref, hist):
        first = pl.program_id(0) == 0
        for c in range(0, CONV_DIM, STRIP):
            cols = slice(c, c + STRIP)
            acc = b_ref[:, cols]
            for s, moved in enumerate(_causal_taps(x_ref[:, cols], hist.at[:, cols], first, SSM_CONV)):
                acc = acc + w_ref[SSM_CONV - 1 - s:SSM_CONV - s, cols] * moved
            xc_ref[:, cols] = acc
            xa_ref[:, cols] = acc * _sigmoid(acc)

    return pl.pallas_call(
        body, grid=(t_rows // tr,),
        in_specs=[_row(tr, CONV_DIM, OFF_XBC // CONV_DIM), _full((SSM_CONV, CONV_DIM)), _full((1, CONV_DIM))],
        out_specs=[_row(tr, CONV_DIM), _row(tr, CONV_DIM)],
        out_shape=[_sds((t_rows, CONV_DIM), F32), _sds((t_rows, CONV_DIM), F32)],
        scratch_shapes=[pltpu.VMEM((HALO, CONV_DIM), F32)],
        name="ssm_conv_fwd", compiler_params=_params(1))(proj, conv_w, conv_b)


def _ssm_post(y, proj, w):
    t_rows = y.shape[0]
    tr = CHUNK

    def body(y_ref, z_ref, w_ref, o_ref):
        z = z_ref[...].astype(F32)
        yz = y_ref[...] * z * _sigmoid(z)
        o_ref[...] = (yz * _rms(yz) * w_ref[...]).astype(BF16)

    return pl.pallas_call(body, grid=(t_rows // tr,),
                          in_specs=[_row(tr, D_INNER), _row(tr, D_INNER, OFF_Z // D_INNER), _full((1, D_INNER))],
                          out_specs=_row(tr, D_INNER), out_shape=_sds((t_rows, D_INNER), BF16),
                          name="ssm_post", compiler_params=_params(1))(y, proj, w)


def _mix_fwd(proj, y_ssm, y_attn):
    t_rows = y_ssm.shape[0]
    tr = _pick(t_rows, (384, 128))

    def body(g_ref, ys_ref, ya_ref, o_ref):
        g = _sigmoid(g_ref[...].astype(F32))
        o_ref[...] = (g[:, :D_MODEL] * ys_ref[...] + g[:, D_MODEL:] * ya_ref[...]).astype(BF16)

    return pl.pallas_call(body, grid=(t_rows // tr,),
                          in_specs=[_row(tr, 2 * D_MODEL, OFF_GATE // (2 * D_MODEL)), _row(tr, D_MODEL), _row(tr, D_MODEL)],
                          out_specs=_row(tr, D_MODEL), out_shape=_sds((t_rows, D_MODEL), BF16),
                          name="mix_fwd", compiler_params=_params(1))(proj, y_ssm, y_attn)


def _postmix(x, meta, mix, w_post, w_pre):
    t_rows = mix.shape[0]
    tr = CHUNK

    def body(x_ref, meta_ref, m_ref, wp_ref, wf_ref, h1_ref, hn_ref):
        m = m_ref[...]
        h1 = _seq_tile(x_ref, meta_ref, pl.program_id(0)) + m * _rms(m) * wp_ref[...]
        h1 = jnp.where(_row_ids(h1.shape, pl.program_id(0), tr) >= META_PAD, h1, 0.0)
        h1_ref[...] = h1
        hn_ref[...] = (h1 * _rms(h1) * wf_ref[...]).astype(BF16)

    return pl.pallas_call(body, grid=(t_rows // tr,),
                          in_specs=_seq_specs() + [_row(tr, D_MODEL), _full((1, D_MODEL)), _full((1, D_MODEL))],
                          out_specs=[_row(tr, D_MODEL), _row(tr, D_MODEL)],
                          out_shape=[_sds((t_rows, D_MODEL), F32), _sds((t_rows, D_MODEL), BF16)],
                          name="postmix", compiler_params=_params(1))(x, meta, mix, w_post, w_pre)


def _ffn_act(up, conv_w, conv_b):
    t_rows = up.shape[0]
    tr = CHUNK
    width = 2 * FFN_DIM

    def body(up_ref, w_ref, b_ref, u_ref, act_ref, hist):
        first = pl.program_id(0) == 0
        for c in range(0, FFN_DIM, STRIP):
            halves = []
            for base in (0, FFN_DIM):
                cols = slice(base + c, base + c + STRIP)
                u = b_ref[:, cols]
                for s, moved in enumerate(_causal_taps(up_ref[:, cols].astype(F32), hist.at[:, cols], first, FFN_CONV)):
                    u = u + w_ref[FFN_CONV - 1 - s:FFN_CONV - s, cols] * moved
                u_ref[:, cols] = u.astype(BF16)
                halves.append(u)
            a, g = halves
            act_ref[:, c:c + STRIP] = (a * _sigmoid(a) * g).astype(BF16)

    return pl.pallas_call(
        body, grid=(t_rows // tr,), in_specs=[_row(tr, width), _full((FFN_CONV, width)), _full((1, width))],
        out_specs=[_row(tr, width), _row(tr, FFN_DIM)],
        out_shape=[_sds((t_rows, width), BF16), _sds((t_rows, FFN_DIM), BF16)],
        scratch_shapes=[pltpu.VMEM((HALO, width), F32)],
        name="ffn_act", compiler_params=_params(1))(up, conv_w, conv_b)


def _final(h1, f, target, w):
    t_rows = h1.shape[0]
    tr = CHUNK

    def body(h1_ref, f_ref, t_ref, w_ref, df_ref, dy_ref, dw_ref, loss_ref):
        i = pl.program_id(0)

        @pl.when(i == 0)
        def _():
            dw_ref[...] = jnp.zeros_like(dw_ref)
            loss_ref[...] = jnp.zeros_like(loss_ref)

        f_val = f_ref[...]
        r = _rms(f_val)
        wv = w_ref[...]
        h2 = h1_ref[...] + f_val * r * wv
        diff = jnp.where(i >= 1, h2 - t_ref[...], 0.0)
        loss_ref[...] += 0.5 * jnp.sum(diff * diff) * (1.0 / D_MODEL)
        dy = diff * (1.0 / D_MODEL)
        dy_ref[...] = dy
        df, dw = _rms_bwd(f_val, r, wv, dy)
        df_ref[...] = df.astype(BF16)
        dw_ref[...] += dw

    tgt_spec = pl.BlockSpec((tr, D_MODEL), lambda i: (jnp.maximum(i - 1, 0), 0))
    return pl.pallas_call(
        body, grid=(t_rows // tr,),
        in_specs=[_row(tr, D_MODEL), _row(tr, D_MODEL), tgt_spec, _full((1, D_MODEL))],
        out_specs=[_row(tr, D_MODEL), _row(tr, D_MODEL), _full((1, D_MODEL)), _full((1, 128))],
        out_shape=[_sds((t_rows, D_MODEL), BF16), _sds((t_rows, D_MODEL), F32), _sds((1, D_MODEL), F32), _sds((1, 128), F32)],
        name="final", compiler_params=_params(1))(h1, f, target, w)


def _ffn_act_bwd(u, up, dact, conv_w):
    t_rows = u.shape[0]
    tr = CHUNK
    nt = t_rows // tr
    width = 2 * FFN_DIM

    def body(u_ref, up_ref, da_ref, w_ref, dup_ref, dw_ref, db_ref, ahead):
        @pl.when(pl.program_id(0) == 0)
        def _():
            dw_ref[...] = jnp.zeros_like(dw_ref)
            db_ref[...] = jnp.zeros_like(db_ref)

        first = pl.program_id(0) == 0
        for c in range(0, FFN_DIM, STRIP_BWD):
            ca, cg = slice(c, c + STRIP_BWD), slice(FFN_DIM + c, FFN_DIM + c + STRIP_BWD)
            a, g, d = u_ref[:, ca].astype(F32), u_ref[:, cg].astype(F32), da_ref[:, ca].astype(F32)
            s = _sigmoid(a)
            for cols, du in ((ca, d * g * s * (1.0 + a * (1.0 - s))), (cg, d * a * s)):
                x = up_ref[:, cols].astype(F32)
                dup = None
                for sh, moved in enumerate(_anticausal_taps(du, ahead.at[:, cols], first, FFN_CONV)):
                    k = FFN_CONV - 1 - sh
                    term = w_ref[k:k + 1, cols] * moved
                    dup = term if dup is None else dup + term
                    dw_ref[k:k + 1, cols] += jnp.sum(moved * x, axis=0, keepdims=True)
                db_ref[:, cols] += jnp.sum(du, axis=0, keepdims=True)
                dup_ref[:, cols] = dup.astype(BF16)

    return pl.pallas_call(
        body, grid=(nt,),
        in_specs=[_row_rev(tr, width, nt), _row_rev(tr, width, nt), _row_rev(tr, FFN_DIM, nt), _full((FFN_CONV, width))],
        out_specs=[_row_rev(tr, width, nt), _full((FFN_CONV, width)), _full((1, width))],
        out_shape=[_sds((t_rows, width), BF16), _sds((FFN_CONV, width), F32), _sds((1, width), F32)],
        scratch_shapes=[pltpu.VMEM((HALO, width), F32)],
        name="ffn_act_bwd", compiler_params=_params(1))(u, up, dact, conv_w)


def _postmix_bwd(h1, dhn2, dy, mix, w_pre, w_post):
    t_rows = h1.shape[0]
    tr = CHUNK

    def body(h1_ref, dhn_ref, dy_ref, m_ref, wf_ref, wp_ref, dmix_ref, dh_ref, dwf_ref, dwp_ref):
        @pl.when(pl.program_id(0) == 0)
        def _():
            dwf_ref[...] = jnp.zeros_like(dwf_ref)
            dwp_ref[...] = jnp.zeros_like(dwp_ref)

        h1v = h1_ref[...]
        dx, dwf = _rms_bwd(h1v, _rms(h1v), wf_ref[...], dhn_ref[...])
        dwf_ref[...] += dwf
        dh1 = dy_ref[...] + dx
        dh1 = jnp.where(_row_ids(dh1.shape, pl.program_id(0), tr) >= META_PAD, dh1, 0.0)
        dh_ref[...] = dh1
        m = m_ref[...]
        dmix, dwp = _rms_bwd(m, _rms(m), wp_ref[...], dh1)
        dwp_ref[...] += dwp
        dmix_ref[...] = dmix.astype(BF16)

    return pl.pallas_call(
        body, grid=(t_rows // tr,),
        in_specs=[_row(tr, D_MODEL)] * 4 + [_full((1, D_MODEL))] * 2,
        out_specs=[_row(tr, D_MODEL), _row(tr, D_MODEL), _full((1, D_MODEL)), _full((1, D_MODEL))],
        out_shape=[_sds((t_rows, D_MODEL), BF16), _sds((t_rows, D_MODEL), F32), _sds((1, D_MODEL), F32), _sds((1, D_MODEL), F32)],
        name="postmix_bwd", compiler_params=_params(1))(h1, dhn2, dy, mix, w_pre, w_post)


_ANY = pl.BlockSpec(memory_space=pl.ANY)


def _mix_bwd(dmixed, proj, y_ssm, y_attn, dproj):
    t_rows = dmixed.shape[0]
    tr = _pick(t_rows, (384, 128))

    def body(d_ref, g_ref, ys_ref, ya_ref, _, dys_ref, dya_ref, dg_ref):
        d = d_ref[...]
        g = _sigmoid(g_ref[...].astype(F32))
        g1, g2 = g[:, :D_MODEL], g[:, D_MODEL:]
        dys_ref[...] = (d * g1).astype(BF16)
        dya_ref[...] = (d * g2).astype(BF16)
        dg_ref[...] = jnp.concatenate([d * ys_ref[...] * g1 * (1.0 - g1), d * ya_ref[...] * g2 * (1.0 - g2)],
                                      axis=1).astype(BF16)

    return pl.pallas_call(
        body, grid=(t_rows // tr,),
        in_specs=[_row(tr, D_MODEL), _row(tr, 2 * D_MODEL, OFF_GATE // (2 * D_MODEL)), _row(tr, D_MODEL), _row(tr, D_MODEL),
                  _ANY],
        out_specs=[_row(tr, D_MODEL), _row(tr, D_MODEL), _row(tr, 2 * D_MODEL, OFF_GATE // (2 * D_MODEL))],
        out_shape=[_sds((t_rows, D_MODEL), BF16), _sds((t_rows, D_MODEL), BF16), _sds(dproj.shape, dproj.dtype)],
        input_output_aliases={4: 2},
        name="mix_bwd", compiler_params=_params(1))(dmixed, proj, y_ssm, y_attn, dproj)


def _ssm_post_bwd(y, proj, dyn, w, dproj):
    t_rows = y.shape[0]
    tr = CHUNK

    def body(y_ref, z_ref, d_ref, w_ref, _, dy_ref, dz_ref, dw_ref):
        @pl.when(pl.program_id(0) == 0)
        def _():
            dw_ref[...] = jnp.zeros_like(dw_ref)

        yv, z = y_ref[...], z_ref[...].astype(F32)
        sz = _sigmoid(z)
        silu = z * sz
        yz = yv * silu
        dyz, dw = _rms_bwd(yz, _rms(yz), w_ref[...], d_ref[...].astype(F32))
        dw_ref[...] += dw
        dy_ref[...] = dyz * silu
        dz_ref[...] = (dyz * yv * sz * (1.0 + z * (1.0 - sz))).astype(BF16)

    return pl.pallas_call(
        body, grid=(t_rows // tr,),
        in_specs=[_row(tr, D_INNER), _row(tr, D_INNER, OFF_Z // D_INNER), _row(tr, D_INNER), _full((1, D_INNER)), _ANY],
        out_specs=[_row(tr, D_INNER), _row(tr, D_INNER, OFF_Z // D_INNER), _full((1, D_INNER))],
        out_shape=[_sds((t_rows, D_INNER), F32), _sds(dproj.shape, dproj.dtype), _sds((1, D_INNER), F32)],
        input_output_aliases={4: 1},
        name="ssm_post_bwd", compiler_params=_params(1))(y, proj, dyn, w, dproj)


def _ssm_conv_bwd(xc, proj, dxs, dbm, dcm, conv_w, dproj):
    t_rows = xc.shape[0]
    tr = CHUNK
    nt = t_rows // tr
    bc_w = SSM_GROUPS * D_STATE

    def body(xc_ref, x_ref, dxs_ref, db_ref, dc_ref, w_ref, _, dx_ref, dw_ref, dbias_ref, ahead):
        first = pl.program_id(0) == 0

        @pl.when(first)
        def _():
            dw_ref[...] = jnp.zeros_like(dw_ref)
            dbias_ref[...] = jnp.zeros_like(dbias_ref)

        for c0 in range(0, CONV_DIM, STRIP_BWD):
            cols = slice(c0, c0 + STRIP_BWD)
            if c0 < D_INNER:
                dact = dxs_ref[:, cols]
            elif c0 < D_INNER + bc_w:
                dact = db_ref[:, c0 - D_INNER:c0 - D_INNER + STRIP_BWD]
            else:
                dact = dc_ref[:, c0 - D_INNER - bc_w:c0 - D_INNER - bc_w + STRIP_BWD]
            c = xc_ref[:, cols]
            s = _sigmoid(c)
            dpre = dact * s * (1.0 + c * (1.0 - s))
            x = x_ref[:, cols]
            dx = None
            for sh, moved in enumerate(_anticausal_taps(dpre, ahead.at[:, cols], first, SSM_CONV)):
                k = SSM_CONV - 1 - sh
                term = w_ref[k:k + 1, cols] * moved
                dx = term if dx is None else dx + term
                dw_ref[k:k + 1, cols] += jnp.sum(moved * x, axis=0, keepdims=True)
            dbias_ref[:, cols] += jnp.sum(dpre, axis=0, keepdims=True)
            dx_ref[:, cols] = dx.astype(BF16)

    xbc_block = OFF_XBC // CONV_DIM
    return pl.pallas_call(
        body, grid=(nt,),
        in_specs=[_row_rev(tr, CONV_DIM, nt), _row_rev(tr, CONV_DIM, nt, xbc_block), _row_rev(tr, D_INNER, nt),
                  _row_rev(tr, bc_w, nt), _row_rev(tr, bc_w, nt), _full((SSM_CONV, CONV_DIM)), _ANY],
        out_specs=[_row_rev(tr, CONV_DIM, nt, xbc_block), _full((SSM_CONV, CONV_DIM)), _full((1, CONV_DIM))],
        out_shape=[_sds(dproj.shape, dproj.dtype), _sds((SSM_CONV, CONV_DIM), F32), _sds((1, CONV_DIM), F32)],
        scratch_shapes=[pltpu.VMEM((HALO, CONV_DIM), F32)],
        input_output_aliases={6: 0},
        name="ssm_conv_bwd", compiler_params=_params(1))(xc, proj, dxs, dbm, dcm, conv_w, dproj)


def _prenorm_bwd(x, meta, dhn, dh, w):
    t_rows = dhn.shape[0]
    tr = CHUNK

    def body(x_ref, meta_ref, d_ref, r_ref, w_ref, dx_ref, dmeta_ref, dw_ref):
        i = pl.program_id(0)

        @pl.when(i == 0)
        def _():
            dw_ref[...] = jnp.zeros_like(dw_ref)

        h = _seq_tile(x_ref, meta_ref, i)
        dx, dw = _rms_bwd(h, _rms(h), w_ref[...], d_ref[...])
        dw_ref[...] += dw
        dh_tile = r_ref[...] + dx
        dx_ref[...] = dh_tile

        @pl.when(i == 0)
        def _():
            dmeta_ref[...] = dh_tile[META_PAD:, :]

    return pl.pallas_call(
        body, grid=(t_rows // tr,), in_specs=_seq_specs() + [_row(tr, D_MODEL)] * 2 + [_full((1, D_MODEL))],
        out_specs=[pl.BlockSpec((tr, D_MODEL), lambda i: (jnp.maximum(i - 1, 0), 0)), _full((N_META, D_MODEL)),
                   _full((1, D_MODEL))],
        out_shape=[_sds((t_rows - tr, D_MODEL), F32), _sds((N_META, D_MODEL), F32), _sds((1, D_MODEL), F32)],
        name="prenorm_bwd", compiler_params=_params(1))(x, meta, dhn, dh, w)


def _dot01(x, m01, x_left, parts):
    acc, rest = None, x
    for i in range(parts):
        piece = rest.astype(BF16)
        term = (jnp.dot(piece, m01, preferred_element_type=F32) if x_left
                else jnp.dot(m01, piece, preferred_element_type=F32))
        acc = term if acc is None else acc + term
        if i + 1 < parts:
            rest = rest - piece.astype(F32)
    return acc


def _ssd_common(dt_raw, dt_bias, a_log, chunk_index):
    rows = lax.broadcasted_iota(jnp.int32, (CHUNK, CHUNK), 0)
    cols = lax.broadcasted_iota(jnp.int32, (CHUNK, CHUNK), 1)
    low = rows >= cols
    raw = dt_raw + dt_bias
    live = _row_ids(raw.shape, chunk_index, CHUNK) >= META_PAD
    dt = jnp.where(live, _softplus(raw), 0.0)
    a_head = -jnp.exp(a_log)
    cs = _dot01(dt * a_head, low.astype(BF16), False, 3)
    grow = jnp.exp(cs)
    fade = jnp.exp(cs[CHUNK - 1:CHUNK, :] - cs)
    expand = (lax.broadcasted_iota(jnp.int32, (CHUNK, GROUP_W), 1) // HEAD_P
              == lax.broadcasted_iota(jnp.int32, (CHUNK, GROUP_W), 0)).astype(BF16)
    fold = (lax.broadcasted_iota(jnp.int32, (GROUP_W, CHUNK), 0) // HEAD_P
            == lax.broadcasted_iota(jnp.int32, (GROUP_W, CHUNK), 1)).astype(BF16)
    return dict(low=low, triu=(rows <= cols).astype(BF16), raw=raw, live=live, dt=dt, a_head=a_head, cs=cs, cs_t=cs.T,
                fold=fold, dtx=_dot01(dt, expand, True, 2), growx=_dot01(grow, expand, True, 2),
                fadex=_dot01(fade, expand, True, 2))


def _decay_matrix(cm, j):
    diff = cm["cs"][:, j:j + 1] - cm["cs_t"][j:j + 1, :]
    return jnp.where(cm["low"], jnp.exp(jnp.where(cm["low"], diff, 0.0)), 0.0)


def _dot(a, b, dims):
    return lax.dot_general(a.astype(BF16), b.astype(BF16), (dims, ((), ())), preferred_element_type=F32)


def _dot_fine(a, b, dims):
    a_hi, b_hi = a.astype(BF16), b.astype(BF16)
    a_lo, b_lo = (a - a_hi.astype(F32)).astype(BF16), (b - b_hi.astype(F32)).astype(BF16)
    dn = (dims, ((), ()))
    return (lax.dot_general(a_hi, b_hi, dn, preferred_element_type=F32)
            + lax.dot_general(a_hi, b_lo, dn, preferred_element_type=F32)
            + lax.dot_general(a_lo, b_hi, dn, preferred_element_type=F32))


def _ssd_specs(nt, rev):
    def idx(c):
        return nt - 1 - c if rev else c
    bc_w = SSM_GROUPS * D_STATE
    xs = pl.BlockSpec((CHUNK, D_INNER), lambda c: (idx(c), 0))
    bm = pl.BlockSpec((CHUNK, bc_w), lambda c: (idx(c), D_INNER // bc_w))
    cm = pl.BlockSpec((CHUNK, bc_w), lambda c: (idx(c), D_INNER // bc_w + 1))
    dtr = pl.BlockSpec((CHUNK, SSM_GROUPS * 128), lambda c: (idx(c), OFF_DT // (SSM_GROUPS * 128)))
    par = _full((SSM_GROUPS, 1, 128))
    par_x = _full((SSM_GROUPS, 1, GROUP_W))
    return xs, bm, cm, dtr, par, par_x, idx


def _group_cols(g, width):
    return slice(g * width, (g + 1) * width)


def _ssd_fwd(xact, proj, dtb, alog, dskip_x):
    t_rows = xact.shape[0]
    nt = t_rows // CHUNK
    xs_spec, b_spec, c_spec, dtr_spec, par, par_x, _ = _ssd_specs(nt, False)

    def body(xs_ref, b_ref, c_ref, dtr_ref, dtb_ref, alog_ref, dsk_ref, y_ref, hst_ref, state):
        c = pl.program_id(0)

        @pl.when(c == 0)
        def _():
            state[...] = jnp.zeros_like(state)

        for g in range(SSM_GROUPS):
            wide, narrow = _group_cols(g, GROUP_W), _group_cols(g, D_STATE)
            cm = _ssd_common(dtr_ref[:, narrow], dtb_ref[g], alog_ref[g], c)
            xs, bm, cmat = xs_ref[:, wide], b_ref[:, narrow], c_ref[:, narrow]
            x_dt = xs * cm["dtx"]
            h_in = state[g]
            hst_ref[0, g] = h_in
            y_ref[:, wide] = _dot(cmat, h_in, ((1,), (0,))) * cm["growx"] + xs * dsk_ref[g]
            cb = _dot(cmat, bm, ((1,), (1,)))
            for j in range(HEADS_PER_GROUP):
                sl = slice(g * GROUP_W + j * HEAD_P, g * GROUP_W + (j + 1) * HEAD_P)
                y_ref[:, sl] += _dot(cb * _decay_matrix(cm, j), x_dt[:, j * HEAD_P:(j + 1) * HEAD_P], ((1,), (0,)))
            state[g] = h_in * cm["growx"][CHUNK - 1:CHUNK, :] + _dot_fine(bm, x_dt * cm["fadex"], ((0,), (0,)))

    return pl.pallas_call(
        body, grid=(nt,),
        in_specs=[xs_spec, b_spec, c_spec, dtr_spec, par, par, par_x],
        out_specs=[xs_spec, pl.BlockSpec((1, SSM_GROUPS, D_STATE, GROUP_W), lambda c: (c, 0, 0, 0))],
        out_shape=[_sds((t_rows, D_INNER), F32), _sds((nt, SSM_GROUPS, D_STATE, GROUP_W), F32)],
        scratch_shapes=[pltpu.VMEM((SSM_GROUPS, D_STATE, GROUP_W), F32)],
        name="ssd_fwd", compiler_params=_params(1))(xact, xact, xact, proj, dtb, alog, dskip_x)


def _ssd_bwd(xact, proj, dtb, alog, dskip_x, dy, hst, dproj):
    t_rows = xact.shape[0]
    nt = t_rows // CHUNK
    xs_spec, b_spec, c_spec, dtr_spec, par, par_x, idx = _ssd_specs(nt, True)
    h_spec = pl.BlockSpec((1, SSM_GROUPS, D_STATE, GROUP_W), lambda c: (idx(c), 0, 0, 0))
    hn_spec = pl.BlockSpec((1, SSM_GROUPS, D_STATE, GROUP_W), lambda c: (jnp.minimum(idx(c) + 1, nt - 1), 0, 0, 0))
    bc_out = pl.BlockSpec((CHUNK, SSM_GROUPS * D_STATE), lambda c: (idx(c), 0))

    def body(xs_ref, b_ref, c_ref, dtr_ref, dtb_ref, alog_ref, dsk_ref, dy_ref, h_ref, hn_ref, _,
             dxs_ref, db_ref, dc_ref, ddt_ref, dalog_ref, ddtb_ref, dd_ref, dstate, dx_buf):
        step = pl.program_id(0)

        @pl.when(step == 0)
        def _():
            dstate[...] = jnp.zeros_like(dstate)
            dalog_ref[...] = jnp.zeros_like(dalog_ref)
            ddtb_ref[...] = jnp.zeros_like(ddtb_ref)
            dd_ref[...] = jnp.zeros_like(dd_ref)

        for g in range(SSM_GROUPS):
            _ssd_bwd_group(g, idx(step), xs_ref, b_ref, c_ref, dtr_ref, dtb_ref, alog_ref, dsk_ref, dy_ref, h_ref, hn_ref,
                           dxs_ref, db_ref, dc_ref, ddt_ref, dalog_ref, ddtb_ref, dd_ref, dstate, dx_buf)

    return pl.pallas_call(
        body, grid=(nt,),
        in_specs=[xs_spec, b_spec, c_spec, dtr_spec, par, par, par_x, xs_spec, h_spec, hn_spec, _ANY],
        out_specs=[xs_spec, bc_out, bc_out, dtr_spec, par, par, par_x],
        out_shape=[_sds((t_rows, D_INNER), F32), _sds((t_rows, SSM_GROUPS * D_STATE), F32),
                   _sds((t_rows, SSM_GROUPS * D_STATE), F32), _sds(dproj.shape, dproj.dtype),
                   _sds((SSM_GROUPS, 1, 128), F32), _sds((SSM_GROUPS, 1, 128), F32), _sds((SSM_GROUPS, 1, GROUP_W), F32)],
        scratch_shapes=[pltpu.VMEM((SSM_GROUPS, D_STATE, GROUP_W), F32), pltpu.VMEM((CHUNK, GROUP_W), F32)],
        input_output_aliases={10: 3},
        name="ssd_bwd", compiler_params=_params(1))(xact, xact, xact, proj, dtb, alog, dskip_x, dy, hst, hst, dproj)


def _ssd_bwd_group(g, chunk, xs_ref, b_ref, c_ref, dtr_ref, dtb_ref, alog_ref, dsk_ref, dy_ref, h_ref, hn_ref,
                   dxs_ref, db_ref, dc_ref, ddt_ref, dalog_ref, ddtb_ref, dd_ref, dstate, dx_buf):
    wide, narrow = _group_cols(g, GROUP_W), _group_cols(g, D_STATE)
    cm = _ssd_common(dtr_ref[:, narrow], dtb_ref[g], alog_ref[g], chunk)
    xs, bm, cmat = xs_ref[:, wide], b_ref[:, narrow], c_ref[:, narrow]
    dsk = dsk_ref[g]
    x_dt = xs * cm["dtx"]
    h_in, h_next = h_ref[0, g], hn_ref[0, g]
    dyv = dy_ref[:, wide]
    dh = dstate[g]
    grow, fade = cm["growx"], cm["fadex"]
    dy_grow = dyv * grow
    x_fade = x_dt * fade
    cb = _dot(cmat, bm, ((1,), (1,)))
    ml = jnp.zeros((CHUNK, CHUNK), F32)
    row_id = lax.broadcasted_iota(jnp.int32, (CHUNK, CHUNK), 0)
    col_id = lax.broadcasted_iota(jnp.int32, (CHUNK, CHUNK), 1)
    w_rows = jnp.zeros((CHUNK, CHUNK), F32)
    w_cols = jnp.zeros((CHUNK, CHUNK), F32)
    for j in range(HEADS_PER_GROUP):
        sl = slice(j * HEAD_P, (j + 1) * HEAD_P)
        lm = _decay_matrix(cm, j)
        mlj = _dot(dyv[:, sl], x_dt[:, sl], ((1,), (1,))) * lm
        ml = ml + mlj
        wm = mlj * cb
        w_rows = jnp.where(col_id == j, jnp.sum(wm, axis=1, keepdims=True), w_rows)
        w_cols = jnp.where(row_id == j, jnp.sum(wm, axis=0, keepdims=True), w_cols)
        dx_buf[:, sl] = _dot(cb * lm, dyv[:, sl], ((0,), (0,)))
    dx_off = fade * _dot_fine(bm, dh, ((1,), (0,)))
    dx = dx_buf[...] + dx_off
    dc_ref[:, narrow] = _dot(ml, bm, ((1,), (0,))) + _dot(dy_grow, h_in, ((1,), (1,)))
    db_ref[:, narrow] = _dot(ml, cmat, ((0,), (0,))) + _dot(x_fade, dh, ((1,), (1,)))
    fold = cm["fold"]
    y_off = _dot_fine(cmat, h_in, ((1,), (0,))) * grow
    dcs = (w_rows - w_cols.T) + _dot01(dyv * y_off - x_dt * dx_off, fold, True, 2)
    tail = jnp.broadcast_to(jnp.sum(dh * h_next, axis=0, keepdims=True), (8, GROUP_W))
    tail = _dot01(tail, fold, True, 2)[0:1, :]
    last_row = lax.broadcasted_iota(jnp.int32, (CHUNK, 128), 0) == CHUNK - 1
    dcs = dcs + jnp.where(last_row, tail, 0.0)
    da = _dot01(dcs, cm["triu"], False, 3)
    ddt = da * cm["a_head"] + _dot01(dx * xs, fold, True, 2)
    ddt_raw = jnp.where(cm["live"], ddt * _sigmoid(cm["raw"]), 0.0)
    ddt_ref[:, narrow] = ddt_raw.astype(BF16)
    ddtb_ref[g] += jnp.sum(ddt_raw, axis=0, keepdims=True)
    dalog_ref[g] += jnp.sum(da * cm["dt"], axis=0, keepdims=True) * cm["a_head"]
    dd_ref[g] += jnp.sum(dyv * xs, axis=0, keepdims=True)
    dxs_ref[:, wide] = dx * cm["dtx"] + dyv * dsk
    dstate[g] = dh * grow[CHUNK - 1:CHUNK, :] + _dot_fine(cmat, dy_grow, ((0,), (0,)))


def _swa_bias():
    rows_q = ATTN_GROUP * CHUNK
    dist = (jnp.arange(rows_q) % CHUNK)[:, None] - jnp.arange(2 * CHUNK)[None, :] + CHUNK
    head = jnp.arange(KV_HEADS)[:, None] * ATTN_GROUP + jnp.arange(rows_q)[None, :] // CHUNK + 1
    slope = jnp.exp2(-8.0 * head.astype(F32) / ATTN_HEADS)
    return jnp.where((dist >= 0) & (dist < CHUNK), -slope[:, :, None] * dist.astype(F32)[None], NEG)


def _swa_probs(q_kv, k_prev, k_cur, k_first, sink, bias, n):
    rows_q = ATTN_GROUP * CHUNK
    qs = jnp.concatenate([q_kv[:, g * DH:(g + 1) * DH] for g in range(ATTN_GROUP)], axis=0) * (DH ** -0.5)
    kcat = jnp.concatenate([k_prev, k_cur], axis=0)
    kmeta = k_first[META_PAD:, :]
    key_ok = lax.broadcasted_iota(jnp.int32, (1, 2 * CHUNK), 1) + n * CHUNK >= 2 * CHUNK
    s_band = jnp.where(key_ok, _dot(qs, kcat, ((1,), (1,))) + bias, NEG)
    q_pos = lax.broadcasted_iota(jnp.int32, (rows_q, N_META), 0) % CHUNK + n * CHUNK - META_PAD
    ok_m = lax.broadcasted_iota(jnp.int32, (rows_q, N_META), 1) <= q_pos
    s_meta = jnp.where(ok_m, _dot(qs, kmeta, ((1,), (1,))), NEG)
    m = jnp.maximum(jnp.maximum(jnp.max(s_band, axis=1, keepdims=True), jnp.max(s_meta, axis=1, keepdims=True)), sink)
    p_band, p_meta, p_sink = jnp.exp(s_band - m), jnp.exp(s_meta - m), jnp.exp(sink - m)
    inv = 1.0 / (jnp.sum(p_band, axis=1, keepdims=True) + jnp.sum(p_meta, axis=1, keepdims=True) + p_sink)
    return qs, kcat, kmeta, p_band * inv, p_meta * inv, p_sink * inv


def _swa_specs(nt, rev):
    def idx(n):
        return nt - 1 - n if rev else n
    o = pl.BlockSpec((CHUNK, ATTN_HEADS * DH), lambda n: (idx(n), 0))
    chunks = (lambda c: jnp.maximum(c - 1, 0)), (lambda c: c), (lambda c: 0)
    qkv = [pl.BlockSpec((CHUNK, QKV_W), lambda n, f=f: (f(idx(n)), OFF_Q // QKV_W)) for f in chunks]
    sink = _full((KV_HEADS, ATTN_GROUP * CHUNK, 1))
    bias = _full((KV_HEADS, ATTN_GROUP * CHUNK, 2 * CHUNK))
    return o, qkv, sink, bias, idx


def _head_cols(k):
    kv_w = ATTN_GROUP * DH
    q0, k0, v0 = k * kv_w, OFF_K - OFF_Q + k * DH, OFF_V - OFF_Q + k * DH
    return slice(q0, q0 + kv_w), slice(k0, k0 + DH), slice(v0, v0 + DH)


def _swa_fwd(proj, sink_rows, bias):
    t_rows = proj.shape[0]
    nt = t_rows // CHUNK
    o_spec, qkv_specs, sink_spec, bias_spec, _ = _swa_specs(nt, False)
    kv_w = ATTN_GROUP * DH

    def body(prev_ref, cur_ref, first_ref, sink_ref, bias_ref, o_ref):
        n = pl.program_id(0)
        for k in range(KV_HEADS):
            qc, kc, vc = _head_cols(k)
            _, _, _, p_band, p_meta, _ = _swa_probs(cur_ref[:, qc], prev_ref[:, kc], cur_ref[:, kc], first_ref[:, kc],
                                                    sink_ref[k], bias_ref[k], n)
            vcat = jnp.concatenate([prev_ref[:, vc], cur_ref[:, vc]], axis=0)
            out = _dot(p_band, vcat, ((1,), (0,))) + _dot(p_meta, first_ref[:, vc][META_PAD:, :], ((1,), (0,)))
            for g in range(ATTN_GROUP):
                o_ref[:, k * kv_w + g * DH:k * kv_w + (g + 1) * DH] = out[g * CHUNK:(g + 1) * CHUNK, :]

    return pl.pallas_call(
        body, grid=(nt,), in_specs=qkv_specs + [sink_spec, bias_spec],
        out_specs=o_spec, out_shape=_sds((t_rows, ATTN_HEADS * DH), F32),
        name="swa_fwd", compiler_params=_params(1))(proj, proj, proj, sink_rows, bias)


def _swa_bwd(proj, sink_rows, bias, out, dout, dproj):
    t_rows = proj.shape[0]
    nt = t_rows // CHUNK
    o_spec, qkv_specs, sink_spec, bias_spec, idx = _swa_specs(nt, True)
    kv_w = ATTN_GROUP * DH
    k_off, v_off = OFF_K - OFF_Q, OFF_V - OFF_Q

    def body(prev_ref, cur_ref, first_ref, sink_ref, bias_ref, o_ref, do_ref, _, dqkv_ref, dsink_ref,
             carry_k, carry_v, meta_k, meta_v, dqkv_buf):
        step = pl.program_id(0)
        n = idx(step)

        @pl.when(step == 0)
        def _():
            carry_k[...] = jnp.zeros_like(carry_k)
            carry_v[...] = jnp.zeros_like(carry_v)
            meta_k[...] = jnp.zeros_like(meta_k)
            meta_v[...] = jnp.zeros_like(meta_v)
            dsink_ref[...] = jnp.zeros_like(dsink_ref)

        for k in range(KV_HEADS):
            cols = slice(k * kv_w, (k + 1) * kv_w)
            hd = slice(k * DH, (k + 1) * DH)
            qc, kc, vc = _head_cols(k)
            qs, kcat, kmeta, p_band, p_meta, p_sink = _swa_probs(cur_ref[:, qc], prev_ref[:, kc], cur_ref[:, kc],
                                                                 first_ref[:, kc], sink_ref[k], bias_ref[k], n)
            vcat = jnp.concatenate([prev_ref[:, vc], cur_ref[:, vc]], axis=0)
            vmeta = first_ref[:, vc][META_PAD:, :]
            o, do = o_ref[:, cols], do_ref[:, cols]
            os_ = jnp.concatenate([o[:, g * DH:(g + 1) * DH] for g in range(ATTN_GROUP)], axis=0)
            dos = jnp.concatenate([do[:, g * DH:(g + 1) * DH] for g in range(ATTN_GROUP)], axis=0)
            delta = jnp.sum(dos * os_, axis=1, keepdims=True)
            ds_band = p_band * (_dot(dos, vcat, ((1,), (1,))) - delta)
            ds_meta = p_meta * (_dot(dos, vmeta, ((1,), (1,))) - delta)
            ds_sink = -p_sink * delta
            dqs = (_dot(ds_band, kcat, ((1,), (0,))) + _dot(ds_meta, kmeta, ((1,), (0,)))) * (DH ** -0.5)
            for g in range(ATTN_GROUP):
                dqkv_buf[:, k * kv_w + g * DH:k * kv_w + (g + 1) * DH] = dqs[g * CHUNK:(g + 1) * CHUNK, :]
                dsink_ref[k, g:g + 1, :] += jnp.sum(ds_sink[g * CHUNK:(g + 1) * CHUNK, :])
            dkcat = _dot(ds_band, qs, ((0,), (0,)))
            dvcat = _dot(p_band, dos, ((0,), (0,)))
            meta_k[:, hd] += _dot(ds_meta, qs, ((0,), (0,)))
            meta_v[:, hd] += _dot(p_meta, dos, ((0,), (0,)))
            dqkv_buf[:, kc] = dkcat[CHUNK:, :] + carry_k[:, hd]
            dqkv_buf[:, vc] = dvcat[CHUNK:, :] + carry_v[:, hd]
            carry_k[:, hd] = dkcat[:CHUNK, :]
            carry_v[:, hd] = dvcat[:CHUNK, :]

        @pl.when(n == 0)
        def _():
            dqkv_buf[META_PAD:, k_off:k_off + KV_W] += meta_k[...]
            dqkv_buf[META_PAD:, v_off:v_off + KV_W] += meta_v[...]

        dqkv_ref[...] = dqkv_buf[...].astype(BF16)

    return pl.pallas_call(
        body, grid=(nt,),
        in_specs=qkv_specs + [sink_spec, bias_spec, o_spec, o_spec, pl.BlockSpec(memory_space=pl.ANY)],
        out_specs=[qkv_specs[1], _full((KV_HEADS, 8, 128))],
        out_shape=[_sds(dproj.shape, dproj.dtype), _sds((KV_HEADS, 8, 128), F32)],
        scratch_shapes=[pltpu.VMEM((CHUNK, KV_W), F32), pltpu.VMEM((CHUNK, KV_W), F32),
                        pltpu.VMEM((N_META, KV_W), F32), pltpu.VMEM((N_META, KV_W), F32),
                        pltpu.VMEM((CHUNK, QKV_W), F32)],
        input_output_aliases={7: 0},
        name="swa_bwd", compiler_params=_params(1))(proj, proj, proj, sink_rows, bias, out, dout, dproj)


def _pack_w_in_t(w_in_t):
    w_dt = w_in_t[CUT_DT:CUT_Q].reshape(SSM_GROUPS, HEADS_PER_GROUP, D_MODEL)
    w_dt = jnp.pad(w_dt, ((0, 0), (0, 128 - HEADS_PER_GROUP), (0, 0))).reshape(SSM_GROUPS * 128, D_MODEL)
    return jnp.concatenate([w_in_t[CUT_Z:CUT_XBC], w_in_t[CUT_G:], w_dt, w_in_t[CUT_Q:CUT_G], w_in_t[CUT_XBC:CUT_DT]], axis=0)


def _unpack_w_in_t(wp_t):
    w_dt = wp_t[OFF_DT:OFF_Q].reshape(SSM_GROUPS, 128, D_MODEL)[:, :HEADS_PER_GROUP].reshape(SSM_HEADS, D_MODEL)
    return jnp.concatenate([wp_t[OFF_Z:OFF_GATE], wp_t[OFF_XBC:], w_dt, wp_t[OFF_Q:OFF_XBC], wp_t[OFF_GATE:OFF_DT]], axis=0)


def _group_rows(v, width):
    return jnp.pad(v.reshape(SSM_GROUPS, 1, HEADS_PER_GROUP), ((0, 0), (0, 0), (0, width - HEADS_PER_GROUP)))


def _local_step(x, target, wt, late_weights=None, on_grad=None, started=None):
    seq = x.shape[0]
    grads = {}

    def emit(name, g):
        grads[name] = g
        return None if on_grad is None else on_grad(name, g)
    meta = wt["meta_tokens"]
    wp_t = _pack_w_in_t(wt["w_in_t"])
    dtb = _group_rows(wt["ssm_dt_bias"].reshape(-1), 128)
    alog = _group_rows(wt["ssm_a_log"].reshape(-1), 128)
    dskip_x = jnp.repeat(wt["ssm_d_skip"].reshape(-1), HEAD_P).reshape(SSM_GROUPS, 1, GROUP_W)
    sink_rows = jnp.repeat(wt["attn_sinks"].reshape(KV_HEADS, ATTN_GROUP), CHUNK, axis=1).reshape(KV_HEADS, ATTN_GROUP * CHUNK, 1)

    hn = _prenorm(x, meta, wt["norm_pre_mix"])
    proj = _matmul(hn, wp_t, tb=True, name="in_proj", after=started)
    xc, xact = _ssm_conv_fwd(proj, wt["ssm_conv_w"], wt["ssm_conv_b"])
    y, hst = _ssd_fwd(xact, proj, dtb, alog, dskip_x)
    yn = _ssm_post(y, proj, wt["ssm_norm"])
    if late_weights is not None:
        wt = {**wt, **late_weights(yn)}
    y_ssm = _matmul(yn, wt["w_ssm_out"], name="ssm_out")
    bias = _swa_bias()
    attn = _swa_fwd(proj, sink_rows, bias)
    y_attn = _matmul(attn, wt["w_attn_out"], name="attn_out")
    mixed = _mix_fwd(proj, y_ssm, y_attn)
    mix = _matmul(mixed, wt["w_mix_out"], name="mix_out")
    h1, hn2 = _postmix(x, meta, mix, wt["norm_post_mix"], wt["norm_pre_ffn"])
    up = _matmul(hn2, wt["w_ffn_up_t"], tb=True, out_dtype=BF16, name="ffn_up")
    u, act = _ffn_act(up, wt["ffn_conv_w"], wt["ffn_conv_b"])
    f = _matmul(act, wt["w_ffn_down"], name="ffn_down")
    df, dy, g_norm_post_ffn, loss_row = _final(h1, f, target, wt["norm_post_ffn"])

    grads["norm_post_ffn"] = g_norm_post_ffn
    sent = emit("w_ffn_down", _matmul(act, df, ta=True, out_dtype=BF16, name="dw_ffn_down"))
    dact = _matmul(df, wt["w_ffn_down"], tb=True, out_dtype=BF16, name="d_act", after=sent)
    dup, grads["ffn_conv_w"], grads["ffn_conv_b"] = _ffn_act_bwd(u, up, dact, wt["ffn_conv_w"])
    sent = emit("w_ffn_up_t", _matmul(dup, hn2, ta=True, out_dtype=BF16, name="dw_ffn_up"))
    dhn2 = _matmul(dup, wt["w_ffn_up_t"], name="d_hn2", after=sent)
    dmix, dh, grads["norm_pre_ffn"], grads["norm_post_mix"] = _postmix_bwd(h1, dhn2, dy, mix, wt["norm_pre_ffn"], wt["norm_post_mix"])
    sent = emit("w_mix_out", _matmul(mixed, dmix, ta=True, out_dtype=BF16, name="dw_mix_out"))
    dmixed = _matmul(dmix, wt["w_mix_out"], tb=True, name="d_mixed", after=sent)
    dy_ssm, dy_attn, dproj = _mix_bwd(dmixed, proj, y_ssm, y_attn, lax.empty(proj.shape, BF16))
    sent = emit("w_ssm_out", _matmul(yn, dy_ssm, ta=True, out_dtype=BF16, name="dw_ssm_out"))
    dyn = _matmul(dy_ssm, wt["w_ssm_out"], tb=True, out_dtype=BF16, name="d_yn", after=sent)
    sent = emit("w_attn_out", _matmul(attn, dy_attn, ta=True, out_dtype=BF16, name="dw_attn_out"))
    dattn = _matmul(dy_attn, wt["w_attn_out"], tb=True, name="d_attn", after=sent)
    dy_ssd, dproj, grads["ssm_norm"] = _ssm_post_bwd(y, proj, dyn, wt["ssm_norm"], dproj)
    dxs, dbm, dcm, dproj, dalog, ddtb, dd_x = _ssd_bwd(xact, proj, dtb, alog, dskip_x, dy_ssd, hst, dproj)
    grads["ssm_a_log"] = dalog[:, 0, :HEADS_PER_GROUP].reshape(1, SSM_HEADS)
    grads["ssm_dt_bias"] = ddtb[:, 0, :HEADS_PER_GROUP].reshape(1, SSM_HEADS)
    grads["ssm_d_skip"] = dd_x.reshape(SSM_HEADS, HEAD_P).sum(axis=1).reshape(1, SSM_HEADS)
    dproj, grads["ssm_conv_w"], grads["ssm_conv_b"] = _ssm_conv_bwd(xc, proj, dxs, dbm, dcm, wt["ssm_conv_w"], dproj)
    dproj, dsink = _swa_bwd(proj, sink_rows, bias, attn, dattn, dproj)
    grads["attn_sinks"] = dsink[:, :ATTN_GROUP, 0].reshape(1, ATTN_HEADS)
    sent = emit("w_in_t", _unpack_w_in_t(_matmul(dproj, hn, ta=True, out_dtype=BF16, name="dw_in")))
    dhn = _matmul(dproj, wp_t, name="d_hn", after=sent)
    grad_x, grads["meta_tokens"], grads["norm_pre_mix"] = _prenorm_bwd(x, meta, dhn, dh, wt["norm_pre_mix"])
    return loss_row[0, 0], grad_x, grads


def _all_gather(shards):
    n = len(shards)

    def body(*refs):
        ins, outs = refs[:n], refs[n:2 * n]
        send_sems, recv_sems, local_sems = refs[2 * n:]
        x, y, c = lax.axis_index("x"), lax.axis_index("y"), lax.axis_index("c")
        me, sibling = (x, y, c), (x, y, 1 - c)
        chips = [(1 - x, y), (x, 1 - y), (1 - x, 1 - y)]

        def slot(a, dev):
            return outs[a].at[4 * dev[0] + 2 * dev[1] + dev[2]]

        def copy(k, a, block, to, src=None):
            return pltpu.make_async_remote_copy(
                src_ref=slot(a, block) if src is None else src, dst_ref=slot(a, block),
                send_sem=send_sems.at[k, a], recv_sem=recv_sems.at[k, a],
                device_id=to, device_id_type=pl.DeviceIdType.MESH)

        mine = [pltpu.make_async_copy(ins[a], slot(a, me), local_sems.at[a]) for a in range(n)]
        for cp in mine:
            cp.start()
        first = [copy(0, a, me, sibling, src=ins[a]) for a in range(n)]
        for j, chip in enumerate(chips):
            first += [copy(1 + j, a, me, (*chip, c), src=ins[a]) for a in range(n)]
        for cp in first:
            cp.start()
        passed = []
        for j, chip in enumerate(chips):
            for a in range(n):
                copy(1 + j, a, (*chip, c), me).wait_recv()
                fwd = copy(4 + j, a, (*chip, c), sibling)
                fwd.start()
                passed.append(fwd)
        for a in range(n):
            copy(0, a, sibling, me).wait_recv()
        for j, chip in enumerate(chips):
            for a in range(n):
                copy(4 + j, a, (*chip, 1 - c), me).wait_recv()
        for cp in first + passed:
            cp.wait_send()
        for cp in mine:
            cp.wait()

    hbm = pl.BlockSpec(memory_space=pl.ANY)
    return pl.pallas_call(
        body, in_specs=[hbm] * n, out_specs=[hbm] * n,
        out_shape=[_sds((N_DEV,) + s.shape, s.dtype) for s in shards],
        scratch_shapes=[pltpu.SemaphoreType.DMA((7, n)), pltpu.SemaphoreType.DMA((7, n)), pltpu.SemaphoreType.DMA((n,))],
        name="gather_weights")(*shards)


def _peer_table():
    x, y, c = lax.axis_index("x"), lax.axis_index("y"), lax.axis_index("c")
    peers = []
    for k in range(N_DEV - 1):
        bits = k + 1
        p = (x ^ ((bits >> 2) & 1), y ^ ((bits >> 1) & 1), c ^ (bits & 1))
        peers.append((k, p, 4 * p[0] + 2 * p[1] + p[2]))
    return 4 * x + 2 * y + c, peers


_HBM = pl.BlockSpec(memory_space=pltpu.HBM)
_SEM = pl.BlockSpec(memory_space=pltpu.SEMAPHORE)
_EFFECT = pltpu.SideEffectType.DATAFLOW_SIDE_EFFECTING


def _push_copy(src, land, send_sems, recv_sems, a, k, p, src_slot, dst_slot):
    sem = a * (N_DEV - 1) + k
    return pltpu.make_async_remote_copy(
        src_ref=src[a] if src_slot is None else src[a].at[src_slot], dst_ref=land[a].at[dst_slot],
        send_sem=send_sems.at[sem], recv_sem=recv_sems.at[sem], device_id=p, device_id_type=pl.DeviceIdType.MESH)


def _push_start(srcs, scatter, name):
    n = len(srcs)
    lands = [lax.empty(s.shape if scatter else (N_DEV,) + s.shape, s.dtype) for s in srcs]

    def body(*refs):
        src, land = refs[:n], refs[n:2 * n]
        send_sems, recv_sems, token = refs[2 * n], refs[2 * n + 1], refs[-1]
        my_id, peers = _peer_table()
        for a in range(n):
            for k, p, p_id in peers:
                _push_copy(src, land, send_sems, recv_sems, a, k, p, p_id if scatter else None, my_id).start()
        token[...] = jnp.zeros_like(token)

    sems = pltpu.SemaphoreType.DMA(((N_DEV - 1) * n,))
    res = pl.pallas_call(
        body, name=name,
        out_shape=(sems, sems, *[pltpu.HBM(a.shape, a.dtype) for a in srcs + lands], _sds((8, 128), F32)),
        in_specs=[_HBM] * (2 * n), out_specs=(_SEM, _SEM, *[_HBM] * (2 * n), pl.BlockSpec(memory_space=pltpu.VMEM)),
        input_output_aliases={i: 2 + i for i in range(2 * n)},
        compiler_params=pltpu.CompilerParams(has_side_effects=_EFFECT),
    )(*[pltpu.with_memory_space_constraint(a, pltpu.HBM) for a in srcs + lands])
    return dict(send=res[0], recv=res[1], src=list(res[2:2 + n]), land=list(res[2 + n:2 + 2 * n]), token=res[-1],
                scatter=scatter)


def _push_wait(handle, after, name):
    n = len(handle["src"])
    scatter = handle["scatter"]

    def body(*refs):
        src, land = refs[:n], refs[n:2 * n]
        send_sems, recv_sems = refs[2 * n], refs[2 * n + 1]
        _, peers = _peer_table()
        for a in range(n):
            for k, p, p_id in peers:
                cp = _push_copy(src, land, send_sems, recv_sems, a, k, p, p_id if scatter else None, p_id)
                cp.wait_send()
                cp.wait_recv()

    arrays = handle["src"] + handle["land"]
    res = pl.pallas_call(
        body, name=name, out_shape=tuple(pltpu.HBM(a.shape, a.dtype) for a in arrays),
        in_specs=[_HBM] * (2 * n) + [_SEM, _SEM, pl.BlockSpec(memory_space=pl.ANY)], out_specs=tuple([_HBM] * (2 * n)),
        input_output_aliases={i: i for i in range(2 * n)},
        compiler_params=pltpu.CompilerParams(has_side_effects=_EFFECT),
    )(*arrays, handle["send"], handle["recv"], after)
    return list(res[:n]), list(res[n:])


def _slot_sum(p_ref, own_ref):
    if own_ref is not None:
        my_id = 4 * lax.axis_index("x") + 2 * lax.axis_index("y") + lax.axis_index("c")
        mine = own_ref[...].astype(F32)
    g = None
    for s in range(p_ref.shape[0]):
        term = p_ref[s].astype(F32)
        if own_ref is not None:
            term = jnp.where(my_id == s, mine, term)
        g = term if g is None else g + term
    return g


def _to_bf16(arrays):
    n = len(arrays)

    def body(*refs):
        for i in range(n):
            refs[n + i][...] = refs[i][...].astype(BF16)

    return pl.pallas_call(body, out_shape=[_sds(a.shape, BF16) for a in arrays], name="weights_to_bf16",
                          compiler_params=pltpu.CompilerParams(vmem_limit_bytes=VMEM_LIMIT))(*arrays)


def _adamw(parts, own, w, m, v, name):
    unit_rows = w.ndim == 3
    rows, cols = w.shape[0], w.shape[-1]
    if rows % 16 == 0:
        tr, tc = _pick(rows, (256, 128, 176, 64, 32, 16)), cols
    else:
        tr, tc = rows, _pick(cols, (256, 128))

    def body(*refs):
        if own is None:
            p_ref, w_ref, m_ref, v_ref, g_ref, d_ref, nm_ref, nv_ref = refs
            own_ref = None
        else:
            p_ref, own_ref, w_ref, m_ref, v_ref, g_ref, d_ref, nm_ref, nv_ref = refs
        g = _slot_sum(p_ref, own_ref)
        if unit_rows:
            g = g.reshape(tr, 1, tc)
        m_new = ADAM_B1 * m_ref[...] + (1.0 - ADAM_B1) * g
        v_new = ADAM_B2 * v_ref[...] + (1.0 - ADAM_B2) * (g * g)
        m_hat = m_new / (1.0 - ADAM_B1 ** ADAM_STEP)
        v_hat = v_new / (1.0 - ADAM_B2 ** ADAM_STEP)
        g_ref[...] = g
        d_ref[...] = -ADAM_LR * (m_hat / (jnp.sqrt(v_hat) + ADAM_EPS) + ADAM_WD * w_ref[...])
        nm_ref[...] = m_new
        nv_ref[...] = v_new

    by_rows = tc == cols
    spec = pl.BlockSpec((tr, tc), (lambda i: (i, 0)) if by_rows else (lambda i: (0, i)))
    state_spec = spec if not unit_rows else pl.BlockSpec((tr, 1, tc), (lambda i: (i, 0, 0)) if by_rows else (lambda i: (0, 0, i)))
    parts_spec = pl.BlockSpec((parts.shape[0], tr, tc), (lambda i: (0, i, 0)) if by_rows else (lambda i: (0, 0, i)))
    operands = (parts, w, m, v) if own is None else (parts, own, w, m, v)
    return pl.pallas_call(
        body, grid=(rows // tr if by_rows else cols // tc,),
        in_specs=[parts_spec] + ([] if own is None else [spec]) + [state_spec] * 3,
        out_specs=[state_spec] * 4, out_shape=[_sds(w.shape, F32)] * 4,
        name=name, compiler_params=_params(1))(*operands)


SMALL_REPLICATED = (("norm_pre_mix", 1024), ("ssm_conv_b", 3072), ("ssm_dt_bias", 32), ("ssm_a_log", 32),
                    ("ssm_d_skip", 32), ("ssm_norm", 2048), ("attn_sinks", 16), ("norm_post_mix", 1024),
                    ("norm_pre_ffn", 1024), ("ffn_conv_b", 5632), ("norm_post_ffn", 1024))
SMALL_SHARDED = (("meta_tokens", (N_META, D_MODEL // N_DEV)), ("ssm_conv_w", (SSM_CONV, CONV_DIM // N_DEV)),
                 ("ffn_conv_w", (FFN_CONV, 2 * FFN_DIM // N_DEV)))
BIG = (("w_in", (D_MODEL, N_IN // N_DEV), 1), ("w_ssm_out", (D_INNER // N_DEV, D_MODEL), 0),
       ("w_attn_out", (D_MODEL // N_DEV, D_MODEL), 0), ("w_mix_out", (D_MODEL // N_DEV, D_MODEL), 0),
       ("w_ffn_up", (D_MODEL, 2 * FFN_DIM // N_DEV), 1), ("w_ffn_down", (FFN_DIM // N_DEV, D_MODEL), 0))


def _rows_of(size):
    return -(-size // 128)


def _as_rows(flat):
    size = flat.shape[-1]
    rows = _rows_of(size)
    flat = jnp.pad(flat, [(0, 0)] * (flat.ndim - 1) + [(0, rows * 128 - size)])
    return flat.reshape(flat.shape[:-1] + (rows, 128))


def _pack_small(rep, sharded):
    pieces = [_as_rows(rep[name].reshape(-1)) for name, _ in SMALL_REPLICATED]
    pieces += [_as_rows(sharded[name].reshape(-1)) for name, _ in SMALL_SHARDED]
    packed = jnp.concatenate(pieces, axis=0)
    return jnp.pad(packed, ((0, -packed.shape[0] % 8), (0, 0)))


def _unpack_small(packed):
    out, row = {}, 0
    for name, size in SMALL_REPLICATED:
        out[name] = packed[row:row + _rows_of(size)].reshape(-1)[:size].reshape(1, size)
        row += _rows_of(size)
    for name, (r, c) in SMALL_SHARDED:
        out[name] = packed[row:row + _rows_of(r * c)].reshape(-1)[:r * c].reshape(r, c)
        row += _rows_of(r * c)
    return out


def _shard_major(g, shape, axis):
    r, c = shape
    if axis == 0:
        return g.reshape(N_DEV, r, c)
    return g.reshape(r, N_DEV, c).transpose(1, 0, 2)


def kernel(x, meta_tokens, norm_pre_mix, w_in, ssm_conv_w, ssm_conv_b, ssm_dt_bias, ssm_a_log, ssm_d_skip, ssm_norm, w_ssm_out, attn_sinks, w_attn_out, w_mix_out, norm_post_mix, norm_pre_ffn, w_ffn_up, ffn_conv_w, ffn_conv_b, w_ffn_down, norm_post_ffn, loss_target, m_meta_tokens, m_norm_pre_mix, m_w_in, m_ssm_conv_w, m_ssm_conv_b, m_ssm_dt_bias, m_ssm_a_log, m_ssm_d_skip, m_ssm_norm, m_w_ssm_out, m_attn_sinks, m_w_attn_out, m_w_mix_out, m_norm_post_mix, m_norm_pre_ffn, m_w_ffn_up, m_ffn_conv_w, m_ffn_conv_b, m_w_ffn_down, m_norm_post_ffn, v_meta_tokens, v_norm_pre_mix, v_w_in, v_ssm_conv_w, v_ssm_conv_b, v_ssm_dt_bias, v_ssm_a_log, v_ssm_d_skip, v_ssm_norm, v_w_ssm_out, v_attn_sinks, v_w_attn_out, v_w_mix_out, v_norm_post_mix, v_norm_pre_ffn, v_w_ffn_up, v_ffn_conv_w, v_ffn_conv_b, v_w_ffn_down, v_norm_post_ffn):
    names = ("meta_tokens", "norm_pre_mix", "w_in", "ssm_conv_w", "ssm_conv_b", "ssm_dt_bias", "ssm_a_log", "ssm_d_skip",
             "ssm_norm", "w_ssm_out", "attn_sinks", "w_attn_out", "w_mix_out", "norm_post_mix", "norm_pre_ffn", "w_ffn_up",
             "ffn_conv_w", "ffn_conv_b", "w_ffn_down", "norm_post_ffn")
    w_loc = dict(zip(names, (meta_tokens, norm_pre_mix, w_in, ssm_conv_w, ssm_conv_b, ssm_dt_bias, ssm_a_log, ssm_d_skip,
                             ssm_norm, w_ssm_out, attn_sinks, w_attn_out, w_mix_out, norm_post_mix, norm_pre_ffn, w_ffn_up,
                             ffn_conv_w, ffn_conv_b, w_ffn_down, norm_post_ffn)))
    m_loc = dict(zip(names, (m_meta_tokens, m_norm_pre_mix, m_w_in, m_ssm_conv_w, m_ssm_conv_b, m_ssm_dt_bias, m_ssm_a_log,
                             m_ssm_d_skip, m_ssm_norm, m_w_ssm_out, m_attn_sinks, m_w_attn_out, m_w_mix_out, m_norm_post_mix,
                             m_norm_pre_ffn, m_w_ffn_up, m_ffn_conv_w, m_ffn_conv_b, m_w_ffn_down, m_norm_post_ffn)))
    v_loc = dict(zip(names, (v_meta_tokens, v_norm_pre_mix, v_w_in, v_ssm_conv_w, v_ssm_conv_b, v_ssm_dt_bias, v_ssm_a_log,
                             v_ssm_d_skip, v_ssm_norm, v_w_ssm_out, v_attn_sinks, v_w_attn_out, v_w_mix_out, v_norm_post_mix,
                             v_norm_pre_ffn, v_w_ffn_up, v_ffn_conv_w, v_ffn_conv_b, v_w_ffn_down, v_norm_post_ffn)))

    def local2d(d, name):
        a = d[name]
        return a if name == "meta_tokens" else a.reshape(a.shape[1:])

    def turned2d(d, name):
        a = jnp.swapaxes(d[name], 1, 2)
        return a.reshape(a.shape[1:])

    my_id = 4 * lax.axis_index("x") + 2 * lax.axis_index("y") + lax.axis_index("c")
    big = {name: (shape, axis) for name, shape, axis in BIG}

    def whole(name, g):
        return g.reshape(N_DEV * g.shape[1], g.shape[2])

    def key(name):
        return name + "_t" if big[name][1] == 1 else name

    by_rows = [name for name, _, axis in BIG if axis == 0]
    send_bf16 = dict(zip(by_rows, _to_bf16([local2d(w_loc, name) for name in by_rows])))
    for name, _, axis in BIG:
        if axis == 1:
            send_bf16[name] = turned2d(w_loc, name).astype(BF16)
    small_shard_pack = jnp.concatenate([_as_rows(local2d(w_loc, name).reshape(-1)) for name, _ in SMALL_SHARDED], axis=0)
    small_shard_pack = jnp.pad(small_shard_pack, ((0, -small_shard_pack.shape[0] % 8), (0, 0)))
    first = _all_gather([send_bf16["w_in"], small_shard_pack])
    rest_names = [name for name, _, _ in BIG if name != "w_in"]
    rest = [send_bf16[name] for name in rest_names]
    rest, first = lax.optimization_barrier((rest, first))
    rest_handle = _push_start(rest, False, "gather_rest_start")
    wt = {"w_in_t": whole("w_in", first[0])}
    row = 0
    for name, (r, c) in SMALL_SHARDED:
        blocks = first[1][:, row:row + _rows_of(r * c)].reshape(N_DEV, -1)[:, :r * c].reshape(N_DEV, r, c)
        wt[name] = blocks.transpose(1, 0, 2).reshape(r, N_DEV * c)
        row += _rows_of(r * c)
    for name, size in SMALL_REPLICATED:
        wt[name] = w_loc[name].reshape(1, size)

    def late_weights(after):
        own, landed = _push_wait(rest_handle, after, "gather_rest_wait")
        out = {}
        for name, mine, land in zip(rest_names, own, landed):
            out[key(name)] = whole(name, lax.dynamic_update_index_in_dim(land, mine, my_id, 0))
        return out

    sent = {}

    def on_grad(known_as, g):
        name = known_as.removesuffix("_t")
        by_owner = g.reshape(N_DEV, g.shape[0] // N_DEV, g.shape[1])
        sent[name] = _push_start([by_owner], True, "send_" + name)
        return sent[name]["token"]

    loss_part, grad_x, grads = _local_step(x[0], loss_target[0], wt, late_weights, on_grad, rest_handle["token"])
    loss = lax.psum(loss_part, AXES)

    small_parts = []
    for name, (r, c) in SMALL_SHARDED:
        small_parts.append(_as_rows(_shard_major(grads[name], (r, c), 1).reshape(N_DEV, r * c)))
    rep_rows = jnp.concatenate([_as_rows(grads[name].reshape(-1)) for name, _ in SMALL_REPLICATED], axis=0)
    small_send = jnp.concatenate([jnp.broadcast_to(rep_rows[None], (N_DEV,) + rep_rows.shape)] + small_parts, axis=1)
    small_send = jnp.pad(small_send, ((0, 0), (0, -small_send.shape[1] % 8), (0, 0)))
    small_handle = _push_start([small_send], True, "send_small")

    def small_pack(d):
        return _pack_small({name: d[name] for name, _ in SMALL_REPLICATED}, {name: local2d(d, name) for name, _ in SMALL_SHARDED})

    def arrived(handle, after, name):
        src, landed = _push_wait(handle, after, "arrived_" + name)
        return landed[0], lax.dynamic_index_in_dim(src[0], my_id, 0, keepdims=False)

    grad_w, delta_w, new_m, new_v = {}, {}, {}, {}
    outs = None
    after = small_handle["token"]
    for name, handle in sent.items():
        if name == "w_in":
            parts, own = arrived(small_handle, after, "small")
            outs = _adamw(parts, own, small_pack(w_loc), small_pack(m_loc), small_pack(v_loc), "adamw_small")
            after = outs[0]
        parts, own = arrived(handle, after, name)
        turned = big[name][1] == 1
        unit_rows = turned and big[name][0][1] % 8 != 0
        if unit_rows:
            state = [jnp.transpose(d[name], (2, 0, 1)) for d in (w_loc, m_loc, v_loc)]
        else:
            state = [turned2d(d, name) if turned else local2d(d, name) for d in (w_loc, m_loc, v_loc)]
        results = _adamw(parts, own, *state, "adamw_" + name)
        after = results[0]
        full = (1,) + big[name][0]
        for dst, a in zip((grad_w, delta_w, new_m, new_v), results):
            if unit_rows:
                dst[name] = jnp.transpose(a, (1, 2, 0))
            else:
                dst[name] = jnp.swapaxes(a[None], 1, 2) if turned else a.reshape(full)
    for dst, packed in zip((grad_w, delta_w, new_m, new_v), outs):
        for name, a in _unpack_small(packed).items():
            dst[name] = a.reshape(w_loc[name].shape)

    return (loss, grad_x[None], *[grad_w[n] for n in names], *[delta_w[n] for n in names],
            *[new_m[n] for n in names], *[new_v[n] for n in names])
```

```python
import jax
import jax.numpy as jnp
from jax import lax
from jax.experimental import pallas as pl
from jax.experimental.pallas import tpu as pltpu

F32 = jnp.float32
BF16 = jnp.bfloat16

D_MODEL = 1024
N_META = 16
CHUNK = 128
META_PAD = CHUNK - N_META
D_INNER = 2048
HEAD_P = 64
SSM_HEADS = 32
SSM_GROUPS = 4
HEADS_PER_GROUP = SSM_HEADS // SSM_GROUPS
GROUP_W = HEADS_PER_GROUP * HEAD_P
D_STATE = 128
SSM_CONV = 4
CONV_DIM = D_INNER + 2 * SSM_GROUPS * D_STATE
ATTN_HEADS = 16
KV_HEADS = 4
ATTN_GROUP = ATTN_HEADS // KV_HEADS
DH = 64
KV_W = KV_HEADS * DH
FFN_DIM = 2816
FFN_CONV = 3
EPS = 1e-6
NEG = -1e30
N_DEV = 8
AXES = ("x", "y", "c")

OFF_Z, OFF_GATE, OFF_DT, OFF_Q, OFF_K, OFF_V, OFF_XBC = 0, 2048, 4096, 4608, 5632, 5888, 6144
N_INP = OFF_XBC + CONV_DIM
QKV_W = OFF_XBC - OFF_Q
CUT_Z, CUT_XBC, CUT_DT, CUT_Q, CUT_K, CUT_V, CUT_G = 0, 2048, 5120, 5152, 6176, 6432, 6688
N_IN = 8736

ADAM_LR, ADAM_B1, ADAM_B2, ADAM_EPS, ADAM_WD, ADAM_STEP = 0.001, 0.9, 0.999, 1e-08, 0.01, 10

VMEM_LIMIT = 56 * 1024 * 1024


def _params(n_grid):
    return pltpu.CompilerParams(dimension_semantics=("arbitrary",) * n_grid, vmem_limit_bytes=VMEM_LIMIT)


def _sds(shape, dtype):
    return jax.ShapeDtypeStruct(shape, dtype)


def _pick(n, prefs):
    for c in prefs:
        if n % c == 0:
            return c
    raise ValueError(f"no tile of {prefs} divides {n}")


def _row(tr, width, cb=0):
    return pl.BlockSpec((tr, width), lambda i: (i, cb))


def _row_rev(tr, width, nt, cb=0):
    return pl.BlockSpec((tr, width), lambda i: (nt - 1 - i, cb))


def _full(shape):
    return pl.BlockSpec(shape, lambda *_: (0,) * len(shape))


def _sigmoid(x):
    return 1.0 / (1.0 + jnp.exp(-x))


def _softplus(x):
    return jnp.maximum(x, 0.0) + jnp.log(1.0 + jnp.exp(-jnp.abs(x)))


def _rms(x):
    return lax.rsqrt(jnp.mean(x * x, axis=-1, keepdims=True) + EPS)


def _rms_bwd(x, r, w, dy):
    xh = x * r
    g = dy * w
    dx = r * (g - xh * jnp.mean(g * xh, axis=-1, keepdims=True))
    return dx, jnp.sum(dy * xh, axis=0, keepdims=True)


def _row_ids(shape, tile_index, tr):
    return tile_index * tr + lax.broadcasted_iota(jnp.int32, shape, 0)


HALO = 8
STRIP = 256
STRIP_BWD = 128


def _causal_taps(x, halo, first_step, taps):
    n = x.shape[0]

    @pl.when(first_step)
    def _():
        halo[...] = jnp.zeros_like(halo)

    before = halo[...]
    row = lax.broadcasted_iota(jnp.int32, before.shape, 0)
    shifted = [x]
    for s in range(1, taps):
        rolled = pltpu.roll(x, s, 0)
        head = jnp.where(row < s, pltpu.roll(before, s, 0), rolled[0:HALO, :])
        shifted.append(jnp.concatenate([head, rolled[HALO:, :]], axis=0))
    halo[...] = x[n - HALO:, :]
    return shifted


def _anticausal_taps(x, halo, first_step, taps):
    n = x.shape[0]

    @pl.when(first_step)
    def _():
        halo[...] = jnp.zeros_like(halo)

    after = halo[...]
    row = lax.broadcasted_iota(jnp.int32, after.shape, 0)
    shifted = [x]
    for s in range(1, taps):
        rolled = pltpu.roll(x, n - s, 0)
        tail = jnp.where(row >= HALO - s, pltpu.roll(after, HALO - s, 0), rolled[n - HALO:, :])
        shifted.append(jnp.concatenate([rolled[:n - HALO, :], tail], axis=0))
    halo[...] = x[0:HALO, :]
    return shifted


def _matmul(a, b, *, ta=False, tb=False, out_dtype=F32, name, after=None):
    if ta:
        k_dim, m_dim = a.shape
    else:
        m_dim, k_dim = a.shape
    n_dim = b.shape[0] if tb else b.shape[1]
    tm = _pick(m_dim, (1408, 1024, 768, 512, 384, 256, 128))
    tn = _pick(n_dim, (1024, 1408, 768, 512, 384, 256, 128))
    if ta:
        tk = _pick(k_dim, (1408, 1024, 768, 512, 384, 256, 128))
    else:
        tk = k_dim if k_dim <= 3072 else _pick(k_dim, (3072, 2816, 2048, 1024))
    nk = k_dim // tk
    dims = (((0 if ta else 1,), (1 if tb else 0,)), ((), ()))

    use_acc = nk > 1 and out_dtype != F32

    def body(a_ref, b_ref, *rest):
        o_ref = rest[-2] if use_acc else rest[-1]
        acc_ref = rest[-1] if use_acc else o_ref
        r = lax.dot_general(a_ref[...].astype(BF16), b_ref[...].astype(BF16), dims, preferred_element_type=F32)
        if nk == 1:
            o_ref[...] = r.astype(o_ref.dtype)
        else:
            k = pl.program_id(2)

            @pl.when(k == 0)
            def _():
                acc_ref[...] = r

            @pl.when(k > 0)
            def _():
                acc_ref[...] += r

            if use_acc:
                @pl.when(k == nk - 1)
                def _():
                    o_ref[...] = acc_ref[...].astype(o_ref.dtype)

    a_spec = pl.BlockSpec((tk, tm), lambda i, j, k: (k, i)) if ta else pl.BlockSpec((tm, tk), lambda i, j, k: (i, k))
    b_spec = pl.BlockSpec((tn, tk), lambda i, j, k: (j, k)) if tb else pl.BlockSpec((tk, tn), lambda i, j, k: (k, j))
    extra_specs, extra = ([], ()) if after is None else ([pl.BlockSpec(memory_space=pl.ANY)], (after,))
    return pl.pallas_call(
        body, grid=(m_dim // tm, n_dim // tn, nk), in_specs=[a_spec, b_spec] + extra_specs,
        out_specs=pl.BlockSpec((tm, tn), lambda i, j, k: (i, j)), out_shape=_sds((m_dim, n_dim), out_dtype),
        scratch_shapes=[pltpu.VMEM((tm, tn), F32)] if use_acc else [],
        name=name, compiler_params=_params(3))(a, b, *extra)


def _seq_rows(t_rows):
    return 384 if t_rows % 384 == 0 and t_rows >= 768 else CHUNK


def _token_rows(tr):
    if tr == CHUNK:
        return pl.BlockSpec((CHUNK, D_MODEL), lambda i: (jnp.maximum(i - 1, 0), 0))
    return pl.BlockSpec((pl.Element(tr), pl.Element(D_MODEL)),
                        lambda i: (pl.multiple_of(jnp.maximum(i * tr - CHUNK, 0), CHUNK), 0))


def _under_tile(rows_ref, head, i):
    rows = rows_ref[...]
    tr = rows.shape[0]
    first = head if tr == CHUNK else jnp.concatenate([head, rows[0:tr - CHUNK, :]], axis=0)
    return jnp.where(i == 0, first, rows)


def _seq_specs(tr=CHUNK):
    return [_token_rows(tr), _full((N_META, D_MODEL))]


def _seq_tile(x_ref, meta_ref, i):
    return _under_tile(x_ref, jnp.concatenate([jnp.zeros((META_PAD, D_MODEL), F32), meta_ref[...]], axis=0), i)


def _prenorm(x, meta, w):
    t_rows = x.shape[0] + CHUNK
    tr = _seq_rows(t_rows)

    def body(x_ref, meta_ref, w_ref, o_ref):
        h = _seq_tile(x_ref, meta_ref, pl.program_id(0))
        o_ref[...] = (h * _rms(h) * w_ref[...]).astype(BF16)

    return pl.pallas_call(body, grid=(t_rows // tr,), in_specs=_seq_specs(tr) + [_full((1, D_MODEL))],
                          out_specs=_row(tr, D_MODEL), out_shape=_sds((t_rows, D_MODEL), BF16),
                          name="prenorm", compiler_params=_params(1))(x, meta, w)


def _ssm_conv_fwd(proj, conv_w, conv_b):
    t_rows = proj.shape[0]
    tr = CHUNK

    def body(x_ref, w_ref, b_ref, xc_ref, xa_ref, hist):
        first = pl.program_id(0) == 0
        for c in range(0, CONV_DIM, STRIP):
            cols = slice(c, c + STRIP)
            acc = b_ref[:, cols]
            for s, moved in enumerate(_causal_taps(x_ref[:, cols], hist.at[:, cols], first, SSM_CONV)):
                acc = acc + w_ref[SSM_CONV - 1 - s:SSM_CONV - s, cols] * moved
            xc_ref[:, cols] = acc
            xa_ref[:, cols] = acc * _sigmoid(acc)

    return pl.pallas_call(
        body, grid=(t_rows // tr,),
        in_specs=[_row(tr, CONV_DIM, OFF_XBC // CONV_DIM), _full((SSM_CONV, CONV_DIM)), _full((1, CONV_DIM))],
        out_specs=[_row(tr, CONV_DIM), _row(tr, CONV_DIM)],
        out_shape=[_sds((t_rows, CONV_DIM), F32), _sds((t_rows, CONV_DIM), F32)],
        scratch_shapes=[pltpu.VMEM((HALO, CONV_DIM), F32)],
        name="ssm_conv_fwd", compiler_params=_params(1))(proj, conv_w, conv_b)


def _ssm_post(y, proj, w):
    t_rows = y.shape[0]
    tr = CHUNK

    def body(y_ref, z_ref, w_ref, o_ref):
        z = z_ref[...].astype(F32)
        yz = y_ref[...] * z * _sigmoid(z)
        o_ref[...] = (yz * _rms(yz) * w_ref[...]).astype(BF16)

    return pl.pallas_call(body, grid=(t_rows // tr,),
                          in_specs=[_row(tr, D_INNER), _row(tr, D_INNER, OFF_Z // D_INNER), _full((1, D_INNER))],
                          out_specs=_row(tr, D_INNER), out_shape=_sds((t_rows, D_INNER), BF16),
                          name="ssm_post", compiler_params=_params(1))(y, proj, w)


def _mix_fwd(proj, y_ssm, y_attn):
    t_rows = y_ssm.shape[0]
    tr = _pick(t_rows, (384, 128))

    def body(g_ref, ys_ref, ya_ref, o_ref):
        g = _sigmoid(g_ref[...].astype(F32))
        o_ref[...] = (g[:, :D_MODEL] * ys_ref[...] + g[:, D_MODEL:] * ya_ref[...]).astype(BF16)

    return pl.pallas_call(body, grid=(t_rows // tr,),
                          in_specs=[_row(tr, 2 * D_MODEL, OFF_GATE // (2 * D_MODEL)), _row(tr, D_MODEL),
                                    _row(tr, D_MODEL)],
                          out_specs=_row(tr, D_MODEL), out_shape=_sds((t_rows, D_MODEL), BF16),
                          name="mix_fwd", compiler_params=_params(1))(proj, y_ssm, y_attn)


def _postmix(x, meta, mix, w_post, w_pre):
    t_rows = mix.shape[0]
    tr = _seq_rows(t_rows)

    def body(x_ref, meta_ref, m_ref, wp_ref, wf_ref, h1_ref, hn_ref):
        m = m_ref[...]
        h1 = _seq_tile(x_ref, meta_ref, pl.program_id(0)) + m * _rms(m) * wp_ref[...]
        h1 = jnp.where(_row_ids(h1.shape, pl.program_id(0), tr) >= META_PAD, h1, 0.0)
        h1_ref[...] = h1
        hn_ref[...] = (h1 * _rms(h1) * wf_ref[...]).astype(BF16)

    return pl.pallas_call(body, grid=(t_rows // tr,),
                          in_specs=_seq_specs(tr) + [_row(tr, D_MODEL), _full((1, D_MODEL)), _full((1, D_MODEL))],
                          out_specs=[_row(tr, D_MODEL), _row(tr, D_MODEL)],
                          out_shape=[_sds((t_rows, D_MODEL), F32), _sds((t_rows, D_MODEL), BF16)],
                          name="postmix", compiler_params=_params(1))(x, meta, mix, w_post, w_pre)


def _ffn_act(up, conv_w, conv_b):
    t_rows = up.shape[0]
    tr = CHUNK
    width = 2 * FFN_DIM

    def body(up_ref, w_ref, b_ref, u_ref, act_ref, hist):
        first = pl.program_id(0) == 0
        for c in range(0, FFN_DIM, STRIP):
            halves = []
            for base in (0, FFN_DIM):
                cols = slice(base + c, base + c + STRIP)
                u = b_ref[:, cols]
                for s, moved in enumerate(_causal_taps(up_ref[:, cols].astype(F32), hist.at[:, cols], first, FFN_CONV)):
                    u = u + w_ref[FFN_CONV - 1 - s:FFN_CONV - s, cols] * moved
                u_ref[:, cols] = u.astype(BF16)
                halves.append(u)
            a, g = halves
            act_ref[:, c:c + STRIP] = (a * _sigmoid(a) * g).astype(BF16)

    return pl.pallas_call(
        body, grid=(t_rows // tr,), in_specs=[_row(tr, width), _full((FFN_CONV, width)), _full((1, width))],
        out_specs=[_row(tr, width), _row(tr, FFN_DIM)],
        out_shape=[_sds((t_rows, width), BF16), _sds((t_rows, FFN_DIM), BF16)],
        scratch_shapes=[pltpu.VMEM((HALO, width), F32)],
        name="ffn_act", compiler_params=_params(1))(up, conv_w, conv_b)


def _final(h1, f, target, w):
    t_rows = h1.shape[0]
    tr = _seq_rows(t_rows)

    def body(h1_ref, f_ref, t_ref, w_ref, df_ref, dy_ref, dw_ref, loss_ref):
        i = pl.program_id(0)

        @pl.when(i == 0)
        def _():
            dw_ref[...] = jnp.zeros_like(dw_ref)
            loss_ref[...] = jnp.zeros_like(loss_ref)

        f_val = f_ref[...]
        r = _rms(f_val)
        wv = w_ref[...]
        h2 = h1_ref[...] + f_val * r * wv
        tgt = _under_tile(t_ref, jnp.zeros((CHUNK, D_MODEL), F32), i)
        diff = jnp.where(_row_ids(h2.shape, i, tr) >= CHUNK, h2 - tgt, 0.0)
        loss_ref[...] += 0.5 * jnp.sum(diff * diff) * (1.0 / D_MODEL)
        dy = diff * (1.0 / D_MODEL)
        dy_ref[...] = dy
        df, dw = _rms_bwd(f_val, r, wv, dy)
        df_ref[...] = df.astype(BF16)
        dw_ref[...] += dw

    return pl.pallas_call(
        body, grid=(t_rows // tr,),
        in_specs=[_row(tr, D_MODEL), _row(tr, D_MODEL), _token_rows(tr), _full((1, D_MODEL))],
        out_specs=[_row(tr, D_MODEL), _row(tr, D_MODEL), _full((1, D_MODEL)), _full((1, 128))],
        out_shape=[_sds((t_rows, D_MODEL), BF16), _sds((t_rows, D_MODEL), F32), _sds((1, D_MODEL), F32), _sds((1, 128), F32)],
        name="final", compiler_params=_params(1))(h1, f, target, w)


def _ffn_act_bwd(u, up, dact, conv_w):
    t_rows = u.shape[0]
    tr = CHUNK
    nt = t_rows // tr
    width = 2 * FFN_DIM

    def body(u_ref, up_ref, da_ref, w_ref, dup_ref, dw_ref, db_ref, ahead):
        @pl.when(pl.program_id(0) == 0)
        def _():
            dw_ref[...] = jnp.zeros_like(dw_ref)
            db_ref[...] = jnp.zeros_like(db_ref)

        first = pl.program_id(0) == 0
        for c in range(0, FFN_DIM, STRIP_BWD):
            ca, cg = slice(c, c + STRIP_BWD), slice(FFN_DIM + c, FFN_DIM + c + STRIP_BWD)
            a, g, d = u_ref[:, ca].astype(F32), u_ref[:, cg].astype(F32), da_ref[:, ca].astype(F32)
            s = _sigmoid(a)
            for cols, du in ((ca, d * g * s * (1.0 + a * (1.0 - s))), (cg, d * a * s)):
                x = up_ref[:, cols].astype(F32)
                dup = None
                for sh, moved in enumerate(_anticausal_taps(du, ahead.at[:, cols], first, FFN_CONV)):
                    k = FFN_CONV - 1 - sh
                    term = w_ref[k:k + 1, cols] * moved
                    dup = term if dup is None else dup + term
                    dw_ref[k:k + 1, cols] += jnp.sum(moved * x, axis=0, keepdims=True)
                db_ref[:, cols] += jnp.sum(du, axis=0, keepdims=True)
                dup_ref[:, cols] = dup.astype(BF16)

    return pl.pallas_call(
        body, grid=(nt,),
        in_specs=[_row_rev(tr, width, nt), _row_rev(tr, width, nt), _row_rev(tr, FFN_DIM, nt), _full((FFN_CONV, width))],
        out_specs=[_row_rev(tr, width, nt), _full((FFN_CONV, width)), _full((1, width))],
        out_shape=[_sds((t_rows, width), BF16), _sds((FFN_CONV, width), F32), _sds((1, width), F32)],
        scratch_shapes=[pltpu.VMEM((HALO, width), F32)],
        name="ffn_act_bwd", compiler_params=_params(1))(u, up, dact, conv_w)


def _postmix_bwd(h1, dhn2, dy, mix, w_pre, w_post):
    t_rows = h1.shape[0]
    tr = _pick(t_rows, (384, 128))

    def body(h1_ref, dhn_ref, dy_ref, m_ref, wf_ref, wp_ref, dmix_ref, dh_ref, dwf_ref, dwp_ref):
        @pl.when(pl.program_id(0) == 0)
        def _():
            dwf_ref[...] = jnp.zeros_like(dwf_ref)
            dwp_ref[...] = jnp.zeros_like(dwp_ref)

        h1v = h1_ref[...]
        dx, dwf = _rms_bwd(h1v, _rms(h1v), wf_ref[...], dhn_ref[...])
        dwf_ref[...] += dwf
        dh1 = dy_ref[...] + dx
        dh1 = jnp.where(_row_ids(dh1.shape, pl.program_id(0), tr) >= META_PAD, dh1, 0.0)
        dh_ref[...] = dh1
        m = m_ref[...]
        dmix, dwp = _rms_bwd(m, _rms(m), wp_ref[...], dh1)
        dwp_ref[...] += dwp
        dmix_ref[...] = dmix.astype(BF16)

    return pl.pallas_call(
        body, grid=(t_rows // tr,),
        in_specs=[_row(tr, D_MODEL) for _ in range(4)] + [_full((1, D_MODEL))] * 2,
        out_specs=[_row(tr, D_MODEL), _row(tr, D_MODEL), _full((1, D_MODEL)), _full((1, D_MODEL))],
        out_shape=[_sds((t_rows, D_MODEL), BF16), _sds((t_rows, D_MODEL), F32), _sds((1, D_MODEL), F32), _sds((1, D_MODEL), F32)],
        name="postmix_bwd", compiler_params=_params(1))(h1, dhn2, dy, mix, w_pre, w_post)


_ANY = pl.BlockSpec(memory_space=pl.ANY)


def _mix_bwd(dmixed, proj, y_ssm, y_attn, dproj):
    t_rows = dmixed.shape[0]
    tr = _pick(t_rows, (384, 128))

    def body(d_ref, g_ref, ys_ref, ya_ref, _, dys_ref, dya_ref, dg_ref):
        d = d_ref[...]
        g = _sigmoid(g_ref[...].astype(F32))
        g1, g2 = g[:, :D_MODEL], g[:, D_MODEL:]
        dys_ref[...] = (d * g1).astype(BF16)
        dya_ref[...] = (d * g2).astype(BF16)
        dg_ref[...] = jnp.concatenate([d * ys_ref[...] * g1 * (1.0 - g1), d * ya_ref[...] * g2 * (1.0 - g2)],
                                      axis=1).astype(BF16)

    return pl.pallas_call(
        body, grid=(t_rows // tr,),
        in_specs=[_row(tr, D_MODEL), _row(tr, 2 * D_MODEL, OFF_GATE // (2 * D_MODEL)), _row(tr, D_MODEL), _row(tr, D_MODEL),
                  _ANY],
        out_specs=[_row(tr, D_MODEL), _row(tr, D_MODEL), _row(tr, 2 * D_MODEL, OFF_GATE // (2 * D_MODEL))],
        out_shape=[_sds((t_rows, D_MODEL), BF16), _sds((t_rows, D_MODEL), BF16), _sds(dproj.shape, dproj.dtype)],
        input_output_aliases={4: 2},
        name="mix_bwd", compiler_params=_params(1))(dmixed, proj, y_ssm, y_attn, dproj)


def _ssm_post_bwd(y, proj, dyn, w, dproj):
    t_rows = y.shape[0]
    tr = CHUNK

    def body(y_ref, z_ref, d_ref, w_ref, _, dy_ref, dz_ref, dw_ref):
        @pl.when(pl.program_id(0) == 0)
        def _():
            dw_ref[...] = jnp.zeros_like(dw_ref)

        yv, z = y_ref[...], z_ref[...].astype(F32)
        sz = _sigmoid(z)
        silu = z * sz
        yz = yv * silu
        dyz, dw = _rms_bwd(yz, _rms(yz), w_ref[...], d_ref[...].astype(F32))
        dw_ref[...] += dw
        dy_ref[...] = dyz * silu
        dz_ref[...] = (dyz * yv * sz * (1.0 + z * (1.0 - sz))).astype(BF16)

    return pl.pallas_call(
        body, grid=(t_rows // tr,),
        in_specs=[_row(tr, D_INNER), _row(tr, D_INNER, OFF_Z // D_INNER), _row(tr, D_INNER), _full((1, D_INNER)), _ANY],
        out_specs=[_row(tr, D_INNER), _row(tr, D_INNER, OFF_Z // D_INNER), _full((1, D_INNER))],
        out_shape=[_sds((t_rows, D_INNER), F32), _sds(dproj.shape, dproj.dtype), _sds((1, D_INNER), F32)],
        input_output_aliases={4: 1},
        name="ssm_post_bwd", compiler_params=_params(1))(y, proj, dyn, w, dproj)


def _ssm_conv_bwd(xc, proj, dxs, dbm, dcm, conv_w, dproj):
    t_rows = xc.shape[0]
    tr = CHUNK
    nt = t_rows // tr
    bc_w = SSM_GROUPS * D_STATE

    def body(xc_ref, x_ref, dxs_ref, db_ref, dc_ref, w_ref, _, dx_ref, dw_ref, dbias_ref, ahead):
        first = pl.program_id(0) == 0

        @pl.when(first)
        def _():
            dw_ref[...] = jnp.zeros_like(dw_ref)
            dbias_ref[...] = jnp.zeros_like(dbias_ref)

        for c0 in range(0, CONV_DIM, STRIP_BWD):
            cols = slice(c0, c0 + STRIP_BWD)
            if c0 < D_INNER:
                dact = dxs_ref[:, cols]
            elif c0 < D_INNER + bc_w:
                dact = db_ref[:, c0 - D_INNER:c0 - D_INNER + STRIP_BWD]
            else:
                dact = dc_ref[:, c0 - D_INNER - bc_w:c0 - D_INNER - bc_w + STRIP_BWD]
            c = xc_ref[:, cols]
            s = _sigmoid(c)
            dpre = dact * s * (1.0 + c * (1.0 - s))
            x = x_ref[:, cols]
            dx = None
            for sh, moved in enumerate(_anticausal_taps(dpre, ahead.at[:, cols], first, SSM_CONV)):
                k = SSM_CONV - 1 - sh
                term = w_ref[k:k + 1, cols] * moved
                dx = term if dx is None else dx + term
                dw_ref[k:k + 1, cols] += jnp.sum(moved * x, axis=0, keepdims=True)
            dbias_ref[:, cols] += jnp.sum(dpre, axis=0, keepdims=True)
            dx_ref[:, cols] = dx.astype(BF16)

    xbc_block = OFF_XBC // CONV_DIM
    return pl.pallas_call(
        body, grid=(nt,),
        in_specs=[_row_rev(tr, CONV_DIM, nt), _row_rev(tr, CONV_DIM, nt, xbc_block), _row_rev(tr, D_INNER, nt),
                  _row_rev(tr, bc_w, nt), _row_rev(tr, bc_w, nt), _full((SSM_CONV, CONV_DIM)), _ANY],
        out_specs=[_row_rev(tr, CONV_DIM, nt, xbc_block), _full((SSM_CONV, CONV_DIM)), _full((1, CONV_DIM))],
        out_shape=[_sds(dproj.shape, dproj.dtype), _sds((SSM_CONV, CONV_DIM), F32), _sds((1, CONV_DIM), F32)],
        scratch_shapes=[pltpu.VMEM((HALO, CONV_DIM), F32)],
        input_output_aliases={6: 0},
        name="ssm_conv_bwd", compiler_params=_params(1))(xc, proj, dxs, dbm, dcm, conv_w, dproj)


def _prenorm_bwd(x, meta, dhn, dh, w):
    t_rows = dhn.shape[0]
    tr = CHUNK

    def body(x_ref, meta_ref, d_ref, r_ref, w_ref, dx_ref, dmeta_ref, dw_ref):
        i = pl.program_id(0)

        @pl.when(i == 0)
        def _():
            dw_ref[...] = jnp.zeros_like(dw_ref)

        h = _seq_tile(x_ref, meta_ref, i)
        dx, dw = _rms_bwd(h, _rms(h), w_ref[...], d_ref[...])
        dw_ref[...] += dw
        dh_tile = r_ref[...] + dx
        dx_ref[...] = dh_tile

        @pl.when(i == 0)
        def _():
            dmeta_ref[...] = dh_tile[META_PAD:, :]

    return pl.pallas_call(
        body, grid=(t_rows // tr,), in_specs=_seq_specs() + [_row(tr, D_MODEL), _row(tr, D_MODEL), _full((1, D_MODEL))],
        out_specs=[pl.BlockSpec((tr, D_MODEL), lambda i: (jnp.maximum(i - 1, 0), 0)), _full((N_META, D_MODEL)),
                   _full((1, D_MODEL))],
        out_shape=[_sds((t_rows - tr, D_MODEL), F32), _sds((N_META, D_MODEL), F32), _sds((1, D_MODEL), F32)],
        name="prenorm_bwd", compiler_params=_params(1))(x, meta, dhn, dh, w)


def _dot01(x, m01, x_left, parts):
    acc, rest = None, x
    for i in range(parts):
        piece = rest.astype(BF16)
        term = (jnp.dot(piece, m01, preferred_element_type=F32) if x_left
                else jnp.dot(m01, piece, preferred_element_type=F32))
        acc = term if acc is None else acc + term
        if i + 1 < parts:
            rest = rest - piece.astype(F32)
    return acc


def _ssd_common(dt_raw, dt_bias, a_log, chunk_index):
    rows = lax.broadcasted_iota(jnp.int32, (CHUNK, CHUNK), 0)
    cols = lax.broadcasted_iota(jnp.int32, (CHUNK, CHUNK), 1)
    low = rows >= cols
    raw = dt_raw + dt_bias
    live = _row_ids(raw.shape, chunk_index, CHUNK) >= META_PAD
    dt = jnp.where(live, _softplus(raw), 0.0)
    a_head = -jnp.exp(a_log)
    cs = _dot01(dt * a_head, low.astype(BF16), False, 3)
    grow = jnp.exp(cs)
    fade = jnp.exp(cs[CHUNK - 1:CHUNK, :] - cs)
    expand = (lax.broadcasted_iota(jnp.int32, (CHUNK, GROUP_W), 1) // HEAD_P
              == lax.broadcasted_iota(jnp.int32, (CHUNK, GROUP_W), 0)).astype(BF16)
    fold = (lax.broadcasted_iota(jnp.int32, (GROUP_W, CHUNK), 0) // HEAD_P
            == lax.broadcasted_iota(jnp.int32, (GROUP_W, CHUNK), 1)).astype(BF16)
    return dict(low=low, triu=(rows <= cols).astype(BF16), raw=raw, live=live, dt=dt, a_head=a_head, cs=cs, cs_t=cs.T,
                fold=fold, dtx=_dot01(dt, expand, True, 2), growx=_dot01(grow, expand, True, 2),
                fadex=_dot01(fade, expand, True, 2))


def _decay_matrix(cm, j):
    diff = cm["cs"][:, j:j + 1] - cm["cs_t"][j:j + 1, :]
    return jnp.where(cm["low"], jnp.exp(jnp.where(cm["low"], diff, 0.0)), 0.0)


def _dot(a, b, dims):
    return lax.dot_general(a.astype(BF16), b.astype(BF16), (dims, ((), ())), preferred_element_type=F32)


def _dot_fine(a, b, dims):
    a_hi, b_hi = a.astype(BF16), b.astype(BF16)
    a_lo, b_lo = (a - a_hi.astype(F32)).astype(BF16), (b - b_hi.astype(F32)).astype(BF16)
    dn = (dims, ((), ()))
    return (lax.dot_general(a_hi, b_hi, dn, preferred_element_type=F32)
            + lax.dot_general(a_hi, b_lo, dn, preferred_element_type=F32)
            + lax.dot_general(a_lo, b_hi, dn, preferred_element_type=F32))


def _ssd_specs(nt, rev):
    def idx(c):
        return nt - 1 - c if rev else c
    bc_w = SSM_GROUPS * D_STATE
    xs = pl.BlockSpec((CHUNK, D_INNER), lambda c: (idx(c), 0))
    bm = pl.BlockSpec((CHUNK, bc_w), lambda c: (idx(c), D_INNER // bc_w))
    cm = pl.BlockSpec((CHUNK, bc_w), lambda c: (idx(c), D_INNER // bc_w + 1))
    dtr = pl.BlockSpec((CHUNK, SSM_GROUPS * 128), lambda c: (idx(c), OFF_DT // (SSM_GROUPS * 128)))
    par = _full((SSM_GROUPS, 1, 128))
    par_x = _full((SSM_GROUPS, 1, GROUP_W))
    return xs, bm, cm, dtr, par, par_x, idx


def _group_cols(g, width):
    return slice(g * width, (g + 1) * width)


def _ssd_fwd(xact, proj, dtb, alog, dskip_x):
    t_rows = xact.shape[0]
    nt = t_rows // CHUNK
    xs_spec, b_spec, c_spec, dtr_spec, par, par_x, _ = _ssd_specs(nt, False)

    def body(xs_ref, b_ref, c_ref, dtr_ref, dtb_ref, alog_ref, dsk_ref, y_ref, hst_ref, state):
        c = pl.program_id(0)

        @pl.when(c == 0)
        def _():
            state[...] = jnp.zeros_like(state)

        for g in range(SSM_GROUPS):
            wide, narrow = _group_cols(g, GROUP_W), _group_cols(g, D_STATE)
            cm = _ssd_common(dtr_ref[:, narrow], dtb_ref[g], alog_ref[g], c)
            xs, bm, cmat = xs_ref[:, wide], b_ref[:, narrow], c_ref[:, narrow]
            x_dt = xs * cm["dtx"]
            h_in = state[g]
            hst_ref[0, g] = h_in
            y_ref[:, wide] = _dot(cmat, h_in, ((1,), (0,))) * cm["growx"] + xs * dsk_ref[g]
            cb = _dot(cmat, bm, ((1,), (1,)))
            for j in range(HEADS_PER_GROUP):
                sl = slice(g * GROUP_W + j * HEAD_P, g * GROUP_W + (j + 1) * HEAD_P)
                y_ref[:, sl] += _dot(cb * _decay_matrix(cm, j), x_dt[:, j * HEAD_P:(j + 1) * HEAD_P], ((1,), (0,)))
            state[g] = h_in * cm["growx"][CHUNK - 1:CHUNK, :] + _dot_fine(bm, x_dt * cm["fadex"], ((0,), (0,)))

    return pl.pallas_call(
        body, grid=(nt,),
        in_specs=[xs_spec, b_spec, c_spec, dtr_spec, par, par, par_x],
        out_specs=[xs_spec, pl.BlockSpec((1, SSM_GROUPS, D_STATE, GROUP_W), lambda c: (c, 0, 0, 0))],
        out_shape=[_sds((t_rows, D_INNER), F32), _sds((nt, SSM_GROUPS, D_STATE, GROUP_W), F32)],
        scratch_shapes=[pltpu.VMEM((SSM_GROUPS, D_STATE, GROUP_W), F32)],
        name="ssd_fwd", compiler_params=_params(1))(xact, xact, xact, proj, dtb, alog, dskip_x)


def _ssd_bwd(xact, proj, dtb, alog, dskip_x, dy, hst, dproj):
    t_rows = xact.shape[0]
    nt = t_rows // CHUNK
    xs_spec, b_spec, c_spec, dtr_spec, par, par_x, idx = _ssd_specs(nt, True)
    h_spec = pl.BlockSpec((1, SSM_GROUPS, D_STATE, GROUP_W), lambda c: (idx(c), 0, 0, 0))
    hn_spec = pl.BlockSpec((1, SSM_GROUPS, D_STATE, GROUP_W), lambda c: (jnp.minimum(idx(c) + 1, nt - 1), 0, 0, 0))
    bc_out = pl.BlockSpec((CHUNK, SSM_GROUPS * D_STATE), lambda c: (idx(c), 0))

    def body(xs_ref, b_ref, c_ref, dtr_ref, dtb_ref, alog_ref, dsk_ref, dy_ref, h_ref, hn_ref, _,
             dxs_ref, db_ref, dc_ref, ddt_ref, dalog_ref, ddtb_ref, dd_ref, dstate, dx_buf):
        step = pl.program_id(0)

        @pl.when(step == 0)
        def _():
            dstate[...] = jnp.zeros_like(dstate)
            dalog_ref[...] = jnp.zeros_like(dalog_ref)
            ddtb_ref[...] = jnp.zeros_like(ddtb_ref)
            dd_ref[...] = jnp.zeros_like(dd_ref)

        for g in range(SSM_GROUPS):
            _ssd_bwd_group(g, idx(step), xs_ref, b_ref, c_ref, dtr_ref, dtb_ref, alog_ref, dsk_ref, dy_ref, h_ref, hn_ref,
                           dxs_ref, db_ref, dc_ref, ddt_ref, dalog_ref, ddtb_ref, dd_ref, dstate, dx_buf)

    return pl.pallas_call(
        body, grid=(nt,),
        in_specs=[xs_spec, b_spec, c_spec, dtr_spec, par, par, par_x, xs_spec, h_spec, hn_spec, _ANY],
        out_specs=[xs_spec, bc_out, bc_out, dtr_spec, par, par, par_x],
        out_shape=[_sds((t_rows, D_INNER), F32), _sds((t_rows, SSM_GROUPS * D_STATE), F32),
                   _sds((t_rows, SSM_GROUPS * D_STATE), F32), _sds(dproj.shape, dproj.dtype),
                   _sds((SSM_GROUPS, 1, 128), F32), _sds((SSM_GROUPS, 1, 128), F32), _sds((SSM_GROUPS, 1, GROUP_W), F32)],
        scratch_shapes=[pltpu.VMEM((SSM_GROUPS, D_STATE, GROUP_W), F32), pltpu.VMEM((CHUNK, GROUP_W), F32)],
        input_output_aliases={10: 3},
        name="ssd_bwd", compiler_params=_params(1))(xact, xact, xact, proj, dtb, alog, dskip_x, dy, hst, hst, dproj)


def _ssd_bwd_group(g, chunk, xs_ref, b_ref, c_ref, dtr_ref, dtb_ref, alog_ref, dsk_ref, dy_ref, h_ref, hn_ref,
                   dxs_ref, db_ref, dc_ref, ddt_ref, dalog_ref, ddtb_ref, dd_ref, dstate, dx_buf):
    wide, narrow = _group_cols(g, GROUP_W), _group_cols(g, D_STATE)
    cm = _ssd_common(dtr_ref[:, narrow], dtb_ref[g], alog_ref[g], chunk)
    xs, bm, cmat = xs_ref[:, wide], b_ref[:, narrow], c_ref[:, narrow]
    dsk = dsk_ref[g]
    x_dt = xs * cm["dtx"]
    h_in, h_next = h_ref[0, g], hn_ref[0, g]
    dyv = dy_ref[:, wide]
    dh = dstate[g]
    grow, fade = cm["growx"], cm["fadex"]
    dy_grow = dyv * grow
    x_fade = x_dt * fade
    cb = _dot(cmat, bm, ((1,), (1,)))
    ml = jnp.zeros((CHUNK, CHUNK), F32)
    row_id = lax.broadcasted_iota(jnp.int32, (CHUNK, CHUNK), 0)
    col_id = lax.broadcasted_iota(jnp.int32, (CHUNK, CHUNK), 1)
    w_rows = jnp.zeros((CHUNK, CHUNK), F32)
    w_cols = jnp.zeros((CHUNK, CHUNK), F32)
    for j in range(HEADS_PER_GROUP):
        sl = slice(j * HEAD_P, (j + 1) * HEAD_P)
        lm = _decay_matrix(cm, j)
        mlj = _dot(dyv[:, sl], x_dt[:, sl], ((1,), (1,))) * lm
        ml = ml + mlj
        wm = mlj * cb
        w_rows = jnp.where(col_id == j, jnp.sum(wm, axis=1, keepdims=True), w_rows)
        w_cols = jnp.where(row_id == j, jnp.sum(wm, axis=0, keepdims=True), w_cols)
        dx_buf[:, sl] = _dot(cb * lm, dyv[:, sl], ((0,), (0,)))
    dx_off = fade * _dot_fine(bm, dh, ((1,), (0,)))
    dx = dx_buf[...] + dx_off
    dc_ref[:, narrow] = _dot(ml, bm, ((1,), (0,))) + _dot(dy_grow, h_in, ((1,), (1,)))
    db_ref[:, narrow] = _dot(ml, cmat, ((0,), (0,))) + _dot(x_fade, dh, ((1,), (1,)))
    fold = cm["fold"]
    y_off = _dot_fine(cmat, h_in, ((1,), (0,))) * grow
    dcs = (w_rows - w_cols.T) + _dot01(dyv * y_off - x_dt * dx_off, fold, True, 2)
    tail = jnp.broadcast_to(jnp.sum(dh * h_next, axis=0, keepdims=True), (8, GROUP_W))
    tail = _dot01(tail, fold, True, 2)[0:1, :]
    last_row = lax.broadcasted_iota(jnp.int32, (CHUNK, 128), 0) == CHUNK - 1
    dcs = dcs + jnp.where(last_row, tail, 0.0)
    da = _dot01(dcs, cm["triu"], False, 3)
    ddt = da * cm["a_head"] + _dot01(dx * xs, fold, True, 2)
    ddt_raw = jnp.where(cm["live"], ddt * _sigmoid(cm["raw"]), 0.0)
    ddt_ref[:, narrow] = ddt_raw.astype(BF16)
    ddtb_ref[g] += jnp.sum(ddt_raw, axis=0, keepdims=True)
    dalog_ref[g] += jnp.sum(da * cm["dt"], axis=0, keepdims=True) * cm["a_head"]
    dd_ref[g] += jnp.sum(dyv * xs, axis=0, keepdims=True)
    dxs_ref[:, wide] = dx * cm["dtx"] + dyv * dsk
    dstate[g] = dh * grow[CHUNK - 1:CHUNK, :] + _dot_fine(cmat, dy_grow, ((0,), (0,)))


def _swa_bias():
    rows_q = ATTN_GROUP * CHUNK
    dist = (jnp.arange(rows_q) % CHUNK)[:, None] - jnp.arange(2 * CHUNK)[None, :] + CHUNK
    head = jnp.arange(KV_HEADS)[:, None] * ATTN_GROUP + jnp.arange(rows_q)[None, :] // CHUNK + 1
    slope = jnp.exp2(-8.0 * head.astype(F32) / ATTN_HEADS)
    return jnp.where((dist >= 0) & (dist < CHUNK), -slope[:, :, None] * dist.astype(F32)[None], NEG)


def _swa_probs(q_kv, k_prev, k_cur, k_first, sink, bias, n):
    rows_q = ATTN_GROUP * CHUNK
    qs = jnp.concatenate([q_kv[:, g * DH:(g + 1) * DH] for g in range(ATTN_GROUP)], axis=0) * (DH ** -0.5)
    kcat = jnp.concatenate([k_prev, k_cur], axis=0)
    kmeta = k_first[META_PAD:, :]
    key_ok = lax.broadcasted_iota(jnp.int32, (1, 2 * CHUNK), 1) + n * CHUNK >= 2 * CHUNK
    s_band = jnp.where(key_ok, _dot(qs, kcat, ((1,), (1,))) + bias, NEG)
    q_pos = lax.broadcasted_iota(jnp.int32, (rows_q, N_META), 0) % CHUNK + n * CHUNK - META_PAD
    ok_m = lax.broadcasted_iota(jnp.int32, (rows_q, N_META), 1) <= q_pos
    s_meta = jnp.where(ok_m, _dot(qs, kmeta, ((1,), (1,))), NEG)
    m = jnp.maximum(jnp.maximum(jnp.max(s_band, axis=1, keepdims=True), jnp.max(s_meta, axis=1, keepdims=True)), sink)
    p_band, p_meta, p_sink = jnp.exp(s_band - m), jnp.exp(s_meta - m), jnp.exp(sink - m)
    inv = 1.0 / (jnp.sum(p_band, axis=1, keepdims=True) + jnp.sum(p_meta, axis=1, keepdims=True) + p_sink)
    return qs, kcat, kmeta, p_band * inv, p_meta * inv, p_sink * inv


def _swa_specs(nt, rev):
    def idx(n):
        return nt - 1 - n if rev else n
    o = pl.BlockSpec((CHUNK, ATTN_HEADS * DH), lambda n: (idx(n), 0))
    chunks = (lambda c: jnp.maximum(c - 1, 0)), (lambda c: c), (lambda c: 0)
    qkv = [pl.BlockSpec((CHUNK, QKV_W), lambda n, f=f: (f(idx(n)), OFF_Q // QKV_W)) for f in chunks]
    sink = _full((KV_HEADS, ATTN_GROUP * CHUNK, 1))
    bias = _full((KV_HEADS, ATTN_GROUP * CHUNK, 2 * CHUNK))
    return o, qkv, sink, bias, idx


def _head_cols(k):
    kv_w = ATTN_GROUP * DH
    q0, k0, v0 = k * kv_w, OFF_K - OFF_Q + k * DH, OFF_V - OFF_Q + k * DH
    return slice(q0, q0 + kv_w), slice(k0, k0 + DH), slice(v0, v0 + DH)


def _swa_fwd(proj, sink_rows, bias):
    t_rows = proj.shape[0]
    nt = t_rows // CHUNK
    o_spec, qkv_specs, sink_spec, bias_spec, _ = _swa_specs(nt, False)
    kv_w = ATTN_GROUP * DH

    def body(prev_ref, cur_ref, first_ref, sink_ref, bias_ref, o_ref):
        n = pl.program_id(0)
        for k in range(KV_HEADS):
            qc, kc, vc = _head_cols(k)
            _, _, _, p_band, p_meta, _ = _swa_probs(cur_ref[:, qc], prev_ref[:, kc], cur_ref[:, kc], first_ref[:, kc],
                                                    sink_ref[k], bias_ref[k], n)
            vcat = jnp.concatenate([prev_ref[:, vc], cur_ref[:, vc]], axis=0)
            out = _dot(p_band, vcat, ((1,), (0,))) + _dot(p_meta, first_ref[:, vc][META_PAD:, :], ((1,), (0,)))
            for g in range(ATTN_GROUP):
                o_ref[:, k * kv_w + g * DH:k * kv_w + (g + 1) * DH] = out[g * CHUNK:(g + 1) * CHUNK, :]

    return pl.pallas_call(
        body, grid=(nt,), in_specs=qkv_specs + [sink_spec, bias_spec],
        out_specs=o_spec, out_shape=_sds((t_rows, ATTN_HEADS * DH), F32),
        name="swa_fwd", compiler_params=_params(1))(proj, proj, proj, sink_rows, bias)


def _swa_bwd(proj, sink_rows, bias, out, dout, dproj):
    t_rows = proj.shape[0]
    nt = t_rows // CHUNK
    o_spec, qkv_specs, sink_spec, bias_spec, idx = _swa_specs(nt, True)
    kv_w = ATTN_GROUP * DH
    k_off, v_off = OFF_K - OFF_Q, OFF_V - OFF_Q

    def body(prev_ref, cur_ref, first_ref, sink_ref, bias_ref, o_ref, do_ref, _, dqkv_ref, dsink_ref,
             carry_k, carry_v, meta_k, meta_v, dqkv_buf):
        step = pl.program_id(0)
        n = idx(step)

        @pl.when(step == 0)
        def _():
            carry_k[...] = jnp.zeros_like(carry_k)
            carry_v[...] = jnp.zeros_like(carry_v)
            meta_k[...] = jnp.zeros_like(meta_k)
            meta_v[...] = jnp.zeros_like(meta_v)
            dsink_ref[...] = jnp.zeros_like(dsink_ref)

        for k in range(KV_HEADS):
            cols = slice(k * kv_w, (k + 1) * kv_w)
            hd = slice(k * DH, (k + 1) * DH)
            qc, kc, vc = _head_cols(k)
            qs, kcat, kmeta, p_band, p_meta, p_sink = _swa_probs(cur_ref[:, qc], prev_ref[:, kc], cur_ref[:, kc],
                                                                 first_ref[:, kc], sink_ref[k], bias_ref[k], n)
            vcat = jnp.concatenate([prev_ref[:, vc], cur_ref[:, vc]], axis=0)
            vmeta = first_ref[:, vc][META_PAD:, :]
            o, do = o_ref[:, cols], do_ref[:, cols]
            os_ = jnp.concatenate([o[:, g * DH:(g + 1) * DH] for g in range(ATTN_GROUP)], axis=0)
            dos = jnp.concatenate([do[:, g * DH:(g + 1) * DH] for g in range(ATTN_GROUP)], axis=0)
            delta = jnp.sum(dos * os_, axis=1, keepdims=True)
            ds_band = p_band * (_dot(dos, vcat, ((1,), (1,))) - delta)
            ds_meta = p_meta * (_dot(dos, vmeta, ((1,), (1,))) - delta)
            ds_sink = -p_sink * delta
            dqs = (_dot(ds_band, kcat, ((1,), (0,))) + _dot(ds_meta, kmeta, ((1,), (0,)))) * (DH ** -0.5)
            for g in range(ATTN_GROUP):
                dqkv_buf[:, k * kv_w + g * DH:k * kv_w + (g + 1) * DH] = dqs[g * CHUNK:(g + 1) * CHUNK, :]
                dsink_ref[k, g:g + 1, :] += jnp.sum(ds_sink[g * CHUNK:(g + 1) * CHUNK, :])
            dkcat = _dot(ds_band, qs, ((0,), (0,)))
            dvcat = _dot(p_band, dos, ((0,), (0,)))
            meta_k[:, hd] += _dot(ds_meta, qs, ((0,), (0,)))
            meta_v[:, hd] += _dot(p_meta, dos, ((0,), (0,)))
            dqkv_buf[:, kc] = dkcat[CHUNK:, :] + carry_k[:, hd]
            dqkv_buf[:, vc] = dvcat[CHUNK:, :] + carry_v[:, hd]
            carry_k[:, hd] = dkcat[:CHUNK, :]
            carry_v[:, hd] = dvcat[:CHUNK, :]

        @pl.when(n == 0)
        def _():
            dqkv_buf[META_PAD:, k_off:k_off + KV_W] += meta_k[...]
            dqkv_buf[META_PAD:, v_off:v_off + KV_W] += meta_v[...]

        dqkv_ref[...] = dqkv_buf[...].astype(BF16)

    return pl.pallas_call(
        body, grid=(nt,),
        in_specs=qkv_specs + [sink_spec, bias_spec, o_spec, o_spec, pl.BlockSpec(memory_space=pl.ANY)],
        out_specs=[qkv_specs[1], _full((KV_HEADS, 8, 128))],
        out_shape=[_sds(dproj.shape, dproj.dtype), _sds((KV_HEADS, 8, 128), F32)],
        scratch_shapes=[pltpu.VMEM((CHUNK, KV_W), F32), pltpu.VMEM((CHUNK, KV_W), F32),
                        pltpu.VMEM((N_META, KV_W), F32), pltpu.VMEM((N_META, KV_W), F32),
                        pltpu.VMEM((CHUNK, QKV_W), F32)],
        input_output_aliases={7: 0},
        name="swa_bwd", compiler_params=_params(1))(proj, proj, proj, sink_rows, bias, out, dout, dproj)


def _pack_w_in_t(w_in_t):
    w_dt = w_in_t[CUT_DT:CUT_Q].reshape(SSM_GROUPS, HEADS_PER_GROUP, D_MODEL)
    w_dt = jnp.pad(w_dt, ((0, 0), (0, 128 - HEADS_PER_GROUP), (0, 0))).reshape(SSM_GROUPS * 128, D_MODEL)
    return jnp.concatenate([w_in_t[CUT_Z:CUT_XBC], w_in_t[CUT_G:], w_dt, w_in_t[CUT_Q:CUT_G], w_in_t[CUT_XBC:CUT_DT]], axis=0)


def _unpack_w_in_t(wp_t):
    w_dt = wp_t[OFF_DT:OFF_Q].reshape(SSM_GROUPS, 128, D_MODEL)[:, :HEADS_PER_GROUP].reshape(SSM_HEADS, D_MODEL)
    return jnp.concatenate([wp_t[OFF_Z:OFF_GATE], wp_t[OFF_XBC:], w_dt, wp_t[OFF_Q:OFF_XBC], wp_t[OFF_GATE:OFF_DT]], axis=0)


def _group_rows(v, width):
    return jnp.pad(v.reshape(SSM_GROUPS, 1, HEADS_PER_GROUP), ((0, 0), (0, 0), (0, width - HEADS_PER_GROUP)))


def _local_step(x, target, wt, late_weights=None, on_grad=None, started=None):
    seq = x.shape[0]
    grads = {}

    def emit(name, g):
        grads[name] = g
        return None if on_grad is None else on_grad(name, g)
    meta = wt["meta_tokens"]
    wp_t = _pack_w_in_t(wt["w_in_t"])
    dtb = _group_rows(wt["ssm_dt_bias"].reshape(-1), 128)
    alog = _group_rows(wt["ssm_a_log"].reshape(-1), 128)
    dskip_x = jnp.repeat(wt["ssm_d_skip"].reshape(-1), HEAD_P).reshape(SSM_GROUPS, 1, GROUP_W)
    sink_rows = jnp.repeat(wt["attn_sinks"].reshape(KV_HEADS, ATTN_GROUP), CHUNK, axis=1).reshape(KV_HEADS, ATTN_GROUP * CHUNK, 1)

    hn = _prenorm(x, meta, wt["norm_pre_mix"])
    proj = _matmul(hn, wp_t, tb=True, name="in_proj", after=started)
    xc, xact = _ssm_conv_fwd(proj, wt["ssm_conv_w"], wt["ssm_conv_b"])
    y, hst = _ssd_fwd(xact, proj, dtb, alog, dskip_x)
    yn = _ssm_post(y, proj, wt["ssm_norm"])
    if late_weights is not None:
        wt = {**wt, **late_weights(yn)}
    y_ssm = _matmul(yn, wt["w_ssm_out"], name="ssm_out")
    bias = _swa_bias()
    attn = _swa_fwd(proj, sink_rows, bias)
    y_attn = _matmul(attn, wt["w_attn_out"], name="attn_out")
    mixed = _mix_fwd(proj, y_ssm, y_attn)
    mix = _matmul(mixed, wt["w_mix_out"], name="mix_out")
    h1, hn2 = _postmix(x, meta, mix, wt["norm_post_mix"], wt["norm_pre_ffn"])
    up = _matmul(hn2, wt["w_ffn_up_t"], tb=True, out_dtype=BF16, name="ffn_up")
    u, act = _ffn_act(up, wt["ffn_conv_w"], wt["ffn_conv_b"])
    f = _matmul(act, wt["w_ffn_down"], name="ffn_down")
    df, dy, g_norm_post_ffn, loss_row = _final(h1, f, target, wt["norm_post_ffn"])

    grads["norm_post_ffn"] = g_norm_post_ffn
    sent = emit("w_ffn_down", _matmul(act, df, ta=True, out_dtype=BF16, name="dw_ffn_down"))
    dact = _matmul(df, wt["w_ffn_down"], tb=True, out_dtype=BF16, name="d_act", after=sent)
    dup, grads["ffn_conv_w"], grads["ffn_conv_b"] = _ffn_act_bwd(u, up, dact, wt["ffn_conv_w"])
    sent = emit("w_ffn_up_t", _matmul(dup, hn2, ta=True, out_dtype=BF16, name="dw_ffn_up"))
    dhn2 = _matmul(dup, wt["w_ffn_up_t"], name="d_hn2", after=sent)
    dmix, dh, grads["norm_pre_ffn"], grads["norm_post_mix"] = _postmix_bwd(h1, dhn2, dy, mix, wt["norm_pre_ffn"], wt["norm_post_mix"])
    sent = emit("w_mix_out", _matmul(mixed, dmix, ta=True, out_dtype=BF16, name="dw_mix_out"))
    dmixed = _matmul(dmix, wt["w_mix_out"], tb=True, name="d_mixed", after=sent)
    dy_ssm, dy_attn, dproj = _mix_bwd(dmixed, proj, y_ssm, y_attn, lax.empty(proj.shape, BF16))
    sent = emit("w_ssm_out", _matmul(yn, dy_ssm, ta=True, out_dtype=BF16, name="dw_ssm_out"))
    dyn = _matmul(dy_ssm, wt["w_ssm_out"], tb=True, out_dtype=BF16, name="d_yn", after=sent)
    sent = emit("w_attn_out", _matmul(attn, dy_attn, ta=True, out_dtype=BF16, name="dw_attn_out"))
    dattn = _matmul(dy_attn, wt["w_attn_out"], tb=True, name="d_attn", after=sent)
    dy_ssd, dproj, grads["ssm_norm"] = _ssm_post_bwd(y, proj, dyn, wt["ssm_norm"], dproj)
    dxs, dbm, dcm, dproj, dalog, ddtb, dd_x = _ssd_bwd(xact, proj, dtb, alog, dskip_x, dy_ssd, hst, dproj)
    grads["ssm_a_log"] = dalog[:, 0, :HEADS_PER_GROUP].reshape(1, SSM_HEADS)
    grads["ssm_dt_bias"] = ddtb[:, 0, :HEADS_PER_GROUP].reshape(1, SSM_HEADS)
    grads["ssm_d_skip"] = dd_x.reshape(SSM_HEADS, HEAD_P).sum(axis=1).reshape(1, SSM_HEADS)
    dproj, grads["ssm_conv_w"], grads["ssm_conv_b"] = _ssm_conv_bwd(xc, proj, dxs, dbm, dcm, wt["ssm_conv_w"], dproj)
    dproj, dsink = _swa_bwd(proj, sink_rows, bias, attn, dattn, dproj)
    grads["attn_sinks"] = dsink[:, :ATTN_GROUP, 0].reshape(1, ATTN_HEADS)
    sent = emit("w_in_t", _unpack_w_in_t(_matmul(dproj, hn, ta=True, out_dtype=BF16, name="dw_in")))
    dhn = _matmul(dproj, wp_t, name="d_hn", after=sent)
    grad_x, grads["meta_tokens"], grads["norm_pre_mix"] = _prenorm_bwd(x, meta, dhn, dh, wt["norm_pre_mix"])
    return loss_row[0, 0], grad_x, grads


def _all_gather(shards):
    n = len(shards)

    def body(*refs):
        ins, outs = refs[:n], refs[n:2 * n]
        send_sems, recv_sems, local_sems = refs[2 * n:]
        x, y, c = lax.axis_index("x"), lax.axis_index("y"), lax.axis_index("c")
        me, sibling = (x, y, c), (x, y, 1 - c)
        chips = [(1 - x, y), (x, 1 - y), (1 - x, 1 - y)]

        def slot(a, dev):
            return outs[a].at[4 * dev[0] + 2 * dev[1] + dev[2]]

        def copy(k, a, block, to, src=None):
            return pltpu.make_async_remote_copy(
                src_ref=slot(a, block) if src is None else src, dst_ref=slot(a, block),
                send_sem=send_sems.at[k, a], recv_sem=recv_sems.at[k, a],
                device_id=to, device_id_type=pl.DeviceIdType.MESH)

        mine = [pltpu.make_async_copy(ins[a], slot(a, me), local_sems.at[a]) for a in range(n)]
        for cp in mine:
            cp.start()
        first = [copy(0, a, me, sibling, src=ins[a]) for a in range(n)]
        for j, chip in enumerate(chips):
            first += [copy(1 + j, a, me, (*chip, c), src=ins[a]) for a in range(n)]
        for cp in first:
            cp.start()
        passed = []
        for j, chip in enumerate(chips):
            for a in range(n):
                copy(1 + j, a, (*chip, c), me).wait_recv()
                fwd = copy(4 + j, a, (*chip, c), sibling)
                fwd.start()
                passed.append(fwd)
        for a in range(n):
            copy(0, a, sibling, me).wait_recv()
        for j, chip in enumerate(chips):
            for a in range(n):
                copy(4 + j, a, (*chip, 1 - c), me).wait_recv()
        for cp in first + passed:
            cp.wait_send()
        for cp in mine:
            cp.wait()

    hbm = pl.BlockSpec(memory_space=pl.ANY)
    return pl.pallas_call(
        body, in_specs=[hbm] * n, out_specs=[hbm] * n,
        out_shape=[_sds((N_DEV,) + s.shape, s.dtype) for s in shards],
        scratch_shapes=[pltpu.SemaphoreType.DMA((7, n)), pltpu.SemaphoreType.DMA((7, n)), pltpu.SemaphoreType.DMA((n,))],
        name="gather_weights")(*shards)


def _peer_table():
    x, y, c = lax.axis_index("x"), lax.axis_index("y"), lax.axis_index("c")
    peers = []
    for k in range(N_DEV - 1):
        bits = k + 1
        p = (x ^ ((bits >> 2) & 1), y ^ ((bits >> 1) & 1), c ^ (bits & 1))
        peers.append((k, p, 4 * p[0] + 2 * p[1] + p[2]))
    return 4 * x + 2 * y + c, peers


_HBM = pl.BlockSpec(memory_space=pltpu.HBM)
_SEM = pl.BlockSpec(memory_space=pltpu.SEMAPHORE)
_EFFECT = pltpu.SideEffectType.DATAFLOW_SIDE_EFFECTING


def _push_copy(src, land, send_sems, recv_sems, a, k, p, src_slot, dst_slot):
    sem = a * (N_DEV - 1) + k
    return pltpu.make_async_remote_copy(
        src_ref=src[a] if src_slot is None else src[a].at[src_slot], dst_ref=land[a].at[dst_slot],
        send_sem=send_sems.at[sem], recv_sem=recv_sems.at[sem], device_id=p, device_id_type=pl.DeviceIdType.MESH)


def _push_start(srcs, scatter, name):
    n = len(srcs)
    lands = [lax.empty(s.shape if scatter else (N_DEV,) + s.shape, s.dtype) for s in srcs]

    def body(*refs):
        src, land = refs[:n], refs[n:2 * n]
        send_sems, recv_sems, token = refs[2 * n], refs[2 * n + 1], refs[-1]
        my_id, peers = _peer_table()
        for a in range(n):
            for k, p, p_id in peers:
                _push_copy(src, land, send_sems, recv_sems, a, k, p, p_id if scatter else None, my_id).start()
        token[...] = jnp.zeros_like(token)

    sems = pltpu.SemaphoreType.DMA(((N_DEV - 1) * n,))
    res = pl.pallas_call(
        body, name=name,
        out_shape=(sems, sems, *[pltpu.HBM(a.shape, a.dtype) for a in srcs + lands], _sds((8, 128), F32)),
        in_specs=[_HBM] * (2 * n), out_specs=(_SEM, _SEM, *[_HBM] * (2 * n), pl.BlockSpec(memory_space=pltpu.VMEM)),
        input_output_aliases={i: 2 + i for i in range(2 * n)},
        compiler_params=pltpu.CompilerParams(has_side_effects=_EFFECT),
    )(*[pltpu.with_memory_space_constraint(a, pltpu.HBM) for a in srcs + lands])
    return dict(send=res[0], recv=res[1], src=list(res[2:2 + n]), land=list(res[2 + n:2 + 2 * n]), token=res[-1],
                scatter=scatter)


def _push_wait(handle, after, name):
    n = len(handle["src"])
    scatter = handle["scatter"]

    def body(*refs):
        src, land = refs[:n], refs[n:2 * n]
        send_sems, recv_sems = refs[2 * n], refs[2 * n + 1]
        _, peers = _peer_table()
        for a in range(n):
            for k, p, p_id in peers:
                cp = _push_copy(src, land, send_sems, recv_sems, a, k, p, p_id if scatter else None, p_id)
                cp.wait_send()
                cp.wait_recv()

    arrays = handle["src"] + handle["land"]
    res = pl.pallas_call(
        body, name=name, out_shape=tuple(pltpu.HBM(a.shape, a.dtype) for a in arrays),
        in_specs=[_HBM] * (2 * n) + [_SEM, _SEM, pl.BlockSpec(memory_space=pl.ANY)], out_specs=tuple([_HBM] * (2 * n)),
        input_output_aliases={i: i for i in range(2 * n)},
        compiler_params=pltpu.CompilerParams(has_side_effects=_EFFECT),
    )(*arrays, handle["send"], handle["recv"], after)
    return list(res[:n]), list(res[n:])


def _slot_sum(p_ref, own_ref):
    if own_ref is not None:
        my_id = 4 * lax.axis_index("x") + 2 * lax.axis_index("y") + lax.axis_index("c")
        mine = own_ref[...].astype(F32)
    g = None
    for s in range(p_ref.shape[0]):
        term = p_ref[s].astype(F32)
        if own_ref is not None:
            term = jnp.where(my_id == s, mine, term)
        g = term if g is None else g + term
    return g


def _to_bf16(arrays):
    n = len(arrays)

    def body(*refs):
        for i in range(n):
            refs[n + i][...] = refs[i][...].astype(BF16)

    return pl.pallas_call(body, out_shape=[_sds(a.shape, BF16) for a in arrays], name="weights_to_bf16",
                          compiler_params=pltpu.CompilerParams(vmem_limit_bytes=VMEM_LIMIT))(*arrays)


def _adamw(parts, own, w, m, v, name):
    unit_rows = w.ndim == 3
    rows, cols = w.shape[0], w.shape[-1]
    if rows % 16 == 0:
        tr, tc = _pick(rows, (256, 128, 176, 64, 32, 16)), cols
    else:
        tr, tc = rows, _pick(cols, (256, 128))

    def body(*refs):
        if own is None:
            p_ref, w_ref, m_ref, v_ref, g_ref, d_ref, nm_ref, nv_ref = refs
            own_ref = None
        else:
            p_ref, own_ref, w_ref, m_ref, v_ref, g_ref, d_ref, nm_ref, nv_ref = refs
        g = _slot_sum(p_ref, own_ref)
        if unit_rows:
            g = g.reshape(tr, 1, tc)
        m_new = ADAM_B1 * m_ref[...] + (1.0 - ADAM_B1) * g
        v_new = ADAM_B2 * v_ref[...] + (1.0 - ADAM_B2) * (g * g)
        m_hat = m_new / (1.0 - ADAM_B1 ** ADAM_STEP)
        v_hat = v_new / (1.0 - ADAM_B2 ** ADAM_STEP)
        g_ref[...] = g
        d_ref[...] = -ADAM_LR * (m_hat / (jnp.sqrt(v_hat) + ADAM_EPS) + ADAM_WD * w_ref[...])
        nm_ref[...] = m_new
        nv_ref[...] = v_new

    by_rows = tc == cols
    spec = pl.BlockSpec((tr, tc), (lambda i: (i, 0)) if by_rows else (lambda i: (0, i)))
    state_spec = spec if not unit_rows else pl.BlockSpec((tr, 1, tc), (lambda i: (i, 0, 0)) if by_rows else (lambda i: (0, 0, i)))
    parts_spec = pl.BlockSpec((parts.shape[0], tr, tc), (lambda i: (0, i, 0)) if by_rows else (lambda i: (0, 0, i)))
    operands = (parts, w, m, v) if own is None else (parts, own, w, m, v)
    return pl.pallas_call(
        body, grid=(rows // tr if by_rows else cols // tc,),
        in_specs=[parts_spec] + ([] if own is None else [spec]) + [state_spec] * 3,
        out_specs=[state_spec] * 4, out_shape=[_sds(w.shape, F32)] * 4,
        name=name, compiler_params=_params(1))(*operands)


SMALL_REPLICATED = (("norm_pre_mix", 1024), ("ssm_conv_b", 3072), ("ssm_dt_bias", 32), ("ssm_a_log", 32),
                    ("ssm_d_skip", 32), ("ssm_norm", 2048), ("attn_sinks", 16), ("norm_post_mix", 1024),
                    ("norm_pre_ffn", 1024), ("ffn_conv_b", 5632), ("norm_post_ffn", 1024))
SMALL_SHARDED = (("meta_tokens", (N_META, D_MODEL // N_DEV)), ("ssm_conv_w", (SSM_CONV, CONV_DIM // N_DEV)),
                 ("ffn_conv_w", (FFN_CONV, 2 * FFN_DIM // N_DEV)))
BIG = (("w_in", (D_MODEL, N_IN // N_DEV), 1), ("w_ssm_out", (D_INNER // N_DEV, D_MODEL), 0),
       ("w_attn_out", (D_MODEL // N_DEV, D_MODEL), 0), ("w_mix_out", (D_MODEL // N_DEV, D_MODEL), 0),
       ("w_ffn_up", (D_MODEL, 2 * FFN_DIM // N_DEV), 1), ("w_ffn_down", (FFN_DIM // N_DEV, D_MODEL), 0))


def _rows_of(size):
    return -(-size // 128)


def _as_rows(flat):
    size = flat.shape[-1]
    rows = _rows_of(size)
    flat = jnp.pad(flat, [(0, 0)] * (flat.ndim - 1) + [(0, rows * 128 - size)])
    return flat.reshape(flat.shape[:-1] + (rows, 128))


def _pack_small(rep, sharded):
    pieces = [_as_rows(rep[name].reshape(-1)) for name, _ in SMALL_REPLICATED]
    pieces += [_as_rows(sharded[name].reshape(-1)) for name, _ in SMALL_SHARDED]
    packed = jnp.concatenate(pieces, axis=0)
    return jnp.pad(packed, ((0, -packed.shape[0] % 8), (0, 0)))


def _unpack_small(packed):
    out, row = {}, 0
    for name, size in SMALL_REPLICATED:
        out[name] = packed[row:row + _rows_of(size)].reshape(-1)[:size].reshape(1, size)
        row += _rows_of(size)
    for name, (r, c) in SMALL_SHARDED:
        out[name] = packed[row:row + _rows_of(r * c)].reshape(-1)[:r * c].reshape(r, c)
        row += _rows_of(r * c)
    return out


def _shard_major(g, shape, axis):
    r, c = shape
    if axis == 0:
        return g.reshape(N_DEV, r, c)
    return g.reshape(r, N_DEV, c).transpose(1, 0, 2)


def kernel(x, meta_tokens, norm_pre_mix, w_in, ssm_conv_w, ssm_conv_b, ssm_dt_bias, ssm_a_log, ssm_d_skip, ssm_norm, w_ssm_out, attn_sinks, w_attn_out, w_mix_out, norm_post_mix, norm_pre_ffn, w_ffn_up, ffn_conv_w, ffn_conv_b, w_ffn_down, norm_post_ffn, loss_target, m_meta_tokens, m_norm_pre_mix, m_w_in, m_ssm_conv_w, m_ssm_conv_b, m_ssm_dt_bias, m_ssm_a_log, m_ssm_d_skip, m_ssm_norm, m_w_ssm_out, m_attn_sinks, m_w_attn_out, m_w_mix_out, m_norm_post_mix, m_norm_pre_ffn, m_w_ffn_up, m_ffn_conv_w, m_ffn_conv_b, m_w_ffn_down, m_norm_post_ffn, v_meta_tokens, v_norm_pre_mix, v_w_in, v_ssm_conv_w, v_ssm_conv_b, v_ssm_dt_bias, v_ssm_a_log, v_ssm_d_skip, v_ssm_norm, v_w_ssm_out, v_attn_sinks, v_w_attn_out, v_w_mix_out, v_norm_post_mix, v_norm_pre_ffn, v_w_ffn_up, v_ffn_conv_w, v_ffn_conv_b, v_w_ffn_down, v_norm_post_ffn):
    names = ("meta_tokens", "norm_pre_mix", "w_in", "ssm_conv_w", "ssm_conv_b", "ssm_dt_bias", "ssm_a_log", "ssm_d_skip",
             "ssm_norm", "w_ssm_out", "attn_sinks", "w_attn_out", "w_mix_out", "norm_post_mix", "norm_pre_ffn", "w_ffn_up",
             "ffn_conv_w", "ffn_conv_b", "w_ffn_down", "norm_post_ffn")
    w_loc = dict(zip(names, (meta_tokens, norm_pre_mix, w_in, ssm_conv_w, ssm_conv_b, ssm_dt_bias, ssm_a_log, ssm_d_skip,
                             ssm_norm, w_ssm_out, attn_sinks, w_attn_out, w_mix_out, norm_post_mix, norm_pre_ffn, w_ffn_up,
                             ffn_conv_w, ffn_conv_b, w_ffn_down, norm_post_ffn)))
    m_loc = dict(zip(names, (m_meta_tokens, m_norm_pre_mix, m_w_in, m_ssm_conv_w, m_ssm_conv_b, m_ssm_dt_bias, m_ssm_a_log,
                             m_ssm_d_skip, m_ssm_norm, m_w_ssm_out, m_attn_sinks, m_w_attn_out, m_w_mix_out, m_norm_post_mix,
                             m_norm_pre_ffn, m_w_ffn_up, m_ffn_conv_w, m_ffn_conv_b, m_w_ffn_down, m_norm_post_ffn)))
    v_loc = dict(zip(names, (v_meta_tokens, v_norm_pre_mix, v_w_in, v_ssm_conv_w, v_ssm_conv_b, v_ssm_dt_bias, v_ssm_a_log,
                             v_ssm_d_skip, v_ssm_norm, v_w_ssm_out, v_attn_sinks, v_w_attn_out, v_w_mix_out, v_norm_post_mix,
                             v_norm_pre_ffn, v_w_ffn_up, v_ffn_conv_w, v_ffn_conv_b, v_w_ffn_down, v_norm_post_ffn)))

    def local2d(d, name):
        a = d[name]
        return a if name == "meta_tokens" else a.reshape(a.shape[1:])

    def turned2d(d, name):
        a = jnp.swapaxes(d[name], 1, 2)
        return a.reshape(a.shape[1:])

    my_id = 4 * lax.axis_index("x") + 2 * lax.axis_index("y") + lax.axis_index("c")
    big = {name: (shape, axis) for name, shape, axis in BIG}

    def whole(name, g):
        return g.reshape(N_DEV * g.shape[1], g.shape[2])

    def key(name):
        return name + "_t" if big[name][1] == 1 else name

    by_rows = [name for name, _, axis in BIG if axis == 0]
    send_bf16 = dict(zip(by_rows, _to_bf16([local2d(w_loc, name) for name in by_rows])))
    for name, _, axis in BIG:
        if axis == 1:
            send_bf16[name] = turned2d(w_loc, name).astype(BF16)
    small_shard_pack = jnp.concatenate([_as_rows(local2d(w_loc, name).reshape(-1)) for name, _ in SMALL_SHARDED], axis=0)
    small_shard_pack = jnp.pad(small_shard_pack, ((0, -small_shard_pack.shape[0] % 8), (0, 0)))
    first = _all_gather([send_bf16["w_in"], small_shard_pack])
    rest_names = [name for name, _, _ in BIG if name != "w_in"]
    rest = [send_bf16[name] for name in rest_names]
    rest, first = lax.optimization_barrier((rest, first))
    rest_handle = _push_start(rest, False, "gather_rest_start")
    wt = {"w_in_t": whole("w_in", first[0])}
    row = 0
    for name, (r, c) in SMALL_SHARDED:
        blocks = first[1][:, row:row + _rows_of(r * c)].reshape(N_DEV, -1)[:, :r * c].reshape(N_DEV, r, c)
        wt[name] = blocks.transpose(1, 0, 2).reshape(r, N_DEV * c)
        row += _rows_of(r * c)
    for name, size in SMALL_REPLICATED:
        wt[name] = w_loc[name].reshape(1, size)

    def late_weights(after):
        own, landed = _push_wait(rest_handle, after, "gather_rest_wait")
        out = {}
        for name, mine, land in zip(rest_names, own, landed):
            out[key(name)] = whole(name, lax.dynamic_update_index_in_dim(land, mine, my_id, 0))
        return out

    sent = {}

    def on_grad(known_as, g):
        name = known_as.removesuffix("_t")
        by_owner = g.reshape(N_DEV, g.shape[0] // N_DEV, g.shape[1])
        sent[name] = _push_start([by_owner], True, "send_" + name)
        return sent[name]["token"]

    loss_part, grad_x, grads = _local_step(x[0], loss_target[0], wt, late_weights, on_grad, rest_handle["token"])
    loss = lax.psum(loss_part, AXES)

    small_parts = []
    for name, (r, c) in SMALL_SHARDED:
        small_parts.append(_as_rows(_shard_major(grads[name], (r, c), 1).reshape(N_DEV, r * c)))
    rep_rows = jnp.concatenate([_as_rows(grads[name].reshape(-1)) for name, _ in SMALL_REPLICATED], axis=0)
    small_send = jnp.concatenate([jnp.broadcast_to(rep_rows[None], (N_DEV,) + rep_rows.shape)] + small_parts, axis=1)
    small_send = jnp.pad(small_send, ((0, 0), (0, -small_send.shape[1] % 8), (0, 0)))
    small_handle = _push_start([small_send], True, "send_small")

    def small_pack(d):
        return _pack_small({name: d[name] for name, _ in SMALL_REPLICATED}, {name: local2d(d, name) for name, _ in SMALL_SHARDED})

    def arrived(handle, after, name):
        src, landed = _push_wait(handle, after, "arrived_" + name)
        return landed[0], lax.dynamic_index_in_dim(src[0], my_id, 0, keepdims=False)

    grad_w, delta_w, new_m, new_v = {}, {}, {}, {}
    outs = None
    after = small_handle["token"]
    for name, handle in sent.items():
        if name == "w_in":
            parts, own = arrived(small_handle, after, "small")
            outs = _adamw(parts, own, small_pack(w_loc), small_pack(m_loc), small_pack(v_loc), "adamw_small")
            after = outs[0]
        parts, own = arrived(handle, after, name)
        turned = big[name][1] == 1
        unit_rows = turned and big[name][0][1] % 8 != 0
        if unit_rows:
            state = [jnp.transpose(d[name], (2, 0, 1)) for d in (w_loc, m_loc, v_loc)]
        else:
            state = [turned2d(d, name) if turned else local2d(d, name) for d in (w_loc, m_loc, v_loc)]
        results = _adamw(parts, own, *state, "adamw_" + name)
        after = results[0]
        full = (1,) + big[name][0]
        for dst, a in zip((grad_w, delta_w, new_m, new_v), results):
            if unit_rows:
                dst[name] = jnp.transpose(a, (1, 2, 0))
            else:
                dst[name] = jnp.swapaxes(a[None], 1, 2) if turned else a.reshape(full)
    for dst, packed in zip((grad_w, delta_w, new_m, new_v), outs):
        for name, a in _unpack_small(packed).items():
            dst[name] = a.reshape(w_loc[name].shape)

    return (loss, grad_x[None], *[grad_w[n] for n in names], *[delta_w[n] for n in names],
            *[new_m[n] for n in names], *[new_v[n] for n in names])
```

```python
import jax
import jax.numpy as jnp
from jax import lax
from jax.experimental import pallas as pl
from jax.experimental.pallas import tpu as pltpu

F32 = jnp.float32
BF16 = jnp.bfloat16

D_MODEL = 1024
N_META = 16
CHUNK = 128
META_PAD = CHUNK - N_META
D_INNER = 2048
HEAD_P = 64
SSM_HEADS = 32
SSM_GROUPS = 4
HEADS_PER_GROUP = SSM_HEADS // SSM_GROUPS
GROUP_W = HEADS_PER_GROUP * HEAD_P
D_STATE = 128
SSM_CONV = 4
CONV_DIM = D_INNER + 2 * SSM_GROUPS * D_STATE
ATTN_HEADS = 16
KV_HEADS = 4
ATTN_GROUP = ATTN_HEADS // KV_HEADS
DH = 64
KV_W = KV_HEADS * DH
FFN_DIM = 2816
FFN_CONV = 3
EPS = 1e-6
NEG = -1e30
N_DEV = 8
AXES = ("x", "y", "c")

OFF_Z, OFF_GATE, OFF_DT, OFF_Q, OFF_K, OFF_V, OFF_XBC = 0, 2048, 4096, 4608, 5632, 5888, 6144
N_INP = OFF_XBC + CONV_DIM
QKV_W = OFF_XBC - OFF_Q
CUT_Z, CUT_XBC, CUT_DT, CUT_Q, CUT_K, CUT_V, CUT_G = 0, 2048, 5120, 5152, 6176, 6432, 6688
N_IN = 8736

ADAM_LR, ADAM_B1, ADAM_B2, ADAM_EPS, ADAM_WD, ADAM_STEP = 0.001, 0.9, 0.999, 1e-08, 0.01, 10

VMEM_LIMIT = 56 * 1024 * 1024


def _params(n_grid):
    return pltpu.CompilerParams(dimension_semantics=("arbitrary",) * n_grid, vmem_limit_bytes=VMEM_LIMIT)


def _sds(shape, dtype):
    return jax.ShapeDtypeStruct(shape, dtype)


def _pick(n, prefs):
    for c in prefs:
        if n % c == 0:
            return c
    raise ValueError(f"no tile of {prefs} divides {n}")


def _row(tr, width, cb=0):
    return pl.BlockSpec((tr, width), lambda i: (i, cb))


def _row_rev(tr, width, nt, cb=0):
    return pl.BlockSpec((tr, width), lambda i: (nt - 1 - i, cb))


def _full(shape):
    return pl.BlockSpec(shape, lambda *_: (0,) * len(shape))


def _sigmoid(x):
    return 1.0 / (1.0 + jnp.exp(-x))


def _softplus(x):
    return jnp.maximum(x, 0.0) + jnp.log(1.0 + jnp.exp(-jnp.abs(x)))


def _rms(x):
    return lax.rsqrt(jnp.mean(x * x, axis=-1, keepdims=True) + EPS)


def _rms_bwd(x, r, w, dy):
    xh = x * r
    g = dy * w
    dx = r * (g - xh * jnp.mean(g * xh, axis=-1, keepdims=True))
    return dx, jnp.sum(dy * xh, axis=0, keepdims=True)


def _row_ids(shape, tile_index, tr):
    return tile_index * tr + lax.broadcasted_iota(jnp.int32, shape, 0)


HALO = 8
STRIP = 256
STRIP_BWD = 128


def _causal_taps(x, halo, first_step, taps):
    n = x.shape[0]

    @pl.when(first_step)
    def _():
        halo[...] = jnp.zeros_like(halo)

    before = halo[...]
    row = lax.broadcasted_iota(jnp.int32, before.shape, 0)
    shifted = [x]
    for s in range(1, taps):
        rolled = pltpu.roll(x, s, 0)
        head = jnp.where(row < s, pltpu.roll(before, s, 0), rolled[0:HALO, :])
        shifted.append(jnp.concatenate([head, rolled[HALO:, :]], axis=0))
    halo[...] = x[n - HALO:, :]
    return shifted


def _anticausal_taps(x, halo, first_step, taps):
    n = x.shape[0]

    @pl.when(first_step)
    def _():
        halo[...] = jnp.zeros_like(halo)

    after = halo[...]
    row = lax.broadcasted_iota(jnp.int32, after.shape, 0)
    shifted = [x]
    for s in range(1, taps):
        rolled = pltpu.roll(x, n - s, 0)
        tail = jnp.where(row >= HALO - s, pltpu.roll(after, HALO - s, 0), rolled[n - HALO:, :])
        shifted.append(jnp.concatenate([rolled[:n - HALO, :], tail], axis=0))
    halo[...] = x[0:HALO, :]
    return shifted


def _matmul(a, b, *, ta=False, tb=False, out_dtype=F32, name, after=None):
    if ta:
        k_dim, m_dim = a.shape
    else:
        m_dim, k_dim = a.shape
    n_dim = b.shape[0] if tb else b.shape[1]
    tm = _pick(m_dim, (1408, 1024, 768, 512, 384, 256, 128))
    tn = _pick(n_dim, (1024, 1408, 768, 512, 384, 256, 128))
    if ta:
        tk = _pick(k_dim, (1408, 1024, 768, 512, 384, 256, 128))
    else:
        tk = k_dim if k_dim <= 3072 else _pick(k_dim, (3072, 2816, 2048, 1024))
    nk = k_dim // tk
    dims = (((0 if ta else 1,), (1 if tb else 0,)), ((), ()))

    use_acc = nk > 1 and out_dtype != F32

    def body(a_ref, b_ref, *rest):
        o_ref = rest[-2] if use_acc else rest[-1]
        acc_ref = rest[-1] if use_acc else o_ref
        r = lax.dot_general(a_ref[...].astype(BF16), b_ref[...].astype(BF16), dims, preferred_element_type=F32)
        if nk == 1:
            o_ref[...] = r.astype(o_ref.dtype)
        else:
            k = pl.program_id(2)

            @pl.when(k == 0)
            def _():
                acc_ref[...] = r

            @pl.when(k > 0)
            def _():
                acc_ref[...] += r

            if use_acc:
                @pl.when(k == nk - 1)
                def _():
                    o_ref[...] = acc_ref[...].astype(o_ref.dtype)

    a_spec = pl.BlockSpec((tk, tm), lambda i, j, k: (k, i)) if ta else pl.BlockSpec((tm, tk), lambda i, j, k: (i, k))
    b_spec = pl.BlockSpec((tn, tk), lambda i, j, k: (j, k)) if tb else pl.BlockSpec((tk, tn), lambda i, j, k: (k, j))
    extra_specs, extra = ([], ()) if after is None else ([pl.BlockSpec(memory_space=pl.ANY)], (after,))
    return pl.pallas_call(
        body, grid=(m_dim // tm, n_dim // tn, nk), in_specs=[a_spec, b_spec] + extra_specs,
        out_specs=pl.BlockSpec((tm, tn), lambda i, j, k: (i, j)), out_shape=_sds((m_dim, n_dim), out_dtype),
        scratch_shapes=[pltpu.VMEM((tm, tn), F32)] if use_acc else [],
        name=name, compiler_params=_params(3))(a, b, *extra)


def _seq_rows(t_rows):
    return 384 if t_rows % 384 == 0 and t_rows >= 768 else CHUNK


def _token_rows(tr):
    if tr == CHUNK:
        return pl.BlockSpec((CHUNK, D_MODEL), lambda i: (jnp.maximum(i - 1, 0), 0))
    return pl.BlockSpec((pl.Element(tr), pl.Element(D_MODEL)),
                        lambda i: (pl.multiple_of(jnp.maximum(i * tr - CHUNK, 0), CHUNK), 0))


def _under_tile(rows_ref, head, i):
    rows = rows_ref[...]
    tr = rows.shape[0]
    first = head if tr == CHUNK else jnp.concatenate([head, rows[0:tr - CHUNK, :]], axis=0)
    return jnp.where(i == 0, first, rows)


def _seq_specs(tr=CHUNK):
    return [_token_rows(tr), _full((N_META, D_MODEL))]


def _seq_tile(x_ref, meta_ref, i):
    return _under_tile(x_ref, jnp.concatenate([jnp.zeros((META_PAD, D_MODEL), F32), meta_ref[...]], axis=0), i)


def _prenorm(x, meta, w):
    t_rows = x.shape[0] + CHUNK
    tr = _seq_rows(t_rows)

    def body(x_ref, meta_ref, w_ref, o_ref):
        h = _seq_tile(x_ref, meta_ref, pl.program_id(0))
        o_ref[...] = (h * _rms(h) * w_ref[...]).astype(BF16)

    return pl.pallas_call(body, grid=(t_rows // tr,), in_specs=_seq_specs(tr) + [_full((1, D_MODEL))],
                          out_specs=_row(tr, D_MODEL), out_shape=_sds((t_rows, D_MODEL), BF16),
                          name="prenorm", compiler_params=_params(1))(x, meta, w)


def _ssm_conv_fwd(proj, conv_w, conv_b):
    t_rows = proj.shape[0]
    tr = CHUNK

    def body(x_ref, w_ref, b_ref, xc_ref, xa_ref, hist):
        first = pl.program_id(0) == 0
        for c in range(0, CONV_DIM, STRIP):
            cols = slice(c, c + STRIP)
            acc = b_ref[:, cols]
            for s, moved in enumerate(_causal_taps(x_ref[:, cols], hist.at[:, cols], first, SSM_CONV)):
                acc = acc + w_ref[SSM_CONV - 1 - s:SSM_CONV - s, cols] * moved
            xc_ref[:, cols] = acc
            xa_ref[:, cols] = acc * _sigmoid(acc)

    return pl.pallas_call(
        body, grid=(t_rows // tr,),
        in_specs=[_row(tr, CONV_DIM, OFF_XBC // CONV_DIM), _full((SSM_CONV, CONV_DIM)), _full((1, CONV_DIM))],
        out_specs=[_row(tr, CONV_DIM), _row(tr, CONV_DIM)],
        out_shape=[_sds((t_rows, CONV_DIM), F32), _sds((t_rows, CONV_DIM), F32)],
        scratch_shapes=[pltpu.VMEM((HALO, CONV_DIM), F32)],
        name="ssm_conv_fwd", compiler_params=_params(1))(proj, conv_w, conv_b)


def _ssm_post(y, proj, w):
    t_rows = y.shape[0]
    tr = _pick(t_rows, (384, 128))

    def body(y_ref, z_ref, w_ref, o_ref):
        z = z_ref[...].astype(F32)
        yz = y_ref[...] * z * _sigmoid(z)
        o_ref[...] = (yz * _rms(yz) * w_ref[...]).astype(BF16)

    return pl.pallas_call(body, grid=(t_rows // tr,),
                          in_specs=[_row(tr, D_INNER), _row(tr, D_INNER, OFF_Z // D_INNER), _full((1, D_INNER))],
                          out_specs=_row(tr, D_INNER), out_shape=_sds((t_rows, D_INNER), BF16),
                          name="ssm_post", compiler_params=_params(1))(y, proj, w)


def _mix_fwd(proj, y_ssm, y_attn):
    t_rows = y_ssm.shape[0]
    tr = _pick(t_rows, (384, 128))

    def body(g_ref, ys_ref, ya_ref, o_ref):
        g = _sigmoid(g_ref[...].astype(F32))
        o_ref[...] = (g[:, :D_MODEL] * ys_ref[...] + g[:, D_MODEL:] * ya_ref[...]).astype(BF16)

    return pl.pallas_call(body, grid=(t_rows // tr,),
                          in_specs=[_row(tr, 2 * D_MODEL, OFF_GATE // (2 * D_MODEL)), _row(tr, D_MODEL),
                                    _row(tr, D_MODEL)],
                          out_specs=_row(tr, D_MODEL), out_shape=_sds((t_rows, D_MODEL), BF16),
                          name="mix_fwd", compiler_params=_params(1))(proj, y_ssm, y_attn)


def _postmix(x, meta, mix, w_post, w_pre):
    t_rows = mix.shape[0]
    tr = _seq_rows(t_rows)

    def body(x_ref, meta_ref, m_ref, wp_ref, wf_ref, h1_ref, hn_ref):
        m = m_ref[...]
        h1 = _seq_tile(x_ref, meta_ref, pl.program_id(0)) + m * _rms(m) * wp_ref[...]
        h1 = jnp.where(_row_ids(h1.shape, pl.program_id(0), tr) >= META_PAD, h1, 0.0)
        h1_ref[...] = h1
        hn_ref[...] = (h1 * _rms(h1) * wf_ref[...]).astype(BF16)

    return pl.pallas_call(body, grid=(t_rows // tr,),
                          in_specs=_seq_specs(tr) + [_row(tr, D_MODEL), _full((1, D_MODEL)), _full((1, D_MODEL))],
                          out_specs=[_row(tr, D_MODEL), _row(tr, D_MODEL)],
                          out_shape=[_sds((t_rows, D_MODEL), F32), _sds((t_rows, D_MODEL), BF16)],
                          name="postmix", compiler_params=_params(1))(x, meta, mix, w_post, w_pre)


def _ffn_act(up, conv_w, conv_b):
    t_rows = up.shape[0]
    tr = CHUNK
    width = 2 * FFN_DIM

    def body(up_ref, w_ref, b_ref, u_ref, act_ref, hist):
        first = pl.program_id(0) == 0
        for c in range(0, FFN_DIM, STRIP):
            halves = []
            for base in (0, FFN_DIM):
                cols = slice(base + c, base + c + STRIP)
                u = b_ref[:, cols]
                for s, moved in enumerate(_causal_taps(up_ref[:, cols].astype(F32), hist.at[:, cols], first, FFN_CONV)):
                    u = u + w_ref[FFN_CONV - 1 - s:FFN_CONV - s, cols] * moved
                u_ref[:, cols] = u.astype(BF16)
                halves.append(u)
            a, g = halves
            act_ref[:, c:c + STRIP] = (a * _sigmoid(a) * g).astype(BF16)

    return pl.pallas_call(
        body, grid=(t_rows // tr,), in_specs=[_row(tr, width), _full((FFN_CONV, width)), _full((1, width))],
        out_specs=[_row(tr, width), _row(tr, FFN_DIM)],
        out_shape=[_sds((t_rows, width), BF16), _sds((t_rows, FFN_DIM), BF16)],
        scratch_shapes=[pltpu.VMEM((HALO, width), F32)],
        name="ffn_act", compiler_params=_params(1))(up, conv_w, conv_b)


def _final(h1, f, target, w):
    t_rows = h1.shape[0]
    tr = _seq_rows(t_rows)

    def body(h1_ref, f_ref, t_ref, w_ref, df_ref, dy_ref, dw_ref, loss_ref):
        i = pl.program_id(0)

        @pl.when(i == 0)
        def _():
            dw_ref[...] = jnp.zeros_like(dw_ref)
            loss_ref[...] = jnp.zeros_like(loss_ref)

        f_val = f_ref[...]
        r = _rms(f_val)
        wv = w_ref[...]
        h2 = h1_ref[...] + f_val * r * wv
        tgt = _under_tile(t_ref, jnp.zeros((CHUNK, D_MODEL), F32), i)
        diff = jnp.where(_row_ids(h2.shape, i, tr) >= CHUNK, h2 - tgt, 0.0)
        loss_ref[...] += 0.5 * jnp.sum(diff * diff) * (1.0 / D_MODEL)
        dy = diff * (1.0 / D_MODEL)
        dy_ref[...] = dy
        df, dw = _rms_bwd(f_val, r, wv, dy)
        df_ref[...] = df.astype(BF16)
        dw_ref[...] += dw

    return pl.pallas_call(
        body, grid=(t_rows // tr,),
        in_specs=[_row(tr, D_MODEL), _row(tr, D_MODEL), _token_rows(tr), _full((1, D_MODEL))],
        out_specs=[_row(tr, D_MODEL), _row(tr, D_MODEL), _full((1, D_MODEL)), _full((1, 128))],
        out_shape=[_sds((t_rows, D_MODEL), BF16), _sds((t_rows, D_MODEL), F32), _sds((1, D_MODEL), F32), _sds((1, 128), F32)],
        name="final", compiler_params=_params(1))(h1, f, target, w)


def _ffn_act_bwd(u, up, dact, conv_w):
    t_rows = u.shape[0]
    tr = CHUNK
    nt = t_rows // tr
    width = 2 * FFN_DIM

    def body(u_ref, up_ref, da_ref, w_ref, dup_ref, dw_ref, db_ref, ahead):
        @pl.when(pl.program_id(0) == 0)
        def _():
            dw_ref[...] = jnp.zeros_like(dw_ref)
            db_ref[...] = jnp.zeros_like(db_ref)

        first = pl.program_id(0) == 0
        for c in range(0, FFN_DIM, STRIP_BWD):
            ca, cg = slice(c, c + STRIP_BWD), slice(FFN_DIM + c, FFN_DIM + c + STRIP_BWD)
            a, g, d = u_ref[:, ca].astype(F32), u_ref[:, cg].astype(F32), da_ref[:, ca].astype(F32)
            s = _sigmoid(a)
            for cols, du in ((ca, d * g * s * (1.0 + a * (1.0 - s))), (cg, d * a * s)):
                x = up_ref[:, cols].astype(F32)
                dup = None
                for sh, moved in enumerate(_anticausal_taps(du, ahead.at[:, cols], first, FFN_CONV)):
                    k = FFN_CONV - 1 - sh
                    term = w_ref[k:k + 1, cols] * moved
                    dup = term if dup is None else dup + term
                    dw_ref[k:k + 1, cols] += jnp.sum(moved * x, axis=0, keepdims=True)
                db_ref[:, cols] += jnp.sum(du, axis=0, keepdims=True)
                dup_ref[:, cols] = dup.astype(BF16)

    return pl.pallas_call(
        body, grid=(nt,),
        in_specs=[_row_rev(tr, width, nt), _row_rev(tr, width, nt), _row_rev(tr, FFN_DIM, nt), _full((FFN_CONV, width))],
        out_specs=[_row_rev(tr, width, nt), _full((FFN_CONV, width)), _full((1, width))],
        out_shape=[_sds((t_rows, width), BF16), _sds((FFN_CONV, width), F32), _sds((1, width), F32)],
        scratch_shapes=[pltpu.VMEM((HALO, width), F32)],
        name="ffn_act_bwd", compiler_params=_params(1))(u, up, dact, conv_w)


def _postmix_bwd(h1, dhn2, dy, mix, w_pre, w_post):
    t_rows = h1.shape[0]
    tr = _pick(t_rows, (384, 128))

    def body(h1_ref, dhn_ref, dy_ref, m_ref, wf_ref, wp_ref, dmix_ref, dh_ref, dwf_ref, dwp_ref):
        @pl.when(pl.program_id(0) == 0)
        def _():
            dwf_ref[...] = jnp.zeros_like(dwf_ref)
            dwp_ref[...] = jnp.zeros_like(dwp_ref)

        h1v = h1_ref[...]
        dx, dwf = _rms_bwd(h1v, _rms(h1v), wf_ref[...], dhn_ref[...])
        dwf_ref[...] += dwf
        dh1 = dy_ref[...] + dx
        dh1 = jnp.where(_row_ids(dh1.shape, pl.program_id(0), tr) >= META_PAD, dh1, 0.0)
        dh_ref[...] = dh1
        m = m_ref[...]
        dmix, dwp = _rms_bwd(m, _rms(m), wp_ref[...], dh1)
        dwp_ref[...] += dwp
        dmix_ref[...] = dmix.astype(BF16)

    return pl.pallas_call(
        body, grid=(t_rows // tr,),
        in_specs=[_row(tr, D_MODEL) for _ in range(4)] + [_full((1, D_MODEL))] * 2,
        out_specs=[_row(tr, D_MODEL), _row(tr, D_MODEL), _full((1, D_MODEL)), _full((1, D_MODEL))],
        out_shape=[_sds((t_rows, D_MODEL), BF16), _sds((t_rows, D_MODEL), F32), _sds((1, D_MODEL), F32), _sds((1, D_MODEL), F32)],
        name="postmix_bwd", compiler_params=_params(1))(h1, dhn2, dy, mix, w_pre, w_post)


_ANY = pl.BlockSpec(memory_space=pl.ANY)


def _mix_bwd(dmixed, proj, y_ssm, y_attn, dproj):
    t_rows = dmixed.shape[0]
    tr = _pick(t_rows, (384, 128))

    def body(d_ref, g_ref, ys_ref, ya_ref, _, dys_ref, dya_ref, dg_ref):
        d = d_ref[...]
        g = _sigmoid(g_ref[...].astype(F32))
        g1, g2 = g[:, :D_MODEL], g[:, D_MODEL:]
        dys_ref[...] = (d * g1).astype(BF16)
        dya_ref[...] = (d * g2).astype(BF16)
        dg_ref[...] = jnp.concatenate([d * ys_ref[...] * g1 * (1.0 - g1), d * ya_ref[...] * g2 * (1.0 - g2)],
                                      axis=1).astype(BF16)

    return pl.pallas_call(
        body, grid=(t_rows // tr,),
        in_specs=[_row(tr, D_MODEL), _row(tr, 2 * D_MODEL, OFF_GATE // (2 * D_MODEL)), _row(tr, D_MODEL), _row(tr, D_MODEL),
                  _ANY],
        out_specs=[_row(tr, D_MODEL), _row(tr, D_MODEL), _row(tr, 2 * D_MODEL, OFF_GATE // (2 * D_MODEL))],
        out_shape=[_sds((t_rows, D_MODEL), BF16), _sds((t_rows, D_MODEL), BF16), _sds(dproj.shape, dproj.dtype)],
        input_output_aliases={4: 2},
        name="mix_bwd", compiler_params=_params(1))(dmixed, proj, y_ssm, y_attn, dproj)


def _ssm_post_bwd(y, proj, dyn, w, dproj):
    t_rows = y.shape[0]
    tr = CHUNK

    def body(y_ref, z_ref, d_ref, w_ref, _, dy_ref, dz_ref, dw_ref):
        @pl.when(pl.program_id(0) == 0)
        def _():
            dw_ref[...] = jnp.zeros_like(dw_ref)

        yv, z = y_ref[...], z_ref[...].astype(F32)
        sz = _sigmoid(z)
        silu = z * sz
        yz = yv * silu
        dyz, dw = _rms_bwd(yz, _rms(yz), w_ref[...], d_ref[...].astype(F32))
        dw_ref[...] += dw
        dy_ref[...] = dyz * silu
        dz_ref[...] = (dyz * yv * sz * (1.0 + z * (1.0 - sz))).astype(BF16)

    return pl.pallas_call(
        body, grid=(t_rows // tr,),
        in_specs=[_row(tr, D_INNER), _row(tr, D_INNER, OFF_Z // D_INNER), _row(tr, D_INNER), _full((1, D_INNER)), _ANY],
        out_specs=[_row(tr, D_INNER), _row(tr, D_INNER, OFF_Z // D_INNER), _full((1, D_INNER))],
        out_shape=[_sds((t_rows, D_INNER), F32), _sds(dproj.shape, dproj.dtype), _sds((1, D_INNER), F32)],
        input_output_aliases={4: 1},
        name="ssm_post_bwd", compiler_params=_params(1))(y, proj, dyn, w, dproj)


def _ssm_conv_bwd(xc, proj, dxs, dbm, dcm, conv_w, dproj):
    t_rows = xc.shape[0]
    tr = CHUNK
    nt = t_rows // tr
    bc_w = SSM_GROUPS * D_STATE

    def body(xc_ref, x_ref, dxs_ref, db_ref, dc_ref, w_ref, _, dx_ref, dw_ref, dbias_ref, ahead):
        first = pl.program_id(0) == 0

        @pl.when(first)
        def _():
            dw_ref[...] = jnp.zeros_like(dw_ref)
            dbias_ref[...] = jnp.zeros_like(dbias_ref)

        for c0 in range(0, CONV_DIM, STRIP_BWD):
            cols = slice(c0, c0 + STRIP_BWD)
            if c0 < D_INNER:
                dact = dxs_ref[:, cols]
            elif c0 < D_INNER + bc_w:
                dact = db_ref[:, c0 - D_INNER:c0 - D_INNER + STRIP_BWD]
            else:
                dact = dc_ref[:, c0 - D_INNER - bc_w:c0 - D_INNER - bc_w + STRIP_BWD]
            c = xc_ref[:, cols]
            s = _sigmoid(c)
            dpre = dact * s * (1.0 + c * (1.0 - s))
            x = x_ref[:, cols]
            dx = None
            for sh, moved in enumerate(_anticausal_taps(dpre, ahead.at[:, cols], first, SSM_CONV)):
                k = SSM_CONV - 1 - sh
                term = w_ref[k:k + 1, cols] * moved
                dx = term if dx is None else dx + term
                dw_ref[k:k + 1, cols] += jnp.sum(moved * x, axis=0, keepdims=True)
            dbias_ref[:, cols] += jnp.sum(dpre, axis=0, keepdims=True)
            dx_ref[:, cols] = dx.astype(BF16)

    xbc_block = OFF_XBC // CONV_DIM
    return pl.pallas_call(
        body, grid=(nt,),
        in_specs=[_row_rev(tr, CONV_DIM, nt), _row_rev(tr, CONV_DIM, nt, xbc_block), _row_rev(tr, D_INNER, nt),
                  _row_rev(tr, bc_w, nt), _row_rev(tr, bc_w, nt), _full((SSM_CONV, CONV_DIM)), _ANY],
        out_specs=[_row_rev(tr, CONV_DIM, nt, xbc_block), _full((SSM_CONV, CONV_DIM)), _full((1, CONV_DIM))],
        out_shape=[_sds(dproj.shape, dproj.dtype), _sds((SSM_CONV, CONV_DIM), F32), _sds((1, CONV_DIM), F32)],
        scratch_shapes=[pltpu.VMEM((HALO, CONV_DIM), F32)],
        input_output_aliases={6: 0},
        name="ssm_conv_bwd", compiler_params=_params(1))(xc, proj, dxs, dbm, dcm, conv_w, dproj)


def _prenorm_bwd(x, meta, dhn, dh, w):
    seq = x.shape[0]
    tr = _pick(seq, (512, 128))

    def body(x_ref, meta_ref, d_ref, r_ref, d0_ref, r0_ref, w_ref, dx_ref, dmeta_ref, dw_ref):
        wv = w_ref[...]

        @pl.when(pl.program_id(0) == 0)
        def _():
            h0 = jnp.concatenate([jnp.zeros((META_PAD, D_MODEL), F32), meta_ref[...]], axis=0)
            dx0, dw0 = _rms_bwd(h0, _rms(h0), wv, d0_ref[...])
            dw_ref[...] = dw0
            dmeta_ref[...] = (r0_ref[...] + dx0)[META_PAD:, :]

        h = x_ref[...]
        dx, dw = _rms_bwd(h, _rms(h), wv, d_ref[...])
        dw_ref[...] += dw
        dx_ref[...] = r_ref[...] + dx

    def shifted(tile):
        return pl.BlockSpec((pl.Element(tile), pl.Element(D_MODEL)), lambda i: (pl.multiple_of(i * tile + CHUNK, CHUNK), 0))

    first = pl.BlockSpec((CHUNK, D_MODEL), lambda i: (0, 0))
    return pl.pallas_call(
        body, grid=(seq // tr,),
        in_specs=[_row(tr, D_MODEL), _full((N_META, D_MODEL)), shifted(tr), shifted(tr), first, first, _full((1, D_MODEL))],
        out_specs=[_row(tr, D_MODEL), _full((N_META, D_MODEL)), _full((1, D_MODEL))],
        out_shape=[_sds((seq, D_MODEL), F32), _sds((N_META, D_MODEL), F32), _sds((1, D_MODEL), F32)],
        name="prenorm_bwd", compiler_params=_params(1))(x, meta, dhn, dh, dhn, dh, w)


def _dot01(x, m01, x_left, parts):
    acc, rest = None, x
    for i in range(parts):
        piece = rest.astype(BF16)
        term = (jnp.dot(piece, m01, preferred_element_type=F32) if x_left
                else jnp.dot(m01, piece, preferred_element_type=F32))
        acc = term if acc is None else acc + term
        if i + 1 < parts:
            rest = rest - piece.astype(F32)
    return acc


def _ssd_common(dt_raw, dt_bias, a_log, chunk_index):
    rows = lax.broadcasted_iota(jnp.int32, (CHUNK, CHUNK), 0)
    cols = lax.broadcasted_iota(jnp.int32, (CHUNK, CHUNK), 1)
    low = rows >= cols
    raw = dt_raw + dt_bias
    live = _row_ids(raw.shape, chunk_index, CHUNK) >= META_PAD
    dt = jnp.where(live, _softplus(raw), 0.0)
    a_head = -jnp.exp(a_log)
    cs = _dot01(dt * a_head, low.astype(BF16), False, 3)
    grow = jnp.exp(cs)
    fade = jnp.exp(cs[CHUNK - 1:CHUNK, :] - cs)
    expand = (lax.broadcasted_iota(jnp.int32, (CHUNK, GROUP_W), 1) // HEAD_P
              == lax.broadcasted_iota(jnp.int32, (CHUNK, GROUP_W), 0)).astype(BF16)
    fold = (lax.broadcasted_iota(jnp.int32, (GROUP_W, CHUNK), 0) // HEAD_P
            == lax.broadcasted_iota(jnp.int32, (GROUP_W, CHUNK), 1)).astype(BF16)
    return dict(low=low, triu=(rows <= cols).astype(BF16), raw=raw, live=live, dt=dt, a_head=a_head, cs=cs, cs_t=cs.T,
                fold=fold, dtx=_dot01(dt, expand, True, 2), growx=_dot01(grow, expand, True, 2),
                fadex=_dot01(fade, expand, True, 2))


def _decay_matrix(cm, j):
    diff = cm["cs"][:, j:j + 1] - cm["cs_t"][j:j + 1, :]
    return jnp.where(cm["low"], jnp.exp(jnp.where(cm["low"], diff, 0.0)), 0.0)


def _dot(a, b, dims):
    return lax.dot_general(a.astype(BF16), b.astype(BF16), (dims, ((), ())), preferred_element_type=F32)


def _dot_fine(a, b, dims):
    a_hi, b_hi = a.astype(BF16), b.astype(BF16)
    a_lo, b_lo = (a - a_hi.astype(F32)).astype(BF16), (b - b_hi.astype(F32)).astype(BF16)
    dn = (dims, ((), ()))
    return (lax.dot_general(a_hi, b_hi, dn, preferred_element_type=F32)
            + lax.dot_general(a_hi, b_lo, dn, preferred_element_type=F32)
            + lax.dot_general(a_lo, b_hi, dn, preferred_element_type=F32))


def _ssd_specs(nt, rev):
    def idx(c):
        return nt - 1 - c if rev else c
    bc_w = SSM_GROUPS * D_STATE
    xs = pl.BlockSpec((CHUNK, D_INNER), lambda c: (idx(c), 0))
    bm = pl.BlockSpec((CHUNK, bc_w), lambda c: (idx(c), D_INNER // bc_w))
    cm = pl.BlockSpec((CHUNK, bc_w), lambda c: (idx(c), D_INNER // bc_w + 1))
    dtr = pl.BlockSpec((CHUNK, SSM_GROUPS * 128), lambda c: (idx(c), OFF_DT // (SSM_GROUPS * 128)))
    par = _full((SSM_GROUPS, 1, 128))
    par_x = _full((SSM_GROUPS, 1, GROUP_W))
    return xs, bm, cm, dtr, par, par_x, idx


def _group_cols(g, width):
    return slice(g * width, (g + 1) * width)


def _ssd_fwd(xact, proj, dtb, alog, dskip_x):
    t_rows = xact.shape[0]
    nt = t_rows // CHUNK
    xs_spec, b_spec, c_spec, dtr_spec, par, par_x, _ = _ssd_specs(nt, False)

    def body(xs_ref, b_ref, c_ref, dtr_ref, dtb_ref, alog_ref, dsk_ref, y_ref, hst_ref, state):
        c = pl.program_id(0)

        @pl.when(c == 0)
        def _():
            state[...] = jnp.zeros_like(state)

        for g in range(SSM_GROUPS):
            wide, narrow = _group_cols(g, GROUP_W), _group_cols(g, D_STATE)
            cm = _ssd_common(dtr_ref[:, narrow], dtb_ref[g], alog_ref[g], c)
            xs, bm, cmat = xs_ref[:, wide], b_ref[:, narrow], c_ref[:, narrow]
            x_dt = xs * cm["dtx"]
            h_in = state[g]
            hst_ref[0, g] = h_in
            y_ref[:, wide] = _dot(cmat, h_in, ((1,), (0,))) * cm["growx"] + xs * dsk_ref[g]
            cb = _dot(cmat, bm, ((1,), (1,)))
            for j in range(HEADS_PER_GROUP):
                sl = slice(g * GROUP_W + j * HEAD_P, g * GROUP_W + (j + 1) * HEAD_P)
                y_ref[:, sl] += _dot(cb * _decay_matrix(cm, j), x_dt[:, j * HEAD_P:(j + 1) * HEAD_P], ((1,), (0,)))
            state[g] = h_in * cm["growx"][CHUNK - 1:CHUNK, :] + _dot_fine(bm, x_dt * cm["fadex"], ((0,), (0,)))

    return pl.pallas_call(
        body, grid=(nt,),
        in_specs=[xs_spec, b_spec, c_spec, dtr_spec, par, par, par_x],
        out_specs=[xs_spec, pl.BlockSpec((1, SSM_GROUPS, D_STATE, GROUP_W), lambda c: (c, 0, 0, 0))],
        out_shape=[_sds((t_rows, D_INNER), F32), _sds((nt, SSM_GROUPS, D_STATE, GROUP_W), F32)],
        scratch_shapes=[pltpu.VMEM((SSM_GROUPS, D_STATE, GROUP_W), F32)],
        name="ssd_fwd", compiler_params=_params(1))(xact, xact, xact, proj, dtb, alog, dskip_x)


def _ssd_bwd(xact, proj, dtb, alog, dskip_x, dy, hst, dproj):
    t_rows = xact.shape[0]
    nt = t_rows // CHUNK
    xs_spec, b_spec, c_spec, dtr_spec, par, par_x, idx = _ssd_specs(nt, True)
    h_spec = pl.BlockSpec((1, SSM_GROUPS, D_STATE, GROUP_W), lambda c: (idx(c), 0, 0, 0))
    hn_spec = pl.BlockSpec((1, SSM_GROUPS, D_STATE, GROUP_W), lambda c: (jnp.minimum(idx(c) + 1, nt - 1), 0, 0, 0))
    bc_out = pl.BlockSpec((CHUNK, SSM_GROUPS * D_STATE), lambda c: (idx(c), 0))

    def body(xs_ref, b_ref, c_ref, dtr_ref, dtb_ref, alog_ref, dsk_ref, dy_ref, h_ref, hn_ref, _,
             dxs_ref, db_ref, dc_ref, ddt_ref, dalog_ref, ddtb_ref, dd_ref, dstate, dx_buf):
        step = pl.program_id(0)

        @pl.when(step == 0)
        def _():
            dstate[...] = jnp.zeros_like(dstate)
            dalog_ref[...] = jnp.zeros_like(dalog_ref)
            ddtb_ref[...] = jnp.zeros_like(ddtb_ref)
            dd_ref[...] = jnp.zeros_like(dd_ref)

        for g in range(SSM_GROUPS):
            _ssd_bwd_group(g, idx(step), xs_ref, b_ref, c_ref, dtr_ref, dtb_ref, alog_ref, dsk_ref, dy_ref, h_ref, hn_ref,
                           dxs_ref, db_ref, dc_ref, ddt_ref, dalog_ref, ddtb_ref, dd_ref, dstate, dx_buf)

    return pl.pallas_call(
        body, grid=(nt,),
        in_specs=[xs_spec, b_spec, c_spec, dtr_spec, par, par, par_x, xs_spec, h_spec, hn_spec, _ANY],
        out_specs=[xs_spec, bc_out, bc_out, dtr_spec, par, par, par_x],
        out_shape=[_sds((t_rows, D_INNER), F32), _sds((t_rows, SSM_GROUPS * D_STATE), F32),
                   _sds((t_rows, SSM_GROUPS * D_STATE), F32), _sds(dproj.shape, dproj.dtype),
                   _sds((SSM_GROUPS, 1, 128), F32), _sds((SSM_GROUPS, 1, 128), F32), _sds((SSM_GROUPS, 1, GROUP_W), F32)],
        scratch_shapes=[pltpu.VMEM((SSM_GROUPS, D_STATE, GROUP_W), F32), pltpu.VMEM((CHUNK, GROUP_W), F32)],
        input_output_aliases={10: 3},
        name="ssd_bwd", compiler_params=_params(1))(xact, xact, xact, proj, dtb, alog, dskip_x, dy, hst, hst, dproj)


def _ssd_bwd_group(g, chunk, xs_ref, b_ref, c_ref, dtr_ref, dtb_ref, alog_ref, dsk_ref, dy_ref, h_ref, hn_ref,
                   dxs_ref, db_ref, dc_ref, ddt_ref, dalog_ref, ddtb_ref, dd_ref, dstate, dx_buf):
    wide, narrow = _group_cols(g, GROUP_W), _group_cols(g, D_STATE)
    cm = _ssd_common(dtr_ref[:, narrow], dtb_ref[g], alog_ref[g], chunk)
    xs, bm, cmat = xs_ref[:, wide], b_ref[:, narrow], c_ref[:, narrow]
    dsk = dsk_ref[g]
    x_dt = xs * cm["dtx"]
    h_in, h_next = h_ref[0, g], hn_ref[0, g]
    dyv = dy_ref[:, wide]
    dh = dstate[g]
    grow, fade = cm["growx"], cm["fadex"]
    dy_grow = dyv * grow
    x_fade = x_dt * fade
    cb = _dot(cmat, bm, ((1,), (1,)))
    ml = jnp.zeros((CHUNK, CHUNK), F32)
    row_id = lax.broadcasted_iota(jnp.int32, (CHUNK, CHUNK), 0)
    col_id = lax.broadcasted_iota(jnp.int32, (CHUNK, CHUNK), 1)
    w_rows = jnp.zeros((CHUNK, CHUNK), F32)
    w_cols = jnp.zeros((CHUNK, CHUNK), F32)
    for j in range(HEADS_PER_GROUP):
        sl = slice(j * HEAD_P, (j + 1) * HEAD_P)
        lm = _decay_matrix(cm, j)
        mlj = _dot(dyv[:, sl], x_dt[:, sl], ((1,), (1,))) * lm
        ml = ml + mlj
        wm = mlj * cb
        w_rows = jnp.where(col_id == j, jnp.sum(wm, axis=1, keepdims=True), w_rows)
        w_cols = jnp.where(row_id == j, jnp.sum(wm, axis=0, keepdims=True), w_cols)
        dx_buf[:, sl] = _dot(cb * lm, dyv[:, sl], ((0,), (0,)))
    dx_off = fade * _dot_fine(bm, dh, ((1,), (0,)))
    dx = dx_buf[...] + dx_off
    dc_ref[:, narrow] = _dot(ml, bm, ((1,), (0,))) + _dot(dy_grow, h_in, ((1,), (1,)))
    db_ref[:, narrow] = _dot(ml, cmat, ((0,), (0,))) + _dot(x_fade, dh, ((1,), (1,)))
    fold = cm["fold"]
    y_off = _dot_fine(cmat, h_in, ((1,), (0,))) * grow
    dcs = (w_rows - w_cols.T) + _dot01(dyv * y_off - x_dt * dx_off, fold, True, 2)
    tail = jnp.broadcast_to(jnp.sum(dh * h_next, axis=0, keepdims=True), (8, GROUP_W))
    tail = _dot01(tail, fold, True, 2)[0:1, :]
    last_row = lax.broadcasted_iota(jnp.int32, (CHUNK, 128), 0) == CHUNK - 1
    dcs = dcs + jnp.where(last_row, tail, 0.0)
    da = _dot01(dcs, cm["triu"], False, 3)
    ddt = da * cm["a_head"] + _dot01(dx * xs, fold, True, 2)
    ddt_raw = jnp.where(cm["live"], ddt * _sigmoid(cm["raw"]), 0.0)
    ddt_ref[:, narrow] = ddt_raw.astype(BF16)
    ddtb_ref[g] += jnp.sum(ddt_raw, axis=0, keepdims=True)
    dalog_ref[g] += jnp.sum(da * cm["dt"], axis=0, keepdims=True) * cm["a_head"]
    dd_ref[g] += jnp.sum(dyv * xs, axis=0, keepdims=True)
    dxs_ref[:, wide] = dx * cm["dtx"] + dyv * dsk
    dstate[g] = dh * grow[CHUNK - 1:CHUNK, :] + _dot_fine(cmat, dy_grow, ((0,), (0,)))


def _swa_bias():
    rows_q = ATTN_GROUP * CHUNK
    dist = (jnp.arange(rows_q) % CHUNK)[:, None] - jnp.arange(2 * CHUNK)[None, :] + CHUNK
    head = jnp.arange(KV_HEADS)[:, None] * ATTN_GROUP + jnp.arange(rows_q)[None, :] // CHUNK + 1
    slope = jnp.exp2(-8.0 * head.astype(F32) / ATTN_HEADS)
    return jnp.where((dist >= 0) & (dist < CHUNK), -slope[:, :, None] * dist.astype(F32)[None], NEG)


def _swa_probs(q_kv, k_prev, k_cur, k_first, sink, bias, n):
    rows_q = ATTN_GROUP * CHUNK
    qs = jnp.concatenate([q_kv[:, g * DH:(g + 1) * DH] for g in range(ATTN_GROUP)], axis=0) * (DH ** -0.5)
    kcat = jnp.concatenate([k_prev, k_cur], axis=0)
    kmeta = k_first[META_PAD:, :]
    key_ok = lax.broadcasted_iota(jnp.int32, (1, 2 * CHUNK), 1) + n * CHUNK >= 2 * CHUNK
    s_band = jnp.where(key_ok, _dot(qs, kcat, ((1,), (1,))) + bias, NEG)
    q_pos = lax.broadcasted_iota(jnp.int32, (rows_q, N_META), 0) % CHUNK + n * CHUNK - META_PAD
    ok_m = lax.broadcasted_iota(jnp.int32, (rows_q, N_META), 1) <= q_pos
    s_meta = jnp.where(ok_m, _dot(qs, kmeta, ((1,), (1,))), NEG)
    m = jnp.maximum(jnp.maximum(jnp.max(s_band, axis=1, keepdims=True), jnp.max(s_meta, axis=1, keepdims=True)), sink)
    p_band, p_meta, p_sink = jnp.exp(s_band - m), jnp.exp(s_meta - m), jnp.exp(sink - m)
    inv = 1.0 / (jnp.sum(p_band, axis=1, keepdims=True) + jnp.sum(p_meta, axis=1, keepdims=True) + p_sink)
    return qs, kcat, kmeta, p_band * inv, p_meta * inv, p_sink * inv


def _swa_specs(nt, rev):
    def idx(n):
        return nt - 1 - n if rev else n
    o = pl.BlockSpec((CHUNK, ATTN_HEADS * DH), lambda n: (idx(n), 0))
    chunks = (lambda c: jnp.maximum(c - 1, 0)), (lambda c: c), (lambda c: 0)
    qkv = [pl.BlockSpec((CHUNK, QKV_W), lambda n, f=f: (f(idx(n)), OFF_Q // QKV_W)) for f in chunks]
    sink = _full((KV_HEADS, ATTN_GROUP * CHUNK, 1))
    bias = _full((KV_HEADS, ATTN_GROUP * CHUNK, 2 * CHUNK))
    return o, qkv, sink, bias, idx


def _head_cols(k):
    kv_w = ATTN_GROUP * DH
    q0, k0, v0 = k * kv_w, OFF_K - OFF_Q + k * DH, OFF_V - OFF_Q + k * DH
    return slice(q0, q0 + kv_w), slice(k0, k0 + DH), slice(v0, v0 + DH)


def _swa_fwd(proj, sink_rows, bias):
    t_rows = proj.shape[0]
    nt = t_rows // CHUNK
    o_spec, qkv_specs, sink_spec, bias_spec, _ = _swa_specs(nt, False)
    kv_w = ATTN_GROUP * DH

    def body(prev_ref, cur_ref, first_ref, sink_ref, bias_ref, o_ref):
        n = pl.program_id(0)
        for k in range(KV_HEADS):
            qc, kc, vc = _head_cols(k)
            _, _, _, p_band, p_meta, _ = _swa_probs(cur_ref[:, qc], prev_ref[:, kc], cur_ref[:, kc], first_ref[:, kc],
                                                    sink_ref[k], bias_ref[k], n)
            vcat = jnp.concatenate([prev_ref[:, vc], cur_ref[:, vc]], axis=0)
            out = _dot(p_band, vcat, ((1,), (0,))) + _dot(p_meta, first_ref[:, vc][META_PAD:, :], ((1,), (0,)))
            for g in range(ATTN_GROUP):
                o_ref[:, k * kv_w + g * DH:k * kv_w + (g + 1) * DH] = out[g * CHUNK:(g + 1) * CHUNK, :]

    return pl.pallas_call(
        body, grid=(nt,), in_specs=qkv_specs + [sink_spec, bias_spec],
        out_specs=o_spec, out_shape=_sds((t_rows, ATTN_HEADS * DH), F32),
        name="swa_fwd", compiler_params=_params(1))(proj, proj, proj, sink_rows, bias)


def _swa_bwd(proj, sink_rows, bias, out, dout, dproj):
    t_rows = proj.shape[0]
    nt = t_rows // CHUNK
    o_spec, qkv_specs, sink_spec, bias_spec, idx = _swa_specs(nt, True)
    kv_w = ATTN_GROUP * DH
    k_off, v_off = OFF_K - OFF_Q, OFF_V - OFF_Q

    def body(prev_ref, cur_ref, first_ref, sink_ref, bias_ref, o_ref, do_ref, _, dqkv_ref, dsink_ref,
             carry_k, carry_v, meta_k, meta_v, dqkv_buf):
        step = pl.program_id(0)
        n = idx(step)

        @pl.when(step == 0)
        def _():
            carry_k[...] = jnp.zeros_like(carry_k)
            carry_v[...] = jnp.zeros_like(carry_v)
            meta_k[...] = jnp.zeros_like(meta_k)
            meta_v[...] = jnp.zeros_like(meta_v)
            dsink_ref[...] = jnp.zeros_like(dsink_ref)

        for k in range(KV_HEADS):
            cols = slice(k * kv_w, (k + 1) * kv_w)
            hd = slice(k * DH, (k + 1) * DH)
            qc, kc, vc = _head_cols(k)
            qs, kcat, kmeta, p_band, p_meta, p_sink = _swa_probs(cur_ref[:, qc], prev_ref[:, kc], cur_ref[:, kc],
                                                                 first_ref[:, kc], sink_ref[k], bias_ref[k], n)
            vcat = jnp.concatenate([prev_ref[:, vc], cur_ref[:, vc]], axis=0)
            vmeta = first_ref[:, vc][META_PAD:, :]
            o, do = o_ref[:, cols], do_ref[:, cols]
            os_ = jnp.concatenate([o[:, g * DH:(g + 1) * DH] for g in range(ATTN_GROUP)], axis=0)
            dos = jnp.concatenate([do[:, g * DH:(g + 1) * DH] for g in range(ATTN_GROUP)], axis=0)
            delta = jnp.sum(dos * os_, axis=1, keepdims=True)
            ds_band = p_band * (_dot(dos, vcat, ((1,), (1,))) - delta)
            ds_meta = p_meta * (_dot(dos, vmeta, ((1,), (1,))) - delta)
            ds_sink = -p_sink * delta
            dqs = (_dot(ds_band, kcat, ((1,), (0,))) + _dot(ds_meta, kmeta, ((1,), (0,)))) * (DH ** -0.5)
            for g in range(ATTN_GROUP):
                dqkv_buf[:, k * kv_w + g * DH:k * kv_w + (g + 1) * DH] = dqs[g * CHUNK:(g + 1) * CHUNK, :]
                dsink_ref[k, g:g + 1, :] += jnp.sum(ds_sink[g * CHUNK:(g + 1) * CHUNK, :])
            dkcat = _dot(ds_band, qs, ((0,), (0,)))
            dvcat = _dot(p_band, dos, ((0,), (0,)))
            meta_k[:, hd] += _dot(ds_meta, qs, ((0,), (0,)))
            meta_v[:, hd] += _dot(p_meta, dos, ((0,), (0,)))
            dqkv_buf[:, kc] = dkcat[CHUNK:, :] + carry_k[:, hd]
            dqkv_buf[:, vc] = dvcat[CHUNK:, :] + carry_v[:, hd]
            carry_k[:, hd] = dkcat[:CHUNK, :]
            carry_v[:, hd] = dvcat[:CHUNK, :]

        @pl.when(n == 0)
        def _():
            dqkv_buf[META_PAD:, k_off:k_off + KV_W] += meta_k[...]
            dqkv_buf[META_PAD:, v_off:v_off + KV_W] += meta_v[...]

        dqkv_ref[...] = dqkv_buf[...].astype(BF16)

    return pl.pallas_call(
        body, grid=(nt,),
        in_specs=qkv_specs + [sink_spec, bias_spec, o_spec, o_spec, pl.BlockSpec(memory_space=pl.ANY)],
        out_specs=[qkv_specs[1], _full((KV_HEADS, 8, 128))],
        out_shape=[_sds(dproj.shape, dproj.dtype), _sds((KV_HEADS, 8, 128), F32)],
        scratch_shapes=[pltpu.VMEM((CHUNK, KV_W), F32), pltpu.VMEM((CHUNK, KV_W), F32),
                        pltpu.VMEM((N_META, KV_W), F32), pltpu.VMEM((N_META, KV_W), F32),
                        pltpu.VMEM((CHUNK, QKV_W), F32)],
        input_output_aliases={7: 0},
        name="swa_bwd", compiler_params=_params(1))(proj, proj, proj, sink_rows, bias, out, dout, dproj)


def _pack_w_in_t(w_in_t):
    w_dt = w_in_t[CUT_DT:CUT_Q].reshape(SSM_GROUPS, HEADS_PER_GROUP, D_MODEL)
    w_dt = jnp.pad(w_dt, ((0, 0), (0, 128 - HEADS_PER_GROUP), (0, 0))).reshape(SSM_GROUPS * 128, D_MODEL)
    return jnp.concatenate([w_in_t[CUT_Z:CUT_XBC], w_in_t[CUT_G:], w_dt, w_in_t[CUT_Q:CUT_G], w_in_t[CUT_XBC:CUT_DT]], axis=0)


def _unpack_w_in_t(wp_t):
    w_dt = wp_t[OFF_DT:OFF_Q].reshape(SSM_GROUPS, 128, D_MODEL)[:, :HEADS_PER_GROUP].reshape(SSM_HEADS, D_MODEL)
    return jnp.concatenate([wp_t[OFF_Z:OFF_GATE], wp_t[OFF_XBC:], w_dt, wp_t[OFF_Q:OFF_XBC], wp_t[OFF_GATE:OFF_DT]], axis=0)


def _group_rows(v, width):
    return jnp.pad(v.reshape(SSM_GROUPS, 1, HEADS_PER_GROUP), ((0, 0), (0, 0), (0, width - HEADS_PER_GROUP)))


def _local_step(x, target, wt, late_weights=None, on_grad=None, started=None):
    seq = x.shape[0]
    grads = {}

    def emit(name, g):
        grads[name] = g
        return None if on_grad is None else on_grad(name, g)
    meta = wt["meta_tokens"]
    wp_t = _pack_w_in_t(wt["w_in_t"])
    dtb = _group_rows(wt["ssm_dt_bias"].reshape(-1), 128)
    alog = _group_rows(wt["ssm_a_log"].reshape(-1), 128)
    dskip_x = jnp.repeat(wt["ssm_d_skip"].reshape(-1), HEAD_P).reshape(SSM_GROUPS, 1, GROUP_W)
    sink_rows = jnp.repeat(wt["attn_sinks"].reshape(KV_HEADS, ATTN_GROUP), CHUNK, axis=1).reshape(KV_HEADS, ATTN_GROUP * CHUNK, 1)

    hn = _prenorm(x, meta, wt["norm_pre_mix"])
    proj = _matmul(hn, wp_t, tb=True, name="in_proj", after=started)
    xc, xact = _ssm_conv_fwd(proj, wt["ssm_conv_w"], wt["ssm_conv_b"])
    y, hst = _ssd_fwd(xact, proj, dtb, alog, dskip_x)
    yn = _ssm_post(y, proj, wt["ssm_norm"])
    if late_weights is not None:
        wt = {**wt, **late_weights(yn)}
    y_ssm = _matmul(yn, wt["w_ssm_out"], name="ssm_out")
    bias = _swa_bias()
    attn = _swa_fwd(proj, sink_rows, bias)
    y_attn = _matmul(attn, wt["w_attn_out"], name="attn_out")
    mixed = _mix_fwd(proj, y_ssm, y_attn)
    mix = _matmul(mixed, wt["w_mix_out"], name="mix_out")
    h1, hn2 = _postmix(x, meta, mix, wt["norm_post_mix"], wt["norm_pre_ffn"])
    up = _matmul(hn2, wt["w_ffn_up_t"], tb=True, out_dtype=BF16, name="ffn_up")
    u, act = _ffn_act(up, wt["ffn_conv_w"], wt["ffn_conv_b"])
    f = _matmul(act, wt["w_ffn_down"], name="ffn_down")
    df, dy, g_norm_post_ffn, loss_row = _final(h1, f, target, wt["norm_post_ffn"])

    grads["norm_post_ffn"] = g_norm_post_ffn
    sent = emit("w_ffn_down", _matmul(act, df, ta=True, out_dtype=BF16, name="dw_ffn_down"))
    dact = _matmul(df, wt["w_ffn_down"], tb=True, out_dtype=BF16, name="d_act", after=sent)
    dup, grads["ffn_conv_w"], grads["ffn_conv_b"] = _ffn_act_bwd(u, up, dact, wt["ffn_conv_w"])
    sent = emit("w_ffn_up_t", _matmul(dup, hn2, ta=True, out_dtype=BF16, name="dw_ffn_up"))
    dhn2 = _matmul(dup, wt["w_ffn_up_t"], name="d_hn2", after=sent)
    dmix, dh, grads["norm_pre_ffn"], grads["norm_post_mix"] = _postmix_bwd(h1, dhn2, dy, mix, wt["norm_pre_ffn"], wt["norm_post_mix"])
    sent = emit("w_mix_out", _matmul(mixed, dmix, ta=True, out_dtype=BF16, name="dw_mix_out"))
    dmixed = _matmul(dmix, wt["w_mix_out"], tb=True, name="d_mixed", after=sent)
    dy_ssm, dy_attn, dproj = _mix_bwd(dmixed, proj, y_ssm, y_attn, lax.empty(proj.shape, BF16))
    sent = emit("w_ssm_out", _matmul(yn, dy_ssm, ta=True, out_dtype=BF16, name="dw_ssm_out"))
    dyn = _matmul(dy_ssm, wt["w_ssm_out"], tb=True, out_dtype=BF16, name="d_yn", after=sent)
    sent = emit("w_attn_out", _matmul(attn, dy_attn, ta=True, out_dtype=BF16, name="dw_attn_out"))
    dattn = _matmul(dy_attn, wt["w_attn_out"], tb=True, name="d_attn", after=sent)
    dy_ssd, dproj, grads["ssm_norm"] = _ssm_post_bwd(y, proj, dyn, wt["ssm_norm"], dproj)
    dxs, dbm, dcm, dproj, dalog, ddtb, dd_x = _ssd_bwd(xact, proj, dtb, alog, dskip_x, dy_ssd, hst, dproj)
    grads["ssm_a_log"] = dalog[:, 0, :HEADS_PER_GROUP].reshape(1, SSM_HEADS)
    grads["ssm_dt_bias"] = ddtb[:, 0, :HEADS_PER_GROUP].reshape(1, SSM_HEADS)
    grads["ssm_d_skip"] = dd_x.reshape(SSM_HEADS, HEAD_P).sum(axis=1).reshape(1, SSM_HEADS)
    dproj, grads["ssm_conv_w"], grads["ssm_conv_b"] = _ssm_conv_bwd(xc, proj, dxs, dbm, dcm, wt["ssm_conv_w"], dproj)
    dproj, dsink = _swa_bwd(proj, sink_rows, bias, attn, dattn, dproj)
    grads["attn_sinks"] = dsink[:, :ATTN_GROUP, 0].reshape(1, ATTN_HEADS)
    sent = emit("w_in_t", _unpack_w_in_t(_matmul(dproj, hn, ta=True, out_dtype=BF16, name="dw_in")))
    dhn = _matmul(dproj, wp_t, name="d_hn", after=sent)
    grad_x, grads["meta_tokens"], grads["norm_pre_mix"] = _prenorm_bwd(x, meta, dhn, dh, wt["norm_pre_mix"])
    return loss_row[0, 0], grad_x, grads


def _all_gather(shards):
    n = len(shards)

    def body(*refs):
        ins, outs = refs[:n], refs[n:2 * n]
        send_sems, recv_sems, local_sems = refs[2 * n:]
        x, y, c = lax.axis_index("x"), lax.axis_index("y"), lax.axis_index("c")
        me, sibling = (x, y, c), (x, y, 1 - c)
        chips = [(1 - x, y), (x, 1 - y), (1 - x, 1 - y)]

        def slot(a, dev):
            return outs[a].at[4 * dev[0] + 2 * dev[1] + dev[2]]

        def copy(k, a, block, to, src=None):
            return pltpu.make_async_remote_copy(
                src_ref=slot(a, block) if src is None else src, dst_ref=slot(a, block),
                send_sem=send_sems.at[k, a], recv_sem=recv_sems.at[k, a],
                device_id=to, device_id_type=pl.DeviceIdType.MESH)

        mine = [pltpu.make_async_copy(ins[a], slot(a, me), local_sems.at[a]) for a in range(n)]
        for cp in mine:
            cp.start()
        first = [copy(0, a, me, sibling, src=ins[a]) for a in range(n)]
        for j, chip in enumerate(chips):
            first += [copy(1 + j, a, me, (*chip, c), src=ins[a]) for a in range(n)]
        for cp in first:
            cp.start()
        passed = []
        for j, chip in enumerate(chips):
            for a in range(n):
                copy(1 + j, a, (*chip, c), me).wait_recv()
                fwd = copy(4 + j, a, (*chip, c), sibling)
                fwd.start()
                passed.append(fwd)
        for a in range(n):
            copy(0, a, sibling, me).wait_recv()
        for j, chip in enumerate(chips):
            for a in range(n):
                copy(4 + j, a, (*chip, 1 - c), me).wait_recv()
        for cp in first + passed:
            cp.wait_send()
        for cp in mine:
            cp.wait()

    hbm = pl.BlockSpec(memory_space=pl.ANY)
    return pl.pallas_call(
        body, in_specs=[hbm] * n, out_specs=[hbm] * n,
        out_shape=[_sds((N_DEV,) + s.shape, s.dtype) for s in shards],
        scratch_shapes=[pltpu.SemaphoreType.DMA((7, n)), pltpu.SemaphoreType.DMA((7, n)), pltpu.SemaphoreType.DMA((n,))],
        name="gather_weights")(*shards)


def _peer_table():
    x, y, c = lax.axis_index("x"), lax.axis_index("y"), lax.axis_index("c")
    peers = []
    for k in range(N_DEV - 1):
        bits = k + 1
        p = (x ^ ((bits >> 2) & 1), y ^ ((bits >> 1) & 1), c ^ (bits & 1))
        peers.append((k, p, 4 * p[0] + 2 * p[1] + p[2]))
    return 4 * x + 2 * y + c, peers


_HBM = pl.BlockSpec(memory_space=pltpu.HBM)
_SEM = pl.BlockSpec(memory_space=pltpu.SEMAPHORE)
_EFFECT = pltpu.SideEffectType.DATAFLOW_SIDE_EFFECTING


def _push_copy(src, land, send_sems, recv_sems, a, k, p, src_slot, dst_slot):
    sem = a * (N_DEV - 1) + k
    return pltpu.make_async_remote_copy(
        src_ref=src[a] if src_slot is None else src[a].at[src_slot], dst_ref=land[a].at[dst_slot],
        send_sem=send_sems.at[sem], recv_sem=recv_sems.at[sem], device_id=p, device_id_type=pl.DeviceIdType.MESH)


def _push_start(srcs, scatter, name):
    n = len(srcs)
    lands = [lax.empty(s.shape if scatter else (N_DEV,) + s.shape, s.dtype) for s in srcs]

    def body(*refs):
        src, land = refs[:n], refs[n:2 * n]
        send_sems, recv_sems, token = refs[2 * n], refs[2 * n + 1], refs[-1]
        my_id, peers = _peer_table()
        for a in range(n):
            for k, p, p_id in peers:
                _push_copy(src, land, send_sems, recv_sems, a, k, p, p_id if scatter else None, my_id).start()
        token[...] = jnp.zeros_like(token)

    sems = pltpu.SemaphoreType.DMA(((N_DEV - 1) * n,))
    res = pl.pallas_call(
        body, name=name,
        out_shape=(sems, sems, *[pltpu.HBM(a.shape, a.dtype) for a in srcs + lands], _sds((8, 128), F32)),
        in_specs=[_HBM] * (2 * n), out_specs=(_SEM, _SEM, *[_HBM] * (2 * n), pl.BlockSpec(memory_space=pltpu.VMEM)),
        input_output_aliases={i: 2 + i for i in range(2 * n)},
        compiler_params=pltpu.CompilerParams(has_side_effects=_EFFECT),
    )(*[pltpu.with_memory_space_constraint(a, pltpu.HBM) for a in srcs + lands])
    return dict(send=res[0], recv=res[1], src=list(res[2:2 + n]), land=list(res[2 + n:2 + 2 * n]), token=res[-1],
                scatter=scatter)


def _push_wait(handle, after, name):
    n = len(handle["src"])
    scatter = handle["scatter"]

    def body(*refs):
        src, land = refs[:n], refs[n:2 * n]
        send_sems, recv_sems = refs[2 * n], refs[2 * n + 1]
        _, peers = _peer_table()
        for a in range(n):
            for k, p, p_id in peers:
                cp = _push_copy(src, land, send_sems, recv_sems, a, k, p, p_id if scatter else None, p_id)
                cp.wait_send()
                cp.wait_recv()

    arrays = handle["src"] + handle["land"]
    res = pl.pallas_call(
        body, name=name, out_shape=tuple(pltpu.HBM(a.shape, a.dtype) for a in arrays),
        in_specs=[_HBM] * (2 * n) + [_SEM, _SEM, pl.BlockSpec(memory_space=pl.ANY)], out_specs=tuple([_HBM] * (2 * n)),
        input_output_aliases={i: i for i in range(2 * n)},
        compiler_params=pltpu.CompilerParams(has_side_effects=_EFFECT),
    )(*arrays, handle["send"], handle["recv"], after)
    return list(res[:n]), list(res[n:])


def _slot_sum(p_ref, own_ref):
    if own_ref is not None:
        my_id = 4 * lax.axis_index("x") + 2 * lax.axis_index("y") + lax.axis_index("c")
        mine = own_ref[...].astype(F32)
    g = None
    for s in range(p_ref.shape[0]):
        term = p_ref[s].astype(F32)
        if own_ref is not None:
            term = jnp.where(my_id == s, mine, term)
        g = term if g is None else g + term
    return g


def _to_bf16(arrays):
    n = len(arrays)

    def body(*refs):
        for i in range(n):
            refs[n + i][...] = refs[i][...].astype(BF16)

    return pl.pallas_call(body, out_shape=[_sds(a.shape, BF16) for a in arrays], name="weights_to_bf16",
                          compiler_params=pltpu.CompilerParams(vmem_limit_bytes=VMEM_LIMIT))(*arrays)


def _adamw(parts, own, w, m, v, name):
    unit_rows = w.ndim == 3
    rows, cols = w.shape[0], w.shape[-1]
    if rows % 16 == 0:
        tr, tc = _pick(rows, (256, 128, 176, 64, 32, 16)), cols
    else:
        tr, tc = rows, _pick(cols, (256, 128))

    def body(*refs):
        if own is None:
            p_ref, w_ref, m_ref, v_ref, g_ref, d_ref, nm_ref, nv_ref = refs
            own_ref = None
        else:
            p_ref, own_ref, w_ref, m_ref, v_ref, g_ref, d_ref, nm_ref, nv_ref = refs
        g = _slot_sum(p_ref, own_ref)
        if unit_rows:
            g = g.reshape(tr, 1, tc)
        m_new = ADAM_B1 * m_ref[...] + (1.0 - ADAM_B1) * g
        v_new = ADAM_B2 * v_ref[...] + (1.0 - ADAM_B2) * (g * g)
        m_hat = m_new / (1.0 - ADAM_B1 ** ADAM_STEP)
        v_hat = v_new / (1.0 - ADAM_B2 ** ADAM_STEP)
        g_ref[...] = g
        d_ref[...] = -ADAM_LR * (m_hat / (jnp.sqrt(v_hat) + ADAM_EPS) + ADAM_WD * w_ref[...])
        nm_ref[...] = m_new
        nv_ref[...] = v_new

    by_rows = tc == cols
    spec = pl.BlockSpec((tr, tc), (lambda i: (i, 0)) if by_rows else (lambda i: (0, i)))
    state_spec = spec if not unit_rows else pl.BlockSpec((tr, 1, tc), (lambda i: (i, 0, 0)) if by_rows else (lambda i: (0, 0, i)))
    parts_spec = pl.BlockSpec((parts.shape[0], tr, tc), (lambda i: (0, i, 0)) if by_rows else (lambda i: (0, 0, i)))
    operands = (parts, w, m, v) if own is None else (parts, own, w, m, v)
    return pl.pallas_call(
        body, grid=(rows // tr if by_rows else cols // tc,),
        in_specs=[parts_spec] + ([] if own is None else [spec]) + [state_spec] * 3,
        out_specs=[state_spec] * 4, out_shape=[_sds(w.shape, F32)] * 4,
        name=name, compiler_params=_params(1))(*operands)


SMALL_REPLICATED = (("norm_pre_mix", 1024), ("ssm_conv_b", 3072), ("ssm_dt_bias", 32), ("ssm_a_log", 32),
                    ("ssm_d_skip", 32), ("ssm_norm", 2048), ("attn_sinks", 16), ("norm_post_mix", 1024),
                    ("norm_pre_ffn", 1024), ("ffn_conv_b", 5632), ("norm_post_ffn", 1024))
SMALL_SHARDED = (("meta_tokens", (N_META, D_MODEL // N_DEV)), ("ssm_conv_w", (SSM_CONV, CONV_DIM // N_DEV)),
                 ("ffn_conv_w", (FFN_CONV, 2 * FFN_DIM // N_DEV)))
BIG = (("w_in", (D_MODEL, N_IN // N_DEV), 1), ("w_ssm_out", (D_INNER // N_DEV, D_MODEL), 0),
       ("w_attn_out", (D_MODEL // N_DEV, D_MODEL), 0), ("w_mix_out", (D_MODEL // N_DEV, D_MODEL), 0),
       ("w_ffn_up", (D_MODEL, 2 * FFN_DIM // N_DEV), 1), ("w_ffn_down", (FFN_DIM // N_DEV, D_MODEL), 0))


def _rows_of(size):
    return -(-size // 128)


def _as_rows(flat):
    size = flat.shape[-1]
    rows = _rows_of(size)
    flat = jnp.pad(flat, [(0, 0)] * (flat.ndim - 1) + [(0, rows * 128 - size)])
    return flat.reshape(flat.shape[:-1] + (rows, 128))


def _pack_small(rep, sharded):
    pieces = [_as_rows(rep[name].reshape(-1)) for name, _ in SMALL_REPLICATED]
    pieces += [_as_rows(sharded[name].reshape(-1)) for name, _ in SMALL_SHARDED]
    packed = jnp.concatenate(pieces, axis=0)
    return jnp.pad(packed, ((0, -packed.shape[0] % 8), (0, 0)))


def _unpack_small(packed):
    out, row = {}, 0
    for name, size in SMALL_REPLICATED:
        out[name] = packed[row:row + _rows_of(size)].reshape(-1)[:size].reshape(1, size)
        row += _rows_of(size)
    for name, (r, c) in SMALL_SHARDED:
        out[name] = packed[row:row + _rows_of(r * c)].reshape(-1)[:r * c].reshape(r, c)
        row += _rows_of(r * c)
    return out


def _shard_major(g, shape, axis):
    r, c = shape
    if axis == 0:
        return g.reshape(N_DEV, r, c)
    return g.reshape(r, N_DEV, c).transpose(1, 0, 2)


def kernel(x, meta_tokens, norm_pre_mix, w_in, ssm_conv_w, ssm_conv_b, ssm_dt_bias, ssm_a_log, ssm_d_skip, ssm_norm, w_ssm_out, attn_sinks, w_attn_out, w_mix_out, norm_post_mix, norm_pre_ffn, w_ffn_up, ffn_conv_w, ffn_conv_b, w_ffn_down, norm_post_ffn, loss_target, m_meta_tokens, m_norm_pre_mix, m_w_in, m_ssm_conv_w, m_ssm_conv_b, m_ssm_dt_bias, m_ssm_a_log, m_ssm_d_skip, m_ssm_norm, m_w_ssm_out, m_attn_sinks, m_w_attn_out, m_w_mix_out, m_norm_post_mix, m_norm_pre_ffn, m_w_ffn_up, m_ffn_conv_w, m_ffn_conv_b, m_w_ffn_down, m_norm_post_ffn, v_meta_tokens, v_norm_pre_mix, v_w_in, v_ssm_conv_w, v_ssm_conv_b, v_ssm_dt_bias, v_ssm_a_log, v_ssm_d_skip, v_ssm_norm, v_w_ssm_out, v_attn_sinks, v_w_attn_out, v_w_mix_out, v_norm_post_mix, v_norm_pre_ffn, v_w_ffn_up, v_ffn_conv_w, v_ffn_conv_b, v_w_ffn_down, v_norm_post_ffn):
    names = ("meta_tokens", "norm_pre_mix", "w_in", "ssm_conv_w", "ssm_conv_b", "ssm_dt_bias", "ssm_a_log", "ssm_d_skip",
             "ssm_norm", "w_ssm_out", "attn_sinks", "w_attn_out", "w_mix_out", "norm_post_mix", "norm_pre_ffn", "w_ffn_up",
             "ffn_conv_w", "ffn_conv_b", "w_ffn_down", "norm_post_ffn")
    w_loc = dict(zip(names, (meta_tokens, norm_pre_mix, w_in, ssm_conv_w, ssm_conv_b, ssm_dt_bias, ssm_a_log, ssm_d_skip,
                             ssm_norm, w_ssm_out, attn_sinks, w_attn_out, w_mix_out, norm_post_mix, norm_pre_ffn, w_ffn_up,
                             ffn_conv_w, ffn_conv_b, w_ffn_down, norm_post_ffn)))
    m_loc = dict(zip(names, (m_meta_tokens, m_norm_pre_mix, m_w_in, m_ssm_conv_w, m_ssm_conv_b, m_ssm_dt_bias, m_ssm_a_log,
                             m_ssm_d_skip, m_ssm_norm, m_w_ssm_out, m_attn_sinks, m_w_attn_out, m_w_mix_out, m_norm_post_mix,
                             m_norm_pre_ffn, m_w_ffn_up, m_ffn_conv_w, m_ffn_conv_b, m_w_ffn_down, m_norm_post_ffn)))
    v_loc = dict(zip(names, (v_meta_tokens, v_norm_pre_mix, v_w_in, v_ssm_conv_w, v_ssm_conv_b, v_ssm_dt_bias, v_ssm_a_log,
                             v_ssm_d_skip, v_ssm_norm, v_w_ssm_out, v_attn_sinks, v_w_attn_out, v_w_mix_out, v_norm_post_mix,
                             v_norm_pre_ffn, v_w_ffn_up, v_ffn_conv_w, v_ffn_conv_b, v_w_ffn_down, v_norm_post_ffn)))

    def local2d(d, name):
        a = d[name]
        return a if name == "meta_tokens" else a.reshape(a.shape[1:])

    def turned2d(d, name):
        a = jnp.swapaxes(d[name], 1, 2)
        return a.reshape(a.shape[1:])

    my_id = 4 * lax.axis_index("x") + 2 * lax.axis_index("y") + lax.axis_index("c")
    big = {name: (shape, axis) for name, shape, axis in BIG}

    def whole(name, g):
        return g.reshape(N_DEV * g.shape[1], g.shape[2])

    def key(name):
        return name + "_t" if big[name][1] == 1 else name

    by_rows = [name for name, _, axis in BIG if axis == 0]
    send_bf16 = dict(zip(by_rows, _to_bf16([local2d(w_loc, name) for name in by_rows])))
    for name, _, axis in BIG:
        if axis == 1:
            send_bf16[name] = turned2d(w_loc, name).astype(BF16)
    small_shard_pack = jnp.concatenate([_as_rows(local2d(w_loc, name).reshape(-1)) for name, _ in SMALL_SHARDED], axis=0)
    small_shard_pack = jnp.pad(small_shard_pack, ((0, -small_shard_pack.shape[0] % 8), (0, 0)))
    first = _all_gather([send_bf16["w_in"], small_shard_pack])
    rest_names = [name for name, _, _ in BIG if name != "w_in"]
    rest = [send_bf16[name] for name in rest_names]
    rest, first = lax.optimization_barrier((rest, first))
    rest_handle = _push_start(rest, False, "gather_rest_start")
    wt = {"w_in_t": whole("w_in", first[0])}
    row = 0
    for name, (r, c) in SMALL_SHARDED:
        blocks = first[1][:, row:row + _rows_of(r * c)].reshape(N_DEV, -1)[:, :r * c].reshape(N_DEV, r, c)
        wt[name] = blocks.transpose(1, 0, 2).reshape(r, N_DEV * c)
        row += _rows_of(r * c)
    for name, size in SMALL_REPLICATED:
        wt[name] = w_loc[name].reshape(1, size)

    def late_weights(after):
        own, landed = _push_wait(rest_handle, after, "gather_rest_wait")
        out = {}
        for name, mine, land in zip(rest_names, own, landed):
            out[key(name)] = whole(name, lax.dynamic_update_index_in_dim(land, mine, my_id, 0))
        return out

    sent = {}

    def on_grad(known_as, g):
        name = known_as.removesuffix("_t")
        by_owner = g.reshape(N_DEV, g.shape[0] // N_DEV, g.shape[1])
        sent[name] = _push_start([by_owner], True, "send_" + name)
        return sent[name]["token"]

    loss_part, grad_x, grads = _local_step(x[0], loss_target[0], wt, late_weights, on_grad, rest_handle["token"])
    loss = lax.psum(loss_part, AXES)

    small_parts = []
    for name, (r, c) in SMALL_SHARDED:
        small_parts.append(_as_rows(_shard_major(grads[name], (r, c), 1).reshape(N_DEV, r * c)))
    rep_rows = jnp.concatenate([_as_rows(grads[name].reshape(-1)) for name, _ in SMALL_REPLICATED], axis=0)
    small_send = jnp.concatenate([jnp.broadcast_to(rep_rows[None], (N_DEV,) + rep_rows.shape)] + small_parts, axis=1)
    small_send = jnp.pad(small_send, ((0, 0), (0, -small_send.shape[1] % 8), (0, 0)))
    small_handle = _push_start([small_send], True, "send_small")

    def small_pack(d):
        return _pack_small({name: d[name] for name, _ in SMALL_REPLICATED}, {name: local2d(d, name) for name, _ in SMALL_SHARDED})

    def arrived(handle, after, name):
        src, landed = _push_wait(handle, after, "arrived_" + name)
        return landed[0], lax.dynamic_index_in_dim(src[0], my_id, 0, keepdims=False)

    grad_w, delta_w, new_m, new_v = {}, {}, {}, {}
    outs = None
    after = small_handle["token"]
    for name, handle in sent.items():
        if name == "w_in":
            parts, own = arrived(small_handle, after, "small")
            outs = _adamw(parts, own, small_pack(w_loc), small_pack(m_loc), small_pack(v_loc), "adamw_small")
            after = outs[0]
        parts, own = arrived(handle, after, name)
        turned = big[name][1] == 1
        unit_rows = turned and big[name][0][1] % 8 != 0
        if unit_rows:
            state = [jnp.transpose(d[name], (2, 0, 1)) for d in (w_loc, m_loc, v_loc)]
        else:
            state = [turned2d(d, name) if turned else local2d(d, name) for d in (w_loc, m_loc, v_loc)]
        results = _adamw(parts, own, *state, "adamw_" + name)
        after = results[0]
        full = (1,) + big[name][0]
        for dst, a in zip((grad_w, delta_w, new_m, new_v), results):
            if unit_rows:
                dst[name] = jnp.transpose(a, (1, 2, 0))
            else:
                dst[name] = jnp.swapaxes(a[None], 1, 2) if turned else a.reshape(full)
    for dst, packed in zip((grad_w, delta_w, new_m, new_v), outs):
        for name, a in _unpack_small(packed).items():
            dst[name] = a.reshape(w_loc[name].shape)

    return (loss, grad_x[None], *[grad_w[n] for n in names], *[delta_w[n] for n in names],
            *[new_m[n] for n in names], *[new_v[n] for n in names])
```

```python
import jax
import jax.numpy as jnp
from jax import lax
from jax.experimental import pallas as pl
from jax.experimental.pallas import tpu as pltpu

F32 = jnp.float32
BF16 = jnp.bfloat16

D_MODEL = 1024
N_META = 16
CHUNK = 128
META_PAD = CHUNK - N_META
D_INNER = 2048
HEAD_P = 64
SSM_HEADS = 32
SSM_GROUPS = 4
HEADS_PER_GROUP = SSM_HEADS // SSM_GROUPS
GROUP_W = HEADS_PER_GROUP * HEAD_P
D_STATE = 128
SSM_CONV = 4
CONV_DIM = D_INNER + 2 * SSM_GROUPS * D_STATE
ATTN_HEADS = 16
KV_HEADS = 4
ATTN_GROUP = ATTN_HEADS // KV_HEADS
DH = 64
KV_W = KV_HEADS * DH
FFN_DIM = 2816
FFN_CONV = 3
EPS = 1e-6
NEG = -1e30
N_DEV = 8
AXES = ("x", "y", "c")

OFF_Z, OFF_GATE, OFF_DT, OFF_Q, OFF_K, OFF_V, OFF_XBC = 0, 2048, 4096, 4608, 5632, 5888, 6144
N_INP = OFF_XBC + CONV_DIM
QKV_W = OFF_XBC - OFF_Q
CUT_Z, CUT_XBC, CUT_DT, CUT_Q, CUT_K, CUT_V, CUT_G = 0, 2048, 5120, 5152, 6176, 6432, 6688
N_IN = 8736

ADAM_LR, ADAM_B1, ADAM_B2, ADAM_EPS, ADAM_WD, ADAM_STEP = 0.001, 0.9, 0.999, 1e-08, 0.01, 10

VMEM_LIMIT = 56 * 1024 * 1024


def _params(n_grid):
    return pltpu.CompilerParams(dimension_semantics=("arbitrary",) * n_grid, vmem_limit_bytes=VMEM_LIMIT)


def _sds(shape, dtype):
    return jax.ShapeDtypeStruct(shape, dtype)


def _pick(n, prefs):
    for c in prefs:
        if n % c == 0:
            return c
    raise ValueError(f"no tile of {prefs} divides {n}")


def _row(tr, width, cb=0):
    return pl.BlockSpec((tr, width), lambda i: (i, cb))


def _row_rev(tr, width, nt, cb=0):
    return pl.BlockSpec((tr, width), lambda i: (nt - 1 - i, cb))


def _full(shape):
    return pl.BlockSpec(shape, lambda *_: (0,) * len(shape))


def _sigmoid(x):
    return 1.0 / (1.0 + jnp.exp(-x))


def _softplus(x):
    return jnp.maximum(x, 0.0) + jnp.log(1.0 + jnp.exp(-jnp.abs(x)))


def _rms(x):
    return lax.rsqrt(jnp.mean(x * x, axis=-1, keepdims=True) + EPS)


def _rms_bwd(x, r, w, dy):
    xh = x * r
    g = dy * w
    dx = r * (g - xh * jnp.mean(g * xh, axis=-1, keepdims=True))
    return dx, jnp.sum(dy * xh, axis=0, keepdims=True)


def _row_ids(shape, tile_index, tr):
    return tile_index * tr + lax.broadcasted_iota(jnp.int32, shape, 0)


HALO = 8
STRIP = 256
STRIP_BWD = 128


def _causal_taps(x, halo, first_step, taps):
    n = x.shape[0]

    @pl.when(first_step)
    def _():
        halo[...] = jnp.zeros_like(halo)

    before = halo[...]
    row = lax.broadcasted_iota(jnp.int32, before.shape, 0)
    shifted = [x]
    for s in range(1, taps):
        rolled = pltpu.roll(x, s, 0)
        head = jnp.where(row < s, pltpu.roll(before, s, 0), rolled[0:HALO, :])
        shifted.append(jnp.concatenate([head, rolled[HALO:, :]], axis=0))
    halo[...] = x[n - HALO:, :]
    return shifted


def _anticausal_taps(x, halo, first_step, taps):
    n = x.shape[0]

    @pl.when(first_step)
    def _():
        halo[...] = jnp.zeros_like(halo)

    after = halo[...]
    row = lax.broadcasted_iota(jnp.int32, after.shape, 0)
    shifted = [x]
    for s in range(1, taps):
        rolled = pltpu.roll(x, n - s, 0)
        tail = jnp.where(row >= HALO - s, pltpu.roll(after, HALO - s, 0), rolled[n - HALO:, :])
        shifted.append(jnp.concatenate([rolled[:n - HALO, :], tail], axis=0))
    halo[...] = x[0:HALO, :]
    return shifted


def _matmul(a, b, *, ta=False, tb=False, out_dtype=F32, name, after=None):
    if ta:
        k_dim, m_dim = a.shape
    else:
        m_dim, k_dim = a.shape
    n_dim = b.shape[0] if tb else b.shape[1]
    tm = _pick(m_dim, (1408, 1024, 768, 512, 384, 256, 128))
    tn = _pick(n_dim, (1024, 1408, 768, 512, 384, 256, 128))
    if ta:
        tk = _pick(k_dim, (1408, 1024, 768, 512, 384, 256, 128))
    else:
        tk = k_dim if k_dim <= 3072 else _pick(k_dim, (3072, 2816, 2048, 1024))
    nk = k_dim // tk
    dims = (((0 if ta else 1,), (1 if tb else 0,)), ((), ()))

    use_acc = nk > 1 and out_dtype != F32

    def body(a_ref, b_ref, *rest):
        o_ref = rest[-2] if use_acc else rest[-1]
        acc_ref = rest[-1] if use_acc else o_ref
        r = lax.dot_general(a_ref[...].astype(BF16), b_ref[...].astype(BF16), dims, preferred_element_type=F32)
        if nk == 1:
            o_ref[...] = r.astype(o_ref.dtype)
        else:
            k = pl.program_id(2)

            @pl.when(k == 0)
            def _():
                acc_ref[...] = r

            @pl.when(k > 0)
            def _():
                acc_ref[...] += r

            if use_acc:
                @pl.when(k == nk - 1)
                def _():
                    o_ref[...] = acc_ref[...].astype(o_ref.dtype)

    a_spec = pl.BlockSpec((tk, tm), lambda i, j, k: (k, i)) if ta else pl.BlockSpec((tm, tk), lambda i, j, k: (i, k))
    b_spec = pl.BlockSpec((tn, tk), lambda i, j, k: (j, k)) if tb else pl.BlockSpec((tk, tn), lambda i, j, k: (k, j))
    extra_specs, extra = ([], ()) if after is None else ([pl.BlockSpec(memory_space=pl.ANY)], (after,))
    return pl.pallas_call(
        body, grid=(m_dim // tm, n_dim // tn, nk), in_specs=[a_spec, b_spec] + extra_specs,
        out_specs=pl.BlockSpec((tm, tn), lambda i, j, k: (i, j)), out_shape=_sds((m_dim, n_dim), out_dtype),
        scratch_shapes=[pltpu.VMEM((tm, tn), F32)] if use_acc else [],
        name=name, compiler_params=_params(3))(a, b, *extra)


def _seq_rows(t_rows):
    return 384 if t_rows % 384 == 0 and t_rows >= 768 else CHUNK


def _token_rows(tr):
    if tr == CHUNK:
        return pl.BlockSpec((CHUNK, D_MODEL), lambda i: (jnp.maximum(i - 1, 0), 0))
    return pl.BlockSpec((pl.Element(tr), pl.Element(D_MODEL)),
                        lambda i: (pl.multiple_of(jnp.maximum(i * tr - CHUNK, 0), CHUNK), 0))


def _under_tile(rows_ref, head, i):
    rows = rows_ref[...]
    tr = rows.shape[0]
    first = head if tr == CHUNK else jnp.concatenate([head, rows[0:tr - CHUNK, :]], axis=0)
    return jnp.where(i == 0, first, rows)


def _seq_specs(tr=CHUNK):
    return [_token_rows(tr), _full((N_META, D_MODEL))]


def _seq_tile(x_ref, meta_ref, i):
    return _under_tile(x_ref, jnp.concatenate([jnp.zeros((META_PAD, D_MODEL), F32), meta_ref[...]], axis=0), i)


def _prenorm(x, meta, w):
    t_rows = x.shape[0] + CHUNK
    tr = _seq_rows(t_rows)

    def body(x_ref, meta_ref, w_ref, o_ref):
        h = _seq_tile(x_ref, meta_ref, pl.program_id(0))
        o_ref[...] = (h * _rms(h) * w_ref[...]).astype(BF16)

    return pl.pallas_call(body, grid=(t_rows // tr,), in_specs=_seq_specs(tr) + [_full((1, D_MODEL))],
                          out_specs=_row(tr, D_MODEL), out_shape=_sds((t_rows, D_MODEL), BF16),
                          name="prenorm", compiler_params=_params(1))(x, meta, w)


def _ssm_conv_fwd(proj, conv_w, conv_b):
    t_rows = proj.shape[0]
    tr = CHUNK

    def body(x_ref, w_ref, b_ref, xc_ref, xa_ref, hist):
        first = pl.program_id(0) == 0
        for c in range(0, CONV_DIM, STRIP):
            cols = slice(c, c + STRIP)
            acc = b_ref[:, cols]
            for s, moved in enumerate(_causal_taps(x_ref[:, cols], hist.at[:, cols], first, SSM_CONV)):
                acc = acc + w_ref[SSM_CONV - 1 - s:SSM_CONV - s, cols] * moved
            xc_ref[:, cols] = acc
            xa_ref[:, cols] = acc * _sigmoid(acc)

    return pl.pallas_call(
        body, grid=(t_rows // tr,),
        in_specs=[_row(tr, CONV_DIM, OFF_XBC // CONV_DIM), _full((SSM_CONV, CONV_DIM)), _full((1, CONV_DIM))],
        out_specs=[_row(tr, CONV_DIM), _row(tr, CONV_DIM)],
        out_shape=[_sds((t_rows, CONV_DIM), F32), _sds((t_rows, CONV_DIM), F32)],
        scratch_shapes=[pltpu.VMEM((HALO, CONV_DIM), F32)],
        name="ssm_conv_fwd", compiler_params=_params(1))(proj, conv_w, conv_b)


def _ssm_post(y, proj, w):
    t_rows = y.shape[0]
    tr = _pick(t_rows, (384, 128))

    def body(y_ref, z_ref, w_ref, o_ref):
        z = z_ref[...].astype(F32)
        yz = y_ref[...] * z * _sigmoid(z)
        o_ref[...] = (yz * _rms(yz) * w_ref[...]).astype(BF16)

    return pl.pallas_call(body, grid=(t_rows // tr,),
                          in_specs=[_row(tr, D_INNER), _row(tr, D_INNER, OFF_Z // D_INNER), _full((1, D_INNER))],
                          out_specs=_row(tr, D_INNER), out_shape=_sds((t_rows, D_INNER), BF16),
                          name="ssm_post", compiler_params=_params(1))(y, proj, w)


def _mix_fwd(proj, y_ssm, y_attn):
    t_rows = y_ssm.shape[0]
    tr = _pick(t_rows, (384, 128))

    def body(g_ref, ys_ref, ya_ref, o_ref):
        g = _sigmoid(g_ref[...].astype(F32))
        o_ref[...] = (g[:, :D_MODEL] * ys_ref[...] + g[:, D_MODEL:] * ya_ref[...]).astype(BF16)

    return pl.pallas_call(body, grid=(t_rows // tr,),
                          in_specs=[_row(tr, 2 * D_MODEL, OFF_GATE // (2 * D_MODEL)), _row(tr, D_MODEL),
                                    _row(tr, D_MODEL)],
                          out_specs=_row(tr, D_MODEL), out_shape=_sds((t_rows, D_MODEL), BF16),
                          name="mix_fwd", compiler_params=_params(1))(proj, y_ssm, y_attn)


def _postmix(x, meta, mix, w_post, w_pre):
    t_rows = mix.shape[0]
    tr = _seq_rows(t_rows)

    def body(x_ref, meta_ref, m_ref, wp_ref, wf_ref, h1_ref, hn_ref):
        m = m_ref[...]
        h1 = _seq_tile(x_ref, meta_ref, pl.program_id(0)) + m * _rms(m) * wp_ref[...]
        h1 = jnp.where(_row_ids(h1.shape, pl.program_id(0), tr) >= META_PAD, h1, 0.0)
        h1_ref[...] = h1
        hn_ref[...] = (h1 * _rms(h1) * wf_ref[...]).astype(BF16)

    return pl.pallas_call(body, grid=(t_rows // tr,),
                          in_specs=_seq_specs(tr) + [_row(tr, D_MODEL), _full((1, D_MODEL)), _full((1, D_MODEL))],
                          out_specs=[_row(tr, D_MODEL), _row(tr, D_MODEL)],
                          out_shape=[_sds((t_rows, D_MODEL), F32), _sds((t_rows, D_MODEL), BF16)],
                          name="postmix", compiler_params=_params(1))(x, meta, mix, w_post, w_pre)


def _ffn_act(up, conv_w, conv_b):
    t_rows = up.shape[0]
    tr = CHUNK
    width = 2 * FFN_DIM

    def body(up_ref, w_ref, b_ref, u_ref, act_ref, hist):
        first = pl.program_id(0) == 0
        for c in range(0, FFN_DIM, STRIP):
            halves = []
            for base in (0, FFN_DIM):
                cols = slice(base + c, base + c + STRIP)
                u = b_ref[:, cols]
                for s, moved in enumerate(_causal_taps(up_ref[:, cols].astype(F32), hist.at[:, cols], first, FFN_CONV)):
                    u = u + w_ref[FFN_CONV - 1 - s:FFN_CONV - s, cols] * moved
                u_ref[:, cols] = u.astype(BF16)
                halves.append(u)
            a, g = halves
            act_ref[:, c:c + STRIP] = (a * _sigmoid(a) * g).astype(BF16)

    return pl.pallas_call(
        body, grid=(t_rows // tr,), in_specs=[_row(tr, width), _full((FFN_CONV, width)), _full((1, width))],
        out_specs=[_row(tr, width), _row(tr, FFN_DIM)],
        out_shape=[_sds((t_rows, width), BF16), _sds((t_rows, FFN_DIM), BF16)],
        scratch_shapes=[pltpu.VMEM((HALO, width), F32)],
        name="ffn_act", compiler_params=_params(1))(up, conv_w, conv_b)


def _final(h1, f, target, w):
    t_rows = h1.shape[0]
    tr = _seq_rows(t_rows)

    def body(h1_ref, f_ref, t_ref, w_ref, df_ref, dy_ref, dw_ref, loss_ref):
        i = pl.program_id(0)

        @pl.when(i == 0)
        def _():
            dw_ref[...] = jnp.zeros_like(dw_ref)
            loss_ref[...] = jnp.zeros_like(loss_ref)

        f_val = f_ref[...]
        r = _rms(f_val)
        wv = w_ref[...]
        h2 = h1_ref[...] + f_val * r * wv
        tgt = _under_tile(t_ref, jnp.zeros((CHUNK, D_MODEL), F32), i)
        diff = jnp.where(_row_ids(h2.shape, i, tr) >= CHUNK, h2 - tgt, 0.0)
        loss_ref[...] += 0.5 * jnp.sum(diff * diff) * (1.0 / D_MODEL)
        dy = diff * (1.0 / D_MODEL)
        dy_ref[...] = dy
        df, dw = _rms_bwd(f_val, r, wv, dy)
        df_ref[...] = df.astype(BF16)
        dw_ref[...] += dw

    return pl.pallas_call(
        body, grid=(t_rows // tr,),
        in_specs=[_row(tr, D_MODEL), _row(tr, D_MODEL), _token_rows(tr), _full((1, D_MODEL))],
        out_specs=[_row(tr, D_MODEL), _row(tr, D_MODEL), _full((1, D_MODEL)), _full((1, 128))],
        out_shape=[_sds((t_rows, D_MODEL), BF16), _sds((t_rows, D_MODEL), F32), _sds((1, D_MODEL), F32), _sds((1, 128), F32)],
        name="final", compiler_params=_params(1))(h1, f, target, w)


def _ffn_act_bwd(u, up, dact, conv_w):
    t_rows = u.shape[0]
    tr = CHUNK
    nt = t_rows // tr
    width = 2 * FFN_DIM

    def body(u_ref, up_ref, da_ref, w_ref, dup_ref, dw_ref, db_ref, ahead):
        @pl.when(pl.program_id(0) == 0)
        def _():
            dw_ref[...] = jnp.zeros_like(dw_ref)
            db_ref[...] = jnp.zeros_like(db_ref)

        first = pl.program_id(0) == 0
        for c in range(0, FFN_DIM, STRIP_BWD):
            ca, cg = slice(c, c + STRIP_BWD), slice(FFN_DIM + c, FFN_DIM + c + STRIP_BWD)
            a, g, d = u_ref[:, ca].astype(F32), u_ref[:, cg].astype(F32), da_ref[:, ca].astype(F32)
            s = _sigmoid(a)
            for cols, du in ((ca, d * g * s * (1.0 + a * (1.0 - s))), (cg, d * a * s)):
                x = up_ref[:, cols].astype(F32)
                dup = None
                for sh, moved in enumerate(_anticausal_taps(du, ahead.at[:, cols], first, FFN_CONV)):
                    k = FFN_CONV - 1 - sh
                    term = w_ref[k:k + 1, cols] * moved
                    dup = term if dup is None else dup + term
                    dw_ref[k:k + 1, cols] += jnp.sum(moved * x, axis=0, keepdims=True)
                db_ref[:, cols] += jnp.sum(du, axis=0, keepdims=True)
                dup_ref[:, cols] = dup.astype(BF16)

    return pl.pallas_call(
        body, grid=(nt,),
        in_specs=[_row_rev(tr, width, nt), _row_rev(tr, width, nt), _row_rev(tr, FFN_DIM, nt), _full((FFN_CONV, width))],
        out_specs=[_row_rev(tr, width, nt), _full((FFN_CONV, width)), _full((1, width))],
        out_shape=[_sds((t_rows, width), BF16), _sds((FFN_CONV, width), F32), _sds((1, width), F32)],
        scratch_shapes=[pltpu.VMEM((HALO, width), F32)],
        name="ffn_act_bwd", compiler_params=_params(1))(u, up, dact, conv_w)


def _postmix_bwd(h1, dhn2, dy, mix, w_pre, w_post):
    t_rows = h1.shape[0]
    tr = _pick(t_rows, (384, 128))

    def body(h1_ref, dhn_ref, dy_ref, m_ref, wf_ref, wp_ref, dmix_ref, dh_ref, dwf_ref, dwp_ref):
        @pl.when(pl.program_id(0) == 0)
        def _():
            dwf_ref[...] = jnp.zeros_like(dwf_ref)
            dwp_ref[...] = jnp.zeros_like(dwp_ref)

        h1v = h1_ref[...]
        dx, dwf = _rms_bwd(h1v, _rms(h1v), wf_ref[...], dhn_ref[...])
        dwf_ref[...] += dwf
        dh1 = dy_ref[...] + dx
        dh1 = jnp.where(_row_ids(dh1.shape, pl.program_id(0), tr) >= META_PAD, dh1, 0.0)
        dh_ref[...] = dh1
        m = m_ref[...]
        dmix, dwp = _rms_bwd(m, _rms(m), wp_ref[...], dh1)
        dwp_ref[...] += dwp
        dmix_ref[...] = dmix.astype(BF16)

    return pl.pallas_call(
        body, grid=(t_rows // tr,),
        in_specs=[_row(tr, D_MODEL) for _ in range(4)] + [_full((1, D_MODEL))] * 2,
        out_specs=[_row(tr, D_MODEL), _row(tr, D_MODEL), _full((1, D_MODEL)), _full((1, D_MODEL))],
        out_shape=[_sds((t_rows, D_MODEL), BF16), _sds((t_rows, D_MODEL), F32), _sds((1, D_MODEL), F32), _sds((1, D_MODEL), F32)],
        name="postmix_bwd", compiler_params=_params(1))(h1, dhn2, dy, mix, w_pre, w_post)


_ANY = pl.BlockSpec(memory_space=pl.ANY)


def _mix_bwd(dmixed, proj, y_ssm, y_attn, dproj):
    t_rows = dmixed.shape[0]
    tr = _pick(t_rows, (384, 128))

    def body(d_ref, g_ref, ys_ref, ya_ref, _, dys_ref, dya_ref, dg_ref):
        d = d_ref[...]
        g = _sigmoid(g_ref[...].astype(F32))
        g1, g2 = g[:, :D_MODEL], g[:, D_MODEL:]
        dys_ref[...] = (d * g1).astype(BF16)
        dya_ref[...] = (d * g2).astype(BF16)
        dg_ref[...] = jnp.concatenate([d * ys_ref[...] * g1 * (1.0 - g1), d * ya_ref[...] * g2 * (1.0 - g2)],
                                      axis=1).astype(BF16)

    return pl.pallas_call(
        body, grid=(t_rows // tr,),
        in_specs=[_row(tr, D_MODEL), _row(tr, 2 * D_MODEL, OFF_GATE // (2 * D_MODEL)), _row(tr, D_MODEL), _row(tr, D_MODEL),
                  _ANY],
        out_specs=[_row(tr, D_MODEL), _row(tr, D_MODEL), _row(tr, 2 * D_MODEL, OFF_GATE // (2 * D_MODEL))],
        out_shape=[_sds((t_rows, D_MODEL), BF16), _sds((t_rows, D_MODEL), BF16), _sds(dproj.shape, dproj.dtype)],
        input_output_aliases={4: 2},
        name="mix_bwd", compiler_params=_params(1))(dmixed, proj, y_ssm, y_attn, dproj)


def _ssm_post_bwd(y, proj, dyn, w, dproj):
    t_rows = y.shape[0]
    tr = CHUNK

    def body(y_ref, z_ref, d_ref, w_ref, _, dy_ref, dz_ref, dw_ref):
        @pl.when(pl.program_id(0) == 0)
        def _():
            dw_ref[...] = jnp.zeros_like(dw_ref)

        yv, z = y_ref[...], z_ref[...].astype(F32)
        sz = _sigmoid(z)
        silu = z * sz
        yz = yv * silu
        dyz, dw = _rms_bwd(yz, _rms(yz), w_ref[...], d_ref[...].astype(F32))
        dw_ref[...] += dw
        dy_ref[...] = dyz * silu
        dz_ref[...] = (dyz * yv * sz * (1.0 + z * (1.0 - sz))).astype(BF16)

    return pl.pallas_call(
        body, grid=(t_rows // tr,),
        in_specs=[_row(tr, D_INNER), _row(tr, D_INNER, OFF_Z // D_INNER), _row(tr, D_INNER), _full((1, D_INNER)), _ANY],
        out_specs=[_row(tr, D_INNER), _row(tr, D_INNER, OFF_Z // D_INNER), _full((1, D_INNER))],
        out_shape=[_sds((t_rows, D_INNER), F32), _sds(dproj.shape, dproj.dtype), _sds((1, D_INNER), F32)],
        input_output_aliases={4: 1},
        name="ssm_post_bwd", compiler_params=_params(1))(y, proj, dyn, w, dproj)


def _ssm_conv_bwd(xc, proj, dxs, dbm, dcm, conv_w, dproj):
    t_rows = xc.shape[0]
    tr = CHUNK
    nt = t_rows // tr
    bc_w = SSM_GROUPS * D_STATE

    def body(xc_ref, x_ref, dxs_ref, db_ref, dc_ref, w_ref, _, dx_ref, dw_ref, dbias_ref, ahead):
        first = pl.program_id(0) == 0

        @pl.when(first)
        def _():
            dw_ref[...] = jnp.zeros_like(dw_ref)
            dbias_ref[...] = jnp.zeros_like(dbias_ref)

        for c0 in range(0, CONV_DIM, STRIP_BWD):
            cols = slice(c0, c0 + STRIP_BWD)
            if c0 < D_INNER:
                dact = dxs_ref[:, cols]
            elif c0 < D_INNER + bc_w:
                dact = db_ref[:, c0 - D_INNER:c0 - D_INNER + STRIP_BWD]
            else:
                dact = dc_ref[:, c0 - D_INNER - bc_w:c0 - D_INNER - bc_w + STRIP_BWD]
            c = xc_ref[:, cols]
            s = _sigmoid(c)
            dpre = dact * s * (1.0 + c * (1.0 - s))
            x = x_ref[:, cols]
            dx = None
            for sh, moved in enumerate(_anticausal_taps(dpre, ahead.at[:, cols], first, SSM_CONV)):
                k = SSM_CONV - 1 - sh
                term = w_ref[k:k + 1, cols] * moved
                dx = term if dx is None else dx + term
                dw_ref[k:k + 1, cols] += jnp.sum(moved * x, axis=0, keepdims=True)
            dbias_ref[:, cols] += jnp.sum(dpre, axis=0, keepdims=True)
            dx_ref[:, cols] = dx.astype(BF16)

    xbc_block = OFF_XBC // CONV_DIM
    return pl.pallas_call(
        body, grid=(nt,),
        in_specs=[_row_rev(tr, CONV_DIM, nt), _row_rev(tr, CONV_DIM, nt, xbc_block), _row_rev(tr, D_INNER, nt),
                  _row_rev(tr, bc_w, nt), _row_rev(tr, bc_w, nt), _full((SSM_CONV, CONV_DIM)), _ANY],
        out_specs=[_row_rev(tr, CONV_DIM, nt, xbc_block), _full((SSM_CONV, CONV_DIM)), _full((1, CONV_DIM))],
        out_shape=[_sds(dproj.shape, dproj.dtype), _sds((SSM_CONV, CONV_DIM), F32), _sds((1, CONV_DIM), F32)],
        scratch_shapes=[pltpu.VMEM((HALO, CONV_DIM), F32)],
        input_output_aliases={6: 0},
        name="ssm_conv_bwd", compiler_params=_params(1))(xc, proj, dxs, dbm, dcm, conv_w, dproj)


def _prenorm_bwd(x, meta, dhn, dh, w):
    seq = x.shape[0]
    tr = _pick(seq, (512, 128))

    def body(x_ref, meta_ref, d_ref, r_ref, d0_ref, r0_ref, w_ref, dx_ref, dmeta_ref, dw_ref):
        wv = w_ref[...]

        @pl.when(pl.program_id(0) == 0)
        def _():
            h0 = jnp.concatenate([jnp.zeros((META_PAD, D_MODEL), F32), meta_ref[...]], axis=0)
            dx0, dw0 = _rms_bwd(h0, _rms(h0), wv, d0_ref[...])
            dw_ref[...] = dw0
            dmeta_ref[...] = (r0_ref[...] + dx0)[META_PAD:, :]

        h = x_ref[...]
        dx, dw = _rms_bwd(h, _rms(h), wv, d_ref[...])
        dw_ref[...] += dw
        dx_ref[...] = r_ref[...] + dx

    def shifted(tile):
        return pl.BlockSpec((pl.Element(tile), pl.Element(D_MODEL)), lambda i: (pl.multiple_of(i * tile + CHUNK, CHUNK), 0))

    first = pl.BlockSpec((CHUNK, D_MODEL), lambda i: (0, 0))
    return pl.pallas_call(
        body, grid=(seq // tr,),
        in_specs=[_row(tr, D_MODEL), _full((N_META, D_MODEL)), shifted(tr), shifted(tr), first, first, _full((1, D_MODEL))],
        out_specs=[_row(tr, D_MODEL), _full((N_META, D_MODEL)), _full((1, D_MODEL))],
        out_shape=[_sds((seq, D_MODEL), F32), _sds((N_META, D_MODEL), F32), _sds((1, D_MODEL), F32)],
        name="prenorm_bwd", compiler_params=_params(1))(x, meta, dhn, dh, dhn, dh, w)


def _dot01(x, m01, x_left, parts):
    acc, rest = None, x
    for i in range(parts):
        piece = rest.astype(BF16)
        term = (jnp.dot(piece, m01, preferred_element_type=F32) if x_left
                else jnp.dot(m01, piece, preferred_element_type=F32))
        acc = term if acc is None else acc + term
        if i + 1 < parts:
            rest = rest - piece.astype(F32)
    return acc


def _ssd_common(dt_raw, dt_bias, a_log, chunk_index):
    rows = lax.broadcasted_iota(jnp.int32, (CHUNK, CHUNK), 0)
    cols = lax.broadcasted_iota(jnp.int32, (CHUNK, CHUNK), 1)
    low = rows >= cols
    raw = dt_raw + dt_bias
    live = _row_ids(raw.shape, chunk_index, CHUNK) >= META_PAD
    dt = jnp.where(live, _softplus(raw), 0.0)
    a_head = -jnp.exp(a_log)
    cs = _dot01(dt * a_head, low.astype(BF16), False, 3)
    grow = jnp.exp(cs)
    fade = jnp.exp(cs[CHUNK - 1:CHUNK, :] - cs)
    expand = (lax.broadcasted_iota(jnp.int32, (CHUNK, GROUP_W), 1) // HEAD_P
              == lax.broadcasted_iota(jnp.int32, (CHUNK, GROUP_W), 0)).astype(BF16)
    fold = (lax.broadcasted_iota(jnp.int32, (GROUP_W, CHUNK), 0) // HEAD_P
            == lax.broadcasted_iota(jnp.int32, (GROUP_W, CHUNK), 1)).astype(BF16)
    return dict(low=low, triu=(rows <= cols).astype(BF16), raw=raw, live=live, dt=dt, a_head=a_head, cs=cs, cs_t=cs.T,
                fold=fold, dtx=_dot01(dt, expand, True, 2), growx=_dot01(grow, expand, True, 2),
                fadex=_dot01(fade, expand, True, 2))


def _decay_matrix(cm, j):
    diff = cm["cs"][:, j:j + 1] - cm["cs_t"][j:j + 1, :]
    return jnp.where(cm["low"], jnp.exp(jnp.where(cm["low"], diff, 0.0)), 0.0)


def _dot(a, b, dims):
    return lax.dot_general(a.astype(BF16), b.astype(BF16), (dims, ((), ())), preferred_element_type=F32)


def _dot_fine(a, b, dims):
    a_hi, b_hi = a.astype(BF16), b.astype(BF16)
    a_lo, b_lo = (a - a_hi.astype(F32)).astype(BF16), (b - b_hi.astype(F32)).astype(BF16)
    dn = (dims, ((), ()))
    return (lax.dot_general(a_hi, b_hi, dn, preferred_element_type=F32)
            + lax.dot_general(a_hi, b_lo, dn, preferred_element_type=F32)
            + lax.dot_general(a_lo, b_hi, dn, preferred_element_type=F32))


def _ssd_specs(nt, rev):
    def idx(c):
        return nt - 1 - c if rev else c
    bc_w = SSM_GROUPS * D_STATE
    xs = pl.BlockSpec((CHUNK, D_INNER), lambda c: (idx(c), 0))
    bm = pl.BlockSpec((CHUNK, bc_w), lambda c: (idx(c), D_INNER // bc_w))
    cm = pl.BlockSpec((CHUNK, bc_w), lambda c: (idx(c), D_INNER // bc_w + 1))
    dtr = pl.BlockSpec((CHUNK, SSM_GROUPS * 128), lambda c: (idx(c), OFF_DT // (SSM_GROUPS * 128)))
    par = _full((SSM_GROUPS, 1, 128))
    par_x = _full((SSM_GROUPS, 1, GROUP_W))
    return xs, bm, cm, dtr, par, par_x, idx


def _group_cols(g, width):
    return slice(g * width, (g + 1) * width)


def _ssd_fwd(xact, proj, dtb, alog, dskip_x):
    t_rows = xact.shape[0]
    nt = t_rows // CHUNK
    xs_spec, b_spec, c_spec, dtr_spec, par, par_x, _ = _ssd_specs(nt, False)

    def body(xs_ref, b_ref, c_ref, dtr_ref, dtb_ref, alog_ref, dsk_ref, y_ref, hst_ref, state):
        c = pl.program_id(0)

        @pl.when(c == 0)
        def _():
            state[...] = jnp.zeros_like(state)

        for g in range(SSM_GROUPS):
            wide, narrow = _group_cols(g, GROUP_W), _group_cols(g, D_STATE)
            cm = _ssd_common(dtr_ref[:, narrow], dtb_ref[g], alog_ref[g], c)
            xs, bm, cmat = xs_ref[:, wide], b_ref[:, narrow], c_ref[:, narrow]
            x_dt = xs * cm["dtx"]
            h_in = state[g]
            hst_ref[0, g] = h_in
            y_ref[:, wide] = _dot(cmat, h_in, ((1,), (0,))) * cm["growx"] + xs * dsk_ref[g]
            cb = _dot(cmat, bm, ((1,), (1,)))
            for j in range(HEADS_PER_GROUP):
                sl = slice(g * GROUP_W + j * HEAD_P, g * GROUP_W + (j + 1) * HEAD_P)
                y_ref[:, sl] += _dot(cb * _decay_matrix(cm, j), x_dt[:, j * HEAD_P:(j + 1) * HEAD_P], ((1,), (0,)))
            state[g] = h_in * cm["growx"][CHUNK - 1:CHUNK, :] + _dot_fine(bm, x_dt * cm["fadex"], ((0,), (0,)))

    return pl.pallas_call(
        body, grid=(nt,),
        in_specs=[xs_spec, b_spec, c_spec, dtr_spec, par, par, par_x],
        out_specs=[xs_spec, pl.BlockSpec((1, SSM_GROUPS, D_STATE, GROUP_W), lambda c: (c, 0, 0, 0))],
        out_shape=[_sds((t_rows, D_INNER), F32), _sds((nt, SSM_GROUPS, D_STATE, GROUP_W), F32)],
        scratch_shapes=[pltpu.VMEM((SSM_GROUPS, D_STATE, GROUP_W), F32)],
        name="ssd_fwd", compiler_params=_params(1))(xact, xact, xact, proj, dtb, alog, dskip_x)


def _ssd_bwd(xact, proj, dtb, alog, dskip_x, dy, hst, dproj):
    t_rows = xact.shape[0]
    nt = t_rows // CHUNK
    xs_spec, b_spec, c_spec, dtr_spec, par, par_x, idx = _ssd_specs(nt, True)
    h_spec = pl.BlockSpec((1, SSM_GROUPS, D_STATE, GROUP_W), lambda c: (idx(c), 0, 0, 0))
    hn_spec = pl.BlockSpec((1, SSM_GROUPS, D_STATE, GROUP_W), lambda c: (jnp.minimum(idx(c) + 1, nt - 1), 0, 0, 0))
    bc_out = pl.BlockSpec((CHUNK, SSM_GROUPS * D_STATE), lambda c: (idx(c), 0))

    def body(xs_ref, b_ref, c_ref, dtr_ref, dtb_ref, alog_ref, dsk_ref, dy_ref, h_ref, hn_ref, _,
             dxs_ref, db_ref, dc_ref, ddt_ref, dalog_ref, ddtb_ref, dd_ref, dstate, dx_buf):
        step = pl.program_id(0)

        @pl.when(step == 0)
        def _():
            dstate[...] = jnp.zeros_like(dstate)
            dalog_ref[...] = jnp.zeros_like(dalog_ref)
            ddtb_ref[...] = jnp.zeros_like(ddtb_ref)
            dd_ref[...] = jnp.zeros_like(dd_ref)

        for g in range(SSM_GROUPS):
            _ssd_bwd_group(g, idx(step), xs_ref, b_ref, c_ref, dtr_ref, dtb_ref, alog_ref, dsk_ref, dy_ref, h_ref, hn_ref,
                           dxs_ref, db_ref, dc_ref, ddt_ref, dalog_ref, ddtb_ref, dd_ref, dstate, dx_buf)

    return pl.pallas_call(
        body, grid=(nt,),
        in_specs=[xs_spec, b_spec, c_spec, dtr_spec, par, par, par_x, xs_spec, h_spec, hn_spec, _ANY],
        out_specs=[xs_spec, bc_out, bc_out, dtr_spec, par, par, par_x],
        out_shape=[_sds((t_rows, D_INNER), F32), _sds((t_rows, SSM_GROUPS * D_STATE), F32),
                   _sds((t_rows, SSM_GROUPS * D_STATE), F32), _sds(dproj.shape, dproj.dtype),
                   _sds((SSM_GROUPS, 1, 128), F32), _sds((SSM_GROUPS, 1, 128), F32), _sds((SSM_GROUPS, 1, GROUP_W), F32)],
        scratch_shapes=[pltpu.VMEM((SSM_GROUPS, D_STATE, GROUP_W), F32), pltpu.VMEM((CHUNK, GROUP_W), F32)],
        input_output_aliases={10: 3},
        name="ssd_bwd", compiler_params=_params(1))(xact, xact, xact, proj, dtb, alog, dskip_x, dy, hst, hst, dproj)


def _ssd_bwd_group(g, chunk, xs_ref, b_ref, c_ref, dtr_ref, dtb_ref, alog_ref, dsk_ref, dy_ref, h_ref, hn_ref,
                   dxs_ref, db_ref, dc_ref, ddt_ref, dalog_ref, ddtb_ref, dd_ref, dstate, dx_buf):
    wide, narrow = _group_cols(g, GROUP_W), _group_cols(g, D_STATE)
    cm = _ssd_common(dtr_ref[:, narrow], dtb_ref[g], alog_ref[g], chunk)
    xs, bm, cmat = xs_ref[:, wide], b_ref[:, narrow], c_ref[:, narrow]
    dsk = dsk_ref[g]
    x_dt = xs * cm["dtx"]
    h_in, h_next = h_ref[0, g], hn_ref[0, g]
    dyv = dy_ref[:, wide]
    dh = dstate[g]
    grow, fade = cm["growx"], cm["fadex"]
    dy_grow = dyv * grow
    x_fade = x_dt * fade
    cb = _dot(cmat, bm, ((1,), (1,)))
    ml = jnp.zeros((CHUNK, CHUNK), F32)
    row_id = lax.broadcasted_iota(jnp.int32, (CHUNK, CHUNK), 0)
    col_id = lax.broadcasted_iota(jnp.int32, (CHUNK, CHUNK), 1)
    w_rows = jnp.zeros((CHUNK, CHUNK), F32)
    w_cols = jnp.zeros((CHUNK, CHUNK), F32)
    for j in range(HEADS_PER_GROUP):
        sl = slice(j * HEAD_P, (j + 1) * HEAD_P)
        lm = _decay_matrix(cm, j)
        mlj = _dot(dyv[:, sl], x_dt[:, sl], ((1,), (1,))) * lm
        ml = ml + mlj
        wm = mlj * cb
        w_rows = jnp.where(col_id == j, jnp.sum(wm, axis=1, keepdims=True), w_rows)
        w_cols = jnp.where(row_id == j, jnp.sum(wm, axis=0, keepdims=True), w_cols)
        dx_buf[:, sl] = _dot(cb * lm, dyv[:, sl], ((0,), (0,)))
    dx_off = fade * _dot_fine(bm, dh, ((1,), (0,)))
    dx = dx_buf[...] + dx_off
    dc_ref[:, narrow] = _dot(ml, bm, ((1,), (0,))) + _dot(dy_grow, h_in, ((1,), (1,)))
    db_ref[:, narrow] = _dot(ml, cmat, ((0,), (0,))) + _dot(x_fade, dh, ((1,), (1,)))
    fold = cm["fold"]
    y_off = _dot_fine(cmat, h_in, ((1,), (0,))) * grow
    dcs = (w_rows - w_cols.T) + _dot01(dyv * y_off - x_dt * dx_off, fold, True, 2)
    tail = jnp.broadcast_to(jnp.sum(dh * h_next, axis=0, keepdims=True), (8, GROUP_W))
    tail = _dot01(tail, fold, True, 2)[0:1, :]
    last_row = lax.broadcasted_iota(jnp.int32, (CHUNK, 128), 0) == CHUNK - 1
    dcs = dcs + jnp.where(last_row, tail, 0.0)
    da = _dot01(dcs, cm["triu"], False, 3)
    ddt = da * cm["a_head"] + _dot01(dx * xs, fold, True, 2)
    ddt_raw = jnp.where(cm["live"], ddt * _sigmoid(cm["raw"]), 0.0)
    ddt_ref[:, narrow] = ddt_raw.astype(BF16)
    ddtb_ref[g] += jnp.sum(ddt_raw, axis=0, keepdims=True)
    dalog_ref[g] += jnp.sum(da * cm["dt"], axis=0, keepdims=True) * cm["a_head"]
    dd_ref[g] += jnp.sum(dyv * xs, axis=0, keepdims=True)
    dxs_ref[:, wide] = dx * cm["dtx"] + dyv * dsk
    dstate[g] = dh * grow[CHUNK - 1:CHUNK, :] + _dot_fine(cmat, dy_grow, ((0,), (0,)))


def _swa_bias():
    rows_q = ATTN_GROUP * CHUNK
    dist = (jnp.arange(rows_q) % CHUNK)[:, None] - jnp.arange(2 * CHUNK)[None, :] + CHUNK
    head = jnp.arange(KV_HEADS)[:, None] * ATTN_GROUP + jnp.arange(rows_q)[None, :] // CHUNK + 1
    slope = jnp.exp2(-8.0 * head.astype(F32) / ATTN_HEADS)
    return jnp.where((dist >= 0) & (dist < CHUNK), -slope[:, :, None] * dist.astype(F32)[None], NEG)


def _swa_probs(q_kv, k_prev, k_cur, k_first, sink, bias, n):
    rows_q = ATTN_GROUP * CHUNK
    qs = jnp.concatenate([q_kv[:, g * DH:(g + 1) * DH] for g in range(ATTN_GROUP)], axis=0) * (DH ** -0.5)
    kcat = jnp.concatenate([k_prev, k_cur], axis=0)
    kmeta = k_first[META_PAD:, :]
    key_ok = lax.broadcasted_iota(jnp.int32, (1, 2 * CHUNK), 1) + n * CHUNK >= 2 * CHUNK
    s_band = jnp.where(key_ok, _dot(qs, kcat, ((1,), (1,))) + bias, NEG)
    q_pos = lax.broadcasted_iota(jnp.int32, (rows_q, N_META), 0) % CHUNK + n * CHUNK - META_PAD
    ok_m = lax.broadcasted_iota(jnp.int32, (rows_q, N_META), 1) <= q_pos
    s_meta = jnp.where(ok_m, _dot(qs, kmeta, ((1,), (1,))), NEG)
    m = jnp.maximum(jnp.maximum(jnp.max(s_band, axis=1, keepdims=True), jnp.max(s_meta, axis=1, keepdims=True)), sink)
    p_band, p_meta, p_sink = jnp.exp(s_band - m), jnp.exp(s_meta - m), jnp.exp(sink - m)
    inv = 1.0 / (jnp.sum(p_band, axis=1, keepdims=True) + jnp.sum(p_meta, axis=1, keepdims=True) + p_sink)
    return qs, kcat, kmeta, p_band * inv, p_meta * inv, p_sink * inv


def _swa_specs(nt, rev):
    def idx(n):
        return nt - 1 - n if rev else n
    o = pl.BlockSpec((CHUNK, ATTN_HEADS * DH), lambda n: (idx(n), 0))
    chunks = (lambda c: jnp.maximum(c - 1, 0)), (lambda c: c), (lambda c: 0)
    qkv = [pl.BlockSpec((CHUNK, QKV_W), lambda n, f=f: (f(idx(n)), OFF_Q // QKV_W)) for f in chunks]
    sink = _full((KV_HEADS, ATTN_GROUP * CHUNK, 1))
    bias = _full((KV_HEADS, ATTN_GROUP * CHUNK, 2 * CHUNK))
    return o, qkv, sink, bias, idx


def _head_cols(k):
    kv_w = ATTN_GROUP * DH
    q0, k0, v0 = k * kv_w, OFF_K - OFF_Q + k * DH, OFF_V - OFF_Q + k * DH
    return slice(q0, q0 + kv_w), slice(k0, k0 + DH), slice(v0, v0 + DH)


def _swa_fwd(proj, sink_rows, bias):
    t_rows = proj.shape[0]
    nt = t_rows // CHUNK
    o_spec, qkv_specs, sink_spec, bias_spec, _ = _swa_specs(nt, False)
    kv_w = ATTN_GROUP * DH

    def body(prev_ref, cur_ref, first_ref, sink_ref, bias_ref, o_ref):
        n = pl.program_id(0)
        for k in range(KV_HEADS):
            qc, kc, vc = _head_cols(k)
            _, _, _, p_band, p_meta, _ = _swa_probs(cur_ref[:, qc], prev_ref[:, kc], cur_ref[:, kc], first_ref[:, kc],
                                                    sink_ref[k], bias_ref[k], n)
            vcat = jnp.concatenate([prev_ref[:, vc], cur_ref[:, vc]], axis=0)
            out = _dot(p_band, vcat, ((1,), (0,))) + _dot(p_meta, first_ref[:, vc][META_PAD:, :], ((1,), (0,)))
            for g in range(ATTN_GROUP):
                o_ref[:, k * kv_w + g * DH:k * kv_w + (g + 1) * DH] = out[g * CHUNK:(g + 1) * CHUNK, :]

    return pl.pallas_call(
        body, grid=(nt,), in_specs=qkv_specs + [sink_spec, bias_spec],
        out_specs=o_spec, out_shape=_sds((t_rows, ATTN_HEADS * DH), F32),
        name="swa_fwd", compiler_params=_params(1))(proj, proj, proj, sink_rows, bias)


def _swa_bwd(proj, sink_rows, bias, out, dout, dproj):
    t_rows = proj.shape[0]
    nt = t_rows // CHUNK
    o_spec, qkv_specs, sink_spec, bias_spec, idx = _swa_specs(nt, True)
    kv_w = ATTN_GROUP * DH
    k_off, v_off = OFF_K - OFF_Q, OFF_V - OFF_Q

    def body(prev_ref, cur_ref, first_ref, sink_ref, bias_ref, o_ref, do_ref, _, dqkv_ref, dsink_ref,
             carry_k, carry_v, meta_k, meta_v, dqkv_buf):
        step = pl.program_id(0)
        n = idx(step)

        @pl.when(step == 0)
        def _():
            carry_k[...] = jnp.zeros_like(carry_k)
            carry_v[...] = jnp.zeros_like(carry_v)
            meta_k[...] = jnp.zeros_like(meta_k)
            meta_v[...] = jnp.zeros_like(meta_v)
            dsink_ref[...] = jnp.zeros_like(dsink_ref)

        for k in range(KV_HEADS):
            cols = slice(k * kv_w, (k + 1) * kv_w)
            hd = slice(k * DH, (k + 1) * DH)
            qc, kc, vc = _head_cols(k)
            qs, kcat, kmeta, p_band, p_meta, p_sink = _swa_probs(cur_ref[:, qc], prev_ref[:, kc], cur_ref[:, kc],
                                                                 first_ref[:, kc], sink_ref[k], bias_ref[k], n)
            vcat = jnp.concatenate([prev_ref[:, vc], cur_ref[:, vc]], axis=0)
            vmeta = first_ref[:, vc][META_PAD:, :]
            o, do = o_ref[:, cols], do_ref[:, cols]
            os_ = jnp.concatenate([o[:, g * DH:(g + 1) * DH] for g in range(ATTN_GROUP)], axis=0)
            dos = jnp.concatenate([do[:, g * DH:(g + 1) * DH] for g in range(ATTN_GROUP)], axis=0)
            delta = jnp.sum(dos * os_, axis=1, keepdims=True)
            ds_band = p_band * (_dot(dos, vcat, ((1,), (1,))) - delta)
            ds_meta = p_meta * (_dot(dos, vmeta, ((1,), (1,))) - delta)
            ds_sink = -p_sink * delta
            dqs = (_dot(ds_band, kcat, ((1,), (0,))) + _dot(ds_meta, kmeta, ((1,), (0,)))) * (DH ** -0.5)
            for g in range(ATTN_GROUP):
                dqkv_buf[:, k * kv_w + g * DH:k * kv_w + (g + 1) * DH] = dqs[g * CHUNK:(g + 1) * CHUNK, :]
                dsink_ref[k, g:g + 1, :] += jnp.sum(ds_sink[g * CHUNK:(g + 1) * CHUNK, :])
            dkcat = _dot(ds_band, qs, ((0,), (0,)))
            dvcat = _dot(p_band, dos, ((0,), (0,)))
            meta_k[:, hd] += _dot(ds_meta, qs, ((0,), (0,)))
            meta_v[:, hd] += _dot(p_meta, dos, ((0,), (0,)))
            dqkv_buf[:, kc] = dkcat[CHUNK:, :] + carry_k[:, hd]
            dqkv_buf[:, vc] = dvcat[CHUNK:, :] + carry_v[:, hd]
            carry_k[:, hd] = dkcat[:CHUNK, :]
            carry_v[:, hd] = dvcat[:CHUNK, :]

        @pl.when(n == 0)
        def _():
            dqkv_buf[META_PAD:, k_off:k_off + KV_W] += meta_k[...]
            dqkv_buf[META_PAD:, v_off:v_off + KV_W] += meta_v[...]

        dqkv_ref[...] = dqkv_buf[...].astype(BF16)

    return pl.pallas_call(
        body, grid=(nt,),
        in_specs=qkv_specs + [sink_spec, bias_spec, o_spec, o_spec, pl.BlockSpec(memory_space=pl.ANY)],
        out_specs=[qkv_specs[1], _full((KV_HEADS, 8, 128))],
        out_shape=[_sds(dproj.shape, dproj.dtype), _sds((KV_HEADS, 8, 128), F32)],
        scratch_shapes=[pltpu.VMEM((CHUNK, KV_W), F32), pltpu.VMEM((CHUNK, KV_W), F32),
                        pltpu.VMEM((N_META, KV_W), F32), pltpu.VMEM((N_META, KV_W), F32),
                        pltpu.VMEM((CHUNK, QKV_W), F32)],
        input_output_aliases={7: 0},
        name="swa_bwd", compiler_params=_params(1))(proj, proj, proj, sink_rows, bias, out, dout, dproj)


def _pack_w_in_t(w_in_t):
    w_dt = w_in_t[CUT_DT:CUT_Q].reshape(SSM_GROUPS, HEADS_PER_GROUP, D_MODEL)
    w_dt = jnp.pad(w_dt, ((0, 0), (0, 128 - HEADS_PER_GROUP), (0, 0))).reshape(SSM_GROUPS * 128, D_MODEL)
    return jnp.concatenate([w_in_t[CUT_Z:CUT_XBC], w_in_t[CUT_G:], w_dt, w_in_t[CUT_Q:CUT_G], w_in_t[CUT_XBC:CUT_DT]], axis=0)


def _unpack_w_in_t(wp_t):
    w_dt = wp_t[OFF_DT:OFF_Q].reshape(SSM_GROUPS, 128, D_MODEL)[:, :HEADS_PER_GROUP].reshape(SSM_HEADS, D_MODEL)
    return jnp.concatenate([wp_t[OFF_Z:OFF_GATE], wp_t[OFF_XBC:], w_dt, wp_t[OFF_Q:OFF_XBC], wp_t[OFF_GATE:OFF_DT]], axis=0)


def _group_rows(v, width):
    return jnp.pad(v.reshape(SSM_GROUPS, 1, HEADS_PER_GROUP), ((0, 0), (0, 0), (0, width - HEADS_PER_GROUP)))


def _local_step(x, target, wt, late_weights=None, on_grad=None, started=None):
    seq = x.shape[0]
    grads = {}

    def emit(name, g):
        grads[name] = g
        return None if on_grad is None else on_grad(name, g)
    meta = wt["meta_tokens"]
    wp_t = _pack_w_in_t(wt["w_in_t"])
    dtb = _group_rows(wt["ssm_dt_bias"].reshape(-1), 128)
    alog = _group_rows(wt["ssm_a_log"].reshape(-1), 128)
    dskip_x = jnp.repeat(wt["ssm_d_skip"].reshape(-1), HEAD_P).reshape(SSM_GROUPS, 1, GROUP_W)
    sink_rows = jnp.repeat(wt["attn_sinks"].reshape(KV_HEADS, ATTN_GROUP), CHUNK, axis=1).reshape(KV_HEADS, ATTN_GROUP * CHUNK, 1)

    hn = _prenorm(x, meta, wt["norm_pre_mix"])
    proj = _matmul(hn, wp_t, tb=True, name="in_proj", after=started)
    xc, xact = _ssm_conv_fwd(proj, wt["ssm_conv_w"], wt["ssm_conv_b"])
    y, hst = _ssd_fwd(xact, proj, dtb, alog, dskip_x)
    yn = _ssm_post(y, proj, wt["ssm_norm"])
    if late_weights is not None:
        wt = {**wt, **late_weights(yn)}
    y_ssm = _matmul(yn, wt["w_ssm_out"], name="ssm_out")
    bias = _swa_bias()
    attn = _swa_fwd(proj, sink_rows, bias)
    y_attn = _matmul(attn, wt["w_attn_out"], name="attn_out")
    mixed = _mix_fwd(proj, y_ssm, y_attn)
    mix = _matmul(mixed, wt["w_mix_out"], name="mix_out")
    h1, hn2 = _postmix(x, meta, mix, wt["norm_post_mix"], wt["norm_pre_ffn"])
    up = _matmul(hn2, wt["w_ffn_up_t"], tb=True, out_dtype=BF16, name="ffn_up")
    u, act = _ffn_act(up, wt["ffn_conv_w"], wt["ffn_conv_b"])
    f = _matmul(act, wt["w_ffn_down"], name="ffn_down")
    df, dy, g_norm_post_ffn, loss_row = _final(h1, f, target, wt["norm_post_ffn"])

    grads["norm_post_ffn"] = g_norm_post_ffn
    sent = emit("w_ffn_down", _matmul(act, df, ta=True, out_dtype=BF16, name="dw_ffn_down"))
    dact = _matmul(df, wt["w_ffn_down"], tb=True, out_dtype=BF16, name="d_act", after=sent)
    dup, grads["ffn_conv_w"], grads["ffn_conv_b"] = _ffn_act_bwd(u, up, dact, wt["ffn_conv_w"])
    sent = emit("w_ffn_up_t", _matmul(dup, hn2, ta=True, out_dtype=BF16, name="dw_ffn_up"))
    dhn2 = _matmul(dup, wt["w_ffn_up_t"], name="d_hn2", after=sent)
    dmix, dh, grads["norm_pre_ffn"], grads["norm_post_mix"] = _postmix_bwd(h1, dhn2, dy, mix, wt["norm_pre_ffn"], wt["norm_post_mix"])
    sent = emit("w_mix_out", _matmul(mixed, dmix, ta=True, out_dtype=BF16, name="dw_mix_out"))
    dmixed = _matmul(dmix, wt["w_mix_out"], tb=True, name="d_mixed", after=sent)
    dy_ssm, dy_attn, dproj = _mix_bwd(dmixed, proj, y_ssm, y_attn, lax.empty(proj.shape, BF16))
    sent = emit("w_ssm_out", _matmul(yn, dy_ssm, ta=True, out_dtype=BF16, name="dw_ssm_out"))
    dyn = _matmul(dy_ssm, wt["w_ssm_out"], tb=True, out_dtype=BF16, name="d_yn", after=sent)
    sent = emit("w_attn_out", _matmul(attn, dy_attn, ta=True, out_dtype=BF16, name="dw_attn_out"))
    dattn = _matmul(dy_attn, wt["w_attn_out"], tb=True, name="d_attn", after=sent)
    dy_ssd, dproj, grads["ssm_norm"] = _ssm_post_bwd(y, proj, dyn, wt["ssm_norm"], dproj)
    dxs, dbm, dcm, dproj, dalog, ddtb, dd_x = _ssd_bwd(xact, proj, dtb, alog, dskip_x, dy_ssd, hst, dproj)
    grads["ssm_a_log"] = dalog[:, 0, :HEADS_PER_GROUP].reshape(1, SSM_HEADS)
    grads["ssm_dt_bias"] = ddtb[:, 0, :HEADS_PER_GROUP].reshape(1, SSM_HEADS)
    grads["ssm_d_skip"] = dd_x.reshape(SSM_HEADS, HEAD_P).sum(axis=1).reshape(1, SSM_HEADS)
    dproj, grads["ssm_conv_w"], grads["ssm_conv_b"] = _ssm_conv_bwd(xc, proj, dxs, dbm, dcm, wt["ssm_conv_w"], dproj)
    dproj, dsink = _swa_bwd(proj, sink_rows, bias, attn, dattn, dproj)
    grads["attn_sinks"] = dsink[:, :ATTN_GROUP, 0].reshape(1, ATTN_HEADS)
    sent = emit("w_in_t", _unpack_w_in_t(_matmul(dproj, hn, ta=True, out_dtype=BF16, name="dw_in")))
    dhn = _matmul(dproj, wp_t, name="d_hn", after=sent)
    grad_x, grads["meta_tokens"], grads["norm_pre_mix"] = _prenorm_bwd(x, meta, dhn, dh, wt["norm_pre_mix"])
    return loss_row[0, 0], grad_x, grads


def _all_gather(shards):
    n = len(shards)

    def body(*refs):
        ins, outs = refs[:n], refs[n:2 * n]
        send_sems, recv_sems, local_sems = refs[2 * n:]
        x, y, c = lax.axis_index("x"), lax.axis_index("y"), lax.axis_index("c")
        me, sibling = (x, y, c), (x, y, 1 - c)
        x_nbr, y_nbr, diag = (1 - x, y), (x, 1 - y), (1 - x, 1 - y)
        relayed = (x ^ (1 - c), y ^ c)
        relay_to = (x ^ c, y ^ (1 - c))

        def slot(a, dev):
            return outs[a].at[4 * dev[0] + 2 * dev[1] + dev[2]]

        def copy(k, a, block, to, src=None):
            return pltpu.make_async_remote_copy(
                src_ref=slot(a, block) if src is None else src, dst_ref=slot(a, block),
                send_sem=send_sems.at[k, a], recv_sem=recv_sems.at[k, a],
                device_id=to, device_id_type=pl.DeviceIdType.MESH)

        mine = [pltpu.make_async_copy(ins[a], slot(a, me), local_sems.at[a]) for a in range(n)]
        for cp in mine:
            cp.start()
        first = [copy(0, a, me, sibling, src=ins[a]) for a in range(n)]
        first += [copy(1, a, me, (*x_nbr, c), src=ins[a]) for a in range(n)]
        first += [copy(2, a, me, (*y_nbr, c), src=ins[a]) for a in range(n)]
        for cp in first:
            cp.start()
        passed = []

        def pass_on(k, block, to):
            for a in range(n):
                cp = copy(k, a, block, to)
                cp.start()
                passed.append(cp)

        for j, chip in enumerate((x_nbr, y_nbr)):
            for a in range(n):
                copy(1 + j, a, (*chip, c), me).wait_recv()
            pass_on(4 + j, (*chip, c), sibling)
        pass_on(3, (*relayed, c), (*relay_to, c))
        for a in range(n):
            copy(3, a, (*diag, c), me).wait_recv()
        pass_on(6, (*diag, c), sibling)
        for a in range(n):
            copy(0, a, sibling, me).wait_recv()
        for j, chip in enumerate((x_nbr, y_nbr, diag)):
            for a in range(n):
                copy(4 + j, a, (*chip, 1 - c), me).wait_recv()
        for cp in first + passed:
            cp.wait_send()
        for cp in mine:
            cp.wait()

    hbm = pl.BlockSpec(memory_space=pl.ANY)
    return pl.pallas_call(
        body, in_specs=[hbm] * n, out_specs=[hbm] * n,
        out_shape=[_sds((N_DEV,) + s.shape, s.dtype) for s in shards],
        scratch_shapes=[pltpu.SemaphoreType.DMA((7, n)), pltpu.SemaphoreType.DMA((7, n)), pltpu.SemaphoreType.DMA((n,))],
        name="gather_weights")(*shards)


def _peer_table():
    x, y, c = lax.axis_index("x"), lax.axis_index("y"), lax.axis_index("c")
    peers = []
    for k in range(N_DEV - 1):
        bits = k + 1
        p = (x ^ ((bits >> 2) & 1), y ^ ((bits >> 1) & 1), c ^ (bits & 1))
        peers.append((k, p, 4 * p[0] + 2 * p[1] + p[2]))
    return 4 * x + 2 * y + c, peers


_HBM = pl.BlockSpec(memory_space=pltpu.HBM)
_SEM = pl.BlockSpec(memory_space=pltpu.SEMAPHORE)
_EFFECT = pltpu.SideEffectType.DATAFLOW_SIDE_EFFECTING


def _push_copy(src, land, send_sems, recv_sems, a, k, p, src_slot, dst_slot):
    sem = a * (N_DEV - 1) + k
    return pltpu.make_async_remote_copy(
        src_ref=src[a] if src_slot is None else src[a].at[src_slot], dst_ref=land[a].at[dst_slot],
        send_sem=send_sems.at[sem], recv_sem=recv_sems.at[sem], device_id=p, device_id_type=pl.DeviceIdType.MESH)


def _push_start(srcs, scatter, name):
    n = len(srcs)
    lands = [lax.empty(s.shape if scatter else (N_DEV,) + s.shape, s.dtype) for s in srcs]

    def body(*refs):
        src, land = refs[:n], refs[n:2 * n]
        send_sems, recv_sems, token = refs[2 * n], refs[2 * n + 1], refs[-1]
        my_id, peers = _peer_table()
        for a in range(n):
            for k, p, p_id in peers:
                _push_copy(src, land, send_sems, recv_sems, a, k, p, p_id if scatter else None, my_id).start()
        token[...] = jnp.zeros_like(token)

    sems = pltpu.SemaphoreType.DMA(((N_DEV - 1) * n,))
    res = pl.pallas_call(
        body, name=name,
        out_shape=(sems, sems, *[pltpu.HBM(a.shape, a.dtype) for a in srcs + lands], _sds((8, 128), F32)),
        in_specs=[_HBM] * (2 * n), out_specs=(_SEM, _SEM, *[_HBM] * (2 * n), pl.BlockSpec(memory_space=pltpu.VMEM)),
        input_output_aliases={i: 2 + i for i in range(2 * n)},
        compiler_params=pltpu.CompilerParams(has_side_effects=_EFFECT),
    )(*[pltpu.with_memory_space_constraint(a, pltpu.HBM) for a in srcs + lands])
    return dict(send=res[0], recv=res[1], src=list(res[2:2 + n]), land=list(res[2 + n:2 + 2 * n]), token=res[-1],
                scatter=scatter)


def _push_wait(handle, after, name):
    n = len(handle["src"])
    scatter = handle["scatter"]

    def body(*refs):
        src, land = refs[:n], refs[n:2 * n]
        send_sems, recv_sems = refs[2 * n], refs[2 * n + 1]
        _, peers = _peer_table()
        for a in range(n):
            for k, p, p_id in peers:
                cp = _push_copy(src, land, send_sems, recv_sems, a, k, p, p_id if scatter else None, p_id)
                cp.wait_send()
                cp.wait_recv()

    arrays = handle["src"] + handle["land"]
    res = pl.pallas_call(
        body, name=name, out_shape=tuple(pltpu.HBM(a.shape, a.dtype) for a in arrays),
        in_specs=[_HBM] * (2 * n) + [_SEM, _SEM, pl.BlockSpec(memory_space=pl.ANY)], out_specs=tuple([_HBM] * (2 * n)),
        input_output_aliases={i: i for i in range(2 * n)},
        compiler_params=pltpu.CompilerParams(has_side_effects=_EFFECT),
    )(*arrays, handle["send"], handle["recv"], after)
    return list(res[:n]), list(res[n:])


def _slot_sum(p_ref, own_ref):
    if own_ref is not None:
        my_id = 4 * lax.axis_index("x") + 2 * lax.axis_index("y") + lax.axis_index("c")
        mine = own_ref[...].astype(F32)
    g = None
    for s in range(p_ref.shape[0]):
        term = p_ref[s].astype(F32)
        if own_ref is not None:
            term = jnp.where(my_id == s, mine, term)
        g = term if g is None else g + term
    return g


def _to_bf16(arrays):
    n = len(arrays)

    def body(*refs):
        for i in range(n):
            refs[n + i][...] = refs[i][...].astype(BF16)

    return pl.pallas_call(body, out_shape=[_sds(a.shape, BF16) for a in arrays], name="weights_to_bf16",
                          compiler_params=pltpu.CompilerParams(vmem_limit_bytes=VMEM_LIMIT))(*arrays)


def _adamw(parts, own, w, m, v, name):
    unit_rows = w.ndim == 3
    rows, cols = w.shape[0], w.shape[-1]
    if rows % 16 == 0:
        tr, tc = _pick(rows, (256, 128, 176, 64, 32, 16)), cols
    else:
        tr, tc = rows, _pick(cols, (256, 128))

    def body(*refs):
        if own is None:
            p_ref, w_ref, m_ref, v_ref, g_ref, d_ref, nm_ref, nv_ref = refs
            own_ref = None
        else:
            p_ref, own_ref, w_ref, m_ref, v_ref, g_ref, d_ref, nm_ref, nv_ref = refs
        g = _slot_sum(p_ref, own_ref)
        if unit_rows:
            g = g.reshape(tr, 1, tc)
        m_new = ADAM_B1 * m_ref[...] + (1.0 - ADAM_B1) * g
        v_new = ADAM_B2 * v_ref[...] + (1.0 - ADAM_B2) * (g * g)
        m_hat = m_new / (1.0 - ADAM_B1 ** ADAM_STEP)
        v_hat = v_new / (1.0 - ADAM_B2 ** ADAM_STEP)
        g_ref[...] = g
        d_ref[...] = -ADAM_LR * (m_hat / (jnp.sqrt(v_hat) + ADAM_EPS) + ADAM_WD * w_ref[...])
        nm_ref[...] = m_new
        nv_ref[...] = v_new

    by_rows = tc == cols
    spec = pl.BlockSpec((tr, tc), (lambda i: (i, 0)) if by_rows else (lambda i: (0, i)))
    state_spec = spec if not unit_rows else pl.BlockSpec((tr, 1, tc), (lambda i: (i, 0, 0)) if by_rows else (lambda i: (0, 0, i)))
    parts_spec = pl.BlockSpec((parts.shape[0], tr, tc), (lambda i: (0, i, 0)) if by_rows else (lambda i: (0, 0, i)))
    operands = (parts, w, m, v) if own is None else (parts, own, w, m, v)
    return pl.pallas_call(
        body, grid=(rows // tr if by_rows else cols // tc,),
        in_specs=[parts_spec] + ([] if own is None else [spec]) + [state_spec] * 3,
        out_specs=[state_spec] * 4, out_shape=[_sds(w.shape, F32)] * 4,
        name=name, compiler_params=_params(1))(*operands)


SMALL_REPLICATED = (("norm_pre_mix", 1024), ("ssm_conv_b", 3072), ("ssm_dt_bias", 32), ("ssm_a_log", 32),
                    ("ssm_d_skip", 32), ("ssm_norm", 2048), ("attn_sinks", 16), ("norm_post_mix", 1024),
                    ("norm_pre_ffn", 1024), ("ffn_conv_b", 5632), ("norm_post_ffn", 1024))
SMALL_SHARDED = (("meta_tokens", (N_META, D_MODEL // N_DEV)), ("ssm_conv_w", (SSM_CONV, CONV_DIM // N_DEV)),
                 ("ffn_conv_w", (FFN_CONV, 2 * FFN_DIM // N_DEV)))
BIG = (("w_in", (D_MODEL, N_IN // N_DEV), 1), ("w_ssm_out", (D_INNER // N_DEV, D_MODEL), 0),
       ("w_attn_out", (D_MODEL // N_DEV, D_MODEL), 0), ("w_mix_out", (D_MODEL // N_DEV, D_MODEL), 0),
       ("w_ffn_up", (D_MODEL, 2 * FFN_DIM // N_DEV), 1), ("w_ffn_down", (FFN_DIM // N_DEV, D_MODEL), 0))


def _rows_of(size):
    return -(-size // 128)


def _as_rows(flat):
    size = flat.shape[-1]
    rows = _rows_of(size)
    flat = jnp.pad(flat, [(0, 0)] * (flat.ndim - 1) + [(0, rows * 128 - size)])
    return flat.reshape(flat.shape[:-1] + (rows, 128))


def _pack_small(rep, sharded):
    pieces = [_as_rows(rep[name].reshape(-1)) for name, _ in SMALL_REPLICATED]
    pieces += [_as_rows(sharded[name].reshape(-1)) for name, _ in SMALL_SHARDED]
    packed = jnp.concatenate(pieces, axis=0)
    return jnp.pad(packed, ((0, -packed.shape[0] % 8), (0, 0)))


def _unpack_small(packed):
    out, row = {}, 0
    for name, size in SMALL_REPLICATED:
        out[name] = packed[row:row + _rows_of(size)].reshape(-1)[:size].reshape(1, size)
        row += _rows_of(size)
    for name, (r, c) in SMALL_SHARDED:
        out[name] = packed[row:row + _rows_of(r * c)].reshape(-1)[:r * c].reshape(r, c)
        row += _rows_of(r * c)
    return out


def _shard_major(g, shape, axis):
    r, c = shape
    if axis == 0:
        return g.reshape(N_DEV, r, c)
    return g.reshape(r, N_DEV, c).transpose(1, 0, 2)


def kernel(x, meta_tokens, norm_pre_mix, w_in, ssm_conv_w, ssm_conv_b, ssm_dt_bias, ssm_a_log, ssm_d_skip, ssm_norm, w_ssm_out, attn_sinks, w_attn_out, w_mix_out, norm_post_mix, norm_pre_ffn, w_ffn_up, ffn_conv_w, ffn_conv_b, w_ffn_down, norm_post_ffn, loss_target, m_meta_tokens, m_norm_pre_mix, m_w_in, m_ssm_conv_w, m_ssm_conv_b, m_ssm_dt_bias, m_ssm_a_log, m_ssm_d_skip, m_ssm_norm, m_w_ssm_out, m_attn_sinks, m_w_attn_out, m_w_mix_out, m_norm_post_mix, m_norm_pre_ffn, m_w_ffn_up, m_ffn_conv_w, m_ffn_conv_b, m_w_ffn_down, m_norm_post_ffn, v_meta_tokens, v_norm_pre_mix, v_w_in, v_ssm_conv_w, v_ssm_conv_b, v_ssm_dt_bias, v_ssm_a_log, v_ssm_d_skip, v_ssm_norm, v_w_ssm_out, v_attn_sinks, v_w_attn_out, v_w_mix_out, v_norm_post_mix, v_norm_pre_ffn, v_w_ffn_up, v_ffn_conv_w, v_ffn_conv_b, v_w_ffn_down, v_norm_post_ffn):
    names = ("meta_tokens", "norm_pre_mix", "w_in", "ssm_conv_w", "ssm_conv_b", "ssm_dt_bias", "ssm_a_log", "ssm_d_skip",
             "ssm_norm", "w_ssm_out", "attn_sinks", "w_attn_out", "w_mix_out", "norm_post_mix", "norm_pre_ffn", "w_ffn_up",
             "ffn_conv_w", "ffn_conv_b", "w_ffn_down", "norm_post_ffn")
    w_loc = dict(zip(names, (meta_tokens, norm_pre_mix, w_in, ssm_conv_w, ssm_conv_b, ssm_dt_bias, ssm_a_log, ssm_d_skip,
                             ssm_norm, w_ssm_out, attn_sinks, w_attn_out, w_mix_out, norm_post_mix, norm_pre_ffn, w_ffn_up,
                             ffn_conv_w, ffn_conv_b, w_ffn_down, norm_post_ffn)))
    m_loc = dict(zip(names, (m_meta_tokens, m_norm_pre_mix, m_w_in, m_ssm_conv_w, m_ssm_conv_b, m_ssm_dt_bias, m_ssm_a_log,
                             m_ssm_d_skip, m_ssm_norm, m_w_ssm_out, m_attn_sinks, m_w_attn_out, m_w_mix_out, m_norm_post_mix,
                             m_norm_pre_ffn, m_w_ffn_up, m_ffn_conv_w, m_ffn_conv_b, m_w_ffn_down, m_norm_post_ffn)))
    v_loc = dict(zip(names, (v_meta_tokens, v_norm_pre_mix, v_w_in, v_ssm_conv_w, v_ssm_conv_b, v_ssm_dt_bias, v_ssm_a_log,
                             v_ssm_d_skip, v_ssm_norm, v_w_ssm_out, v_attn_sinks, v_w_attn_out, v_w_mix_out, v_norm_post_mix,
                             v_norm_pre_ffn, v_w_ffn_up, v_ffn_conv_w, v_ffn_conv_b, v_w_ffn_down, v_norm_post_ffn)))

    def local2d(d, name):
        a = d[name]
        return a if name == "meta_tokens" else a.reshape(a.shape[1:])

    def turned2d(d, name):
        a = jnp.swapaxes(d[name], 1, 2)
        return a.reshape(a.shape[1:])

    my_id = 4 * lax.axis_index("x") + 2 * lax.axis_index("y") + lax.axis_index("c")
    big = {name: (shape, axis) for name, shape, axis in BIG}

    def whole(name, g):
        return g.reshape(N_DEV * g.shape[1], g.shape[2])

    def key(name):
        return name + "_t" if big[name][1] == 1 else name

    by_rows = [name for name, _, axis in BIG if axis == 0]
    send_bf16 = dict(zip(by_rows, _to_bf16([local2d(w_loc, name) for name in by_rows])))
    for name, _, axis in BIG:
        if axis == 1:
            send_bf16[name] = turned2d(w_loc, name).astype(BF16)
    small_shard_pack = jnp.concatenate([_as_rows(local2d(w_loc, name).reshape(-1)) for name, _ in SMALL_SHARDED], axis=0)
    small_shard_pack = jnp.pad(small_shard_pack, ((0, -small_shard_pack.shape[0] % 8), (0, 0)))
    first = _all_gather([send_bf16["w_in"], small_shard_pack])
    rest_names = [name for name, _, _ in BIG if name != "w_in"]
    rest = [send_bf16[name] for name in rest_names]
    rest, first = lax.optimization_barrier((rest, first))
    rest_handle = _push_start(rest, False, "gather_rest_start")
    wt = {"w_in_t": whole("w_in", first[0])}
    row = 0
    for name, (r, c) in SMALL_SHARDED:
        blocks = first[1][:, row:row + _rows_of(r * c)].reshape(N_DEV, -1)[:, :r * c].reshape(N_DEV, r, c)
        wt[name] = blocks.transpose(1, 0, 2).reshape(r, N_DEV * c)
        row += _rows_of(r * c)
    for name, size in SMALL_REPLICATED:
        wt[name] = w_loc[name].reshape(1, size)

    def late_weights(after):
        own, landed = _push_wait(rest_handle, after, "gather_rest_wait")
        out = {}
        for name, mine, land in zip(rest_names, own, landed):
            out[key(name)] = whole(name, lax.dynamic_update_index_in_dim(land, mine, my_id, 0))
        return out

    sent = {}

    def on_grad(known_as, g):
        name = known_as.removesuffix("_t")
        by_owner = g.reshape(N_DEV, g.shape[0] // N_DEV, g.shape[1])
        sent[name] = _push_start([by_owner], True, "send_" + name)
        return sent[name]["token"]

    loss_part, grad_x, grads = _local_step(x[0], loss_target[0], wt, late_weights, on_grad, rest_handle["token"])
    loss = lax.psum(loss_part, AXES)

    small_parts = []
    for name, (r, c) in SMALL_SHARDED:
        small_parts.append(_as_rows(_shard_major(grads[name], (r, c), 1).reshape(N_DEV, r * c)))
    rep_rows = jnp.concatenate([_as_rows(grads[name].reshape(-1)) for name, _ in SMALL_REPLICATED], axis=0)
    small_send = jnp.concatenate([jnp.broadcast_to(rep_rows[None], (N_DEV,) + rep_rows.shape)] + small_parts, axis=1)
    small_send = jnp.pad(small_send, ((0, 0), (0, -small_send.shape[1] % 8), (0, 0)))
    small_handle = _push_start([small_send], True, "send_small")

    def small_pack(d):
        return _pack_small({name: d[name] for name, _ in SMALL_REPLICATED}, {name: local2d(d, name) for name, _ in SMALL_SHARDED})

    def arrived(handle, after, name):
        src, landed = _push_wait(handle, after, "arrived_" + name)
        return landed[0], lax.dynamic_index_in_dim(src[0], my_id, 0, keepdims=False)

    grad_w, delta_w, new_m, new_v = {}, {}, {}, {}
    outs = None
    after = small_handle["token"]
    for name, handle in sent.items():
        if name == "w_in":
            parts, own = arrived(small_handle, after, "small")
            outs = _adamw(parts, own, small_pack(w_loc), small_pack(m_loc), small_pack(v_loc), "adamw_small")
            after = outs[0]
        parts, own = arrived(handle, after, name)
        turned = big[name][1] == 1
        unit_rows = turned and big[name][0][1] % 8 != 0
        if unit_rows:
            state = [jnp.transpose(d[name], (2, 0, 1)) for d in (w_loc, m_loc, v_loc)]
        else:
            state = [turned2d(d, name) if turned else local2d(d, name) for d in (w_loc, m_loc, v_loc)]
        results = _adamw(parts, own, *state, "adamw_" + name)
        after = results[0]
        full = (1,) + big[name][0]
        for dst, a in zip((grad_w, delta_w, new_m, new_v), results):
            if unit_rows:
                dst[name] = jnp.transpose(a, (1, 2, 0))
            else:
                dst[name] = jnp.swapaxes(a[None], 1, 2) if turned else a.reshape(full)
    for dst, packed in zip((grad_w, delta_w, new_m, new_v), outs):
        for name, a in _unpack_small(packed).items():
            dst[name] = a.reshape(w_loc[name].shape)

    return (loss, grad_x[None], *[grad_w[n] for n in names], *[delta_w[n] for n in names],
            *[new_m[n] for n in names], *[new_v[n] for n in names])
```

```python
import jax
import jax.numpy as jnp
from jax import lax
from jax.experimental import pallas as pl
from jax.experimental.pallas import tpu as pltpu

F32 = jnp.float32
BF16 = jnp.bfloat16

D_MODEL = 1024
N_META = 16
CHUNK = 128
META_PAD = CHUNK - N_META
D_INNER = 2048
HEAD_P = 64
SSM_HEADS = 32
SSM_GROUPS = 4
HEADS_PER_GROUP = SSM_HEADS // SSM_GROUPS
GROUP_W = HEADS_PER_GROUP * HEAD_P
D_STATE = 128
SSM_CONV = 4
CONV_DIM = D_INNER + 2 * SSM_GROUPS * D_STATE
ATTN_HEADS = 16
KV_HEADS = 4
ATTN_GROUP = ATTN_HEADS // KV_HEADS
DH = 64
KV_W = KV_HEADS * DH
FFN_DIM = 2816
FFN_CONV = 3
EPS = 1e-6
NEG = -1e30
N_DEV = 8
AXES = ("x", "y", "c")

OFF_Z, OFF_GATE, OFF_DT, OFF_Q, OFF_K, OFF_V, OFF_XBC = 0, 2048, 4096, 4608, 5632, 5888, 6144
N_INP = OFF_XBC + CONV_DIM
QKV_W = OFF_XBC - OFF_Q
CUT_Z, CUT_XBC, CUT_DT, CUT_Q, CUT_K, CUT_V, CUT_G = 0, 2048, 5120, 5152, 6176, 6432, 6688
N_IN = 8736

ADAM_LR, ADAM_B1, ADAM_B2, ADAM_EPS, ADAM_WD, ADAM_STEP = 0.001, 0.9, 0.999, 1e-08, 0.01, 10

VMEM_LIMIT = 56 * 1024 * 1024


def _params(n_grid):
    return pltpu.CompilerParams(dimension_semantics=("arbitrary",) * n_grid, vmem_limit_bytes=VMEM_LIMIT)


def _sds(shape, dtype):
    return jax.ShapeDtypeStruct(shape, dtype)


def _pick(n, prefs):
    for c in prefs:
        if n % c == 0:
            return c
    raise ValueError(f"no tile of {prefs} divides {n}")


def _row(tr, width, cb=0):
    return pl.BlockSpec((tr, width), lambda i: (i, cb))


def _row_rev(tr, width, nt, cb=0):
    return pl.BlockSpec((tr, width), lambda i: (nt - 1 - i, cb))


def _full(shape):
    return pl.BlockSpec(shape, lambda *_: (0,) * len(shape))


def _sigmoid(x):
    return 1.0 / (1.0 + jnp.exp(-x))


def _softplus(x):
    return jnp.maximum(x, 0.0) + jnp.log(1.0 + jnp.exp(-jnp.abs(x)))


def _rms(x):
    return lax.rsqrt(jnp.mean(x * x, axis=-1, keepdims=True) + EPS)


def _rms_bwd(x, r, w, dy):
    xh = x * r
    g = dy * w
    dx = r * (g - xh * jnp.mean(g * xh, axis=-1, keepdims=True))
    return dx, jnp.sum(dy * xh, axis=0, keepdims=True)


def _row_ids(shape, tile_index, tr):
    return tile_index * tr + lax.broadcasted_iota(jnp.int32, shape, 0)


HALO = 8
STRIP = 256
STRIP_BWD = 128


def _causal_taps(x, halo, first_step, taps):
    n = x.shape[0]

    @pl.when(first_step)
    def _():
        halo[...] = jnp.zeros_like(halo)

    before = halo[...]
    row = lax.broadcasted_iota(jnp.int32, before.shape, 0)
    shifted = [x]
    for s in range(1, taps):
        rolled = pltpu.roll(x, s, 0)
        head = jnp.where(row < s, pltpu.roll(before, s, 0), rolled[0:HALO, :])
        shifted.append(jnp.concatenate([head, rolled[HALO:, :]], axis=0))
    halo[...] = x[n - HALO:, :]
    return shifted


def _anticausal_taps(x, halo, first_step, taps):
    n = x.shape[0]

    @pl.when(first_step)
    def _():
        halo[...] = jnp.zeros_like(halo)

    after = halo[...]
    row = lax.broadcasted_iota(jnp.int32, after.shape, 0)
    shifted = [x]
    for s in range(1, taps):
        rolled = pltpu.roll(x, n - s, 0)
        tail = jnp.where(row >= HALO - s, pltpu.roll(after, HALO - s, 0), rolled[n - HALO:, :])
        shifted.append(jnp.concatenate([rolled[:n - HALO, :], tail], axis=0))
    halo[...] = x[0:HALO, :]
    return shifted


def _matmul(a, b, *, ta=False, tb=False, out_dtype=F32, name, after=None):
    if ta:
        k_dim, m_dim = a.shape
    else:
        m_dim, k_dim = a.shape
    n_dim = b.shape[0] if tb else b.shape[1]
    tm = _pick(m_dim, (1408, 1024, 768, 512, 384, 256, 128))
    tn = _pick(n_dim, (1024, 1408, 768, 512, 384, 256, 128))
    if ta:
        tk = _pick(k_dim, (1408, 1024, 768, 512, 384, 256, 128))
    else:
        tk = k_dim if k_dim <= 3072 else _pick(k_dim, (3072, 2816, 2048, 1024))
    nk = k_dim // tk
    dims = (((0 if ta else 1,), (1 if tb else 0,)), ((), ()))

    use_acc = nk > 1 and out_dtype != F32

    def body(a_ref, b_ref, *rest):
        o_ref = rest[-2] if use_acc else rest[-1]
        acc_ref = rest[-1] if use_acc else o_ref
        r = lax.dot_general(a_ref[...].astype(BF16), b_ref[...].astype(BF16), dims, preferred_element_type=F32)
        if nk == 1:
            o_ref[...] = r.astype(o_ref.dtype)
        else:
            k = pl.program_id(2)

            @pl.when(k == 0)
            def _():
                acc_ref[...] = r

            @pl.when(k > 0)
            def _():
                acc_ref[...] += r

            if use_acc:
                @pl.when(k == nk - 1)
                def _():
                    o_ref[...] = acc_ref[...].astype(o_ref.dtype)

    a_spec = pl.BlockSpec((tk, tm), lambda i, j, k: (k, i)) if ta else pl.BlockSpec((tm, tk), lambda i, j, k: (i, k))
    b_spec = pl.BlockSpec((tn, tk), lambda i, j, k: (j, k)) if tb else pl.BlockSpec((tk, tn), lambda i, j, k: (k, j))
    extra_specs, extra = ([], ()) if after is None else ([pl.BlockSpec(memory_space=pl.ANY)], (after,))
    return pl.pallas_call(
        body, grid=(m_dim // tm, n_dim // tn, nk), in_specs=[a_spec, b_spec] + extra_specs,
        out_specs=pl.BlockSpec((tm, tn), lambda i, j, k: (i, j)), out_shape=_sds((m_dim, n_dim), out_dtype),
        scratch_shapes=[pltpu.VMEM((tm, tn), F32)] if use_acc else [],
        name=name, compiler_params=_params(3))(a, b, *extra)


def _seq_rows(t_rows):
    return 384 if t_rows % 384 == 0 and t_rows >= 768 else CHUNK


def _token_rows(tr):
    if tr == CHUNK:
        return pl.BlockSpec((CHUNK, D_MODEL), lambda i: (jnp.maximum(i - 1, 0), 0))
    return pl.BlockSpec((pl.Element(tr), pl.Element(D_MODEL)),
                        lambda i: (pl.multiple_of(jnp.maximum(i * tr - CHUNK, 0), CHUNK), 0))


def _under_tile(rows_ref, head, i):
    rows = rows_ref[...]
    tr = rows.shape[0]
    first = head if tr == CHUNK else jnp.concatenate([head, rows[0:tr - CHUNK, :]], axis=0)
    return jnp.where(i == 0, first, rows)


def _seq_specs(tr=CHUNK):
    return [_token_rows(tr), _full((N_META, D_MODEL))]


def _seq_tile(x_ref, meta_ref, i):
    return _under_tile(x_ref, jnp.concatenate([jnp.zeros((META_PAD, D_MODEL), F32), meta_ref[...]], axis=0), i)


def _prenorm(x, meta, w):
    t_rows = x.shape[0] + CHUNK
    tr = _seq_rows(t_rows)

    def body(x_ref, meta_ref, w_ref, o_ref):
        h = _seq_tile(x_ref, meta_ref, pl.program_id(0))
        o_ref[...] = (h * _rms(h) * w_ref[...]).astype(BF16)

    return pl.pallas_call(body, grid=(t_rows // tr,), in_specs=_seq_specs(tr) + [_full((1, D_MODEL))],
                          out_specs=_row(tr, D_MODEL), out_shape=_sds((t_rows, D_MODEL), BF16),
                          name="prenorm", compiler_params=_params(1))(x, meta, w)


def _ssm_conv_fwd(proj, conv_w, conv_b):
    t_rows = proj.shape[0]
    tr = CHUNK

    def body(x_ref, w_ref, b_ref, xc_ref, xa_ref, hist):
        first = pl.program_id(0) == 0
        for c in range(0, CONV_DIM, STRIP):
            cols = slice(c, c + STRIP)
            acc = b_ref[:, cols]
            for s, moved in enumerate(_causal_taps(x_ref[:, cols], hist.at[:, cols], first, SSM_CONV)):
                acc = acc + w_ref[SSM_CONV - 1 - s:SSM_CONV - s, cols] * moved
            xc_ref[:, cols] = acc
            xa_ref[:, cols] = acc * _sigmoid(acc)

    return pl.pallas_call(
        body, grid=(t_rows // tr,),
        in_specs=[_row(tr, CONV_DIM, OFF_XBC // CONV_DIM), _full((SSM_CONV, CONV_DIM)), _full((1, CONV_DIM))],
        out_specs=[_row(tr, CONV_DIM), _row(tr, CONV_DIM)],
        out_shape=[_sds((t_rows, CONV_DIM), F32), _sds((t_rows, CONV_DIM), F32)],
        scratch_shapes=[pltpu.VMEM((HALO, CONV_DIM), F32)],
        name="ssm_conv_fwd", compiler_params=_params(1))(proj, conv_w, conv_b)


def _ssm_post(y, proj, w):
    t_rows = y.shape[0]
    tr = _pick(t_rows, (384, 128))

    def body(y_ref, z_ref, w_ref, o_ref):
        z = z_ref[...].astype(F32)
        yz = y_ref[...] * z * _sigmoid(z)
        o_ref[...] = (yz * _rms(yz) * w_ref[...]).astype(BF16)

    return pl.pallas_call(body, grid=(t_rows // tr,),
                          in_specs=[_row(tr, D_INNER), _row(tr, D_INNER, OFF_Z // D_INNER), _full((1, D_INNER))],
                          out_specs=_row(tr, D_INNER), out_shape=_sds((t_rows, D_INNER), BF16),
                          name="ssm_post", compiler_params=_params(1))(y, proj, w)


def _mix_fwd(proj, y_ssm, y_attn):
    t_rows = y_ssm.shape[0]
    tr = _pick(t_rows, (384, 128))

    def body(g_ref, ys_ref, ya_ref, o_ref):
        g = _sigmoid(g_ref[...].astype(F32))
        o_ref[...] = (g[:, :D_MODEL] * ys_ref[...] + g[:, D_MODEL:] * ya_ref[...]).astype(BF16)

    return pl.pallas_call(body, grid=(t_rows // tr,),
                          in_specs=[_row(tr, 2 * D_MODEL, OFF_GATE // (2 * D_MODEL)), _row(tr, D_MODEL),
                                    _row(tr, D_MODEL)],
                          out_specs=_row(tr, D_MODEL), out_shape=_sds((t_rows, D_MODEL), BF16),
                          name="mix_fwd", compiler_params=_params(1))(proj, y_ssm, y_attn)


def _postmix(x, meta, mix, w_post, w_pre):
    t_rows = mix.shape[0]
    tr = _seq_rows(t_rows)

    def body(x_ref, meta_ref, m_ref, wp_ref, wf_ref, h1_ref, hn_ref):
        m = m_ref[...]
        h1 = _seq_tile(x_ref, meta_ref, pl.program_id(0)) + m * _rms(m) * wp_ref[...]
        h1 = jnp.where(_row_ids(h1.shape, pl.program_id(0), tr) >= META_PAD, h1, 0.0)
        h1_ref[...] = h1
        hn_ref[...] = (h1 * _rms(h1) * wf_ref[...]).astype(BF16)

    return pl.pallas_call(body, grid=(t_rows // tr,),
                          in_specs=_seq_specs(tr) + [_row(tr, D_MODEL), _full((1, D_MODEL)), _full((1, D_MODEL))],
                          out_specs=[_row(tr, D_MODEL), _row(tr, D_MODEL)],
                          out_shape=[_sds((t_rows, D_MODEL), F32), _sds((t_rows, D_MODEL), BF16)],
                          name="postmix", compiler_params=_params(1))(x, meta, mix, w_post, w_pre)


def _ffn_act(up, conv_w, conv_b):
    t_rows = up.shape[0]
    tr = CHUNK
    width = 2 * FFN_DIM

    def body(up_ref, w_ref, b_ref, u_ref, act_ref, hist):
        first = pl.program_id(0) == 0
        for c in range(0, FFN_DIM, STRIP):
            halves = []
            for base in (0, FFN_DIM):
                cols = slice(base + c, base + c + STRIP)
                u = b_ref[:, cols]
                for s, moved in enumerate(_causal_taps(up_ref[:, cols].astype(F32), hist.at[:, cols], first, FFN_CONV)):
                    u = u + w_ref[FFN_CONV - 1 - s:FFN_CONV - s, cols] * moved
                u_ref[:, cols] = u.astype(BF16)
                halves.append(u)
            a, g = halves
            act_ref[:, c:c + STRIP] = (a * _sigmoid(a) * g).astype(BF16)

    return pl.pallas_call(
        body, grid=(t_rows // tr,), in_specs=[_row(tr, width), _full((FFN_CONV, width)), _full((1, width))],
        out_specs=[_row(tr, width), _row(tr, FFN_DIM)],
        out_shape=[_sds((t_rows, width), BF16), _sds((t_rows, FFN_DIM), BF16)],
        scratch_shapes=[pltpu.VMEM((HALO, width), F32)],
        name="ffn_act", compiler_params=_params(1))(up, conv_w, conv_b)


def _final(h1, f, target, w):
    t_rows = h1.shape[0]
    tr = _seq_rows(t_rows)

    def body(h1_ref, f_ref, t_ref, w_ref, df_ref, dy_ref, dw_ref, loss_ref):
        i = pl.program_id(0)

        @pl.when(i == 0)
        def _():
            dw_ref[...] = jnp.zeros_like(dw_ref)
            loss_ref[...] = jnp.zeros_like(loss_ref)

        f_val = f_ref[...]
        r = _rms(f_val)
        wv = w_ref[...]
        h2 = h1_ref[...] + f_val * r * wv
        tgt = _under_tile(t_ref, jnp.zeros((CHUNK, D_MODEL), F32), i)
        diff = jnp.where(_row_ids(h2.shape, i, tr) >= CHUNK, h2 - tgt, 0.0)
        loss_ref[...] += 0.5 * jnp.sum(diff * diff) * (1.0 / D_MODEL)
        dy = diff * (1.0 / D_MODEL)
        dy_ref[...] = dy
        df, dw = _rms_bwd(f_val, r, wv, dy)
        df_ref[...] = df.astype(BF16)
        dw_ref[...] += dw

    return pl.pallas_call(
        body, grid=(t_rows // tr,),
        in_specs=[_row(tr, D_MODEL), _row(tr, D_MODEL), _token_rows(tr), _full((1, D_MODEL))],
        out_specs=[_row(tr, D_MODEL), _row(tr, D_MODEL), _full((1, D_MODEL)), _full((1, 128))],
        out_shape=[_sds((t_rows, D_MODEL), BF16), _sds((t_rows, D_MODEL), F32), _sds((1, D_MODEL), F32), _sds((1, 128), F32)],
        name="final", compiler_params=_params(1))(h1, f, target, w)


def _ffn_act_bwd(u, up, dact, conv_w):
    t_rows = u.shape[0]
    tr = CHUNK
    nt = t_rows // tr
    width = 2 * FFN_DIM

    def body(u_ref, up_ref, da_ref, w_ref, dup_ref, dw_ref, db_ref, ahead):
        @pl.when(pl.program_id(0) == 0)
        def _():
            dw_ref[...] = jnp.zeros_like(dw_ref)
            db_ref[...] = jnp.zeros_like(db_ref)

        first = pl.program_id(0) == 0
        for c in range(0, FFN_DIM, STRIP_BWD):
            ca, cg = slice(c, c + STRIP_BWD), slice(FFN_DIM + c, FFN_DIM + c + STRIP_BWD)
            a, g, d = u_ref[:, ca].astype(F32), u_ref[:, cg].astype(F32), da_ref[:, ca].astype(F32)
            s = _sigmoid(a)
            for cols, du in ((ca, d * g * s * (1.0 + a * (1.0 - s))), (cg, d * a * s)):
                x = up_ref[:, cols].astype(F32)
                dup = None
                for sh, moved in enumerate(_anticausal_taps(du, ahead.at[:, cols], first, FFN_CONV)):
                    k = FFN_CONV - 1 - sh
                    term = w_ref[k:k + 1, cols] * moved
                    dup = term if dup is None else dup + term
                    dw_ref[k:k + 1, cols] += jnp.sum(moved * x, axis=0, keepdims=True)
                db_ref[:, cols] += jnp.sum(du, axis=0, keepdims=True)
                dup_ref[:, cols] = dup.astype(BF16)

    return pl.pallas_call(
        body, grid=(nt,),
        in_specs=[_row_rev(tr, width, nt), _row_rev(tr, width, nt), _row_rev(tr, FFN_DIM, nt), _full((FFN_CONV, width))],
        out_specs=[_row_rev(tr, width, nt), _full((FFN_CONV, width)), _full((1, width))],
        out_shape=[_sds((t_rows, width), BF16), _sds((FFN_CONV, width), F32), _sds((1, width), F32)],
        scratch_shapes=[pltpu.VMEM((HALO, width), F32)],
        name="ffn_act_bwd", compiler_params=_params(1))(u, up, dact, conv_w)


def _postmix_bwd(h1, dhn2, dy, mix, w_pre, w_post):
    t_rows = h1.shape[0]
    tr = _pick(t_rows, (384, 128))

    def body(h1_ref, dhn_ref, dy_ref, m_ref, wf_ref, wp_ref, dmix_ref, dh_ref, dwf_ref, dwp_ref):
        @pl.when(pl.program_id(0) == 0)
        def _():
            dwf_ref[...] = jnp.zeros_like(dwf_ref)
            dwp_ref[...] = jnp.zeros_like(dwp_ref)

        h1v = h1_ref[...]
        dx, dwf = _rms_bwd(h1v, _rms(h1v), wf_ref[...], dhn_ref[...])
        dwf_ref[...] += dwf
        dh1 = dy_ref[...] + dx
        dh1 = jnp.where(_row_ids(dh1.shape, pl.program_id(0), tr) >= META_PAD, dh1, 0.0)
        dh_ref[...] = dh1
        m = m_ref[...]
        dmix, dwp = _rms_bwd(m, _rms(m), wp_ref[...], dh1)
        dwp_ref[...] += dwp
        dmix_ref[...] = dmix.astype(BF16)

    return pl.pallas_call(
        body, grid=(t_rows // tr,),
        in_specs=[_row(tr, D_MODEL) for _ in range(4)] + [_full((1, D_MODEL))] * 2,
        out_specs=[_row(tr, D_MODEL), _row(tr, D_MODEL), _full((1, D_MODEL)), _full((1, D_MODEL))],
        out_shape=[_sds((t_rows, D_MODEL), BF16), _sds((t_rows, D_MODEL), F32), _sds((1, D_MODEL), F32), _sds((1, D_MODEL), F32)],
        name="postmix_bwd", compiler_params=_params(1))(h1, dhn2, dy, mix, w_pre, w_post)


_ANY = pl.BlockSpec(memory_space=pl.ANY)


def _mix_bwd(dmixed, proj, y_ssm, y_attn, dproj):
    t_rows = dmixed.shape[0]
    tr = _pick(t_rows, (384, 128))

    def body(d_ref, g_ref, ys_ref, ya_ref, _, dys_ref, dya_ref, dg_ref):
        d = d_ref[...]
        g = _sigmoid(g_ref[...].astype(F32))
        g1, g2 = g[:, :D_MODEL], g[:, D_MODEL:]
        dys_ref[...] = (d * g1).astype(BF16)
        dya_ref[...] = (d * g2).astype(BF16)
        dg_ref[...] = jnp.concatenate([d * ys_ref[...] * g1 * (1.0 - g1), d * ya_ref[...] * g2 * (1.0 - g2)],
                                      axis=1).astype(BF16)

    return pl.pallas_call(
        body, grid=(t_rows // tr,),
        in_specs=[_row(tr, D_MODEL), _row(tr, 2 * D_MODEL, OFF_GATE // (2 * D_MODEL)), _row(tr, D_MODEL), _row(tr, D_MODEL),
                  _ANY],
        out_specs=[_row(tr, D_MODEL), _row(tr, D_MODEL), _row(tr, 2 * D_MODEL, OFF_GATE // (2 * D_MODEL))],
        out_shape=[_sds((t_rows, D_MODEL), BF16), _sds((t_rows, D_MODEL), BF16), _sds(dproj.shape, dproj.dtype)],
        input_output_aliases={4: 2},
        name="mix_bwd", compiler_params=_params(1))(dmixed, proj, y_ssm, y_attn, dproj)


def _ssm_post_bwd(y, proj, dyn, w, dproj):
    t_rows = y.shape[0]
    tr = CHUNK

    def body(y_ref, z_ref, d_ref, w_ref, _, dy_ref, dz_ref, dw_ref):
        @pl.when(pl.program_id(0) == 0)
        def _():
            dw_ref[...] = jnp.zeros_like(dw_ref)

        yv, z = y_ref[...], z_ref[...].astype(F32)
        sz = _sigmoid(z)
        silu = z * sz
        yz = yv * silu
        dyz, dw = _rms_bwd(yz, _rms(yz), w_ref[...], d_ref[...].astype(F32))
        dw_ref[...] += dw
        dy_ref[...] = dyz * silu
        dz_ref[...] = (dyz * yv * sz * (1.0 + z * (1.0 - sz))).astype(BF16)

    return pl.pallas_call(
        body, grid=(t_rows // tr,),
        in_specs=[_row(tr, D_INNER), _row(tr, D_INNER, OFF_Z // D_INNER), _row(tr, D_INNER), _full((1, D_INNER)), _ANY],
        out_specs=[_row(tr, D_INNER), _row(tr, D_INNER, OFF_Z // D_INNER), _full((1, D_INNER))],
        out_shape=[_sds((t_rows, D_INNER), F32), _sds(dproj.shape, dproj.dtype), _sds((1, D_INNER), F32)],
        input_output_aliases={4: 1},
        name="ssm_post_bwd", compiler_params=_params(1))(y, proj, dyn, w, dproj)


def _ssm_conv_bwd(xc, proj, dxs, dbm, dcm, conv_w, dproj):
    t_rows = xc.shape[0]
    tr = CHUNK
    nt = t_rows // tr
    bc_w = SSM_GROUPS * D_STATE

    def body(xc_ref, x_ref, dxs_ref, db_ref, dc_ref, w_ref, _, dx_ref, dw_ref, dbias_ref, ahead):
        first = pl.program_id(0) == 0

        @pl.when(first)
        def _():
            dw_ref[...] = jnp.zeros_like(dw_ref)
            dbias_ref[...] = jnp.zeros_like(dbias_ref)

        for c0 in range(0, CONV_DIM, STRIP_BWD):
            cols = slice(c0, c0 + STRIP_BWD)
            if c0 < D_INNER:
                dact = dxs_ref[:, cols]
            elif c0 < D_INNER + bc_w:
                dact = db_ref[:, c0 - D_INNER:c0 - D_INNER + STRIP_BWD]
            else:
                dact = dc_ref[:, c0 - D_INNER - bc_w:c0 - D_INNER - bc_w + STRIP_BWD]
            c = xc_ref[:, cols]
            s = _sigmoid(c)
            dpre = dact * s * (1.0 + c * (1.0 - s))
            x = x_ref[:, cols]
            dx = None
            for sh, moved in enumerate(_anticausal_taps(dpre, ahead.at[:, cols], first, SSM_CONV)):
                k = SSM_CONV - 1 - sh
                term = w_ref[k:k + 1, cols] * moved
                dx = term if dx is None else dx + term
                dw_ref[k:k + 1, cols] += jnp.sum(moved * x, axis=0, keepdims=True)
            dbias_ref[:, cols] += jnp.sum(dpre, axis=0, keepdims=True)
            dx_ref[:, cols] = dx.astype(BF16)

    xbc_block = OFF_XBC // CONV_DIM
    return pl.pallas_call(
        body, grid=(nt,),
        in_specs=[_row_rev(tr, CONV_DIM, nt), _row_rev(tr, CONV_DIM, nt, xbc_block), _row_rev(tr, D_INNER, nt),
                  _row_rev(tr, bc_w, nt), _row_rev(tr, bc_w, nt), _full((SSM_CONV, CONV_DIM)), _ANY],
        out_specs=[_row_rev(tr, CONV_DIM, nt, xbc_block), _full((SSM_CONV, CONV_DIM)), _full((1, CONV_DIM))],
        out_shape=[_sds(dproj.shape, dproj.dtype), _sds((SSM_CONV, CONV_DIM), F32), _sds((1, CONV_DIM), F32)],
        scratch_shapes=[pltpu.VMEM((HALO, CONV_DIM), F32)],
        input_output_aliases={6: 0},
        name="ssm_conv_bwd", compiler_params=_params(1))(xc, proj, dxs, dbm, dcm, conv_w, dproj)


def _prenorm_bwd(x, meta, dhn, dh, w):
    seq = x.shape[0]
    tr = _pick(seq, (512, 128))

    def body(x_ref, meta_ref, d_ref, r_ref, d0_ref, r0_ref, w_ref, dx_ref, dmeta_ref, dw_ref):
        wv = w_ref[...]

        @pl.when(pl.program_id(0) == 0)
        def _():
            h0 = jnp.concatenate([jnp.zeros((META_PAD, D_MODEL), F32), meta_ref[...]], axis=0)
            dx0, dw0 = _rms_bwd(h0, _rms(h0), wv, d0_ref[...])
            dw_ref[...] = dw0
            dmeta_ref[...] = (r0_ref[...] + dx0)[META_PAD:, :]

        h = x_ref[...]
        dx, dw = _rms_bwd(h, _rms(h), wv, d_ref[...])
        dw_ref[...] += dw
        dx_ref[...] = r_ref[...] + dx

    def shifted(tile):
        return pl.BlockSpec((pl.Element(tile), pl.Element(D_MODEL)), lambda i: (pl.multiple_of(i * tile + CHUNK, CHUNK), 0))

    first = pl.BlockSpec((CHUNK, D_MODEL), lambda i: (0, 0))
    return pl.pallas_call(
        body, grid=(seq // tr,),
        in_specs=[_row(tr, D_MODEL), _full((N_META, D_MODEL)), shifted(tr), shifted(tr), first, first, _full((1, D_MODEL))],
        out_specs=[_row(tr, D_MODEL), _full((N_META, D_MODEL)), _full((1, D_MODEL))],
        out_shape=[_sds((seq, D_MODEL), F32), _sds((N_META, D_MODEL), F32), _sds((1, D_MODEL), F32)],
        name="prenorm_bwd", compiler_params=_params(1))(x, meta, dhn, dh, dhn, dh, w)


def _dot01(x, m01, x_left, parts):
    acc, rest = None, x
    for i in range(parts):
        piece = rest.astype(BF16)
        term = (jnp.dot(piece, m01, preferred_element_type=F32) if x_left
                else jnp.dot(m01, piece, preferred_element_type=F32))
        acc = term if acc is None else acc + term
        if i + 1 < parts:
            rest = rest - piece.astype(F32)
    return acc


def _ssd_common(dtr_ref, dt_bias, a_log, chunk_index):
    rows = lax.broadcasted_iota(jnp.int32, (CHUNK, CHUNK), 0)
    cols = lax.broadcasted_iota(jnp.int32, (CHUNK, CHUNK), 1)
    low = rows >= cols
    raw = dtr_ref[:, :128]
    for g in range(1, SSM_GROUPS):
        raw = raw + pltpu.roll(dtr_ref[:, g * 128:(g + 1) * 128], HEADS_PER_GROUP * g, 1)
    raw = raw + dt_bias
    live = _row_ids(raw.shape, chunk_index, CHUNK) >= META_PAD
    dt = jnp.where(live, _softplus(raw), 0.0)
    a_head = -jnp.exp(a_log)
    cs = _dot01(dt * a_head, low.astype(BF16), False, 3)
    return dict(low=low, triu=(rows <= cols).astype(BF16), raw=raw, live=live, dt=dt, a_head=a_head, cs=cs, cs_t=cs.T,
                grow=jnp.exp(cs),
                fade=jnp.exp(cs[CHUNK - 1:CHUNK, :] - cs))


def _ssd_expand(cm, g):
    first = HEADS_PER_GROUP * g
    expand = (lax.broadcasted_iota(jnp.int32, (CHUNK, GROUP_W), 1) // HEAD_P + first
              == lax.broadcasted_iota(jnp.int32, (CHUNK, GROUP_W), 0)).astype(BF16)
    fold = (lax.broadcasted_iota(jnp.int32, (GROUP_W, CHUNK), 0) // HEAD_P + first
            == lax.broadcasted_iota(jnp.int32, (GROUP_W, CHUNK), 1)).astype(BF16)
    return dict(fold=fold, dtx=_dot01(cm["dt"], expand, True, 2), growx=_dot01(cm["grow"], expand, True, 2),
                fadex=_dot01(cm["fade"], expand, True, 2))


def _decay_matrix(cm, j):
    diff = cm["cs"][:, j:j + 1] - cm["cs_t"][j:j + 1, :]
    return jnp.where(cm["low"], jnp.exp(jnp.where(cm["low"], diff, 0.0)), 0.0)


def _dot(a, b, dims):
    return lax.dot_general(a.astype(BF16), b.astype(BF16), (dims, ((), ())), preferred_element_type=F32)


def _dot_fine(a, b, dims):
    a_hi, b_hi = a.astype(BF16), b.astype(BF16)
    a_lo, b_lo = (a - a_hi.astype(F32)).astype(BF16), (b - b_hi.astype(F32)).astype(BF16)
    dn = (dims, ((), ()))
    return (lax.dot_general(a_hi, b_hi, dn, preferred_element_type=F32)
            + lax.dot_general(a_hi, b_lo, dn, preferred_element_type=F32)
            + lax.dot_general(a_lo, b_hi, dn, preferred_element_type=F32))


def _ssd_specs(nt, rev):
    def idx(c):
        return nt - 1 - c if rev else c
    bc_w = SSM_GROUPS * D_STATE
    xs = pl.BlockSpec((CHUNK, D_INNER), lambda c: (idx(c), 0))
    bm = pl.BlockSpec((CHUNK, bc_w), lambda c: (idx(c), D_INNER // bc_w))
    cm = pl.BlockSpec((CHUNK, bc_w), lambda c: (idx(c), D_INNER // bc_w + 1))
    dtr = pl.BlockSpec((CHUNK, SSM_GROUPS * 128), lambda c: (idx(c), OFF_DT // (SSM_GROUPS * 128)))
    par = _full((1, 128))
    par_x = _full((SSM_GROUPS, 1, GROUP_W))
    return xs, bm, cm, dtr, par, par_x, idx


def _group_cols(g, width):
    return slice(g * width, (g + 1) * width)


def _ssd_fwd(xact, proj, dtb, alog, dskip_x):
    t_rows = xact.shape[0]
    nt = t_rows // CHUNK
    xs_spec, b_spec, c_spec, dtr_spec, par, par_x, _ = _ssd_specs(nt, False)

    def body(xs_ref, b_ref, c_ref, dtr_ref, dtb_ref, alog_ref, dsk_ref, y_ref, hst_ref, state):
        c = pl.program_id(0)

        @pl.when(c == 0)
        def _():
            state[...] = jnp.zeros_like(state)

        cm = _ssd_common(dtr_ref, dtb_ref[...], alog_ref[...], c)
        for g in range(SSM_GROUPS):
            wide, narrow = _group_cols(g, GROUP_W), _group_cols(g, D_STATE)
            ex = _ssd_expand(cm, g)
            xs, bm, cmat = xs_ref[:, wide], b_ref[:, narrow], c_ref[:, narrow]
            x_dt = xs * ex["dtx"]
            h_in = state[g]
            hst_ref[0, g] = h_in
            y_ref[:, wide] = _dot(cmat, h_in, ((1,), (0,))) * ex["growx"] + xs * dsk_ref[g]
            cb = _dot(cmat, bm, ((1,), (1,)))
            for j in range(HEADS_PER_GROUP):
                sl = slice(g * GROUP_W + j * HEAD_P, g * GROUP_W + (j + 1) * HEAD_P)
                decay = _decay_matrix(cm, HEADS_PER_GROUP * g + j)
                y_ref[:, sl] += _dot(cb * decay, x_dt[:, j * HEAD_P:(j + 1) * HEAD_P], ((1,), (0,)))
            state[g] = h_in * ex["growx"][CHUNK - 1:CHUNK, :] + _dot_fine(bm, x_dt * ex["fadex"], ((0,), (0,)))

    return pl.pallas_call(
        body, grid=(nt,),
        in_specs=[xs_spec, b_spec, c_spec, dtr_spec, par, par, par_x],
        out_specs=[xs_spec, pl.BlockSpec((1, SSM_GROUPS, D_STATE, GROUP_W), lambda c: (c, 0, 0, 0))],
        out_shape=[_sds((t_rows, D_INNER), F32), _sds((nt, SSM_GROUPS, D_STATE, GROUP_W), F32)],
        scratch_shapes=[pltpu.VMEM((SSM_GROUPS, D_STATE, GROUP_W), F32)],
        name="ssd_fwd", compiler_params=_params(1))(xact, xact, xact, proj, dtb, alog, dskip_x)


def _ssd_bwd(xact, proj, dtb, alog, dskip_x, dy, hst, dproj):
    t_rows = xact.shape[0]
    nt = t_rows // CHUNK
    xs_spec, b_spec, c_spec, dtr_spec, par, par_x, idx = _ssd_specs(nt, True)
    h_spec = pl.BlockSpec((1, SSM_GROUPS, D_STATE, GROUP_W), lambda c: (idx(c), 0, 0, 0))
    hn_spec = pl.BlockSpec((1, SSM_GROUPS, D_STATE, GROUP_W), lambda c: (jnp.minimum(idx(c) + 1, nt - 1), 0, 0, 0))
    bc_out = pl.BlockSpec((CHUNK, SSM_GROUPS * D_STATE), lambda c: (idx(c), 0))

    def body(xs_ref, b_ref, c_ref, dtr_ref, dtb_ref, alog_ref, dsk_ref, dy_ref, h_ref, hn_ref, _,
             dxs_ref, db_ref, dc_ref, ddt_ref, dalog_ref, ddtb_ref, dd_ref, dstate, dx_buf):
        step = pl.program_id(0)

        @pl.when(step == 0)
        def _():
            dstate[...] = jnp.zeros_like(dstate)
            dalog_ref[...] = jnp.zeros_like(dalog_ref)
            ddtb_ref[...] = jnp.zeros_like(ddtb_ref)
            dd_ref[...] = jnp.zeros_like(dd_ref)

        cm = _ssd_common(dtr_ref, dtb_ref[...], alog_ref[...], idx(step))
        for g in range(SSM_GROUPS):
            _ssd_bwd_group(g, cm, xs_ref, b_ref, c_ref, dsk_ref, dy_ref, h_ref, hn_ref,
                           dxs_ref, db_ref, dc_ref, ddt_ref, dalog_ref, ddtb_ref, dd_ref, dstate, dx_buf)

    return pl.pallas_call(
        body, grid=(nt,),
        in_specs=[xs_spec, b_spec, c_spec, dtr_spec, par, par, par_x, xs_spec, h_spec, hn_spec, _ANY],
        out_specs=[xs_spec, bc_out, bc_out, dtr_spec, par, par, par_x],
        out_shape=[_sds((t_rows, D_INNER), F32), _sds((t_rows, SSM_GROUPS * D_STATE), F32),
                   _sds((t_rows, SSM_GROUPS * D_STATE), F32), _sds(dproj.shape, dproj.dtype),
                   _sds((1, 128), F32), _sds((1, 128), F32), _sds((SSM_GROUPS, 1, GROUP_W), F32)],
        scratch_shapes=[pltpu.VMEM((SSM_GROUPS, D_STATE, GROUP_W), F32), pltpu.VMEM((CHUNK, GROUP_W), F32)],
        input_output_aliases={10: 3},
        name="ssd_bwd", compiler_params=_params(1))(xact, xact, xact, proj, dtb, alog, dskip_x, dy, hst, hst, dproj)


def _ssd_bwd_group(g, cm, xs_ref, b_ref, c_ref, dsk_ref, dy_ref, h_ref, hn_ref,
                   dxs_ref, db_ref, dc_ref, ddt_ref, dalog_ref, ddtb_ref, dd_ref, dstate, dx_buf):
    wide, narrow = _group_cols(g, GROUP_W), _group_cols(g, D_STATE)
    first = HEADS_PER_GROUP * g
    ex = _ssd_expand(cm, g)
    xs, bm, cmat = xs_ref[:, wide], b_ref[:, narrow], c_ref[:, narrow]
    dsk = dsk_ref[g]
    x_dt = xs * ex["dtx"]
    h_in, h_next = h_ref[0, g], hn_ref[0, g]
    dyv = dy_ref[:, wide]
    dh = dstate[g]
    grow, fade = ex["growx"], ex["fadex"]
    dy_grow = dyv * grow
    x_fade = x_dt * fade
    cb = _dot(cmat, bm, ((1,), (1,)))
    ml = jnp.zeros((CHUNK, CHUNK), F32)
    row_id = lax.broadcasted_iota(jnp.int32, (CHUNK, CHUNK), 0)
    col_id = lax.broadcasted_iota(jnp.int32, (CHUNK, CHUNK), 1)
    w_rows = jnp.zeros((CHUNK, CHUNK), F32)
    w_cols = jnp.zeros((CHUNK, CHUNK), F32)
    for j in range(HEADS_PER_GROUP):
        sl = slice(j * HEAD_P, (j + 1) * HEAD_P)
        lm = _decay_matrix(cm, first + j)
        mlj = _dot(dyv[:, sl], x_dt[:, sl], ((1,), (1,))) * lm
        ml = ml + mlj
        wm = mlj * cb
        w_rows = jnp.where(col_id == first + j, jnp.sum(wm, axis=1, keepdims=True), w_rows)
        w_cols = jnp.where(row_id == first + j, jnp.sum(wm, axis=0, keepdims=True), w_cols)
        dx_buf[:, sl] = _dot(cb * lm, dyv[:, sl], ((0,), (0,)))
    dx_off = fade * _dot_fine(bm, dh, ((1,), (0,)))
    dx = dx_buf[...] + dx_off
    dc_ref[:, narrow] = _dot(ml, bm, ((1,), (0,))) + _dot(dy_grow, h_in, ((1,), (1,)))
    db_ref[:, narrow] = _dot(ml, cmat, ((0,), (0,))) + _dot(x_fade, dh, ((1,), (1,)))
    fold = ex["fold"]
    y_off = _dot_fine(cmat, h_in, ((1,), (0,))) * grow
    dcs = (w_rows - w_cols.T) + _dot01(dyv * y_off - x_dt * dx_off, fold, True, 2)
    tail = jnp.broadcast_to(jnp.sum(dh * h_next, axis=0, keepdims=True), (8, GROUP_W))
    tail = _dot01(tail, fold, True, 2)[0:1, :]
    last_row = lax.broadcasted_iota(jnp.int32, (CHUNK, 128), 0) == CHUNK - 1
    dcs = dcs + jnp.where(last_row, tail, 0.0)
    da = _dot01(dcs, cm["triu"], False, 3)
    ddt = da * cm["a_head"] + _dot01(dx * xs, fold, True, 2)
    ddt_raw = jnp.where(cm["live"], ddt * _sigmoid(cm["raw"]), 0.0)
    ddt_ref[:, narrow] = (ddt_raw if g == 0 else pltpu.roll(ddt_raw, 128 - first, 1)).astype(BF16)
    ddtb_ref[...] += jnp.sum(ddt_raw, axis=0, keepdims=True)
    dalog_ref[...] += jnp.sum(da * cm["dt"], axis=0, keepdims=True) * cm["a_head"]
    dd_ref[g] += jnp.sum(dyv * xs, axis=0, keepdims=True)
    dxs_ref[:, wide] = dx * ex["dtx"] + dyv * dsk
    dstate[g] = dh * grow[CHUNK - 1:CHUNK, :] + _dot_fine(cmat, dy_grow, ((0,), (0,)))


def _swa_bias():
    rows_q = ATTN_GROUP * CHUNK
    dist = (jnp.arange(rows_q) % CHUNK)[:, None] - jnp.arange(2 * CHUNK)[None, :] + CHUNK
    head = jnp.arange(KV_HEADS)[:, None] * ATTN_GROUP + jnp.arange(rows_q)[None, :] // CHUNK + 1
    slope = jnp.exp2(-8.0 * head.astype(F32) / ATTN_HEADS)
    return jnp.where((dist >= 0) & (dist < CHUNK), -slope[:, :, None] * dist.astype(F32)[None], NEG)


def _swa_probs(q_kv, k_prev, k_cur, k_first, sink, bias, n):
    rows_q = ATTN_GROUP * CHUNK
    qs = jnp.concatenate([q_kv[:, g * DH:(g + 1) * DH] for g in range(ATTN_GROUP)], axis=0) * (DH ** -0.5)
    kcat = jnp.concatenate([k_prev, k_cur], axis=0)
    kmeta = k_first[META_PAD:, :]
    key_ok = lax.broadcasted_iota(jnp.int32, (1, 2 * CHUNK), 1) + n * CHUNK >= 2 * CHUNK
    s_band = jnp.where(key_ok, _dot(qs, kcat, ((1,), (1,))) + bias, NEG)
    q_pos = lax.broadcasted_iota(jnp.int32, (rows_q, N_META), 0) % CHUNK + n * CHUNK - META_PAD
    ok_m = lax.broadcasted_iota(jnp.int32, (rows_q, N_META), 1) <= q_pos
    s_meta = jnp.where(ok_m, _dot(qs, kmeta, ((1,), (1,))), NEG)
    m = jnp.maximum(jnp.maximum(jnp.max(s_band, axis=1, keepdims=True), jnp.max(s_meta, axis=1, keepdims=True)), sink)
    p_band, p_meta, p_sink = jnp.exp(s_band - m), jnp.exp(s_meta - m), jnp.exp(sink - m)
    inv = 1.0 / (jnp.sum(p_band, axis=1, keepdims=True) + jnp.sum(p_meta, axis=1, keepdims=True) + p_sink)
    return qs, kcat, kmeta, p_band * inv, p_meta * inv, p_sink * inv


def _swa_specs(nt, rev):
    def idx(n):
        return nt - 1 - n if rev else n
    o = pl.BlockSpec((CHUNK, ATTN_HEADS * DH), lambda n: (idx(n), 0))
    chunks = (lambda c: jnp.maximum(c - 1, 0)), (lambda c: c), (lambda c: 0)
    qkv = [pl.BlockSpec((CHUNK, QKV_W), lambda n, f=f: (f(idx(n)), OFF_Q // QKV_W)) for f in chunks]
    sink = _full((KV_HEADS, ATTN_GROUP * CHUNK, 1))
    bias = _full((KV_HEADS, ATTN_GROUP * CHUNK, 2 * CHUNK))
    return o, qkv, sink, bias, idx


def _head_cols(k):
    kv_w = ATTN_GROUP * DH
    q0, k0, v0 = k * kv_w, OFF_K - OFF_Q + k * DH, OFF_V - OFF_Q + k * DH
    return slice(q0, q0 + kv_w), slice(k0, k0 + DH), slice(v0, v0 + DH)


def _swa_fwd(proj, sink_rows, bias):
    t_rows = proj.shape[0]
    nt = t_rows // CHUNK
    o_spec, qkv_specs, sink_spec, bias_spec, _ = _swa_specs(nt, False)
    kv_w = ATTN_GROUP * DH

    def body(prev_ref, cur_ref, first_ref, sink_ref, bias_ref, o_ref):
        n = pl.program_id(0)
        for k in range(KV_HEADS):
            qc, kc, vc = _head_cols(k)
            _, _, _, p_band, p_meta, _ = _swa_probs(cur_ref[:, qc], prev_ref[:, kc], cur_ref[:, kc], first_ref[:, kc],
                                                    sink_ref[k], bias_ref[k], n)
            vcat = jnp.concatenate([prev_ref[:, vc], cur_ref[:, vc]], axis=0)
            out = _dot(p_band, vcat, ((1,), (0,))) + _dot(p_meta, first_ref[:, vc][META_PAD:, :], ((1,), (0,)))
            for g in range(ATTN_GROUP):
                o_ref[:, k * kv_w + g * DH:k * kv_w + (g + 1) * DH] = out[g * CHUNK:(g + 1) * CHUNK, :]

    return pl.pallas_call(
        body, grid=(nt,), in_specs=qkv_specs + [sink_spec, bias_spec],
        out_specs=o_spec, out_shape=_sds((t_rows, ATTN_HEADS * DH), F32),
        name="swa_fwd", compiler_params=_params(1))(proj, proj, proj, sink_rows, bias)


def _swa_bwd(proj, sink_rows, bias, out, dout, dproj):
    t_rows = proj.shape[0]
    nt = t_rows // CHUNK
    o_spec, qkv_specs, sink_spec, bias_spec, idx = _swa_specs(nt, True)
    kv_w = ATTN_GROUP * DH
    k_off, v_off = OFF_K - OFF_Q, OFF_V - OFF_Q

    def body(prev_ref, cur_ref, first_ref, sink_ref, bias_ref, o_ref, do_ref, _, dqkv_ref, dsink_ref,
             carry_k, carry_v, meta_k, meta_v, dqkv_buf):
        step = pl.program_id(0)
        n = idx(step)

        @pl.when(step == 0)
        def _():
            carry_k[...] = jnp.zeros_like(carry_k)
            carry_v[...] = jnp.zeros_like(carry_v)
            meta_k[...] = jnp.zeros_like(meta_k)
            meta_v[...] = jnp.zeros_like(meta_v)
            dsink_ref[...] = jnp.zeros_like(dsink_ref)

        for k in range(KV_HEADS):
            cols = slice(k * kv_w, (k + 1) * kv_w)
            hd = slice(k * DH, (k + 1) * DH)
            qc, kc, vc = _head_cols(k)
            qs, kcat, kmeta, p_band, p_meta, p_sink = _swa_probs(cur_ref[:, qc], prev_ref[:, kc], cur_ref[:, kc],
                                                                 first_ref[:, kc], sink_ref[k], bias_ref[k], n)
            vcat = jnp.concatenate([prev_ref[:, vc], cur_ref[:, vc]], axis=0)
            vmeta = first_ref[:, vc][META_PAD:, :]
            o, do = o_ref[:, cols], do_ref[:, cols]
            os_ = jnp.concatenate([o[:, g * DH:(g + 1) * DH] for g in range(ATTN_GROUP)], axis=0)
            dos = jnp.concatenate([do[:, g * DH:(g + 1) * DH] for g in range(ATTN_GROUP)], axis=0)
            delta = jnp.sum(dos * os_, axis=1, keepdims=True)
            ds_band = p_band * (_dot(dos, vcat, ((1,), (1,))) - delta)
            ds_meta = p_meta * (_dot(dos, vmeta, ((1,), (1,))) - delta)
            ds_sink = -p_sink * delta
            dqs = (_dot(ds_band, kcat, ((1,), (0,))) + _dot(ds_meta, kmeta, ((1,), (0,)))) * (DH ** -0.5)
            for g in range(ATTN_GROUP):
                dqkv_buf[:, k * kv_w + g * DH:k * kv_w + (g + 1) * DH] = dqs[g * CHUNK:(g + 1) * CHUNK, :]
                dsink_ref[k, g:g + 1, :] += jnp.sum(ds_sink[g * CHUNK:(g + 1) * CHUNK, :])
            dkcat = _dot(ds_band, qs, ((0,), (0,)))
            dvcat = _dot(p_band, dos, ((0,), (0,)))
            meta_k[:, hd] += _dot(ds_meta, qs, ((0,), (0,)))
            meta_v[:, hd] += _dot(p_meta, dos, ((0,), (0,)))
            dqkv_buf[:, kc] = dkcat[CHUNK:, :] + carry_k[:, hd]
            dqkv_buf[:, vc] = dvcat[CHUNK:, :] + carry_v[:, hd]
            carry_k[:, hd] = dkcat[:CHUNK, :]
            carry_v[:, hd] = dvcat[:CHUNK, :]

        @pl.when(n == 0)
        def _():
            dqkv_buf[META_PAD:, k_off:k_off + KV_W] += meta_k[...]
            dqkv_buf[META_PAD:, v_off:v_off + KV_W] += meta_v[...]

        dqkv_ref[...] = dqkv_buf[...].astype(BF16)

    return pl.pallas_call(
        body, grid=(nt,),
        in_specs=qkv_specs + [sink_spec, bias_spec, o_spec, o_spec, pl.BlockSpec(memory_space=pl.ANY)],
        out_specs=[qkv_specs[1], _full((KV_HEADS, 8, 128))],
        out_shape=[_sds(dproj.shape, dproj.dtype), _sds((KV_HEADS, 8, 128), F32)],
        scratch_shapes=[pltpu.VMEM((CHUNK, KV_W), F32), pltpu.VMEM((CHUNK, KV_W), F32),
                        pltpu.VMEM((N_META, KV_W), F32), pltpu.VMEM((N_META, KV_W), F32),
                        pltpu.VMEM((CHUNK, QKV_W), F32)],
        input_output_aliases={7: 0},
        name="swa_bwd", compiler_params=_params(1))(proj, proj, proj, sink_rows, bias, out, dout, dproj)


def _pack_w_in_t(w_in_t):
    w_dt = w_in_t[CUT_DT:CUT_Q].reshape(SSM_GROUPS, HEADS_PER_GROUP, D_MODEL)
    w_dt = jnp.pad(w_dt, ((0, 0), (0, 128 - HEADS_PER_GROUP), (0, 0))).reshape(SSM_GROUPS * 128, D_MODEL)
    return jnp.concatenate([w_in_t[CUT_Z:CUT_XBC], w_in_t[CUT_G:], w_dt, w_in_t[CUT_Q:CUT_G], w_in_t[CUT_XBC:CUT_DT]], axis=0)


def _unpack_w_in_t(wp_t):
    w_dt = wp_t[OFF_DT:OFF_Q].reshape(SSM_GROUPS, 128, D_MODEL)[:, :HEADS_PER_GROUP].reshape(SSM_HEADS, D_MODEL)
    return jnp.concatenate([wp_t[OFF_Z:OFF_GATE], wp_t[OFF_XBC:], w_dt, wp_t[OFF_Q:OFF_XBC], wp_t[OFF_GATE:OFF_DT]], axis=0)


def _head_lanes(v):
    return jnp.pad(v.reshape(1, SSM_HEADS), ((0, 0), (0, 128 - SSM_HEADS)))


def _local_step(x, target, wt, late_weights=None, on_grad=None, started=None):
    seq = x.shape[0]
    grads = {}

    def emit(name, g):
        grads[name] = g
        return None if on_grad is None else on_grad(name, g)
    meta = wt["meta_tokens"]
    wp_t = _pack_w_in_t(wt["w_in_t"])
    dtb = _head_lanes(wt["ssm_dt_bias"].reshape(-1))
    alog = _head_lanes(wt["ssm_a_log"].reshape(-1))
    dskip_x = jnp.repeat(wt["ssm_d_skip"].reshape(-1), HEAD_P).reshape(SSM_GROUPS, 1, GROUP_W)
    sink_rows = jnp.repeat(wt["attn_sinks"].reshape(KV_HEADS, ATTN_GROUP), CHUNK, axis=1).reshape(KV_HEADS, ATTN_GROUP * CHUNK, 1)

    hn = _prenorm(x, meta, wt["norm_pre_mix"])
    proj = _matmul(hn, wp_t, tb=True, name="in_proj", after=started)
    xc, xact = _ssm_conv_fwd(proj, wt["ssm_conv_w"], wt["ssm_conv_b"])
    y, hst = _ssd_fwd(xact, proj, dtb, alog, dskip_x)
    yn = _ssm_post(y, proj, wt["ssm_norm"])
    if late_weights is not None:
        wt = {**wt, **late_weights(yn)}
    y_ssm = _matmul(yn, wt["w_ssm_out"], name="ssm_out")
    bias = _swa_bias()
    attn = _swa_fwd(proj, sink_rows, bias)
    y_attn = _matmul(attn, wt["w_attn_out"], name="attn_out")
    mixed = _mix_fwd(proj, y_ssm, y_attn)
    mix = _matmul(mixed, wt["w_mix_out"], name="mix_out")
    h1, hn2 = _postmix(x, meta, mix, wt["norm_post_mix"], wt["norm_pre_ffn"])
    up = _matmul(hn2, wt["w_ffn_up_t"], tb=True, out_dtype=BF16, name="ffn_up")
    u, act = _ffn_act(up, wt["ffn_conv_w"], wt["ffn_conv_b"])
    f = _matmul(act, wt["w_ffn_down"], name="ffn_down")
    df, dy, g_norm_post_ffn, loss_row = _final(h1, f, target, wt["norm_post_ffn"])

    grads["norm_post_ffn"] = g_norm_post_ffn
    sent = emit("w_ffn_down", _matmul(act, df, ta=True, out_dtype=BF16, name="dw_ffn_down"))
    dact = _matmul(df, wt["w_ffn_down"], tb=True, out_dtype=BF16, name="d_act", after=sent)
    dup, grads["ffn_conv_w"], grads["ffn_conv_b"] = _ffn_act_bwd(u, up, dact, wt["ffn_conv_w"])
    sent = emit("w_ffn_up_t", _matmul(dup, hn2, ta=True, out_dtype=BF16, name="dw_ffn_up"))
    dhn2 = _matmul(dup, wt["w_ffn_up_t"], name="d_hn2", after=sent)
    dmix, dh, grads["norm_pre_ffn"], grads["norm_post_mix"] = _postmix_bwd(h1, dhn2, dy, mix, wt["norm_pre_ffn"], wt["norm_post_mix"])
    sent = emit("w_mix_out", _matmul(mixed, dmix, ta=True, out_dtype=BF16, name="dw_mix_out"))
    dmixed = _matmul(dmix, wt["w_mix_out"], tb=True, name="d_mixed", after=sent)
    dy_ssm, dy_attn, dproj = _mix_bwd(dmixed, proj, y_ssm, y_attn, lax.empty(proj.shape, BF16))
    sent = emit("w_ssm_out", _matmul(yn, dy_ssm, ta=True, out_dtype=BF16, name="dw_ssm_out"))
    dyn = _matmul(dy_ssm, wt["w_ssm_out"], tb=True, out_dtype=BF16, name="d_yn", after=sent)
    sent = emit("w_attn_out", _matmul(attn, dy_attn, ta=True, out_dtype=BF16, name="dw_attn_out"))
    dattn = _matmul(dy_attn, wt["w_attn_out"], tb=True, name="d_attn", after=sent)
    dy_ssd, dproj, grads["ssm_norm"] = _ssm_post_bwd(y, proj, dyn, wt["ssm_norm"], dproj)
    dxs, dbm, dcm, dproj, dalog, ddtb, dd_x = _ssd_bwd(xact, proj, dtb, alog, dskip_x, dy_ssd, hst, dproj)
    grads["ssm_a_log"] = dalog[:, :SSM_HEADS]
    grads["ssm_dt_bias"] = ddtb[:, :SSM_HEADS]
    grads["ssm_d_skip"] = dd_x.reshape(SSM_HEADS, HEAD_P).sum(axis=1).reshape(1, SSM_HEADS)
    dproj, grads["ssm_conv_w"], grads["ssm_conv_b"] = _ssm_conv_bwd(xc, proj, dxs, dbm, dcm, wt["ssm_conv_w"], dproj)
    dproj, dsink = _swa_bwd(proj, sink_rows, bias, attn, dattn, dproj)
    grads["attn_sinks"] = dsink[:, :ATTN_GROUP, 0].reshape(1, ATTN_HEADS)
    sent = emit("w_in_t", _unpack_w_in_t(_matmul(dproj, hn, ta=True, out_dtype=BF16, name="dw_in")))
    dhn = _matmul(dproj, wp_t, name="d_hn", after=sent)
    grad_x, grads["meta_tokens"], grads["norm_pre_mix"] = _prenorm_bwd(x, meta, dhn, dh, wt["norm_pre_mix"])
    return loss_row[0, 0], grad_x, grads


def _all_gather(shards):
    n = len(shards)

    def body(*refs):
        ins, outs = refs[:n], refs[n:2 * n]
        send_sems, recv_sems, local_sems = refs[2 * n:]
        x, y, c = lax.axis_index("x"), lax.axis_index("y"), lax.axis_index("c")
        me, sibling = (x, y, c), (x, y, 1 - c)
        x_nbr, y_nbr, diag = (1 - x, y), (x, 1 - y), (1 - x, 1 - y)
        relayed = (x ^ (1 - c), y ^ c)
        relay_to = (x ^ c, y ^ (1 - c))

        def slot(a, dev):
            return outs[a].at[4 * dev[0] + 2 * dev[1] + dev[2]]

        def copy(k, a, block, to, src=None):
            return pltpu.make_async_remote_copy(
                src_ref=slot(a, block) if src is None else src, dst_ref=slot(a, block),
                send_sem=send_sems.at[k, a], recv_sem=recv_sems.at[k, a],
                device_id=to, device_id_type=pl.DeviceIdType.MESH)

        mine = [pltpu.make_async_copy(ins[a], slot(a, me), local_sems.at[a]) for a in range(n)]
        for cp in mine:
            cp.start()
        first = [copy(0, a, me, sibling, src=ins[a]) for a in range(n)]
        first += [copy(1, a, me, (*x_nbr, c), src=ins[a]) for a in range(n)]
        first += [copy(2, a, me, (*y_nbr, c), src=ins[a]) for a in range(n)]
        for cp in first:
            cp.start()
        passed = []

        def pass_on(k, block, to):
            for a in range(n):
                cp = copy(k, a, block, to)
                cp.start()
                passed.append(cp)

        for j, chip in enumerate((x_nbr, y_nbr)):
            for a in range(n):
                copy(1 + j, a, (*chip, c), me).wait_recv()
            pass_on(4 + j, (*chip, c), sibling)
        pass_on(3, (*relayed, c), (*relay_to, c))
        for a in range(n):
            copy(3, a, (*diag, c), me).wait_recv()
        pass_on(6, (*diag, c), sibling)
        for a in range(n):
            copy(0, a, sibling, me).wait_recv()
        for j, chip in enumerate((x_nbr, y_nbr, diag)):
            for a in range(n):
                copy(4 + j, a, (*chip, 1 - c), me).wait_recv()
        for cp in first + passed:
            cp.wait_send()
        for cp in mine:
            cp.wait()

    hbm = pl.BlockSpec(memory_space=pl.ANY)
    return pl.pallas_call(
        body, in_specs=[hbm] * n, out_specs=[hbm] * n,
        out_shape=[_sds((N_DEV,) + s.shape, s.dtype) for s in shards],
        scratch_shapes=[pltpu.SemaphoreType.DMA((7, n)), pltpu.SemaphoreType.DMA((7, n)), pltpu.SemaphoreType.DMA((n,))],
        name="gather_weights")(*shards)


def _peer_table():
    x, y, c = lax.axis_index("x"), lax.axis_index("y"), lax.axis_index("c")
    peers = []
    for k in range(N_DEV - 1):
        bits = k + 1
        p = (x ^ ((bits >> 2) & 1), y ^ ((bits >> 1) & 1), c ^ (bits & 1))
        peers.append((k, p, 4 * p[0] + 2 * p[1] + p[2]))
    return 4 * x + 2 * y + c, peers


_HBM = pl.BlockSpec(memory_space=pltpu.HBM)
_SEM = pl.BlockSpec(memory_space=pltpu.SEMAPHORE)
_EFFECT = pltpu.SideEffectType.DATAFLOW_SIDE_EFFECTING


def _push_copy(src, land, send_sems, recv_sems, a, k, p, src_slot, dst_slot):
    sem = a * (N_DEV - 1) + k
    return pltpu.make_async_remote_copy(
        src_ref=src[a] if src_slot is None else src[a].at[src_slot], dst_ref=land[a].at[dst_slot],
        send_sem=send_sems.at[sem], recv_sem=recv_sems.at[sem], device_id=p, device_id_type=pl.DeviceIdType.MESH)


def _push_start(srcs, scatter, name):
    n = len(srcs)
    lands = [lax.empty(s.shape if scatter else (N_DEV,) + s.shape, s.dtype) for s in srcs]

    def body(*refs):
        src, land = refs[:n], refs[n:2 * n]
        send_sems, recv_sems, token = refs[2 * n], refs[2 * n + 1], refs[-1]
        my_id, peers = _peer_table()
        for a in range(n):
            for k, p, p_id in peers:
                _push_copy(src, land, send_sems, recv_sems, a, k, p, p_id if scatter else None, my_id).start()
        token[...] = jnp.zeros_like(token)

    sems = pltpu.SemaphoreType.DMA(((N_DEV - 1) * n,))
    res = pl.pallas_call(
        body, name=name,
        out_shape=(sems, sems, *[pltpu.HBM(a.shape, a.dtype) for a in srcs + lands], _sds((8, 128), F32)),
        in_specs=[_HBM] * (2 * n), out_specs=(_SEM, _SEM, *[_HBM] * (2 * n), pl.BlockSpec(memory_space=pltpu.VMEM)),
        input_output_aliases={i: 2 + i for i in range(2 * n)},
        compiler_params=pltpu.CompilerParams(has_side_effects=_EFFECT),
    )(*[pltpu.with_memory_space_constraint(a, pltpu.HBM) for a in srcs + lands])
    return dict(send=res[0], recv=res[1], src=list(res[2:2 + n]), land=list(res[2 + n:2 + 2 * n]), token=res[-1],
                scatter=scatter)


def _push_wait(handle, after, name):
    n = len(handle["src"])
    scatter = handle["scatter"]

    def body(*refs):
        src, land = refs[:n], refs[n:2 * n]
        send_sems, recv_sems = refs[2 * n], refs[2 * n + 1]
        _, peers = _peer_table()
        for a in range(n):
            for k, p, p_id in peers:
                cp = _push_copy(src, land, send_sems, recv_sems, a, k, p, p_id if scatter else None, p_id)
                cp.wait_send()
                cp.wait_recv()

    arrays = handle["src"] + handle["land"]
    res = pl.pallas_call(
        body, name=name, out_shape=tuple(pltpu.HBM(a.shape, a.dtype) for a in arrays),
        in_specs=[_HBM] * (2 * n) + [_SEM, _SEM, pl.BlockSpec(memory_space=pl.ANY)], out_specs=tuple([_HBM] * (2 * n)),
        input_output_aliases={i: i for i in range(2 * n)},
        compiler_params=pltpu.CompilerParams(has_side_effects=_EFFECT),
    )(*arrays, handle["send"], handle["recv"], after)
    return list(res[:n]), list(res[n:])


def _slot_sum(p_ref, own_ref):
    if own_ref is not None:
        my_id = 4 * lax.axis_index("x") + 2 * lax.axis_index("y") + lax.axis_index("c")
        mine = own_ref[...].astype(F32)
    g = None
    for s in range(p_ref.shape[0]):
        term = p_ref[s].astype(F32)
        if own_ref is not None:
            term = jnp.where(my_id == s, mine, term)
        g = term if g is None else g + term
    return g


def _to_bf16(arrays):
    n = len(arrays)

    def body(*refs):
        for i in range(n):
            refs[n + i][...] = refs[i][...].astype(BF16)

    return pl.pallas_call(body, out_shape=[_sds(a.shape, BF16) for a in arrays], name="weights_to_bf16",
                          compiler_params=pltpu.CompilerParams(vmem_limit_bytes=VMEM_LIMIT))(*arrays)


def _adamw(parts, own, w, m, v, name):
    unit_rows = w.ndim == 3
    rows, cols = w.shape[0], w.shape[-1]
    if rows % 16 == 0:
        tr, tc = _pick(rows, (256, 128, 176, 64, 32, 16)), cols
    else:
        tr, tc = rows, _pick(cols, (256, 128))

    def body(*refs):
        if own is None:
            p_ref, w_ref, m_ref, v_ref, g_ref, d_ref, nm_ref, nv_ref = refs
            own_ref = None
        else:
            p_ref, own_ref, w_ref, m_ref, v_ref, g_ref, d_ref, nm_ref, nv_ref = refs
        g = _slot_sum(p_ref, own_ref)
        if unit_rows:
            g = g.reshape(tr, 1, tc)
        m_new = ADAM_B1 * m_ref[...] + (1.0 - ADAM_B1) * g
        v_new = ADAM_B2 * v_ref[...] + (1.0 - ADAM_B2) * (g * g)
        m_hat = m_new / (1.0 - ADAM_B1 ** ADAM_STEP)
        v_hat = v_new / (1.0 - ADAM_B2 ** ADAM_STEP)
        g_ref[...] = g
        d_ref[...] = -ADAM_LR * (m_hat / (jnp.sqrt(v_hat) + ADAM_EPS) + ADAM_WD * w_ref[...])
        nm_ref[...] = m_new
        nv_ref[...] = v_new

    by_rows = tc == cols
    spec = pl.BlockSpec((tr, tc), (lambda i: (i, 0)) if by_rows else (lambda i: (0, i)))
    state_spec = spec if not unit_rows else pl.BlockSpec((tr, 1, tc), (lambda i: (i, 0, 0)) if by_rows else (lambda i: (0, 0, i)))
    parts_spec = pl.BlockSpec((parts.shape[0], tr, tc), (lambda i: (0, i, 0)) if by_rows else (lambda i: (0, 0, i)))
    operands = (parts, w, m, v) if own is None else (parts, own, w, m, v)
    return pl.pallas_call(
        body, grid=(rows // tr if by_rows else cols // tc,),
        in_specs=[parts_spec] + ([] if own is None else [spec]) + [state_spec] * 3,
        out_specs=[state_spec] * 4, out_shape=[_sds(w.shape, F32)] * 4,
        name=name, compiler_params=_params(1))(*operands)


SMALL_REPLICATED = (("norm_pre_mix", 1024), ("ssm_conv_b", 3072), ("ssm_dt_bias", 32), ("ssm_a_log", 32),
                    ("ssm_d_skip", 32), ("ssm_norm", 2048), ("attn_sinks", 16), ("norm_post_mix", 1024),
                    ("norm_pre_ffn", 1024), ("ffn_conv_b", 5632), ("norm_post_ffn", 1024))
SMALL_SHARDED = (("meta_tokens", (N_META, D_MODEL // N_DEV)), ("ssm_conv_w", (SSM_CONV, CONV_DIM // N_DEV)),
                 ("ffn_conv_w", (FFN_CONV, 2 * FFN_DIM // N_DEV)))
BIG = (("w_in", (D_MODEL, N_IN // N_DEV), 1), ("w_ssm_out", (D_INNER // N_DEV, D_MODEL), 0),
       ("w_attn_out", (D_MODEL // N_DEV, D_MODEL), 0), ("w_mix_out", (D_MODEL // N_DEV, D_MODEL), 0),
       ("w_ffn_up", (D_MODEL, 2 * FFN_DIM // N_DEV), 1), ("w_ffn_down", (FFN_DIM // N_DEV, D_MODEL), 0))


def _rows_of(size):
    return -(-size // 128)


def _as_rows(flat):
    size = flat.shape[-1]
    rows = _rows_of(size)
    flat = jnp.pad(flat, [(0, 0)] * (flat.ndim - 1) + [(0, rows * 128 - size)])
    return flat.reshape(flat.shape[:-1] + (rows, 128))


def _pack_small(rep, sharded):
    pieces = [_as_rows(rep[name].reshape(-1)) for name, _ in SMALL_REPLICATED]
    pieces += [_as_rows(sharded[name].reshape(-1)) for name, _ in SMALL_SHARDED]
    packed = jnp.concatenate(pieces, axis=0)
    return jnp.pad(packed, ((0, -packed.shape[0] % 8), (0, 0)))


def _unpack_small(packed):
    out, row = {}, 0
    for name, size in SMALL_REPLICATED:
        out[name] = packed[row:row + _rows_of(size)].reshape(-1)[:size].reshape(1, size)
        row += _rows_of(size)
    for name, (r, c) in SMALL_SHARDED:
        out[name] = packed[row:row + _rows_of(r * c)].reshape(-1)[:r * c].reshape(r, c)
        row += _rows_of(r * c)
    return out


def _shard_major(g, shape, axis):
    r, c = shape
    if axis == 0:
        return g.reshape(N_DEV, r, c)
    return g.reshape(r, N_DEV, c).transpose(1, 0, 2)


def kernel(x, meta_tokens, norm_pre_mix, w_in, ssm_conv_w, ssm_conv_b, ssm_dt_bias, ssm_a_log, ssm_d_skip, ssm_norm, w_ssm_out, attn_sinks, w_attn_out, w_mix_out, norm_post_mix, norm_pre_ffn, w_ffn_up, ffn_conv_w, ffn_conv_b, w_ffn_down, norm_post_ffn, loss_target, m_meta_tokens, m_norm_pre_mix, m_w_in, m_ssm_conv_w, m_ssm_conv_b, m_ssm_dt_bias, m_ssm_a_log, m_ssm_d_skip, m_ssm_norm, m_w_ssm_out, m_attn_sinks, m_w_attn_out, m_w_mix_out, m_norm_post_mix, m_norm_pre_ffn, m_w_ffn_up, m_ffn_conv_w, m_ffn_conv_b, m_w_ffn_down, m_norm_post_ffn, v_meta_tokens, v_norm_pre_mix, v_w_in, v_ssm_conv_w, v_ssm_conv_b, v_ssm_dt_bias, v_ssm_a_log, v_ssm_d_skip, v_ssm_norm, v_w_ssm_out, v_attn_sinks, v_w_attn_out, v_w_mix_out, v_norm_post_mix, v_norm_pre_ffn, v_w_ffn_up, v_ffn_conv_w, v_ffn_conv_b, v_w_ffn_down, v_norm_post_ffn):
    names = ("meta_tokens", "norm_pre_mix", "w_in", "ssm_conv_w", "ssm_conv_b", "ssm_dt_bias", "ssm_a_log", "ssm_d_skip",
             "ssm_norm", "w_ssm_out", "attn_sinks", "w_attn_out", "w_mix_out", "norm_post_mix", "norm_pre_ffn", "w_ffn_up",
             "ffn_conv_w", "ffn_conv_b", "w_ffn_down", "norm_post_ffn")
    w_loc = dict(zip(names, (meta_tokens, norm_pre_mix, w_in, ssm_conv_w, ssm_conv_b, ssm_dt_bias, ssm_a_log, ssm_d_skip,
                             ssm_norm, w_ssm_out, attn_sinks, w_attn_out, w_mix_out, norm_post_mix, norm_pre_ffn, w_ffn_up,
                             ffn_conv_w, ffn_conv_b, w_ffn_down, norm_post_ffn)))
    m_loc = dict(zip(names, (m_meta_tokens, m_norm_pre_mix, m_w_in, m_ssm_conv_w, m_ssm_conv_b, m_ssm_dt_bias, m_ssm_a_log,
                             m_ssm_d_skip, m_ssm_norm, m_w_ssm_out, m_attn_sinks, m_w_attn_out, m_w_mix_out, m_norm_post_mix,
                             m_norm_pre_ffn, m_w_ffn_up, m_ffn_conv_w, m_ffn_conv_b, m_w_ffn_down, m_norm_post_ffn)))
    v_loc = dict(zip(names, (v_meta_tokens, v_norm_pre_mix, v_w_in, v_ssm_conv_w, v_ssm_conv_b, v_ssm_dt_bias, v_ssm_a_log,
                             v_ssm_d_skip, v_ssm_norm, v_w_ssm_out, v_attn_sinks, v_w_attn_out, v_w_mix_out, v_norm_post_mix,
                             v_norm_pre_ffn, v_w_ffn_up, v_ffn_conv_w, v_ffn_conv_b, v_w_ffn_down, v_norm_post_ffn)))

    def local2d(d, name):
        a = d[name]
        return a if name == "meta_tokens" else a.reshape(a.shape[1:])

    def turned2d(d, name):
        a = jnp.swapaxes(d[name], 1, 2)
        return a.reshape(a.shape[1:])

    my_id = 4 * lax.axis_index("x") + 2 * lax.axis_index("y") + lax.axis_index("c")
    big = {name: (shape, axis) for name, shape, axis in BIG}

    def whole(name, g):
        return g.reshape(N_DEV * g.shape[1], g.shape[2])

    def key(name):
        return name + "_t" if big[name][1] == 1 else name

    by_rows = [name for name, _, axis in BIG if axis == 0]
    send_bf16 = dict(zip(by_rows, _to_bf16([local2d(w_loc, name) for name in by_rows])))
    for name, _, axis in BIG:
        if axis == 1:
            send_bf16[name] = turned2d(w_loc, name).astype(BF16)
    small_shard_pack = jnp.concatenate([_as_rows(local2d(w_loc, name).reshape(-1)) for name, _ in SMALL_SHARDED], axis=0)
    small_shard_pack = jnp.pad(small_shard_pack, ((0, -small_shard_pack.shape[0] % 8), (0, 0)))
    first = _all_gather([send_bf16["w_in"], small_shard_pack])
    rest_names = [name for name, _, _ in BIG if name != "w_in"]
    rest = [send_bf16[name] for name in rest_names]
    rest, first = lax.optimization_barrier((rest, first))
    rest_handle = _push_start(rest, False, "gather_rest_start")
    wt = {"w_in_t": whole("w_in", first[0])}
    row = 0
    for name, (r, c) in SMALL_SHARDED:
        blocks = first[1][:, row:row + _rows_of(r * c)].reshape(N_DEV, -1)[:, :r * c].reshape(N_DEV, r, c)
        wt[name] = blocks.transpose(1, 0, 2).reshape(r, N_DEV * c)
        row += _rows_of(r * c)
    for name, size in SMALL_REPLICATED:
        wt[name] = w_loc[name].reshape(1, size)

    def late_weights(after):
        own, landed = _push_wait(rest_handle, after, "gather_rest_wait")
        out = {}
        for name, mine, land in zip(rest_names, own, landed):
            out[key(name)] = whole(name, lax.dynamic_update_index_in_dim(land, mine, my_id, 0))
        return out

    sent = {}

    def on_grad(known_as, g):
        name = known_as.removesuffix("_t")
        by_owner = g.reshape(N_DEV, g.shape[0] // N_DEV, g.shape[1])
        sent[name] = _push_start([by_owner], True, "send_" + name)
        return sent[name]["token"]

    loss_part, grad_x, grads = _local_step(x[0], loss_target[0], wt, late_weights, on_grad, rest_handle["token"])
    loss = lax.psum(loss_part, AXES)

    small_parts = []
    for name, (r, c) in SMALL_SHARDED:
        small_parts.append(_as_rows(_shard_major(grads[name], (r, c), 1).reshape(N_DEV, r * c)))
    rep_rows = jnp.concatenate([_as_rows(grads[name].reshape(-1)) for name, _ in SMALL_REPLICATED], axis=0)
    small_send = jnp.concatenate([jnp.broadcast_to(rep_rows[None], (N_DEV,) + rep_rows.shape)] + small_parts, axis=1)
    small_send = jnp.pad(small_send, ((0, 0), (0, -small_send.shape[1] % 8), (0, 0)))
    small_handle = _push_start([small_send], True, "send_small")

    def small_pack(d):
        return _pack_small({name: d[name] for name, _ in SMALL_REPLICATED}, {name: local2d(d, name) for name, _ in SMALL_SHARDED})

    def arrived(handle, after, name):
        src, landed = _push_wait(handle, after, "arrived_" + name)
        return landed[0], lax.dynamic_index_in_dim(src[0], my_id, 0, keepdims=False)

    grad_w, delta_w, new_m, new_v = {}, {}, {}, {}
    outs = None
    after = small_handle["token"]
    for name, handle in sent.items():
        if name == "w_in":
            parts, own = arrived(small_handle, after, "small")
            outs = _adamw(parts, own, small_pack(w_loc), small_pack(m_loc), small_pack(v_loc), "adamw_small")
            after = outs[0]
        parts, own = arrived(handle, after, name)
        turned = big[name][1] == 1
        unit_rows = turned and big[name][0][1] % 8 != 0
        if unit_rows:
            state = [jnp.transpose(d[name], (2, 0, 1)) for d in (w_loc, m_loc, v_loc)]
        else:
            state = [turned2d(d, name) if turned else local2d(d, name) for d in (w_loc, m_loc, v_loc)]
        results = _adamw(parts, own, *state, "adamw_" + name)
        after = results[0]
        full = (1,) + big[name][0]
        for dst, a in zip((grad_w, delta_w, new_m, new_v), results):
            if unit_rows:
                dst[name] = jnp.transpose(a, (1, 2, 0))
            else:
                dst[name] = jnp.swapaxes(a[None], 1, 2) if turned else a.reshape(full)
    for dst, packed in zip((grad_w, delta_w, new_m, new_v), outs):
        for name, a in _unpack_small(packed).items():
            dst[name] = a.reshape(w_loc[name].shape)

    return (loss, grad_x[None], *[grad_w[n] for n in names], *[delta_w[n] for n in names],
            *[new_m[n] for n in names], *[new_v[n] for n in names])
```

```python
import jax
import jax.numpy as jnp
from jax import lax
from jax.experimental import pallas as pl
from jax.experimental.pallas import tpu as pltpu

F32 = jnp.float32
BF16 = jnp.bfloat16

D_MODEL = 1024
N_META = 16
CHUNK = 128
META_PAD = CHUNK - N_META
D_INNER = 2048
HEAD_P = 64
SSM_HEADS = 32
SSM_GROUPS = 4
HEADS_PER_GROUP = SSM_HEADS // SSM_GROUPS
GROUP_W = HEADS_PER_GROUP * HEAD_P
D_STATE = 128
SSM_CONV = 4
CONV_DIM = D_INNER + 2 * SSM_GROUPS * D_STATE
ATTN_HEADS = 16
KV_HEADS = 4
ATTN_GROUP = ATTN_HEADS // KV_HEADS
DH = 64
KV_W = KV_HEADS * DH
FFN_DIM = 2816
FFN_CONV = 3
EPS = 1e-6
NEG = -1e30
N_DEV = 8
AXES = ("x", "y", "c")

OFF_Z, OFF_GATE, OFF_DT, OFF_Q, OFF_K, OFF_V, OFF_XBC = 0, 2048, 4096, 4608, 5632, 5888, 6144
N_INP = OFF_XBC + CONV_DIM
QKV_W = OFF_XBC - OFF_Q
CUT_Z, CUT_XBC, CUT_DT, CUT_Q, CUT_K, CUT_V, CUT_G = 0, 2048, 5120, 5152, 6176, 6432, 6688
N_IN = 8736

ADAM_LR, ADAM_B1, ADAM_B2, ADAM_EPS, ADAM_WD, ADAM_STEP = 0.001, 0.9, 0.999, 1e-08, 0.01, 10

VMEM_LIMIT = 56 * 1024 * 1024


def _params(n_grid):
    return pltpu.CompilerParams(dimension_semantics=("arbitrary",) * n_grid, vmem_limit_bytes=VMEM_LIMIT)


def _sds(shape, dtype):
    return jax.ShapeDtypeStruct(shape, dtype)


def _pick(n, prefs):
    for c in prefs:
        if n % c == 0:
            return c
    raise ValueError(f"no tile of {prefs} divides {n}")


def _row(tr, width, cb=0):
    return pl.BlockSpec((tr, width), lambda i: (i, cb))


def _row_rev(tr, width, nt, cb=0):
    return pl.BlockSpec((tr, width), lambda i: (nt - 1 - i, cb))


def _full(shape):
    return pl.BlockSpec(shape, lambda *_: (0,) * len(shape))


def _sigmoid(x):
    return 1.0 / (1.0 + jnp.exp(-x))


def _softplus(x):
    return jnp.maximum(x, 0.0) + jnp.log(1.0 + jnp.exp(-jnp.abs(x)))


def _rms(x):
    return lax.rsqrt(jnp.mean(x * x, axis=-1, keepdims=True) + EPS)


def _rms_bwd(x, r, w, dy):
    xh = x * r
    g = dy * w
    dx = r * (g - xh * jnp.mean(g * xh, axis=-1, keepdims=True))
    return dx, jnp.sum(dy * xh, axis=0, keepdims=True)


def _row_ids(shape, tile_index, tr):
    return tile_index * tr + lax.broadcasted_iota(jnp.int32, shape, 0)


HALO = 8
STRIP = 256
STRIP_BWD = 128


def _causal_taps(x, halo, first_step, taps):
    n = x.shape[0]

    @pl.when(first_step)
    def _():
        halo[...] = jnp.zeros_like(halo)

    before = halo[...]
    row = lax.broadcasted_iota(jnp.int32, before.shape, 0)
    shifted = [x]
    for s in range(1, taps):
        rolled = pltpu.roll(x, s, 0)
        head = jnp.where(row < s, pltpu.roll(before, s, 0), rolled[0:HALO, :])
        shifted.append(jnp.concatenate([head, rolled[HALO:, :]], axis=0))
    halo[...] = x[n - HALO:, :]
    return shifted


def _anticausal_taps(x, halo, first_step, taps):
    n = x.shape[0]

    @pl.when(first_step)
    def _():
        halo[...] = jnp.zeros_like(halo)

    after = halo[...]
    row = lax.broadcasted_iota(jnp.int32, after.shape, 0)
    shifted = [x]
    for s in range(1, taps):
        rolled = pltpu.roll(x, n - s, 0)
        tail = jnp.where(row >= HALO - s, pltpu.roll(after, HALO - s, 0), rolled[n - HALO:, :])
        shifted.append(jnp.concatenate([rolled[:n - HALO, :], tail], axis=0))
    halo[...] = x[0:HALO, :]
    return shifted


def _matmul(a, b, *, ta=False, tb=False, out_dtype=F32, name, after=None):
    if ta:
        k_dim, m_dim = a.shape
    else:
        m_dim, k_dim = a.shape
    n_dim = b.shape[0] if tb else b.shape[1]
    tm = _pick(m_dim, (1408, 1024, 768, 512, 384, 256, 128))
    tn = _pick(n_dim, (1024, 1408, 768, 512, 384, 256, 128))
    if ta:
        tk = _pick(k_dim, (1408, 1024, 768, 512, 384, 256, 128))
    else:
        tk = k_dim if k_dim <= 3072 else _pick(k_dim, (3072, 2816, 2048, 1024))
    nk = k_dim // tk
    dims = (((0 if ta else 1,), (1 if tb else 0,)), ((), ()))

    use_acc = nk > 1 and out_dtype != F32

    def body(a_ref, b_ref, *rest):
        o_ref = rest[-2] if use_acc else rest[-1]
        acc_ref = rest[-1] if use_acc else o_ref
        r = lax.dot_general(a_ref[...].astype(BF16), b_ref[...].astype(BF16), dims, preferred_element_type=F32)
        if nk == 1:
            o_ref[...] = r.astype(o_ref.dtype)
        else:
            k = pl.program_id(2)

            @pl.when(k == 0)
            def _():
                acc_ref[...] = r

            @pl.when(k > 0)
            def _():
                acc_ref[...] += r

            if use_acc:
                @pl.when(k == nk - 1)
                def _():
                    o_ref[...] = acc_ref[...].astype(o_ref.dtype)

    a_spec = pl.BlockSpec((tk, tm), lambda i, j, k: (k, i)) if ta else pl.BlockSpec((tm, tk), lambda i, j, k: (i, k))
    b_spec = pl.BlockSpec((tn, tk), lambda i, j, k: (j, k)) if tb else pl.BlockSpec((tk, tn), lambda i, j, k: (k, j))
    extra_specs, extra = ([], ()) if after is None else ([pl.BlockSpec(memory_space=pl.ANY)], (after,))
    return pl.pallas_call(
        body, grid=(m_dim // tm, n_dim // tn, nk), in_specs=[a_spec, b_spec] + extra_specs,
        out_specs=pl.BlockSpec((tm, tn), lambda i, j, k: (i, j)), out_shape=_sds((m_dim, n_dim), out_dtype),
        scratch_shapes=[pltpu.VMEM((tm, tn), F32)] if use_acc else [],
        name=name, compiler_params=_params(3))(a, b, *extra)


def _seq_rows(t_rows):
    return 384 if t_rows % 384 == 0 and t_rows >= 768 else CHUNK


def _token_rows(tr):
    if tr == CHUNK:
        return pl.BlockSpec((CHUNK, D_MODEL), lambda i: (jnp.maximum(i - 1, 0), 0))
    return pl.BlockSpec((pl.Element(tr), pl.Element(D_MODEL)),
                        lambda i: (pl.multiple_of(jnp.maximum(i * tr - CHUNK, 0), CHUNK), 0))


def _under_tile(rows_ref, head, i):
    rows = rows_ref[...]
    tr = rows.shape[0]
    first = head if tr == CHUNK else jnp.concatenate([head, rows[0:tr - CHUNK, :]], axis=0)
    return jnp.where(i == 0, first, rows)


def _seq_specs(tr=CHUNK):
    return [_token_rows(tr), _full((N_META, D_MODEL))]


def _seq_tile(x_ref, meta_ref, i):
    return _under_tile(x_ref, jnp.concatenate([jnp.zeros((META_PAD, D_MODEL), F32), meta_ref[...]], axis=0), i)


def _prenorm(x, meta, w):
    t_rows = x.shape[0] + CHUNK
    tr = _seq_rows(t_rows)

    def body(x_ref, meta_ref, w_ref, o_ref):
        h = _seq_tile(x_ref, meta_ref, pl.program_id(0))
        o_ref[...] = (h * _rms(h) * w_ref[...]).astype(BF16)

    return pl.pallas_call(body, grid=(t_rows // tr,), in_specs=_seq_specs(tr) + [_full((1, D_MODEL))],
                          out_specs=_row(tr, D_MODEL), out_shape=_sds((t_rows, D_MODEL), BF16),
                          name="prenorm", compiler_params=_params(1))(x, meta, w)


def _ssm_conv_fwd(proj, conv_w, conv_b):
    t_rows = proj.shape[0]
    tr = CHUNK

    def body(x_ref, w_ref, b_ref, xc_ref, xa_ref, hist):
        first = pl.program_id(0) == 0
        for c in range(0, CONV_DIM, STRIP):
            cols = slice(c, c + STRIP)
            acc = b_ref[:, cols]
            for s, moved in enumerate(_causal_taps(x_ref[:, cols], hist.at[:, cols], first, SSM_CONV)):
                acc = acc + w_ref[SSM_CONV - 1 - s:SSM_CONV - s, cols] * moved
            xc_ref[:, cols] = acc
            xa_ref[:, cols] = acc * _sigmoid(acc)

    return pl.pallas_call(
        body, grid=(t_rows // tr,),
        in_specs=[_row(tr, CONV_DIM, OFF_XBC // CONV_DIM), _full((SSM_CONV, CONV_DIM)), _full((1, CONV_DIM))],
        out_specs=[_row(tr, CONV_DIM), _row(tr, CONV_DIM)],
        out_shape=[_sds((t_rows, CONV_DIM), F32), _sds((t_rows, CONV_DIM), F32)],
        scratch_shapes=[pltpu.VMEM((HALO, CONV_DIM), F32)],
        name="ssm_conv_fwd", compiler_params=_params(1))(proj, conv_w, conv_b)


def _ssm_post(y, proj, w):
    t_rows = y.shape[0]
    tr = _pick(t_rows, (384, 128))

    def body(y_ref, z_ref, w_ref, o_ref):
        z = z_ref[...].astype(F32)
        yz = y_ref[...] * z * _sigmoid(z)
        o_ref[...] = (yz * _rms(yz) * w_ref[...]).astype(BF16)

    return pl.pallas_call(body, grid=(t_rows // tr,),
                          in_specs=[_row(tr, D_INNER), _row(tr, D_INNER, OFF_Z // D_INNER), _full((1, D_INNER))],
                          out_specs=_row(tr, D_INNER), out_shape=_sds((t_rows, D_INNER), BF16),
                          name="ssm_post", compiler_params=_params(1))(y, proj, w)


def _mix_fwd(proj, y_ssm, y_attn):
    t_rows = y_ssm.shape[0]
    tr = _pick(t_rows, (384, 128))

    def body(g_ref, ys_ref, ya_ref, o_ref):
        g = _sigmoid(g_ref[...].astype(F32))
        o_ref[...] = (g[:, :D_MODEL] * ys_ref[...] + g[:, D_MODEL:] * ya_ref[...]).astype(BF16)

    return pl.pallas_call(body, grid=(t_rows // tr,),
                          in_specs=[_row(tr, 2 * D_MODEL, OFF_GATE // (2 * D_MODEL)), _row(tr, D_MODEL),
                                    _row(tr, D_MODEL)],
                          out_specs=_row(tr, D_MODEL), out_shape=_sds((t_rows, D_MODEL), BF16),
                          name="mix_fwd", compiler_params=_params(1))(proj, y_ssm, y_attn)


def _postmix(x, meta, mix, w_post, w_pre):
    t_rows = mix.shape[0]
    tr = _seq_rows(t_rows)

    def body(x_ref, meta_ref, m_ref, wp_ref, wf_ref, h1_ref, hn_ref):
        m = m_ref[...]
        h1 = _seq_tile(x_ref, meta_ref, pl.program_id(0)) + m * _rms(m) * wp_ref[...]
        h1 = jnp.where(_row_ids(h1.shape, pl.program_id(0), tr) >= META_PAD, h1, 0.0)
        h1_ref[...] = h1
        hn_ref[...] = (h1 * _rms(h1) * wf_ref[...]).astype(BF16)

    return pl.pallas_call(body, grid=(t_rows // tr,),
                          in_specs=_seq_specs(tr) + [_row(tr, D_MODEL), _full((1, D_MODEL)), _full((1, D_MODEL))],
                          out_specs=[_row(tr, D_MODEL), _row(tr, D_MODEL)],
                          out_shape=[_sds((t_rows, D_MODEL), F32), _sds((t_rows, D_MODEL), BF16)],
                          name="postmix", compiler_params=_params(1))(x, meta, mix, w_post, w_pre)


def _ffn_act(up, conv_w, conv_b):
    t_rows = up.shape[0]
    tr = CHUNK
    width = 2 * FFN_DIM

    def body(up_ref, w_ref, b_ref, u_ref, act_ref, hist):
        first = pl.program_id(0) == 0
        for c in range(0, FFN_DIM, STRIP):
            halves = []
            for base in (0, FFN_DIM):
                cols = slice(base + c, base + c + STRIP)
                u = b_ref[:, cols]
                for s, moved in enumerate(_causal_taps(up_ref[:, cols].astype(F32), hist.at[:, cols], first, FFN_CONV)):
                    u = u + w_ref[FFN_CONV - 1 - s:FFN_CONV - s, cols] * moved
                u_ref[:, cols] = u.astype(BF16)
                halves.append(u)
            a, g = halves
            act_ref[:, c:c + STRIP] = (a * _sigmoid(a) * g).astype(BF16)

    return pl.pallas_call(
        body, grid=(t_rows // tr,), in_specs=[_row(tr, width), _full((FFN_CONV, width)), _full((1, width))],
        out_specs=[_row(tr, width), _row(tr, FFN_DIM)],
        out_shape=[_sds((t_rows, width), BF16), _sds((t_rows, FFN_DIM), BF16)],
        scratch_shapes=[pltpu.VMEM((HALO, width), F32)],
        name="ffn_act", compiler_params=_params(1))(up, conv_w, conv_b)


def _final(h1, f, target, w):
    t_rows = h1.shape[0]
    tr = _seq_rows(t_rows)

    def body(h1_ref, f_ref, t_ref, w_ref, df_ref, dy_ref, dw_ref, loss_ref):
        i = pl.program_id(0)

        @pl.when(i == 0)
        def _():
            dw_ref[...] = jnp.zeros_like(dw_ref)
            loss_ref[...] = jnp.zeros_like(loss_ref)

        f_val = f_ref[...]
        r = _rms(f_val)
        wv = w_ref[...]
        h2 = h1_ref[...] + f_val * r * wv
        tgt = _under_tile(t_ref, jnp.zeros((CHUNK, D_MODEL), F32), i)
        diff = jnp.where(_row_ids(h2.shape, i, tr) >= CHUNK, h2 - tgt, 0.0)
        loss_ref[...] += 0.5 * jnp.sum(diff * diff) * (1.0 / D_MODEL)
        dy = diff * (1.0 / D_MODEL)
        dy_ref[...] = dy
        df, dw = _rms_bwd(f_val, r, wv, dy)
        df_ref[...] = df.astype(BF16)
        dw_ref[...] += dw

    return pl.pallas_call(
        body, grid=(t_rows // tr,),
        in_specs=[_row(tr, D_MODEL), _row(tr, D_MODEL), _token_rows(tr), _full((1, D_MODEL))],
        out_specs=[_row(tr, D_MODEL), _row(tr, D_MODEL), _full((1, D_MODEL)), _full((1, 128))],
        out_shape=[_sds((t_rows, D_MODEL), BF16), _sds((t_rows, D_MODEL), F32), _sds((1, D_MODEL), F32), _sds((1, 128), F32)],
        name="final", compiler_params=_params(1))(h1, f, target, w)


def _ffn_act_bwd(u, up, dact, conv_w):
    t_rows = u.shape[0]
    tr = CHUNK
    nt = t_rows // tr
    width = 2 * FFN_DIM

    def body(u_ref, up_ref, da_ref, w_ref, dup_ref, dw_ref, db_ref, ahead):
        @pl.when(pl.program_id(0) == 0)
        def _():
            dw_ref[...] = jnp.zeros_like(dw_ref)
            db_ref[...] = jnp.zeros_like(db_ref)

        first = pl.program_id(0) == 0
        for c in range(0, FFN_DIM, STRIP_BWD):
            ca, cg = slice(c, c + STRIP_BWD), slice(FFN_DIM + c, FFN_DIM + c + STRIP_BWD)
            a, g, d = u_ref[:, ca].astype(F32), u_ref[:, cg].astype(F32), da_ref[:, ca].astype(F32)
            s = _sigmoid(a)
            for cols, du in ((ca, d * g * s * (1.0 + a * (1.0 - s))), (cg, d * a * s)):
                x = up_ref[:, cols].astype(F32)
                dup = None
                for sh, moved in enumerate(_anticausal_taps(du, ahead.at[:, cols], first, FFN_CONV)):
                    k = FFN_CONV - 1 - sh
                    term = w_ref[k:k + 1, cols] * moved
                    dup = term if dup is None else dup + term
                    dw_ref[k:k + 1, cols] += jnp.sum(moved * x, axis=0, keepdims=True)
                db_ref[:, cols] += jnp.sum(du, axis=0, keepdims=True)
                dup_ref[:, cols] = dup.astype(BF16)

    return pl.pallas_call(
        body, grid=(nt,),
        in_specs=[_row_rev(tr, width, nt), _row_rev(tr, width, nt), _row_rev(tr, FFN_DIM, nt), _full((FFN_CONV, width))],
        out_specs=[_row_rev(tr, width, nt), _full((FFN_CONV, width)), _full((1, width))],
        out_shape=[_sds((t_rows, width), BF16), _sds((FFN_CONV, width), F32), _sds((1, width), F32)],
        scratch_shapes=[pltpu.VMEM((HALO, width), F32)],
        name="ffn_act_bwd", compiler_params=_params(1))(u, up, dact, conv_w)


def _postmix_bwd(h1, dhn2, dy, mix, w_pre, w_post):
    t_rows = h1.shape[0]
    tr = _pick(t_rows, (384, 128))

    def body(h1_ref, dhn_ref, dy_ref, m_ref, wf_ref, wp_ref, dmix_ref, dh_ref, dwf_ref, dwp_ref):
        @pl.when(pl.program_id(0) == 0)
        def _():
            dwf_ref[...] = jnp.zeros_like(dwf_ref)
            dwp_ref[...] = jnp.zeros_like(dwp_ref)

        h1v = h1_ref[...]
        dx, dwf = _rms_bwd(h1v, _rms(h1v), wf_ref[...], dhn_ref[...])
        dwf_ref[...] += dwf
        dh1 = dy_ref[...] + dx
        dh1 = jnp.where(_row_ids(dh1.shape, pl.program_id(0), tr) >= META_PAD, dh1, 0.0)
        dh_ref[...] = dh1
        m = m_ref[...]
        dmix, dwp = _rms_bwd(m, _rms(m), wp_ref[...], dh1)
        dwp_ref[...] += dwp
        dmix_ref[...] = dmix.astype(BF16)

    return pl.pallas_call(
        body, grid=(t_rows // tr,),
        in_specs=[_row(tr, D_MODEL) for _ in range(4)] + [_full((1, D_MODEL))] * 2,
        out_specs=[_row(tr, D_MODEL), _row(tr, D_MODEL), _full((1, D_MODEL)), _full((1, D_MODEL))],
        out_shape=[_sds((t_rows, D_MODEL), BF16), _sds((t_rows, D_MODEL), F32), _sds((1, D_MODEL), F32), _sds((1, D_MODEL), F32)],
        name="postmix_bwd", compiler_params=_params(1))(h1, dhn2, dy, mix, w_pre, w_post)


_ANY = pl.BlockSpec(memory_space=pl.ANY)


def _mix_bwd(dmixed, proj, y_ssm, y_attn, dproj):
    t_rows = dmixed.shape[0]
    tr = _pick(t_rows, (384, 128))

    def body(d_ref, g_ref, ys_ref, ya_ref, _, dys_ref, dya_ref, dg_ref):
        d = d_ref[...]
        g = _sigmoid(g_ref[...].astype(F32))
        g1, g2 = g[:, :D_MODEL], g[:, D_MODEL:]
        dys_ref[...] = (d * g1).astype(BF16)
        dya_ref[...] = (d * g2).astype(BF16)
        dg_ref[...] = jnp.concatenate([d * ys_ref[...] * g1 * (1.0 - g1), d * ya_ref[...] * g2 * (1.0 - g2)],
                                      axis=1).astype(BF16)

    return pl.pallas_call(
        body, grid=(t_rows // tr,),
        in_specs=[_row(tr, D_MODEL), _row(tr, 2 * D_MODEL, OFF_GATE // (2 * D_MODEL)), _row(tr, D_MODEL), _row(tr, D_MODEL),
                  _ANY],
        out_specs=[_row(tr, D_MODEL), _row(tr, D_MODEL), _row(tr, 2 * D_MODEL, OFF_GATE // (2 * D_MODEL))],
        out_shape=[_sds((t_rows, D_MODEL), BF16), _sds((t_rows, D_MODEL), BF16), _sds(dproj.shape, dproj.dtype)],
        input_output_aliases={4: 2},
        name="mix_bwd", compiler_params=_params(1))(dmixed, proj, y_ssm, y_attn, dproj)


def _ssm_post_bwd(y, proj, dyn, w, dproj):
    t_rows = y.shape[0]
    tr = CHUNK

    def body(y_ref, z_ref, d_ref, w_ref, _, dy_ref, dz_ref, dw_ref):
        @pl.when(pl.program_id(0) == 0)
        def _():
            dw_ref[...] = jnp.zeros_like(dw_ref)

        yv, z = y_ref[...], z_ref[...].astype(F32)
        sz = _sigmoid(z)
        silu = z * sz
        yz = yv * silu
        dyz, dw = _rms_bwd(yz, _rms(yz), w_ref[...], d_ref[...].astype(F32))
        dw_ref[...] += dw
        dy_ref[...] = dyz * silu
        dz_ref[...] = (dyz * yv * sz * (1.0 + z * (1.0 - sz))).astype(BF16)

    return pl.pallas_call(
        body, grid=(t_rows // tr,),
        in_specs=[_row(tr, D_INNER), _row(tr, D_INNER, OFF_Z // D_INNER), _row(tr, D_INNER), _full((1, D_INNER)), _ANY],
        out_specs=[_row(tr, D_INNER), _row(tr, D_INNER, OFF_Z // D_INNER), _full((1, D_INNER))],
        out_shape=[_sds((t_rows, D_INNER), F32), _sds(dproj.shape, dproj.dtype), _sds((1, D_INNER), F32)],
        input_output_aliases={4: 1},
        name="ssm_post_bwd", compiler_params=_params(1))(y, proj, dyn, w, dproj)


def _ssm_conv_bwd(xc, proj, dxs, dbm, dcm, conv_w, dproj):
    t_rows = xc.shape[0]
    tr = CHUNK
    nt = t_rows // tr
    bc_w = SSM_GROUPS * D_STATE

    def body(xc_ref, x_ref, dxs_ref, db_ref, dc_ref, w_ref, _, dx_ref, dw_ref, dbias_ref, ahead):
        first = pl.program_id(0) == 0

        @pl.when(first)
        def _():
            dw_ref[...] = jnp.zeros_like(dw_ref)
            dbias_ref[...] = jnp.zeros_like(dbias_ref)

        for c0 in range(0, CONV_DIM, STRIP_BWD):
            cols = slice(c0, c0 + STRIP_BWD)
            if c0 < D_INNER:
                dact = dxs_ref[:, cols]
            elif c0 < D_INNER + bc_w:
                dact = db_ref[:, c0 - D_INNER:c0 - D_INNER + STRIP_BWD]
            else:
                dact = dc_ref[:, c0 - D_INNER - bc_w:c0 - D_INNER - bc_w + STRIP_BWD]
            c = xc_ref[:, cols]
            s = _sigmoid(c)
            dpre = dact * s * (1.0 + c * (1.0 - s))
            x = x_ref[:, cols]
            dx = None
            for sh, moved in enumerate(_anticausal_taps(dpre, ahead.at[:, cols], first, SSM_CONV)):
                k = SSM_CONV - 1 - sh
                term = w_ref[k:k + 1, cols] * moved
                dx = term if dx is None else dx + term
                dw_ref[k:k + 1, cols] += jnp.sum(moved * x, axis=0, keepdims=True)
            dbias_ref[:, cols] += jnp.sum(dpre, axis=0, keepdims=True)
            dx_ref[:, cols] = dx.astype(BF16)

    xbc_block = OFF_XBC // CONV_DIM
    return pl.pallas_call(
        body, grid=(nt,),
        in_specs=[_row_rev(tr, CONV_DIM, nt), _row_rev(tr, CONV_DIM, nt, xbc_block), _row_rev(tr, D_INNER, nt),
                  _row_rev(tr, bc_w, nt), _row_rev(tr, bc_w, nt), _full((SSM_CONV, CONV_DIM)), _ANY],
        out_specs=[_row_rev(tr, CONV_DIM, nt, xbc_block), _full((SSM_CONV, CONV_DIM)), _full((1, CONV_DIM))],
        out_shape=[_sds(dproj.shape, dproj.dtype), _sds((SSM_CONV, CONV_DIM), F32), _sds((1, CONV_DIM), F32)],
        scratch_shapes=[pltpu.VMEM((HALO, CONV_DIM), F32)],
        input_output_aliases={6: 0},
        name="ssm_conv_bwd", compiler_params=_params(1))(xc, proj, dxs, dbm, dcm, conv_w, dproj)


def _prenorm_bwd(x, meta, dhn, dh, w):
    seq = x.shape[0]
    tr = _pick(seq, (512, 128))

    def body(x_ref, meta_ref, d_ref, r_ref, d0_ref, r0_ref, w_ref, dx_ref, dmeta_ref, dw_ref):
        wv = w_ref[...]

        @pl.when(pl.program_id(0) == 0)
        def _():
            h0 = jnp.concatenate([jnp.zeros((META_PAD, D_MODEL), F32), meta_ref[...]], axis=0)
            dx0, dw0 = _rms_bwd(h0, _rms(h0), wv, d0_ref[...])
            dw_ref[...] = dw0
            dmeta_ref[...] = (r0_ref[...] + dx0)[META_PAD:, :]

        h = x_ref[...]
        dx, dw = _rms_bwd(h, _rms(h), wv, d_ref[...])
        dw_ref[...] += dw
        dx_ref[...] = r_ref[...] + dx

    def shifted(tile):
        return pl.BlockSpec((pl.Element(tile), pl.Element(D_MODEL)), lambda i: (pl.multiple_of(i * tile + CHUNK, CHUNK), 0))

    first = pl.BlockSpec((CHUNK, D_MODEL), lambda i: (0, 0))
    return pl.pallas_call(
        body, grid=(seq // tr,),
        in_specs=[_row(tr, D_MODEL), _full((N_META, D_MODEL)), shifted(tr), shifted(tr), first, first, _full((1, D_MODEL))],
        out_specs=[_row(tr, D_MODEL), _full((N_META, D_MODEL)), _full((1, D_MODEL))],
        out_shape=[_sds((seq, D_MODEL), F32), _sds((N_META, D_MODEL), F32), _sds((1, D_MODEL), F32)],
        name="prenorm_bwd", compiler_params=_params(1))(x, meta, dhn, dh, dhn, dh, w)


def _dot01(x, m01, x_left, parts):
    acc, rest = None, x
    for i in range(parts):
        piece = rest.astype(BF16)
        term = (jnp.dot(piece, m01, preferred_element_type=F32) if x_left
                else jnp.dot(m01, piece, preferred_element_type=F32))
        acc = term if acc is None else acc + term
        if i + 1 < parts:
            rest = rest - piece.astype(F32)
    return acc


def _ssd_common(dtr_ref, dt_bias, a_log, chunk_index):
    rows = lax.broadcasted_iota(jnp.int32, (CHUNK, CHUNK), 0)
    cols = lax.broadcasted_iota(jnp.int32, (CHUNK, CHUNK), 1)
    low = rows >= cols
    raw = dtr_ref[:, :128]
    for g in range(1, SSM_GROUPS):
        raw = raw + pltpu.roll(dtr_ref[:, g * 128:(g + 1) * 128], HEADS_PER_GROUP * g, 1)
    raw = raw + dt_bias
    live = _row_ids(raw.shape, chunk_index, CHUNK) >= META_PAD
    dt = jnp.where(live, _softplus(raw), 0.0)
    a_head = -jnp.exp(a_log)
    cs = _dot01(dt * a_head, low.astype(BF16), False, 3)
    return dict(low=low, triu=(rows <= cols).astype(BF16), raw=raw, live=live, dt=dt, a_head=a_head, cs=cs, cs_t=cs.T,
                grow=jnp.exp(cs),
                fade=jnp.exp(cs[CHUNK - 1:CHUNK, :] - cs))


def _ssd_expand(cm, g):
    first = HEADS_PER_GROUP * g
    expand = (lax.broadcasted_iota(jnp.int32, (CHUNK, GROUP_W), 1) // HEAD_P + first
              == lax.broadcasted_iota(jnp.int32, (CHUNK, GROUP_W), 0)).astype(BF16)
    fold = (lax.broadcasted_iota(jnp.int32, (GROUP_W, CHUNK), 0) // HEAD_P + first
            == lax.broadcasted_iota(jnp.int32, (GROUP_W, CHUNK), 1)).astype(BF16)
    return dict(fold=fold, dtx=_dot01(cm["dt"], expand, True, 2), growx=_dot01(cm["grow"], expand, True, 2),
                fadex=_dot01(cm["fade"], expand, True, 2))


def _decay_matrix(cm, j):
    diff = cm["cs"][:, j:j + 1] - cm["cs_t"][j:j + 1, :]
    return jnp.where(cm["low"], jnp.exp(jnp.where(cm["low"], diff, 0.0)), 0.0)


def _dot(a, b, dims):
    return lax.dot_general(a.astype(BF16), b.astype(BF16), (dims, ((), ())), preferred_element_type=F32)


def _dot_fine(a, b, dims):
    a_hi, b_hi = a.astype(BF16), b.astype(BF16)
    a_lo, b_lo = (a - a_hi.astype(F32)).astype(BF16), (b - b_hi.astype(F32)).astype(BF16)
    dn = (dims, ((), ()))
    return (lax.dot_general(a_hi, b_hi, dn, preferred_element_type=F32)
            + lax.dot_general(a_hi, b_lo, dn, preferred_element_type=F32)
            + lax.dot_general(a_lo, b_hi, dn, preferred_element_type=F32))


def _ssd_specs(nt, rev):
    def idx(c):
        return nt - 1 - c if rev else c
    bc_w = SSM_GROUPS * D_STATE
    xs = pl.BlockSpec((CHUNK, D_INNER), lambda c: (idx(c), 0))
    bm = pl.BlockSpec((CHUNK, bc_w), lambda c: (idx(c), D_INNER // bc_w))
    cm = pl.BlockSpec((CHUNK, bc_w), lambda c: (idx(c), D_INNER // bc_w + 1))
    dtr = pl.BlockSpec((CHUNK, SSM_GROUPS * 128), lambda c: (idx(c), OFF_DT // (SSM_GROUPS * 128)))
    par = _full((1, 128))
    par_x = _full((SSM_GROUPS, 1, GROUP_W))
    return xs, bm, cm, dtr, par, par_x, idx


def _group_cols(g, width):
    return slice(g * width, (g + 1) * width)


def _ssd_fwd(xact, proj, dtb, alog, dskip_x):
    t_rows = xact.shape[0]
    nt = t_rows // CHUNK
    xs_spec, b_spec, c_spec, dtr_spec, par, par_x, _ = _ssd_specs(nt, False)

    def body(xs_ref, b_ref, c_ref, dtr_ref, dtb_ref, alog_ref, dsk_ref, y_ref, hst_ref, state):
        c = pl.program_id(0)

        @pl.when(c == 0)
        def _():
            state[...] = jnp.zeros_like(state)

        cm = _ssd_common(dtr_ref, dtb_ref[...], alog_ref[...], c)
        for g in range(SSM_GROUPS):
            wide, narrow = _group_cols(g, GROUP_W), _group_cols(g, D_STATE)
            ex = _ssd_expand(cm, g)
            xs, bm, cmat = xs_ref[:, wide], b_ref[:, narrow], c_ref[:, narrow]
            x_dt = xs * ex["dtx"]
            h_in = state[g]
            hst_ref[0, g] = h_in
            y_ref[:, wide] = _dot(cmat, h_in, ((1,), (0,))) * ex["growx"] + xs * dsk_ref[g]
            cb = _dot(cmat, bm, ((1,), (1,)))
            for j in range(HEADS_PER_GROUP):
                sl = slice(g * GROUP_W + j * HEAD_P, g * GROUP_W + (j + 1) * HEAD_P)
                decay = _decay_matrix(cm, HEADS_PER_GROUP * g + j)
                y_ref[:, sl] += _dot(cb * decay, x_dt[:, j * HEAD_P:(j + 1) * HEAD_P], ((1,), (0,)))
            state[g] = h_in * ex["growx"][CHUNK - 1:CHUNK, :] + _dot_fine(bm, x_dt * ex["fadex"], ((0,), (0,)))

    return pl.pallas_call(
        body, grid=(nt,),
        in_specs=[xs_spec, b_spec, c_spec, dtr_spec, par, par, par_x],
        out_specs=[xs_spec, pl.BlockSpec((1, SSM_GROUPS, D_STATE, GROUP_W), lambda c: (c, 0, 0, 0))],
        out_shape=[_sds((t_rows, D_INNER), F32), _sds((nt, SSM_GROUPS, D_STATE, GROUP_W), F32)],
        scratch_shapes=[pltpu.VMEM((SSM_GROUPS, D_STATE, GROUP_W), F32)],
        name="ssd_fwd", compiler_params=_params(1))(xact, xact, xact, proj, dtb, alog, dskip_x)


def _ssd_bwd(xact, proj, dtb, alog, dskip_x, dy, hst, dproj):
    t_rows = xact.shape[0]
    nt = t_rows // CHUNK
    xs_spec, b_spec, c_spec, dtr_spec, par, par_x, idx = _ssd_specs(nt, True)
    h_spec = pl.BlockSpec((1, SSM_GROUPS, D_STATE, GROUP_W), lambda c: (idx(c), 0, 0, 0))
    hn_spec = pl.BlockSpec((1, SSM_GROUPS, D_STATE, GROUP_W), lambda c: (jnp.minimum(idx(c) + 1, nt - 1), 0, 0, 0))
    bc_out = pl.BlockSpec((CHUNK, SSM_GROUPS * D_STATE), lambda c: (idx(c), 0))

    def body(xs_ref, b_ref, c_ref, dtr_ref, dtb_ref, alog_ref, dsk_ref, dy_ref, h_ref, hn_ref, _,
             dxs_ref, db_ref, dc_ref, ddt_ref, dalog_ref, ddtb_ref, dd_ref, dstate, dx_buf):
        step = pl.program_id(0)

        @pl.when(step == 0)
        def _():
            dstate[...] = jnp.zeros_like(dstate)
            dalog_ref[...] = jnp.zeros_like(dalog_ref)
            ddtb_ref[...] = jnp.zeros_like(ddtb_ref)
            dd_ref[...] = jnp.zeros_like(dd_ref)

        cm = _ssd_common(dtr_ref, dtb_ref[...], alog_ref[...], idx(step))
        for g in range(SSM_GROUPS):
            _ssd_bwd_group(g, cm, xs_ref, b_ref, c_ref, dsk_ref, dy_ref, h_ref, hn_ref,
                           dxs_ref, db_ref, dc_ref, ddt_ref, dalog_ref, ddtb_ref, dd_ref, dstate, dx_buf)

    return pl.pallas_call(
        body, grid=(nt,),
        in_specs=[xs_spec, b_spec, c_spec, dtr_spec, par, par, par_x, xs_spec, h_spec, hn_spec, _ANY],
        out_specs=[xs_spec, bc_out, bc_out, dtr_spec, par, par, par_x],
        out_shape=[_sds((t_rows, D_INNER), F32), _sds((t_rows, SSM_GROUPS * D_STATE), F32),
                   _sds((t_rows, SSM_GROUPS * D_STATE), F32), _sds(dproj.shape, dproj.dtype),
                   _sds((1, 128), F32), _sds((1, 128), F32), _sds((SSM_GROUPS, 1, GROUP_W), F32)],
        scratch_shapes=[pltpu.VMEM((SSM_GROUPS, D_STATE, GROUP_W), F32), pltpu.VMEM((CHUNK, GROUP_W), F32)],
        input_output_aliases={10: 3},
        name="ssd_bwd", compiler_params=_params(1))(xact, xact, xact, proj, dtb, alog, dskip_x, dy, hst, hst, dproj)


def _ssd_bwd_group(g, cm, xs_ref, b_ref, c_ref, dsk_ref, dy_ref, h_ref, hn_ref,
                   dxs_ref, db_ref, dc_ref, ddt_ref, dalog_ref, ddtb_ref, dd_ref, dstate, dx_buf):
    wide, narrow = _group_cols(g, GROUP_W), _group_cols(g, D_STATE)
    first = HEADS_PER_GROUP * g
    ex = _ssd_expand(cm, g)
    xs, bm, cmat = xs_ref[:, wide], b_ref[:, narrow], c_ref[:, narrow]
    dsk = dsk_ref[g]
    x_dt = xs * ex["dtx"]
    h_in, h_next = h_ref[0, g], hn_ref[0, g]
    dyv = dy_ref[:, wide]
    dh = dstate[g]
    grow, fade = ex["growx"], ex["fadex"]
    dy_grow = dyv * grow
    x_fade = x_dt * fade
    cb = _dot(cmat, bm, ((1,), (1,)))
    ml = jnp.zeros((CHUNK, CHUNK), F32)
    row_id = lax.broadcasted_iota(jnp.int32, (CHUNK, CHUNK), 0)
    col_id = lax.broadcasted_iota(jnp.int32, (CHUNK, CHUNK), 1)
    w_rows = jnp.zeros((CHUNK, CHUNK), F32)
    w_cols = jnp.zeros((CHUNK, CHUNK), F32)
    for j in range(HEADS_PER_GROUP):
        sl = slice(j * HEAD_P, (j + 1) * HEAD_P)
        lm = _decay_matrix(cm, first + j)
        mlj = _dot(dyv[:, sl], x_dt[:, sl], ((1,), (1,))) * lm
        ml = ml + mlj
        wm = mlj * cb
        w_rows = jnp.where(col_id == first + j, jnp.sum(wm, axis=1, keepdims=True), w_rows)
        w_cols = jnp.where(row_id == first + j, jnp.sum(wm, axis=0, keepdims=True), w_cols)
        dx_buf[:, sl] = _dot(cb * lm, dyv[:, sl], ((0,), (0,)))
    dx_off = fade * _dot_fine(bm, dh, ((1,), (0,)))
    dx = dx_buf[...] + dx_off
    dc_ref[:, narrow] = _dot(ml, bm, ((1,), (0,))) + _dot(dy_grow, h_in, ((1,), (1,)))
    db_ref[:, narrow] = _dot(ml, cmat, ((0,), (0,))) + _dot(x_fade, dh, ((1,), (1,)))
    fold = ex["fold"]
    y_off = _dot_fine(cmat, h_in, ((1,), (0,))) * grow
    dcs = (w_rows - w_cols.T) + _dot01(dyv * y_off - x_dt * dx_off, fold, True, 2)
    tail = jnp.broadcast_to(jnp.sum(dh * h_next, axis=0, keepdims=True), (8, GROUP_W))
    tail = _dot01(tail, fold, True, 2)[0:1, :]
    last_row = lax.broadcasted_iota(jnp.int32, (CHUNK, 128), 0) == CHUNK - 1
    dcs = dcs + jnp.where(last_row, tail, 0.0)
    da = _dot01(dcs, cm["triu"], False, 3)
    ddt = da * cm["a_head"] + _dot01(dx * xs, fold, True, 2)
    ddt_raw = jnp.where(cm["live"], ddt * _sigmoid(cm["raw"]), 0.0)
    ddt_ref[:, narrow] = (ddt_raw if g == 0 else pltpu.roll(ddt_raw, 128 - first, 1)).astype(BF16)
    ddtb_ref[...] += jnp.sum(ddt_raw, axis=0, keepdims=True)
    dalog_ref[...] += jnp.sum(da * cm["dt"], axis=0, keepdims=True) * cm["a_head"]
    dd_ref[g] += jnp.sum(dyv * xs, axis=0, keepdims=True)
    dxs_ref[:, wide] = dx * ex["dtx"] + dyv * dsk
    dstate[g] = dh * grow[CHUNK - 1:CHUNK, :] + _dot_fine(cmat, dy_grow, ((0,), (0,)))


def _swa_bias():
    rows_q = ATTN_GROUP * CHUNK
    dist = (jnp.arange(rows_q) % CHUNK)[:, None] - jnp.arange(2 * CHUNK)[None, :] + CHUNK
    head = jnp.arange(KV_HEADS)[:, None] * ATTN_GROUP + jnp.arange(rows_q)[None, :] // CHUNK + 1
    slope = jnp.exp2(-8.0 * head.astype(F32) / ATTN_HEADS)
    return jnp.where((dist >= 0) & (dist < CHUNK), -slope[:, :, None] * dist.astype(F32)[None], NEG)


def _swa_probs(q_kv, k_prev, k_cur, k_first, sink, bias, n):
    rows_q = ATTN_GROUP * CHUNK
    qs = jnp.concatenate([q_kv[:, g * DH:(g + 1) * DH] for g in range(ATTN_GROUP)], axis=0) * (DH ** -0.5)
    kcat = jnp.concatenate([k_prev, k_cur], axis=0)
    kmeta = k_first[META_PAD:, :]
    key_ok = lax.broadcasted_iota(jnp.int32, (1, 2 * CHUNK), 1) + n * CHUNK >= 2 * CHUNK
    s_band = jnp.where(key_ok, _dot(qs, kcat, ((1,), (1,))) + bias, NEG)
    q_pos = lax.broadcasted_iota(jnp.int32, (rows_q, N_META), 0) % CHUNK + n * CHUNK - META_PAD
    ok_m = lax.broadcasted_iota(jnp.int32, (rows_q, N_META), 1) <= q_pos
    s_meta = jnp.where(ok_m, _dot(qs, kmeta, ((1,), (1,))), NEG)
    m = jnp.maximum(jnp.maximum(jnp.max(s_band, axis=1, keepdims=True), jnp.max(s_meta, axis=1, keepdims=True)), sink)
    p_band, p_meta, p_sink = jnp.exp(s_band - m), jnp.exp(s_meta - m), jnp.exp(sink - m)
    inv = 1.0 / (jnp.sum(p_band, axis=1, keepdims=True) + jnp.sum(p_meta, axis=1, keepdims=True) + p_sink)
    return qs, kcat, kmeta, p_band * inv, p_meta * inv, p_sink * inv


def _swa_specs(nt, rev):
    def idx(n):
        return nt - 1 - n if rev else n
    o = pl.BlockSpec((CHUNK, ATTN_HEADS * DH), lambda n: (idx(n), 0))
    chunks = (lambda c: jnp.maximum(c - 1, 0)), (lambda c: c), (lambda c: 0)
    qkv = [pl.BlockSpec((CHUNK, QKV_W), lambda n, f=f: (f(idx(n)), OFF_Q // QKV_W)) for f in chunks]
    sink = _full((KV_HEADS, ATTN_GROUP * CHUNK, 1))
    bias = _full((KV_HEADS, ATTN_GROUP * CHUNK, 2 * CHUNK))
    return o, qkv, sink, bias, idx


def _head_cols(k):
    kv_w = ATTN_GROUP * DH
    q0, k0, v0 = k * kv_w, OFF_K - OFF_Q + k * DH, OFF_V - OFF_Q + k * DH
    return slice(q0, q0 + kv_w), slice(k0, k0 + DH), slice(v0, v0 + DH)


def _swa_fwd(proj, sink_rows, bias):
    t_rows = proj.shape[0]
    nt = t_rows // CHUNK
    o_spec, qkv_specs, sink_spec, bias_spec, _ = _swa_specs(nt, False)
    kv_w = ATTN_GROUP * DH

    def body(prev_ref, cur_ref, first_ref, sink_ref, bias_ref, o_ref):
        n = pl.program_id(0)
        for k in range(KV_HEADS):
            qc, kc, vc = _head_cols(k)
            _, _, _, p_band, p_meta, _ = _swa_probs(cur_ref[:, qc], prev_ref[:, kc], cur_ref[:, kc], first_ref[:, kc],
                                                    sink_ref[k], bias_ref[k], n)
            vcat = jnp.concatenate([prev_ref[:, vc], cur_ref[:, vc]], axis=0)
            out = _dot(p_band, vcat, ((1,), (0,))) + _dot(p_meta, first_ref[:, vc][META_PAD:, :], ((1,), (0,)))
            for g in range(ATTN_GROUP):
                o_ref[:, k * kv_w + g * DH:k * kv_w + (g + 1) * DH] = out[g * CHUNK:(g + 1) * CHUNK, :]

    return pl.pallas_call(
        body, grid=(nt,), in_specs=qkv_specs + [sink_spec, bias_spec],
        out_specs=o_spec, out_shape=_sds((t_rows, ATTN_HEADS * DH), F32),
        name="swa_fwd", compiler_params=_params(1))(proj, proj, proj, sink_rows, bias)


def _swa_bwd(proj, sink_rows, bias, out, dout, dproj):
    t_rows = proj.shape[0]
    nt = t_rows // CHUNK
    o_spec, qkv_specs, sink_spec, bias_spec, idx = _swa_specs(nt, True)
    kv_w = ATTN_GROUP * DH
    k_off, v_off = OFF_K - OFF_Q, OFF_V - OFF_Q

    def body(prev_ref, cur_ref, first_ref, sink_ref, bias_ref, o_ref, do_ref, _, dqkv_ref, dsink_ref,
             carry_k, carry_v, meta_k, meta_v, dqkv_buf):
        step = pl.program_id(0)
        n = idx(step)

        @pl.when(step == 0)
        def _():
            carry_k[...] = jnp.zeros_like(carry_k)
            carry_v[...] = jnp.zeros_like(carry_v)
            meta_k[...] = jnp.zeros_like(meta_k)
            meta_v[...] = jnp.zeros_like(meta_v)
            dsink_ref[...] = jnp.zeros_like(dsink_ref)

        for k in range(KV_HEADS):
            cols = slice(k * kv_w, (k + 1) * kv_w)
            hd = slice(k * DH, (k + 1) * DH)
            qc, kc, vc = _head_cols(k)
            qs, kcat, kmeta, p_band, p_meta, p_sink = _swa_probs(cur_ref[:, qc], prev_ref[:, kc], cur_ref[:, kc],
                                                                 first_ref[:, kc], sink_ref[k], bias_ref[k], n)
            vcat = jnp.concatenate([prev_ref[:, vc], cur_ref[:, vc]], axis=0)
            vmeta = first_ref[:, vc][META_PAD:, :]
            o, do = o_ref[:, cols], do_ref[:, cols]
            os_ = jnp.concatenate([o[:, g * DH:(g + 1) * DH] for g in range(ATTN_GROUP)], axis=0)
            dos = jnp.concatenate([do[:, g * DH:(g + 1) * DH] for g in range(ATTN_GROUP)], axis=0)
            delta = jnp.sum(dos * os_, axis=1, keepdims=True)
            ds_band = p_band * (_dot(dos, vcat, ((1,), (1,))) - delta)
            ds_meta = p_meta * (_dot(dos, vmeta, ((1,), (1,))) - delta)
            ds_sink = -p_sink * delta
            dqs = (_dot(ds_band, kcat, ((1,), (0,))) + _dot(ds_meta, kmeta, ((1,), (0,)))) * (DH ** -0.5)
            for g in range(ATTN_GROUP):
                dqkv_buf[:, k * kv_w + g * DH:k * kv_w + (g + 1) * DH] = dqs[g * CHUNK:(g + 1) * CHUNK, :]
                dsink_ref[k, g:g + 1, :] += jnp.sum(ds_sink[g * CHUNK:(g + 1) * CHUNK, :])
            dkcat = _dot(ds_band, qs, ((0,), (0,)))
            dvcat = _dot(p_band, dos, ((0,), (0,)))
            meta_k[:, hd] += _dot(ds_meta, qs, ((0,), (0,)))
            meta_v[:, hd] += _dot(p_meta, dos, ((0,), (0,)))
            dqkv_buf[:, kc] = dkcat[CHUNK:, :] + carry_k[:, hd]
            dqkv_buf[:, vc] = dvcat[CHUNK:, :] + carry_v[:, hd]
            carry_k[:, hd] = dkcat[:CHUNK, :]
            carry_v[:, hd] = dvcat[:CHUNK, :]

        @pl.when(n == 0)
        def _():
            dqkv_buf[META_PAD:, k_off:k_off + KV_W] += meta_k[...]
            dqkv_buf[META_PAD:, v_off:v_off + KV_W] += meta_v[...]

        dqkv_ref[...] = dqkv_buf[...].astype(BF16)

    return pl.pallas_call(
        body, grid=(nt,),
        in_specs=qkv_specs + [sink_spec, bias_spec, o_spec, o_spec, pl.BlockSpec(memory_space=pl.ANY)],
        out_specs=[qkv_specs[1], _full((KV_HEADS, 8, 128))],
        out_shape=[_sds(dproj.shape, dproj.dtype), _sds((KV_HEADS, 8, 128), F32)],
        scratch_shapes=[pltpu.VMEM((CHUNK, KV_W), F32), pltpu.VMEM((CHUNK, KV_W), F32),
                        pltpu.VMEM((N_META, KV_W), F32), pltpu.VMEM((N_META, KV_W), F32),
                        pltpu.VMEM((CHUNK, QKV_W), F32)],
        input_output_aliases={7: 0},
        name="swa_bwd", compiler_params=_params(1))(proj, proj, proj, sink_rows, bias, out, dout, dproj)


def _pack_w_in_t(w_in_t):
    w_dt = w_in_t[CUT_DT:CUT_Q].reshape(SSM_GROUPS, HEADS_PER_GROUP, D_MODEL)
    w_dt = jnp.pad(w_dt, ((0, 0), (0, 128 - HEADS_PER_GROUP), (0, 0))).reshape(SSM_GROUPS * 128, D_MODEL)
    return jnp.concatenate([w_in_t[CUT_Z:CUT_XBC], w_in_t[CUT_G:], w_dt, w_in_t[CUT_Q:CUT_G], w_in_t[CUT_XBC:CUT_DT]], axis=0)


def _unpack_w_in_t(wp_t):
    w_dt = wp_t[OFF_DT:OFF_Q].reshape(SSM_GROUPS, 128, D_MODEL)[:, :HEADS_PER_GROUP].reshape(SSM_HEADS, D_MODEL)
    return jnp.concatenate([wp_t[OFF_Z:OFF_GATE], wp_t[OFF_XBC:], w_dt, wp_t[OFF_Q:OFF_XBC], wp_t[OFF_GATE:OFF_DT]], axis=0)


def _head_lanes(v):
    return jnp.pad(v.reshape(1, SSM_HEADS), ((0, 0), (0, 128 - SSM_HEADS)))


def _local_step(x, target, wt, late_weights=None, on_grad=None, started=None):
    seq = x.shape[0]
    grads = {}

    def emit(name, g):
        grads[name] = g
        return None if on_grad is None else on_grad(name, g)
    meta = wt["meta_tokens"]
    wp_t = _pack_w_in_t(wt["w_in_t"])
    dtb = _head_lanes(wt["ssm_dt_bias"].reshape(-1))
    alog = _head_lanes(wt["ssm_a_log"].reshape(-1))
    dskip_x = jnp.repeat(wt["ssm_d_skip"].reshape(-1), HEAD_P).reshape(SSM_GROUPS, 1, GROUP_W)
    sink_rows = jnp.repeat(wt["attn_sinks"].reshape(KV_HEADS, ATTN_GROUP), CHUNK, axis=1).reshape(KV_HEADS, ATTN_GROUP * CHUNK, 1)

    hn = _prenorm(x, meta, wt["norm_pre_mix"])
    proj = _matmul(hn, wp_t, tb=True, name="in_proj", after=started)
    xc, xact = _ssm_conv_fwd(proj, wt["ssm_conv_w"], wt["ssm_conv_b"])
    y, hst = _ssd_fwd(xact, proj, dtb, alog, dskip_x)
    yn = _ssm_post(y, proj, wt["ssm_norm"])
    if late_weights is not None:
        wt = {**wt, **late_weights(yn)}
    y_ssm = _matmul(yn, wt["w_ssm_out"], name="ssm_out")
    bias = _swa_bias()
    attn = _swa_fwd(proj, sink_rows, bias)
    y_attn = _matmul(attn, wt["w_attn_out"], name="attn_out")
    mixed = _mix_fwd(proj, y_ssm, y_attn)
    mix = _matmul(mixed, wt["w_mix_out"], name="mix_out")
    h1, hn2 = _postmix(x, meta, mix, wt["norm_post_mix"], wt["norm_pre_ffn"])
    up = _matmul(hn2, wt["w_ffn_up_t"], tb=True, out_dtype=BF16, name="ffn_up")
    u, act = _ffn_act(up, wt["ffn_conv_w"], wt["ffn_conv_b"])
    f = _matmul(act, wt["w_ffn_down"], name="ffn_down")
    df, dy, g_norm_post_ffn, loss_row = _final(h1, f, target, wt["norm_post_ffn"])

    grads["norm_post_ffn"] = g_norm_post_ffn
    sent = emit("w_ffn_down", _matmul(act, df, ta=True, out_dtype=BF16, name="dw_ffn_down"))
    dact = _matmul(df, wt["w_ffn_down"], tb=True, out_dtype=BF16, name="d_act", after=sent)
    dup, grads["ffn_conv_w"], grads["ffn_conv_b"] = _ffn_act_bwd(u, up, dact, wt["ffn_conv_w"])
    sent = emit("w_ffn_up_t", _matmul(dup, hn2, ta=True, out_dtype=BF16, name="dw_ffn_up"))
    dhn2 = _matmul(dup, wt["w_ffn_up_t"], name="d_hn2", after=sent)
    dmix, dh, grads["norm_pre_ffn"], grads["norm_post_mix"] = _postmix_bwd(h1, dhn2, dy, mix, wt["norm_pre_ffn"], wt["norm_post_mix"])
    sent = emit("w_mix_out", _matmul(mixed, dmix, ta=True, out_dtype=BF16, name="dw_mix_out"))
    dmixed = _matmul(dmix, wt["w_mix_out"], tb=True, name="d_mixed", after=sent)
    dy_ssm, dy_attn, dproj = _mix_bwd(dmixed, proj, y_ssm, y_attn, lax.empty(proj.shape, BF16))
    sent = emit("w_ssm_out", _matmul(yn, dy_ssm, ta=True, out_dtype=BF16, name="dw_ssm_out"))
    dyn = _matmul(dy_ssm, wt["w_ssm_out"], tb=True, out_dtype=BF16, name="d_yn", after=sent)
    sent = emit("w_attn_out", _matmul(attn, dy_attn, ta=True, out_dtype=BF16, name="dw_attn_out"))
    dattn = _matmul(dy_attn, wt["w_attn_out"], tb=True, name="d_attn", after=sent)
    dy_ssd, dproj, grads["ssm_norm"] = _ssm_post_bwd(y, proj, dyn, wt["ssm_norm"], dproj)
    dxs, dbm, dcm, dproj, dalog, ddtb, dd_x = _ssd_bwd(xact, proj, dtb, alog, dskip_x, dy_ssd, hst, dproj)
    grads["ssm_a_log"] = dalog[:, :SSM_HEADS]
    grads["ssm_dt_bias"] = ddtb[:, :SSM_HEADS]
    grads["ssm_d_skip"] = dd_x.reshape(SSM_HEADS, HEAD_P).sum(axis=1).reshape(1, SSM_HEADS)
    dproj, grads["ssm_conv_w"], grads["ssm_conv_b"] = _ssm_conv_bwd(xc, proj, dxs, dbm, dcm, wt["ssm_conv_w"], dproj)
    dproj, dsink = _swa_bwd(proj, sink_rows, bias, attn, dattn, dproj)
    grads["attn_sinks"] = dsink[:, :ATTN_GROUP, 0].reshape(1, ATTN_HEADS)
    sent = emit("w_in_t", _unpack_w_in_t(_matmul(dproj, hn, ta=True, out_dtype=BF16, name="dw_in")))
    dhn = _matmul(dproj, wp_t, name="d_hn", after=sent)
    grad_x, grads["meta_tokens"], grads["norm_pre_mix"] = _prenorm_bwd(x, meta, dhn, dh, wt["norm_pre_mix"])
    return loss_row[0, 0], grad_x, grads


def _all_gather(shards):
    n = len(shards)

    def body(*refs):
        ins, outs = refs[:n], refs[n:2 * n]
        send_sems, recv_sems, local_sems = refs[2 * n:]
        x, y, c = lax.axis_index("x"), lax.axis_index("y"), lax.axis_index("c")
        me, sibling = (x, y, c), (x, y, 1 - c)
        x_nbr, y_nbr, diag = (1 - x, y), (x, 1 - y), (1 - x, 1 - y)
        relayed = (x ^ (1 - c), y ^ c)
        relay_to = (x ^ c, y ^ (1 - c))

        def slot(a, dev):
            return outs[a].at[4 * dev[0] + 2 * dev[1] + dev[2]]

        def copy(k, a, block, to, src=None):
            return pltpu.make_async_remote_copy(
                src_ref=slot(a, block) if src is None else src, dst_ref=slot(a, block),
                send_sem=send_sems.at[k, a], recv_sem=recv_sems.at[k, a],
                device_id=to, device_id_type=pl.DeviceIdType.MESH)

        mine = [pltpu.make_async_copy(ins[a], slot(a, me), local_sems.at[a]) for a in range(n)]
        for cp in mine:
            cp.start()
        first = [copy(0, a, me, sibling, src=ins[a]) for a in range(n)]
        first += [copy(1, a, me, (*x_nbr, c), src=ins[a]) for a in range(n)]
        first += [copy(2, a, me, (*y_nbr, c), src=ins[a]) for a in range(n)]
        for cp in first:
            cp.start()
        passed = []

        def pass_on(k, block, to):
            for a in range(n):
                cp = copy(k, a, block, to)
                cp.start()
                passed.append(cp)

        for j, chip in enumerate((x_nbr, y_nbr)):
            for a in range(n):
                copy(1 + j, a, (*chip, c), me).wait_recv()
            pass_on(4 + j, (*chip, c), sibling)
        pass_on(3, (*relayed, c), (*relay_to, c))
        for a in range(n):
            copy(3, a, (*diag, c), me).wait_recv()
        pass_on(6, (*diag, c), sibling)
        for a in range(n):
            copy(0, a, sibling, me).wait_recv()
        for j, chip in enumerate((x_nbr, y_nbr, diag)):
            for a in range(n):
                copy(4 + j, a, (*chip, 1 - c), me).wait_recv()
        for cp in first + passed:
            cp.wait_send()
        for cp in mine:
            cp.wait()

    hbm = pl.BlockSpec(memory_space=pl.ANY)
    return pl.pallas_call(
        body, in_specs=[hbm] * n, out_specs=[hbm] * n,
        out_shape=[_sds((N_DEV,) + s.shape, s.dtype) for s in shards],
        scratch_shapes=[pltpu.SemaphoreType.DMA((7, n)), pltpu.SemaphoreType.DMA((7, n)), pltpu.SemaphoreType.DMA((n,))],
        name="gather_weights")(*shards)


def _peer_table():
    x, y, c = lax.axis_index("x"), lax.axis_index("y"), lax.axis_index("c")
    peers = []
    for k in range(N_DEV - 1):
        bits = k + 1
        p = (x ^ ((bits >> 2) & 1), y ^ ((bits >> 1) & 1), c ^ (bits & 1))
        peers.append((k, p, 4 * p[0] + 2 * p[1] + p[2]))
    return 4 * x + 2 * y + c, peers


_HBM = pl.BlockSpec(memory_space=pltpu.HBM)
_SEM = pl.BlockSpec(memory_space=pltpu.SEMAPHORE)
_EFFECT = pltpu.SideEffectType.DATAFLOW_SIDE_EFFECTING


def _push_copy(src, land, send_sems, recv_sems, a, k, p, src_slot, dst_slot):
    sem = a * (N_DEV - 1) + k
    return pltpu.make_async_remote_copy(
        src_ref=src[a] if src_slot is None else src[a].at[src_slot], dst_ref=land[a].at[dst_slot],
        send_sem=send_sems.at[sem], recv_sem=recv_sems.at[sem], device_id=p, device_id_type=pl.DeviceIdType.MESH)


def _push_start(srcs, scatter, name):
    n = len(srcs)
    lands = [lax.empty(s.shape if scatter else (N_DEV,) + s.shape, s.dtype) for s in srcs]

    def body(*refs):
        src, land = refs[:n], refs[n:2 * n]
        send_sems, recv_sems, token = refs[2 * n], refs[2 * n + 1], refs[-1]
        my_id, peers = _peer_table()
        for a in range(n):
            for k, p, p_id in peers:
                _push_copy(src, land, send_sems, recv_sems, a, k, p, p_id if scatter else None, my_id).start()
        token[...] = jnp.zeros_like(token)

    sems = pltpu.SemaphoreType.DMA(((N_DEV - 1) * n,))
    res = pl.pallas_call(
        body, name=name,
        out_shape=(sems, sems, *[pltpu.HBM(a.shape, a.dtype) for a in srcs + lands], _sds((8, 128), F32)),
        in_specs=[_HBM] * (2 * n), out_specs=(_SEM, _SEM, *[_HBM] * (2 * n), pl.BlockSpec(memory_space=pltpu.VMEM)),
        input_output_aliases={i: 2 + i for i in range(2 * n)},
        compiler_params=pltpu.CompilerParams(has_side_effects=_EFFECT),
    )(*[pltpu.with_memory_space_constraint(a, pltpu.HBM) for a in srcs + lands])
    return dict(send=res[0], recv=res[1], src=list(res[2:2 + n]), land=list(res[2 + n:2 + 2 * n]), token=res[-1],
                scatter=scatter)


def _push_wait(handle, after, name):
    n = len(handle["src"])
    scatter = handle["scatter"]

    def body(*refs):
        src, land = refs[:n], refs[n:2 * n]
        send_sems, recv_sems = refs[2 * n], refs[2 * n + 1]
        _, peers = _peer_table()
        for a in range(n):
            for k, p, p_id in peers:
                cp = _push_copy(src, land, send_sems, recv_sems, a, k, p, p_id if scatter else None, p_id)
                cp.wait_send()
                cp.wait_recv()

    arrays = handle["src"] + handle["land"]
    res = pl.pallas_call(
        body, name=name, out_shape=tuple(pltpu.HBM(a.shape, a.dtype) for a in arrays),
        in_specs=[_HBM] * (2 * n) + [_SEM, _SEM, pl.BlockSpec(memory_space=pl.ANY)], out_specs=tuple([_HBM] * (2 * n)),
        input_output_aliases={i: i for i in range(2 * n)},
        compiler_params=pltpu.CompilerParams(has_side_effects=_EFFECT),
    )(*arrays, handle["send"], handle["recv"], after)
    return list(res[:n]), list(res[n:])


def _slot_sum(p_ref, own_ref):
    if own_ref is not None:
        my_id = 4 * lax.axis_index("x") + 2 * lax.axis_index("y") + lax.axis_index("c")
        mine = own_ref[...].astype(F32)
    g = None
    for s in range(p_ref.shape[0]):
        term = p_ref[s].astype(F32)
        if own_ref is not None:
            term = jnp.where(my_id == s, mine, term)
        g = term if g is None else g + term
    return g


def _to_bf16(arrays):
    n = len(arrays)

    def body(*refs):
        for i in range(n):
            refs[n + i][...] = refs[i][...].astype(BF16)

    return pl.pallas_call(body, out_shape=[_sds(a.shape, BF16) for a in arrays], name="weights_to_bf16",
                          compiler_params=pltpu.CompilerParams(vmem_limit_bytes=VMEM_LIMIT))(*arrays)


def _adamw(parts, own, w, m, v, name):
    unit_rows = w.ndim == 3
    rows, cols = w.shape[0], w.shape[-1]
    if rows % 16 == 0:
        tr, tc = _pick(rows, (256, 128, 176, 64, 32, 16)), cols
    else:
        tr, tc = rows, _pick(cols, (256, 128))

    def body(*refs):
        if own is None:
            p_ref, w_ref, m_ref, v_ref, g_ref, d_ref, nm_ref, nv_ref = refs
            own_ref = None
        else:
            p_ref, own_ref, w_ref, m_ref, v_ref, g_ref, d_ref, nm_ref, nv_ref = refs
        g = _slot_sum(p_ref, own_ref)
        if unit_rows:
            g = g.reshape(tr, 1, tc)
        m_new = ADAM_B1 * m_ref[...] + (1.0 - ADAM_B1) * g
        v_new = ADAM_B2 * v_ref[...] + (1.0 - ADAM_B2) * (g * g)
        m_hat = m_new / (1.0 - ADAM_B1 ** ADAM_STEP)
        v_hat = v_new / (1.0 - ADAM_B2 ** ADAM_STEP)
        g_ref[...] = g
        d_ref[...] = -ADAM_LR * (m_hat / (jnp.sqrt(v_hat) + ADAM_EPS) + ADAM_WD * w_ref[...])
        nm_ref[...] = m_new
        nv_ref[...] = v_new

    by_rows = tc == cols
    spec = pl.BlockSpec((tr, tc), (lambda i: (i, 0)) if by_rows else (lambda i: (0, i)))
    state_spec = spec if not unit_rows else pl.BlockSpec((tr, 1, tc), (lambda i: (i, 0, 0)) if by_rows else (lambda i: (0, 0, i)))
    parts_spec = pl.BlockSpec((parts.shape[0], tr, tc), (lambda i: (0, i, 0)) if by_rows else (lambda i: (0, 0, i)))
    operands = (parts, w, m, v) if own is None else (parts, own, w, m, v)
    return pl.pallas_call(
        body, grid=(rows // tr if by_rows else cols // tc,),
        in_specs=[parts_spec] + ([] if own is None else [spec]) + [state_spec] * 3,
        out_specs=[state_spec] * 4, out_shape=[_sds(w.shape, F32)] * 4,
        name=name, compiler_params=_params(1))(*operands)


SMALL_REPLICATED = (("norm_pre_mix", 1024), ("ssm_conv_b", 3072), ("ssm_dt_bias", 32), ("ssm_a_log", 32),
                    ("ssm_d_skip", 32), ("ssm_norm", 2048), ("attn_sinks", 16), ("norm_post_mix", 1024),
                    ("norm_pre_ffn", 1024), ("ffn_conv_b", 5632), ("norm_post_ffn", 1024))
SMALL_SHARDED = (("meta_tokens", (N_META, D_MODEL // N_DEV)), ("ssm_conv_w", (SSM_CONV, CONV_DIM // N_DEV)),
                 ("ffn_conv_w", (FFN_CONV, 2 * FFN_DIM // N_DEV)))
BIG = (("w_in", (D_MODEL, N_IN // N_DEV), 1), ("w_ssm_out", (D_INNER // N_DEV, D_MODEL), 0),
       ("w_attn_out", (D_MODEL // N_DEV, D_MODEL), 0), ("w_mix_out", (D_MODEL // N_DEV, D_MODEL), 0),
       ("w_ffn_up", (D_MODEL, 2 * FFN_DIM // N_DEV), 1), ("w_ffn_down", (FFN_DIM // N_DEV, D_MODEL), 0))


def _rows_of(size):
    return -(-size // 128)


def _as_rows(flat):
    size = flat.shape[-1]
    rows = _rows_of(size)
    flat = jnp.pad(flat, [(0, 0)] * (flat.ndim - 1) + [(0, rows * 128 - size)])
    return flat.reshape(flat.shape[:-1] + (rows, 128))


def _pack_small(rep, sharded):
    pieces = [_as_rows(rep[name].reshape(-1)) for name, _ in SMALL_REPLICATED]
    pieces += [_as_rows(sharded[name].reshape(-1)) for name, _ in SMALL_SHARDED]
    packed = jnp.concatenate(pieces, axis=0)
    return jnp.pad(packed, ((0, -packed.shape[0] % 8), (0, 0)))


def _unpack_small(packed):
    out, row = {}, 0
    for name, size in SMALL_REPLICATED:
        out[name] = packed[row:row + _rows_of(size)].reshape(-1)[:size].reshape(1, size)
        row += _rows_of(size)
    for name, (r, c) in SMALL_SHARDED:
        out[name] = packed[row:row + _rows_of(r * c)].reshape(-1)[:r * c].reshape(r, c)
        row += _rows_of(r * c)
    return out


def _shard_major(g, shape, axis):
    r, c = shape
    if axis == 0:
        return g.reshape(N_DEV, r, c)
    return g.reshape(r, N_DEV, c).transpose(1, 0, 2)


def kernel(x, meta_tokens, norm_pre_mix, w_in, ssm_conv_w, ssm_conv_b, ssm_dt_bias, ssm_a_log, ssm_d_skip, ssm_norm, w_ssm_out, attn_sinks, w_attn_out, w_mix_out, norm_post_mix, norm_pre_ffn, w_ffn_up, ffn_conv_w, ffn_conv_b, w_ffn_down, norm_post_ffn, loss_target, m_meta_tokens, m_norm_pre_mix, m_w_in, m_ssm_conv_w, m_ssm_conv_b, m_ssm_dt_bias, m_ssm_a_log, m_ssm_d_skip, m_ssm_norm, m_w_ssm_out, m_attn_sinks, m_w_attn_out, m_w_mix_out, m_norm_post_mix, m_norm_pre_ffn, m_w_ffn_up, m_ffn_conv_w, m_ffn_conv_b, m_w_ffn_down, m_norm_post_ffn, v_meta_tokens, v_norm_pre_mix, v_w_in, v_ssm_conv_w, v_ssm_conv_b, v_ssm_dt_bias, v_ssm_a_log, v_ssm_d_skip, v_ssm_norm, v_w_ssm_out, v_attn_sinks, v_w_attn_out, v_w_mix_out, v_norm_post_mix, v_norm_pre_ffn, v_w_ffn_up, v_ffn_conv_w, v_ffn_conv_b, v_w_ffn_down, v_norm_post_ffn):
    names = ("meta_tokens", "norm_pre_mix", "w_in", "ssm_conv_w", "ssm_conv_b", "ssm_dt_bias", "ssm_a_log", "ssm_d_skip",
             "ssm_norm", "w_ssm_out", "attn_sinks", "w_attn_out", "w_mix_out", "norm_post_mix", "norm_pre_ffn", "w_ffn_up",
             "ffn_conv_w", "ffn_conv_b", "w_ffn_down", "norm_post_ffn")
    w_loc = dict(zip(names, (meta_tokens, norm_pre_mix, w_in, ssm_conv_w, ssm_conv_b, ssm_dt_bias, ssm_a_log, ssm_d_skip,
                             ssm_norm, w_ssm_out, attn_sinks, w_attn_out, w_mix_out, norm_post_mix, norm_pre_ffn, w_ffn_up,
                             ffn_conv_w, ffn_conv_b, w_ffn_down, norm_post_ffn)))
    m_loc = dict(zip(names, (m_meta_tokens, m_norm_pre_mix, m_w_in, m_ssm_conv_w, m_ssm_conv_b, m_ssm_dt_bias, m_ssm_a_log,
                             m_ssm_d_skip, m_ssm_norm, m_w_ssm_out, m_attn_sinks, m_w_attn_out, m_w_mix_out, m_norm_post_mix,
                             m_norm_pre_ffn, m_w_ffn_up, m_ffn_conv_w, m_ffn_conv_b, m_w_ffn_down, m_norm_post_ffn)))
    v_loc = dict(zip(names, (v_meta_tokens, v_norm_pre_mix, v_w_in, v_ssm_conv_w, v_ssm_conv_b, v_ssm_dt_bias, v_ssm_a_log,
                             v_ssm_d_skip, v_ssm_norm, v_w_ssm_out, v_attn_sinks, v_w_attn_out, v_w_mix_out, v_norm_post_mix,
                             v_norm_pre_ffn, v_w_ffn_up, v_ffn_conv_w, v_ffn_conv_b, v_w_ffn_down, v_norm_post_ffn)))

    def local2d(d, name):
        a = d[name]
        return a if name == "meta_tokens" else a.reshape(a.shape[1:])

    def turned2d(d, name):
        a = jnp.swapaxes(d[name], 1, 2)
        return a.reshape(a.shape[1:])

    my_id = 4 * lax.axis_index("x") + 2 * lax.axis_index("y") + lax.axis_index("c")
    big = {name: (shape, axis) for name, shape, axis in BIG}

    def whole(name, g):
        return g.reshape(N_DEV * g.shape[1], g.shape[2])

    def key(name):
        return name + "_t" if big[name][1] == 1 else name

    by_rows = [name for name, _, axis in BIG if axis == 0]
    send_bf16 = dict(zip(by_rows, _to_bf16([local2d(w_loc, name) for name in by_rows])))
    for name, _, axis in BIG:
        if axis == 1:
            send_bf16[name] = turned2d(w_loc, name).astype(BF16)
    small_shard_pack = jnp.concatenate([_as_rows(local2d(w_loc, name).reshape(-1)) for name, _ in SMALL_SHARDED], axis=0)
    small_shard_pack = jnp.pad(small_shard_pack, ((0, -small_shard_pack.shape[0] % 8), (0, 0)))
    first = _all_gather([send_bf16["w_in"], small_shard_pack])
    rest_names = [name for name, _, _ in BIG if name != "w_in"]
    rest = [send_bf16[name] for name in rest_names]
    rest, first = lax.optimization_barrier((rest, first))
    rest_handle = _push_start(rest, False, "gather_rest_start")
    wt = {"w_in_t": whole("w_in", first[0])}
    row = 0
    for name, (r, c) in SMALL_SHARDED:
        blocks = first[1][:, row:row + _rows_of(r * c)].reshape(N_DEV, -1)[:, :r * c].reshape(N_DEV, r, c)
        wt[name] = blocks.transpose(1, 0, 2).reshape(r, N_DEV * c)
        row += _rows_of(r * c)
    for name, size in SMALL_REPLICATED:
        wt[name] = w_loc[name].reshape(1, size)

    def late_weights(after):
        own, landed = _push_wait(rest_handle, after, "gather_rest_wait")
        out = {}
        for name, mine, land in zip(rest_names, own, landed):
            out[key(name)] = whole(name, lax.dynamic_update_index_in_dim(land, mine, my_id, 0))
        return out

    sent = {}

    def on_grad(known_as, g):
        name = known_as.removesuffix("_t")
        by_owner = g.reshape(N_DEV, g.shape[0] // N_DEV, g.shape[1])
        sent[name] = _push_start([by_owner], True, "send_" + name)
        return sent[name]["token"]

    loss_part, grad_x, grads = _local_step(x[0], loss_target[0], wt, late_weights, on_grad, rest_handle["token"])

    small_parts = []
    for name, (r, c) in SMALL_SHARDED:
        small_parts.append(_as_rows(_shard_major(grads[name], (r, c), 1).reshape(N_DEV, r * c)))
    rep_rows = jnp.concatenate([_as_rows(grads[name].reshape(-1)) for name, _ in SMALL_REPLICATED], axis=0)
    small_send = jnp.concatenate([jnp.broadcast_to(rep_rows[None], (N_DEV,) + rep_rows.shape)] + small_parts, axis=1)
    small_send = jnp.pad(small_send, ((0, 0), (0, -small_send.shape[1] % 8), (0, 0)))
    loss_tile = jnp.broadcast_to(jnp.pad(loss_part.reshape(1, 1, 1), ((0, 0), (0, 7), (0, 127))), (N_DEV, 8, 128))
    small_send = jnp.concatenate([small_send, loss_tile], axis=1)
    small_handle = _push_start([small_send], True, "send_small")

    def small_pack(d):
        pack = _pack_small({name: d[name] for name, _ in SMALL_REPLICATED}, {name: local2d(d, name) for name, _ in SMALL_SHARDED})
        return jnp.pad(pack, ((0, 8), (0, 0)))

    def arrived(handle, after, name):
        src, landed = _push_wait(handle, after, "arrived_" + name)
        return landed[0], lax.dynamic_index_in_dim(src[0], my_id, 0, keepdims=False)

    grad_w, delta_w, new_m, new_v = {}, {}, {}, {}
    outs = None
    after = small_handle["token"]
    for name, handle in sent.items():
        if name == "w_in":
            parts, own = arrived(small_handle, after, "small")
            outs = _adamw(parts, own, small_pack(w_loc), small_pack(m_loc), small_pack(v_loc), "adamw_small")
            after = outs[0]
        parts, own = arrived(handle, after, name)
        turned = big[name][1] == 1
        unit_rows = turned and big[name][0][1] % 8 != 0
        if unit_rows:
            state = [jnp.transpose(d[name], (2, 0, 1)) for d in (w_loc, m_loc, v_loc)]
        else:
            state = [turned2d(d, name) if turned else local2d(d, name) for d in (w_loc, m_loc, v_loc)]
        results = _adamw(parts, own, *state, "adamw_" + name)
        after = results[0]
        full = (1,) + big[name][0]
        for dst, a in zip((grad_w, delta_w, new_m, new_v), results):
            if unit_rows:
                dst[name] = jnp.transpose(a, (1, 2, 0))
            else:
                dst[name] = jnp.swapaxes(a[None], 1, 2) if turned else a.reshape(full)
    for dst, packed in zip((grad_w, delta_w, new_m, new_v), outs):
        for name, a in _unpack_small(packed).items():
            dst[name] = a.reshape(w_loc[name].shape)
    loss = outs[0][-8, 0]

    return (loss, grad_x[None], *[grad_w[n] for n in names], *[delta_w[n] for n in names],
            *[new_m[n] for n in names], *[new_v[n] for n in names])
```

```python
import jax
import jax.numpy as jnp
from jax import lax
from jax.experimental import pallas as pl
from jax.experimental.pallas import tpu as pltpu

F32 = jnp.float32
BF16 = jnp.bfloat16

D_MODEL = 1024
N_META = 16
CHUNK = 128
META_PAD = CHUNK - N_META
D_INNER = 2048
HEAD_P = 64
SSM_HEADS = 32
SSM_GROUPS = 4
HEADS_PER_GROUP = SSM_HEADS // SSM_GROUPS
GROUP_W = HEADS_PER_GROUP * HEAD_P
D_STATE = 128
SSM_CONV = 4
CONV_DIM = D_INNER + 2 * SSM_GROUPS * D_STATE
ATTN_HEADS = 16
KV_HEADS = 4
ATTN_GROUP = ATTN_HEADS // KV_HEADS
DH = 64
KV_W = KV_HEADS * DH
FFN_DIM = 2816
FFN_CONV = 3
EPS = 1e-6
NEG = -1e30
N_DEV = 8
AXES = ("x", "y", "c")

OFF_Z, OFF_GATE, OFF_DT, OFF_Q, OFF_K, OFF_V, OFF_XBC = 0, 2048, 4096, 4608, 5632, 5888, 6144
N_INP = OFF_XBC + CONV_DIM
QKV_W = OFF_XBC - OFF_Q
CUT_Z, CUT_XBC, CUT_DT, CUT_Q, CUT_K, CUT_V, CUT_G = 0, 2048, 5120, 5152, 6176, 6432, 6688
N_IN = 8736

ADAM_LR, ADAM_B1, ADAM_B2, ADAM_EPS, ADAM_WD, ADAM_STEP = 0.001, 0.9, 0.999, 1e-08, 0.01, 10

VMEM_LIMIT = 56 * 1024 * 1024


def _params(n_grid):
    return pltpu.CompilerParams(dimension_semantics=("arbitrary",) * n_grid, vmem_limit_bytes=VMEM_LIMIT)


def _sds(shape, dtype):
    return jax.ShapeDtypeStruct(shape, dtype)


def _pick(n, prefs):
    for c in prefs:
        if n % c == 0:
            return c
    raise ValueError(f"no tile of {prefs} divides {n}")


def _row(tr, width, cb=0):
    return pl.BlockSpec((tr, width), lambda i: (i, cb))


def _row_rev(tr, width, nt, cb=0):
    return pl.BlockSpec((tr, width), lambda i: (nt - 1 - i, cb))


def _full(shape):
    return pl.BlockSpec(shape, lambda *_: (0,) * len(shape))


def _sigmoid(x):
    return 1.0 / (1.0 + jnp.exp(-x))


def _softplus(x):
    return jnp.maximum(x, 0.0) + jnp.log(1.0 + jnp.exp(-jnp.abs(x)))


def _rms(x):
    return lax.rsqrt(jnp.mean(x * x, axis=-1, keepdims=True) + EPS)


def _rms_bwd(x, r, w, dy):
    xh = x * r
    g = dy * w
    dx = r * (g - xh * jnp.mean(g * xh, axis=-1, keepdims=True))
    return dx, jnp.sum(dy * xh, axis=0, keepdims=True)


def _row_ids(shape, tile_index, tr):
    return tile_index * tr + lax.broadcasted_iota(jnp.int32, shape, 0)


HALO = 8
STRIP = 256
STRIP_BWD = 128


def _causal_taps(x, halo, first_step, taps):
    n = x.shape[0]

    @pl.when(first_step)
    def _():
        halo[...] = jnp.zeros_like(halo)

    before = halo[...]
    row = lax.broadcasted_iota(jnp.int32, before.shape, 0)
    shifted = [x]
    for s in range(1, taps):
        rolled = pltpu.roll(x, s, 0)
        head = jnp.where(row < s, pltpu.roll(before, s, 0), rolled[0:HALO, :])
        shifted.append(jnp.concatenate([head, rolled[HALO:, :]], axis=0))
    halo[...] = x[n - HALO:, :]
    return shifted


def _anticausal_taps(x, halo, first_step, taps):
    n = x.shape[0]

    @pl.when(first_step)
    def _():
        halo[...] = jnp.zeros_like(halo)

    after = halo[...]
    row = lax.broadcasted_iota(jnp.int32, after.shape, 0)
    shifted = [x]
    for s in range(1, taps):
        rolled = pltpu.roll(x, n - s, 0)
        tail = jnp.where(row >= HALO - s, pltpu.roll(after, HALO - s, 0), rolled[n - HALO:, :])
        shifted.append(jnp.concatenate([rolled[:n - HALO, :], tail], axis=0))
    halo[...] = x[0:HALO, :]
    return shifted


def _matmul_ta(a, b, *, out_dtype, name, after=None):
    k_dim, m_dim = a.shape
    n_dim = b.shape[1]
    tm = _pick(m_dim, (1408, 1024, 768, 512, 384, 256, 128))
    tk = _pick(k_dim, (1408, 1024, 768, 512, 384, 256, 128))
    nk = k_dim // tk

    def body(a_ref, b_ref, *rest):
        o_ref, bt_ref, acc_ref = rest[-3:]
        i, k = pl.program_id(0), pl.program_id(1)

        @pl.when(i == 0)
        def _():
            bt_ref[k] = b_ref[...].astype(F32).T.astype(BF16)

        r = lax.dot_general(bt_ref[k], a_ref[...].astype(BF16), (((1,), (0,)), ((), ())), preferred_element_type=F32)

        @pl.when(k == 0)
        def _():
            acc_ref[...] = r

        @pl.when(k > 0)
        def _():
            acc_ref[...] += r

        @pl.when(k == nk - 1)
        def _():
            o_ref[...] = acc_ref[...].T.astype(o_ref.dtype)

    extra_specs, extra = ([], ()) if after is None else ([pl.BlockSpec(memory_space=pl.ANY)], (after,))
    return pl.pallas_call(
        body, grid=(m_dim // tm, nk),
        in_specs=[pl.BlockSpec((tk, tm), lambda i, k: (k, i)), pl.BlockSpec((tk, n_dim), lambda i, k: (k, 0))] + extra_specs,
        out_specs=pl.BlockSpec((tm, n_dim), lambda i, k: (i, 0)), out_shape=_sds((m_dim, n_dim), out_dtype),
        scratch_shapes=[pltpu.VMEM((nk, n_dim, tk), BF16), pltpu.VMEM((n_dim, tm), F32)],
        name=name, compiler_params=_params(2))(a, b, *extra)


def _matmul(a, b, *, ta=False, tb=False, out_dtype=F32, name, after=None):
    if ta:
        assert not tb
        return _matmul_ta(a, b, out_dtype=out_dtype, name=name, after=after)
    m_dim, k_dim = a.shape
    n_dim = b.shape[0] if tb else b.shape[1]
    tm = _pick(m_dim, (1408, 1024, 768, 512, 384, 256, 128))
    tn = _pick(n_dim, (1024, 1408, 768, 512, 384, 256, 128))
    tk = k_dim if k_dim <= 3072 else _pick(k_dim, (3072, 2816, 2048, 1024))
    nk = k_dim // tk
    dims = (((1,), (1 if tb else 0,)), ((), ()))

    use_acc = nk > 1 and out_dtype != F32

    def body(a_ref, b_ref, *rest):
        o_ref = rest[-2] if use_acc else rest[-1]
        acc_ref = rest[-1] if use_acc else o_ref
        r = lax.dot_general(a_ref[...].astype(BF16), b_ref[...].astype(BF16), dims, preferred_element_type=F32)
        if nk == 1:
            o_ref[...] = r.astype(o_ref.dtype)
        else:
            k = pl.program_id(2)

            @pl.when(k == 0)
            def _():
                acc_ref[...] = r

            @pl.when(k > 0)
            def _():
                acc_ref[...] += r

            if use_acc:
                @pl.when(k == nk - 1)
                def _():
                    o_ref[...] = acc_ref[...].astype(o_ref.dtype)

    a_spec = pl.BlockSpec((tm, tk), lambda i, j, k: (i, k))
    b_spec = pl.BlockSpec((tn, tk), lambda i, j, k: (j, k)) if tb else pl.BlockSpec((tk, tn), lambda i, j, k: (k, j))
    extra_specs, extra = ([], ()) if after is None else ([pl.BlockSpec(memory_space=pl.ANY)], (after,))
    return pl.pallas_call(
        body, grid=(m_dim // tm, n_dim // tn, nk), in_specs=[a_spec, b_spec] + extra_specs,
        out_specs=pl.BlockSpec((tm, tn), lambda i, j, k: (i, j)), out_shape=_sds((m_dim, n_dim), out_dtype),
        scratch_shapes=[pltpu.VMEM((tm, tn), F32)] if use_acc else [],
        name=name, compiler_params=_params(3))(a, b, *extra)


def _seq_rows(t_rows):
    return 384 if t_rows % 384 == 0 and t_rows >= 768 else CHUNK


def _token_rows(tr):
    if tr == CHUNK:
        return pl.BlockSpec((CHUNK, D_MODEL), lambda i: (jnp.maximum(i - 1, 0), 0))
    return pl.BlockSpec((pl.Element(tr), pl.Element(D_MODEL)),
                        lambda i: (pl.multiple_of(jnp.maximum(i * tr - CHUNK, 0), CHUNK), 0))


def _under_tile(rows_ref, head, i):
    rows = rows_ref[...]
    tr = rows.shape[0]
    first = head if tr == CHUNK else jnp.concatenate([head, rows[0:tr - CHUNK, :]], axis=0)
    return jnp.where(i == 0, first, rows)


def _seq_specs(tr=CHUNK):
    return [_token_rows(tr), _full((N_META, D_MODEL))]


def _seq_tile(x_ref, meta_ref, i):
    return _under_tile(x_ref, jnp.concatenate([jnp.zeros((META_PAD, D_MODEL), F32), meta_ref[...]], axis=0), i)


def _prenorm(x, meta, w):
    t_rows = x.shape[0] + CHUNK
    tr = _seq_rows(t_rows)

    def body(x_ref, meta_ref, w_ref, o_ref):
        h = _seq_tile(x_ref, meta_ref, pl.program_id(0))
        o_ref[...] = (h * _rms(h) * w_ref[...]).astype(BF16)

    return pl.pallas_call(body, grid=(t_rows // tr,), in_specs=_seq_specs(tr) + [_full((1, D_MODEL))],
                          out_specs=_row(tr, D_MODEL), out_shape=_sds((t_rows, D_MODEL), BF16),
                          name="prenorm", compiler_params=_params(1))(x, meta, w)


def _ssm_conv_fwd(proj, conv_w, conv_b):
    t_rows = proj.shape[0]
    tr = CHUNK

    def body(x_ref, w_ref, b_ref, xc_ref, xa_ref, hist):
        first = pl.program_id(0) == 0
        for c in range(0, CONV_DIM, STRIP):
            cols = slice(c, c + STRIP)
            acc = b_ref[:, cols]
            for s, moved in enumerate(_causal_taps(x_ref[:, cols], hist.at[:, cols], first, SSM_CONV)):
                acc = acc + w_ref[SSM_CONV - 1 - s:SSM_CONV - s, cols] * moved
            xc_ref[:, cols] = acc
            xa_ref[:, cols] = acc * _sigmoid(acc)

    return pl.pallas_call(
        body, grid=(t_rows // tr,),
        in_specs=[_row(tr, CONV_DIM, OFF_XBC // CONV_DIM), _full((SSM_CONV, CONV_DIM)), _full((1, CONV_DIM))],
        out_specs=[_row(tr, CONV_DIM), _row(tr, CONV_DIM)],
        out_shape=[_sds((t_rows, CONV_DIM), F32), _sds((t_rows, CONV_DIM), F32)],
        scratch_shapes=[pltpu.VMEM((HALO, CONV_DIM), F32)],
        name="ssm_conv_fwd", compiler_params=_params(1))(proj, conv_w, conv_b)


def _ssm_post(y, proj, w):
    t_rows = y.shape[0]
    tr = _pick(t_rows, (384, 128))

    def body(y_ref, z_ref, w_ref, o_ref):
        z = z_ref[...].astype(F32)
        yz = y_ref[...] * z * _sigmoid(z)
        o_ref[...] = (yz * _rms(yz) * w_ref[...]).astype(BF16)

    return pl.pallas_call(body, grid=(t_rows // tr,),
                          in_specs=[_row(tr, D_INNER), _row(tr, D_INNER, OFF_Z // D_INNER), _full((1, D_INNER))],
                          out_specs=_row(tr, D_INNER), out_shape=_sds((t_rows, D_INNER), BF16),
                          name="ssm_post", compiler_params=_params(1))(y, proj, w)


def _mix_fwd(proj, y_ssm, y_attn):
    t_rows = y_ssm.shape[0]
    tr = _pick(t_rows, (384, 128))

    def body(g_ref, ys_ref, ya_ref, o_ref):
        g = _sigmoid(g_ref[...].astype(F32))
        o_ref[...] = (g[:, :D_MODEL] * ys_ref[...] + g[:, D_MODEL:] * ya_ref[...]).astype(BF16)

    return pl.pallas_call(body, grid=(t_rows // tr,),
                          in_specs=[_row(tr, 2 * D_MODEL, OFF_GATE // (2 * D_MODEL)), _row(tr, D_MODEL),
                                    _row(tr, D_MODEL)],
                          out_specs=_row(tr, D_MODEL), out_shape=_sds((t_rows, D_MODEL), BF16),
                          name="mix_fwd", compiler_params=_params(1))(proj, y_ssm, y_attn)


def _postmix(x, meta, mix, w_post, w_pre):
    t_rows = mix.shape[0]
    tr = _seq_rows(t_rows)

    def body(x_ref, meta_ref, m_ref, wp_ref, wf_ref, h1_ref, hn_ref):
        m = m_ref[...]
        h1 = _seq_tile(x_ref, meta_ref, pl.program_id(0)) + m * _rms(m) * wp_ref[...]
        h1 = jnp.where(_row_ids(h1.shape, pl.program_id(0), tr) >= META_PAD, h1, 0.0)
        h1_ref[...] = h1
        hn_ref[...] = (h1 * _rms(h1) * wf_ref[...]).astype(BF16)

    return pl.pallas_call(body, grid=(t_rows // tr,),
                          in_specs=_seq_specs(tr) + [_row(tr, D_MODEL), _full((1, D_MODEL)), _full((1, D_MODEL))],
                          out_specs=[_row(tr, D_MODEL), _row(tr, D_MODEL)],
                          out_shape=[_sds((t_rows, D_MODEL), F32), _sds((t_rows, D_MODEL), BF16)],
                          name="postmix", compiler_params=_params(1))(x, meta, mix, w_post, w_pre)


def _ffn_act(up, conv_w, conv_b):
    t_rows = up.shape[0]
    tr = CHUNK
    width = 2 * FFN_DIM

    def body(up_ref, w_ref, b_ref, u_ref, act_ref, hist):
        first = pl.program_id(0) == 0
        for c in range(0, FFN_DIM, STRIP):
            halves = []
            for base in (0, FFN_DIM):
                cols = slice(base + c, base + c + STRIP)
                u = b_ref[:, cols]
                for s, moved in enumerate(_causal_taps(up_ref[:, cols].astype(F32), hist.at[:, cols], first, FFN_CONV)):
                    u = u + w_ref[FFN_CONV - 1 - s:FFN_CONV - s, cols] * moved
                u_ref[:, cols] = u.astype(BF16)
                halves.append(u)
            a, g = halves
            act_ref[:, c:c + STRIP] = (a * _sigmoid(a) * g).astype(BF16)

    return pl.pallas_call(
        body, grid=(t_rows // tr,), in_specs=[_row(tr, width), _full((FFN_CONV, width)), _full((1, width))],
        out_specs=[_row(tr, width), _row(tr, FFN_DIM)],
        out_shape=[_sds((t_rows, width), BF16), _sds((t_rows, FFN_DIM), BF16)],
        scratch_shapes=[pltpu.VMEM((HALO, width), F32)],
        name="ffn_act", compiler_params=_params(1))(up, conv_w, conv_b)


def _final(h1, f, target, w):
    t_rows = h1.shape[0]
    tr = _seq_rows(t_rows)

    def body(h1_ref, f_ref, t_ref, w_ref, df_ref, dy_ref, dw_ref, loss_ref):
        i = pl.program_id(0)

        @pl.when(i == 0)
        def _():
            dw_ref[...] = jnp.zeros_like(dw_ref)
            loss_ref[...] = jnp.zeros_like(loss_ref)

        f_val = f_ref[...]
        r = _rms(f_val)
        wv = w_ref[...]
        h2 = h1_ref[...] + f_val * r * wv
        tgt = _under_tile(t_ref, jnp.zeros((CHUNK, D_MODEL), F32), i)
        diff = jnp.where(_row_ids(h2.shape, i, tr) >= CHUNK, h2 - tgt, 0.0)
        loss_ref[...] += 0.5 * jnp.sum(diff * diff) * (1.0 / D_MODEL)
        dy = diff * (1.0 / D_MODEL)
        dy_ref[...] = dy
        df, dw = _rms_bwd(f_val, r, wv, dy)
        df_ref[...] = df.astype(BF16)
        dw_ref[...] += dw

    return pl.pallas_call(
        body, grid=(t_rows // tr,),
        in_specs=[_row(tr, D_MODEL), _row(tr, D_MODEL), _token_rows(tr), _full((1, D_MODEL))],
        out_specs=[_row(tr, D_MODEL), _row(tr, D_MODEL), _full((1, D_MODEL)), _full((1, 128))],
        out_shape=[_sds((t_rows, D_MODEL), BF16), _sds((t_rows, D_MODEL), F32), _sds((1, D_MODEL), F32), _sds((1, 128), F32)],
        name="final", compiler_params=_params(1))(h1, f, target, w)


def _ffn_act_bwd(u, up, dact, conv_w):
    t_rows = u.shape[0]
    tr = CHUNK
    nt = t_rows // tr
    width = 2 * FFN_DIM

    def body(u_ref, up_ref, da_ref, w_ref, dup_ref, dw_ref, db_ref, ahead):
        @pl.when(pl.program_id(0) == 0)
        def _():
            dw_ref[...] = jnp.zeros_like(dw_ref)
            db_ref[...] = jnp.zeros_like(db_ref)

        first = pl.program_id(0) == 0
        for c in range(0, FFN_DIM, STRIP_BWD):
            ca, cg = slice(c, c + STRIP_BWD), slice(FFN_DIM + c, FFN_DIM + c + STRIP_BWD)
            a, g, d = u_ref[:, ca].astype(F32), u_ref[:, cg].astype(F32), da_ref[:, ca].astype(F32)
            s = _sigmoid(a)
            for cols, du in ((ca, d * g * s * (1.0 + a * (1.0 - s))), (cg, d * a * s)):
                x = up_ref[:, cols].astype(F32)
                dup = None
                for sh, moved in enumerate(_anticausal_taps(du, ahead.at[:, cols], first, FFN_CONV)):
                    k = FFN_CONV - 1 - sh
                    term = w_ref[k:k + 1, cols] * moved
                    dup = term if dup is None else dup + term
                    dw_ref[k:k + 1, cols] += jnp.sum(moved * x, axis=0, keepdims=True)
                db_ref[:, cols] += jnp.sum(du, axis=0, keepdims=True)
                dup_ref[:, cols] = dup.astype(BF16)

    return pl.pallas_call(
        body, grid=(nt,),
        in_specs=[_row_rev(tr, width, nt), _row_rev(tr, width, nt), _row_rev(tr, FFN_DIM, nt), _full((FFN_CONV, width))],
        out_specs=[_row_rev(tr, width, nt), _full((FFN_CONV, width)), _full((1, width))],
        out_shape=[_sds((t_rows, width), BF16), _sds((FFN_CONV, width), F32), _sds((1, width), F32)],
        scratch_shapes=[pltpu.VMEM((HALO, width), F32)],
        name="ffn_act_bwd", compiler_params=_params(1))(u, up, dact, conv_w)


def _postmix_bwd(h1, dhn2, dy, mix, w_pre, w_post):
    t_rows = h1.shape[0]
    tr = _pick(t_rows, (384, 128))

    def body(h1_ref, dhn_ref, dy_ref, m_ref, wf_ref, wp_ref, dmix_ref, dh_ref, dwf_ref, dwp_ref):
        @pl.when(pl.program_id(0) == 0)
        def _():
            dwf_ref[...] = jnp.zeros_like(dwf_ref)
            dwp_ref[...] = jnp.zeros_like(dwp_ref)

        h1v = h1_ref[...]
        dx, dwf = _rms_bwd(h1v, _rms(h1v), wf_ref[...], dhn_ref[...])
        dwf_ref[...] += dwf
        dh1 = dy_ref[...] + dx
        dh1 = jnp.where(_row_ids(dh1.shape, pl.program_id(0), tr) >= META_PAD, dh1, 0.0)
        dh_ref[...] = dh1
        m = m_ref[...]
        dmix, dwp = _rms_bwd(m, _rms(m), wp_ref[...], dh1)
        dwp_ref[...] += dwp
        dmix_ref[...] = dmix.astype(BF16)

    return pl.pallas_call(
        body, grid=(t_rows // tr,),
        in_specs=[_row(tr, D_MODEL) for _ in range(4)] + [_full((1, D_MODEL))] * 2,
        out_specs=[_row(tr, D_MODEL), _row(tr, D_MODEL), _full((1, D_MODEL)), _full((1, D_MODEL))],
        out_shape=[_sds((t_rows, D_MODEL), BF16), _sds((t_rows, D_MODEL), F32), _sds((1, D_MODEL), F32), _sds((1, D_MODEL), F32)],
        name="postmix_bwd", compiler_params=_params(1))(h1, dhn2, dy, mix, w_pre, w_post)


_ANY = pl.BlockSpec(memory_space=pl.ANY)


def _mix_bwd(dmixed, proj, y_ssm, y_attn, dproj):
    t_rows = dmixed.shape[0]
    tr = _pick(t_rows, (384, 128))

    def body(d_ref, g_ref, ys_ref, ya_ref, _, dys_ref, dya_ref, dg_ref):
        d = d_ref[...]
        g = _sigmoid(g_ref[...].astype(F32))
        g1, g2 = g[:, :D_MODEL], g[:, D_MODEL:]
        dys_ref[...] = (d * g1).astype(BF16)
        dya_ref[...] = (d * g2).astype(BF16)
        dg_ref[...] = jnp.concatenate([d * ys_ref[...] * g1 * (1.0 - g1), d * ya_ref[...] * g2 * (1.0 - g2)],
                                      axis=1).astype(BF16)

    return pl.pallas_call(
        body, grid=(t_rows // tr,),
        in_specs=[_row(tr, D_MODEL), _row(tr, 2 * D_MODEL, OFF_GATE // (2 * D_MODEL)), _row(tr, D_MODEL), _row(tr, D_MODEL),
                  _ANY],
        out_specs=[_row(tr, D_MODEL), _row(tr, D_MODEL), _row(tr, 2 * D_MODEL, OFF_GATE // (2 * D_MODEL))],
        out_shape=[_sds((t_rows, D_MODEL), BF16), _sds((t_rows, D_MODEL), BF16), _sds(dproj.shape, dproj.dtype)],
        input_output_aliases={4: 2},
        name="mix_bwd", compiler_params=_params(1))(dmixed, proj, y_ssm, y_attn, dproj)


def _ssm_post_bwd(y, proj, dyn, w, dproj):
    t_rows = y.shape[0]
    tr = CHUNK

    def body(y_ref, z_ref, d_ref, w_ref, _, dy_ref, dz_ref, dw_ref):
        @pl.when(pl.program_id(0) == 0)
        def _():
            dw_ref[...] = jnp.zeros_like(dw_ref)

        yv, z = y_ref[...], z_ref[...].astype(F32)
        sz = _sigmoid(z)
        silu = z * sz
        yz = yv * silu
        dyz, dw = _rms_bwd(yz, _rms(yz), w_ref[...], d_ref[...].astype(F32))
        dw_ref[...] += dw
        dy_ref[...] = dyz * silu
        dz_ref[...] = (dyz * yv * sz * (1.0 + z * (1.0 - sz))).astype(BF16)

    return pl.pallas_call(
        body, grid=(t_rows // tr,),
        in_specs=[_row(tr, D_INNER), _row(tr, D_INNER, OFF_Z // D_INNER), _row(tr, D_INNER), _full((1, D_INNER)), _ANY],
        out_specs=[_row(tr, D_INNER), _row(tr, D_INNER, OFF_Z // D_INNER), _full((1, D_INNER))],
        out_shape=[_sds((t_rows, D_INNER), F32), _sds(dproj.shape, dproj.dtype), _sds((1, D_INNER), F32)],
        input_output_aliases={4: 1},
        name="ssm_post_bwd", compiler_params=_params(1))(y, proj, dyn, w, dproj)


def _ssm_conv_bwd(xc, proj, dxs, dbm, dcm, conv_w, dproj):
    t_rows = xc.shape[0]
    tr = CHUNK
    nt = t_rows // tr
    bc_w = SSM_GROUPS * D_STATE

    def body(xc_ref, x_ref, dxs_ref, db_ref, dc_ref, w_ref, _, dx_ref, dw_ref, dbias_ref, ahead):
        first = pl.program_id(0) == 0

        @pl.when(first)
        def _():
            dw_ref[...] = jnp.zeros_like(dw_ref)
            dbias_ref[...] = jnp.zeros_like(dbias_ref)

        for c0 in range(0, CONV_DIM, STRIP_BWD):
            cols = slice(c0, c0 + STRIP_BWD)
            if c0 < D_INNER:
                dact = dxs_ref[:, cols]
            elif c0 < D_INNER + bc_w:
                dact = db_ref[:, c0 - D_INNER:c0 - D_INNER + STRIP_BWD]
            else:
                dact = dc_ref[:, c0 - D_INNER - bc_w:c0 - D_INNER - bc_w + STRIP_BWD]
            c = xc_ref[:, cols]
            s = _sigmoid(c)
            dpre = dact * s * (1.0 + c * (1.0 - s))
            x = x_ref[:, cols]
            dx = None
            for sh, moved in enumerate(_anticausal_taps(dpre, ahead.at[:, cols], first, SSM_CONV)):
                k = SSM_CONV - 1 - sh
                term = w_ref[k:k + 1, cols] * moved
                dx = term if dx is None else dx + term
                dw_ref[k:k + 1, cols] += jnp.sum(moved * x, axis=0, keepdims=True)
            dbias_ref[:, cols] += jnp.sum(dpre, axis=0, keepdims=True)
            dx_ref[:, cols] = dx.astype(BF16)

    xbc_block = OFF_XBC // CONV_DIM
    return pl.pallas_call(
        body, grid=(nt,),
        in_specs=[_row_rev(tr, CONV_DIM, nt), _row_rev(tr, CONV_DIM, nt, xbc_block), _row_rev(tr, D_INNER, nt),
                  _row_rev(tr, bc_w, nt), _row_rev(tr, bc_w, nt), _full((SSM_CONV, CONV_DIM)), _ANY],
        out_specs=[_row_rev(tr, CONV_DIM, nt, xbc_block), _full((SSM_CONV, CONV_DIM)), _full((1, CONV_DIM))],
        out_shape=[_sds(dproj.shape, dproj.dtype), _sds((SSM_CONV, CONV_DIM), F32), _sds((1, CONV_DIM), F32)],
        scratch_shapes=[pltpu.VMEM((HALO, CONV_DIM), F32)],
        input_output_aliases={6: 0},
        name="ssm_conv_bwd", compiler_params=_params(1))(xc, proj, dxs, dbm, dcm, conv_w, dproj)


def _prenorm_bwd(x, meta, dhn, dh, w):
    seq = x.shape[0]
    tr = _pick(seq, (512, 128))

    def body(x_ref, meta_ref, d_ref, r_ref, d0_ref, r0_ref, w_ref, dx_ref, dmeta_ref, dw_ref):
        wv = w_ref[...]

        @pl.when(pl.program_id(0) == 0)
        def _():
            h0 = jnp.concatenate([jnp.zeros((META_PAD, D_MODEL), F32), meta_ref[...]], axis=0)
            dx0, dw0 = _rms_bwd(h0, _rms(h0), wv, d0_ref[...])
            dw_ref[...] = dw0
            dmeta_ref[...] = (r0_ref[...] + dx0)[META_PAD:, :]

        h = x_ref[...]
        dx, dw = _rms_bwd(h, _rms(h), wv, d_ref[...])
        dw_ref[...] += dw
        dx_ref[...] = r_ref[...] + dx

    def shifted(tile):
        return pl.BlockSpec((pl.Element(tile), pl.Element(D_MODEL)), lambda i: (pl.multiple_of(i * tile + CHUNK, CHUNK), 0))

    first = pl.BlockSpec((CHUNK, D_MODEL), lambda i: (0, 0))
    return pl.pallas_call(
        body, grid=(seq // tr,),
        in_specs=[_row(tr, D_MODEL), _full((N_META, D_MODEL)), shifted(tr), shifted(tr), first, first, _full((1, D_MODEL))],
        out_specs=[_row(tr, D_MODEL), _full((N_META, D_MODEL)), _full((1, D_MODEL))],
        out_shape=[_sds((seq, D_MODEL), F32), _sds((N_META, D_MODEL), F32), _sds((1, D_MODEL), F32)],
        name="prenorm_bwd", compiler_params=_params(1))(x, meta, dhn, dh, dhn, dh, w)


def _dot01(x, m01, x_left, parts):
    acc, rest = None, x
    for i in range(parts):
        piece = rest.astype(BF16)
        term = (jnp.dot(piece, m01, preferred_element_type=F32) if x_left
                else jnp.dot(m01, piece, preferred_element_type=F32))
        acc = term if acc is None else acc + term
        if i + 1 < parts:
            rest = rest - piece.astype(F32)
    return acc


def _ssd_common(dtr_ref, dt_bias, a_log, chunk_index):
    rows = lax.broadcasted_iota(jnp.int32, (CHUNK, CHUNK), 0)
    cols = lax.broadcasted_iota(jnp.int32, (CHUNK, CHUNK), 1)
    low = rows >= cols
    raw = dtr_ref[:, :128]
    for g in range(1, SSM_GROUPS):
        raw = raw + pltpu.roll(dtr_ref[:, g * 128:(g + 1) * 128], HEADS_PER_GROUP * g, 1)
    raw = raw + dt_bias
    live = _row_ids(raw.shape, chunk_index, CHUNK) >= META_PAD
    dt = jnp.where(live, _softplus(raw), 0.0)
    a_head = -jnp.exp(a_log)
    cs = _dot01(dt * a_head, low.astype(BF16), False, 3)
    return dict(low=low, triu=(rows <= cols).astype(BF16), raw=raw, live=live, dt=dt, a_head=a_head, cs=cs, cs_t=cs.T,
                grow=jnp.exp(cs),
                fade=jnp.exp(cs[CHUNK - 1:CHUNK, :] - cs))


def _ssd_expand(cm, g):
    first = HEADS_PER_GROUP * g
    expand = (lax.broadcasted_iota(jnp.int32, (CHUNK, GROUP_W), 1) // HEAD_P + first
              == lax.broadcasted_iota(jnp.int32, (CHUNK, GROUP_W), 0)).astype(BF16)
    fold = (lax.broadcasted_iota(jnp.int32, (GROUP_W, CHUNK), 0) // HEAD_P + first
            == lax.broadcasted_iota(jnp.int32, (GROUP_W, CHUNK), 1)).astype(BF16)
    return dict(fold=fold, dtx=_dot01(cm["dt"], expand, True, 2), growx=_dot01(cm["grow"], expand, True, 2),
                fadex=_dot01(cm["fade"], expand, True, 2))


def _decay_matrix(cm, j):
    diff = cm["cs"][:, j:j + 1] - cm["cs_t"][j:j + 1, :]
    return jnp.where(cm["low"], jnp.exp(jnp.where(cm["low"], diff, 0.0)), 0.0)


def _dot(a, b, dims):
    return lax.dot_general(a.astype(BF16), b.astype(BF16), (dims, ((), ())), preferred_element_type=F32)


def _dot_fine(a, b, dims):
    a_hi, b_hi = a.astype(BF16), b.astype(BF16)
    a_lo, b_lo = (a - a_hi.astype(F32)).astype(BF16), (b - b_hi.astype(F32)).astype(BF16)
    dn = (dims, ((), ()))
    return (lax.dot_general(a_hi, b_hi, dn, preferred_element_type=F32)
            + lax.dot_general(a_hi, b_lo, dn, preferred_element_type=F32)
            + lax.dot_general(a_lo, b_hi, dn, preferred_element_type=F32))


def _ssd_specs(nt, rev):
    def idx(c):
        return nt - 1 - c if rev else c
    bc_w = SSM_GROUPS * D_STATE
    xs = pl.BlockSpec((CHUNK, D_INNER), lambda c: (idx(c), 0))
    bm = pl.BlockSpec((CHUNK, bc_w), lambda c: (idx(c), D_INNER // bc_w))
    cm = pl.BlockSpec((CHUNK, bc_w), lambda c: (idx(c), D_INNER // bc_w + 1))
    dtr = pl.BlockSpec((CHUNK, SSM_GROUPS * 128), lambda c: (idx(c), OFF_DT // (SSM_GROUPS * 128)))
    par = _full((1, 128))
    par_x = _full((SSM_GROUPS, 1, GROUP_W))
    return xs, bm, cm, dtr, par, par_x, idx


def _group_cols(g, width):
    return slice(g * width, (g + 1) * width)


def _ssd_fwd(xact, proj, dtb, alog, dskip_x):
    t_rows = xact.shape[0]
    nt = t_rows // CHUNK
    xs_spec, b_spec, c_spec, dtr_spec, par, par_x, _ = _ssd_specs(nt, False)

    def body(xs_ref, b_ref, c_ref, dtr_ref, dtb_ref, alog_ref, dsk_ref, y_ref, hst_ref, state):
        c = pl.program_id(0)

        @pl.when(c == 0)
        def _():
            state[...] = jnp.zeros_like(state)

        cm = _ssd_common(dtr_ref, dtb_ref[...], alog_ref[...], c)
        for g in range(SSM_GROUPS):
            wide, narrow = _group_cols(g, GROUP_W), _group_cols(g, D_STATE)
            ex = _ssd_expand(cm, g)
            xs, bm, cmat = xs_ref[:, wide], b_ref[:, narrow], c_ref[:, narrow]
            x_dt = xs * ex["dtx"]
            h_in = state[g]
            hst_ref[0, g] = h_in
            y_ref[:, wide] = _dot(cmat, h_in, ((1,), (0,))) * ex["growx"] + xs * dsk_ref[g]
            cb = _dot(cmat, bm, ((1,), (1,)))
            for j in range(HEADS_PER_GROUP):
                sl = slice(g * GROUP_W + j * HEAD_P, g * GROUP_W + (j + 1) * HEAD_P)
                decay = _decay_matrix(cm, HEADS_PER_GROUP * g + j)
                y_ref[:, sl] += _dot(cb * decay, x_dt[:, j * HEAD_P:(j + 1) * HEAD_P], ((1,), (0,)))
            state[g] = h_in * ex["growx"][CHUNK - 1:CHUNK, :] + _dot_fine(bm, x_dt * ex["fadex"], ((0,), (0,)))

    return pl.pallas_call(
        body, grid=(nt,),
        in_specs=[xs_spec, b_spec, c_spec, dtr_spec, par, par, par_x],
        out_specs=[xs_spec, pl.BlockSpec((1, SSM_GROUPS, D_STATE, GROUP_W), lambda c: (c, 0, 0, 0))],
        out_shape=[_sds((t_rows, D_INNER), F32), _sds((nt, SSM_GROUPS, D_STATE, GROUP_W), F32)],
        scratch_shapes=[pltpu.VMEM((SSM_GROUPS, D_STATE, GROUP_W), F32)],
        name="ssd_fwd", compiler_params=_params(1))(xact, xact, xact, proj, dtb, alog, dskip_x)


def _ssd_bwd(xact, proj, dtb, alog, dskip_x, dy, hst, dproj):
    t_rows = xact.shape[0]
    nt = t_rows // CHUNK
    xs_spec, b_spec, c_spec, dtr_spec, par, par_x, idx = _ssd_specs(nt, True)
    h_spec = pl.BlockSpec((1, SSM_GROUPS, D_STATE, GROUP_W), lambda c: (idx(c), 0, 0, 0))
    hn_spec = pl.BlockSpec((1, SSM_GROUPS, D_STATE, GROUP_W), lambda c: (jnp.minimum(idx(c) + 1, nt - 1), 0, 0, 0))
    bc_out = pl.BlockSpec((CHUNK, SSM_GROUPS * D_STATE), lambda c: (idx(c), 0))

    def body(xs_ref, b_ref, c_ref, dtr_ref, dtb_ref, alog_ref, dsk_ref, dy_ref, h_ref, hn_ref, _,
             dxs_ref, db_ref, dc_ref, ddt_ref, dalog_ref, ddtb_ref, dd_ref, dstate, dx_buf):
        step = pl.program_id(0)

        @pl.when(step == 0)
        def _():
            dstate[...] = jnp.zeros_like(dstate)
            dalog_ref[...] = jnp.zeros_like(dalog_ref)
            ddtb_ref[...] = jnp.zeros_like(ddtb_ref)
            dd_ref[...] = jnp.zeros_like(dd_ref)

        cm = _ssd_common(dtr_ref, dtb_ref[...], alog_ref[...], idx(step))
        for g in range(SSM_GROUPS):
            _ssd_bwd_group(g, cm, xs_ref, b_ref, c_ref, dsk_ref, dy_ref, h_ref, hn_ref,
                           dxs_ref, db_ref, dc_ref, ddt_ref, dalog_ref, ddtb_ref, dd_ref, dstate, dx_buf)

    return pl.pallas_call(
        body, grid=(nt,),
        in_specs=[xs_spec, b_spec, c_spec, dtr_spec, par, par, par_x, xs_spec, h_spec, hn_spec, _ANY],
        out_specs=[xs_spec, bc_out, bc_out, dtr_spec, par, par, par_x],
        out_shape=[_sds((t_rows, D_INNER), F32), _sds((t_rows, SSM_GROUPS * D_STATE), F32),
                   _sds((t_rows, SSM_GROUPS * D_STATE), F32), _sds(dproj.shape, dproj.dtype),
                   _sds((1, 128), F32), _sds((1, 128), F32), _sds((SSM_GROUPS, 1, GROUP_W), F32)],
        scratch_shapes=[pltpu.VMEM((SSM_GROUPS, D_STATE, GROUP_W), F32), pltpu.VMEM((CHUNK, GROUP_W), F32)],
        input_output_aliases={10: 3},
        name="ssd_bwd", compiler_params=_params(1))(xact, xact, xact, proj, dtb, alog, dskip_x, dy, hst, hst, dproj)


def _ssd_bwd_group(g, cm, xs_ref, b_ref, c_ref, dsk_ref, dy_ref, h_ref, hn_ref,
                   dxs_ref, db_ref, dc_ref, ddt_ref, dalog_ref, ddtb_ref, dd_ref, dstate, dx_buf):
    wide, narrow = _group_cols(g, GROUP_W), _group_cols(g, D_STATE)
    first = HEADS_PER_GROUP * g
    ex = _ssd_expand(cm, g)
    xs, bm, cmat = xs_ref[:, wide], b_ref[:, narrow], c_ref[:, narrow]
    dsk = dsk_ref[g]
    x_dt = xs * ex["dtx"]
    h_in, h_next = h_ref[0, g], hn_ref[0, g]
    dyv = dy_ref[:, wide]
    dh = dstate[g]
    grow, fade = ex["growx"], ex["fadex"]
    dy_grow = dyv * grow
    x_fade = x_dt * fade
    cb = _dot(cmat, bm, ((1,), (1,)))
    ml = jnp.zeros((CHUNK, CHUNK), F32)
    row_id = lax.broadcasted_iota(jnp.int32, (CHUNK, CHUNK), 0)
    col_id = lax.broadcasted_iota(jnp.int32, (CHUNK, CHUNK), 1)
    w_rows = jnp.zeros((CHUNK, CHUNK), F32)
    w_cols = jnp.zeros((CHUNK, CHUNK), F32)
    for j in range(HEADS_PER_GROUP):
        sl = slice(j * HEAD_P, (j + 1) * HEAD_P)
        lm = _decay_matrix(cm, first + j)
        mlj = _dot(dyv[:, sl], x_dt[:, sl], ((1,), (1,))) * lm
        ml = ml + mlj
        wm = mlj * cb
        w_rows = jnp.where(col_id == first + j, jnp.sum(wm, axis=1, keepdims=True), w_rows)
        w_cols = jnp.where(row_id == first + j, jnp.sum(wm, axis=0, keepdims=True), w_cols)
        dx_buf[:, sl] = _dot(cb * lm, dyv[:, sl], ((0,), (0,)))
    dx_off = fade * _dot_fine(bm, dh, ((1,), (0,)))
    dx = dx_buf[...] + dx_off
    dc_ref[:, narrow] = _dot(ml, bm, ((1,), (0,))) + _dot(dy_grow, h_in, ((1,), (1,)))
    db_ref[:, narrow] = _dot(ml, cmat, ((0,), (0,))) + _dot(x_fade, dh, ((1,), (1,)))
    fold = ex["fold"]
    y_off = _dot_fine(cmat, h_in, ((1,), (0,))) * grow
    dcs = (w_rows - w_cols.T) + _dot01(dyv * y_off - x_dt * dx_off, fold, True, 2)
    tail = jnp.broadcast_to(jnp.sum(dh * h_next, axis=0, keepdims=True), (8, GROUP_W))
    tail = _dot01(tail, fold, True, 2)[0:1, :]
    last_row = lax.broadcasted_iota(jnp.int32, (CHUNK, 128), 0) == CHUNK - 1
    dcs = dcs + jnp.where(last_row, tail, 0.0)
    da = _dot01(dcs, cm["triu"], False, 3)
    ddt = da * cm["a_head"] + _dot01(dx * xs, fold, True, 2)
    ddt_raw = jnp.where(cm["live"], ddt * _sigmoid(cm["raw"]), 0.0)
    ddt_ref[:, narrow] = (ddt_raw if g == 0 else pltpu.roll(ddt_raw, 128 - first, 1)).astype(BF16)
    ddtb_ref[...] += jnp.sum(ddt_raw, axis=0, keepdims=True)
    dalog_ref[...] += jnp.sum(da * cm["dt"], axis=0, keepdims=True) * cm["a_head"]
    dd_ref[g] += jnp.sum(dyv * xs, axis=0, keepdims=True)
    dxs_ref[:, wide] = dx * ex["dtx"] + dyv * dsk
    dstate[g] = dh * grow[CHUNK - 1:CHUNK, :] + _dot_fine(cmat, dy_grow, ((0,), (0,)))


def _swa_bias():
    rows_q = ATTN_GROUP * CHUNK
    dist = (jnp.arange(rows_q) % CHUNK)[:, None] - jnp.arange(2 * CHUNK)[None, :] + CHUNK
    head = jnp.arange(KV_HEADS)[:, None] * ATTN_GROUP + jnp.arange(rows_q)[None, :] // CHUNK + 1
    slope = jnp.exp2(-8.0 * head.astype(F32) / ATTN_HEADS)
    return jnp.where((dist >= 0) & (dist < CHUNK), -slope[:, :, None] * dist.astype(F32)[None], NEG)


def _swa_probs(q_kv, k_prev, k_cur, k_first, sink, bias, n):
    rows_q = ATTN_GROUP * CHUNK
    qs = jnp.concatenate([q_kv[:, g * DH:(g + 1) * DH] for g in range(ATTN_GROUP)], axis=0) * (DH ** -0.5)
    kcat = jnp.concatenate([k_prev, k_cur], axis=0)
    kmeta = k_first[META_PAD:, :]
    key_ok = lax.broadcasted_iota(jnp.int32, (1, 2 * CHUNK), 1) + n * CHUNK >= 2 * CHUNK
    s_band = jnp.where(key_ok, _dot(qs, kcat, ((1,), (1,))) + bias, NEG)
    q_pos = lax.broadcasted_iota(jnp.int32, (rows_q, N_META), 0) % CHUNK + n * CHUNK - META_PAD
    ok_m = lax.broadcasted_iota(jnp.int32, (rows_q, N_META), 1) <= q_pos
    s_meta = jnp.where(ok_m, _dot(qs, kmeta, ((1,), (1,))), NEG)
    m = jnp.maximum(jnp.maximum(jnp.max(s_band, axis=1, keepdims=True), jnp.max(s_meta, axis=1, keepdims=True)), sink)
    p_band, p_meta, p_sink = jnp.exp(s_band - m), jnp.exp(s_meta - m), jnp.exp(sink - m)
    inv = 1.0 / (jnp.sum(p_band, axis=1, keepdims=True) + jnp.sum(p_meta, axis=1, keepdims=True) + p_sink)
    return qs, kcat, kmeta, p_band * inv, p_meta * inv, p_sink * inv


def _swa_specs(nt, rev):
    def idx(n):
        return nt - 1 - n if rev else n
    o = pl.BlockSpec((CHUNK, ATTN_HEADS * DH), lambda n: (idx(n), 0))
    chunks = (lambda c: jnp.maximum(c - 1, 0)), (lambda c: c), (lambda c: 0)
    qkv = [pl.BlockSpec((CHUNK, QKV_W), lambda n, f=f: (f(idx(n)), OFF_Q // QKV_W)) for f in chunks]
    sink = _full((KV_HEADS, ATTN_GROUP * CHUNK, 1))
    bias = _full((KV_HEADS, ATTN_GROUP * CHUNK, 2 * CHUNK))
    return o, qkv, sink, bias, idx


def _head_cols(k):
    kv_w = ATTN_GROUP * DH
    q0, k0, v0 = k * kv_w, OFF_K - OFF_Q + k * DH, OFF_V - OFF_Q + k * DH
    return slice(q0, q0 + kv_w), slice(k0, k0 + DH), slice(v0, v0 + DH)


def _swa_fwd(proj, sink_rows, bias):
    t_rows = proj.shape[0]
    nt = t_rows // CHUNK
    o_spec, qkv_specs, sink_spec, bias_spec, _ = _swa_specs(nt, False)
    kv_w = ATTN_GROUP * DH

    def body(prev_ref, cur_ref, first_ref, sink_ref, bias_ref, o_ref):
        n = pl.program_id(0)
        for k in range(KV_HEADS):
            qc, kc, vc = _head_cols(k)
            _, _, _, p_band, p_meta, _ = _swa_probs(cur_ref[:, qc], prev_ref[:, kc], cur_ref[:, kc], first_ref[:, kc],
                                                    sink_ref[k], bias_ref[k], n)
            vcat = jnp.concatenate([prev_ref[:, vc], cur_ref[:, vc]], axis=0)
            out = _dot(p_band, vcat, ((1,), (0,))) + _dot(p_meta, first_ref[:, vc][META_PAD:, :], ((1,), (0,)))
            for g in range(ATTN_GROUP):
                o_ref[:, k * kv_w + g * DH:k * kv_w + (g + 1) * DH] = out[g * CHUNK:(g + 1) * CHUNK, :]

    return pl.pallas_call(
        body, grid=(nt,), in_specs=qkv_specs + [sink_spec, bias_spec],
        out_specs=o_spec, out_shape=_sds((t_rows, ATTN_HEADS * DH), F32),
        name="swa_fwd", compiler_params=_params(1))(proj, proj, proj, sink_rows, bias)


def _swa_bwd(proj, sink_rows, bias, out, dout, dproj):
    t_rows = proj.shape[0]
    nt = t_rows // CHUNK
    o_spec, qkv_specs, sink_spec, bias_spec, idx = _swa_specs(nt, True)
    kv_w = ATTN_GROUP * DH
    k_off, v_off = OFF_K - OFF_Q, OFF_V - OFF_Q

    def body(prev_ref, cur_ref, first_ref, sink_ref, bias_ref, o_ref, do_ref, _, dqkv_ref, dsink_ref,
             carry_k, carry_v, meta_k, meta_v, dqkv_buf):
        step = pl.program_id(0)
        n = idx(step)

        @pl.when(step == 0)
        def _():
            carry_k[...] = jnp.zeros_like(carry_k)
            carry_v[...] = jnp.zeros_like(carry_v)
            meta_k[...] = jnp.zeros_like(meta_k)
            meta_v[...] = jnp.zeros_like(meta_v)
            dsink_ref[...] = jnp.zeros_like(dsink_ref)

        for k in range(KV_HEADS):
            cols = slice(k * kv_w, (k + 1) * kv_w)
            hd = slice(k * DH, (k + 1) * DH)
            qc, kc, vc = _head_cols(k)
            qs, kcat, kmeta, p_band, p_meta, p_sink = _swa_probs(cur_ref[:, qc], prev_ref[:, kc], cur_ref[:, kc],
                                                                 first_ref[:, kc], sink_ref[k], bias_ref[k], n)
            vcat = jnp.concatenate([prev_ref[:, vc], cur_ref[:, vc]], axis=0)
            vmeta = first_ref[:, vc][META_PAD:, :]
            o, do = o_ref[:, cols], do_ref[:, cols]
            os_ = jnp.concatenate([o[:, g * DH:(g + 1) * DH] for g in range(ATTN_GROUP)], axis=0)
            dos = jnp.concatenate([do[:, g * DH:(g + 1) * DH] for g in range(ATTN_GROUP)], axis=0)
            delta = jnp.sum(dos * os_, axis=1, keepdims=True)
            ds_band = p_band * (_dot(dos, vcat, ((1,), (1,))) - delta)
            ds_meta = p_meta * (_dot(dos, vmeta, ((1,), (1,))) - delta)
            ds_sink = -p_sink * delta
            dqs = (_dot(ds_band, kcat, ((1,), (0,))) + _dot(ds_meta, kmeta, ((1,), (0,)))) * (DH ** -0.5)
            for g in range(ATTN_GROUP):
                dqkv_buf[:, k * kv_w + g * DH:k * kv_w + (g + 1) * DH] = dqs[g * CHUNK:(g + 1) * CHUNK, :]
                dsink_ref[k, g:g + 1, :] += jnp.sum(ds_sink[g * CHUNK:(g + 1) * CHUNK, :])
            dkcat = _dot(ds_band, qs, ((0,), (0,)))
            dvcat = _dot(p_band, dos, ((0,), (0,)))
            meta_k[:, hd] += _dot(ds_meta, qs, ((0,), (0,)))
            meta_v[:, hd] += _dot(p_meta, dos, ((0,), (0,)))
            dqkv_buf[:, kc] = dkcat[CHUNK:, :] + carry_k[:, hd]
            dqkv_buf[:, vc] = dvcat[CHUNK:, :] + carry_v[:, hd]
            carry_k[:, hd] = dkcat[:CHUNK, :]
            carry_v[:, hd] = dvcat[:CHUNK, :]

        @pl.when(n == 0)
        def _():
            dqkv_buf[META_PAD:, k_off:k_off + KV_W] += meta_k[...]
            dqkv_buf[META_PAD:, v_off:v_off + KV_W] += meta_v[...]

        dqkv_ref[...] = dqkv_buf[...].astype(BF16)

    return pl.pallas_call(
        body, grid=(nt,),
        in_specs=qkv_specs + [sink_spec, bias_spec, o_spec, o_spec, pl.BlockSpec(memory_space=pl.ANY)],
        out_specs=[qkv_specs[1], _full((KV_HEADS, 8, 128))],
        out_shape=[_sds(dproj.shape, dproj.dtype), _sds((KV_HEADS, 8, 128), F32)],
        scratch_shapes=[pltpu.VMEM((CHUNK, KV_W), F32), pltpu.VMEM((CHUNK, KV_W), F32),
                        pltpu.VMEM((N_META, KV_W), F32), pltpu.VMEM((N_META, KV_W), F32),
                        pltpu.VMEM((CHUNK, QKV_W), F32)],
        input_output_aliases={7: 0},
        name="swa_bwd", compiler_params=_params(1))(proj, proj, proj, sink_rows, bias, out, dout, dproj)


def _pack_w_in_t(w_in_t):
    w_dt = w_in_t[CUT_DT:CUT_Q].reshape(SSM_GROUPS, HEADS_PER_GROUP, D_MODEL)
    w_dt = jnp.pad(w_dt, ((0, 0), (0, 128 - HEADS_PER_GROUP), (0, 0))).reshape(SSM_GROUPS * 128, D_MODEL)
    return jnp.concatenate([w_in_t[CUT_Z:CUT_XBC], w_in_t[CUT_G:], w_dt, w_in_t[CUT_Q:CUT_G], w_in_t[CUT_XBC:CUT_DT]], axis=0)


def _unpack_w_in_t(wp_t):
    w_dt = wp_t[OFF_DT:OFF_Q].reshape(SSM_GROUPS, 128, D_MODEL)[:, :HEADS_PER_GROUP].reshape(SSM_HEADS, D_MODEL)
    return jnp.concatenate([wp_t[OFF_Z:OFF_GATE], wp_t[OFF_XBC:], w_dt, wp_t[OFF_Q:OFF_XBC], wp_t[OFF_GATE:OFF_DT]], axis=0)


def _head_lanes(v):
    return jnp.pad(v.reshape(1, SSM_HEADS), ((0, 0), (0, 128 - SSM_HEADS)))


def _local_step(x, target, wt, late_weights=None, on_grad=None, started=None):
    seq = x.shape[0]
    grads = {}

    def emit(name, g):
        grads[name] = g
        return None if on_grad is None else on_grad(name, g)
    meta = wt["meta_tokens"]
    wp_t = _pack_w_in_t(wt["w_in_t"])
    dtb = _head_lanes(wt["ssm_dt_bias"].reshape(-1))
    alog = _head_lanes(wt["ssm_a_log"].reshape(-1))
    dskip_x = jnp.repeat(wt["ssm_d_skip"].reshape(-1), HEAD_P).reshape(SSM_GROUPS, 1, GROUP_W)
    sink_rows = jnp.repeat(wt["attn_sinks"].reshape(KV_HEADS, ATTN_GROUP), CHUNK, axis=1).reshape(KV_HEADS, ATTN_GROUP * CHUNK, 1)

    hn = _prenorm(x, meta, wt["norm_pre_mix"])
    proj = _matmul(hn, wp_t, tb=True, name="in_proj", after=started)
    xc, xact = _ssm_conv_fwd(proj, wt["ssm_conv_w"], wt["ssm_conv_b"])
    y, hst = _ssd_fwd(xact, proj, dtb, alog, dskip_x)
    yn = _ssm_post(y, proj, wt["ssm_norm"])
    if late_weights is not None:
        wt = {**wt, **late_weights(yn)}
    y_ssm = _matmul(yn, wt["w_ssm_out"], name="ssm_out")
    bias = _swa_bias()
    attn = _swa_fwd(proj, sink_rows, bias)
    y_attn = _matmul(attn, wt["w_attn_out"], name="attn_out")
    mixed = _mix_fwd(proj, y_ssm, y_attn)
    mix = _matmul(mixed, wt["w_mix_out"], name="mix_out")
    h1, hn2 = _postmix(x, meta, mix, wt["norm_post_mix"], wt["norm_pre_ffn"])
    up = _matmul(hn2, wt["w_ffn_up_t"], tb=True, out_dtype=BF16, name="ffn_up")
    u, act = _ffn_act(up, wt["ffn_conv_w"], wt["ffn_conv_b"])
    f = _matmul(act, wt["w_ffn_down"], name="ffn_down")
    df, dy, g_norm_post_ffn, loss_row = _final(h1, f, target, wt["norm_post_ffn"])

    grads["norm_post_ffn"] = g_norm_post_ffn
    sent = emit("w_ffn_down", _matmul(act, df, ta=True, out_dtype=BF16, name="dw_ffn_down"))
    dact = _matmul(df, wt["w_ffn_down"], tb=True, out_dtype=BF16, name="d_act", after=sent)
    dup, grads["ffn_conv_w"], grads["ffn_conv_b"] = _ffn_act_bwd(u, up, dact, wt["ffn_conv_w"])
    sent = emit("w_ffn_up_t", _matmul(dup, hn2, ta=True, out_dtype=BF16, name="dw_ffn_up"))
    dhn2 = _matmul(dup, wt["w_ffn_up_t"], name="d_hn2", after=sent)
    dmix, dh, grads["norm_pre_ffn"], grads["norm_post_mix"] = _postmix_bwd(h1, dhn2, dy, mix, wt["norm_pre_ffn"], wt["norm_post_mix"])
    sent = emit("w_mix_out", _matmul(mixed, dmix, ta=True, out_dtype=BF16, name="dw_mix_out"))
    dmixed = _matmul(dmix, wt["w_mix_out"], tb=True, name="d_mixed", after=sent)
    dy_ssm, dy_attn, dproj = _mix_bwd(dmixed, proj, y_ssm, y_attn, lax.empty(proj.shape, BF16))
    sent = emit("w_ssm_out", _matmul(yn, dy_ssm, ta=True, out_dtype=BF16, name="dw_ssm_out"))
    dyn = _matmul(dy_ssm, wt["w_ssm_out"], tb=True, out_dtype=BF16, name="d_yn", after=sent)
    sent = emit("w_attn_out", _matmul(attn, dy_attn, ta=True, out_dtype=BF16, name="dw_attn_out"))
    dattn = _matmul(dy_attn, wt["w_attn_out"], tb=True, name="d_attn", after=sent)
    dy_ssd, dproj, grads["ssm_norm"] = _ssm_post_bwd(y, proj, dyn, wt["ssm_norm"], dproj)
    dxs, dbm, dcm, dproj, dalog, ddtb, dd_x = _ssd_bwd(xact, proj, dtb, alog, dskip_x, dy_ssd, hst, dproj)
    grads["ssm_a_log"] = dalog[:, :SSM_HEADS]
    grads["ssm_dt_bias"] = ddtb[:, :SSM_HEADS]
    grads["ssm_d_skip"] = dd_x.reshape(SSM_HEADS, HEAD_P).sum(axis=1).reshape(1, SSM_HEADS)
    dproj, grads["ssm_conv_w"], grads["ssm_conv_b"] = _ssm_conv_bwd(xc, proj, dxs, dbm, dcm, wt["ssm_conv_w"], dproj)
    dproj, dsink = _swa_bwd(proj, sink_rows, bias, attn, dattn, dproj)
    grads["attn_sinks"] = dsink[:, :ATTN_GROUP, 0].reshape(1, ATTN_HEADS)
    sent = emit("w_in_t", _unpack_w_in_t(_matmul(dproj, hn, ta=True, out_dtype=BF16, name="dw_in")))
    dhn = _matmul(dproj, wp_t, name="d_hn", after=sent)
    grad_x, grads["meta_tokens"], grads["norm_pre_mix"] = _prenorm_bwd(x, meta, dhn, dh, wt["norm_pre_mix"])
    return loss_row[0, 0], grad_x, grads


def _all_gather(shards):
    n = len(shards)

    def body(*refs):
        ins, outs = refs[:n], refs[n:2 * n]
        send_sems, recv_sems, local_sems = refs[2 * n:]
        x, y, c = lax.axis_index("x"), lax.axis_index("y"), lax.axis_index("c")
        me, sibling = (x, y, c), (x, y, 1 - c)
        x_nbr, y_nbr, diag = (1 - x, y), (x, 1 - y), (1 - x, 1 - y)
        relayed = (x ^ (1 - c), y ^ c)
        relay_to = (x ^ c, y ^ (1 - c))

        def slot(a, dev):
            return outs[a].at[4 * dev[0] + 2 * dev[1] + dev[2]]

        def copy(k, a, block, to, src=None):
            return pltpu.make_async_remote_copy(
                src_ref=slot(a, block) if src is None else src, dst_ref=slot(a, block),
                send_sem=send_sems.at[k, a], recv_sem=recv_sems.at[k, a],
                device_id=to, device_id_type=pl.DeviceIdType.MESH)

        mine = [pltpu.make_async_copy(ins[a], slot(a, me), local_sems.at[a]) for a in range(n)]
        for cp in mine:
            cp.start()
        first = [copy(0, a, me, sibling, src=ins[a]) for a in range(n)]
        first += [copy(1, a, me, (*x_nbr, c), src=ins[a]) for a in range(n)]
        first += [copy(2, a, me, (*y_nbr, c), src=ins[a]) for a in range(n)]
        for cp in first:
            cp.start()
        passed = []

        def pass_on(k, block, to):
            for a in range(n):
                cp = copy(k, a, block, to)
                cp.start()
                passed.append(cp)

        for j, chip in enumerate((x_nbr, y_nbr)):
            for a in range(n):
                copy(1 + j, a, (*chip, c), me).wait_recv()
            pass_on(4 + j, (*chip, c), sibling)
        pass_on(3, (*relayed, c), (*relay_to, c))
        for a in range(n):
            copy(3, a, (*diag, c), me).wait_recv()
        pass_on(6, (*diag, c), sibling)
        for a in range(n):
            copy(0, a, sibling, me).wait_recv()
        for j, chip in enumerate((x_nbr, y_nbr, diag)):
            for a in range(n):
                copy(4 + j, a, (*chip, 1 - c), me).wait_recv()
        for cp in first + passed:
            cp.wait_send()
        for cp in mine:
            cp.wait()

    hbm = pl.BlockSpec(memory_space=pl.ANY)
    return pl.pallas_call(
        body, in_specs=[hbm] * n, out_specs=[hbm] * n,
        out_shape=[_sds((N_DEV,) + s.shape, s.dtype) for s in shards],
        scratch_shapes=[pltpu.SemaphoreType.DMA((7, n)), pltpu.SemaphoreType.DMA((7, n)), pltpu.SemaphoreType.DMA((n,))],
        name="gather_weights")(*shards)


def _peer_table():
    x, y, c = lax.axis_index("x"), lax.axis_index("y"), lax.axis_index("c")
    peers = []
    for k in range(N_DEV - 1):
        bits = k + 1
        p = (x ^ ((bits >> 2) & 1), y ^ ((bits >> 1) & 1), c ^ (bits & 1))
        peers.append((k, p, 4 * p[0] + 2 * p[1] + p[2]))
    return 4 * x + 2 * y + c, peers


_HBM = pl.BlockSpec(memory_space=pltpu.HBM)
_SEM = pl.BlockSpec(memory_space=pltpu.SEMAPHORE)
_EFFECT = pltpu.SideEffectType.DATAFLOW_SIDE_EFFECTING


def _push_copy(src, land, send_sems, recv_sems, a, k, p, src_slot, dst_slot):
    sem = a * (N_DEV - 1) + k
    return pltpu.make_async_remote_copy(
        src_ref=src[a] if src_slot is None else src[a].at[src_slot], dst_ref=land[a].at[dst_slot],
        send_sem=send_sems.at[sem], recv_sem=recv_sems.at[sem], device_id=p, device_id_type=pl.DeviceIdType.MESH)


def _push_start(srcs, scatter, name):
    n = len(srcs)
    lands = [lax.empty(s.shape if scatter else (N_DEV,) + s.shape, s.dtype) for s in srcs]

    def body(*refs):
        src, land = refs[:n], refs[n:2 * n]
        send_sems, recv_sems, token = refs[2 * n], refs[2 * n + 1], refs[-1]
        my_id, peers = _peer_table()
        for a in range(n):
            for k, p, p_id in peers:
                _push_copy(src, land, send_sems, recv_sems, a, k, p, p_id if scatter else None, my_id).start()
        token[...] = jnp.zeros_like(token)

    sems = pltpu.SemaphoreType.DMA(((N_DEV - 1) * n,))
    res = pl.pallas_call(
        body, name=name,
        out_shape=(sems, sems, *[pltpu.HBM(a.shape, a.dtype) for a in srcs + lands], _sds((8, 128), F32)),
        in_specs=[_HBM] * (2 * n), out_specs=(_SEM, _SEM, *[_HBM] * (2 * n), pl.BlockSpec(memory_space=pltpu.VMEM)),
        input_output_aliases={i: 2 + i for i in range(2 * n)},
        compiler_params=pltpu.CompilerParams(has_side_effects=_EFFECT),
    )(*[pltpu.with_memory_space_constraint(a, pltpu.HBM) for a in srcs + lands])
    return dict(send=res[0], recv=res[1], src=list(res[2:2 + n]), land=list(res[2 + n:2 + 2 * n]), token=res[-1],
                scatter=scatter)


def _push_wait(handle, after, name):
    n = len(handle["src"])
    scatter = handle["scatter"]

    def body(*refs):
        src, land = refs[:n], refs[n:2 * n]
        send_sems, recv_sems = refs[2 * n], refs[2 * n + 1]
        _, peers = _peer_table()
        for a in range(n):
            for k, p, p_id in peers:
                cp = _push_copy(src, land, send_sems, recv_sems, a, k, p, p_id if scatter else None, p_id)
                cp.wait_send()
                cp.wait_recv()

    arrays = handle["src"] + handle["land"]
    res = pl.pallas_call(
        body, name=name, out_shape=tuple(pltpu.HBM(a.shape, a.dtype) for a in arrays),
        in_specs=[_HBM] * (2 * n) + [_SEM, _SEM, pl.BlockSpec(memory_space=pl.ANY)], out_specs=tuple([_HBM] * (2 * n)),
        input_output_aliases={i: i for i in range(2 * n)},
        compiler_params=pltpu.CompilerParams(has_side_effects=_EFFECT),
    )(*arrays, handle["send"], handle["recv"], after)
    return list(res[:n]), list(res[n:])


def _slot_sum(p_ref, own_ref):
    if own_ref is not None:
        my_id = 4 * lax.axis_index("x") + 2 * lax.axis_index("y") + lax.axis_index("c")
        mine = own_ref[...].astype(F32)
    g = None
    for s in range(p_ref.shape[0]):
        term = p_ref[s].astype(F32)
        if own_ref is not None:
            term = jnp.where(my_id == s, mine, term)
        g = term if g is None else g + term
    return g


def _to_bf16(arrays):
    n = len(arrays)

    def body(*refs):
        for i in range(n):
            refs[n + i][...] = refs[i][...].astype(BF16)

    return pl.pallas_call(body, out_shape=[_sds(a.shape, BF16) for a in arrays], name="weights_to_bf16",
                          compiler_params=pltpu.CompilerParams(vmem_limit_bytes=VMEM_LIMIT))(*arrays)


def _adamw(parts, own, w, m, v, name):
    unit_rows = w.ndim == 3
    rows, cols = w.shape[0], w.shape[-1]
    if rows % 16 == 0:
        tr, tc = _pick(rows, (256, 128, 176, 64, 32, 16)), cols
    else:
        tr, tc = rows, _pick(cols, (256, 128))

    def body(*refs):
        if own is None:
            p_ref, w_ref, m_ref, v_ref, g_ref, d_ref, nm_ref, nv_ref = refs
            own_ref = None
        else:
            p_ref, own_ref, w_ref, m_ref, v_ref, g_ref, d_ref, nm_ref, nv_ref = refs
        g = _slot_sum(p_ref, own_ref)
        if unit_rows:
            g = g.reshape(tr, 1, tc)
        m_new = ADAM_B1 * m_ref[...] + (1.0 - ADAM_B1) * g
        v_new = ADAM_B2 * v_ref[...] + (1.0 - ADAM_B2) * (g * g)
        m_hat = m_new / (1.0 - ADAM_B1 ** ADAM_STEP)
        v_hat = v_new / (1.0 - ADAM_B2 ** ADAM_STEP)
        g_ref[...] = g
        d_ref[...] = -ADAM_LR * (m_hat / (jnp.sqrt(v_hat) + ADAM_EPS) + ADAM_WD * w_ref[...])
        nm_ref[...] = m_new
        nv_ref[...] = v_new

    by_rows = tc == cols
    spec = pl.BlockSpec((tr, tc), (lambda i: (i, 0)) if by_rows else (lambda i: (0, i)))
    state_spec = spec if not unit_rows else pl.BlockSpec((tr, 1, tc), (lambda i: (i, 0, 0)) if by_rows else (lambda i: (0, 0, i)))
    parts_spec = pl.BlockSpec((parts.shape[0], tr, tc), (lambda i: (0, i, 0)) if by_rows else (lambda i: (0, 0, i)))
    operands = (parts, w, m, v) if own is None else (parts, own, w, m, v)
    return pl.pallas_call(
        body, grid=(rows // tr if by_rows else cols // tc,),
        in_specs=[parts_spec] + ([] if own is None else [spec]) + [state_spec] * 3,
        out_specs=[state_spec] * 4, out_shape=[_sds(w.shape, F32)] * 4,
        name=name, compiler_params=_params(1))(*operands)


SMALL_REPLICATED = (("norm_pre_mix", 1024), ("ssm_conv_b", 3072), ("ssm_dt_bias", 32), ("ssm_a_log", 32),
                    ("ssm_d_skip", 32), ("ssm_norm", 2048), ("attn_sinks", 16), ("norm_post_mix", 1024),
                    ("norm_pre_ffn", 1024), ("ffn_conv_b", 5632), ("norm_post_ffn", 1024))
SMALL_SHARDED = (("meta_tokens", (N_META, D_MODEL // N_DEV)), ("ssm_conv_w", (SSM_CONV, CONV_DIM // N_DEV)),
                 ("ffn_conv_w", (FFN_CONV, 2 * FFN_DIM // N_DEV)))
BIG = (("w_in", (D_MODEL, N_IN // N_DEV), 1), ("w_ssm_out", (D_INNER // N_DEV, D_MODEL), 0),
       ("w_attn_out", (D_MODEL // N_DEV, D_MODEL), 0), ("w_mix_out", (D_MODEL // N_DEV, D_MODEL), 0),
       ("w_ffn_up", (D_MODEL, 2 * FFN_DIM // N_DEV), 1), ("w_ffn_down", (FFN_DIM // N_DEV, D_MODEL), 0))


def _rows_of(size):
    return -(-size // 128)


def _as_rows(flat):
    size = flat.shape[-1]
    rows = _rows_of(size)
    flat = jnp.pad(flat, [(0, 0)] * (flat.ndim - 1) + [(0, rows * 128 - size)])
    return flat.reshape(flat.shape[:-1] + (rows, 128))


def _pack_small(rep, sharded):
    pieces = [_as_rows(rep[name].reshape(-1)) for name, _ in SMALL_REPLICATED]
    pieces += [_as_rows(sharded[name].reshape(-1)) for name, _ in SMALL_SHARDED]
    packed = jnp.concatenate(pieces, axis=0)
    return jnp.pad(packed, ((0, -packed.shape[0] % 8), (0, 0)))


def _unpack_small(packed):
    out, row = {}, 0
    for name, size in SMALL_REPLICATED:
        out[name] = packed[row:row + _rows_of(size)].reshape(-1)[:size].reshape(1, size)
        row += _rows_of(size)
    for name, (r, c) in SMALL_SHARDED:
        out[name] = packed[row:row + _rows_of(r * c)].reshape(-1)[:r * c].reshape(r, c)
        row += _rows_of(r * c)
    return out


def _shard_major(g, shape, axis):
    r, c = shape
    if axis == 0:
        return g.reshape(N_DEV, r, c)
    return g.reshape(r, N_DEV, c).transpose(1, 0, 2)


def kernel(x, meta_tokens, norm_pre_mix, w_in, ssm_conv_w, ssm_conv_b, ssm_dt_bias, ssm_a_log, ssm_d_skip, ssm_norm, w_ssm_out, attn_sinks, w_attn_out, w_mix_out, norm_post_mix, norm_pre_ffn, w_ffn_up, ffn_conv_w, ffn_conv_b, w_ffn_down, norm_post_ffn, loss_target, m_meta_tokens, m_norm_pre_mix, m_w_in, m_ssm_conv_w, m_ssm_conv_b, m_ssm_dt_bias, m_ssm_a_log, m_ssm_d_skip, m_ssm_norm, m_w_ssm_out, m_attn_sinks, m_w_attn_out, m_w_mix_out, m_norm_post_mix, m_norm_pre_ffn, m_w_ffn_up, m_ffn_conv_w, m_ffn_conv_b, m_w_ffn_down, m_norm_post_ffn, v_meta_tokens, v_norm_pre_mix, v_w_in, v_ssm_conv_w, v_ssm_conv_b, v_ssm_dt_bias, v_ssm_a_log, v_ssm_d_skip, v_ssm_norm, v_w_ssm_out, v_attn_sinks, v_w_attn_out, v_w_mix_out, v_norm_post_mix, v_norm_pre_ffn, v_w_ffn_up, v_ffn_conv_w, v_ffn_conv_b, v_w_ffn_down, v_norm_post_ffn):
    names = ("meta_tokens", "norm_pre_mix", "w_in", "ssm_conv_w", "ssm_conv_b", "ssm_dt_bias", "ssm_a_log", "ssm_d_skip",
             "ssm_norm", "w_ssm_out", "attn_sinks", "w_attn_out", "w_mix_out", "norm_post_mix", "norm_pre_ffn", "w_ffn_up",
             "ffn_conv_w", "ffn_conv_b", "w_ffn_down", "norm_post_ffn")
    w_loc = dict(zip(names, (meta_tokens, norm_pre_mix, w_in, ssm_conv_w, ssm_conv_b, ssm_dt_bias, ssm_a_log, ssm_d_skip,
                             ssm_norm, w_ssm_out, attn_sinks, w_attn_out, w_mix_out, norm_post_mix, norm_pre_ffn, w_ffn_up,
                             ffn_conv_w, ffn_conv_b, w_ffn_down, norm_post_ffn)))
    m_loc = dict(zip(names, (m_meta_tokens, m_norm_pre_mix, m_w_in, m_ssm_conv_w, m_ssm_conv_b, m_ssm_dt_bias, m_ssm_a_log,
                             m_ssm_d_skip, m_ssm_norm, m_w_ssm_out, m_attn_sinks, m_w_attn_out, m_w_mix_out, m_norm_post_mix,
                             m_norm_pre_ffn, m_w_ffn_up, m_ffn_conv_w, m_ffn_conv_b, m_w_ffn_down, m_norm_post_ffn)))
    v_loc = dict(zip(names, (v_meta_tokens, v_norm_pre_mix, v_w_in, v_ssm_conv_w, v_ssm_conv_b, v_ssm_dt_bias, v_ssm_a_log,
                             v_ssm_d_skip, v_ssm_norm, v_w_ssm_out, v_attn_sinks, v_w_attn_out, v_w_mix_out, v_norm_post_mix,
                             v_norm_pre_ffn, v_w_ffn_up, v_ffn_conv_w, v_ffn_conv_b, v_w_ffn_down, v_norm_post_ffn)))

    def local2d(d, name):
        a = d[name]
        return a if name == "meta_tokens" else a.reshape(a.shape[1:])

    def turned2d(d, name):
        a = jnp.swapaxes(d[name], 1, 2)
        return a.reshape(a.shape[1:])

    my_id = 4 * lax.axis_index("x") + 2 * lax.axis_index("y") + lax.axis_index("c")
    big = {name: (shape, axis) for name, shape, axis in BIG}

    def whole(name, g):
        return g.reshape(N_DEV * g.shape[1], g.shape[2])

    def key(name):
        return name + "_t" if big[name][1] == 1 else name

    by_rows = [name for name, _, axis in BIG if axis == 0]
    send_bf16 = dict(zip(by_rows, _to_bf16([local2d(w_loc, name) for name in by_rows])))
    for name, _, axis in BIG:
        if axis == 1:
            send_bf16[name] = turned2d(w_loc, name).astype(BF16)
    small_shard_pack = jnp.concatenate([_as_rows(local2d(w_loc, name).reshape(-1)) for name, _ in SMALL_SHARDED], axis=0)
    small_shard_pack = jnp.pad(small_shard_pack, ((0, -small_shard_pack.shape[0] % 8), (0, 0)))
    first = _all_gather([send_bf16["w_in"], small_shard_pack])
    rest_names = [name for name, _, _ in BIG if name != "w_in"]
    rest = [send_bf16[name] for name in rest_names]
    rest, first = lax.optimization_barrier((rest, first))
    rest_handle = _push_start(rest, False, "gather_rest_start")
    wt = {"w_in_t": whole("w_in", first[0])}
    row = 0
    for name, (r, c) in SMALL_SHARDED:
        blocks = first[1][:, row:row + _rows_of(r * c)].reshape(N_DEV, -1)[:, :r * c].reshape(N_DEV, r, c)
        wt[name] = blocks.transpose(1, 0, 2).reshape(r, N_DEV * c)
        row += _rows_of(r * c)
    for name, size in SMALL_REPLICATED:
        wt[name] = w_loc[name].reshape(1, size)

    def late_weights(after):
        own, landed = _push_wait(rest_handle, after, "gather_rest_wait")
        out = {}
        for name, mine, land in zip(rest_names, own, landed):
            out[key(name)] = whole(name, lax.dynamic_update_index_in_dim(land, mine, my_id, 0))
        return out

    sent = {}

    def on_grad(known_as, g):
        name = known_as.removesuffix("_t")
        by_owner = g.reshape(N_DEV, g.shape[0] // N_DEV, g.shape[1])
        sent[name] = _push_start([by_owner], True, "send_" + name)
        return sent[name]["token"]

    loss_part, grad_x, grads = _local_step(x[0], loss_target[0], wt, late_weights, on_grad, rest_handle["token"])

    small_parts = []
    for name, (r, c) in SMALL_SHARDED:
        small_parts.append(_as_rows(_shard_major(grads[name], (r, c), 1).reshape(N_DEV, r * c)))
    rep_rows = jnp.concatenate([_as_rows(grads[name].reshape(-1)) for name, _ in SMALL_REPLICATED], axis=0)
    small_send = jnp.concatenate([jnp.broadcast_to(rep_rows[None], (N_DEV,) + rep_rows.shape)] + small_parts, axis=1)
    small_send = jnp.pad(small_send, ((0, 0), (0, -small_send.shape[1] % 8), (0, 0)))
    loss_tile = jnp.broadcast_to(jnp.pad(loss_part.reshape(1, 1, 1), ((0, 0), (0, 7), (0, 127))), (N_DEV, 8, 128))
    small_send = jnp.concatenate([small_send, loss_tile], axis=1)
    small_handle = _push_start([small_send], True, "send_small")

    def small_pack(d):
        pack = _pack_small({name: d[name] for name, _ in SMALL_REPLICATED}, {name: local2d(d, name) for name, _ in SMALL_SHARDED})
        return jnp.pad(pack, ((0, 8), (0, 0)))

    def arrived(handle, after, name):
        src, landed = _push_wait(handle, after, "arrived_" + name)
        return landed[0], lax.dynamic_index_in_dim(src[0], my_id, 0, keepdims=False)

    grad_w, delta_w, new_m, new_v = {}, {}, {}, {}
    outs = None
    after = small_handle["token"]
    for name, handle in sent.items():
        if name == "w_in":
            parts, own = arrived(small_handle, after, "small")
            outs = _adamw(parts, own, small_pack(w_loc), small_pack(m_loc), small_pack(v_loc), "adamw_small")
            after = outs[0]
        parts, own = arrived(handle, after, name)
        turned = big[name][1] == 1
        unit_rows = turned and big[name][0][1] % 8 != 0
        if unit_rows:
            state = [jnp.transpose(d[name], (2, 0, 1)) for d in (w_loc, m_loc, v_loc)]
        else:
            state = [turned2d(d, name) if turned else local2d(d, name) for d in (w_loc, m_loc, v_loc)]
        results = _adamw(parts, own, *state, "adamw_" + name)
        after = results[0]
        full = (1,) + big[name][0]
        for dst, a in zip((grad_w, delta_w, new_m, new_v), results):
            if unit_rows:
                dst[name] = jnp.transpose(a, (1, 2, 0))
            else:
                dst[name] = jnp.swapaxes(a[None], 1, 2) if turned else a.reshape(full)
    for dst, packed in zip((grad_w, delta_w, new_m, new_v), outs):
        for name, a in _unpack_small(packed).items():
            dst[name] = a.reshape(w_loc[name].shape)
    loss = outs[0][-8, 0]

    return (loss, grad_x[None], *[grad_w[n] for n in names], *[delta_w[n] for n in names],
            *[new_m[n] for n in names], *[new_v[n] for n in names])
```

```python
import jax
import jax.numpy as jnp
from jax import lax
from jax.experimental import pallas as pl
from jax.experimental.pallas import tpu as pltpu

F32 = jnp.float32
BF16 = jnp.bfloat16

D_MODEL = 1024
N_META = 16
CHUNK = 128
META_PAD = CHUNK - N_META
D_INNER = 2048
HEAD_P = 64
SSM_HEADS = 32
SSM_GROUPS = 4
HEADS_PER_GROUP = SSM_HEADS // SSM_GROUPS
GROUP_W = HEADS_PER_GROUP * HEAD_P
D_STATE = 128
SSM_CONV = 4
CONV_DIM = D_INNER + 2 * SSM_GROUPS * D_STATE
ATTN_HEADS = 16
KV_HEADS = 4
ATTN_GROUP = ATTN_HEADS // KV_HEADS
DH = 64
KV_W = KV_HEADS * DH
FFN_DIM = 2816
FFN_CONV = 3
EPS = 1e-6
NEG = -1e30
N_DEV = 8
AXES = ("x", "y", "c")

OFF_Z, OFF_GATE, OFF_DT, OFF_Q, OFF_K, OFF_V, OFF_XBC = 0, 2048, 4096, 4608, 5632, 5888, 6144
N_INP = OFF_XBC + CONV_DIM
QKV_W = OFF_XBC - OFF_Q
CUT_Z, CUT_XBC, CUT_DT, CUT_Q, CUT_K, CUT_V, CUT_G = 0, 2048, 5120, 5152, 6176, 6432, 6688
N_IN = 8736

ADAM_LR, ADAM_B1, ADAM_B2, ADAM_EPS, ADAM_WD, ADAM_STEP = 0.001, 0.9, 0.999, 1e-08, 0.01, 10

VMEM_LIMIT = 56 * 1024 * 1024


def _params(n_grid):
    return pltpu.CompilerParams(dimension_semantics=("arbitrary",) * n_grid, vmem_limit_bytes=VMEM_LIMIT)


def _sds(shape, dtype):
    return jax.ShapeDtypeStruct(shape, dtype)


def _pick(n, prefs):
    for c in prefs:
        if n % c == 0:
            return c
    raise ValueError(f"no tile of {prefs} divides {n}")


def _row(tr, width, cb=0):
    return pl.BlockSpec((tr, width), lambda i: (i, cb))


def _row_rev(tr, width, nt, cb=0):
    return pl.BlockSpec((tr, width), lambda i: (nt - 1 - i, cb))


def _full(shape):
    return pl.BlockSpec(shape, lambda *_: (0,) * len(shape))


def _sigmoid(x):
    return 1.0 / (1.0 + jnp.exp(-x))


def _softplus(x):
    return jnp.maximum(x, 0.0) + jnp.log(1.0 + jnp.exp(-jnp.abs(x)))


def _rms(x):
    return lax.rsqrt(jnp.mean(x * x, axis=-1, keepdims=True) + EPS)


def _rms_bwd(x, r, w, dy):
    xh = x * r
    g = dy * w
    dx = r * (g - xh * jnp.mean(g * xh, axis=-1, keepdims=True))
    return dx, jnp.sum(dy * xh, axis=0, keepdims=True)


def _row_ids(shape, tile_index, tr):
    return tile_index * tr + lax.broadcasted_iota(jnp.int32, shape, 0)


HALO = 8
STRIP = 256
STRIP_BWD = 128


def _causal_taps(x, halo, first_step, taps):
    n = x.shape[0]

    @pl.when(first_step)
    def _():
        halo[...] = jnp.zeros_like(halo)

    before = halo[...]
    row = lax.broadcasted_iota(jnp.int32, before.shape, 0)
    shifted = [x]
    for s in range(1, taps):
        rolled = pltpu.roll(x, s, 0)
        head = jnp.where(row < s, pltpu.roll(before, s, 0), rolled[0:HALO, :])
        shifted.append(jnp.concatenate([head, rolled[HALO:, :]], axis=0))
    halo[...] = x[n - HALO:, :]
    return shifted


def _anticausal_taps(x, halo, first_step, taps):
    n = x.shape[0]

    @pl.when(first_step)
    def _():
        halo[...] = jnp.zeros_like(halo)

    after = halo[...]
    row = lax.broadcasted_iota(jnp.int32, after.shape, 0)
    shifted = [x]
    for s in range(1, taps):
        rolled = pltpu.roll(x, n - s, 0)
        tail = jnp.where(row >= HALO - s, pltpu.roll(after, HALO - s, 0), rolled[n - HALO:, :])
        shifted.append(jnp.concatenate([rolled[:n - HALO, :], tail], axis=0))
    halo[...] = x[0:HALO, :]
    return shifted


def _matmul(a, b, *, ta=False, tb=False, out_dtype=F32, name, after=None):
    if ta:
        k_dim, m_dim = a.shape
    else:
        m_dim, k_dim = a.shape
    n_dim = b.shape[0] if tb else b.shape[1]
    tm = _pick(m_dim, (1408, 1024, 768, 512, 384, 256, 128))
    tn = _pick(n_dim, (1024, 1408, 768, 512, 384, 256, 128))
    if ta:
        tk = _pick(k_dim, (1408, 1024, 768, 512, 384, 256, 128))
    else:
        tk = k_dim if k_dim <= 3072 else _pick(k_dim, (3072, 2816, 2048, 1024))
    nk = k_dim // tk
    dims = (((0 if ta else 1,), (1 if tb else 0,)), ((), ()))

    use_acc = nk > 1 and out_dtype != F32

    def body(a_ref, b_ref, *rest):
        o_ref = rest[-2] if use_acc else rest[-1]
        acc_ref = rest[-1] if use_acc else o_ref
        def product():
            return lax.dot_general(a_ref[...].astype(BF16), b_ref[...].astype(BF16), dims, preferred_element_type=F32)

        if nk == 1:
            o_ref[...] = product().astype(o_ref.dtype)
        else:
            k = pl.program_id(2)

            @pl.when(k == 0)
            def _():
                acc_ref[...] = jnp.zeros_like(acc_ref)

            acc_ref[...] += product()

            if use_acc:
                @pl.when(k == nk - 1)
                def _():
                    o_ref[...] = acc_ref[...].astype(o_ref.dtype)

    a_spec = pl.BlockSpec((tk, tm), lambda i, j, k: (k, i)) if ta else pl.BlockSpec((tm, tk), lambda i, j, k: (i, k))
    b_spec = pl.BlockSpec((tn, tk), lambda i, j, k: (j, k)) if tb else pl.BlockSpec((tk, tn), lambda i, j, k: (k, j))
    extra_specs, extra = ([], ()) if after is None else ([pl.BlockSpec(memory_space=pl.ANY)], (after,))
    return pl.pallas_call(
        body, grid=(m_dim // tm, n_dim // tn, nk), in_specs=[a_spec, b_spec] + extra_specs,
        out_specs=pl.BlockSpec((tm, tn), lambda i, j, k: (i, j)), out_shape=_sds((m_dim, n_dim), out_dtype),
        scratch_shapes=[pltpu.VMEM((tm, tn), F32)] if use_acc else [],
        name=name, compiler_params=_params(3))(a, b, *extra)


def _seq_rows(t_rows):
    return 384 if t_rows % 384 == 0 and t_rows >= 768 else CHUNK


def _token_rows(tr):
    if tr == CHUNK:
        return pl.BlockSpec((CHUNK, D_MODEL), lambda i: (jnp.maximum(i - 1, 0), 0))
    return pl.BlockSpec((pl.Element(tr), pl.Element(D_MODEL)),
                        lambda i: (pl.multiple_of(jnp.maximum(i * tr - CHUNK, 0), CHUNK), 0))


def _under_tile(rows_ref, head, i):
    rows = rows_ref[...]
    tr = rows.shape[0]
    first = head if tr == CHUNK else jnp.concatenate([head, rows[0:tr - CHUNK, :]], axis=0)
    return jnp.where(i == 0, first, rows)


def _seq_specs(tr=CHUNK):
    return [_token_rows(tr), _full((N_META, D_MODEL))]


def _seq_tile(x_ref, meta_ref, i):
    return _under_tile(x_ref, jnp.concatenate([jnp.zeros((META_PAD, D_MODEL), F32), meta_ref[...]], axis=0), i)


def _prenorm(x, meta, w):
    t_rows = x.shape[0] + CHUNK
    tr = _seq_rows(t_rows)

    def body(x_ref, meta_ref, w_ref, o_ref):
        h = _seq_tile(x_ref, meta_ref, pl.program_id(0))
        o_ref[...] = (h * _rms(h) * w_ref[...]).astype(BF16)

    return pl.pallas_call(body, grid=(t_rows // tr,), in_specs=_seq_specs(tr) + [_full((1, D_MODEL))],
                          out_specs=_row(tr, D_MODEL), out_shape=_sds((t_rows, D_MODEL), BF16),
                          name="prenorm", compiler_params=_params(1))(x, meta, w)


def _ssm_conv_fwd(proj, conv_w, conv_b):
    t_rows = proj.shape[0]
    tr = CHUNK

    def body(x_ref, w_ref, b_ref, xc_ref, xa_ref, hist):
        first = pl.program_id(0) == 0
        for c in range(0, CONV_DIM, STRIP):
            cols = slice(c, c + STRIP)
            acc = b_ref[:, cols]
            for s, moved in enumerate(_causal_taps(x_ref[:, cols], hist.at[:, cols], first, SSM_CONV)):
                acc = acc + w_ref[SSM_CONV - 1 - s:SSM_CONV - s, cols] * moved
            xc_ref[:, cols] = acc
            xa_ref[:, cols] = acc * _sigmoid(acc)

    return pl.pallas_call(
        body, grid=(t_rows // tr,),
        in_specs=[_row(tr, CONV_DIM, OFF_XBC // CONV_DIM), _full((SSM_CONV, CONV_DIM)), _full((1, CONV_DIM))],
        out_specs=[_row(tr, CONV_DIM), _row(tr, CONV_DIM)],
        out_shape=[_sds((t_rows, CONV_DIM), F32), _sds((t_rows, CONV_DIM), F32)],
        scratch_shapes=[pltpu.VMEM((HALO, CONV_DIM), F32)],
        name="ssm_conv_fwd", compiler_params=_params(1))(proj, conv_w, conv_b)


def _ssm_post(y, proj, w):
    t_rows = y.shape[0]
    tr = _pick(t_rows, (384, 128))

    def body(y_ref, z_ref, w_ref, o_ref):
        z = z_ref[...].astype(F32)
        yz = y_ref[...] * z * _sigmoid(z)
        o_ref[...] = (yz * _rms(yz) * w_ref[...]).astype(BF16)

    return pl.pallas_call(body, grid=(t_rows // tr,),
                          in_specs=[_row(tr, D_INNER), _row(tr, D_INNER, OFF_Z // D_INNER), _full((1, D_INNER))],
                          out_specs=_row(tr, D_INNER), out_shape=_sds((t_rows, D_INNER), BF16),
                          name="ssm_post", compiler_params=_params(1))(y, proj, w)


def _mix_fwd(proj, y_ssm, y_attn):
    t_rows = y_ssm.shape[0]
    tr = _pick(t_rows, (384, 128))

    def body(g_ref, ys_ref, ya_ref, o_ref):
        g = _sigmoid(g_ref[...].astype(F32))
        o_ref[...] = (g[:, :D_MODEL] * ys_ref[...] + g[:, D_MODEL:] * ya_ref[...]).astype(BF16)

    return pl.pallas_call(body, grid=(t_rows // tr,),
                          in_specs=[_row(tr, 2 * D_MODEL, OFF_GATE // (2 * D_MODEL)), _row(tr, D_MODEL),
                                    _row(tr, D_MODEL)],
                          out_specs=_row(tr, D_MODEL), out_shape=_sds((t_rows, D_MODEL), BF16),
                          name="mix_fwd", compiler_params=_params(1))(proj, y_ssm, y_attn)


def _postmix(x, meta, mix, w_post, w_pre):
    t_rows = mix.shape[0]
    tr = _seq_rows(t_rows)

    def body(x_ref, meta_ref, m_ref, wp_ref, wf_ref, h1_ref, hn_ref):
        m = m_ref[...]
        h1 = _seq_tile(x_ref, meta_ref, pl.program_id(0)) + m * _rms(m) * wp_ref[...]
        h1 = jnp.where(_row_ids(h1.shape, pl.program_id(0), tr) >= META_PAD, h1, 0.0)
        h1_ref[...] = h1
        hn_ref[...] = (h1 * _rms(h1) * wf_ref[...]).astype(BF16)

    return pl.pallas_call(body, grid=(t_rows // tr,),
                          in_specs=_seq_specs(tr) + [_row(tr, D_MODEL), _full((1, D_MODEL)), _full((1, D_MODEL))],
                          out_specs=[_row(tr, D_MODEL), _row(tr, D_MODEL)],
                          out_shape=[_sds((t_rows, D_MODEL), F32), _sds((t_rows, D_MODEL), BF16)],
                          name="postmix", compiler_params=_params(1))(x, meta, mix, w_post, w_pre)


def _ffn_act(up, conv_w, conv_b):
    t_rows = up.shape[0]
    tr = CHUNK
    width = 2 * FFN_DIM

    def body(up_ref, w_ref, b_ref, u_ref, act_ref, hist):
        first = pl.program_id(0) == 0
        for c in range(0, FFN_DIM, STRIP):
            halves = []
            for base in (0, FFN_DIM):
                cols = slice(base + c, base + c + STRIP)
                u = b_ref[:, cols]
                for s, moved in enumerate(_causal_taps(up_ref[:, cols].astype(F32), hist.at[:, cols], first, FFN_CONV)):
                    u = u + w_ref[FFN_CONV - 1 - s:FFN_CONV - s, cols] * moved
                u_ref[:, cols] = u.astype(BF16)
                halves.append(u)
            a, g = halves
            act_ref[:, c:c + STRIP] = (a * _sigmoid(a) * g).astype(BF16)

    return pl.pallas_call(
        body, grid=(t_rows // tr,), in_specs=[_row(tr, width), _full((FFN_CONV, width)), _full((1, width))],
        out_specs=[_row(tr, width), _row(tr, FFN_DIM)],
        out_shape=[_sds((t_rows, width), BF16), _sds((t_rows, FFN_DIM), BF16)],
        scratch_shapes=[pltpu.VMEM((HALO, width), F32)],
        name="ffn_act", compiler_params=_params(1))(up, conv_w, conv_b)


def _final(h1, f, target, w):
    t_rows = h1.shape[0]
    tr = _seq_rows(t_rows)

    def body(h1_ref, f_ref, t_ref, w_ref, df_ref, dy_ref, dw_ref, loss_ref):
        i = pl.program_id(0)

        @pl.when(i == 0)
        def _():
            dw_ref[...] = jnp.zeros_like(dw_ref)
            loss_ref[...] = jnp.zeros_like(loss_ref)

        f_val = f_ref[...]
        r = _rms(f_val)
        wv = w_ref[...]
        h2 = h1_ref[...] + f_val * r * wv
        tgt = _under_tile(t_ref, jnp.zeros((CHUNK, D_MODEL), F32), i)
        diff = jnp.where(_row_ids(h2.shape, i, tr) >= CHUNK, h2 - tgt, 0.0)
        loss_ref[...] += 0.5 * jnp.sum(diff * diff) * (1.0 / D_MODEL)
        dy = diff * (1.0 / D_MODEL)
        dy_ref[...] = dy
        df, dw = _rms_bwd(f_val, r, wv, dy)
        df_ref[...] = df.astype(BF16)
        dw_ref[...] += dw

    return pl.pallas_call(
        body, grid=(t_rows // tr,),
        in_specs=[_row(tr, D_MODEL), _row(tr, D_MODEL), _token_rows(tr), _full((1, D_MODEL))],
        out_specs=[_row(tr, D_MODEL), _row(tr, D_MODEL), _full((1, D_MODEL)), _full((1, 128))],
        out_shape=[_sds((t_rows, D_MODEL), BF16), _sds((t_rows, D_MODEL), F32), _sds((1, D_MODEL), F32), _sds((1, 128), F32)],
        name="final", compiler_params=_params(1))(h1, f, target, w)


def _ffn_act_bwd(u, up, dact, conv_w):
    t_rows = u.shape[0]
    tr = CHUNK
    nt = t_rows // tr
    width = 2 * FFN_DIM

    def body(u_ref, up_ref, da_ref, w_ref, dup_ref, dw_ref, db_ref, ahead):
        @pl.when(pl.program_id(0) == 0)
        def _():
            dw_ref[...] = jnp.zeros_like(dw_ref)
            db_ref[...] = jnp.zeros_like(db_ref)

        first = pl.program_id(0) == 0
        for c in range(0, FFN_DIM, STRIP_BWD):
            ca, cg = slice(c, c + STRIP_BWD), slice(FFN_DIM + c, FFN_DIM + c + STRIP_BWD)
            a, g, d = u_ref[:, ca].astype(F32), u_ref[:, cg].astype(F32), da_ref[:, ca].astype(F32)
            s = _sigmoid(a)
            for cols, du in ((ca, d * g * s * (1.0 + a * (1.0 - s))), (cg, d * a * s)):
                x = up_ref[:, cols].astype(F32)
                dup = None
                for sh, moved in enumerate(_anticausal_taps(du, ahead.at[:, cols], first, FFN_CONV)):
                    k = FFN_CONV - 1 - sh
                    term = w_ref[k:k + 1, cols] * moved
                    dup = term if dup is None else dup + term
                    dw_ref[k:k + 1, cols] += jnp.sum(moved * x, axis=0, keepdims=True)
                db_ref[:, cols] += jnp.sum(du, axis=0, keepdims=True)
                dup_ref[:, cols] = dup.astype(BF16)

    return pl.pallas_call(
        body, grid=(nt,),
        in_specs=[_row_rev(tr, width, nt), _row_rev(tr, width, nt), _row_rev(tr, FFN_DIM, nt), _full((FFN_CONV, width))],
        out_specs=[_row_rev(tr, width, nt), _full((FFN_CONV, width)), _full((1, width))],
        out_shape=[_sds((t_rows, width), BF16), _sds((FFN_CONV, width), F32), _sds((1, width), F32)],
        scratch_shapes=[pltpu.VMEM((HALO, width), F32)],
        name="ffn_act_bwd", compiler_params=_params(1))(u, up, dact, conv_w)


def _postmix_bwd(h1, dhn2, dy, mix, w_pre, w_post):
    t_rows = h1.shape[0]
    tr = _pick(t_rows, (384, 128))

    def body(h1_ref, dhn_ref, dy_ref, m_ref, wf_ref, wp_ref, dmix_ref, dh_ref, dwf_ref, dwp_ref):
        @pl.when(pl.program_id(0) == 0)
        def _():
            dwf_ref[...] = jnp.zeros_like(dwf_ref)
            dwp_ref[...] = jnp.zeros_like(dwp_ref)

        h1v = h1_ref[...]
        dx, dwf = _rms_bwd(h1v, _rms(h1v), wf_ref[...], dhn_ref[...])
        dwf_ref[...] += dwf
        dh1 = dy_ref[...] + dx
        dh1 = jnp.where(_row_ids(dh1.shape, pl.program_id(0), tr) >= META_PAD, dh1, 0.0)
        dh_ref[...] = dh1
        m = m_ref[...]
        dmix, dwp = _rms_bwd(m, _rms(m), wp_ref[...], dh1)
        dwp_ref[...] += dwp
        dmix_ref[...] = dmix.astype(BF16)

    return pl.pallas_call(
        body, grid=(t_rows // tr,),
        in_specs=[_row(tr, D_MODEL) for _ in range(4)] + [_full((1, D_MODEL))] * 2,
        out_specs=[_row(tr, D_MODEL), _row(tr, D_MODEL), _full((1, D_MODEL)), _full((1, D_MODEL))],
        out_shape=[_sds((t_rows, D_MODEL), BF16), _sds((t_rows, D_MODEL), F32), _sds((1, D_MODEL), F32), _sds((1, D_MODEL), F32)],
        name="postmix_bwd", compiler_params=_params(1))(h1, dhn2, dy, mix, w_pre, w_post)


_ANY = pl.BlockSpec(memory_space=pl.ANY)


def _mix_bwd(dmixed, proj, y_ssm, y_attn, dproj):
    t_rows = dmixed.shape[0]
    tr = _pick(t_rows, (384, 128))

    def body(d_ref, g_ref, ys_ref, ya_ref, _, dys_ref, dya_ref, dg_ref):
        d = d_ref[...]
        g = _sigmoid(g_ref[...].astype(F32))
        g1, g2 = g[:, :D_MODEL], g[:, D_MODEL:]
        dys_ref[...] = (d * g1).astype(BF16)
        dya_ref[...] = (d * g2).astype(BF16)
        dg_ref[...] = jnp.concatenate([d * ys_ref[...] * g1 * (1.0 - g1), d * ya_ref[...] * g2 * (1.0 - g2)],
                                      axis=1).astype(BF16)

    return pl.pallas_call(
        body, grid=(t_rows // tr,),
        in_specs=[_row(tr, D_MODEL), _row(tr, 2 * D_MODEL, OFF_GATE // (2 * D_MODEL)), _row(tr, D_MODEL), _row(tr, D_MODEL),
                  _ANY],
        out_specs=[_row(tr, D_MODEL), _row(tr, D_MODEL), _row(tr, 2 * D_MODEL, OFF_GATE // (2 * D_MODEL))],
        out_shape=[_sds((t_rows, D_MODEL), BF16), _sds((t_rows, D_MODEL), BF16), _sds(dproj.shape, dproj.dtype)],
        input_output_aliases={4: 2},
        name="mix_bwd", compiler_params=_params(1))(dmixed, proj, y_ssm, y_attn, dproj)


def _ssm_post_bwd(y, proj, dyn, w, dproj):
    t_rows = y.shape[0]
    tr = CHUNK

    def body(y_ref, z_ref, d_ref, w_ref, _, dy_ref, dz_ref, dw_ref):
        @pl.when(pl.program_id(0) == 0)
        def _():
            dw_ref[...] = jnp.zeros_like(dw_ref)

        yv, z = y_ref[...], z_ref[...].astype(F32)
        sz = _sigmoid(z)
        silu = z * sz
        yz = yv * silu
        dyz, dw = _rms_bwd(yz, _rms(yz), w_ref[...], d_ref[...].astype(F32))
        dw_ref[...] += dw
        dy_ref[...] = dyz * silu
        dz_ref[...] = (dyz * yv * sz * (1.0 + z * (1.0 - sz))).astype(BF16)

    return pl.pallas_call(
        body, grid=(t_rows // tr,),
        in_specs=[_row(tr, D_INNER), _row(tr, D_INNER, OFF_Z // D_INNER), _row(tr, D_INNER), _full((1, D_INNER)), _ANY],
        out_specs=[_row(tr, D_INNER), _row(tr, D_INNER, OFF_Z // D_INNER), _full((1, D_INNER))],
        out_shape=[_sds((t_rows, D_INNER), F32), _sds(dproj.shape, dproj.dtype), _sds((1, D_INNER), F32)],
        input_output_aliases={4: 1},
        name="ssm_post_bwd", compiler_params=_params(1))(y, proj, dyn, w, dproj)


def _ssm_conv_bwd(xc, proj, dxs, dbm, dcm, conv_w, dproj):
    t_rows = xc.shape[0]
    tr = CHUNK
    nt = t_rows // tr
    bc_w = SSM_GROUPS * D_STATE

    def body(xc_ref, x_ref, dxs_ref, db_ref, dc_ref, w_ref, _, dx_ref, dw_ref, dbias_ref, ahead):
        first = pl.program_id(0) == 0

        @pl.when(first)
        def _():
            dw_ref[...] = jnp.zeros_like(dw_ref)
            dbias_ref[...] = jnp.zeros_like(dbias_ref)

        for c0 in range(0, CONV_DIM, STRIP_BWD):
            cols = slice(c0, c0 + STRIP_BWD)
            if c0 < D_INNER:
                dact = dxs_ref[:, cols]
            elif c0 < D_INNER + bc_w:
                dact = db_ref[:, c0 - D_INNER:c0 - D_INNER + STRIP_BWD]
            else:
                dact = dc_ref[:, c0 - D_INNER - bc_w:c0 - D_INNER - bc_w + STRIP_BWD]
            c = xc_ref[:, cols]
            s = _sigmoid(c)
            dpre = dact * s * (1.0 + c * (1.0 - s))
            x = x_ref[:, cols]
            dx = None
            for sh, moved in enumerate(_anticausal_taps(dpre, ahead.at[:, cols], first, SSM_CONV)):
                k = SSM_CONV - 1 - sh
                term = w_ref[k:k + 1, cols] * moved
                dx = term if dx is None else dx + term
                dw_ref[k:k + 1, cols] += jnp.sum(moved * x, axis=0, keepdims=True)
            dbias_ref[:, cols] += jnp.sum(dpre, axis=0, keepdims=True)
            dx_ref[:, cols] = dx.astype(BF16)

    xbc_block = OFF_XBC // CONV_DIM
    return pl.pallas_call(
        body, grid=(nt,),
        in_specs=[_row_rev(tr, CONV_DIM, nt), _row_rev(tr, CONV_DIM, nt, xbc_block), _row_rev(tr, D_INNER, nt),
                  _row_rev(tr, bc_w, nt), _row_rev(tr, bc_w, nt), _full((SSM_CONV, CONV_DIM)), _ANY],
        out_specs=[_row_rev(tr, CONV_DIM, nt, xbc_block), _full((SSM_CONV, CONV_DIM)), _full((1, CONV_DIM))],
        out_shape=[_sds(dproj.shape, dproj.dtype), _sds((SSM_CONV, CONV_DIM), F32), _sds((1, CONV_DIM), F32)],
        scratch_shapes=[pltpu.VMEM((HALO, CONV_DIM), F32)],
        input_output_aliases={6: 0},
        name="ssm_conv_bwd", compiler_params=_params(1))(xc, proj, dxs, dbm, dcm, conv_w, dproj)


def _prenorm_bwd(x, meta, dhn, dh, w):
    seq = x.shape[0]
    tr = _pick(seq, (512, 128))

    def body(x_ref, meta_ref, d_ref, r_ref, d0_ref, r0_ref, w_ref, dx_ref, dmeta_ref, dw_ref):
        wv = w_ref[...]

        @pl.when(pl.program_id(0) == 0)
        def _():
            h0 = jnp.concatenate([jnp.zeros((META_PAD, D_MODEL), F32), meta_ref[...]], axis=0)
            dx0, dw0 = _rms_bwd(h0, _rms(h0), wv, d0_ref[...])
            dw_ref[...] = dw0
            dmeta_ref[...] = (r0_ref[...] + dx0)[META_PAD:, :]

        h = x_ref[...]
        dx, dw = _rms_bwd(h, _rms(h), wv, d_ref[...])
        dw_ref[...] += dw
        dx_ref[...] = r_ref[...] + dx

    def shifted(tile):
        return pl.BlockSpec((pl.Element(tile), pl.Element(D_MODEL)), lambda i: (pl.multiple_of(i * tile + CHUNK, CHUNK), 0))

    first = pl.BlockSpec((CHUNK, D_MODEL), lambda i: (0, 0))
    return pl.pallas_call(
        body, grid=(seq // tr,),
        in_specs=[_row(tr, D_MODEL), _full((N_META, D_MODEL)), shifted(tr), shifted(tr), first, first, _full((1, D_MODEL))],
        out_specs=[_row(tr, D_MODEL), _full((N_META, D_MODEL)), _full((1, D_MODEL))],
        out_shape=[_sds((seq, D_MODEL), F32), _sds((N_META, D_MODEL), F32), _sds((1, D_MODEL), F32)],
        name="prenorm_bwd", compiler_params=_params(1))(x, meta, dhn, dh, dhn, dh, w)


def _dot01(x, m01, x_left, parts):
    acc, rest = None, x
    for i in range(parts):
        piece = rest.astype(BF16)
        term = (jnp.dot(piece, m01, preferred_element_type=F32) if x_left
                else jnp.dot(m01, piece, preferred_element_type=F32))
        acc = term if acc is None else acc + term
        if i + 1 < parts:
            rest = rest - piece.astype(F32)
    return acc


def _ssd_common(dtr_ref, dt_bias, a_log, chunk_index):
    rows = lax.broadcasted_iota(jnp.int32, (CHUNK, CHUNK), 0)
    cols = lax.broadcasted_iota(jnp.int32, (CHUNK, CHUNK), 1)
    low = rows >= cols
    raw = dtr_ref[:, :128]
    for g in range(1, SSM_GROUPS):
        raw = raw + pltpu.roll(dtr_ref[:, g * 128:(g + 1) * 128], HEADS_PER_GROUP * g, 1)
    raw = raw + dt_bias
    live = _row_ids(raw.shape, chunk_index, CHUNK) >= META_PAD
    dt = jnp.where(live, _softplus(raw), 0.0)
    a_head = -jnp.exp(a_log)
    cs = _dot01(dt * a_head, low.astype(BF16), False, 3)
    return dict(low=low, triu=(rows <= cols).astype(BF16), raw=raw, live=live, dt=dt, a_head=a_head, cs=cs, cs_t=cs.T,
                grow=jnp.exp(cs),
                fade=jnp.exp(cs[CHUNK - 1:CHUNK, :] - cs))


def _ssd_expand(cm, g):
    first = HEADS_PER_GROUP * g
    expand = (lax.broadcasted_iota(jnp.int32, (CHUNK, GROUP_W), 1) // HEAD_P + first
              == lax.broadcasted_iota(jnp.int32, (CHUNK, GROUP_W), 0)).astype(BF16)
    fold = (lax.broadcasted_iota(jnp.int32, (GROUP_W, CHUNK), 0) // HEAD_P + first
            == lax.broadcasted_iota(jnp.int32, (GROUP_W, CHUNK), 1)).astype(BF16)
    return dict(fold=fold, dtx=_dot01(cm["dt"], expand, True, 2), growx=_dot01(cm["grow"], expand, True, 2),
                fadex=_dot01(cm["fade"], expand, True, 2))


def _decay_matrix(cm, j):
    diff = cm["cs"][:, j:j + 1] - cm["cs_t"][j:j + 1, :]
    return jnp.where(cm["low"], jnp.exp(jnp.where(cm["low"], diff, 0.0)), 0.0)


def _dot(a, b, dims):
    return lax.dot_general(a.astype(BF16), b.astype(BF16), (dims, ((), ())), preferred_element_type=F32)


def _dot_fine(a, b, dims):
    a_hi, b_hi = a.astype(BF16), b.astype(BF16)
    a_lo, b_lo = (a - a_hi.astype(F32)).astype(BF16), (b - b_hi.astype(F32)).astype(BF16)
    dn = (dims, ((), ()))
    return (lax.dot_general(a_hi, b_hi, dn, preferred_element_type=F32)
            + lax.dot_general(a_hi, b_lo, dn, preferred_element_type=F32)
            + lax.dot_general(a_lo, b_hi, dn, preferred_element_type=F32))


def _ssd_specs(nt, rev):
    def idx(c):
        return nt - 1 - c if rev else c
    bc_w = SSM_GROUPS * D_STATE
    xs = pl.BlockSpec((CHUNK, D_INNER), lambda c: (idx(c), 0))
    bm = pl.BlockSpec((CHUNK, bc_w), lambda c: (idx(c), D_INNER // bc_w))
    cm = pl.BlockSpec((CHUNK, bc_w), lambda c: (idx(c), D_INNER // bc_w + 1))
    dtr = pl.BlockSpec((CHUNK, SSM_GROUPS * 128), lambda c: (idx(c), OFF_DT // (SSM_GROUPS * 128)))
    par = _full((1, 128))
    par_x = _full((SSM_GROUPS, 1, GROUP_W))
    return xs, bm, cm, dtr, par, par_x, idx


def _group_cols(g, width):
    return slice(g * width, (g + 1) * width)


def _ssd_fwd(xact, proj, dtb, alog, dskip_x):
    t_rows = xact.shape[0]
    nt = t_rows // CHUNK
    xs_spec, b_spec, c_spec, dtr_spec, par, par_x, _ = _ssd_specs(nt, False)

    def body(xs_ref, b_ref, c_ref, dtr_ref, dtb_ref, alog_ref, dsk_ref, y_ref, hst_ref, state):
        c = pl.program_id(0)

        @pl.when(c == 0)
        def _():
            state[...] = jnp.zeros_like(state)

        cm = _ssd_common(dtr_ref, dtb_ref[...], alog_ref[...], c)
        for g in range(SSM_GROUPS):
            wide, narrow = _group_cols(g, GROUP_W), _group_cols(g, D_STATE)
            ex = _ssd_expand(cm, g)
            xs, bm, cmat = xs_ref[:, wide], b_ref[:, narrow], c_ref[:, narrow]
            x_dt = xs * ex["dtx"]
            h_in = state[g]
            hst_ref[0, g] = h_in
            y_ref[:, wide] = _dot(cmat, h_in, ((1,), (0,))) * ex["growx"] + xs * dsk_ref[g]
            cb = _dot(cmat, bm, ((1,), (1,)))
            for j in range(HEADS_PER_GROUP):
                sl = slice(g * GROUP_W + j * HEAD_P, g * GROUP_W + (j + 1) * HEAD_P)
                decay = _decay_matrix(cm, HEADS_PER_GROUP * g + j)
                y_ref[:, sl] += _dot(cb * decay, x_dt[:, j * HEAD_P:(j + 1) * HEAD_P], ((1,), (0,)))
            state[g] = h_in * ex["growx"][CHUNK - 1:CHUNK, :] + _dot_fine(bm, x_dt * ex["fadex"], ((0,), (0,)))

    return pl.pallas_call(
        body, grid=(nt,),
        in_specs=[xs_spec, b_spec, c_spec, dtr_spec, par, par, par_x],
        out_specs=[xs_spec, pl.BlockSpec((1, SSM_GROUPS, D_STATE, GROUP_W), lambda c: (c, 0, 0, 0))],
        out_shape=[_sds((t_rows, D_INNER), F32), _sds((nt, SSM_GROUPS, D_STATE, GROUP_W), F32)],
        scratch_shapes=[pltpu.VMEM((SSM_GROUPS, D_STATE, GROUP_W), F32)],
        name="ssd_fwd", compiler_params=_params(1))(xact, xact, xact, proj, dtb, alog, dskip_x)


def _ssd_bwd(xact, proj, dtb, alog, dskip_x, dy, hst, dproj):
    t_rows = xact.shape[0]
    nt = t_rows // CHUNK
    xs_spec, b_spec, c_spec, dtr_spec, par, par_x, idx = _ssd_specs(nt, True)
    h_spec = pl.BlockSpec((1, SSM_GROUPS, D_STATE, GROUP_W), lambda c: (idx(c), 0, 0, 0))
    hn_spec = pl.BlockSpec((1, SSM_GROUPS, D_STATE, GROUP_W), lambda c: (jnp.minimum(idx(c) + 1, nt - 1), 0, 0, 0))
    bc_out = pl.BlockSpec((CHUNK, SSM_GROUPS * D_STATE), lambda c: (idx(c), 0))

    def body(xs_ref, b_ref, c_ref, dtr_ref, dtb_ref, alog_ref, dsk_ref, dy_ref, h_ref, hn_ref, _,
             dxs_ref, db_ref, dc_ref, ddt_ref, dalog_ref, ddtb_ref, dd_ref, dstate, dx_buf):
        step = pl.program_id(0)

        @pl.when(step == 0)
        def _():
            dstate[...] = jnp.zeros_like(dstate)
            dalog_ref[...] = jnp.zeros_like(dalog_ref)
            ddtb_ref[...] = jnp.zeros_like(ddtb_ref)
            dd_ref[...] = jnp.zeros_like(dd_ref)

        cm = _ssd_common(dtr_ref, dtb_ref[...], alog_ref[...], idx(step))
        for g in range(SSM_GROUPS):
            _ssd_bwd_group(g, cm, xs_ref, b_ref, c_ref, dsk_ref, dy_ref, h_ref, hn_ref,
                           dxs_ref, db_ref, dc_ref, ddt_ref, dalog_ref, ddtb_ref, dd_ref, dstate, dx_buf)

    return pl.pallas_call(
        body, grid=(nt,),
        in_specs=[xs_spec, b_spec, c_spec, dtr_spec, par, par, par_x, xs_spec, h_spec, hn_spec, _ANY],
        out_specs=[xs_spec, bc_out, bc_out, dtr_spec, par, par, par_x],
        out_shape=[_sds((t_rows, D_INNER), F32), _sds((t_rows, SSM_GROUPS * D_STATE), F32),
                   _sds((t_rows, SSM_GROUPS * D_STATE), F32), _sds(dproj.shape, dproj.dtype),
                   _sds((1, 128), F32), _sds((1, 128), F32), _sds((SSM_GROUPS, 1, GROUP_W), F32)],
        scratch_shapes=[pltpu.VMEM((SSM_GROUPS, D_STATE, GROUP_W), F32), pltpu.VMEM((CHUNK, GROUP_W), F32)],
        input_output_aliases={10: 3},
        name="ssd_bwd", compiler_params=_params(1))(xact, xact, xact, proj, dtb, alog, dskip_x, dy, hst, hst, dproj)


def _ssd_bwd_group(g, cm, xs_ref, b_ref, c_ref, dsk_ref, dy_ref, h_ref, hn_ref,
                   dxs_ref, db_ref, dc_ref, ddt_ref, dalog_ref, ddtb_ref, dd_ref, dstate, dx_buf):
    wide, narrow = _group_cols(g, GROUP_W), _group_cols(g, D_STATE)
    first = HEADS_PER_GROUP * g
    ex = _ssd_expand(cm, g)
    xs, bm, cmat = xs_ref[:, wide], b_ref[:, narrow], c_ref[:, narrow]
    dsk = dsk_ref[g]
    x_dt = xs * ex["dtx"]
    h_in, h_next = h_ref[0, g], hn_ref[0, g]
    dyv = dy_ref[:, wide]
    dh = dstate[g]
    grow, fade = ex["growx"], ex["fadex"]
    dy_grow = dyv * grow
    x_fade = x_dt * fade
    cb = _dot(cmat, bm, ((1,), (1,)))
    ml = jnp.zeros((CHUNK, CHUNK), F32)
    row_id = lax.broadcasted_iota(jnp.int32, (CHUNK, CHUNK), 0)
    col_id = lax.broadcasted_iota(jnp.int32, (CHUNK, CHUNK), 1)
    w_rows = jnp.zeros((CHUNK, CHUNK), F32)
    w_cols = jnp.zeros((CHUNK, CHUNK), F32)
    for j in range(HEADS_PER_GROUP):
        sl = slice(j * HEAD_P, (j + 1) * HEAD_P)
        lm = _decay_matrix(cm, first + j)
        mlj = _dot(dyv[:, sl], x_dt[:, sl], ((1,), (1,))) * lm
        ml = ml + mlj
        wm = mlj * cb
        w_rows = jnp.where(col_id == first + j, jnp.sum(wm, axis=1, keepdims=True), w_rows)
        w_cols = jnp.where(row_id == first + j, jnp.sum(wm, axis=0, keepdims=True), w_cols)
        dx_buf[:, sl] = _dot(cb * lm, dyv[:, sl], ((0,), (0,)))
    dx_off = fade * _dot_fine(bm, dh, ((1,), (0,)))
    dx = dx_buf[...] + dx_off
    dc_ref[:, narrow] = _dot(ml, bm, ((1,), (0,))) + _dot(dy_grow, h_in, ((1,), (1,)))
    db_ref[:, narrow] = _dot(ml, cmat, ((0,), (0,))) + _dot(x_fade, dh, ((1,), (1,)))
    fold = ex["fold"]
    y_off = _dot_fine(cmat, h_in, ((1,), (0,))) * grow
    dcs = (w_rows - w_cols.T) + _dot01(dyv * y_off - x_dt * dx_off, fold, True, 2)
    tail = jnp.broadcast_to(jnp.sum(dh * h_next, axis=0, keepdims=True), (8, GROUP_W))
    tail = _dot01(tail, fold, True, 2)[0:1, :]
    last_row = lax.broadcasted_iota(jnp.int32, (CHUNK, 128), 0) == CHUNK - 1
    dcs = dcs + jnp.where(last_row, tail, 0.0)
    da = _dot01(dcs, cm["triu"], False, 3)
    ddt = da * cm["a_head"] + _dot01(dx * xs, fold, True, 2)
    ddt_raw = jnp.where(cm["live"], ddt * _sigmoid(cm["raw"]), 0.0)
    ddt_ref[:, narrow] = (ddt_raw if g == 0 else pltpu.roll(ddt_raw, 128 - first, 1)).astype(BF16)
    ddtb_ref[...] += jnp.sum(ddt_raw, axis=0, keepdims=True)
    dalog_ref[...] += jnp.sum(da * cm["dt"], axis=0, keepdims=True) * cm["a_head"]
    dd_ref[g] += jnp.sum(dyv * xs, axis=0, keepdims=True)
    dxs_ref[:, wide] = dx * ex["dtx"] + dyv * dsk
    dstate[g] = dh * grow[CHUNK - 1:CHUNK, :] + _dot_fine(cmat, dy_grow, ((0,), (0,)))


def _swa_bias():
    rows_q = ATTN_GROUP * CHUNK
    dist = (jnp.arange(rows_q) % CHUNK)[:, None] - jnp.arange(2 * CHUNK)[None, :] + CHUNK
    head = jnp.arange(KV_HEADS)[:, None] * ATTN_GROUP + jnp.arange(rows_q)[None, :] // CHUNK + 1
    slope = jnp.exp2(-8.0 * head.astype(F32) / ATTN_HEADS)
    return jnp.where((dist >= 0) & (dist < CHUNK), -slope[:, :, None] * dist.astype(F32)[None], NEG)


def _swa_probs(q_kv, k_prev, k_cur, k_first, sink, bias, n):
    rows_q = ATTN_GROUP * CHUNK
    qs = jnp.concatenate([q_kv[:, g * DH:(g + 1) * DH] for g in range(ATTN_GROUP)], axis=0) * (DH ** -0.5)
    kcat = jnp.concatenate([k_prev, k_cur], axis=0)
    kmeta = k_first[META_PAD:, :]
    key_ok = lax.broadcasted_iota(jnp.int32, (1, 2 * CHUNK), 1) + n * CHUNK >= 2 * CHUNK
    s_band = jnp.where(key_ok, _dot(qs, kcat, ((1,), (1,))) + bias, NEG)
    q_pos = lax.broadcasted_iota(jnp.int32, (rows_q, N_META), 0) % CHUNK + n * CHUNK - META_PAD
    ok_m = lax.broadcasted_iota(jnp.int32, (rows_q, N_META), 1) <= q_pos
    s_meta = jnp.where(ok_m, _dot(qs, kmeta, ((1,), (1,))), NEG)
    m = jnp.maximum(jnp.maximum(jnp.max(s_band, axis=1, keepdims=True), jnp.max(s_meta, axis=1, keepdims=True)), sink)
    p_band, p_meta, p_sink = jnp.exp(s_band - m), jnp.exp(s_meta - m), jnp.exp(sink - m)
    inv = 1.0 / (jnp.sum(p_band, axis=1, keepdims=True) + jnp.sum(p_meta, axis=1, keepdims=True) + p_sink)
    return qs, kcat, kmeta, p_band * inv, p_meta * inv, p_sink * inv


def _swa_specs(nt, rev):
    def idx(n):
        return nt - 1 - n if rev else n
    o = pl.BlockSpec((CHUNK, ATTN_HEADS * DH), lambda n: (idx(n), 0))
    chunks = (lambda c: jnp.maximum(c - 1, 0)), (lambda c: c), (lambda c: 0)
    qkv = [pl.BlockSpec((CHUNK, QKV_W), lambda n, f=f: (f(idx(n)), OFF_Q // QKV_W)) for f in chunks]
    sink = _full((KV_HEADS, ATTN_GROUP * CHUNK, 1))
    bias = _full((KV_HEADS, ATTN_GROUP * CHUNK, 2 * CHUNK))
    return o, qkv, sink, bias, idx


def _head_cols(k):
    kv_w = ATTN_GROUP * DH
    q0, k0, v0 = k * kv_w, OFF_K - OFF_Q + k * DH, OFF_V - OFF_Q + k * DH
    return slice(q0, q0 + kv_w), slice(k0, k0 + DH), slice(v0, v0 + DH)


def _swa_fwd(proj, sink_rows, bias):
    t_rows = proj.shape[0]
    nt = t_rows // CHUNK
    o_spec, qkv_specs, sink_spec, bias_spec, _ = _swa_specs(nt, False)
    kv_w = ATTN_GROUP * DH

    def body(prev_ref, cur_ref, first_ref, sink_ref, bias_ref, o_ref):
        n = pl.program_id(0)
        for k in range(KV_HEADS):
            qc, kc, vc = _head_cols(k)
            _, _, _, p_band, p_meta, _ = _swa_probs(cur_ref[:, qc], prev_ref[:, kc], cur_ref[:, kc], first_ref[:, kc],
                                                    sink_ref[k], bias_ref[k], n)
            vcat = jnp.concatenate([prev_ref[:, vc], cur_ref[:, vc]], axis=0)
            out = _dot(p_band, vcat, ((1,), (0,))) + _dot(p_meta, first_ref[:, vc][META_PAD:, :], ((1,), (0,)))
            for g in range(ATTN_GROUP):
                o_ref[:, k * kv_w + g * DH:k * kv_w + (g + 1) * DH] = out[g * CHUNK:(g + 1) * CHUNK, :]

    return pl.pallas_call(
        body, grid=(nt,), in_specs=qkv_specs + [sink_spec, bias_spec],
        out_specs=o_spec, out_shape=_sds((t_rows, ATTN_HEADS * DH), F32),
        name="swa_fwd", compiler_params=_params(1))(proj, proj, proj, sink_rows, bias)


def _swa_bwd(proj, sink_rows, bias, out, dout, dproj):
    t_rows = proj.shape[0]
    nt = t_rows // CHUNK
    o_spec, qkv_specs, sink_spec, bias_spec, idx = _swa_specs(nt, True)
    kv_w = ATTN_GROUP * DH
    k_off, v_off = OFF_K - OFF_Q, OFF_V - OFF_Q

    def body(prev_ref, cur_ref, first_ref, sink_ref, bias_ref, o_ref, do_ref, _, dqkv_ref, dsink_ref,
             carry_k, carry_v, meta_k, meta_v, dqkv_buf):
        step = pl.program_id(0)
        n = idx(step)

        @pl.when(step == 0)
        def _():
            carry_k[...] = jnp.zeros_like(carry_k)
            carry_v[...] = jnp.zeros_like(carry_v)
            meta_k[...] = jnp.zeros_like(meta_k)
            meta_v[...] = jnp.zeros_like(meta_v)
            dsink_ref[...] = jnp.zeros_like(dsink_ref)

        for k in range(KV_HEADS):
            cols = slice(k * kv_w, (k + 1) * kv_w)
            hd = slice(k * DH, (k + 1) * DH)
            qc, kc, vc = _head_cols(k)
            qs, kcat, kmeta, p_band, p_meta, p_sink = _swa_probs(cur_ref[:, qc], prev_ref[:, kc], cur_ref[:, kc],
                                                                 first_ref[:, kc], sink_ref[k], bias_ref[k], n)
            vcat = jnp.concatenate([prev_ref[:, vc], cur_ref[:, vc]], axis=0)
            vmeta = first_ref[:, vc][META_PAD:, :]
            o, do = o_ref[:, cols], do_ref[:, cols]
            os_ = jnp.concatenate([o[:, g * DH:(g + 1) * DH] for g in range(ATTN_GROUP)], axis=0)
            dos = jnp.concatenate([do[:, g * DH:(g + 1) * DH] for g in range(ATTN_GROUP)], axis=0)
            delta = jnp.sum(dos * os_, axis=1, keepdims=True)
            ds_band = p_band * (_dot(dos, vcat, ((1,), (1,))) - delta)
            ds_meta = p_meta * (_dot(dos, vmeta, ((1,), (1,))) - delta)
            ds_sink = -p_sink * delta
            dqs = (_dot(ds_band, kcat, ((1,), (0,))) + _dot(ds_meta, kmeta, ((1,), (0,)))) * (DH ** -0.5)
            for g in range(ATTN_GROUP):
                dqkv_buf[:, k * kv_w + g * DH:k * kv_w + (g + 1) * DH] = dqs[g * CHUNK:(g + 1) * CHUNK, :]
                dsink_ref[k, g:g + 1, :] += jnp.sum(ds_sink[g * CHUNK:(g + 1) * CHUNK, :])
            dkcat = _dot(ds_band, qs, ((0,), (0,)))
            dvcat = _dot(p_band, dos, ((0,), (0,)))
            meta_k[:, hd] += _dot(ds_meta, qs, ((0,), (0,)))
            meta_v[:, hd] += _dot(p_meta, dos, ((0,), (0,)))
            dqkv_buf[:, kc] = dkcat[CHUNK:, :] + carry_k[:, hd]
            dqkv_buf[:, vc] = dvcat[CHUNK:, :] + carry_v[:, hd]
            carry_k[:, hd] = dkcat[:CHUNK, :]
            carry_v[:, hd] = dvcat[:CHUNK, :]

        @pl.when(n == 0)
        def _():
            dqkv_buf[META_PAD:, k_off:k_off + KV_W] += meta_k[...]
            dqkv_buf[META_PAD:, v_off:v_off + KV_W] += meta_v[...]

        dqkv_ref[...] = dqkv_buf[...].astype(BF16)

    return pl.pallas_call(
        body, grid=(nt,),
        in_specs=qkv_specs + [sink_spec, bias_spec, o_spec, o_spec, pl.BlockSpec(memory_space=pl.ANY)],
        out_specs=[qkv_specs[1], _full((KV_HEADS, 8, 128))],
        out_shape=[_sds(dproj.shape, dproj.dtype), _sds((KV_HEADS, 8, 128), F32)],
        scratch_shapes=[pltpu.VMEM((CHUNK, KV_W), F32), pltpu.VMEM((CHUNK, KV_W), F32),
                        pltpu.VMEM((N_META, KV_W), F32), pltpu.VMEM((N_META, KV_W), F32),
                        pltpu.VMEM((CHUNK, QKV_W), F32)],
        input_output_aliases={7: 0},
        name="swa_bwd", compiler_params=_params(1))(proj, proj, proj, sink_rows, bias, out, dout, dproj)


def _pack_w_in_t(w_in_t):
    w_dt = w_in_t[CUT_DT:CUT_Q].reshape(SSM_GROUPS, HEADS_PER_GROUP, D_MODEL)
    w_dt = jnp.pad(w_dt, ((0, 0), (0, 128 - HEADS_PER_GROUP), (0, 0))).reshape(SSM_GROUPS * 128, D_MODEL)
    return jnp.concatenate([w_in_t[CUT_Z:CUT_XBC], w_in_t[CUT_G:], w_dt, w_in_t[CUT_Q:CUT_G], w_in_t[CUT_XBC:CUT_DT]], axis=0)


def _unpack_w_in_t(wp_t):
    w_dt = wp_t[OFF_DT:OFF_Q].reshape(SSM_GROUPS, 128, D_MODEL)[:, :HEADS_PER_GROUP].reshape(SSM_HEADS, D_MODEL)
    return jnp.concatenate([wp_t[OFF_Z:OFF_GATE], wp_t[OFF_XBC:], w_dt, wp_t[OFF_Q:OFF_XBC], wp_t[OFF_GATE:OFF_DT]], axis=0)


def _head_lanes(v):
    return jnp.pad(v.reshape(1, SSM_HEADS), ((0, 0), (0, 128 - SSM_HEADS)))


def _local_step(x, target, wt, late_weights=None, on_grad=None, started=None):
    seq = x.shape[0]
    grads = {}

    def emit(name, g):
        grads[name] = g
        return None if on_grad is None else on_grad(name, g)
    meta = wt["meta_tokens"]
    wp_t = _pack_w_in_t(wt["w_in_t"])
    dtb = _head_lanes(wt["ssm_dt_bias"].reshape(-1))
    alog = _head_lanes(wt["ssm_a_log"].reshape(-1))
    dskip_x = jnp.repeat(wt["ssm_d_skip"].reshape(-1), HEAD_P).reshape(SSM_GROUPS, 1, GROUP_W)
    sink_rows = jnp.repeat(wt["attn_sinks"].reshape(KV_HEADS, ATTN_GROUP), CHUNK, axis=1).reshape(KV_HEADS, ATTN_GROUP * CHUNK, 1)

    hn = _prenorm(x, meta, wt["norm_pre_mix"])
    proj = _matmul(hn, wp_t, tb=True, name="in_proj", after=started)
    xc, xact = _ssm_conv_fwd(proj, wt["ssm_conv_w"], wt["ssm_conv_b"])
    y, hst = _ssd_fwd(xact, proj, dtb, alog, dskip_x)
    yn = _ssm_post(y, proj, wt["ssm_norm"])
    if late_weights is not None:
        wt = {**wt, **late_weights(yn)}
    y_ssm = _matmul(yn, wt["w_ssm_out"], name="ssm_out")
    bias = _swa_bias()
    attn = _swa_fwd(proj, sink_rows, bias)
    y_attn = _matmul(attn, wt["w_attn_out"], name="attn_out")
    mixed = _mix_fwd(proj, y_ssm, y_attn)
    mix = _matmul(mixed, wt["w_mix_out"], name="mix_out")
    h1, hn2 = _postmix(x, meta, mix, wt["norm_post_mix"], wt["norm_pre_ffn"])
    up = _matmul(hn2, wt["w_ffn_up_t"], tb=True, out_dtype=BF16, name="ffn_up")
    u, act = _ffn_act(up, wt["ffn_conv_w"], wt["ffn_conv_b"])
    f = _matmul(act, wt["w_ffn_down"], name="ffn_down")
    df, dy, g_norm_post_ffn, loss_row = _final(h1, f, target, wt["norm_post_ffn"])

    grads["norm_post_ffn"] = g_norm_post_ffn
    sent = emit("w_ffn_down", _matmul(act, df, ta=True, out_dtype=BF16, name="dw_ffn_down"))
    dact = _matmul(df, wt["w_ffn_down"], tb=True, out_dtype=BF16, name="d_act", after=sent)
    dup, grads["ffn_conv_w"], grads["ffn_conv_b"] = _ffn_act_bwd(u, up, dact, wt["ffn_conv_w"])
    sent = emit("w_ffn_up_t", _matmul(dup, hn2, ta=True, out_dtype=BF16, name="dw_ffn_up"))
    dhn2 = _matmul(dup, wt["w_ffn_up_t"], name="d_hn2", after=sent)
    dmix, dh, grads["norm_pre_ffn"], grads["norm_post_mix"] = _postmix_bwd(h1, dhn2, dy, mix, wt["norm_pre_ffn"], wt["norm_post_mix"])
    sent = emit("w_mix_out", _matmul(mixed, dmix, ta=True, out_dtype=BF16, name="dw_mix_out"))
    dmixed = _matmul(dmix, wt["w_mix_out"], tb=True, name="d_mixed", after=sent)
    dy_ssm, dy_attn, dproj = _mix_bwd(dmixed, proj, y_ssm, y_attn, lax.empty(proj.shape, BF16))
    sent = emit("w_ssm_out", _matmul(yn, dy_ssm, ta=True, out_dtype=BF16, name="dw_ssm_out"))
    dyn = _matmul(dy_ssm, wt["w_ssm_out"], tb=True, out_dtype=BF16, name="d_yn", after=sent)
    sent = emit("w_attn_out", _matmul(attn, dy_attn, ta=True, out_dtype=BF16, name="dw_attn_out"))
    dattn = _matmul(dy_attn, wt["w_attn_out"], tb=True, name="d_attn", after=sent)
    dy_ssd, dproj, grads["ssm_norm"] = _ssm_post_bwd(y, proj, dyn, wt["ssm_norm"], dproj)
    dxs, dbm, dcm, dproj, dalog, ddtb, dd_x = _ssd_bwd(xact, proj, dtb, alog, dskip_x, dy_ssd, hst, dproj)
    grads["ssm_a_log"] = dalog[:, :SSM_HEADS]
    grads["ssm_dt_bias"] = ddtb[:, :SSM_HEADS]
    grads["ssm_d_skip"] = dd_x.reshape(SSM_HEADS, HEAD_P).sum(axis=1).reshape(1, SSM_HEADS)
    dproj, grads["ssm_conv_w"], grads["ssm_conv_b"] = _ssm_conv_bwd(xc, proj, dxs, dbm, dcm, wt["ssm_conv_w"], dproj)
    dproj, dsink = _swa_bwd(proj, sink_rows, bias, attn, dattn, dproj)
    grads["attn_sinks"] = dsink[:, :ATTN_GROUP, 0].reshape(1, ATTN_HEADS)
    sent = emit("w_in_t", _unpack_w_in_t(_matmul(dproj, hn, ta=True, out_dtype=BF16, name="dw_in")))
    dhn = _matmul(dproj, wp_t, name="d_hn", after=sent)
    grad_x, grads["meta_tokens"], grads["norm_pre_mix"] = _prenorm_bwd(x, meta, dhn, dh, wt["norm_pre_mix"])
    return loss_row[0, 0], grad_x, grads


def _all_gather(shards):
    n = len(shards)

    def body(*refs):
        ins, outs = refs[:n], refs[n:2 * n]
        send_sems, recv_sems, local_sems = refs[2 * n:]
        x, y, c = lax.axis_index("x"), lax.axis_index("y"), lax.axis_index("c")
        me, sibling = (x, y, c), (x, y, 1 - c)
        x_nbr, y_nbr, diag = (1 - x, y), (x, 1 - y), (1 - x, 1 - y)
        relayed = (x ^ (1 - c), y ^ c)
        relay_to = (x ^ c, y ^ (1 - c))

        def slot(a, dev):
            return outs[a].at[4 * dev[0] + 2 * dev[1] + dev[2]]

        def copy(k, a, block, to, src=None):
            return pltpu.make_async_remote_copy(
                src_ref=slot(a, block) if src is None else src, dst_ref=slot(a, block),
                send_sem=send_sems.at[k, a], recv_sem=recv_sems.at[k, a],
                device_id=to, device_id_type=pl.DeviceIdType.MESH)

        mine = [pltpu.make_async_copy(ins[a], slot(a, me), local_sems.at[a]) for a in range(n)]
        for cp in mine:
            cp.start()
        first = [copy(0, a, me, sibling, src=ins[a]) for a in range(n)]
        first += [copy(1, a, me, (*x_nbr, c), src=ins[a]) for a in range(n)]
        first += [copy(2, a, me, (*y_nbr, c), src=ins[a]) for a in range(n)]
        for cp in first:
            cp.start()
        passed = []

        def pass_on(k, block, to):
            for a in range(n):
                cp = copy(k, a, block, to)
                cp.start()
                passed.append(cp)

        for j, chip in enumerate((x_nbr, y_nbr)):
            for a in range(n):
                copy(1 + j, a, (*chip, c), me).wait_recv()
            pass_on(4 + j, (*chip, c), sibling)
        pass_on(3, (*relayed, c), (*relay_to, c))
        for a in range(n):
            copy(3, a, (*diag, c), me).wait_recv()
        pass_on(6, (*diag, c), sibling)
        for a in range(n):
            copy(0, a, sibling, me).wait_recv()
        for j, chip in enumerate((x_nbr, y_nbr, diag)):
            for a in range(n):
                copy(4 + j, a, (*chip, 1 - c), me).wait_recv()
        for cp in first + passed:
            cp.wait_send()
        for cp in mine:
            cp.wait()

    hbm = pl.BlockSpec(memory_space=pl.ANY)
    return pl.pallas_call(
        body, in_specs=[hbm] * n, out_specs=[hbm] * n,
        out_shape=[_sds((N_DEV,) + s.shape, s.dtype) for s in shards],
        scratch_shapes=[pltpu.SemaphoreType.DMA((7, n)), pltpu.SemaphoreType.DMA((7, n)), pltpu.SemaphoreType.DMA((n,))],
        name="gather_weights")(*shards)


def _peer_table():
    x, y, c = lax.axis_index("x"), lax.axis_index("y"), lax.axis_index("c")
    peers = []
    for k in range(N_DEV - 1):
        bits = k + 1
        p = (x ^ ((bits >> 2) & 1), y ^ ((bits >> 1) & 1), c ^ (bits & 1))
        peers.append((k, p, 4 * p[0] + 2 * p[1] + p[2]))
    return 4 * x + 2 * y + c, peers


_HBM = pl.BlockSpec(memory_space=pltpu.HBM)
_SEM = pl.BlockSpec(memory_space=pltpu.SEMAPHORE)
_EFFECT = pltpu.SideEffectType.DATAFLOW_SIDE_EFFECTING


def _push_copy(src, land, send_sems, recv_sems, a, k, p, src_slot, dst_slot):
    sem = a * (N_DEV - 1) + k
    return pltpu.make_async_remote_copy(
        src_ref=src[a] if src_slot is None else src[a].at[src_slot], dst_ref=land[a].at[dst_slot],
        send_sem=send_sems.at[sem], recv_sem=recv_sems.at[sem], device_id=p, device_id_type=pl.DeviceIdType.MESH)


def _push_start(srcs, scatter, name):
    n = len(srcs)
    lands = [lax.empty(s.shape if scatter else (N_DEV,) + s.shape, s.dtype) for s in srcs]

    def body(*refs):
        src, land = refs[:n], refs[n:2 * n]
        send_sems, recv_sems, token = refs[2 * n], refs[2 * n + 1], refs[-1]
        my_id, peers = _peer_table()
        for a in range(n):
            for k, p, p_id in peers:
                _push_copy(src, land, send_sems, recv_sems, a, k, p, p_id if scatter else None, my_id).start()
        token[...] = jnp.zeros_like(token)

    sems = pltpu.SemaphoreType.DMA(((N_DEV - 1) * n,))
    res = pl.pallas_call(
        body, name=name,
        out_shape=(sems, sems, *[pltpu.HBM(a.shape, a.dtype) for a in srcs + lands], _sds((8, 128), F32)),
        in_specs=[_HBM] * (2 * n), out_specs=(_SEM, _SEM, *[_HBM] * (2 * n), pl.BlockSpec(memory_space=pltpu.VMEM)),
        input_output_aliases={i: 2 + i for i in range(2 * n)},
        compiler_params=pltpu.CompilerParams(has_side_effects=_EFFECT),
    )(*[pltpu.with_memory_space_constraint(a, pltpu.HBM) for a in srcs + lands])
    return dict(send=res[0], recv=res[1], src=list(res[2:2 + n]), land=list(res[2 + n:2 + 2 * n]), token=res[-1],
                scatter=scatter)


def _push_wait(handle, after, name):
    n = len(handle["src"])
    scatter = handle["scatter"]

    def body(*refs):
        src, land = refs[:n], refs[n:2 * n]
        send_sems, recv_sems = refs[2 * n], refs[2 * n + 1]
        _, peers = _peer_table()
        for a in range(n):
            for k, p, p_id in peers:
                cp = _push_copy(src, land, send_sems, recv_sems, a, k, p, p_id if scatter else None, p_id)
                cp.wait_send()
                cp.wait_recv()

    arrays = handle["src"] + handle["land"]
    res = pl.pallas_call(
        body, name=name, out_shape=tuple(pltpu.HBM(a.shape, a.dtype) for a in arrays),
        in_specs=[_HBM] * (2 * n) + [_SEM, _SEM, pl.BlockSpec(memory_space=pl.ANY)], out_specs=tuple([_HBM] * (2 * n)),
        input_output_aliases={i: i for i in range(2 * n)},
        compiler_params=pltpu.CompilerParams(has_side_effects=_EFFECT),
    )(*arrays, handle["send"], handle["recv"], after)
    return list(res[:n]), list(res[n:])


def _slot_sum(p_ref, own_ref):
    if own_ref is not None:
        my_id = 4 * lax.axis_index("x") + 2 * lax.axis_index("y") + lax.axis_index("c")
        mine = own_ref[...].astype(F32)
    g = None
    for s in range(p_ref.shape[0]):
        term = p_ref[s].astype(F32)
        if own_ref is not None:
            term = jnp.where(my_id == s, mine, term)
        g = term if g is None else g + term
    return g


def _to_bf16(arrays):
    n = len(arrays)

    def body(*refs):
        for i in range(n):
            refs[n + i][...] = refs[i][...].astype(BF16)

    return pl.pallas_call(body, out_shape=[_sds(a.shape, BF16) for a in arrays], name="weights_to_bf16",
                          compiler_params=pltpu.CompilerParams(vmem_limit_bytes=VMEM_LIMIT))(*arrays)


def _adamw(parts, own, w, m, v, name):
    unit_rows = w.ndim == 3
    rows, cols = w.shape[0], w.shape[-1]
    if rows % 16 == 0:
        tr, tc = _pick(rows, (256, 128, 176, 64, 32, 16)), cols
    else:
        tr, tc = rows, _pick(cols, (256, 128))

    def body(*refs):
        if own is None:
            p_ref, w_ref, m_ref, v_ref, g_ref, d_ref, nm_ref, nv_ref = refs
            own_ref = None
        else:
            p_ref, own_ref, w_ref, m_ref, v_ref, g_ref, d_ref, nm_ref, nv_ref = refs
        g = _slot_sum(p_ref, own_ref)
        if unit_rows:
            g = g.reshape(tr, 1, tc)
        m_new = ADAM_B1 * m_ref[...] + (1.0 - ADAM_B1) * g
        v_new = ADAM_B2 * v_ref[...] + (1.0 - ADAM_B2) * (g * g)
        m_hat = m_new / (1.0 - ADAM_B1 ** ADAM_STEP)
        v_hat = v_new / (1.0 - ADAM_B2 ** ADAM_STEP)
        g_ref[...] = g
        d_ref[...] = -ADAM_LR * (m_hat / (jnp.sqrt(v_hat) + ADAM_EPS) + ADAM_WD * w_ref[...])
        nm_ref[...] = m_new
        nv_ref[...] = v_new

    by_rows = tc == cols
    spec = pl.BlockSpec((tr, tc), (lambda i: (i, 0)) if by_rows else (lambda i: (0, i)))
    state_spec = spec if not unit_rows else pl.BlockSpec((tr, 1, tc), (lambda i: (i, 0, 0)) if by_rows else (lambda i: (0, 0, i)))
    parts_spec = pl.BlockSpec((parts.shape[0], tr, tc), (lambda i: (0, i, 0)) if by_rows else (lambda i: (0, 0, i)))
    operands = (parts, w, m, v) if own is None else (parts, own, w, m, v)
    return pl.pallas_call(
        body, grid=(rows // tr if by_rows else cols // tc,),
        in_specs=[parts_spec] + ([] if own is None else [spec]) + [state_spec] * 3,
        out_specs=[state_spec] * 4, out_shape=[_sds(w.shape, F32)] * 4,
        name=name, compiler_params=_params(1))(*operands)


SMALL_REPLICATED = (("norm_pre_mix", 1024), ("ssm_conv_b", 3072), ("ssm_dt_bias", 32), ("ssm_a_log", 32),
                    ("ssm_d_skip", 32), ("ssm_norm", 2048), ("attn_sinks", 16), ("norm_post_mix", 1024),
                    ("norm_pre_ffn", 1024), ("ffn_conv_b", 5632), ("norm_post_ffn", 1024))
SMALL_SHARDED = (("meta_tokens", (N_META, D_MODEL // N_DEV)), ("ssm_conv_w", (SSM_CONV, CONV_DIM // N_DEV)),
                 ("ffn_conv_w", (FFN_CONV, 2 * FFN_DIM // N_DEV)))
BIG = (("w_in", (D_MODEL, N_IN // N_DEV), 1), ("w_ssm_out", (D_INNER // N_DEV, D_MODEL), 0),
       ("w_attn_out", (D_MODEL // N_DEV, D_MODEL), 0), ("w_mix_out", (D_MODEL // N_DEV, D_MODEL), 0),
       ("w_ffn_up", (D_MODEL, 2 * FFN_DIM // N_DEV), 1), ("w_ffn_down", (FFN_DIM // N_DEV, D_MODEL), 0))


def _rows_of(size):
    return -(-size // 128)


def _as_rows(flat):
    size = flat.shape[-1]
    rows = _rows_of(size)
    flat = jnp.pad(flat, [(0, 0)] * (flat.ndim - 1) + [(0, rows * 128 - size)])
    return flat.reshape(flat.shape[:-1] + (rows, 128))


def _pack_small(rep, sharded):
    pieces = [_as_rows(rep[name].reshape(-1)) for name, _ in SMALL_REPLICATED]
    pieces += [_as_rows(sharded[name].reshape(-1)) for name, _ in SMALL_SHARDED]
    packed = jnp.concatenate(pieces, axis=0)
    return jnp.pad(packed, ((0, -packed.shape[0] % 8), (0, 0)))


def _unpack_small(packed):
    out, row = {}, 0
    for name, size in SMALL_REPLICATED:
        out[name] = packed[row:row + _rows_of(size)].reshape(-1)[:size].reshape(1, size)
        row += _rows_of(size)
    for name, (r, c) in SMALL_SHARDED:
        out[name] = packed[row:row + _rows_of(r * c)].reshape(-1)[:r * c].reshape(r, c)
        row += _rows_of(r * c)
    return out


def _shard_major(g, shape, axis):
    r, c = shape
    if axis == 0:
        return g.reshape(N_DEV, r, c)
    return g.reshape(r, N_DEV, c).transpose(1, 0, 2)


def kernel(x, meta_tokens, norm_pre_mix, w_in, ssm_conv_w, ssm_conv_b, ssm_dt_bias, ssm_a_log, ssm_d_skip, ssm_norm, w_ssm_out, attn_sinks, w_attn_out, w_mix_out, norm_post_mix, norm_pre_ffn, w_ffn_up, ffn_conv_w, ffn_conv_b, w_ffn_down, norm_post_ffn, loss_target, m_meta_tokens, m_norm_pre_mix, m_w_in, m_ssm_conv_w, m_ssm_conv_b, m_ssm_dt_bias, m_ssm_a_log, m_ssm_d_skip, m_ssm_norm, m_w_ssm_out, m_attn_sinks, m_w_attn_out, m_w_mix_out, m_norm_post_mix, m_norm_pre_ffn, m_w_ffn_up, m_ffn_conv_w, m_ffn_conv_b, m_w_ffn_down, m_norm_post_ffn, v_meta_tokens, v_norm_pre_mix, v_w_in, v_ssm_conv_w, v_ssm_conv_b, v_ssm_dt_bias, v_ssm_a_log, v_ssm_d_skip, v_ssm_norm, v_w_ssm_out, v_attn_sinks, v_w_attn_out, v_w_mix_out, v_norm_post_mix, v_norm_pre_ffn, v_w_ffn_up, v_ffn_conv_w, v_ffn_conv_b, v_w_ffn_down, v_norm_post_ffn):
    names = ("meta_tokens", "norm_pre_mix", "w_in", "ssm_conv_w", "ssm_conv_b", "ssm_dt_bias", "ssm_a_log", "ssm_d_skip",
             "ssm_norm", "w_ssm_out", "attn_sinks", "w_attn_out", "w_mix_out", "norm_post_mix", "norm_pre_ffn", "w_ffn_up",
             "ffn_conv_w", "ffn_conv_b", "w_ffn_down", "norm_post_ffn")
    w_loc = dict(zip(names, (meta_tokens, norm_pre_mix, w_in, ssm_conv_w, ssm_conv_b, ssm_dt_bias, ssm_a_log, ssm_d_skip,
                             ssm_norm, w_ssm_out, attn_sinks, w_attn_out, w_mix_out, norm_post_mix, norm_pre_ffn, w_ffn_up,
                             ffn_conv_w, ffn_conv_b, w_ffn_down, norm_post_ffn)))
    m_loc = dict(zip(names, (m_meta_tokens, m_norm_pre_mix, m_w_in, m_ssm_conv_w, m_ssm_conv_b, m_ssm_dt_bias, m_ssm_a_log,
                             m_ssm_d_skip, m_ssm_norm, m_w_ssm_out, m_attn_sinks, m_w_attn_out, m_w_mix_out, m_norm_post_mix,
                             m_norm_pre_ffn, m_w_ffn_up, m_ffn_conv_w, m_ffn_conv_b, m_w_ffn_down, m_norm_post_ffn)))
    v_loc = dict(zip(names, (v_meta_tokens, v_norm_pre_mix, v_w_in, v_ssm_conv_w, v_ssm_conv_b, v_ssm_dt_bias, v_ssm_a_log,
                             v_ssm_d_skip, v_ssm_norm, v_w_ssm_out, v_attn_sinks, v_w_attn_out, v_w_mix_out, v_norm_post_mix,
                             v_norm_pre_ffn, v_w_ffn_up, v_ffn_conv_w, v_ffn_conv_b, v_w_ffn_down, v_norm_post_ffn)))

    def local2d(d, name):
        a = d[name]
        return a if name == "meta_tokens" else a.reshape(a.shape[1:])

    def turned2d(d, name):
        a = jnp.swapaxes(d[name], 1, 2)
        return a.reshape(a.shape[1:])

    my_id = 4 * lax.axis_index("x") + 2 * lax.axis_index("y") + lax.axis_index("c")
    big = {name: (shape, axis) for name, shape, axis in BIG}

    def whole(name, g):
        return g.reshape(N_DEV * g.shape[1], g.shape[2])

    def key(name):
        return name + "_t" if big[name][1] == 1 else name

    by_rows = [name for name, _, axis in BIG if axis == 0]
    send_bf16 = dict(zip(by_rows, _to_bf16([local2d(w_loc, name) for name in by_rows])))
    for name, _, axis in BIG:
        if axis == 1:
            send_bf16[name] = turned2d(w_loc, name).astype(BF16)
    small_shard_pack = jnp.concatenate([_as_rows(local2d(w_loc, name).reshape(-1)) for name, _ in SMALL_SHARDED], axis=0)
    small_shard_pack = jnp.pad(small_shard_pack, ((0, -small_shard_pack.shape[0] % 8), (0, 0)))
    first = _all_gather([send_bf16["w_in"], small_shard_pack])
    rest_names = [name for name, _, _ in BIG if name != "w_in"]
    rest = [send_bf16[name] for name in rest_names]
    rest, first = lax.optimization_barrier((rest, first))
    rest_handle = _push_start(rest, False, "gather_rest_start")
    wt = {"w_in_t": whole("w_in", first[0])}
    row = 0
    for name, (r, c) in SMALL_SHARDED:
        blocks = first[1][:, row:row + _rows_of(r * c)].reshape(N_DEV, -1)[:, :r * c].reshape(N_DEV, r, c)
        wt[name] = blocks.transpose(1, 0, 2).reshape(r, N_DEV * c)
        row += _rows_of(r * c)
    for name, size in SMALL_REPLICATED:
        wt[name] = w_loc[name].reshape(1, size)

    def late_weights(after):
        own, landed = _push_wait(rest_handle, after, "gather_rest_wait")
        out = {}
        for name, mine, land in zip(rest_names, own, landed):
            out[key(name)] = whole(name, lax.dynamic_update_index_in_dim(land, mine, my_id, 0))
        return out

    sent = {}

    def on_grad(known_as, g):
        name = known_as.removesuffix("_t")
        by_owner = g.reshape(N_DEV, g.shape[0] // N_DEV, g.shape[1])
        sent[name] = _push_start([by_owner], True, "send_" + name)
        return sent[name]["token"]

    loss_part, grad_x, grads = _local_step(x[0], loss_target[0], wt, late_weights, on_grad, rest_handle["token"])

    small_parts = []
    for name, (r, c) in SMALL_SHARDED:
        small_parts.append(_as_rows(_shard_major(grads[name], (r, c), 1).reshape(N_DEV, r * c)))
    rep_rows = jnp.concatenate([_as_rows(grads[name].reshape(-1)) for name, _ in SMALL_REPLICATED], axis=0)
    small_send = jnp.concatenate([jnp.broadcast_to(rep_rows[None], (N_DEV,) + rep_rows.shape)] + small_parts, axis=1)
    small_send = jnp.pad(small_send, ((0, 0), (0, -small_send.shape[1] % 8), (0, 0)))
    loss_tile = jnp.broadcast_to(jnp.pad(loss_part.reshape(1, 1, 1), ((0, 0), (0, 7), (0, 127))), (N_DEV, 8, 128))
    small_send = jnp.concatenate([small_send, loss_tile], axis=1)
    small_handle = _push_start([small_send], True, "send_small")

    def small_pack(d):
        pack = _pack_small({name: d[name] for name, _ in SMALL_REPLICATED}, {name: local2d(d, name) for name, _ in SMALL_SHARDED})
        return jnp.pad(pack, ((0, 8), (0, 0)))

    def arrived(handle, after, name):
        src, landed = _push_wait(handle, after, "arrived_" + name)
        return landed[0], lax.dynamic_index_in_dim(src[0], my_id, 0, keepdims=False)

    grad_w, delta_w, new_m, new_v = {}, {}, {}, {}
    outs = None
    after = small_handle["token"]
    for name, handle in sent.items():
        if name == "w_in":
            parts, own = arrived(small_handle, after, "small")
            outs = _adamw(parts, own, small_pack(w_loc), small_pack(m_loc), small_pack(v_loc), "adamw_small")
            after = outs[0]
        parts, own = arrived(handle, after, name)
        turned = big[name][1] == 1
        unit_rows = turned and big[name][0][1] % 8 != 0
        if unit_rows:
            state = [jnp.transpose(d[name], (2, 0, 1)) for d in (w_loc, m_loc, v_loc)]
        else:
            state = [turned2d(d, name) if turned else local2d(d, name) for d in (w_loc, m_loc, v_loc)]
        results = _adamw(parts, own, *state, "adamw_" + name)
        after = results[0]
        full = (1,) + big[name][0]
        for dst, a in zip((grad_w, delta_w, new_m, new_v), results):
            if unit_rows:
                dst[name] = jnp.transpose(a, (1, 2, 0))
            else:
                dst[name] = jnp.swapaxes(a[None], 1, 2) if turned else a.reshape(full)
    for dst, packed in zip((grad_w, delta_w, new_m, new_v), outs):
        for name, a in _unpack_small(packed).items():
            dst[name] = a.reshape(w_loc[name].shape)
    loss = outs[0][-8, 0]

    return (loss, grad_x[None], *[grad_w[n] for n in names], *[delta_w[n] for n in names],
            *[new_m[n] for n in names], *[new_v[n] for n in names])
```

```python
import jax
import jax.numpy as jnp
from jax import lax
from jax.experimental import pallas as pl
from jax.experimental.pallas import tpu as pltpu

F32 = jnp.float32
BF16 = jnp.bfloat16

D_MODEL = 1024
N_META = 16
CHUNK = 128
META_PAD = CHUNK - N_META
D_INNER = 2048
HEAD_P = 64
SSM_HEADS = 32
SSM_GROUPS = 4
HEADS_PER_GROUP = SSM_HEADS // SSM_GROUPS
GROUP_W = HEADS_PER_GROUP * HEAD_P
D_STATE = 128
SSM_CONV = 4
CONV_DIM = D_INNER + 2 * SSM_GROUPS * D_STATE
ATTN_HEADS = 16
KV_HEADS = 4
ATTN_GROUP = ATTN_HEADS // KV_HEADS
DH = 64
KV_W = KV_HEADS * DH
FFN_DIM = 2816
FFN_CONV = 3
EPS = 1e-6
NEG = -1e30
N_DEV = 8
AXES = ("x", "y", "c")

OFF_Z, OFF_GATE, OFF_DT, OFF_Q, OFF_K, OFF_V, OFF_XBC = 0, 2048, 4096, 4608, 5632, 5888, 6144
N_INP = OFF_XBC + CONV_DIM
QKV_W = OFF_XBC - OFF_Q
CUT_Z, CUT_XBC, CUT_DT, CUT_Q, CUT_K, CUT_V, CUT_G = 0, 2048, 5120, 5152, 6176, 6432, 6688
N_IN = 8736

ADAM_LR, ADAM_B1, ADAM_B2, ADAM_EPS, ADAM_WD, ADAM_STEP = 0.001, 0.9, 0.999, 1e-08, 0.01, 10

VMEM_LIMIT = 56 * 1024 * 1024


def _params(n_grid):
    return pltpu.CompilerParams(dimension_semantics=("arbitrary",) * n_grid, vmem_limit_bytes=VMEM_LIMIT)


def _sds(shape, dtype):
    return jax.ShapeDtypeStruct(shape, dtype)


def _pick(n, prefs):
    for c in prefs:
        if n % c == 0:
            return c
    raise ValueError(f"no tile of {prefs} divides {n}")


def _row(tr, width, cb=0):
    return pl.BlockSpec((tr, width), lambda i: (i, cb))


def _row_rev(tr, width, nt, cb=0):
    return pl.BlockSpec((tr, width), lambda i: (nt - 1 - i, cb))


def _full(shape):
    return pl.BlockSpec(shape, lambda *_: (0,) * len(shape))


def _sigmoid(x):
    return 1.0 / (1.0 + jnp.exp(-x))


def _softplus(x):
    return jnp.maximum(x, 0.0) + jnp.log(1.0 + jnp.exp(-jnp.abs(x)))


def _rms(x):
    return lax.rsqrt(jnp.mean(x * x, axis=-1, keepdims=True) + EPS)


def _rms_bwd(x, r, w, dy):
    xh = x * r
    g = dy * w
    dx = r * (g - xh * jnp.mean(g * xh, axis=-1, keepdims=True))
    return dx, jnp.sum(dy * xh, axis=0, keepdims=True)


def _row_ids(shape, tile_index, tr):
    return tile_index * tr + lax.broadcasted_iota(jnp.int32, shape, 0)


HALO = 8
STRIP = 256
STRIP_BWD = 128


def _causal_taps(x, halo, first_step, taps):
    n = x.shape[0]

    @pl.when(first_step)
    def _():
        halo[...] = jnp.zeros_like(halo)

    before = halo[...]
    row = lax.broadcasted_iota(jnp.int32, before.shape, 0)
    shifted = [x]
    for s in range(1, taps):
        rolled = pltpu.roll(x, s, 0)
        head = jnp.where(row < s, pltpu.roll(before, s, 0), rolled[0:HALO, :])
        shifted.append(jnp.concatenate([head, rolled[HALO:, :]], axis=0))
    halo[...] = x[n - HALO:, :]
    return shifted


def _anticausal_taps(x, halo, first_step, taps):
    n = x.shape[0]

    @pl.when(first_step)
    def _():
        halo[...] = jnp.zeros_like(halo)

    after = halo[...]
    row = lax.broadcasted_iota(jnp.int32, after.shape, 0)
    shifted = [x]
    for s in range(1, taps):
        rolled = pltpu.roll(x, n - s, 0)
        tail = jnp.where(row >= HALO - s, pltpu.roll(after, HALO - s, 0), rolled[n - HALO:, :])
        shifted.append(jnp.concatenate([rolled[:n - HALO, :], tail], axis=0))
    halo[...] = x[0:HALO, :]
    return shifted


def _matmul(a, b, *, ta=False, tb=False, out_dtype=F32, name, after=None):
    if ta:
        k_dim, m_dim = a.shape
    else:
        m_dim, k_dim = a.shape
    n_dim = b.shape[0] if tb else b.shape[1]
    tm = _pick(m_dim, (1408, 1024, 768, 512, 384, 256, 128))
    tn = _pick(n_dim, (1024, 1408, 768, 512, 384, 256, 128))
    if ta:
        tk = _pick(k_dim, (2112, 1408, 1024, 768, 512, 384, 256, 128))
    else:
        tk = k_dim if k_dim <= 3072 else _pick(k_dim, (3072, 2816, 2048, 1024))
    nk = k_dim // tk
    dims = (((0 if ta else 1,), (1 if tb else 0,)), ((), ()))

    use_acc = nk > 1 and out_dtype != F32

    def body(a_ref, b_ref, *rest):
        o_ref = rest[-2] if use_acc else rest[-1]
        acc_ref = rest[-1] if use_acc else o_ref
        def product():
            return lax.dot_general(a_ref[...].astype(BF16), b_ref[...].astype(BF16), dims, preferred_element_type=F32)

        if nk == 1:
            o_ref[...] = product().astype(o_ref.dtype)
        else:
            k = pl.program_id(2)

            @pl.when(k == 0)
            def _():
                acc_ref[...] = jnp.zeros_like(acc_ref)

            acc_ref[...] += product()

            if use_acc:
                @pl.when(k == nk - 1)
                def _():
                    o_ref[...] = acc_ref[...].astype(o_ref.dtype)

    a_spec = pl.BlockSpec((tk, tm), lambda i, j, k: (k, i)) if ta else pl.BlockSpec((tm, tk), lambda i, j, k: (i, k))
    b_spec = pl.BlockSpec((tn, tk), lambda i, j, k: (j, k)) if tb else pl.BlockSpec((tk, tn), lambda i, j, k: (k, j))
    extra_specs, extra = ([], ()) if after is None else ([pl.BlockSpec(memory_space=pl.ANY)], (after,))
    return pl.pallas_call(
        body, grid=(m_dim // tm, n_dim // tn, nk), in_specs=[a_spec, b_spec] + extra_specs,
        out_specs=pl.BlockSpec((tm, tn), lambda i, j, k: (i, j)), out_shape=_sds((m_dim, n_dim), out_dtype),
        scratch_shapes=[pltpu.VMEM((tm, tn), F32)] if use_acc else [],
        name=name, compiler_params=_params(3))(a, b, *extra)


def _seq_rows(t_rows):
    return 384 if t_rows % 384 == 0 and t_rows >= 768 else CHUNK


def _token_rows(tr):
    if tr == CHUNK:
        return pl.BlockSpec((CHUNK, D_MODEL), lambda i: (jnp.maximum(i - 1, 0), 0))
    return pl.BlockSpec((pl.Element(tr), pl.Element(D_MODEL)),
                        lambda i: (pl.multiple_of(jnp.maximum(i * tr - CHUNK, 0), CHUNK), 0))


def _under_tile(rows_ref, head, i):
    rows = rows_ref[...]
    tr = rows.shape[0]
    first = head if tr == CHUNK else jnp.concatenate([head, rows[0:tr - CHUNK, :]], axis=0)
    return jnp.where(i == 0, first, rows)


def _seq_specs(tr=CHUNK):
    return [_token_rows(tr), _full((N_META, D_MODEL))]


def _seq_tile(x_ref, meta_ref, i):
    return _under_tile(x_ref, jnp.concatenate([jnp.zeros((META_PAD, D_MODEL), F32), meta_ref[...]], axis=0), i)


def _prenorm(x, meta, w):
    t_rows = x.shape[0] + CHUNK
    tr = _seq_rows(t_rows)

    def body(x_ref, meta_ref, w_ref, o_ref):
        h = _seq_tile(x_ref, meta_ref, pl.program_id(0))
        o_ref[...] = (h * _rms(h) * w_ref[...]).astype(BF16)

    return pl.pallas_call(body, grid=(t_rows // tr,), in_specs=_seq_specs(tr) + [_full((1, D_MODEL))],
                          out_specs=_row(tr, D_MODEL), out_shape=_sds((t_rows, D_MODEL), BF16),
                          name="prenorm", compiler_params=_params(1))(x, meta, w)


def _ssm_conv_fwd(proj, conv_w, conv_b):
    t_rows = proj.shape[0]
    tr = CHUNK

    def body(x_ref, w_ref, b_ref, xc_ref, xa_ref, hist):
        first = pl.program_id(0) == 0
        for c in range(0, CONV_DIM, STRIP):
            cols = slice(c, c + STRIP)
            acc = b_ref[:, cols]
            for s, moved in enumerate(_causal_taps(x_ref[:, cols], hist.at[:, cols], first, SSM_CONV)):
                acc = acc + w_ref[SSM_CONV - 1 - s:SSM_CONV - s, cols] * moved
            xc_ref[:, cols] = acc
            xa_ref[:, cols] = acc * _sigmoid(acc)

    return pl.pallas_call(
        body, grid=(t_rows // tr,),
        in_specs=[_row(tr, CONV_DIM, OFF_XBC // CONV_DIM), _full((SSM_CONV, CONV_DIM)), _full((1, CONV_DIM))],
        out_specs=[_row(tr, CONV_DIM), _row(tr, CONV_DIM)],
        out_shape=[_sds((t_rows, CONV_DIM), F32), _sds((t_rows, CONV_DIM), F32)],
        scratch_shapes=[pltpu.VMEM((HALO, CONV_DIM), F32)],
        name="ssm_conv_fwd", compiler_params=_params(1))(proj, conv_w, conv_b)


def _ssm_post(y, proj, w):
    t_rows = y.shape[0]
    tr = _pick(t_rows, (384, 128))

    def body(y_ref, z_ref, w_ref, o_ref):
        z = z_ref[...].astype(F32)
        yz = y_ref[...] * z * _sigmoid(z)
        o_ref[...] = (yz * _rms(yz) * w_ref[...]).astype(BF16)

    return pl.pallas_call(body, grid=(t_rows // tr,),
                          in_specs=[_row(tr, D_INNER), _row(tr, D_INNER, OFF_Z // D_INNER), _full((1, D_INNER))],
                          out_specs=_row(tr, D_INNER), out_shape=_sds((t_rows, D_INNER), BF16),
                          name="ssm_post", compiler_params=_params(1))(y, proj, w)


def _mix_fwd(proj, y_ssm, y_attn):
    t_rows = y_ssm.shape[0]
    tr = _pick(t_rows, (384, 128))

    def body(g_ref, ys_ref, ya_ref, o_ref):
        g = _sigmoid(g_ref[...].astype(F32))
        o_ref[...] = (g[:, :D_MODEL] * ys_ref[...] + g[:, D_MODEL:] * ya_ref[...]).astype(BF16)

    return pl.pallas_call(body, grid=(t_rows // tr,),
                          in_specs=[_row(tr, 2 * D_MODEL, OFF_GATE // (2 * D_MODEL)), _row(tr, D_MODEL),
                                    _row(tr, D_MODEL)],
                          out_specs=_row(tr, D_MODEL), out_shape=_sds((t_rows, D_MODEL), BF16),
                          name="mix_fwd", compiler_params=_params(1))(proj, y_ssm, y_attn)


def _postmix(x, meta, mix, w_post, w_pre):
    t_rows = mix.shape[0]
    tr = _seq_rows(t_rows)

    def body(x_ref, meta_ref, m_ref, wp_ref, wf_ref, h1_ref, hn_ref):
        m = m_ref[...]
        h1 = _seq_tile(x_ref, meta_ref, pl.program_id(0)) + m * _rms(m) * wp_ref[...]
        h1 = jnp.where(_row_ids(h1.shape, pl.program_id(0), tr) >= META_PAD, h1, 0.0)
        h1_ref[...] = h1
        hn_ref[...] = (h1 * _rms(h1) * wf_ref[...]).astype(BF16)

    return pl.pallas_call(body, grid=(t_rows // tr,),
                          in_specs=_seq_specs(tr) + [_row(tr, D_MODEL), _full((1, D_MODEL)), _full((1, D_MODEL))],
                          out_specs=[_row(tr, D_MODEL), _row(tr, D_MODEL)],
                          out_shape=[_sds((t_rows, D_MODEL), F32), _sds((t_rows, D_MODEL), BF16)],
                          name="postmix", compiler_params=_params(1))(x, meta, mix, w_post, w_pre)


def _ffn_act(up, conv_w, conv_b):
    t_rows = up.shape[0]
    tr = CHUNK
    width = 2 * FFN_DIM

    def body(up_ref, w_ref, b_ref, u_ref, act_ref, hist):
        first = pl.program_id(0) == 0
        for c in range(0, FFN_DIM, STRIP):
            halves = []
            for base in (0, FFN_DIM):
                cols = slice(base + c, base + c + STRIP)
                u = b_ref[:, cols]
                for s, moved in enumerate(_causal_taps(up_ref[:, cols].astype(F32), hist.at[:, cols], first, FFN_CONV)):
                    u = u + w_ref[FFN_CONV - 1 - s:FFN_CONV - s, cols] * moved
                u_ref[:, cols] = u.astype(BF16)
                halves.append(u)
            a, g = halves
            act_ref[:, c:c + STRIP] = (a * _sigmoid(a) * g).astype(BF16)

    return pl.pallas_call(
        body, grid=(t_rows // tr,), in_specs=[_row(tr, width), _full((FFN_CONV, width)), _full((1, width))],
        out_specs=[_row(tr, width), _row(tr, FFN_DIM)],
        out_shape=[_sds((t_rows, width), BF16), _sds((t_rows, FFN_DIM), BF16)],
        scratch_shapes=[pltpu.VMEM((HALO, width), F32)],
        name="ffn_act", compiler_params=_params(1))(up, conv_w, conv_b)


def _final(h1, f, target, w):
    t_rows = h1.shape[0]
    tr = _seq_rows(t_rows)

    def body(h1_ref, f_ref, t_ref, w_ref, df_ref, dy_ref, dw_ref, loss_ref):
        i = pl.program_id(0)

        @pl.when(i == 0)
        def _():
            dw_ref[...] = jnp.zeros_like(dw_ref)
            loss_ref[...] = jnp.zeros_like(loss_ref)

        f_val = f_ref[...]
        r = _rms(f_val)
        wv = w_ref[...]
        h2 = h1_ref[...] + f_val * r * wv
        tgt = _under_tile(t_ref, jnp.zeros((CHUNK, D_MODEL), F32), i)
        diff = jnp.where(_row_ids(h2.shape, i, tr) >= CHUNK, h2 - tgt, 0.0)
        loss_ref[...] += 0.5 * jnp.sum(diff * diff) * (1.0 / D_MODEL)
        dy = diff * (1.0 / D_MODEL)
        dy_ref[...] = dy
        df, dw = _rms_bwd(f_val, r, wv, dy)
        df_ref[...] = df.astype(BF16)
        dw_ref[...] += dw

    return pl.pallas_call(
        body, grid=(t_rows // tr,),
        in_specs=[_row(tr, D_MODEL), _row(tr, D_MODEL), _token_rows(tr), _full((1, D_MODEL))],
        out_specs=[_row(tr, D_MODEL), _row(tr, D_MODEL), _full((1, D_MODEL)), _full((1, 128))],
        out_shape=[_sds((t_rows, D_MODEL), BF16), _sds((t_rows, D_MODEL), F32), _sds((1, D_MODEL), F32), _sds((1, 128), F32)],
        name="final", compiler_params=_params(1))(h1, f, target, w)


def _ffn_act_bwd(u, up, dact, conv_w):
    t_rows = u.shape[0]
    tr = CHUNK
    nt = t_rows // tr
    width = 2 * FFN_DIM

    def body(u_ref, up_ref, da_ref, w_ref, dup_ref, dw_ref, db_ref, ahead):
        @pl.when(pl.program_id(0) == 0)
        def _():
            dw_ref[...] = jnp.zeros_like(dw_ref)
            db_ref[...] = jnp.zeros_like(db_ref)

        first = pl.program_id(0) == 0
        for c in range(0, FFN_DIM, STRIP_BWD):
            ca, cg = slice(c, c + STRIP_BWD), slice(FFN_DIM + c, FFN_DIM + c + STRIP_BWD)
            a, g, d = u_ref[:, ca].astype(F32), u_ref[:, cg].astype(F32), da_ref[:, ca].astype(F32)
            s = _sigmoid(a)
            for cols, du in ((ca, d * g * s * (1.0 + a * (1.0 - s))), (cg, d * a * s)):
                x = up_ref[:, cols].astype(F32)
                dup = None
                for sh, moved in enumerate(_anticausal_taps(du, ahead.at[:, cols], first, FFN_CONV)):
                    k = FFN_CONV - 1 - sh
                    term = w_ref[k:k + 1, cols] * moved
                    dup = term if dup is None else dup + term
                    dw_ref[k:k + 1, cols] += jnp.sum(moved * x, axis=0, keepdims=True)
                db_ref[:, cols] += jnp.sum(du, axis=0, keepdims=True)
                dup_ref[:, cols] = dup.astype(BF16)

    return pl.pallas_call(
        body, grid=(nt,),
        in_specs=[_row_rev(tr, width, nt), _row_rev(tr, width, nt), _row_rev(tr, FFN_DIM, nt), _full((FFN_CONV, width))],
        out_specs=[_row_rev(tr, width, nt), _full((FFN_CONV, width)), _full((1, width))],
        out_shape=[_sds((t_rows, width), BF16), _sds((FFN_CONV, width), F32), _sds((1, width), F32)],
        scratch_shapes=[pltpu.VMEM((HALO, width), F32)],
        name="ffn_act_bwd", compiler_params=_params(1))(u, up, dact, conv_w)


def _postmix_bwd(h1, dhn2, dy, mix, w_pre, w_post):
    t_rows = h1.shape[0]
    tr = _pick(t_rows, (384, 128))

    def body(h1_ref, dhn_ref, dy_ref, m_ref, wf_ref, wp_ref, dmix_ref, dh_ref, dwf_ref, dwp_ref):
        @pl.when(pl.program_id(0) == 0)
        def _():
            dwf_ref[...] = jnp.zeros_like(dwf_ref)
            dwp_ref[...] = jnp.zeros_like(dwp_ref)

        h1v = h1_ref[...]
        dx, dwf = _rms_bwd(h1v, _rms(h1v), wf_ref[...], dhn_ref[...])
        dwf_ref[...] += dwf
        dh1 = dy_ref[...] + dx
        dh1 = jnp.where(_row_ids(dh1.shape, pl.program_id(0), tr) >= META_PAD, dh1, 0.0)
        dh_ref[...] = dh1
        m = m_ref[...]
        dmix, dwp = _rms_bwd(m, _rms(m), wp_ref[...], dh1)
        dwp_ref[...] += dwp
        dmix_ref[...] = dmix.astype(BF16)

    return pl.pallas_call(
        body, grid=(t_rows // tr,),
        in_specs=[_row(tr, D_MODEL) for _ in range(4)] + [_full((1, D_MODEL))] * 2,
        out_specs=[_row(tr, D_MODEL), _row(tr, D_MODEL), _full((1, D_MODEL)), _full((1, D_MODEL))],
        out_shape=[_sds((t_rows, D_MODEL), BF16), _sds((t_rows, D_MODEL), F32), _sds((1, D_MODEL), F32), _sds((1, D_MODEL), F32)],
        name="postmix_bwd", compiler_params=_params(1))(h1, dhn2, dy, mix, w_pre, w_post)


_ANY = pl.BlockSpec(memory_space=pl.ANY)


def _mix_bwd(dmixed, proj, y_ssm, y_attn, dproj):
    t_rows = dmixed.shape[0]
    tr = _pick(t_rows, (384, 128))

    def body(d_ref, g_ref, ys_ref, ya_ref, _, dys_ref, dya_ref, dg_ref):
        d = d_ref[...]
        g = _sigmoid(g_ref[...].astype(F32))
        g1, g2 = g[:, :D_MODEL], g[:, D_MODEL:]
        dys_ref[...] = (d * g1).astype(BF16)
        dya_ref[...] = (d * g2).astype(BF16)
        dg_ref[...] = jnp.concatenate([d * ys_ref[...] * g1 * (1.0 - g1), d * ya_ref[...] * g2 * (1.0 - g2)],
                                      axis=1).astype(BF16)

    return pl.pallas_call(
        body, grid=(t_rows // tr,),
        in_specs=[_row(tr, D_MODEL), _row(tr, 2 * D_MODEL, OFF_GATE // (2 * D_MODEL)), _row(tr, D_MODEL), _row(tr, D_MODEL),
                  _ANY],
        out_specs=[_row(tr, D_MODEL), _row(tr, D_MODEL), _row(tr, 2 * D_MODEL, OFF_GATE // (2 * D_MODEL))],
        out_shape=[_sds((t_rows, D_MODEL), BF16), _sds((t_rows, D_MODEL), BF16), _sds(dproj.shape, dproj.dtype)],
        input_output_aliases={4: 2},
        name="mix_bwd", compiler_params=_params(1))(dmixed, proj, y_ssm, y_attn, dproj)


def _ssm_post_bwd(y, proj, dyn, w, dproj):
    t_rows = y.shape[0]
    tr = CHUNK

    def body(y_ref, z_ref, d_ref, w_ref, _, dy_ref, dz_ref, dw_ref):
        @pl.when(pl.program_id(0) == 0)
        def _():
            dw_ref[...] = jnp.zeros_like(dw_ref)

        yv, z = y_ref[...], z_ref[...].astype(F32)
        sz = _sigmoid(z)
        silu = z * sz
        yz = yv * silu
        dyz, dw = _rms_bwd(yz, _rms(yz), w_ref[...], d_ref[...].astype(F32))
        dw_ref[...] += dw
        dy_ref[...] = dyz * silu
        dz_ref[...] = (dyz * yv * sz * (1.0 + z * (1.0 - sz))).astype(BF16)

    return pl.pallas_call(
        body, grid=(t_rows // tr,),
        in_specs=[_row(tr, D_INNER), _row(tr, D_INNER, OFF_Z // D_INNER), _row(tr, D_INNER), _full((1, D_INNER)), _ANY],
        out_specs=[_row(tr, D_INNER), _row(tr, D_INNER, OFF_Z // D_INNER), _full((1, D_INNER))],
        out_shape=[_sds((t_rows, D_INNER), F32), _sds(dproj.shape, dproj.dtype), _sds((1, D_INNER), F32)],
        input_output_aliases={4: 1},
        name="ssm_post_bwd", compiler_params=_params(1))(y, proj, dyn, w, dproj)


def _ssm_conv_bwd(xc, proj, dxs, dbm, dcm, conv_w, dproj):
    t_rows = xc.shape[0]
    tr = CHUNK
    nt = t_rows // tr
    bc_w = SSM_GROUPS * D_STATE

    def body(xc_ref, x_ref, dxs_ref, db_ref, dc_ref, w_ref, _, dx_ref, dw_ref, dbias_ref, ahead):
        first = pl.program_id(0) == 0

        @pl.when(first)
        def _():
            dw_ref[...] = jnp.zeros_like(dw_ref)
            dbias_ref[...] = jnp.zeros_like(dbias_ref)

        for c0 in range(0, CONV_DIM, STRIP_BWD):
            cols = slice(c0, c0 + STRIP_BWD)
            if c0 < D_INNER:
                dact = dxs_ref[:, cols]
            elif c0 < D_INNER + bc_w:
                dact = db_ref[:, c0 - D_INNER:c0 - D_INNER + STRIP_BWD]
            else:
                dact = dc_ref[:, c0 - D_INNER - bc_w:c0 - D_INNER - bc_w + STRIP_BWD]
            c = xc_ref[:, cols]
            s = _sigmoid(c)
            dpre = dact * s * (1.0 + c * (1.0 - s))
            x = x_ref[:, cols]
            dx = None
            for sh, moved in enumerate(_anticausal_taps(dpre, ahead.at[:, cols], first, SSM_CONV)):
                k = SSM_CONV - 1 - sh
                term = w_ref[k:k + 1, cols] * moved
                dx = term if dx is None else dx + term
                dw_ref[k:k + 1, cols] += jnp.sum(moved * x, axis=0, keepdims=True)
            dbias_ref[:, cols] += jnp.sum(dpre, axis=0, keepdims=True)
            dx_ref[:, cols] = dx.astype(BF16)

    xbc_block = OFF_XBC // CONV_DIM
    return pl.pallas_call(
        body, grid=(nt,),
        in_specs=[_row_rev(tr, CONV_DIM, nt), _row_rev(tr, CONV_DIM, nt, xbc_block), _row_rev(tr, D_INNER, nt),
                  _row_rev(tr, bc_w, nt), _row_rev(tr, bc_w, nt), _full((SSM_CONV, CONV_DIM)), _ANY],
        out_specs=[_row_rev(tr, CONV_DIM, nt, xbc_block), _full((SSM_CONV, CONV_DIM)), _full((1, CONV_DIM))],
        out_shape=[_sds(dproj.shape, dproj.dtype), _sds((SSM_CONV, CONV_DIM), F32), _sds((1, CONV_DIM), F32)],
        scratch_shapes=[pltpu.VMEM((HALO, CONV_DIM), F32)],
        input_output_aliases={6: 0},
        name="ssm_conv_bwd", compiler_params=_params(1))(xc, proj, dxs, dbm, dcm, conv_w, dproj)


def _prenorm_bwd(x, meta, dhn, dh, w):
    seq = x.shape[0]
    tr = _pick(seq, (512, 128))

    def body(x_ref, meta_ref, d_ref, r_ref, d0_ref, r0_ref, w_ref, dx_ref, dmeta_ref, dw_ref):
        wv = w_ref[...]

        @pl.when(pl.program_id(0) == 0)
        def _():
            h0 = jnp.concatenate([jnp.zeros((META_PAD, D_MODEL), F32), meta_ref[...]], axis=0)
            dx0, dw0 = _rms_bwd(h0, _rms(h0), wv, d0_ref[...])
            dw_ref[...] = dw0
            dmeta_ref[...] = (r0_ref[...] + dx0)[META_PAD:, :]

        h = x_ref[...]
        dx, dw = _rms_bwd(h, _rms(h), wv, d_ref[...])
        dw_ref[...] += dw
        dx_ref[...] = r_ref[...] + dx

    def shifted(tile):
        return pl.BlockSpec((pl.Element(tile), pl.Element(D_MODEL)), lambda i: (pl.multiple_of(i * tile + CHUNK, CHUNK), 0))

    first = pl.BlockSpec((CHUNK, D_MODEL), lambda i: (0, 0))
    return pl.pallas_call(
        body, grid=(seq // tr,),
        in_specs=[_row(tr, D_MODEL), _full((N_META, D_MODEL)), shifted(tr), shifted(tr), first, first, _full((1, D_MODEL))],
        out_specs=[_row(tr, D_MODEL), _full((N_META, D_MODEL)), _full((1, D_MODEL))],
        out_shape=[_sds((seq, D_MODEL), F32), _sds((N_META, D_MODEL), F32), _sds((1, D_MODEL), F32)],
        name="prenorm_bwd", compiler_params=_params(1))(x, meta, dhn, dh, dhn, dh, w)


def _dot01(x, m01, x_left, parts):
    acc, rest = None, x
    for i in range(parts):
        piece = rest.astype(BF16)
        term = (jnp.dot(piece, m01, preferred_element_type=F32) if x_left
                else jnp.dot(m01, piece, preferred_element_type=F32))
        acc = term if acc is None else acc + term
        if i + 1 < parts:
            rest = rest - piece.astype(F32)
    return acc


def _ssd_common(dtr_ref, dt_bias, a_log, chunk_index):
    rows = lax.broadcasted_iota(jnp.int32, (CHUNK, CHUNK), 0)
    cols = lax.broadcasted_iota(jnp.int32, (CHUNK, CHUNK), 1)
    low = rows >= cols
    raw = dtr_ref[:, :128]
    for g in range(1, SSM_GROUPS):
        raw = raw + pltpu.roll(dtr_ref[:, g * 128:(g + 1) * 128], HEADS_PER_GROUP * g, 1)
    raw = raw + dt_bias
    live = _row_ids(raw.shape, chunk_index, CHUNK) >= META_PAD
    dt = jnp.where(live, _softplus(raw), 0.0)
    a_head = -jnp.exp(a_log)
    cs = _dot01(dt * a_head, low.astype(BF16), False, 3)
    return dict(low=low, triu=(rows <= cols).astype(BF16), raw=raw, live=live, dt=dt, a_head=a_head, cs=cs, cs_t=cs.T,
                grow=jnp.exp(cs),
                fade=jnp.exp(cs[CHUNK - 1:CHUNK, :] - cs))


def _ssd_expand(cm, g):
    first = HEADS_PER_GROUP * g
    expand = (lax.broadcasted_iota(jnp.int32, (CHUNK, GROUP_W), 1) // HEAD_P + first
              == lax.broadcasted_iota(jnp.int32, (CHUNK, GROUP_W), 0)).astype(BF16)
    fold = (lax.broadcasted_iota(jnp.int32, (GROUP_W, CHUNK), 0) // HEAD_P + first
            == lax.broadcasted_iota(jnp.int32, (GROUP_W, CHUNK), 1)).astype(BF16)
    return dict(fold=fold, dtx=_dot01(cm["dt"], expand, True, 2), growx=_dot01(cm["grow"], expand, True, 2),
                fadex=_dot01(cm["fade"], expand, True, 2))


def _decay_matrix(cm, j):
    diff = cm["cs"][:, j:j + 1] - cm["cs_t"][j:j + 1, :]
    return jnp.where(cm["low"], jnp.exp(jnp.where(cm["low"], diff, 0.0)), 0.0)


def _dot(a, b, dims):
    return lax.dot_general(a.astype(BF16), b.astype(BF16), (dims, ((), ())), preferred_element_type=F32)


def _dot_fine(a, b, dims):
    a_hi, b_hi = a.astype(BF16), b.astype(BF16)
    a_lo, b_lo = (a - a_hi.astype(F32)).astype(BF16), (b - b_hi.astype(F32)).astype(BF16)
    dn = (dims, ((), ()))
    return (lax.dot_general(a_hi, b_hi, dn, preferred_element_type=F32)
            + lax.dot_general(a_hi, b_lo, dn, preferred_element_type=F32)
            + lax.dot_general(a_lo, b_hi, dn, preferred_element_type=F32))


def _ssd_specs(nt, rev):
    def idx(c):
        return nt - 1 - c if rev else c
    bc_w = SSM_GROUPS * D_STATE
    xs = pl.BlockSpec((CHUNK, D_INNER), lambda c: (idx(c), 0))
    bm = pl.BlockSpec((CHUNK, bc_w), lambda c: (idx(c), D_INNER // bc_w))
    cm = pl.BlockSpec((CHUNK, bc_w), lambda c: (idx(c), D_INNER // bc_w + 1))
    dtr = pl.BlockSpec((CHUNK, SSM_GROUPS * 128), lambda c: (idx(c), OFF_DT // (SSM_GROUPS * 128)))
    par = _full((1, 128))
    par_x = _full((SSM_GROUPS, 1, GROUP_W))
    return xs, bm, cm, dtr, par, par_x, idx


def _group_cols(g, width):
    return slice(g * width, (g + 1) * width)


def _ssd_fwd(xact, proj, dtb, alog, dskip_x):
    t_rows = xact.shape[0]
    nt = t_rows // CHUNK
    xs_spec, b_spec, c_spec, dtr_spec, par, par_x, _ = _ssd_specs(nt, False)

    def body(xs_ref, b_ref, c_ref, dtr_ref, dtb_ref, alog_ref, dsk_ref, y_ref, hst_ref, state):
        c = pl.program_id(0)

        @pl.when(c == 0)
        def _():
            state[...] = jnp.zeros_like(state)

        cm = _ssd_common(dtr_ref, dtb_ref[...], alog_ref[...], c)
        for g in range(SSM_GROUPS):
            wide, narrow = _group_cols(g, GROUP_W), _group_cols(g, D_STATE)
            ex = _ssd_expand(cm, g)
            xs, bm, cmat = xs_ref[:, wide], b_ref[:, narrow], c_ref[:, narrow]
            x_dt = xs * ex["dtx"]
            h_in = state[g]
            hst_ref[0, g] = h_in
            y_ref[:, wide] = _dot(cmat, h_in, ((1,), (0,))) * ex["growx"] + xs * dsk_ref[g]
            cb = _dot(cmat, bm, ((1,), (1,)))
            for j in range(HEADS_PER_GROUP):
                sl = slice(g * GROUP_W + j * HEAD_P, g * GROUP_W + (j + 1) * HEAD_P)
                decay = _decay_matrix(cm, HEADS_PER_GROUP * g + j)
                y_ref[:, sl] += _dot(cb * decay, x_dt[:, j * HEAD_P:(j + 1) * HEAD_P], ((1,), (0,)))
            state[g] = h_in * ex["growx"][CHUNK - 1:CHUNK, :] + _dot_fine(bm, x_dt * ex["fadex"], ((0,), (0,)))

    return pl.pallas_call(
        body, grid=(nt,),
        in_specs=[xs_spec, b_spec, c_spec, dtr_spec, par, par, par_x],
        out_specs=[xs_spec, pl.BlockSpec((1, SSM_GROUPS, D_STATE, GROUP_W), lambda c: (c, 0, 0, 0))],
        out_shape=[_sds((t_rows, D_INNER), F32), _sds((nt, SSM_GROUPS, D_STATE, GROUP_W), F32)],
        scratch_shapes=[pltpu.VMEM((SSM_GROUPS, D_STATE, GROUP_W), F32)],
        name="ssd_fwd", compiler_params=_params(1))(xact, xact, xact, proj, dtb, alog, dskip_x)


def _ssd_bwd(xact, proj, dtb, alog, dskip_x, dy, hst, dproj):
    t_rows = xact.shape[0]
    nt = t_rows // CHUNK
    xs_spec, b_spec, c_spec, dtr_spec, par, par_x, idx = _ssd_specs(nt, True)
    h_spec = pl.BlockSpec((1, SSM_GROUPS, D_STATE, GROUP_W), lambda c: (idx(c), 0, 0, 0))
    hn_spec = pl.BlockSpec((1, SSM_GROUPS, D_STATE, GROUP_W), lambda c: (jnp.minimum(idx(c) + 1, nt - 1), 0, 0, 0))
    bc_out = pl.BlockSpec((CHUNK, SSM_GROUPS * D_STATE), lambda c: (idx(c), 0))

    def body(xs_ref, b_ref, c_ref, dtr_ref, dtb_ref, alog_ref, dsk_ref, dy_ref, h_ref, hn_ref, _,
             dxs_ref, db_ref, dc_ref, ddt_ref, dalog_ref, ddtb_ref, dd_ref, dstate, dx_buf):
        step = pl.program_id(0)

        @pl.when(step == 0)
        def _():
            dstate[...] = jnp.zeros_like(dstate)
            dalog_ref[...] = jnp.zeros_like(dalog_ref)
            ddtb_ref[...] = jnp.zeros_like(ddtb_ref)
            dd_ref[...] = jnp.zeros_like(dd_ref)

        cm = _ssd_common(dtr_ref, dtb_ref[...], alog_ref[...], idx(step))
        for g in range(SSM_GROUPS):
            _ssd_bwd_group(g, cm, xs_ref, b_ref, c_ref, dsk_ref, dy_ref, h_ref, hn_ref,
                           dxs_ref, db_ref, dc_ref, ddt_ref, dalog_ref, ddtb_ref, dd_ref, dstate, dx_buf)

    return pl.pallas_call(
        body, grid=(nt,),
        in_specs=[xs_spec, b_spec, c_spec, dtr_spec, par, par, par_x, xs_spec, h_spec, hn_spec, _ANY],
        out_specs=[xs_spec, bc_out, bc_out, dtr_spec, par, par, par_x],
        out_shape=[_sds((t_rows, D_INNER), F32), _sds((t_rows, SSM_GROUPS * D_STATE), F32),
                   _sds((t_rows, SSM_GROUPS * D_STATE), F32), _sds(dproj.shape, dproj.dtype),
                   _sds((1, 128), F32), _sds((1, 128), F32), _sds((SSM_GROUPS, 1, GROUP_W), F32)],
        scratch_shapes=[pltpu.VMEM((SSM_GROUPS, D_STATE, GROUP_W), F32), pltpu.VMEM((CHUNK, GROUP_W), F32)],
        input_output_aliases={10: 3},
        name="ssd_bwd", compiler_params=_params(1))(xact, xact, xact, proj, dtb, alog, dskip_x, dy, hst, hst, dproj)


def _ssd_bwd_group(g, cm, xs_ref, b_ref, c_ref, dsk_ref, dy_ref, h_ref, hn_ref,
                   dxs_ref, db_ref, dc_ref, ddt_ref, dalog_ref, ddtb_ref, dd_ref, dstate, dx_buf):
    wide, narrow = _group_cols(g, GROUP_W), _group_cols(g, D_STATE)
    first = HEADS_PER_GROUP * g
    ex = _ssd_expand(cm, g)
    xs, bm, cmat = xs_ref[:, wide], b_ref[:, narrow], c_ref[:, narrow]
    dsk = dsk_ref[g]
    x_dt = xs * ex["dtx"]
    h_in, h_next = h_ref[0, g], hn_ref[0, g]
    dyv = dy_ref[:, wide]
    dh = dstate[g]
    grow, fade = ex["growx"], ex["fadex"]
    dy_grow = dyv * grow
    x_fade = x_dt * fade
    cb = _dot(cmat, bm, ((1,), (1,)))
    ml = jnp.zeros((CHUNK, CHUNK), F32)
    row_id = lax.broadcasted_iota(jnp.int32, (CHUNK, CHUNK), 0)
    col_id = lax.broadcasted_iota(jnp.int32, (CHUNK, CHUNK), 1)
    w_rows = jnp.zeros((CHUNK, CHUNK), F32)
    w_cols = jnp.zeros((CHUNK, CHUNK), F32)
    for j in range(HEADS_PER_GROUP):
        sl = slice(j * HEAD_P, (j + 1) * HEAD_P)
        lm = _decay_matrix(cm, first + j)
        mlj = _dot(dyv[:, sl], x_dt[:, sl], ((1,), (1,))) * lm
        ml = ml + mlj
        wm = mlj * cb
        w_rows = jnp.where(col_id == first + j, jnp.sum(wm, axis=1, keepdims=True), w_rows)
        w_cols = jnp.where(row_id == first + j, jnp.sum(wm, axis=0, keepdims=True), w_cols)
        dx_buf[:, sl] = _dot(cb * lm, dyv[:, sl], ((0,), (0,)))
    dx_off = fade * _dot_fine(bm, dh, ((1,), (0,)))
    dx = dx_buf[...] + dx_off
    dc_ref[:, narrow] = _dot(ml, bm, ((1,), (0,))) + _dot(dy_grow, h_in, ((1,), (1,)))
    db_ref[:, narrow] = _dot(ml, cmat, ((0,), (0,))) + _dot(x_fade, dh, ((1,), (1,)))
    fold = ex["fold"]
    y_off = _dot_fine(cmat, h_in, ((1,), (0,))) * grow
    dcs = (w_rows - w_cols.T) + _dot01(dyv * y_off - x_dt * dx_off, fold, True, 2)
    tail = jnp.broadcast_to(jnp.sum(dh * h_next, axis=0, keepdims=True), (8, GROUP_W))
    tail = _dot01(tail, fold, True, 2)[0:1, :]
    last_row = lax.broadcasted_iota(jnp.int32, (CHUNK, 128), 0) == CHUNK - 1
    dcs = dcs + jnp.where(last_row, tail, 0.0)
    da = _dot01(dcs, cm["triu"], False, 3)
    ddt = da * cm["a_head"] + _dot01(dx * xs, fold, True, 2)
    ddt_raw = jnp.where(cm["live"], ddt * _sigmoid(cm["raw"]), 0.0)
    ddt_ref[:, narrow] = (ddt_raw if g == 0 else pltpu.roll(ddt_raw, 128 - first, 1)).astype(BF16)
    ddtb_ref[...] += jnp.sum(ddt_raw, axis=0, keepdims=True)
    dalog_ref[...] += jnp.sum(da * cm["dt"], axis=0, keepdims=True) * cm["a_head"]
    dd_ref[g] += jnp.sum(dyv * xs, axis=0, keepdims=True)
    dxs_ref[:, wide] = dx * ex["dtx"] + dyv * dsk
    dstate[g] = dh * grow[CHUNK - 1:CHUNK, :] + _dot_fine(cmat, dy_grow, ((0,), (0,)))


def _swa_bias():
    rows_q = ATTN_GROUP * CHUNK
    dist = (jnp.arange(rows_q) % CHUNK)[:, None] - jnp.arange(2 * CHUNK)[None, :] + CHUNK
    head = jnp.arange(KV_HEADS)[:, None] * ATTN_GROUP + jnp.arange(rows_q)[None, :] // CHUNK + 1
    slope = jnp.exp2(-8.0 * head.astype(F32) / ATTN_HEADS)
    return jnp.where((dist >= 0) & (dist < CHUNK), -slope[:, :, None] * dist.astype(F32)[None], NEG)


def _swa_probs(q_kv, k_prev, k_cur, k_first, sink, bias, n):
    rows_q = ATTN_GROUP * CHUNK
    qs = jnp.concatenate([q_kv[:, g * DH:(g + 1) * DH] for g in range(ATTN_GROUP)], axis=0) * (DH ** -0.5)
    kcat = jnp.concatenate([k_prev, k_cur], axis=0)
    kmeta = k_first[META_PAD:, :]
    key_ok = lax.broadcasted_iota(jnp.int32, (1, 2 * CHUNK), 1) + n * CHUNK >= 2 * CHUNK
    s_band = jnp.where(key_ok, _dot(qs, kcat, ((1,), (1,))) + bias, NEG)
    q_pos = lax.broadcasted_iota(jnp.int32, (rows_q, N_META), 0) % CHUNK + n * CHUNK - META_PAD
    ok_m = lax.broadcasted_iota(jnp.int32, (rows_q, N_META), 1) <= q_pos
    s_meta = jnp.where(ok_m, _dot(qs, kmeta, ((1,), (1,))), NEG)
    m = jnp.maximum(jnp.maximum(jnp.max(s_band, axis=1, keepdims=True), jnp.max(s_meta, axis=1, keepdims=True)), sink)
    p_band, p_meta, p_sink = jnp.exp(s_band - m), jnp.exp(s_meta - m), jnp.exp(sink - m)
    inv = 1.0 / (jnp.sum(p_band, axis=1, keepdims=True) + jnp.sum(p_meta, axis=1, keepdims=True) + p_sink)
    return qs, kcat, kmeta, p_band * inv, p_meta * inv, p_sink * inv


def _swa_specs(nt, rev):
    def idx(n):
        return nt - 1 - n if rev else n
    o = pl.BlockSpec((CHUNK, ATTN_HEADS * DH), lambda n: (idx(n), 0))
    chunks = (lambda c: jnp.maximum(c - 1, 0)), (lambda c: c), (lambda c: 0)
    qkv = [pl.BlockSpec((CHUNK, QKV_W), lambda n, f=f: (f(idx(n)), OFF_Q // QKV_W)) for f in chunks]
    sink = _full((KV_HEADS, ATTN_GROUP * CHUNK, 1))
    bias = _full((KV_HEADS, ATTN_GROUP * CHUNK, 2 * CHUNK))
    return o, qkv, sink, bias, idx


def _head_cols(k):
    kv_w = ATTN_GROUP * DH
    q0, k0, v0 = k * kv_w, OFF_K - OFF_Q + k * DH, OFF_V - OFF_Q + k * DH
    return slice(q0, q0 + kv_w), slice(k0, k0 + DH), slice(v0, v0 + DH)


def _swa_fwd(proj, sink_rows, bias):
    t_rows = proj.shape[0]
    nt = t_rows // CHUNK
    o_spec, qkv_specs, sink_spec, bias_spec, _ = _swa_specs(nt, False)
    kv_w = ATTN_GROUP * DH

    def body(prev_ref, cur_ref, first_ref, sink_ref, bias_ref, o_ref):
        n = pl.program_id(0)
        for k in range(KV_HEADS):
            qc, kc, vc = _head_cols(k)
            _, _, _, p_band, p_meta, _ = _swa_probs(cur_ref[:, qc], prev_ref[:, kc], cur_ref[:, kc], first_ref[:, kc],
                                                    sink_ref[k], bias_ref[k], n)
            vcat = jnp.concatenate([prev_ref[:, vc], cur_ref[:, vc]], axis=0)
            out = _dot(p_band, vcat, ((1,), (0,))) + _dot(p_meta, first_ref[:, vc][META_PAD:, :], ((1,), (0,)))
            for g in range(ATTN_GROUP):
                o_ref[:, k * kv_w + g * DH:k * kv_w + (g + 1) * DH] = out[g * CHUNK:(g + 1) * CHUNK, :]

    return pl.pallas_call(
        body, grid=(nt,), in_specs=qkv_specs + [sink_spec, bias_spec],
        out_specs=o_spec, out_shape=_sds((t_rows, ATTN_HEADS * DH), F32),
        name="swa_fwd", compiler_params=_params(1))(proj, proj, proj, sink_rows, bias)


def _swa_bwd(proj, sink_rows, bias, out, dout, dproj):
    t_rows = proj.shape[0]
    nt = t_rows // CHUNK
    o_spec, qkv_specs, sink_spec, bias_spec, idx = _swa_specs(nt, True)
    kv_w = ATTN_GROUP * DH
    k_off, v_off = OFF_K - OFF_Q, OFF_V - OFF_Q

    def body(prev_ref, cur_ref, first_ref, sink_ref, bias_ref, o_ref, do_ref, _, dqkv_ref, dsink_ref,
             carry_k, carry_v, meta_k, meta_v, dqkv_buf):
        step = pl.program_id(0)
        n = idx(step)

        @pl.when(step == 0)
        def _():
            carry_k[...] = jnp.zeros_like(carry_k)
            carry_v[...] = jnp.zeros_like(carry_v)
            meta_k[...] = jnp.zeros_like(meta_k)
            meta_v[...] = jnp.zeros_like(meta_v)
            dsink_ref[...] = jnp.zeros_like(dsink_ref)

        for k in range(KV_HEADS):
            cols = slice(k * kv_w, (k + 1) * kv_w)
            hd = slice(k * DH, (k + 1) * DH)
            qc, kc, vc = _head_cols(k)
            qs, kcat, kmeta, p_band, p_meta, p_sink = _swa_probs(cur_ref[:, qc], prev_ref[:, kc], cur_ref[:, kc],
                                                                 first_ref[:, kc], sink_ref[k], bias_ref[k], n)
            vcat = jnp.concatenate([prev_ref[:, vc], cur_ref[:, vc]], axis=0)
            vmeta = first_ref[:, vc][META_PAD:, :]
            o, do = o_ref[:, cols], do_ref[:, cols]
            os_ = jnp.concatenate([o[:, g * DH:(g + 1) * DH] for g in range(ATTN_GROUP)], axis=0)
            dos = jnp.concatenate([do[:, g * DH:(g + 1) * DH] for g in range(ATTN_GROUP)], axis=0)
            delta = jnp.sum(dos * os_, axis=1, keepdims=True)
            ds_band = p_band * (_dot(dos, vcat, ((1,), (1,))) - delta)
            ds_meta = p_meta * (_dot(dos, vmeta, ((1,), (1,))) - delta)
            ds_sink = -p_sink * delta
            dqs = (_dot(ds_band, kcat, ((1,), (0,))) + _dot(ds_meta, kmeta, ((1,), (0,)))) * (DH ** -0.5)
            for g in range(ATTN_GROUP):
                dqkv_buf[:, k * kv_w + g * DH:k * kv_w + (g + 1) * DH] = dqs[g * CHUNK:(g + 1) * CHUNK, :]
                dsink_ref[k, g:g + 1, :] += jnp.sum(ds_sink[g * CHUNK:(g + 1) * CHUNK, :])
            dkcat = _dot(ds_band, qs, ((0,), (0,)))
            dvcat = _dot(p_band, dos, ((0,), (0,)))
            meta_k[:, hd] += _dot(ds_meta, qs, ((0,), (0,)))
            meta_v[:, hd] += _dot(p_meta, dos, ((0,), (0,)))
            dqkv_buf[:, kc] = dkcat[CHUNK:, :] + carry_k[:, hd]
            dqkv_buf[:, vc] = dvcat[CHUNK:, :] + carry_v[:, hd]
            carry_k[:, hd] = dkcat[:CHUNK, :]
            carry_v[:, hd] = dvcat[:CHUNK, :]

        @pl.when(n == 0)
        def _():
            dqkv_buf[META_PAD:, k_off:k_off + KV_W] += meta_k[...]
            dqkv_buf[META_PAD:, v_off:v_off + KV_W] += meta_v[...]

        dqkv_ref[...] = dqkv_buf[...].astype(BF16)

    return pl.pallas_call(
        body, grid=(nt,),
        in_specs=qkv_specs + [sink_spec, bias_spec, o_spec, o_spec, pl.BlockSpec(memory_space=pl.ANY)],
        out_specs=[qkv_specs[1], _full((KV_HEADS, 8, 128))],
        out_shape=[_sds(dproj.shape, dproj.dtype), _sds((KV_HEADS, 8, 128), F32)],
        scratch_shapes=[pltpu.VMEM((CHUNK, KV_W), F32), pltpu.VMEM((CHUNK, KV_W), F32),
                        pltpu.VMEM((N_META, KV_W), F32), pltpu.VMEM((N_META, KV_W), F32),
                        pltpu.VMEM((CHUNK, QKV_W), F32)],
        input_output_aliases={7: 0},
        name="swa_bwd", compiler_params=_params(1))(proj, proj, proj, sink_rows, bias, out, dout, dproj)


def _pack_w_in_t(w_in_t):
    w_dt = w_in_t[CUT_DT:CUT_Q].reshape(SSM_GROUPS, HEADS_PER_GROUP, D_MODEL)
    w_dt = jnp.pad(w_dt, ((0, 0), (0, 128 - HEADS_PER_GROUP), (0, 0))).reshape(SSM_GROUPS * 128, D_MODEL)
    return jnp.concatenate([w_in_t[CUT_Z:CUT_XBC], w_in_t[CUT_G:], w_dt, w_in_t[CUT_Q:CUT_G], w_in_t[CUT_XBC:CUT_DT]], axis=0)


def _unpack_w_in_t(wp_t):
    w_dt = wp_t[OFF_DT:OFF_Q].reshape(SSM_GROUPS, 128, D_MODEL)[:, :HEADS_PER_GROUP].reshape(SSM_HEADS, D_MODEL)
    return jnp.concatenate([wp_t[OFF_Z:OFF_GATE], wp_t[OFF_XBC:], w_dt, wp_t[OFF_Q:OFF_XBC], wp_t[OFF_GATE:OFF_DT]], axis=0)


def _head_lanes(v):
    return jnp.pad(v.reshape(1, SSM_HEADS), ((0, 0), (0, 128 - SSM_HEADS)))


def _local_step(x, target, wt, late_weights=None, on_grad=None, started=None):
    seq = x.shape[0]
    grads = {}

    def emit(name, g):
        grads[name] = g
        return None if on_grad is None else on_grad(name, g)
    meta = wt["meta_tokens"]
    wp_t = _pack_w_in_t(wt["w_in_t"])
    dtb = _head_lanes(wt["ssm_dt_bias"].reshape(-1))
    alog = _head_lanes(wt["ssm_a_log"].reshape(-1))
    dskip_x = jnp.repeat(wt["ssm_d_skip"].reshape(-1), HEAD_P).reshape(SSM_GROUPS, 1, GROUP_W)
    sink_rows = jnp.repeat(wt["attn_sinks"].reshape(KV_HEADS, ATTN_GROUP), CHUNK, axis=1).reshape(KV_HEADS, ATTN_GROUP * CHUNK, 1)

    hn = _prenorm(x, meta, wt["norm_pre_mix"])
    proj = _matmul(hn, wp_t, tb=True, name="in_proj", after=started)
    xc, xact = _ssm_conv_fwd(proj, wt["ssm_conv_w"], wt["ssm_conv_b"])
    y, hst = _ssd_fwd(xact, proj, dtb, alog, dskip_x)
    yn = _ssm_post(y, proj, wt["ssm_norm"])
    if late_weights is not None:
        wt = {**wt, **late_weights(yn)}
    y_ssm = _matmul(yn, wt["w_ssm_out"], name="ssm_out")
    bias = _swa_bias()
    attn = _swa_fwd(proj, sink_rows, bias)
    y_attn = _matmul(attn, wt["w_attn_out"], name="attn_out")
    mixed = _mix_fwd(proj, y_ssm, y_attn)
    mix = _matmul(mixed, wt["w_mix_out"], name="mix_out")
    h1, hn2 = _postmix(x, meta, mix, wt["norm_post_mix"], wt["norm_pre_ffn"])
    up = _matmul(hn2, wt["w_ffn_up_t"], tb=True, out_dtype=BF16, name="ffn_up")
    u, act = _ffn_act(up, wt["ffn_conv_w"], wt["ffn_conv_b"])
    f = _matmul(act, wt["w_ffn_down"], name="ffn_down")
    df, dy, g_norm_post_ffn, loss_row = _final(h1, f, target, wt["norm_post_ffn"])

    grads["norm_post_ffn"] = g_norm_post_ffn
    sent = emit("w_ffn_down", _matmul(act, df, ta=True, out_dtype=BF16, name="dw_ffn_down"))
    dact = _matmul(df, wt["w_ffn_down"], tb=True, out_dtype=BF16, name="d_act", after=sent)
    dup, grads["ffn_conv_w"], grads["ffn_conv_b"] = _ffn_act_bwd(u, up, dact, wt["ffn_conv_w"])
    sent = emit("w_ffn_up_t", _matmul(dup, hn2, ta=True, out_dtype=BF16, name="dw_ffn_up"))
    dhn2 = _matmul(dup, wt["w_ffn_up_t"], name="d_hn2", after=sent)
    dmix, dh, grads["norm_pre_ffn"], grads["norm_post_mix"] = _postmix_bwd(h1, dhn2, dy, mix, wt["norm_pre_ffn"], wt["norm_post_mix"])
    sent = emit("w_mix_out", _matmul(mixed, dmix, ta=True, out_dtype=BF16, name="dw_mix_out"))
    dmixed = _matmul(dmix, wt["w_mix_out"], tb=True, name="d_mixed", after=sent)
    dy_ssm, dy_attn, dproj = _mix_bwd(dmixed, proj, y_ssm, y_attn, lax.empty(proj.shape, BF16))
    sent = emit("w_ssm_out", _matmul(yn, dy_ssm, ta=True, out_dtype=BF16, name="dw_ssm_out"))
    dyn = _matmul(dy_ssm, wt["w_ssm_out"], tb=True, out_dtype=BF16, name="d_yn", after=sent)
    sent = emit("w_attn_out", _matmul(attn, dy_attn, ta=True, out_dtype=BF16, name="dw_attn_out"))
    dattn = _matmul(dy_attn, wt["w_attn_out"], tb=True, name="d_attn", after=sent)
    dy_ssd, dproj, grads["ssm_norm"] = _ssm_post_bwd(y, proj, dyn, wt["ssm_norm"], dproj)
    dxs, dbm, dcm, dproj, dalog, ddtb, dd_x = _ssd_bwd(xact, proj, dtb, alog, dskip_x, dy_ssd, hst, dproj)
    grads["ssm_a_log"] = dalog[:, :SSM_HEADS]
    grads["ssm_dt_bias"] = ddtb[:, :SSM_HEADS]
    grads["ssm_d_skip"] = dd_x.reshape(SSM_HEADS, HEAD_P).sum(axis=1).reshape(1, SSM_HEADS)
    dproj, grads["ssm_conv_w"], grads["ssm_conv_b"] = _ssm_conv_bwd(xc, proj, dxs, dbm, dcm, wt["ssm_conv_w"], dproj)
    dproj, dsink = _swa_bwd(proj, sink_rows, bias, attn, dattn, dproj)
    grads["attn_sinks"] = dsink[:, :ATTN_GROUP, 0].reshape(1, ATTN_HEADS)
    sent = emit("w_in_t", _unpack_w_in_t(_matmul(dproj, hn, ta=True, out_dtype=BF16, name="dw_in")))
    dhn = _matmul(dproj, wp_t, name="d_hn", after=sent)
    grad_x, grads["meta_tokens"], grads["norm_pre_mix"] = _prenorm_bwd(x, meta, dhn, dh, wt["norm_pre_mix"])
    return loss_row[0, 0], grad_x, grads


def _all_gather(shards):
    n = len(shards)

    def body(*refs):
        ins, outs = refs[:n], refs[n:2 * n]
        send_sems, recv_sems, local_sems = refs[2 * n:]
        x, y, c = lax.axis_index("x"), lax.axis_index("y"), lax.axis_index("c")
        me, sibling = (x, y, c), (x, y, 1 - c)
        x_nbr, y_nbr, diag = (1 - x, y), (x, 1 - y), (1 - x, 1 - y)
        relayed = (x ^ (1 - c), y ^ c)
        relay_to = (x ^ c, y ^ (1 - c))

        def slot(a, dev):
            return outs[a].at[4 * dev[0] + 2 * dev[1] + dev[2]]

        def copy(k, a, block, to, src=None):
            return pltpu.make_async_remote_copy(
                src_ref=slot(a, block) if src is None else src, dst_ref=slot(a, block),
                send_sem=send_sems.at[k, a], recv_sem=recv_sems.at[k, a],
                device_id=to, device_id_type=pl.DeviceIdType.MESH)

        mine = [pltpu.make_async_copy(ins[a], slot(a, me), local_sems.at[a]) for a in range(n)]
        for cp in mine:
            cp.start()
        first = [copy(0, a, me, sibling, src=ins[a]) for a in range(n)]
        first += [copy(1, a, me, (*x_nbr, c), src=ins[a]) for a in range(n)]
        first += [copy(2, a, me, (*y_nbr, c), src=ins[a]) for a in range(n)]
        for cp in first:
            cp.start()
        passed = []

        def pass_on(k, block, to):
            for a in range(n):
                cp = copy(k, a, block, to)
                cp.start()
                passed.append(cp)

        for j, chip in enumerate((x_nbr, y_nbr)):
            for a in range(n):
                copy(1 + j, a, (*chip, c), me).wait_recv()
            pass_on(4 + j, (*chip, c), sibling)
        pass_on(3, (*relayed, c), (*relay_to, c))
        for a in range(n):
            copy(3, a, (*diag, c), me).wait_recv()
        pass_on(6, (*diag, c), sibling)
        for a in range(n):
            copy(0, a, sibling, me).wait_recv()
        for j, chip in enumerate((x_nbr, y_nbr, diag)):
            for a in range(n):
                copy(4 + j, a, (*chip, 1 - c), me).wait_recv()
        for cp in first + passed:
            cp.wait_send()
        for cp in mine:
            cp.wait()

    hbm = pl.BlockSpec(memory_space=pl.ANY)
    return pl.pallas_call(
        body, in_specs=[hbm] * n, out_specs=[hbm] * n,
        out_shape=[_sds((N_DEV,) + s.shape, s.dtype) for s in shards],
        scratch_shapes=[pltpu.SemaphoreType.DMA((7, n)), pltpu.SemaphoreType.DMA((7, n)), pltpu.SemaphoreType.DMA((n,))],
        name="gather_weights")(*shards)


def _peer_table():
    x, y, c = lax.axis_index("x"), lax.axis_index("y"), lax.axis_index("c")
    peers = []
    for k in range(N_DEV - 1):
        bits = k + 1
        p = (x ^ ((bits >> 2) & 1), y ^ ((bits >> 1) & 1), c ^ (bits & 1))
        peers.append((k, p, 4 * p[0] + 2 * p[1] + p[2]))
    return 4 * x + 2 * y + c, peers


_HBM = pl.BlockSpec(memory_space=pltpu.HBM)
_SEM = pl.BlockSpec(memory_space=pltpu.SEMAPHORE)
_EFFECT = pltpu.SideEffectType.DATAFLOW_SIDE_EFFECTING


def _push_copy(src, land, send_sems, recv_sems, a, k, p, src_slot, dst_slot):
    sem = a * (N_DEV - 1) + k
    return pltpu.make_async_remote_copy(
        src_ref=src[a] if src_slot is None else src[a].at[src_slot], dst_ref=land[a].at[dst_slot],
        send_sem=send_sems.at[sem], recv_sem=recv_sems.at[sem], device_id=p, device_id_type=pl.DeviceIdType.MESH)


def _push_start(srcs, scatter, name):
    n = len(srcs)
    lands = [lax.empty(s.shape if scatter else (N_DEV,) + s.shape, s.dtype) for s in srcs]

    def body(*refs):
        src, land = refs[:n], refs[n:2 * n]
        send_sems, recv_sems, token = refs[2 * n], refs[2 * n + 1], refs[-1]
        my_id, peers = _peer_table()
        for a in range(n):
            for k, p, p_id in peers:
                _push_copy(src, land, send_sems, recv_sems, a, k, p, p_id if scatter else None, my_id).start()
        token[...] = jnp.zeros_like(token)

    sems = pltpu.SemaphoreType.DMA(((N_DEV - 1) * n,))
    res = pl.pallas_call(
        body, name=name,
        out_shape=(sems, sems, *[pltpu.HBM(a.shape, a.dtype) for a in srcs + lands], _sds((8, 128), F32)),
        in_specs=[_HBM] * (2 * n), out_specs=(_SEM, _SEM, *[_HBM] * (2 * n), pl.BlockSpec(memory_space=pltpu.VMEM)),
        input_output_aliases={i: 2 + i for i in range(2 * n)},
        compiler_params=pltpu.CompilerParams(has_side_effects=_EFFECT),
    )(*[pltpu.with_memory_space_constraint(a, pltpu.HBM) for a in srcs + lands])
    return dict(send=res[0], recv=res[1], src=list(res[2:2 + n]), land=list(res[2 + n:2 + 2 * n]), token=res[-1],
                scatter=scatter)


def _push_wait(handle, after, name):
    n = len(handle["src"])
    scatter = handle["scatter"]

    def body(*refs):
        src, land = refs[:n], refs[n:2 * n]
        send_sems, recv_sems = refs[2 * n], refs[2 * n + 1]
        _, peers = _peer_table()
        for a in range(n):
            for k, p, p_id in peers:
                cp = _push_copy(src, land, send_sems, recv_sems, a, k, p, p_id if scatter else None, p_id)
                cp.wait_send()
                cp.wait_recv()

    arrays = handle["src"] + handle["land"]
    res = pl.pallas_call(
        body, name=name, out_shape=tuple(pltpu.HBM(a.shape, a.dtype) for a in arrays),
        in_specs=[_HBM] * (2 * n) + [_SEM, _SEM, pl.BlockSpec(memory_space=pl.ANY)], out_specs=tuple([_HBM] * (2 * n)),
        input_output_aliases={i: i for i in range(2 * n)},
        compiler_params=pltpu.CompilerParams(has_side_effects=_EFFECT),
    )(*arrays, handle["send"], handle["recv"], after)
    return list(res[:n]), list(res[n:])


def _slot_sum(p_ref, own_ref):
    if own_ref is not None:
        my_id = 4 * lax.axis_index("x") + 2 * lax.axis_index("y") + lax.axis_index("c")
        mine = own_ref[...].astype(F32)
    g = None
    for s in range(p_ref.shape[0]):
        term = p_ref[s].astype(F32)
        if own_ref is not None:
            term = jnp.where(my_id == s, mine, term)
        g = term if g is None else g + term
    return g


def _to_bf16(arrays):
    n = len(arrays)

    def body(*refs):
        for i in range(n):
            refs[n + i][...] = refs[i][...].astype(BF16)

    return pl.pallas_call(body, out_shape=[_sds(a.shape, BF16) for a in arrays], name="weights_to_bf16",
                          compiler_params=pltpu.CompilerParams(vmem_limit_bytes=VMEM_LIMIT))(*arrays)


def _adamw(parts, own, w, m, v, name):
    unit_rows = w.ndim == 3
    rows, cols = w.shape[0], w.shape[-1]
    if rows % 16 == 0:
        tr, tc = _pick(rows, (256, 128, 176, 64, 32, 16)), cols
    else:
        tr, tc = rows, _pick(cols, (256, 128))

    def body(*refs):
        if own is None:
            p_ref, w_ref, m_ref, v_ref, g_ref, d_ref, nm_ref, nv_ref = refs
            own_ref = None
        else:
            p_ref, own_ref, w_ref, m_ref, v_ref, g_ref, d_ref, nm_ref, nv_ref = refs
        g = _slot_sum(p_ref, own_ref)
        if unit_rows:
            g = g.reshape(tr, 1, tc)
        m_new = ADAM_B1 * m_ref[...] + (1.0 - ADAM_B1) * g
        v_new = ADAM_B2 * v_ref[...] + (1.0 - ADAM_B2) * (g * g)
        m_hat = m_new / (1.0 - ADAM_B1 ** ADAM_STEP)
        v_hat = v_new / (1.0 - ADAM_B2 ** ADAM_STEP)
        g_ref[...] = g
        d_ref[...] = -ADAM_LR * (m_hat / (jnp.sqrt(v_hat) + ADAM_EPS) + ADAM_WD * w_ref[...])
        nm_ref[...] = m_new
        nv_ref[...] = v_new

    by_rows = tc == cols
    spec = pl.BlockSpec((tr, tc), (lambda i: (i, 0)) if by_rows else (lambda i: (0, i)))
    state_spec = spec if not unit_rows else pl.BlockSpec((tr, 1, tc), (lambda i: (i, 0, 0)) if by_rows else (lambda i: (0, 0, i)))
    parts_spec = pl.BlockSpec((parts.shape[0], tr, tc), (lambda i: (0, i, 0)) if by_rows else (lambda i: (0, 0, i)))
    operands = (parts, w, m, v) if own is None else (parts, own, w, m, v)
    return pl.pallas_call(
        body, grid=(rows // tr if by_rows else cols // tc,),
        in_specs=[parts_spec] + ([] if own is None else [spec]) + [state_spec] * 3,
        out_specs=[state_spec] * 4, out_shape=[_sds(w.shape, F32)] * 4,
        name=name, compiler_params=_params(1))(*operands)


SMALL_REPLICATED = (("norm_pre_mix", 1024), ("ssm_conv_b", 3072), ("ssm_dt_bias", 32), ("ssm_a_log", 32),
                    ("ssm_d_skip", 32), ("ssm_norm", 2048), ("attn_sinks", 16), ("norm_post_mix", 1024),
                    ("norm_pre_ffn", 1024), ("ffn_conv_b", 5632), ("norm_post_ffn", 1024))
SMALL_SHARDED = (("meta_tokens", (N_META, D_MODEL // N_DEV)), ("ssm_conv_w", (SSM_CONV, CONV_DIM // N_DEV)),
                 ("ffn_conv_w", (FFN_CONV, 2 * FFN_DIM // N_DEV)))
BIG = (("w_in", (D_MODEL, N_IN // N_DEV), 1), ("w_ssm_out", (D_INNER // N_DEV, D_MODEL), 0),
       ("w_attn_out", (D_MODEL // N_DEV, D_MODEL), 0), ("w_mix_out", (D_MODEL // N_DEV, D_MODEL), 0),
       ("w_ffn_up", (D_MODEL, 2 * FFN_DIM // N_DEV), 1), ("w_ffn_down", (FFN_DIM // N_DEV, D_MODEL), 0))


def _rows_of(size):
    return -(-size // 128)


def _as_rows(flat):
    size = flat.shape[-1]
    rows = _rows_of(size)
    flat = jnp.pad(flat, [(0, 0)] * (flat.ndim - 1) + [(0, rows * 128 - size)])
    return flat.reshape(flat.shape[:-1] + (rows, 128))


def _pack_small(rep, sharded):
    pieces = [_as_rows(rep[name].reshape(-1)) for name, _ in SMALL_REPLICATED]
    pieces += [_as_rows(sharded[name].reshape(-1)) for name, _ in SMALL_SHARDED]
    packed = jnp.concatenate(pieces, axis=0)
    return jnp.pad(packed, ((0, -packed.shape[0] % 8), (0, 0)))


def _unpack_small(packed):
    out, row = {}, 0
    for name, size in SMALL_REPLICATED:
        out[name] = packed[row:row + _rows_of(size)].reshape(-1)[:size].reshape(1, size)
        row += _rows_of(size)
    for name, (r, c) in SMALL_SHARDED:
        out[name] = packed[row:row + _rows_of(r * c)].reshape(-1)[:r * c].reshape(r, c)
        row += _rows_of(r * c)
    return out


def _shard_major(g, shape, axis):
    r, c = shape
    if axis == 0:
        return g.reshape(N_DEV, r, c)
    return g.reshape(r, N_DEV, c).transpose(1, 0, 2)


def kernel(x, meta_tokens, norm_pre_mix, w_in, ssm_conv_w, ssm_conv_b, ssm_dt_bias, ssm_a_log, ssm_d_skip, ssm_norm, w_ssm_out, attn_sinks, w_attn_out, w_mix_out, norm_post_mix, norm_pre_ffn, w_ffn_up, ffn_conv_w, ffn_conv_b, w_ffn_down, norm_post_ffn, loss_target, m_meta_tokens, m_norm_pre_mix, m_w_in, m_ssm_conv_w, m_ssm_conv_b, m_ssm_dt_bias, m_ssm_a_log, m_ssm_d_skip, m_ssm_norm, m_w_ssm_out, m_attn_sinks, m_w_attn_out, m_w_mix_out, m_norm_post_mix, m_norm_pre_ffn, m_w_ffn_up, m_ffn_conv_w, m_ffn_conv_b, m_w_ffn_down, m_norm_post_ffn, v_meta_tokens, v_norm_pre_mix, v_w_in, v_ssm_conv_w, v_ssm_conv_b, v_ssm_dt_bias, v_ssm_a_log, v_ssm_d_skip, v_ssm_norm, v_w_ssm_out, v_attn_sinks, v_w_attn_out, v_w_mix_out, v_norm_post_mix, v_norm_pre_ffn, v_w_ffn_up, v_ffn_conv_w, v_ffn_conv_b, v_w_ffn_down, v_norm_post_ffn):
    names = ("meta_tokens", "norm_pre_mix", "w_in", "ssm_conv_w", "ssm_conv_b", "ssm_dt_bias", "ssm_a_log", "ssm_d_skip",
             "ssm_norm", "w_ssm_out", "attn_sinks", "w_attn_out", "w_mix_out", "norm_post_mix", "norm_pre_ffn", "w_ffn_up",
             "ffn_conv_w", "ffn_conv_b", "w_ffn_down", "norm_post_ffn")
    w_loc = dict(zip(names, (meta_tokens, norm_pre_mix, w_in, ssm_conv_w, ssm_conv_b, ssm_dt_bias, ssm_a_log, ssm_d_skip,
                             ssm_norm, w_ssm_out, attn_sinks, w_attn_out, w_mix_out, norm_post_mix, norm_pre_ffn, w_ffn_up,
                             ffn_conv_w, ffn_conv_b, w_ffn_down, norm_post_ffn)))
    m_loc = dict(zip(names, (m_meta_tokens, m_norm_pre_mix, m_w_in, m_ssm_conv_w, m_ssm_conv_b, m_ssm_dt_bias, m_ssm_a_log,
                             m_ssm_d_skip, m_ssm_norm, m_w_ssm_out, m_attn_sinks, m_w_attn_out, m_w_mix_out, m_norm_post_mix,
                             m_norm_pre_ffn, m_w_ffn_up, m_ffn_conv_w, m_ffn_conv_b, m_w_ffn_down, m_norm_post_ffn)))
    v_loc = dict(zip(names, (v_meta_tokens, v_norm_pre_mix, v_w_in, v_ssm_conv_w, v_ssm_conv_b, v_ssm_dt_bias, v_ssm_a_log,
                             v_ssm_d_skip, v_ssm_norm, v_w_ssm_out, v_attn_sinks, v_w_attn_out, v_w_mix_out, v_norm_post_mix,
                             v_norm_pre_ffn, v_w_ffn_up, v_ffn_conv_w, v_ffn_conv_b, v_w_ffn_down, v_norm_post_ffn)))

    def local2d(d, name):
        a = d[name]
        return a if name == "meta_tokens" else a.reshape(a.shape[1:])

    def turned2d(d, name):
        a = jnp.swapaxes(d[name], 1, 2)
        return a.reshape(a.shape[1:])

    my_id = 4 * lax.axis_index("x") + 2 * lax.axis_index("y") + lax.axis_index("c")
    big = {name: (shape, axis) for name, shape, axis in BIG}

    def whole(name, g):
        return g.reshape(N_DEV * g.shape[1], g.shape[2])

    def key(name):
        return name + "_t" if big[name][1] == 1 else name

    by_rows = [name for name, _, axis in BIG if axis == 0]
    send_bf16 = dict(zip(by_rows, _to_bf16([local2d(w_loc, name) for name in by_rows])))
    for name, _, axis in BIG:
        if axis == 1:
            send_bf16[name] = turned2d(w_loc, name).astype(BF16)
    small_shard_pack = jnp.concatenate([_as_rows(local2d(w_loc, name).reshape(-1)) for name, _ in SMALL_SHARDED], axis=0)
    small_shard_pack = jnp.pad(small_shard_pack, ((0, -small_shard_pack.shape[0] % 8), (0, 0)))
    first = _all_gather([send_bf16["w_in"], small_shard_pack])
    rest_names = [name for name, _, _ in BIG if name != "w_in"]
    rest = [send_bf16[name] for name in rest_names]
    rest, first = lax.optimization_barrier((rest, first))
    rest_handle = _push_start(rest, False, "gather_rest_start")
    wt = {"w_in_t": whole("w_in", first[0])}
    row = 0
    for name, (r, c) in SMALL_SHARDED:
        blocks = first[1][:, row:row + _rows_of(r * c)].reshape(N_DEV, -1)[:, :r * c].reshape(N_DEV, r, c)
        wt[name] = blocks.transpose(1, 0, 2).reshape(r, N_DEV * c)
        row += _rows_of(r * c)
    for name, size in SMALL_REPLICATED:
        wt[name] = w_loc[name].reshape(1, size)

    def late_weights(after):
        own, landed = _push_wait(rest_handle, after, "gather_rest_wait")
        out = {}
        for name, mine, land in zip(rest_names, own, landed):
            out[key(name)] = whole(name, lax.dynamic_update_index_in_dim(land, mine, my_id, 0))
        return out

    sent = {}

    def on_grad(known_as, g):
        name = known_as.removesuffix("_t")
        by_owner = g.reshape(N_DEV, g.shape[0] // N_DEV, g.shape[1])
        sent[name] = _push_start([by_owner], True, "send_" + name)
        return sent[name]["token"]

    loss_part, grad_x, grads = _local_step(x[0], loss_target[0], wt, late_weights, on_grad, rest_handle["token"])

    small_parts = []
    for name, (r, c) in SMALL_SHARDED:
        small_parts.append(_as_rows(_shard_major(grads[name], (r, c), 1).reshape(N_DEV, r * c)))
    rep_rows = jnp.concatenate([_as_rows(grads[name].reshape(-1)) for name, _ in SMALL_REPLICATED], axis=0)
    small_send = jnp.concatenate([jnp.broadcast_to(rep_rows[None], (N_DEV,) + rep_rows.shape)] + small_parts, axis=1)
    small_send = jnp.pad(small_send, ((0, 0), (0, -small_send.shape[1] % 8), (0, 0)))
    loss_tile = jnp.broadcast_to(jnp.pad(loss_part.reshape(1, 1, 1), ((0, 0), (0, 7), (0, 127))), (N_DEV, 8, 128))
    small_send = jnp.concatenate([small_send, loss_tile], axis=1)
    small_handle = _push_start([small_send], True, "send_small")

    def small_pack(d):
        pack = _pack_small({name: d[name] for name, _ in SMALL_REPLICATED}, {name: local2d(d, name) for name, _ in SMALL_SHARDED})
        return jnp.pad(pack, ((0, 8), (0, 0)))

    def arrived(handle, after, name):
        src, landed = _push_wait(handle, after, "arrived_" + name)
        return landed[0], lax.dynamic_index_in_dim(src[0], my_id, 0, keepdims=False)

    grad_w, delta_w, new_m, new_v = {}, {}, {}, {}
    outs = None
    after = small_handle["token"]
    for name, handle in sent.items():
        if name == "w_in":
            parts, own = arrived(small_handle, after, "small")
            outs = _adamw(parts, own, small_pack(w_loc), small_pack(m_loc), small_pack(v_loc), "adamw_small")
            after = outs[0]
        parts, own = arrived(handle, after, name)
        turned = big[name][1] == 1
        unit_rows = turned and big[name][0][1] % 8 != 0
        if unit_rows:
            state = [jnp.transpose(d[name], (2, 0, 1)) for d in (w_loc, m_loc, v_loc)]
        else:
            state = [turned2d(d, name) if turned else local2d(d, name) for d in (w_loc, m_loc, v_loc)]
        results = _adamw(parts, own, *state, "adamw_" + name)
        after = results[0]
        full = (1,) + big[name][0]
        for dst, a in zip((grad_w, delta_w, new_m, new_v), results):
            if unit_rows:
                dst[name] = jnp.transpose(a, (1, 2, 0))
            else:
                dst[name] = jnp.swapaxes(a[None], 1, 2) if turned else a.reshape(full)
    for dst, packed in zip((grad_w, delta_w, new_m, new_v), outs):
        for name, a in _unpack_small(packed).items():
            dst[name] = a.reshape(w_loc[name].shape)
    loss = outs[0][-8, 0]

    return (loss, grad_x[None], *[grad_w[n] for n in names], *[delta_w[n] for n in names],
            *[new_m[n] for n in names], *[new_v[n] for n in names])
```
